```python
import math
import jax, jax.numpy as jnp
from jax import lax
import numpy as np

D_MODEL = 2048
BATCH = 4
SEQ = 4096
DEPTH = 1

NORM_EPS = 1e-6
SSD_D_INNER = D_MODEL
SSD_HEAD_DIM = 64
SSD_N_HEADS = SSD_D_INNER // SSD_HEAD_DIM
SSD_N_GROUPS = 8
SSD_D_STATE = 128
SSD_CONV_WIDTH = 4
SSD_CHUNK = 128
SSD_CONV_DIM = SSD_D_INNER + 2 * SSD_N_GROUPS * SSD_D_STATE
ATTN_HEAD_DIM = 128
DILATION_PATTERNS = ((128, 1), (512, 4), (2048, 16))
ATTN_HEADS_PER_GROUP = 4
ATTN_N_HEADS = ATTN_HEADS_PER_GROUP * len(DILATION_PATTERNS)
ATTN_WIDTH = ATTN_N_HEADS * ATTN_HEAD_DIM
ATTN_OUT_WIDTH = ATTN_HEADS_PER_GROUP * ATTN_HEAD_DIM
ATTN_BLOCK = 128
N_BRANCHES = 2
IN_COLS = SSD_D_INNER + SSD_CONV_DIM + SSD_N_HEADS + 3 * ATTN_WIDTH + N_BRANCHES * D_MODEL
N_EXPERT_GROUPS = 4
EXPERTS_PER_GROUP = 8
N_EXPERTS = N_EXPERT_GROUPS * EXPERTS_PER_GROUP
EXPERT_TOP_K = 2
EXPERT_D_FF = 1024

kernel_name = "hybrid_ssd_dilated_attn_hmoe_block"


def rms_norm(x, w):
    xf = x.astype(jnp.float32)
    y = xf * lax.rsqrt(jnp.mean(xf * xf, axis=-1, keepdims=True) + NORM_EPS)
    return (y * w.astype(jnp.float32)).astype(x.dtype)


def alibi_slopes(n):
    return jnp.asarray(2.0 ** (-8.0 * np.arange(1, n + 1) / n), dtype=jnp.float32)


def causal_depthwise_conv(u, w, bias):
    width, ch = w.shape
    out = lax.conv_general_dilated(
        u, w[:, None, :].astype(u.dtype), window_strides=(1,), padding=[(width - 1, 0)],
        dimension_numbers=("NWC", "WIO", "NWC"), feature_group_count=ch)
    return out + bias.astype(u.dtype)


def ssd_chunked_scan(xs, dt, a, bm, cm):
    b, s, h, p = xs.shape
    g, n = bm.shape[2], bm.shape[3]
    r = h // g
    l = SSD_CHUNK
    c = s // l
    x_dt = (xs * dt[..., None]).reshape(b, c, l, g, r, p)
    a_cum = jnp.cumsum((dt * a[None, None, :]).reshape(b, c, l, g, r), axis=2)
    bm = bm.reshape(b, c, l, g, n)
    cm = cm.reshape(b, c, l, g, n)
    pos = jnp.arange(l)
    causal = (pos[:, None] >= pos[None, :])[None, None, :, :, None, None]
    seg = a_cum[:, :, :, None] - a_cum[:, :, None, :]
    decay = jnp.exp(jnp.where(causal, seg, -jnp.inf))
    cb = jnp.einsum("bclgn,bcsgn->bclsg", cm, bm)
    y_diag = jnp.einsum("bclsgr,bcsgrp->bclgrp", cb[..., None] * decay, x_dt)
    decay_to_end = jnp.exp(a_cum[:, :, -1:] - a_cum)
    states = jnp.einsum("bclgn,bclgr,bclgrp->bcgrpn", bm, decay_to_end, x_dt)
    chunk_decay = jnp.exp(a_cum[:, :, -1])

    def step(carry, inp):
        st, dec = inp
        return carry * dec[..., None, None] + st, carry

    init = jnp.zeros((b, g, r, p, n), jnp.float32)
    _, states_in = lax.scan(step, init, (jnp.moveaxis(states, 1, 0), jnp.moveaxis(chunk_decay, 1, 0)))
    states_in = jnp.moveaxis(states_in, 0, 1)
    y_off = jnp.einsum("bclgn,bcgrpn,bclgr->bclgrp", cm, states_in, jnp.exp(a_cum))
    return (y_diag + y_off).reshape(b, s, h, p)


def dilated_window_attention(q, k, v, slopes, window, dilation):
    b, s, h, e = q.shape
    hops = window // dilation
    span = dilation * ATTN_BLOCK
    s_pad = -(-s // span) * span
    nb = s_pad // span
    pad = ((0, 0), (0, s_pad - s), (0, 0), (0, 0))

    def to_blocks(t):
        return jnp.pad(t, pad).reshape(b, nb, ATTN_BLOCK, dilation, h, e)

    qb, kb, vb = to_blocks(q), to_blocks(k), to_blocks(v)
    shift = ((0, 0), (1, 0), (0, 0), (0, 0), (0, 0), (0, 0))
    kk = jnp.concatenate([jnp.pad(kb, shift)[:, :-1], kb], axis=2)
    vv = jnp.concatenate([jnp.pad(vb, shift)[:, :-1], vb], axis=2)
    scores = jnp.einsum("bnqrhe,bnkrhe->brhnqk", qb, kk, preferred_element_type=jnp.float32)
    rel = (jnp.arange(ATTN_BLOCK)[:, None] + ATTN_BLOCK) - jnp.arange(2 * ATTN_BLOCK)[None, :]
    key_ok = (jnp.arange(nb)[:, None] > 0) | (jnp.arange(2 * ATTN_BLOCK)[None, :] >= ATTN_BLOCK)
    valid = ((rel >= 0) & (rel <= hops))[None] & key_ok[:, None, :]
    bias = -slopes.astype(jnp.float32)[:, None, None] * (rel * dilation).astype(jnp.float32)[None]
    logits = scores * (ATTN_HEAD_DIM ** -0.5) + bias[:, None]
    logits = jnp.where(valid, logits, -jnp.inf)
    lse = jax.nn.logsumexp(logits, axis=-1)
    probs = jnp.exp(logits - lse[..., None])
    out = jnp.einsum("brhnqk,bnkrhe->bnqrhe", probs, vv.astype(jnp.float32))
    out = out.reshape(b, s_pad, h, e)[:, :s]
    lse = jnp.transpose(lse, (0, 3, 4, 1, 2)).reshape(b, s_pad, h)[:, :s]
    return out, lse


def hierarchical_moe(hn, w_gr, b_gr, w_er, b_er, w_g, w_u, w_d):
    b, s, d = hn.shape
    t = hn.reshape(b * s, d)
    n_tok = t.shape[0]
    group_logits = jnp.dot(t, w_gr, preferred_element_type=jnp.float32) + b_gr.astype(jnp.float32)
    group_prob = jax.nn.softmax(group_logits, axis=-1)
    group_idx = jnp.argmax(group_logits, axis=-1)
    group_gate = jnp.take_along_axis(group_prob, group_idx[:, None], axis=-1)
    expert_logits = (jnp.dot(t, w_er, preferred_element_type=jnp.float32)
                     + b_er.astype(jnp.float32)).reshape(n_tok, N_EXPERT_GROUPS, EXPERTS_PER_GROUP)
    in_group = jnp.take_along_axis(expert_logits, group_idx[:, None, None], axis=1)[:, 0]
    top_vals, top_idx = lax.top_k(in_group, EXPERT_TOP_K)
    gate = jax.nn.softmax(top_vals, axis=-1) * group_gate
    expert_id = (group_idx[:, None] * EXPERTS_PER_GROUP + top_idx).reshape(-1)
    order = jnp.argsort(expert_id)
    token_of = order // EXPERT_TOP_K
    xs = t[token_of]
    sizes = jnp.bincount(expert_id, length=N_EXPERTS).astype(jnp.int32)
    hmid = jax.nn.silu(lax.ragged_dot(xs, w_g, sizes)) * lax.ragged_dot(xs, w_u, sizes)
    ys = lax.ragged_dot(hmid, w_d, sizes) * gate.reshape(-1)[order][:, None].astype(t.dtype)
    out = jnp.zeros_like(t).at[token_of].add(ys)
    return out.reshape(b, s, d)


def setup_inputs(seed: int = 0) -> dict:
    key = jax.random.key(seed)
    ks = jax.random.split(key, 24)
    f32 = jnp.float32
    L = DEPTH

    def nrm(k, shape, scale):
        return jax.random.normal(k, shape, f32) * scale

    dt0 = jnp.exp(jax.random.uniform(ks[6], (L, SSD_N_HEADS), f32, math.log(1e-3), math.log(1e-1)))
    return {
        "x": nrm(ks[0], (BATCH, SEQ, D_MODEL), 1.0),
        "attn_norm_w": 1.0 + nrm(ks[1], (L, D_MODEL), 0.02),
        "w_in": nrm(ks[2], (L, D_MODEL, IN_COLS), D_MODEL ** -0.5),
        "b_gate": nrm(ks[3], (L, N_BRANCHES * D_MODEL), 0.02),
        "conv_w": nrm(ks[4], (L, SSD_CONV_WIDTH, SSD_CONV_DIM), SSD_CONV_WIDTH ** -0.5),
        "conv_b": nrm(ks[5], (L, SSD_CONV_DIM), 0.02),
        "dt_bias": dt0 + jnp.log(-jnp.expm1(-dt0)),
        "a_log": jnp.log(jax.random.uniform(ks[7], (L, SSD_N_HEADS), f32, 1.0, 16.0)),
        "d_skip": 1.0 + nrm(ks[8], (L, SSD_N_HEADS), 0.1),
        "ssd_norm_w": 1.0 + nrm(ks[9], (L, SSD_D_INNER), 0.02),
        "w_ssd_out": nrm(ks[10], (L, SSD_D_INNER, D_MODEL), SSD_D_INNER ** -0.5),
        "w_attn_out": nrm(ks[11], (L, ATTN_OUT_WIDTH, D_MODEL), ATTN_OUT_WIDTH ** -0.5),
        "w_out": nrm(ks[12], (L, D_MODEL, D_MODEL), D_MODEL ** -0.5),
        "ffn_norm_w": 1.0 + nrm(ks[13], (L, D_MODEL), 0.02),
        "w_group_router": nrm(ks[14], (L, D_MODEL, N_EXPERT_GROUPS), D_MODEL ** -0.5),
        "b_group_router": nrm(ks[15], (L, N_EXPERT_GROUPS), 0.01),
        "w_expert_router": nrm(ks[16], (L, D_MODEL, N_EXPERTS), D_MODEL ** -0.5),
        "b_expert_router": nrm(ks[17], (L, N_EXPERTS), 0.01),
        "w_exp_gate": nrm(ks[18], (L, N_EXPERTS, D_MODEL, EXPERT_D_FF), D_MODEL ** -0.5),
        "w_exp_up": nrm(ks[19], (L, N_EXPERTS, D_MODEL, EXPERT_D_FF), D_MODEL ** -0.5),
        "w_exp_down": nrm(ks[20], (L, N_EXPERTS, EXPERT_D_FF, D_MODEL), EXPERT_D_FF ** -0.5),
        "final_norm_w": 1.0 + nrm(ks[21], (D_MODEL,), 0.02),
    }


def reference(x, attn_norm_w, w_in, b_gate, conv_w, conv_b, dt_bias, a_log, d_skip, ssd_norm_w,
              w_ssd_out, w_attn_out, w_out, ffn_norm_w, w_group_router, b_group_router,
              w_expert_router, b_expert_router, w_exp_gate, w_exp_up, w_exp_down, final_norm_w):
    b, s, _ = x.shape
    f32 = jnp.float32
    slopes = alibi_slopes(ATTN_N_HEADS)
    cuts = [int(c) for c in np.cumsum([SSD_D_INNER, SSD_CONV_DIM, SSD_N_HEADS,
                                       ATTN_WIDTH, ATTN_WIDTH, ATTN_WIDTH])]
    bc_cuts = [SSD_D_INNER, SSD_D_INNER + SSD_N_GROUPS * SSD_D_STATE]
    for layer in range(DEPTH):
        h = rms_norm(x, attn_norm_w[layer])
        proj = jnp.einsum("bsd,dc->bsc", h, w_in[layer])
        z, xbc, dt_raw, q, k, v, gate_raw = jnp.split(proj, cuts, axis=-1)

        xbc = jax.nn.silu(causal_depthwise_conv(xbc, conv_w[layer], conv_b[layer]))
        xs, bm, cm = jnp.split(xbc, bc_cuts, axis=-1)
        xs = xs.astype(f32).reshape(b, s, SSD_N_HEADS, SSD_HEAD_DIM)
        bm = bm.astype(f32).reshape(b, s, SSD_N_GROUPS, SSD_D_STATE)
        cm = cm.astype(f32).reshape(b, s, SSD_N_GROUPS, SSD_D_STATE)
        dt = jax.nn.softplus(dt_raw.astype(f32) + dt_bias[layer].astype(f32))
        a = -jnp.exp(a_log[layer].astype(f32))
        y = ssd_chunked_scan(xs, dt, a, bm, cm)
        y = y + d_skip[layer].astype(f32)[:, None] * xs
        y = y.reshape(b, s, SSD_D_INNER) * jax.nn.silu(z.astype(f32))
        y = rms_norm(y, ssd_norm_w[layer]).astype(x.dtype)
        y_ssd = y @ w_ssd_out[layer]

        q = q.reshape(b, s, ATTN_N_HEADS, ATTN_HEAD_DIM)
        k = k.reshape(b, s, ATTN_N_HEADS, ATTN_HEAD_DIM)
        v = v.reshape(b, s, ATTN_N_HEADS, ATTN_HEAD_DIM)
        outs, lses = [], []
        for gi, (window, dilation) in enumerate(DILATION_PATTERNS):
            hs = slice(gi * ATTN_HEADS_PER_GROUP, (gi + 1) * ATTN_HEADS_PER_GROUP)
            o_g, lse_g = dilated_window_attention(q[:, :, hs], k[:, :, hs], v[:, :, hs],
                                                  slopes[hs], window, dilation)
            outs.append(o_g)
            lses.append(lse_g)
        weights = jax.nn.softmax(jnp.stack(lses), axis=0)
        o = jnp.sum(weights[..., None] * jnp.stack(outs), axis=0)
        o = o.reshape(b, s, ATTN_OUT_WIDTH).astype(x.dtype)
        y_attn = o @ w_attn_out[layer]

        gates = jax.nn.sigmoid(gate_raw.astype(f32) + b_gate[layer].astype(f32))
        gates = gates.reshape(b, s, N_BRANCHES, D_MODEL)
        merged = (gates[:, :, 0] * y_ssd.astype(f32) + gates[:, :, 1] * y_attn.astype(f32)).astype(x.dtype)
        x = x + merged @ w_out[layer]

        x = x + hierarchical_moe(rms_norm(x, ffn_norm_w[layer]), w_group_router[layer],
                                 b_group_router[layer], w_expert_router[layer], b_expert_router[layer],
                                 w_exp_gate[layer], w_exp_up[layer], w_exp_down[layer])
    return rms_norm(x, final_norm_w)
```

```python
import functools
import math

import jax
import jax.numpy as jnp
import numpy as np
from jax import lax
from jax.experimental import pallas as pl
from jax.experimental.pallas import tpu as pltpu

F32 = jnp.float32
BF16 = jnp.bfloat16

NORM_EPS = 1e-6
SSD_HEAD_DIM = 64
SSD_N_GROUPS = 8
SSD_D_STATE = 128
SSD_CONV_WIDTH = 4
SSD_CHUNK = 128
ATTN_HEAD_DIM = 128
DILATION_PATTERNS = ((128, 1), (512, 4), (2048, 16))
ATTN_HEADS_PER_GROUP = 4
ATTN_BLOCK = 128
N_EXPERT_GROUPS = 4
EXPERTS_PER_GROUP = 8
N_EXPERTS = N_EXPERT_GROUPS * EXPERTS_PER_GROUP
EXPERT_TOP_K = 2

LANES = 128
SUBLANES = 8
VMEM_LIMIT_BYTES = 56 * 1024 * 1024

ROUTER_EXPERT_ROW0 = SUBLANES
ROUTER_ROWS = ROUTER_EXPERT_ROW0 + N_EXPERTS


def _params(*semantics):
    return pltpu.CompilerParams(dimension_semantics=semantics, vmem_limit_bytes=VMEM_LIMIT_BYTES)


def _split_bf16(v):
    hi = v.astype(BF16)
    lo = (v - hi.astype(F32)).astype(BF16)
    return hi, lo


def _silu(v):
    return v * (1.0 / (1.0 + jnp.exp(-v)))


def _in_proj_kernel(x_ref, nw_ref, w_ref, wdt_ref, wdtt_ref, proj_ref, dt_ref, dtt_ref, h_ref):
    j = pl.program_id(1)

    @pl.when(j == 0)
    def _():
        xf = x_ref[...]
        ms = jnp.mean(xf * xf, axis=-1, keepdims=True)
        h = (xf * lax.rsqrt(ms + NORM_EPS) * nw_ref[...]).astype(BF16)
        h_ref[...] = h
        dt_ref[...] = jnp.dot(h, wdt_ref[...], preferred_element_type=F32)
        dtt_ref[...] = lax.dot_general(wdtt_ref[...], h, (((1,), (1,)), ((), ())),
                                       preferred_element_type=F32)

    proj_ref[...] = jnp.dot(h_ref[...], w_ref[...], preferred_element_type=F32).astype(BF16)


def _in_proj(x2, norm_w, w_main, w_dt, *, tm, tn):
    n, d = x2.shape
    cols = w_main.shape[1]
    nh = w_dt.shape[1]
    return pl.pallas_call(
        _in_proj_kernel,
        grid=(n // tm, cols // tn),
        in_specs=[
            pl.BlockSpec((tm, d), lambda i, j: (i, 0)),
            pl.BlockSpec((1, d), lambda i, j: (0, 0)),
            pl.BlockSpec((d, tn), lambda i, j: (0, j)),
            pl.BlockSpec((d, nh), lambda i, j: (0, 0)),
            pl.BlockSpec((nh, d), lambda i, j: (0, 0)),
        ],
        out_specs=[
            pl.BlockSpec((tm, tn), lambda i, j: (i, j)),
            pl.BlockSpec((tm, nh), lambda i, j: (i, 0)),
            pl.BlockSpec((nh, tm), lambda i, j: (0, i)),
        ],
        out_shape=[
            jax.ShapeDtypeStruct((n, cols), BF16),
            jax.ShapeDtypeStruct((n, nh), F32),
            jax.ShapeDtypeStruct((nh, n), F32),
        ],
        scratch_shapes=[pltpu.VMEM((tm, d), BF16)],
        compiler_params=_params("arbitrary", "arbitrary"),
        name="in_proj",
    )(x2, norm_w.reshape(1, d), w_main, w_dt.astype(BF16), w_dt.T.astype(BF16))


def _ssd_kernel(xbc_ref, z_ref, dt_ref, dtt_ref, cw_ref, cb_ref, dtb_ref, dtbt_ref, alog_ref,
                alogt_ref, dskip_ref, nw_ref, expand_ref, out_ref,
                xbuf_ref, state_ref, y_ref, *, n_heads, d_inner):
    L = SSD_CHUNK
    P = SSD_HEAD_DIM
    NS = SSD_D_STATE
    G = SSD_N_GROUPS
    R = n_heads // G
    GW = R * P
    W = SSD_CONV_WIDTH
    c = pl.program_id(1)

    @pl.when(c == 0)
    def _():
        xbuf_ref[0:SUBLANES, :] = jnp.zeros((SUBLANES, xbuf_ref.shape[1]), F32)
        state_ref[...] = jnp.zeros(state_ref.shape, F32)

    @pl.when(c > 0)
    def _():
        xbuf_ref[0:SUBLANES, :] = xbuf_ref[L:L + SUBLANES, :]

    xbuf_ref[SUBLANES:SUBLANES + L, :] = xbc_ref[...].astype(F32)

    def conv(col0, width):
        acc = cb_ref[:, col0:col0 + width]
        for w in range(W):
            r0 = SUBLANES - (W - 1) + w
            acc = acc + cw_ref[w:w + 1, col0:col0 + width] * xbuf_ref[r0:r0 + L, col0:col0 + width]
        return _silu(acc)

    def softplus(v):
        return jnp.maximum(v, 0.0) + jnp.log1p(jnp.exp(-jnp.abs(v)))

    dt = softplus(dt_ref[...] + dtb_ref[...])
    dtt = softplus(dtt_ref[...] + dtbt_ref[...])
    da = dt * (-jnp.exp(alog_ref[...]))
    dat = dtt * (-jnp.exp(alogt_ref[...]))
    row = lax.broadcasted_iota(jnp.int32, (L, L), 0)
    col = lax.broadcasted_iota(jnp.int32, (L, L), 1)
    causal = row >= col
    tri = jnp.where(causal, 1.0, 0.0).astype(BF16)
    trit = jnp.where(row <= col, 1.0, 0.0).astype(BF16)

    def split3(v):
        a = v.astype(BF16)
        r1 = v - a.astype(F32)
        b = r1.astype(BF16)
        cc = (r1 - b.astype(F32)).astype(BF16)
        return a, b, cc

    a_cum = sum(jnp.dot(tri, p, preferred_element_type=F32) for p in split3(da))
    a_cumt = sum(jnp.dot(p, trit, preferred_element_type=F32) for p in split3(dat))
    a_last = a_cum[L - 1:L, :]

    expand = expand_ref[...]

    def expand_heads(v):
        hi, lo = _split_bf16(v)
        return (jnp.dot(hi, expand, preferred_element_type=F32)
                + jnp.dot(lo, expand, preferred_element_type=F32))

    dt_e = expand_heads(dt)
    in_scale_e = expand_heads(dt * jnp.exp(a_last - a_cum))
    out_scale_e = expand_heads(jnp.exp(a_cum))
    tail8 = jnp.concatenate([jnp.exp(a_last), dskip_ref[...],
                             jnp.zeros((SUBLANES - 2, n_heads), F32)], axis=0)
    tail8_e = expand_heads(tail8)
    chunk_decay_e = tail8_e[0:1, :]
    dskip_e = tail8_e[1:2, :]

    lane_head = lax.broadcasted_iota(jnp.int32, (L, GW), 1) // P

    for g in range(G):
        x0 = g * GW
        xs = conv(x0, GW)
        bm = conv(d_inner + g * NS, NS).astype(BF16)
        cm = conv(d_inner + G * NS + g * NS, NS).astype(BF16)
        cb = lax.dot_general(cm, bm, (((1,), (1,)), ((), ())), preferred_element_type=F32)
        xdt = (xs * dt_e[:, x0:x0 + GW]).astype(BF16)
        y = xs * dskip_e[:, x0:x0 + GW]
        for r in range(R):
            h = g * R + r
            seg = a_cum[:, h:h + 1] - a_cumt[h:h + 1, :]
            decay = jnp.exp(jnp.where(causal, seg, -jnp.inf))
            wgt = (cb * decay).astype(BF16)
            xh = jnp.where(lane_head == r, xdt, jnp.zeros_like(xdt))
            y = y + jnp.dot(wgt, xh, preferred_element_type=F32)
        st = state_ref[:, x0:x0 + GW]
        y = y + (jnp.dot(cm, st.astype(BF16), preferred_element_type=F32)
                 * out_scale_e[:, x0:x0 + GW])
        y_ref[:, x0:x0 + GW] = y
        xin = (xs * in_scale_e[:, x0:x0 + GW]).astype(BF16)
        st_new = lax.dot_general(bm, xin, (((0,), (0,)), ((), ())), preferred_element_type=F32)
        state_ref[:, x0:x0 + GW] = st * chunk_decay_e[:, x0:x0 + GW] + st_new

    yz = y_ref[...] * _silu(z_ref[...].astype(F32))
    ms = jnp.mean(yz * yz, axis=-1, keepdims=True)
    out_ref[...] = (yz * lax.rsqrt(ms + NORM_EPS) * nw_ref[...]).astype(out_ref.dtype)


def _ssd(proj, dt, dtt, conv_w, conv_b, dt_bias, a_log, d_skip, norm_w, *, batch, seq, d_inner,
         xbc_block, z_block):
    n = batch * seq
    n_heads = dt.shape[1]
    L = SSD_CHUNK
    nc = seq // L
    conv_dim = conv_w.shape[1]
    expand = (np.arange(d_inner)[None, :] // SSD_HEAD_DIM == np.arange(n_heads)[:, None])
    expand = jnp.asarray(expand, BF16)
    kern = functools.partial(_ssd_kernel, n_heads=n_heads, d_inner=d_inner)
    small = lambda shape: pl.BlockSpec(shape, lambda b, c: (0, 0))
    return pl.pallas_call(
        kern,
        grid=(batch, nc),
        in_specs=[
            pl.BlockSpec((L, conv_dim), lambda b, c: (b * nc + c, xbc_block)),
            pl.BlockSpec((L, d_inner), lambda b, c: (b * nc + c, z_block)),
            pl.BlockSpec((L, n_heads), lambda b, c: (b * nc + c, 0)),
            pl.BlockSpec((n_heads, L), lambda b, c: (0, b * nc + c)),
            small((SSD_CONV_WIDTH, conv_dim)),
            small((1, conv_dim)),
            small((1, n_heads)),
            small((n_heads, 1)),
            small((1, n_heads)),
            small((n_heads, 1)),
            small((1, n_heads)),
            small((1, d_inner)),
            small((n_heads, d_inner)),
        ],
        out_specs=pl.BlockSpec((L, d_inner), lambda b, c: (b * nc + c, 0)),
        out_shape=jax.ShapeDtypeStruct((n, d_inner), BF16),
        scratch_shapes=[
            pltpu.VMEM((L + SUBLANES, conv_dim), F32),
            pltpu.VMEM((SSD_D_STATE, d_inner), F32),
            pltpu.VMEM((L, d_inner), F32),
        ],
        compiler_params=_params("arbitrary", "arbitrary"),
        name="ssd",
    )(proj, proj, dt, dtt, conv_w, conv_b.reshape(1, -1), dt_bias.reshape(1, -1),
      dt_bias.reshape(-1, 1), a_log.reshape(1, -1), a_log.reshape(-1, 1), d_skip.reshape(1, -1),
      norm_w.reshape(1, -1), expand)


def _attn_kernel(q_ref, k_ref, v_ref, o_ref, lse_ref, kp_ref, vp_ref, *, slopes, dilation, hops):
    nb = pl.program_id(2)
    BLK = ATTN_BLOCK
    E = ATTN_HEAD_DIM

    @pl.when(nb == 0)
    def _():
        kp_ref[...] = jnp.zeros(kp_ref.shape, kp_ref.dtype)
        vp_ref[...] = jnp.zeros(vp_ref.shape, vp_ref.dtype)

    qi = lax.broadcasted_iota(jnp.int32, (BLK, BLK), 0)
    ki = lax.broadcasted_iota(jnp.int32, (BLK, BLK), 1)
    rel_cur = qi - ki
    rel_prev = rel_cur + BLK
    ok_cur = rel_cur >= 0
    ok_prev = rel_prev <= jnp.where(nb > 0, hops, -1)
    dist_cur = (rel_cur * dilation).astype(F32)
    dist_prev = (rel_prev * dilation).astype(F32)
    lane = lax.broadcasted_iota(jnp.int32, (BLK, LANES), 1)
    scale = E ** -0.5
    nt = (((1,), (1,)), ((), ()))

    q = q_ref[0]
    k = k_ref[0]
    v = v_ref[0]
    kp = kp_ref[...]
    vp = vp_ref[...]
    lse_tile = jnp.zeros((BLK, LANES), F32)
    for h, slope in enumerate(slopes):
        sl = slice(h * E, (h + 1) * E)
        qh = q[:, sl]
        s_cur = lax.dot_general(qh, k[:, sl], nt, preferred_element_type=F32)
        s_prev = lax.dot_general(qh, kp[:, sl], nt, preferred_element_type=F32)
        l_cur = jnp.where(ok_cur, s_cur * scale - slope * dist_cur, -jnp.inf)
        l_prev = jnp.where(ok_prev, s_prev * scale - slope * dist_prev, -jnp.inf)
        m = jnp.max(jnp.maximum(l_cur, l_prev), axis=-1, keepdims=True)
        p_cur = jnp.exp(l_cur - m)
        p_prev = jnp.exp(l_prev - m)
        den = jnp.sum(p_cur + p_prev, axis=-1, keepdims=True)
        acc = (jnp.dot(p_cur.astype(BF16), v[:, sl], preferred_element_type=F32)
               + jnp.dot(p_prev.astype(BF16), vp[:, sl], preferred_element_type=F32))
        o_ref[0, :, sl] = (acc / den).astype(o_ref.dtype)
        lse_tile = jnp.where(lane == h, m + jnp.log(den), lse_tile)
    lse_ref[0] = lse_tile
    kp_ref[...] = k
    vp_ref[...] = v


def _attn_group(proj, gi, *, batch, seq, cols, q_block0, k_block0, v_block0):
    window, dilation = DILATION_PATTERNS[gi]
    hops = window // dilation
    n_heads_total = ATTN_HEADS_PER_GROUP * len(DILATION_PATTERNS)
    slopes = tuple(float(2.0 ** (-8.0 * (gi * ATTN_HEADS_PER_GROUP + h + 1) / n_heads_total))
                   for h in range(ATTN_HEADS_PER_GROUP))
    gw = ATTN_HEADS_PER_GROUP * ATTN_HEAD_DIM
    assert seq % (dilation * ATTN_BLOCK) == 0 and cols % gw == 0
    sub = seq // dilation
    nb = sub // ATTN_BLOCK
    cb = cols // gw
    pv = proj.reshape(batch, sub, dilation * cols)
    kern = functools.partial(_attn_kernel, slopes=slopes, dilation=dilation, hops=hops)
    o, lse = pl.pallas_call(
        kern,
        grid=(batch, dilation, nb),
        in_specs=[
            pl.BlockSpec((1, ATTN_BLOCK, gw), lambda b, r, n: (b, n, r * cb + q_block0 + gi)),
            pl.BlockSpec((1, ATTN_BLOCK, gw), lambda b, r, n: (b, n, r * cb + k_block0 + gi)),
            pl.BlockSpec((1, ATTN_BLOCK, gw), lambda b, r, n: (b, n, r * cb + v_block0 + gi)),
        ],
        out_specs=[
            pl.BlockSpec((1, ATTN_BLOCK, gw), lambda b, r, n: (b, n, r)),
            pl.BlockSpec((1, ATTN_BLOCK, LANES), lambda b, r, n: (b, n, r)),
        ],
        out_shape=[
            jax.ShapeDtypeStruct((batch, sub, dilation * gw), BF16),
            jax.ShapeDtypeStruct((batch, sub, dilation * LANES), F32),
        ],
        scratch_shapes=[pltpu.VMEM((ATTN_BLOCK, gw), BF16), pltpu.VMEM((ATTN_BLOCK, gw), BF16)],
        compiler_params=_params("arbitrary", "arbitrary", "arbitrary"),
        name=f"attn_g{gi}",
    )(pv, pv, pv)
    return o.reshape(batch * seq, gw), lse.reshape(batch * seq, LANES)


def _mix_kernel(yn_ref, o0_ref, o1_ref, o2_ref, l0_ref, l1_ref, l2_ref, gate_ref, x_ref,
                wssd_ref, wattn_ref, wout_ref, bg_ref, fnw_ref, wrh_ref, wrl_ref, rb_ref,
                x1_ref, hn_ref, eid_ref, gcol_ref, rank_ref, cnt_ref, carry_ref, *, d_model):
    i = pl.program_id(0)
    tm = x_ref.shape[0]
    E = ATTN_HEAD_DIM

    @pl.when(i == 0)
    def _():
        carry_ref[...] = jnp.zeros(carry_ref.shape, F32)

    y_ssd = jnp.dot(yn_ref[...], wssd_ref[...], preferred_element_type=F32)

    l0, l1, l2 = l0_ref[...], l1_ref[...], l2_ref[...]
    lm = jnp.maximum(jnp.maximum(l0, l1), l2)
    e0, e1, e2 = jnp.exp(l0 - lm), jnp.exp(l1 - lm), jnp.exp(l2 - lm)
    inv = 1.0 / (e0 + e1 + e2)
    parts = []
    for h in range(ATTN_HEADS_PER_GROUP):
        sl = slice(h * E, (h + 1) * E)
        parts.append((e0[:, h:h + 1] * inv[:, h:h + 1]) * o0_ref[:, sl].astype(F32)
                     + (e1[:, h:h + 1] * inv[:, h:h + 1]) * o1_ref[:, sl].astype(F32)
                     + (e2[:, h:h + 1] * inv[:, h:h + 1]) * o2_ref[:, sl].astype(F32))
    o = jnp.concatenate(parts, axis=-1).astype(BF16)
    y_attn = jnp.dot(o, wattn_ref[...], preferred_element_type=F32)

    graw = gate_ref[...].astype(F32) + bg_ref[...]
    gates = 1.0 / (1.0 + jnp.exp(-graw))
    merged = (gates[:, :d_model] * y_ssd + gates[:, d_model:] * y_attn).astype(BF16)
    x1 = x_ref[...] + jnp.dot(merged, wout_ref[...], preferred_element_type=F32)
    x1_ref[...] = x1

    ms = jnp.mean(x1 * x1, axis=-1, keepdims=True)
    hn = x1 * lax.rsqrt(ms + NORM_EPS) * fnw_ref[...]
    hn_ref[...] = hn

    hn_hi, hn_lo = _split_bf16(hn)
    nt = (((1,), (1,)), ((), ()))
    logits = (lax.dot_general(wrh_ref[...], hn_hi, nt, preferred_element_type=F32)
              + lax.dot_general(wrh_ref[...], hn_lo, nt, preferred_element_type=F32)
              + lax.dot_general(wrl_ref[...], hn_hi, nt, preferred_element_type=F32)
              + rb_ref[...])

    grow = lax.broadcasted_iota(jnp.int32, (SUBLANES, tm), 0)
    gl = jnp.where(grow < N_EXPERT_GROUPS, logits[0:SUBLANES, :], -jnp.inf)
    gmax = jnp.max(gl, axis=0, keepdims=True)
    gidx = jnp.min(jnp.where(gl == gmax, grow, N_EXPERT_GROUPS), axis=0, keepdims=True)
    group_gate = 1.0 / jnp.sum(jnp.exp(gl - gmax), axis=0, keepdims=True)

    in_group = jnp.zeros((EXPERTS_PER_GROUP, tm), F32)
    for g in range(N_EXPERT_GROUPS):
        r0 = ROUTER_EXPERT_ROW0 + g * EXPERTS_PER_GROUP
        in_group = jnp.where(gidx == g, logits[r0:r0 + EXPERTS_PER_GROUP, :], in_group)
    erow = lax.broadcasted_iota(jnp.int32, in_group.shape, 0)
    v1 = jnp.max(in_group, axis=0, keepdims=True)
    i1 = jnp.min(jnp.where(in_group == v1, erow, EXPERTS_PER_GROUP), axis=0, keepdims=True)
    rest = jnp.where(erow == i1, -jnp.inf, in_group)
    v2 = jnp.max(rest, axis=0, keepdims=True)
    i2 = jnp.min(jnp.where(rest == v2, erow, EXPERTS_PER_GROUP), axis=0, keepdims=True)
    t = jnp.exp(v2 - v1)
    g1 = group_gate / (1.0 + t)
    g2 = group_gate * t / (1.0 + t)
    eid1 = gidx * EXPERTS_PER_GROUP + i1
    eid2 = gidx * EXPERTS_PER_GROUP + i2
    slot = lax.broadcasted_iota(jnp.int32, (SUBLANES, tm), 0)
    eid_ref[...] = jnp.where(slot == 0, eid1, jnp.where(slot == 1, eid2, 0))

    grow8 = lax.broadcasted_iota(jnp.int32, (LANES, tm), 0)
    gt = jnp.where(grow8 == 0, g1, jnp.where(grow8 == 1, g2, 0.0))
    gcol_ref[...] = gt.T

    xrow = lax.broadcasted_iota(jnp.int32, (N_EXPERTS, tm), 0)
    oh1 = xrow == eid1
    oh2 = xrow == eid2
    oh = jnp.where(oh1 | oh2, 1.0, 0.0)
    ti = lax.broadcasted_iota(jnp.int32, (tm, tm), 0)
    tj = lax.broadcasted_iota(jnp.int32, (tm, tm), 1)
    before = jnp.where(ti < tj, 1.0, 0.0).astype(BF16)
    prior = jnp.dot(oh.astype(BF16), before, preferred_element_type=F32) + carry_ref[:, 0:1]
    r1 = jnp.sum(jnp.where(oh1, prior, 0.0), axis=0, keepdims=True)
    r2 = jnp.sum(jnp.where(oh2, prior, 0.0), axis=0, keepdims=True)
    rank_ref[...] = jnp.where(slot == 0, r1, jnp.where(slot == 1, r2, 0.0)).astype(jnp.int32)
    carry_ref[...] = carry_ref[...] + jnp.sum(oh, axis=1, keepdims=True)
    cnt_ref[...] = carry_ref[...].astype(jnp.int32)


def _mix(yn, outs, lses, proj, x2, w_ssd_out, w_attn_out, w_out, b_gate, ffn_norm_w, wr, rb, *,
         tm, gate_block):
    n, d = x2.shape
    aw = outs[0].shape[1]
    wrt = jnp.zeros((ROUTER_ROWS, d), F32)
    wrt = wrt.at[0:N_EXPERT_GROUPS].set(wr[0].T).at[ROUTER_EXPERT_ROW0:].set(wr[1].T)
    rbc = jnp.zeros((ROUTER_ROWS, 1), F32)
    rbc = rbc.at[0:N_EXPERT_GROUPS, 0].set(rb[0]).at[ROUTER_EXPERT_ROW0:, 0].set(rb[1])
    wrh, wrl = _split_bf16(wrt)
    row = lambda w: pl.BlockSpec((tm, w), lambda i: (i, 0))
    const = lambda shape: pl.BlockSpec(shape, lambda i: (0, 0), pipeline_mode=pl.Buffered(1))
    kern = functools.partial(_mix_kernel, d_model=d)
    return pl.pallas_call(
        kern,
        grid=(n // tm,),
        in_specs=[
            row(d), row(aw), row(aw), row(aw), row(LANES), row(LANES), row(LANES),
            pl.BlockSpec((tm, 2 * d), lambda i: (i, gate_block)),
            row(d),
            const((d, d)), const((aw, d)), const((d, d)), const((1, 2 * d)), const((1, d)),
            const((ROUTER_ROWS, d)), const((ROUTER_ROWS, d)), const((ROUTER_ROWS, 1)),
        ],
        out_specs=[
            row(d), row(d),
            pl.BlockSpec((SUBLANES, tm), lambda i: (0, i)),
            row(LANES),
            pl.BlockSpec((SUBLANES, tm), lambda i: (0, i)),
            pl.BlockSpec((N_EXPERTS, LANES), lambda i: (0, 0)),
        ],
        out_shape=[
            jax.ShapeDtypeStruct((n, d), F32),
            jax.ShapeDtypeStruct((n, d), F32),
            jax.ShapeDtypeStruct((SUBLANES, n), jnp.int32),
            jax.ShapeDtypeStruct((n, LANES), F32),
            jax.ShapeDtypeStruct((SUBLANES, n), jnp.int32),
            jax.ShapeDtypeStruct((N_EXPERTS, LANES), jnp.int32),
        ],
        scratch_shapes=[pltpu.VMEM((N_EXPERTS, LANES), F32)],
        compiler_params=_params("arbitrary"),
        name="mix",
    )(yn, outs[0], outs[1], outs[2], lses[0], lses[1], lses[2], proj, x2,
      w_ssd_out.astype(BF16), w_attn_out.astype(BF16), w_out.astype(BF16),
      b_gate.reshape(1, -1), ffn_norm_w.reshape(1, -1), wrh, wrl, rbc)


def _scatter_kernel(pos_ref, hn_ref, init_ref, xs_ref, sem):
    del init_ref
    tm = hn_ref.shape[0]

    def copy(t, k):
        dst = pos_ref[0, 0, k * tm + t]
        return pltpu.make_async_copy(hn_ref.at[pl.ds(t, 1), :], xs_ref.at[pl.ds(dst, 1), :], sem)

    def start(t, carry):
        for k in range(EXPERT_TOP_K):
            copy(t, k).start()
        return carry

    def wait(t, carry):
        for k in range(EXPERT_TOP_K):
            copy(t, k).wait()
        return carry

    lax.fori_loop(0, tm, start, 0)
    lax.fori_loop(0, tm, wait, 0)


def _scatter(hn, pos_tiles, xs_init, *, tm):
    n, d = hn.shape
    return pl.pallas_call(
        _scatter_kernel,
        grid=(n // tm,),
        in_specs=[
            pl.BlockSpec((1, 1, EXPERT_TOP_K * tm), lambda i: (i, 0, 0), memory_space=pltpu.SMEM),
            pl.BlockSpec((tm, d), lambda i: (i, 0)),
            pl.BlockSpec(memory_space=pl.ANY),
        ],
        out_specs=pl.BlockSpec(memory_space=pl.ANY),
        out_shape=jax.ShapeDtypeStruct(xs_init.shape, xs_init.dtype),
        scratch_shapes=[pltpu.SemaphoreType.DMA(())],
        input_output_aliases={2: 0},
        compiler_params=_params("arbitrary"),
        name="scatter",
    )(pos_tiles, hn, xs_init)


def _experts_kernel(tile_expert_ref, n_active_ref, xs_ref, wg_ref, wu_ref, wd_ref, ys_ref):
    i = pl.program_id(0)

    @pl.when(i < n_active_ref[0])
    def _():
        xb = xs_ref[...].astype(BF16)
        hg = jnp.dot(xb, wg_ref[0], preferred_element_type=F32)
        hu = jnp.dot(xb, wu_ref[0], preferred_element_type=F32)
        hmid = (_silu(hg) * hu).astype(BF16)
        ys_ref[...] = jnp.dot(hmid, wd_ref[0], preferred_element_type=F32)

    @pl.when(i >= n_active_ref[0])
    def _():
        ys_ref[...] = jnp.zeros(ys_ref.shape, F32)


def _experts(xs, tile_expert, n_active, w_g, w_u, w_d, *, tm):
    p, d = xs.shape
    ne, _, f = w_g.shape
    n_tiles = p // tm

    def rows(i, te, na):
        return (jnp.minimum(i, na[0] - 1), 0)

    def wsel(i, te, na):
        return (te[jnp.minimum(i, na[0] - 1)], 0, 0)

    grid_spec = pltpu.PrefetchScalarGridSpec(
        num_scalar_prefetch=2,
        grid=(n_tiles,),
        in_specs=[
            pl.BlockSpec((tm, d), rows),
            pl.BlockSpec((1, d, f), wsel),
            pl.BlockSpec((1, d, f), wsel),
            pl.BlockSpec((1, f, d), wsel),
        ],
        out_specs=pl.BlockSpec((tm, d), lambda i, te, na: (i, 0)),
    )
    return pl.pallas_call(
        _experts_kernel,
        grid_spec=grid_spec,
        out_shape=jax.ShapeDtypeStruct((p, d), F32),
        compiler_params=_params("arbitrary"),
        name="experts",
    )(tile_expert, n_active, xs, w_g, w_u, w_d)


def _combine_kernel(pos_ref, ys_ref, x1_ref, gcol_ref, nw_ref, out_ref, buf_ref, sem, *, final):
    tm = x1_ref.shape[0]

    def copy(t, k):
        src = pos_ref[0, 0, k * tm + t]
        return pltpu.make_async_copy(ys_ref.at[pl.ds(src, 1), :], buf_ref.at[k, pl.ds(t, 1), :], sem)

    def start(t, carry):
        for k in range(EXPERT_TOP_K):
            copy(t, k).start()
        return carry

    def wait(t, carry):
        for k in range(EXPERT_TOP_K):
            copy(t, k).wait()
        return carry

    lax.fori_loop(0, tm, start, 0)
    lax.fori_loop(0, tm, wait, 0)
    g = gcol_ref[...]
    xo = x1_ref[...] + g[:, 0:1] * buf_ref[0] + g[:, 1:2] * buf_ref[1]
    if final:
        ms = jnp.mean(xo * xo, axis=-1, keepdims=True)
        xo = xo * lax.rsqrt(ms + NORM_EPS) * nw_ref[...]
    out_ref[...] = xo


def _combine(ys, pos_tiles, x1, gcol, norm_w, *, tm, final):
    n, d = x1.shape
    return pl.pallas_call(
        functools.partial(_combine_kernel, final=final),
        grid=(n // tm,),
        in_specs=[
            pl.BlockSpec((1, 1, EXPERT_TOP_K * tm), lambda i: (i, 0, 0), memory_space=pltpu.SMEM),
            pl.BlockSpec(memory_space=pl.ANY),
            pl.BlockSpec((tm, d), lambda i: (i, 0)),
            pl.BlockSpec((tm, LANES), lambda i: (i, 0)),
            pl.BlockSpec((1, d), lambda i: (0, 0)),
        ],
        out_specs=pl.BlockSpec((tm, d), lambda i: (i, 0)),
        out_shape=jax.ShapeDtypeStruct((n, d), F32),
        scratch_shapes=[pltpu.VMEM((EXPERT_TOP_K, tm, d), F32), pltpu.SemaphoreType.DMA(())],
        compiler_params=_params("arbitrary"),
        name="combine",
    )(pos_tiles, ys, x1, gcol, norm_w.reshape(1, d))


def _tiles(n):
    return dict(proj_tm=math.gcd(n, 1024), mix_tm=math.gcd(n, 256), moe_tm=256,
                route_tm=math.gcd(n, 256))


def _layer(x2, batch, seq, attn_norm_w, w_in, b_gate, conv_w, conv_b, dt_bias, a_log, d_skip,
           ssd_norm_w, w_ssd_out, w_attn_out, w_out, ffn_norm_w, w_gr, b_gr, w_er, b_er,
           w_g, w_u, w_d):
    n, d = x2.shape
    n_heads = dt_bias.shape[0]
    d_inner = ssd_norm_w.shape[0]
    conv_dim = conv_w.shape[1]
    aw_total = ATTN_HEADS_PER_GROUP * len(DILATION_PATTERNS) * ATTN_HEAD_DIM
    gw = ATTN_HEADS_PER_GROUP * ATTN_HEAD_DIM
    tiles = _tiles(n)

    c_z, c_xbc, c_dt = d_inner, d_inner + conv_dim, d_inner + conv_dim + n_heads
    c_q, c_k, c_v = c_dt + aw_total, c_dt + 2 * aw_total, c_dt + 3 * aw_total
    w_main = jnp.concatenate([w_in[:, c_v:], w_in[:, c_z:c_xbc], w_in[:, :c_z],
                              w_in[:, c_dt:c_v]], axis=1).astype(BF16)
    w_dt = w_in[:, c_xbc:c_dt]
    cols = w_main.shape[1]
    off_gate, off_xbc, off_z = 0, 2 * d, 2 * d + conv_dim
    off_q = off_z + d_inner
    assert off_xbc % conv_dim == 0 and off_z % d_inner == 0 and off_q % gw == 0 and cols % gw == 0

    proj, dt, dtt = _in_proj(x2, attn_norm_w, w_main, w_dt, tm=tiles["proj_tm"], tn=gw)

    yn = _ssd(proj, dt, dtt, conv_w, conv_b, dt_bias, a_log, d_skip, ssd_norm_w, batch=batch,
              seq=seq, d_inner=d_inner, xbc_block=off_xbc // conv_dim, z_block=off_z // d_inner)

    outs, lses = [], []
    for gi in range(len(DILATION_PATTERNS)):
        o_g, lse_g = _attn_group(proj, gi, batch=batch, seq=seq, cols=cols,
                                 q_block0=off_q // gw, k_block0=(off_q + aw_total) // gw,
                                 v_block0=(off_q + 2 * aw_total) // gw)
        outs.append(o_g)
        lses.append(lse_g)

    x1, hn, eid, gcol, rank, counts = _mix(
        yn, outs, lses, proj, x2, w_ssd_out, w_attn_out, w_out, b_gate, ffn_norm_w,
        (w_gr, w_er), (b_gr, b_er), tm=tiles["mix_tm"], gate_block=off_gate // (2 * d))

    tme = tiles["moe_tm"]
    cnt = counts[:, 0]
    padded = ((cnt + tme - 1) // tme) * tme
    ends = jnp.cumsum(padded)
    starts = ends - padded
    pos = starts[eid[:EXPERT_TOP_K]] + rank[:EXPERT_TOP_K]
    p_max = (EXPERT_TOP_K * n // tme + N_EXPERTS) * tme
    n_tiles = p_max // tme
    tile_expert = jnp.minimum(
        jnp.searchsorted(ends, jnp.arange(n_tiles, dtype=jnp.int32) * tme, side="right"),
        N_EXPERTS - 1).astype(jnp.int32)
    n_active = (ends[-1:] // tme).astype(jnp.int32)

    rtm = tiles["route_tm"]
    pos_tiles = pos.reshape(EXPERT_TOP_K, n // rtm, rtm).transpose(1, 0, 2).reshape(
        n // rtm, 1, EXPERT_TOP_K * rtm)
    xs = _scatter(hn, pos_tiles, jnp.zeros((p_max, d), F32), tm=rtm)
    ys = _experts(xs, tile_expert, n_active, w_g.astype(BF16), w_u.astype(BF16),
                  w_d.astype(BF16), tm=tme)
    return ys, pos_tiles, x1, gcol, rtm


def kernel(x, attn_norm_w, w_in, b_gate, conv_w, conv_b, dt_bias, a_log, d_skip, ssd_norm_w,
           w_ssd_out, w_attn_out, w_out, ffn_norm_w, w_group_router, b_group_router,
           w_expert_router, b_expert_router, w_exp_gate, w_exp_up, w_exp_down, final_norm_w):
    batch, seq, d = x.shape
    depth = w_in.shape[0]
    x2 = x.reshape(batch * seq, d)
    for layer in range(depth):
        ys, pos_tiles, x1, gcol, rtm = _layer(
            x2, batch, seq, attn_norm_w[layer], w_in[layer], b_gate[layer], conv_w[layer],
            conv_b[layer], dt_bias[layer], a_log[layer], d_skip[layer], ssd_norm_w[layer],
            w_ssd_out[layer], w_attn_out[layer], w_out[layer], ffn_norm_w[layer],
            w_group_router[layer], b_group_router[layer], w_expert_router[layer],
            b_expert_router[layer], w_exp_gate[layer], w_exp_up[layer], w_exp_down[layer])
        x2 = _combine(ys, pos_tiles, x1, gcol, final_norm_w, tm=rtm, final=layer == depth - 1)
    return x2.reshape(batch, seq, d)
```

```python
import functools
import math

import jax
import jax.numpy as jnp
import numpy as np
from jax import lax
from jax.experimental import pallas as pl
from jax.experimental.pallas import tpu as pltpu

F32 = jnp.float32
BF16 = jnp.bfloat16

NORM_EPS = 1e-6
SSD_HEAD_DIM = 64
SSD_N_GROUPS = 8
SSD_D_STATE = 128
SSD_CONV_WIDTH = 4
SSD_CHUNK = 128
ATTN_HEAD_DIM = 128
DILATION_PATTERNS = ((128, 1), (512, 4), (2048, 16))
ATTN_HEADS_PER_GROUP = 4
ATTN_BLOCK = 128
N_EXPERT_GROUPS = 4
EXPERTS_PER_GROUP = 8
N_EXPERTS = N_EXPERT_GROUPS * EXPERTS_PER_GROUP
EXPERT_TOP_K = 2

LANES = 128
SUBLANES = 8
VMEM_LIMIT_BYTES = 56 * 1024 * 1024

ROUTER_EXPERT_ROW0 = SUBLANES
ROUTER_ROWS = ROUTER_EXPERT_ROW0 + N_EXPERTS


def _params(*semantics):
    return pltpu.CompilerParams(dimension_semantics=semantics, vmem_limit_bytes=VMEM_LIMIT_BYTES)


def _split_bf16(v):
    hi = v.astype(BF16)
    lo = (v - hi.astype(F32)).astype(BF16)
    return hi, lo


def _silu(v):
    return v * (1.0 / (1.0 + jnp.exp(-v)))


QKV_PARTS = 3


def _in_proj_kernel(x_ref, nw_ref, w_ref, wdt_ref, wdtt_ref, proj_ref, a0_ref, a1_ref, a2_ref,
                    dt_ref, dtt_ref, h_ref, stage_ref, *, n_plain):
    j = pl.program_id(1)
    tm, tn = proj_ref.shape

    @pl.when(j == 0)
    def _():
        xf = x_ref[...]
        ms = jnp.mean(xf * xf, axis=-1, keepdims=True)
        h = (xf * lax.rsqrt(ms + NORM_EPS) * nw_ref[...]).astype(BF16)
        h_ref[...] = h
        dt_ref[...] = jnp.dot(h, wdt_ref[...], preferred_element_type=F32)
        dtt_ref[...] = lax.dot_general(wdtt_ref[...], h, (((1,), (1,)), ((), ())),
                                       preferred_element_type=F32)

    @pl.when(j < n_plain)
    def _():
        proj_ref[...] = jnp.dot(h_ref[...], w_ref[...], preferred_element_type=F32).astype(BF16)

    for gi, a_ref in enumerate((a0_ref, a1_ref, a2_ref)):
        dil = DILATION_PATTERNS[gi][1]
        j0 = n_plain + QKV_PARTS * gi

        @pl.when((j >= j0) & (j < j0 + QKV_PARTS))
        def _(a_ref=a_ref, dil=dil):
            res = jnp.dot(h_ref[...], w_ref[...], preferred_element_type=F32)
            if dil == 1:
                a_ref[...] = res.astype(BF16)
            else:
                for s in range(tn // LANES):
                    stage_ref[s] = res[:, s * LANES:(s + 1) * LANES]
                for r in range(dil):
                    for s in range(tn // LANES):
                        c0 = r * tn + s * LANES
                        a_ref[:, c0:c0 + LANES] = stage_ref[
                            s, pl.ds(r, tm // dil, stride=dil), :].astype(BF16)


def _in_proj(x2, norm_w, w_main, w_dt, *, tm, tn, n_plain):
    n, d = x2.shape
    nh = w_dt.shape[1]
    n_blocks = w_main.shape[1] // tn
    assert n_blocks == n_plain + QKV_PARTS * len(DILATION_PATTERNS)

    def a_spec(gi):
        dil = DILATION_PATTERNS[gi][1]
        j0 = n_plain + QKV_PARTS * gi
        return pl.BlockSpec((tm // dil, dil * tn),
                            lambda i, j: (i, jnp.clip(j - j0, 0, QKV_PARTS - 1)))

    def a_shape(gi):
        dil = DILATION_PATTERNS[gi][1]
        return jax.ShapeDtypeStruct((n // dil, dil * QKV_PARTS * tn), BF16)

    return pl.pallas_call(
        functools.partial(_in_proj_kernel, n_plain=n_plain),
        grid=(n // tm, n_blocks),
        in_specs=[
            pl.BlockSpec((tm, d), lambda i, j: (i, 0)),
            pl.BlockSpec((1, d), lambda i, j: (0, 0)),
            pl.BlockSpec((d, tn), lambda i, j: (0, j)),
            pl.BlockSpec((d, nh), lambda i, j: (0, 0)),
            pl.BlockSpec((nh, d), lambda i, j: (0, 0)),
        ],
        out_specs=[
            pl.BlockSpec((tm, tn), lambda i, j: (i, jnp.minimum(j, n_plain - 1))),
            a_spec(0), a_spec(1), a_spec(2),
            pl.BlockSpec((tm, nh), lambda i, j: (i, 0)),
            pl.BlockSpec((nh, tm), lambda i, j: (0, i)),
        ],
        out_shape=[
            jax.ShapeDtypeStruct((n, n_plain * tn), BF16),
            a_shape(0), a_shape(1), a_shape(2),
            jax.ShapeDtypeStruct((n, nh), F32),
            jax.ShapeDtypeStruct((nh, n), F32),
        ],
        scratch_shapes=[pltpu.VMEM((tm, d), BF16), pltpu.VMEM((tn // LANES, tm, LANES), F32)],
        compiler_params=_params("arbitrary", "arbitrary"),
        name="in_proj",
    )(x2, norm_w.reshape(1, d), w_main, w_dt.astype(BF16), w_dt.T.astype(BF16))


def _ssd_kernel(xbc_ref, z_ref, dt_ref, dtt_ref, cw_ref, cb_ref, dtb_ref, dtbt_ref, alog_ref,
                alogt_ref, dskip_ref, nw_ref, expand_ref, out_ref,
                xbuf_ref, state_ref, y_ref, *, n_heads, d_inner):
    L = SSD_CHUNK
    P = SSD_HEAD_DIM
    NS = SSD_D_STATE
    G = SSD_N_GROUPS
    R = n_heads // G
    GW = R * P
    W = SSD_CONV_WIDTH
    c = pl.program_id(1)

    @pl.when(c == 0)
    def _():
        xbuf_ref[0:SUBLANES, :] = jnp.zeros((SUBLANES, xbuf_ref.shape[1]), F32)
        state_ref[...] = jnp.zeros(state_ref.shape, F32)

    @pl.when(c > 0)
    def _():
        xbuf_ref[0:SUBLANES, :] = xbuf_ref[L:L + SUBLANES, :]

    xbuf_ref[SUBLANES:SUBLANES + L, :] = xbc_ref[...].astype(F32)

    def conv(col0, width):
        acc = cb_ref[:, col0:col0 + width]
        for w in range(W):
            r0 = SUBLANES - (W - 1) + w
            acc = acc + cw_ref[w:w + 1, col0:col0 + width] * xbuf_ref[r0:r0 + L, col0:col0 + width]
        return _silu(acc)

    def softplus(v):
        return jnp.maximum(v, 0.0) + jnp.log1p(jnp.exp(-jnp.abs(v)))

    dt = softplus(dt_ref[...] + dtb_ref[...])
    dtt = softplus(dtt_ref[...] + dtbt_ref[...])
    da = dt * (-jnp.exp(alog_ref[...]))
    dat = dtt * (-jnp.exp(alogt_ref[...]))
    row = lax.broadcasted_iota(jnp.int32, (L, L), 0)
    col = lax.broadcasted_iota(jnp.int32, (L, L), 1)
    causal = row >= col
    tri = jnp.where(causal, 1.0, 0.0).astype(BF16)
    trit = jnp.where(row <= col, 1.0, 0.0).astype(BF16)

    def split3(v):
        a = v.astype(BF16)
        r1 = v - a.astype(F32)
        b = r1.astype(BF16)
        cc = (r1 - b.astype(F32)).astype(BF16)
        return a, b, cc

    a_cum = sum(jnp.dot(tri, p, preferred_element_type=F32) for p in split3(da))
    a_cumt = sum(jnp.dot(p, trit, preferred_element_type=F32) for p in split3(dat))
    a_last = a_cum[L - 1:L, :]

    expand = expand_ref[...]

    def expand_heads(v):
        hi, lo = _split_bf16(v)
        return (jnp.dot(hi, expand, preferred_element_type=F32)
                + jnp.dot(lo, expand, preferred_element_type=F32))

    dt_e = expand_heads(dt)
    in_scale_e = expand_heads(dt * jnp.exp(a_last - a_cum))
    out_scale_e = expand_heads(jnp.exp(a_cum))
    tail8 = jnp.concatenate([jnp.exp(a_last), dskip_ref[...],
                             jnp.zeros((SUBLANES - 2, n_heads), F32)], axis=0)
    tail8_e = expand_heads(tail8)
    chunk_decay_e = tail8_e[0:1, :]
    dskip_e = tail8_e[1:2, :]

    lane_head = lax.broadcasted_iota(jnp.int32, (L, GW), 1) // P

    for g in range(G):
        x0 = g * GW
        xs = conv(x0, GW)
        bm = conv(d_inner + g * NS, NS).astype(BF16)
        cm = conv(d_inner + G * NS + g * NS, NS).astype(BF16)
        cb = lax.dot_general(cm, bm, (((1,), (1,)), ((), ())), preferred_element_type=F32)
        xdt = (xs * dt_e[:, x0:x0 + GW]).astype(BF16)
        y = xs * dskip_e[:, x0:x0 + GW]
        for r in range(R):
            h = g * R + r
            seg = a_cum[:, h:h + 1] - a_cumt[h:h + 1, :]
            decay = jnp.exp(jnp.where(causal, seg, -jnp.inf))
            wgt = (cb * decay).astype(BF16)
            xh = jnp.where(lane_head == r, xdt, jnp.zeros_like(xdt))
            y = y + jnp.dot(wgt, xh, preferred_element_type=F32)
        st = state_ref[:, x0:x0 + GW]
        y = y + (jnp.dot(cm, st.astype(BF16), preferred_element_type=F32)
                 * out_scale_e[:, x0:x0 + GW])
        y_ref[:, x0:x0 + GW] = y
        xin = (xs * in_scale_e[:, x0:x0 + GW]).astype(BF16)
        st_new = lax.dot_general(bm, xin, (((0,), (0,)), ((), ())), preferred_element_type=F32)
        state_ref[:, x0:x0 + GW] = st * chunk_decay_e[:, x0:x0 + GW] + st_new

    yz = y_ref[...] * _silu(z_ref[...].astype(F32))
    ms = jnp.mean(yz * yz, axis=-1, keepdims=True)
    out_ref[...] = (yz * lax.rsqrt(ms + NORM_EPS) * nw_ref[...]).astype(out_ref.dtype)


def _ssd(proj, dt, dtt, conv_w, conv_b, dt_bias, a_log, d_skip, norm_w, *, batch, seq, d_inner,
         xbc_block, z_block):
    n = batch * seq
    n_heads = dt.shape[1]
    L = SSD_CHUNK
    nc = seq // L
    conv_dim = conv_w.shape[1]
    expand = (np.arange(d_inner)[None, :] // SSD_HEAD_DIM == np.arange(n_heads)[:, None])
    expand = jnp.asarray(expand, BF16)
    kern = functools.partial(_ssd_kernel, n_heads=n_heads, d_inner=d_inner)
    small = lambda shape: pl.BlockSpec(shape, lambda b, c: (0, 0))
    return pl.pallas_call(
        kern,
        grid=(batch, nc),
        in_specs=[
            pl.BlockSpec((L, conv_dim), lambda b, c: (b * nc + c, xbc_block)),
            pl.BlockSpec((L, d_inner), lambda b, c: (b * nc + c, z_block)),
            pl.BlockSpec((L, n_heads), lambda b, c: (b * nc + c, 0)),
            pl.BlockSpec((n_heads, L), lambda b, c: (0, b * nc + c)),
            small((SSD_CONV_WIDTH, conv_dim)),
            small((1, conv_dim)),
            small((1, n_heads)),
            small((n_heads, 1)),
            small((1, n_heads)),
            small((n_heads, 1)),
            small((1, n_heads)),
            small((1, d_inner)),
            small((n_heads, d_inner)),
        ],
        out_specs=pl.BlockSpec((L, d_inner), lambda b, c: (b * nc + c, 0)),
        out_shape=jax.ShapeDtypeStruct((n, d_inner), BF16),
        scratch_shapes=[
            pltpu.VMEM((L + SUBLANES, conv_dim), F32),
            pltpu.VMEM((SSD_D_STATE, d_inner), F32),
            pltpu.VMEM((L, d_inner), F32),
        ],
        compiler_params=_params("arbitrary", "arbitrary"),
        name="ssd",
    )(proj, proj, dt, dtt, conv_w, conv_b.reshape(1, -1), dt_bias.reshape(1, -1),
      dt_bias.reshape(-1, 1), a_log.reshape(1, -1), a_log.reshape(-1, 1), d_skip.reshape(1, -1),
      norm_w.reshape(1, -1), expand)


def _attn_kernel(q_ref, k_ref, v_ref, o_ref, lse_ref, kp_ref, vp_ref, *, slopes, dilation, hops):
    nb = pl.program_id(2)
    BLK = ATTN_BLOCK
    E = ATTN_HEAD_DIM

    @pl.when(nb == 0)
    def _():
        kp_ref[...] = jnp.zeros(kp_ref.shape, kp_ref.dtype)
        vp_ref[...] = jnp.zeros(vp_ref.shape, vp_ref.dtype)

    qi = lax.broadcasted_iota(jnp.int32, (BLK, BLK), 0)
    ki = lax.broadcasted_iota(jnp.int32, (BLK, BLK), 1)
    rel_cur = qi - ki
    rel_prev = rel_cur + BLK
    ok_cur = rel_cur >= 0
    ok_prev = rel_prev <= jnp.where(nb > 0, hops, -1)
    dist_cur = (rel_cur * dilation).astype(F32)
    dist_prev = (rel_prev * dilation).astype(F32)
    lane = lax.broadcasted_iota(jnp.int32, (BLK, LANES), 1)
    scale = E ** -0.5
    nt = (((1,), (1,)), ((), ()))

    q = q_ref[...]
    k = k_ref[...]
    v = v_ref[...]
    kp = kp_ref[...]
    vp = vp_ref[...]
    lse_tile = jnp.zeros((BLK, LANES), F32)
    for h, slope in enumerate(slopes):
        sl = slice(h * E, (h + 1) * E)
        qh = q[:, sl]
        s_cur = lax.dot_general(qh, k[:, sl], nt, preferred_element_type=F32)
        s_prev = lax.dot_general(qh, kp[:, sl], nt, preferred_element_type=F32)
        l_cur = jnp.where(ok_cur, s_cur * scale - slope * dist_cur, -jnp.inf)
        l_prev = jnp.where(ok_prev, s_prev * scale - slope * dist_prev, -jnp.inf)
        m = jnp.max(jnp.maximum(l_cur, l_prev), axis=-1, keepdims=True)
        p_cur = jnp.exp(l_cur - m)
        p_prev = jnp.exp(l_prev - m)
        den = jnp.sum(p_cur + p_prev, axis=-1, keepdims=True)
        acc = (jnp.dot(p_cur.astype(BF16), v[:, sl], preferred_element_type=F32)
               + jnp.dot(p_prev.astype(BF16), vp[:, sl], preferred_element_type=F32))
        o_ref[:, sl] = (acc / den).astype(o_ref.dtype)
        lse_tile = jnp.where(lane == h, m + jnp.log(den), lse_tile)
    lse_ref[...] = lse_tile
    kp_ref[...] = k
    vp_ref[...] = v


def _attn_group(qkv, gi, *, batch, seq):
    window, dilation = DILATION_PATTERNS[gi]
    hops = window // dilation
    n_heads_total = ATTN_HEADS_PER_GROUP * len(DILATION_PATTERNS)
    slopes = tuple(float(2.0 ** (-8.0 * (gi * ATTN_HEADS_PER_GROUP + h + 1) / n_heads_total))
                   for h in range(ATTN_HEADS_PER_GROUP))
    gw = ATTN_HEADS_PER_GROUP * ATTN_HEAD_DIM
    assert seq % (dilation * ATTN_BLOCK) == 0
    sub = seq // dilation
    nb = sub // ATTN_BLOCK
    kern = functools.partial(_attn_kernel, slopes=slopes, dilation=dilation, hops=hops)

    def part(p):
        return pl.BlockSpec((ATTN_BLOCK, gw), lambda b, r, n: (b * nb + n, p * dilation + r))

    o, lse = pl.pallas_call(
        kern,
        grid=(batch, dilation, nb),
        in_specs=[part(0), part(1), part(2)],
        out_specs=[
            pl.BlockSpec((ATTN_BLOCK, gw), lambda b, r, n: (b * nb + n, r)),
            pl.BlockSpec((ATTN_BLOCK, LANES), lambda b, r, n: (b * nb + n, r)),
        ],
        out_shape=[
            jax.ShapeDtypeStruct((batch * sub, dilation * gw), BF16),
            jax.ShapeDtypeStruct((batch * sub, dilation * LANES), F32),
        ],
        scratch_shapes=[pltpu.VMEM((ATTN_BLOCK, gw), BF16), pltpu.VMEM((ATTN_BLOCK, gw), BF16)],
        compiler_params=_params("arbitrary", "arbitrary", "arbitrary"),
        name=f"attn_g{gi}",
    )(qkv, qkv, qkv)
    return o.reshape(batch * seq, gw), lse.reshape(batch * seq, LANES)


def _mix_kernel(yn_ref, o0_ref, o1_ref, o2_ref, l0_ref, l1_ref, l2_ref, gate_ref, x_ref,
                wssd_ref, wattn_ref, wout_ref, bg_ref, fnw_ref, wrh_ref, wrl_ref, rb_ref,
                x1_ref, hn_ref, eid_ref, gcol_ref, rank_ref, cnt_ref, carry_ref, *, d_model):
    i = pl.program_id(0)
    tm = x_ref.shape[0]
    E = ATTN_HEAD_DIM

    @pl.when(i == 0)
    def _():
        carry_ref[...] = jnp.zeros(carry_ref.shape, F32)

    y_ssd = jnp.dot(yn_ref[...], wssd_ref[...], preferred_element_type=F32)

    l0, l1, l2 = l0_ref[...], l1_ref[...], l2_ref[...]
    lm = jnp.maximum(jnp.maximum(l0, l1), l2)
    e0, e1, e2 = jnp.exp(l0 - lm), jnp.exp(l1 - lm), jnp.exp(l2 - lm)
    inv = 1.0 / (e0 + e1 + e2)
    parts = []
    for h in range(ATTN_HEADS_PER_GROUP):
        sl = slice(h * E, (h + 1) * E)
        parts.append((e0[:, h:h + 1] * inv[:, h:h + 1]) * o0_ref[:, sl].astype(F32)
                     + (e1[:, h:h + 1] * inv[:, h:h + 1]) * o1_ref[:, sl].astype(F32)
                     + (e2[:, h:h + 1] * inv[:, h:h + 1]) * o2_ref[:, sl].astype(F32))
    o = jnp.concatenate(parts, axis=-1).astype(BF16)
    y_attn = jnp.dot(o, wattn_ref[...], preferred_element_type=F32)

    graw = gate_ref[...].astype(F32) + bg_ref[...]
    gates = 1.0 / (1.0 + jnp.exp(-graw))
    merged = (gates[:, :d_model] * y_ssd + gates[:, d_model:] * y_attn).astype(BF16)
    x1 = x_ref[...] + jnp.dot(merged, wout_ref[...], preferred_element_type=F32)
    x1_ref[...] = x1

    ms = jnp.mean(x1 * x1, axis=-1, keepdims=True)
    hn = x1 * lax.rsqrt(ms + NORM_EPS) * fnw_ref[...]
    hn_ref[...] = hn

    hn_hi, hn_lo = _split_bf16(hn)
    nt = (((1,), (1,)), ((), ()))
    logits = (lax.dot_general(wrh_ref[...], hn_hi, nt, preferred_element_type=F32)
              + lax.dot_general(wrh_ref[...], hn_lo, nt, preferred_element_type=F32)
              + lax.dot_general(wrl_ref[...], hn_hi, nt, preferred_element_type=F32)
              + rb_ref[...])

    grow = lax.broadcasted_iota(jnp.int32, (SUBLANES, tm), 0)
    gl = jnp.where(grow < N_EXPERT_GROUPS, logits[0:SUBLANES, :], -jnp.inf)
    gmax = jnp.max(gl, axis=0, keepdims=True)
    gidx = jnp.min(jnp.where(gl == gmax, grow, N_EXPERT_GROUPS), axis=0, keepdims=True)
    group_gate = 1.0 / jnp.sum(jnp.exp(gl - gmax), axis=0, keepdims=True)

    in_group = jnp.zeros((EXPERTS_PER_GROUP, tm), F32)
    for g in range(N_EXPERT_GROUPS):
        r0 = ROUTER_EXPERT_ROW0 + g * EXPERTS_PER_GROUP
        in_group = jnp.where(gidx == g, logits[r0:r0 + EXPERTS_PER_GROUP, :], in_group)
    erow = lax.broadcasted_iota(jnp.int32, in_group.shape, 0)
    v1 = jnp.max(in_group, axis=0, keepdims=True)
    i1 = jnp.min(jnp.where(in_group == v1, erow, EXPERTS_PER_GROUP), axis=0, keepdims=True)
    rest = jnp.where(erow == i1, -jnp.inf, in_group)
    v2 = jnp.max(rest, axis=0, keepdims=True)
    i2 = jnp.min(jnp.where(rest == v2, erow, EXPERTS_PER_GROUP), axis=0, keepdims=True)
    t = jnp.exp(v2 - v1)
    g1 = group_gate / (1.0 + t)
    g2 = group_gate * t / (1.0 + t)
    eid1 = gidx * EXPERTS_PER_GROUP + i1
    eid2 = gidx * EXPERTS_PER_GROUP + i2
    slot = lax.broadcasted_iota(jnp.int32, (SUBLANES, tm), 0)
    eid_ref[...] = jnp.where(slot == 0, eid1, jnp.where(slot == 1, eid2, 0))

    grow8 = lax.broadcasted_iota(jnp.int32, (LANES, tm), 0)
    gt = jnp.where(grow8 == 0, g1, jnp.where(grow8 == 1, g2, 0.0))
    gcol_ref[...] = gt.T

    xrow = lax.broadcasted_iota(jnp.int32, (N_EXPERTS, tm), 0)
    oh1 = xrow == eid1
    oh2 = xrow == eid2
    oh = jnp.where(oh1 | oh2, 1.0, 0.0)
    ti = lax.broadcasted_iota(jnp.int32, (tm, tm), 0)
    tj = lax.broadcasted_iota(jnp.int32, (tm, tm), 1)
    before = jnp.where(ti < tj, 1.0, 0.0).astype(BF16)
    prior = jnp.dot(oh.astype(BF16), before, preferred_element_type=F32) + carry_ref[:, 0:1]
    r1 = jnp.sum(jnp.where(oh1, prior, 0.0), axis=0, keepdims=True)
    r2 = jnp.sum(jnp.where(oh2, prior, 0.0), axis=0, keepdims=True)
    rank_ref[...] = jnp.where(slot == 0, r1, jnp.where(slot == 1, r2, 0.0)).astype(jnp.int32)
    carry_ref[...] = carry_ref[...] + jnp.sum(oh, axis=1, keepdims=True)
    cnt_ref[...] = carry_ref[...].astype(jnp.int32)


def _mix(yn, outs, lses, proj, x2, w_ssd_out, w_attn_out, w_out, b_gate, ffn_norm_w, wr, rb, *,
         tm, gate_block):
    n, d = x2.shape
    aw = outs[0].shape[1]
    wrt = jnp.zeros((ROUTER_ROWS, d), F32)
    wrt = wrt.at[0:N_EXPERT_GROUPS].set(wr[0].T).at[ROUTER_EXPERT_ROW0:].set(wr[1].T)
    rbc = jnp.zeros((ROUTER_ROWS, 1), F32)
    rbc = rbc.at[0:N_EXPERT_GROUPS, 0].set(rb[0]).at[ROUTER_EXPERT_ROW0:, 0].set(rb[1])
    wrh, wrl = _split_bf16(wrt)
    row = lambda w: pl.BlockSpec((tm, w), lambda i: (i, 0))
    const = lambda shape: pl.BlockSpec(shape, lambda i: (0, 0), pipeline_mode=pl.Buffered(1))
    kern = functools.partial(_mix_kernel, d_model=d)
    return pl.pallas_call(
        kern,
        grid=(n // tm,),
        in_specs=[
            row(d), row(aw), row(aw), row(aw), row(LANES), row(LANES), row(LANES),
            pl.BlockSpec((tm, 2 * d), lambda i: (i, gate_block)),
            row(d),
            const((d, d)), const((aw, d)), const((d, d)), const((1, 2 * d)), const((1, d)),
            const((ROUTER_ROWS, d)), const((ROUTER_ROWS, d)), const((ROUTER_ROWS, 1)),
        ],
        out_specs=[
            row(d), row(d),
            pl.BlockSpec((SUBLANES, tm), lambda i: (0, i)),
            row(LANES),
            pl.BlockSpec((SUBLANES, tm), lambda i: (0, i)),
            pl.BlockSpec((N_EXPERTS, LANES), lambda i: (0, 0)),
        ],
        out_shape=[
            jax.ShapeDtypeStruct((n, d), F32),
            jax.ShapeDtypeStruct((n, d), F32),
            jax.ShapeDtypeStruct((SUBLANES, n), jnp.int32),
            jax.ShapeDtypeStruct((n, LANES), F32),
            jax.ShapeDtypeStruct((SUBLANES, n), jnp.int32),
            jax.ShapeDtypeStruct((N_EXPERTS, LANES), jnp.int32),
        ],
        scratch_shapes=[pltpu.VMEM((N_EXPERTS, LANES), F32)],
        compiler_params=_params("arbitrary"),
        name="mix",
    )(yn, outs[0], outs[1], outs[2], lses[0], lses[1], lses[2], proj, x2,
      w_ssd_out.astype(BF16), w_attn_out.astype(BF16), w_out.astype(BF16),
      b_gate.reshape(1, -1), ffn_norm_w.reshape(1, -1), wrh, wrl, rbc)


def _scatter_kernel(pad_start_ref, pad_len_ref, n_active_ref, pos_ref, hn_ref, xs_ref, zero_ref,
                    sem, zsem, *, tme, n_tiles):
    tm = hn_ref.shape[0]
    zrows = zero_ref.shape[0]

    @pl.when(pl.program_id(0) == 0)
    def _():
        zero_ref[...] = jnp.zeros(zero_ref.shape, zero_ref.dtype)

        def fills(act):
            def per_expert(e, carry):
                off = pad_start_ref[e]
                left = pad_len_ref[e]
                head = left & (SUBLANES - 1)
                for r in range(SUBLANES - 1):
                    @pl.when(r < head)
                    def _(r=r):
                        act(pltpu.make_async_copy(zero_ref.at[pl.ds(0, 1), :],
                                                  xs_ref.at[pl.ds(off + r, 1), :], zsem))

                off = off + head
                bit = zrows
                while bit >= SUBLANES:
                    take = left & bit

                    @pl.when(take != 0)
                    def _(off=off, bit=bit):
                        act(pltpu.make_async_copy(
                            zero_ref.at[pl.ds(0, bit), :],
                            xs_ref.at[pl.ds(pl.multiple_of(off, SUBLANES), bit), :], zsem))

                    off = off + take
                    bit //= 2
                return carry

            def per_tile(t, carry):
                @pl.when(t >= n_active_ref[0])
                def _():
                    for part in range(tme // zrows):
                        row0 = pl.multiple_of(t * tme + part * zrows, zrows)
                        act(pltpu.make_async_copy(zero_ref, xs_ref.at[pl.ds(row0, zrows), :], zsem))

                return carry

            lax.fori_loop(0, N_EXPERTS, per_expert, 0)
            lax.fori_loop(0, n_tiles, per_tile, 0)

        fills(lambda cp: cp.start())
        fills(lambda cp: cp.wait())

    def copy(t, k):
        dst = pos_ref[0, 0, k * tm + t]
        return pltpu.make_async_copy(hn_ref.at[pl.ds(t, 1), :], xs_ref.at[pl.ds(dst, 1), :], sem)

    def start(t, carry):
        for k in range(EXPERT_TOP_K):
            copy(t, k).start()
        return carry

    def wait(t, carry):
        for k in range(EXPERT_TOP_K):
            copy(t, k).wait()
        return carry

    lax.fori_loop(0, tm, start, 0)
    lax.fori_loop(0, tm, wait, 0)


def _scatter(hn, pos_tiles, pad_start, pad_len, n_active, *, tm, tme, n_tiles):
    n, d = hn.shape
    assert tme % 2 == 0 and (tme // 2) & (tme // 2 - 1) == 0
    grid_spec = pltpu.PrefetchScalarGridSpec(
        num_scalar_prefetch=3,
        grid=(n // tm,),
        in_specs=[
            pl.BlockSpec((1, 1, EXPERT_TOP_K * tm), lambda i, *_: (i, 0, 0),
                         memory_space=pltpu.SMEM),
            pl.BlockSpec((tm, d), lambda i, *_: (i, 0)),
        ],
        out_specs=pl.BlockSpec(memory_space=pl.ANY),
        scratch_shapes=[pltpu.VMEM((tme // 2, d), hn.dtype), pltpu.SemaphoreType.DMA(()),
                        pltpu.SemaphoreType.DMA(())],
    )
    return pl.pallas_call(
        functools.partial(_scatter_kernel, tme=tme, n_tiles=n_tiles),
        grid_spec=grid_spec,
        out_shape=jax.ShapeDtypeStruct((n_tiles * tme, d), hn.dtype),
        compiler_params=_params("arbitrary"),
        name="scatter",
    )(pad_start, pad_len, n_active, pos_tiles, hn)


def _experts_kernel(tile_expert_ref, n_active_ref, xs_ref, wg_ref, wu_ref, wd_ref, ys_ref):
    i = pl.program_id(0)

    @pl.when(i < n_active_ref[0])
    def _():
        xb = xs_ref[...].astype(BF16)
        hg = jnp.dot(xb, wg_ref[0], preferred_element_type=F32)
        hu = jnp.dot(xb, wu_ref[0], preferred_element_type=F32)
        hmid = (_silu(hg) * hu).astype(BF16)
        ys_ref[...] = jnp.dot(hmid, wd_ref[0], preferred_element_type=F32)

    @pl.when(i >= n_active_ref[0])
    def _():
        ys_ref[...] = jnp.zeros(ys_ref.shape, F32)


def _experts(xs, tile_expert, n_active, w_g, w_u, w_d, *, tm):
    p, d = xs.shape
    ne, _, f = w_g.shape
    n_tiles = p // tm

    def rows(i, te, na):
        return (jnp.minimum(i, na[0] - 1), 0)

    def wsel(i, te, na):
        return (te[jnp.minimum(i, na[0] - 1)], 0, 0)

    grid_spec = pltpu.PrefetchScalarGridSpec(
        num_scalar_prefetch=2,
        grid=(n_tiles,),
        in_specs=[
            pl.BlockSpec((tm, d), rows),
            pl.BlockSpec((1, d, f), wsel),
            pl.BlockSpec((1, d, f), wsel),
            pl.BlockSpec((1, f, d), wsel),
        ],
        out_specs=pl.BlockSpec((tm, d), lambda i, te, na: (i, 0)),
    )
    return pl.pallas_call(
        _experts_kernel,
        grid_spec=grid_spec,
        out_shape=jax.ShapeDtypeStruct((p, d), F32),
        compiler_params=_params("arbitrary"),
        name="experts",
    )(tile_expert, n_active, xs, w_g, w_u, w_d)


def _combine_kernel(pos_ref, ys_ref, x1_ref, gcol_ref, nw_ref, out_ref, buf_ref, sem, *, final):
    tm = x1_ref.shape[0]

    def copy(t, k):
        src = pos_ref[0, 0, k * tm + t]
        return pltpu.make_async_copy(ys_ref.at[pl.ds(src, 1), :], buf_ref.at[k, pl.ds(t, 1), :], sem)

    def start(t, carry):
        for k in range(EXPERT_TOP_K):
            copy(t, k).start()
        return carry

    def wait(t, carry):
        for k in range(EXPERT_TOP_K):
            copy(t, k).wait()
        return carry

    lax.fori_loop(0, tm, start, 0)
    lax.fori_loop(0, tm, wait, 0)
    g = gcol_ref[...]
    xo = x1_ref[...] + g[:, 0:1] * buf_ref[0] + g[:, 1:2] * buf_ref[1]
    if final:
        ms = jnp.mean(xo * xo, axis=-1, keepdims=True)
        xo = xo * lax.rsqrt(ms + NORM_EPS) * nw_ref[...]
    out_ref[...] = xo


def _combine(ys, pos_tiles, x1, gcol, norm_w, *, tm, final):
    n, d = x1.shape
    return pl.pallas_call(
        functools.partial(_combine_kernel, final=final),
        grid=(n // tm,),
        in_specs=[
            pl.BlockSpec((1, 1, EXPERT_TOP_K * tm), lambda i: (i, 0, 0), memory_space=pltpu.SMEM),
            pl.BlockSpec(memory_space=pl.ANY),
            pl.BlockSpec((tm, d), lambda i: (i, 0)),
            pl.BlockSpec((tm, LANES), lambda i: (i, 0)),
            pl.BlockSpec((1, d), lambda i: (0, 0)),
        ],
        out_specs=pl.BlockSpec((tm, d), lambda i: (i, 0)),
        out_shape=jax.ShapeDtypeStruct((n, d), F32),
        scratch_shapes=[pltpu.VMEM((EXPERT_TOP_K, tm, d), F32), pltpu.SemaphoreType.DMA(())],
        compiler_params=_params("arbitrary"),
        name="combine",
    )(pos_tiles, ys, x1, gcol, norm_w.reshape(1, d))


def _tiles(n):
    return dict(proj_tm=math.gcd(n, 1024), mix_tm=math.gcd(n, 256), moe_tm=256,
                route_tm=math.gcd(n, 256))


def _layer(x2, batch, seq, attn_norm_w, w_in, b_gate, conv_w, conv_b, dt_bias, a_log, d_skip,
           ssd_norm_w, w_ssd_out, w_attn_out, w_out, ffn_norm_w, w_gr, b_gr, w_er, b_er,
           w_g, w_u, w_d):
    n, d = x2.shape
    n_heads = dt_bias.shape[0]
    d_inner = ssd_norm_w.shape[0]
    conv_dim = conv_w.shape[1]
    aw_total = ATTN_HEADS_PER_GROUP * len(DILATION_PATTERNS) * ATTN_HEAD_DIM
    gw = ATTN_HEADS_PER_GROUP * ATTN_HEAD_DIM
    tiles = _tiles(n)

    c_z, c_xbc, c_dt = d_inner, d_inner + conv_dim, d_inner + conv_dim + n_heads
    c_gate = c_dt + QKV_PARTS * aw_total
    qkv_cols = [w_in[:, c_dt + p * aw_total + gi * gw:c_dt + p * aw_total + (gi + 1) * gw]
                for gi in range(len(DILATION_PATTERNS)) for p in range(QKV_PARTS)]
    w_main = jnp.concatenate([w_in[:, c_gate:], w_in[:, c_z:c_xbc], w_in[:, :c_z]] + qkv_cols,
                             axis=1).astype(BF16)
    w_dt = w_in[:, c_xbc:c_dt]
    off_gate, off_xbc, off_z = 0, 2 * d, 2 * d + conv_dim
    plain_cols = off_z + d_inner
    assert off_xbc % conv_dim == 0 and off_z % d_inner == 0 and plain_cols % gw == 0

    proj, qkv0, qkv1, qkv2, dt, dtt = _in_proj(x2, attn_norm_w, w_main, w_dt, tm=tiles["proj_tm"],
                                               tn=gw, n_plain=plain_cols // gw)

    yn = _ssd(proj, dt, dtt, conv_w, conv_b, dt_bias, a_log, d_skip, ssd_norm_w, batch=batch,
              seq=seq, d_inner=d_inner, xbc_block=off_xbc // conv_dim, z_block=off_z // d_inner)

    outs, lses = [], []
    for gi, qkv in enumerate((qkv0, qkv1, qkv2)):
        o_g, lse_g = _attn_group(qkv, gi, batch=batch, seq=seq)
        outs.append(o_g)
        lses.append(lse_g)

    x1, hn, eid, gcol, rank, counts = _mix(
        yn, outs, lses, proj, x2, w_ssd_out, w_attn_out, w_out, b_gate, ffn_norm_w,
        (w_gr, w_er), (b_gr, b_er), tm=tiles["mix_tm"], gate_block=off_gate // (2 * d))

    tme = tiles["moe_tm"]
    cnt = counts[:, 0]
    padded = ((cnt + tme - 1) // tme) * tme
    ends = jnp.cumsum(padded)
    starts = ends - padded
    experts = jnp.arange(N_EXPERTS, dtype=jnp.int32)[:, None, None]
    pos = rank[:EXPERT_TOP_K] + jnp.sum(
        jnp.where(eid[None, :EXPERT_TOP_K] == experts, starts[:, None, None], 0), axis=0)
    n_tiles = EXPERT_TOP_K * n // tme + N_EXPERTS
    tile_start = jnp.arange(n_tiles, dtype=jnp.int32) * tme
    tile_expert = jnp.minimum(jnp.sum(ends[None, :] <= tile_start[:, None], axis=1),
                              N_EXPERTS - 1).astype(jnp.int32)
    n_active = (ends[-1:] // tme).astype(jnp.int32)

    rtm = tiles["route_tm"]
    pos_tiles = pos.reshape(EXPERT_TOP_K, n // rtm, rtm).transpose(1, 0, 2).reshape(
        n // rtm, 1, EXPERT_TOP_K * rtm)
    xs = _scatter(hn, pos_tiles, (starts + cnt).astype(jnp.int32), (padded - cnt).astype(jnp.int32),
                  n_active, tm=rtm, tme=tme, n_tiles=n_tiles)
    ys = _experts(xs, tile_expert, n_active, w_g.astype(BF16), w_u.astype(BF16),
                  w_d.astype(BF16), tm=tme)
    return ys, pos_tiles, x1, gcol, rtm


def kernel(x, attn_norm_w, w_in, b_gate, conv_w, conv_b, dt_bias, a_log, d_skip, ssd_norm_w,
           w_ssd_out, w_attn_out, w_out, ffn_norm_w, w_group_router, b_group_router,
           w_expert_router, b_expert_router, w_exp_gate, w_exp_up, w_exp_down, final_norm_w):
    batch, seq, d = x.shape
    depth = w_in.shape[0]
    x2 = x.reshape(batch * seq, d)
    for layer in range(depth):
        ys, pos_tiles, x1, gcol, rtm = _layer(
            x2, batch, seq, attn_norm_w[layer], w_in[layer], b_gate[layer], conv_w[layer],
            conv_b[layer], dt_bias[layer], a_log[layer], d_skip[layer], ssd_norm_w[layer],
            w_ssd_out[layer], w_attn_out[layer], w_out[layer], ffn_norm_w[layer],
            w_group_router[layer], b_group_router[layer], w_expert_router[layer],
            b_expert_router[layer], w_exp_gate[layer], w_exp_up[layer], w_exp_down[layer])
        x2 = _combine(ys, pos_tiles, x1, gcol, final_norm_w, tm=rtm, final=layer == depth - 1)
    return x2.reshape(batch, seq, d)
```

```python
import functools
import math

import jax
import jax.numpy as jnp
import numpy as np
from jax import lax
from jax.experimental import pallas as pl
from jax.experimental.pallas import tpu as pltpu

F32 = jnp.float32
BF16 = jnp.bfloat16

NORM_EPS = 1e-6
SSD_HEAD_DIM = 64
SSD_N_GROUPS = 8
SSD_D_STATE = 128
SSD_CONV_WIDTH = 4
SSD_CHUNK = 128
ATTN_HEAD_DIM = 128
DILATION_PATTERNS = ((128, 1), (512, 4), (2048, 16))
ATTN_HEADS_PER_GROUP = 4
ATTN_BLOCK = 128
N_EXPERT_GROUPS = 4
EXPERTS_PER_GROUP = 8
N_EXPERTS = N_EXPERT_GROUPS * EXPERTS_PER_GROUP
EXPERT_TOP_K = 2

LANES = 128
SUBLANES = 8
VMEM_LIMIT_BYTES = 56 * 1024 * 1024
ROW_DMA_UNROLL = 8

ROUTER_EXPERT_ROW0 = SUBLANES
ROUTER_ROWS = ROUTER_EXPERT_ROW0 + N_EXPERTS


def _params(*semantics):
    return pltpu.CompilerParams(dimension_semantics=semantics, vmem_limit_bytes=VMEM_LIMIT_BYTES)


def _split_bf16(v):
    hi = v.astype(BF16)
    lo = (v - hi.astype(F32)).astype(BF16)
    return hi, lo


def _silu(v):
    return v * (1.0 / (1.0 + jnp.exp(-v)))


QKV_PARTS = 3


def _in_proj_kernel(x_ref, nw_ref, w_ref, wdt_ref, wdtt_ref, proj_ref, a0_ref, a1_ref, a2_ref,
                    dt_ref, dtt_ref, h_ref, stage_ref, *, n_plain):
    j = pl.program_id(1)
    tm, tn = proj_ref.shape

    @pl.when(j == 0)
    def _():
        xf = x_ref[...]
        ms = jnp.mean(xf * xf, axis=-1, keepdims=True)
        h = (xf * lax.rsqrt(ms + NORM_EPS) * nw_ref[...]).astype(BF16)
        h_ref[...] = h
        dt_ref[...] = jnp.dot(h, wdt_ref[...], preferred_element_type=F32)
        dtt_ref[...] = lax.dot_general(wdtt_ref[...], h, (((1,), (1,)), ((), ())),
                                       preferred_element_type=F32)

    @pl.when(j < n_plain)
    def _():
        proj_ref[...] = jnp.dot(h_ref[...], w_ref[...], preferred_element_type=F32).astype(BF16)

    for gi, a_ref in enumerate((a0_ref, a1_ref, a2_ref)):
        dil = DILATION_PATTERNS[gi][1]
        j0 = n_plain + QKV_PARTS * gi

        @pl.when((j >= j0) & (j < j0 + QKV_PARTS))
        def _(a_ref=a_ref, dil=dil):
            res = jnp.dot(h_ref[...], w_ref[...], preferred_element_type=F32)
            if dil == 1:
                a_ref[...] = res.astype(BF16)
            else:
                for s in range(tn // LANES):
                    stage_ref[s] = res[:, s * LANES:(s + 1) * LANES]
                for r in range(dil):
                    for s in range(tn // LANES):
                        c0 = r * tn + s * LANES
                        a_ref[:, c0:c0 + LANES] = stage_ref[
                            s, pl.ds(r, tm // dil, stride=dil), :].astype(BF16)


def _in_proj(x2, norm_w, w_main, w_dt, *, tm, tn, n_plain):
    n, d = x2.shape
    nh = w_dt.shape[1]
    n_blocks = w_main.shape[1] // tn
    assert n_blocks == n_plain + QKV_PARTS * len(DILATION_PATTERNS)

    def a_spec(gi):
        dil = DILATION_PATTERNS[gi][1]
        j0 = n_plain + QKV_PARTS * gi
        return pl.BlockSpec((tm // dil, dil * tn),
                            lambda i, j: (i, jnp.clip(j - j0, 0, QKV_PARTS - 1)))

    def a_shape(gi):
        dil = DILATION_PATTERNS[gi][1]
        return jax.ShapeDtypeStruct((n // dil, dil * QKV_PARTS * tn), BF16)

    return pl.pallas_call(
        functools.partial(_in_proj_kernel, n_plain=n_plain),
        grid=(n // tm, n_blocks),
        in_specs=[
            pl.BlockSpec((tm, d), lambda i, j: (i, 0)),
            pl.BlockSpec((1, d), lambda i, j: (0, 0)),
            pl.BlockSpec((d, tn), lambda i, j: (0, j)),
            pl.BlockSpec((d, nh), lambda i, j: (0, 0)),
            pl.BlockSpec((nh, d), lambda i, j: (0, 0)),
        ],
        out_specs=[
            pl.BlockSpec((tm, tn), lambda i, j: (i, jnp.minimum(j, n_plain - 1))),
            a_spec(0), a_spec(1), a_spec(2),
            pl.BlockSpec((tm, nh), lambda i, j: (i, 0)),
            pl.BlockSpec((nh, tm), lambda i, j: (0, i)),
        ],
        out_shape=[
            jax.ShapeDtypeStruct((n, n_plain * tn), BF16),
            a_shape(0), a_shape(1), a_shape(2),
            jax.ShapeDtypeStruct((n, nh), F32),
            jax.ShapeDtypeStruct((nh, n), F32),
        ],
        scratch_shapes=[pltpu.VMEM((tm, d), BF16), pltpu.VMEM((tn // LANES, tm, LANES), F32)],
        compiler_params=_params("arbitrary", "arbitrary"),
        name="in_proj",
    )(x2, norm_w.reshape(1, d), w_main, w_dt.astype(BF16), w_dt.T.astype(BF16))


def _ssd_kernel(xbc_ref, z_ref, dt_ref, dtt_ref, cw_ref, cb_ref, dtb_ref, dtbt_ref, alog_ref,
                alogt_ref, dskip_ref, nw_ref, expand_ref, out_ref,
                xbuf_ref, state_ref, y_ref, *, n_heads, d_inner):
    L = SSD_CHUNK
    P = SSD_HEAD_DIM
    NS = SSD_D_STATE
    G = SSD_N_GROUPS
    R = n_heads // G
    GW = R * P
    W = SSD_CONV_WIDTH
    c = pl.program_id(1)

    @pl.when(c == 0)
    def _():
        xbuf_ref[0:SUBLANES, :] = jnp.zeros((SUBLANES, xbuf_ref.shape[1]), F32)
        state_ref[...] = jnp.zeros(state_ref.shape, F32)

    @pl.when(c > 0)
    def _():
        xbuf_ref[0:SUBLANES, :] = xbuf_ref[L:L + SUBLANES, :]

    xbuf_ref[SUBLANES:SUBLANES + L, :] = xbc_ref[...].astype(F32)

    def conv(col0, width):
        acc = cb_ref[:, col0:col0 + width]
        for w in range(W):
            r0 = SUBLANES - (W - 1) + w
            acc = acc + cw_ref[w:w + 1, col0:col0 + width] * xbuf_ref[r0:r0 + L, col0:col0 + width]
        return _silu(acc)

    def softplus(v):
        return jnp.maximum(v, 0.0) + jnp.log1p(jnp.exp(-jnp.abs(v)))

    dt = softplus(dt_ref[...] + dtb_ref[...])
    dtt = softplus(dtt_ref[...] + dtbt_ref[...])
    da = dt * (-jnp.exp(alog_ref[...]))
    dat = dtt * (-jnp.exp(alogt_ref[...]))
    row = lax.broadcasted_iota(jnp.int32, (L, L), 0)
    col = lax.broadcasted_iota(jnp.int32, (L, L), 1)
    causal = row >= col
    tri = jnp.where(causal, 1.0, 0.0).astype(BF16)
    trit = jnp.where(row <= col, 1.0, 0.0).astype(BF16)

    def split3(v):
        a = v.astype(BF16)
        r1 = v - a.astype(F32)
        b = r1.astype(BF16)
        cc = (r1 - b.astype(F32)).astype(BF16)
        return a, b, cc

    a_cum = sum(jnp.dot(tri, p, preferred_element_type=F32) for p in split3(da))
    a_cumt = sum(jnp.dot(p, trit, preferred_element_type=F32) for p in split3(dat))
    a_last = a_cum[L - 1:L, :]

    expand = expand_ref[...]

    def expand_heads(v):
        hi, lo = _split_bf16(v)
        return (jnp.dot(hi, expand, preferred_element_type=F32)
                + jnp.dot(lo, expand, preferred_element_type=F32))

    dt_e = expand_heads(dt)
    in_scale_e = expand_heads(dt * jnp.exp(a_last - a_cum))
    out_scale_e = expand_heads(jnp.exp(a_cum))
    tail8 = jnp.concatenate([jnp.exp(a_last), dskip_ref[...],
                             jnp.zeros((SUBLANES - 2, n_heads), F32)], axis=0)
    tail8_e = expand_heads(tail8)
    chunk_decay_e = tail8_e[0:1, :]
    dskip_e = tail8_e[1:2, :]

    lane_head = lax.broadcasted_iota(jnp.int32, (L, GW), 1) // P

    for g in range(G):
        x0 = g * GW
        xs = conv(x0, GW)
        bm = conv(d_inner + g * NS, NS).astype(BF16)
        cm = conv(d_inner + G * NS + g * NS, NS).astype(BF16)
        cb = lax.dot_general(cm, bm, (((1,), (1,)), ((), ())), preferred_element_type=F32)
        xdt = (xs * dt_e[:, x0:x0 + GW]).astype(BF16)
        y = xs * dskip_e[:, x0:x0 + GW]
        for r in range(R):
            h = g * R + r
            seg = a_cum[:, h:h + 1] - a_cumt[h:h + 1, :]
            decay = jnp.exp(jnp.where(causal, seg, -jnp.inf))
            wgt = (cb * decay).astype(BF16)
            xh = jnp.where(lane_head == r, xdt, jnp.zeros_like(xdt))
            y = y + jnp.dot(wgt, xh, preferred_element_type=F32)
        st = state_ref[:, x0:x0 + GW]
        y = y + (jnp.dot(cm, st.astype(BF16), preferred_element_type=F32)
                 * out_scale_e[:, x0:x0 + GW])
        y_ref[:, x0:x0 + GW] = y
        xin = (xs * in_scale_e[:, x0:x0 + GW]).astype(BF16)
        st_new = lax.dot_general(bm, xin, (((0,), (0,)), ((), ())), preferred_element_type=F32)
        state_ref[:, x0:x0 + GW] = st * chunk_decay_e[:, x0:x0 + GW] + st_new

    yz = y_ref[...] * _silu(z_ref[...].astype(F32))
    ms = jnp.mean(yz * yz, axis=-1, keepdims=True)
    out_ref[...] = (yz * lax.rsqrt(ms + NORM_EPS) * nw_ref[...]).astype(out_ref.dtype)


def _ssd(proj, dt, dtt, conv_w, conv_b, dt_bias, a_log, d_skip, norm_w, *, batch, seq, d_inner,
         xbc_block, z_block):
    n = batch * seq
    n_heads = dt.shape[1]
    L = SSD_CHUNK
    nc = seq // L
    conv_dim = conv_w.shape[1]
    expand = (np.arange(d_inner)[None, :] // SSD_HEAD_DIM == np.arange(n_heads)[:, None])
    expand = jnp.asarray(expand, BF16)
    kern = functools.partial(_ssd_kernel, n_heads=n_heads, d_inner=d_inner)
    small = lambda shape: pl.BlockSpec(shape, lambda b, c: (0, 0))
    return pl.pallas_call(
        kern,
        grid=(batch, nc),
        in_specs=[
            pl.BlockSpec((L, conv_dim), lambda b, c: (b * nc + c, xbc_block)),
            pl.BlockSpec((L, d_inner), lambda b, c: (b * nc + c, z_block)),
            pl.BlockSpec((L, n_heads), lambda b, c: (b * nc + c, 0)),
            pl.BlockSpec((n_heads, L), lambda b, c: (0, b * nc + c)),
            small((SSD_CONV_WIDTH, conv_dim)),
            small((1, conv_dim)),
            small((1, n_heads)),
            small((n_heads, 1)),
            small((1, n_heads)),
            small((n_heads, 1)),
            small((1, n_heads)),
            small((1, d_inner)),
            small((n_heads, d_inner)),
        ],
        out_specs=pl.BlockSpec((L, d_inner), lambda b, c: (b * nc + c, 0)),
        out_shape=jax.ShapeDtypeStruct((n, d_inner), BF16),
        scratch_shapes=[
            pltpu.VMEM((L + SUBLANES, conv_dim), F32),
            pltpu.VMEM((SSD_D_STATE, d_inner), F32),
            pltpu.VMEM((L, d_inner), F32),
        ],
        compiler_params=_params("arbitrary", "arbitrary"),
        name="ssd",
    )(proj, proj, dt, dtt, conv_w, conv_b.reshape(1, -1), dt_bias.reshape(1, -1),
      dt_bias.reshape(-1, 1), a_log.reshape(1, -1), a_log.reshape(-1, 1), d_skip.reshape(1, -1),
      norm_w.reshape(1, -1), expand)


def _attn_kernel(q_ref, k_ref, v_ref, o_ref, lse_ref, kp_ref, vp_ref, *, slopes, dilation, hops):
    nb = pl.program_id(2)
    BLK = ATTN_BLOCK
    E = ATTN_HEAD_DIM

    @pl.when(nb == 0)
    def _():
        kp_ref[...] = jnp.zeros(kp_ref.shape, kp_ref.dtype)
        vp_ref[...] = jnp.zeros(vp_ref.shape, vp_ref.dtype)

    qi = lax.broadcasted_iota(jnp.int32, (BLK, BLK), 0)
    ki = lax.broadcasted_iota(jnp.int32, (BLK, BLK), 1)
    rel_cur = qi - ki
    rel_prev = rel_cur + BLK
    ok_cur = rel_cur >= 0
    ok_prev = rel_prev <= jnp.where(nb > 0, hops, -1)
    dist_cur = (rel_cur * dilation).astype(F32)
    dist_prev = (rel_prev * dilation).astype(F32)
    lane = lax.broadcasted_iota(jnp.int32, (BLK, LANES), 1)
    scale = E ** -0.5
    nt = (((1,), (1,)), ((), ()))

    q = q_ref[...]
    k = k_ref[...]
    v = v_ref[...]
    kp = kp_ref[...]
    vp = vp_ref[...]
    lse_tile = jnp.zeros((BLK, LANES), F32)
    for h, slope in enumerate(slopes):
        sl = slice(h * E, (h + 1) * E)
        qh = q[:, sl]
        s_cur = lax.dot_general(qh, k[:, sl], nt, preferred_element_type=F32)
        s_prev = lax.dot_general(qh, kp[:, sl], nt, preferred_element_type=F32)
        l_cur = jnp.where(ok_cur, s_cur * scale - slope * dist_cur, -jnp.inf)
        l_prev = jnp.where(ok_prev, s_prev * scale - slope * dist_prev, -jnp.inf)
        m = jnp.max(jnp.maximum(l_cur, l_prev), axis=-1, keepdims=True)
        p_cur = jnp.exp(l_cur - m)
        p_prev = jnp.exp(l_prev - m)
        den = jnp.sum(p_cur + p_prev, axis=-1, keepdims=True)
        acc = (jnp.dot(p_cur.astype(BF16), v[:, sl], preferred_element_type=F32)
               + jnp.dot(p_prev.astype(BF16), vp[:, sl], preferred_element_type=F32))
        o_ref[:, sl] = (acc / den).astype(o_ref.dtype)
        lse_tile = jnp.where(lane == h, m + jnp.log(den), lse_tile)
    lse_ref[...] = lse_tile
    kp_ref[...] = k
    vp_ref[...] = v


def _attn_group(qkv, gi, *, batch, seq):
    window, dilation = DILATION_PATTERNS[gi]
    hops = window // dilation
    n_heads_total = ATTN_HEADS_PER_GROUP * len(DILATION_PATTERNS)
    slopes = tuple(float(2.0 ** (-8.0 * (gi * ATTN_HEADS_PER_GROUP + h + 1) / n_heads_total))
                   for h in range(ATTN_HEADS_PER_GROUP))
    gw = ATTN_HEADS_PER_GROUP * ATTN_HEAD_DIM
    assert seq % (dilation * ATTN_BLOCK) == 0
    sub = seq // dilation
    nb = sub // ATTN_BLOCK
    kern = functools.partial(_attn_kernel, slopes=slopes, dilation=dilation, hops=hops)

    def part(p):
        return pl.BlockSpec((ATTN_BLOCK, gw), lambda b, r, n: (b * nb + n, p * dilation + r))

    o, lse = pl.pallas_call(
        kern,
        grid=(batch, dilation, nb),
        in_specs=[part(0), part(1), part(2)],
        out_specs=[
            pl.BlockSpec((ATTN_BLOCK, gw), lambda b, r, n: (b * nb + n, r)),
            pl.BlockSpec((ATTN_BLOCK, LANES), lambda b, r, n: (b * nb + n, r)),
        ],
        out_shape=[
            jax.ShapeDtypeStruct((batch * sub, dilation * gw), BF16),
            jax.ShapeDtypeStruct((batch * sub, dilation * LANES), F32),
        ],
        scratch_shapes=[pltpu.VMEM((ATTN_BLOCK, gw), BF16), pltpu.VMEM((ATTN_BLOCK, gw), BF16)],
        compiler_params=_params("arbitrary", "arbitrary", "arbitrary"),
        name=f"attn_g{gi}",
    )(qkv, qkv, qkv)
    return o.reshape(batch * seq, gw), lse.reshape(batch * seq, LANES)


def _mix_kernel(yn_ref, o0_ref, o1_ref, o2_ref, l0_ref, l1_ref, l2_ref, gate_ref, x_ref,
                wssd_ref, wattn_ref, wout_ref, bg_ref, fnw_ref, wrh_ref, wrl_ref, rb_ref,
                x1_ref, hn_ref, eid_ref, gcol_ref, rank_ref, cnt_ref, carry_ref, *, d_model):
    i = pl.program_id(0)
    tm = x_ref.shape[0]
    E = ATTN_HEAD_DIM

    @pl.when(i == 0)
    def _():
        carry_ref[...] = jnp.zeros(carry_ref.shape, F32)

    y_ssd = jnp.dot(yn_ref[...], wssd_ref[...], preferred_element_type=F32)

    l0, l1, l2 = l0_ref[...], l1_ref[...], l2_ref[...]
    lm = jnp.maximum(jnp.maximum(l0, l1), l2)
    e0, e1, e2 = jnp.exp(l0 - lm), jnp.exp(l1 - lm), jnp.exp(l2 - lm)
    inv = 1.0 / (e0 + e1 + e2)
    parts = []
    for h in range(ATTN_HEADS_PER_GROUP):
        sl = slice(h * E, (h + 1) * E)
        parts.append((e0[:, h:h + 1] * inv[:, h:h + 1]) * o0_ref[:, sl].astype(F32)
                     + (e1[:, h:h + 1] * inv[:, h:h + 1]) * o1_ref[:, sl].astype(F32)
                     + (e2[:, h:h + 1] * inv[:, h:h + 1]) * o2_ref[:, sl].astype(F32))
    o = jnp.concatenate(parts, axis=-1).astype(BF16)
    y_attn = jnp.dot(o, wattn_ref[...], preferred_element_type=F32)

    graw = gate_ref[...].astype(F32) + bg_ref[...]
    gates = 1.0 / (1.0 + jnp.exp(-graw))
    merged = (gates[:, :d_model] * y_ssd + gates[:, d_model:] * y_attn).astype(BF16)
    x1 = x_ref[...] + jnp.dot(merged, wout_ref[...], preferred_element_type=F32)
    x1_ref[...] = x1

    ms = jnp.mean(x1 * x1, axis=-1, keepdims=True)
    hn = x1 * lax.rsqrt(ms + NORM_EPS) * fnw_ref[...]
    hn_ref[...] = hn

    hn_hi, hn_lo = _split_bf16(hn)
    nt = (((1,), (1,)), ((), ()))
    logits = (lax.dot_general(wrh_ref[...], hn_hi, nt, preferred_element_type=F32)
              + lax.dot_general(wrh_ref[...], hn_lo, nt, preferred_element_type=F32)
              + lax.dot_general(wrl_ref[...], hn_hi, nt, preferred_element_type=F32)
              + rb_ref[...])

    grow = lax.broadcasted_iota(jnp.int32, (SUBLANES, tm), 0)
    gl = jnp.where(grow < N_EXPERT_GROUPS, logits[0:SUBLANES, :], -jnp.inf)
    gmax = jnp.max(gl, axis=0, keepdims=True)
    gidx = jnp.min(jnp.where(gl == gmax, grow, N_EXPERT_GROUPS), axis=0, keepdims=True)
    group_gate = 1.0 / jnp.sum(jnp.exp(gl - gmax), axis=0, keepdims=True)

    in_group = jnp.zeros((EXPERTS_PER_GROUP, tm), F32)
    for g in range(N_EXPERT_GROUPS):
        r0 = ROUTER_EXPERT_ROW0 + g * EXPERTS_PER_GROUP
        in_group = jnp.where(gidx == g, logits[r0:r0 + EXPERTS_PER_GROUP, :], in_group)
    erow = lax.broadcasted_iota(jnp.int32, in_group.shape, 0)
    v1 = jnp.max(in_group, axis=0, keepdims=True)
    i1 = jnp.min(jnp.where(in_group == v1, erow, EXPERTS_PER_GROUP), axis=0, keepdims=True)
    rest = jnp.where(erow == i1, -jnp.inf, in_group)
    v2 = jnp.max(rest, axis=0, keepdims=True)
    i2 = jnp.min(jnp.where(rest == v2, erow, EXPERTS_PER_GROUP), axis=0, keepdims=True)
    t = jnp.exp(v2 - v1)
    g1 = group_gate / (1.0 + t)
    g2 = group_gate * t / (1.0 + t)
    eid1 = gidx * EXPERTS_PER_GROUP + i1
    eid2 = gidx * EXPERTS_PER_GROUP + i2
    slot = lax.broadcasted_iota(jnp.int32, (SUBLANES, tm), 0)
    eid_ref[...] = jnp.where(slot == 0, eid1, jnp.where(slot == 1, eid2, 0))

    grow8 = lax.broadcasted_iota(jnp.int32, (LANES, tm), 0)
    gt = jnp.where(grow8 == 0, g1, jnp.where(grow8 == 1, g2, 0.0))
    gcol_ref[...] = gt.T

    xrow = lax.broadcasted_iota(jnp.int32, (N_EXPERTS, tm), 0)
    oh1 = xrow == eid1
    oh2 = xrow == eid2
    oh = jnp.where(oh1 | oh2, 1.0, 0.0)
    ti = lax.broadcasted_iota(jnp.int32, (tm, tm), 0)
    tj = lax.broadcasted_iota(jnp.int32, (tm, tm), 1)
    before = jnp.where(ti < tj, 1.0, 0.0).astype(BF16)
    prior = jnp.dot(oh.astype(BF16), before, preferred_element_type=F32) + carry_ref[:, 0:1]
    r1 = jnp.sum(jnp.where(oh1, prior, 0.0), axis=0, keepdims=True)
    r2 = jnp.sum(jnp.where(oh2, prior, 0.0), axis=0, keepdims=True)
    rank_ref[...] = jnp.where(slot == 0, r1, jnp.where(slot == 1, r2, 0.0)).astype(jnp.int32)
    carry_ref[...] = carry_ref[...] + jnp.sum(oh, axis=1, keepdims=True)
    cnt_ref[...] = carry_ref[...].astype(jnp.int32)


def _mix(yn, outs, lses, proj, x2, w_ssd_out, w_attn_out, w_out, b_gate, ffn_norm_w, wr, rb, *,
         tm, gate_block):
    n, d = x2.shape
    aw = outs[0].shape[1]
    wrt = jnp.zeros((ROUTER_ROWS, d), F32)
    wrt = wrt.at[0:N_EXPERT_GROUPS].set(wr[0].T).at[ROUTER_EXPERT_ROW0:].set(wr[1].T)
    rbc = jnp.zeros((ROUTER_ROWS, 1), F32)
    rbc = rbc.at[0:N_EXPERT_GROUPS, 0].set(rb[0]).at[ROUTER_EXPERT_ROW0:, 0].set(rb[1])
    wrh, wrl = _split_bf16(wrt)
    row = lambda w: pl.BlockSpec((tm, w), lambda i: (i, 0))
    const = lambda shape: pl.BlockSpec(shape, lambda i: (0, 0), pipeline_mode=pl.Buffered(1))
    kern = functools.partial(_mix_kernel, d_model=d)
    return pl.pallas_call(
        kern,
        grid=(n // tm,),
        in_specs=[
            row(d), row(aw), row(aw), row(aw), row(LANES), row(LANES), row(LANES),
            pl.BlockSpec((tm, 2 * d), lambda i: (i, gate_block)),
            row(d),
            const((d, d)), const((aw, d)), const((d, d)), const((1, 2 * d)), const((1, d)),
            const((ROUTER_ROWS, d)), const((ROUTER_ROWS, d)), const((ROUTER_ROWS, 1)),
        ],
        out_specs=[
            row(d), row(d),
            pl.BlockSpec((SUBLANES, tm), lambda i: (0, i)),
            row(LANES),
            pl.BlockSpec((SUBLANES, tm), lambda i: (0, i)),
            pl.BlockSpec((N_EXPERTS, LANES), lambda i: (0, 0)),
        ],
        out_shape=[
            jax.ShapeDtypeStruct((n, d), F32),
            jax.ShapeDtypeStruct((n, d), F32),
            jax.ShapeDtypeStruct((SUBLANES, n), jnp.int32),
            jax.ShapeDtypeStruct((n, LANES), F32),
            jax.ShapeDtypeStruct((SUBLANES, n), jnp.int32),
            jax.ShapeDtypeStruct((N_EXPERTS, LANES), jnp.int32),
        ],
        scratch_shapes=[pltpu.VMEM((N_EXPERTS, LANES), F32)],
        compiler_params=_params("arbitrary"),
        name="mix",
    )(yn, outs[0], outs[1], outs[2], lses[0], lses[1], lses[2], proj, x2,
      w_ssd_out.astype(BF16), w_attn_out.astype(BF16), w_out.astype(BF16),
      b_gate.reshape(1, -1), ffn_norm_w.reshape(1, -1), wrh, wrl, rbc)


def _scatter_kernel(pad_start_ref, pad_len_ref, n_active_ref, pos_ref, hn_ref, xs_ref, zero_ref,
                    sem, zsem, *, tme, n_tiles):
    tm = hn_ref.shape[0]
    zrows = zero_ref.shape[0]

    @pl.when(pl.program_id(0) == 0)
    def _():
        zero_ref[...] = jnp.zeros(zero_ref.shape, zero_ref.dtype)

        def fills(act):
            def per_expert(e, carry):
                off = pad_start_ref[e]
                left = pad_len_ref[e]
                head = left & (SUBLANES - 1)
                for r in range(SUBLANES - 1):
                    @pl.when(r < head)
                    def _(r=r):
                        act(pltpu.make_async_copy(zero_ref.at[pl.ds(0, 1), :],
                                                  xs_ref.at[pl.ds(off + r, 1), :], zsem))

                off = off + head
                bit = zrows
                while bit >= SUBLANES:
                    take = left & bit

                    @pl.when(take != 0)
                    def _(off=off, bit=bit):
                        act(pltpu.make_async_copy(
                            zero_ref.at[pl.ds(0, bit), :],
                            xs_ref.at[pl.ds(pl.multiple_of(off, SUBLANES), bit), :], zsem))

                    off = off + take
                    bit //= 2
                return carry

            def per_tile(t, carry):
                @pl.when(t >= n_active_ref[0])
                def _():
                    for part in range(tme // zrows):
                        row0 = pl.multiple_of(t * tme + part * zrows, zrows)
                        act(pltpu.make_async_copy(zero_ref, xs_ref.at[pl.ds(row0, zrows), :], zsem))

                return carry

            lax.fori_loop(0, N_EXPERTS, per_expert, 0)
            lax.fori_loop(0, n_tiles, per_tile, 0)

        fills(lambda cp: cp.start())
        fills(lambda cp: cp.wait())

    def start(tb, carry):
        for u in range(ROW_DMA_UNROLL):
            t = tb * ROW_DMA_UNROLL + u
            for k in range(EXPERT_TOP_K):
                dst = pos_ref[0, 0, k * tm + t]
                pltpu.make_async_copy(hn_ref.at[pl.ds(t, 1), :], xs_ref.at[pl.ds(dst, 1), :],
                                      sem).start(priority=(u + k) % 2)
        return carry

    lax.fori_loop(0, tm // ROW_DMA_UNROLL, start, 0)
    for k in range(EXPERT_TOP_K):
        pltpu.make_async_copy(hn_ref, xs_ref.at[pl.ds(0, tm), :], sem).wait()


def _scatter(hn, pos_tiles, pad_start, pad_len, n_active, *, tm, tme, n_tiles):
    n, d = hn.shape
    assert tme % 2 == 0 and (tme // 2) & (tme // 2 - 1) == 0
    grid_spec = pltpu.PrefetchScalarGridSpec(
        num_scalar_prefetch=3,
        grid=(n // tm,),
        in_specs=[
            pl.BlockSpec((1, 1, EXPERT_TOP_K * tm), lambda i, *_: (i, 0, 0),
                         memory_space=pltpu.SMEM),
            pl.BlockSpec((tm, d), lambda i, *_: (i, 0)),
        ],
        out_specs=pl.BlockSpec(memory_space=pl.ANY),
        scratch_shapes=[pltpu.VMEM((tme // 2, d), hn.dtype), pltpu.SemaphoreType.DMA(()),
                        pltpu.SemaphoreType.DMA(())],
    )
    return pl.pallas_call(
        functools.partial(_scatter_kernel, tme=tme, n_tiles=n_tiles),
        grid_spec=grid_spec,
        out_shape=jax.ShapeDtypeStruct((n_tiles * tme, d), hn.dtype),
        compiler_params=_params("arbitrary"),
        name="scatter",
    )(pad_start, pad_len, n_active, pos_tiles, hn)


def _experts_kernel(tile_expert_ref, n_active_ref, xs_ref, wg_ref, wu_ref, wd_ref, ys_ref):
    i = pl.program_id(0)

    @pl.when(i < n_active_ref[0])
    def _():
        xb = xs_ref[...].astype(BF16)
        hg = jnp.dot(xb, wg_ref[0], preferred_element_type=F32)
        hu = jnp.dot(xb, wu_ref[0], preferred_element_type=F32)
        hmid = (_silu(hg) * hu).astype(BF16)
        ys_ref[...] = jnp.dot(hmid, wd_ref[0], preferred_element_type=F32)

    @pl.when(i >= n_active_ref[0])
    def _():
        ys_ref[...] = jnp.zeros(ys_ref.shape, F32)


def _experts(xs, tile_expert, n_active, w_g, w_u, w_d, *, tm):
    p, d = xs.shape
    ne, _, f = w_g.shape
    n_tiles = p // tm

    def rows(i, te, na):
        return (jnp.minimum(i, na[0] - 1), 0)

    def wsel(i, te, na):
        return (te[jnp.minimum(i, na[0] - 1)], 0, 0)

    grid_spec = pltpu.PrefetchScalarGridSpec(
        num_scalar_prefetch=2,
        grid=(n_tiles,),
        in_specs=[
            pl.BlockSpec((tm, d), rows),
            pl.BlockSpec((1, d, f), wsel),
            pl.BlockSpec((1, d, f), wsel),
            pl.BlockSpec((1, f, d), wsel),
        ],
        out_specs=pl.BlockSpec((tm, d), lambda i, te, na: (i, 0)),
    )
    return pl.pallas_call(
        _experts_kernel,
        grid_spec=grid_spec,
        out_shape=jax.ShapeDtypeStruct((p, d), F32),
        compiler_params=_params("arbitrary"),
        name="experts",
    )(tile_expert, n_active, xs, w_g, w_u, w_d)


def _combine_kernel(pos_ref, pos_next_ref, ys_ref, x1_ref, gcol_ref, nw_ref, out_ref, buf_ref, sem,
                    *, final):
    i = pl.program_id(0)
    n_steps = pl.num_programs(0)
    tm = x1_ref.shape[0]

    def issue(p_ref, slot):
        def start(tb, carry):
            for u in range(ROW_DMA_UNROLL):
                t = tb * ROW_DMA_UNROLL + u
                for k in range(EXPERT_TOP_K):
                    src = p_ref[0, 0, k * tm + t]
                    pltpu.make_async_copy(ys_ref.at[pl.ds(src, 1), :],
                                          buf_ref.at[slot, k, pl.ds(t, 1), :],
                                          sem.at[slot]).start(priority=(u + k) % 2)
            return carry

        lax.fori_loop(0, tm // ROW_DMA_UNROLL, start, 0)

    slot = i % 2

    @pl.when(i == 0)
    def _():
        issue(pos_ref, 0)

    @pl.when(i + 1 < n_steps)
    def _():
        issue(pos_next_ref, 1 - slot)

    for k in range(EXPERT_TOP_K):
        pltpu.make_async_copy(ys_ref.at[pl.ds(0, tm), :], buf_ref.at[slot, k], sem.at[slot]).wait()
    g = gcol_ref[...]
    xo = x1_ref[...] + g[:, 0:1] * buf_ref[slot, 0] + g[:, 1:2] * buf_ref[slot, 1]
    if final:
        ms = jnp.mean(xo * xo, axis=-1, keepdims=True)
        xo = xo * lax.rsqrt(ms + NORM_EPS) * nw_ref[...]
    out_ref[...] = xo


def _combine(ys, pos_tiles, x1, gcol, norm_w, *, tm, final):
    n, d = x1.shape
    last = n // tm - 1
    return pl.pallas_call(
        functools.partial(_combine_kernel, final=final),
        grid=(n // tm,),
        in_specs=[
            pl.BlockSpec((1, 1, EXPERT_TOP_K * tm), lambda i: (i, 0, 0), memory_space=pltpu.SMEM),
            pl.BlockSpec((1, 1, EXPERT_TOP_K * tm), lambda i: (jnp.minimum(i + 1, last), 0, 0),
                         memory_space=pltpu.SMEM),
            pl.BlockSpec(memory_space=pl.ANY),
            pl.BlockSpec((tm, d), lambda i: (i, 0)),
            pl.BlockSpec((tm, LANES), lambda i: (i, 0)),
            pl.BlockSpec((1, d), lambda i: (0, 0)),
        ],
        out_specs=pl.BlockSpec((tm, d), lambda i: (i, 0)),
        out_shape=jax.ShapeDtypeStruct((n, d), F32),
        scratch_shapes=[pltpu.VMEM((2, EXPERT_TOP_K, tm, d), F32), pltpu.SemaphoreType.DMA((2,))],
        compiler_params=_params("arbitrary"),
        name="combine",
    )(pos_tiles, pos_tiles, ys, x1, gcol, norm_w.reshape(1, d))


def _tiles(n):
    return dict(proj_tm=math.gcd(n, 1024), mix_tm=math.gcd(n, 256), moe_tm=256,
                route_tm=math.gcd(n, 256))


def _layer(x2, batch, seq, attn_norm_w, w_in, b_gate, conv_w, conv_b, dt_bias, a_log, d_skip,
           ssd_norm_w, w_ssd_out, w_attn_out, w_out, ffn_norm_w, w_gr, b_gr, w_er, b_er,
           w_g, w_u, w_d):
    n, d = x2.shape
    n_heads = dt_bias.shape[0]
    d_inner = ssd_norm_w.shape[0]
    conv_dim = conv_w.shape[1]
    aw_total = ATTN_HEADS_PER_GROUP * len(DILATION_PATTERNS) * ATTN_HEAD_DIM
    gw = ATTN_HEADS_PER_GROUP * ATTN_HEAD_DIM
    tiles = _tiles(n)

    c_z, c_xbc, c_dt = d_inner, d_inner + conv_dim, d_inner + conv_dim + n_heads
    c_gate = c_dt + QKV_PARTS * aw_total
    qkv_cols = [w_in[:, c_dt + p * aw_total + gi * gw:c_dt + p * aw_total + (gi + 1) * gw]
                for gi in range(len(DILATION_PATTERNS)) for p in range(QKV_PARTS)]
    w_main = jnp.concatenate([w_in[:, c_gate:], w_in[:, c_z:c_xbc], w_in[:, :c_z]] + qkv_cols,
                             axis=1).astype(BF16)
    w_dt = w_in[:, c_xbc:c_dt]
    off_gate, off_xbc, off_z = 0, 2 * d, 2 * d + conv_dim
    plain_cols = off_z + d_inner
    assert off_xbc % conv_dim == 0 and off_z % d_inner == 0 and plain_cols % gw == 0

    proj, qkv0, qkv1, qkv2, dt, dtt = _in_proj(x2, attn_norm_w, w_main, w_dt, tm=tiles["proj_tm"],
                                               tn=gw, n_plain=plain_cols // gw)

    yn = _ssd(proj, dt, dtt, conv_w, conv_b, dt_bias, a_log, d_skip, ssd_norm_w, batch=batch,
              seq=seq, d_inner=d_inner, xbc_block=off_xbc // conv_dim, z_block=off_z // d_inner)

    outs, lses = [], []
    for gi, qkv in enumerate((qkv0, qkv1, qkv2)):
        o_g, lse_g = _attn_group(qkv, gi, batch=batch, seq=seq)
        outs.append(o_g)
        lses.append(lse_g)

    x1, hn, eid, gcol, rank, counts = _mix(
        yn, outs, lses, proj, x2, w_ssd_out, w_attn_out, w_out, b_gate, ffn_norm_w,
        (w_gr, w_er), (b_gr, b_er), tm=tiles["mix_tm"], gate_block=off_gate // (2 * d))

    tme = tiles["moe_tm"]
    cnt = counts[:, 0]
    padded = ((cnt + tme - 1) // tme) * tme
    ends = jnp.cumsum(padded)
    starts = ends - padded
    experts = jnp.arange(N_EXPERTS, dtype=jnp.int32)[:, None, None]
    pos = rank[:EXPERT_TOP_K] + jnp.sum(
        jnp.where(eid[None, :EXPERT_TOP_K] == experts, starts[:, None, None], 0), axis=0)
    n_tiles = EXPERT_TOP_K * n // tme + N_EXPERTS
    tile_start = jnp.arange(n_tiles, dtype=jnp.int32) * tme
    tile_expert = jnp.minimum(jnp.sum(ends[None, :] <= tile_start[:, None], axis=1),
                              N_EXPERTS - 1).astype(jnp.int32)
    n_active = (ends[-1:] // tme).astype(jnp.int32)

    rtm = tiles["route_tm"]
    pos_tiles = pos.reshape(EXPERT_TOP_K, n // rtm, rtm).transpose(1, 0, 2).reshape(
        n // rtm, 1, EXPERT_TOP_K * rtm)
    xs = _scatter(hn, pos_tiles, (starts + cnt).astype(jnp.int32), (padded - cnt).astype(jnp.int32),
                  n_active, tm=rtm, tme=tme, n_tiles=n_tiles)
    ys = _experts(xs, tile_expert, n_active, w_g.astype(BF16), w_u.astype(BF16),
                  w_d.astype(BF16), tm=tme)
    return ys, pos_tiles, x1, gcol, rtm


def kernel(x, attn_norm_w, w_in, b_gate, conv_w, conv_b, dt_bias, a_log, d_skip, ssd_norm_w,
           w_ssd_out, w_attn_out, w_out, ffn_norm_w, w_group_router, b_group_router,
           w_expert_router, b_expert_router, w_exp_gate, w_exp_up, w_exp_down, final_norm_w):
    batch, seq, d = x.shape
    depth = w_in.shape[0]
    x2 = x.reshape(batch * seq, d)
    for layer in range(depth):
        ys, pos_tiles, x1, gcol, rtm = _layer(
            x2, batch, seq, attn_norm_w[layer], w_in[layer], b_gate[layer], conv_w[layer],
            conv_b[layer], dt_bias[layer], a_log[layer], d_skip[layer], ssd_norm_w[layer],
            w_ssd_out[layer], w_attn_out[layer], w_out[layer], ffn_norm_w[layer],
            w_group_router[layer], b_group_router[layer], w_expert_router[layer],
            b_expert_router[layer], w_exp_gate[layer], w_exp_up[layer], w_exp_down[layer])
        x2 = _combine(ys, pos_tiles, x1, gcol, final_norm_w, tm=rtm, final=layer == depth - 1)
    return x2.reshape(batch, seq, d)
```

```python
import functools
import math

import jax
import jax.numpy as jnp
import numpy as np
from jax import lax
from jax.experimental import pallas as pl
from jax.experimental.pallas import tpu as pltpu

F32 = jnp.float32
BF16 = jnp.bfloat16

NORM_EPS = 1e-6
SSD_HEAD_DIM = 64
SSD_N_GROUPS = 8
SSD_D_STATE = 128
SSD_CONV_WIDTH = 4
SSD_CHUNK = 128
ATTN_HEAD_DIM = 128
DILATION_PATTERNS = ((128, 1), (512, 4), (2048, 16))
ATTN_HEADS_PER_GROUP = 4
ATTN_BLOCK = 128
N_EXPERT_GROUPS = 4
EXPERTS_PER_GROUP = 8
N_EXPERTS = N_EXPERT_GROUPS * EXPERTS_PER_GROUP
EXPERT_TOP_K = 2

LANES = 128
SUBLANES = 8
VMEM_LIMIT_BYTES = 56 * 1024 * 1024
ROW_DMA_UNROLL = 8

ROUTER_EXPERT_ROW0 = SUBLANES
ROUTER_ROWS = ROUTER_EXPERT_ROW0 + N_EXPERTS


def _params(*semantics):
    return pltpu.CompilerParams(dimension_semantics=semantics, vmem_limit_bytes=VMEM_LIMIT_BYTES)


def _split_bf16(v):
    hi = v.astype(BF16)
    lo = (v - hi.astype(F32)).astype(BF16)
    return hi, lo


def _silu(v):
    return v * (1.0 / (1.0 + jnp.exp(-v)))


def _regroup_weight_kernel(w_ref, out_ref, *, segments):
    off = 0
    for start, width in segments:
        out_ref[:, off:off + width] = w_ref[:, start:start + width].astype(out_ref.dtype)
        off += width


def _regroup_weight(w, segments, *, rows=256):
    k, cols = w.shape
    out_cols = sum(width for _, width in segments)
    assert k % rows == 0 and out_cols % LANES == 0
    return pl.pallas_call(
        functools.partial(_regroup_weight_kernel, segments=segments),
        grid=(k // rows,),
        in_specs=[pl.BlockSpec((rows, cols), lambda i: (i, 0))],
        out_specs=pl.BlockSpec((rows, out_cols), lambda i: (i, 0)),
        out_shape=jax.ShapeDtypeStruct((k, out_cols), BF16),
        compiler_params=_params("arbitrary"),
        name="regroup_weight",
    )(w)


QKV_PARTS = 3


def _in_proj_kernel(x_ref, nw_ref, w_ref, wdt_ref, wdtt_ref, proj_ref, a0_ref, a1_ref, a2_ref,
                    dt_ref, dtt_ref, h_ref, stage_ref, *, n_plain):
    j = pl.program_id(1)
    tm, tn = proj_ref.shape

    @pl.when(j == 0)
    def _():
        xf = x_ref[...]
        ms = jnp.mean(xf * xf, axis=-1, keepdims=True)
        h = (xf * lax.rsqrt(ms + NORM_EPS) * nw_ref[...]).astype(BF16)
        h_ref[...] = h
        dt_ref[...] = jnp.dot(h, wdt_ref[...], preferred_element_type=F32)
        dtt_ref[...] = lax.dot_general(wdtt_ref[...], h, (((1,), (1,)), ((), ())),
                                       preferred_element_type=F32)

    @pl.when(j < n_plain)
    def _():
        proj_ref[...] = jnp.dot(h_ref[...], w_ref[...], preferred_element_type=F32).astype(BF16)

    for gi, a_ref in enumerate((a0_ref, a1_ref, a2_ref)):
        dil = DILATION_PATTERNS[gi][1]
        j0 = n_plain + QKV_PARTS * gi

        @pl.when((j >= j0) & (j < j0 + QKV_PARTS))
        def _(a_ref=a_ref, dil=dil):
            res = jnp.dot(h_ref[...], w_ref[...], preferred_element_type=F32)
            if dil == 1:
                a_ref[...] = res.astype(BF16)
            else:
                for s in range(tn // LANES):
                    stage_ref[s] = res[:, s * LANES:(s + 1) * LANES]
                for r in range(dil):
                    for s in range(tn // LANES):
                        c0 = r * tn + s * LANES
                        a_ref[:, c0:c0 + LANES] = stage_ref[
                            s, pl.ds(r, tm // dil, stride=dil), :].astype(BF16)


def _in_proj(x2, norm_w, w_main, w_dt, *, tm, tn, n_plain):
    n, d = x2.shape
    nh = w_dt.shape[1]
    n_blocks = w_main.shape[1] // tn
    assert n_blocks == n_plain + QKV_PARTS * len(DILATION_PATTERNS)

    def a_spec(gi):
        dil = DILATION_PATTERNS[gi][1]
        j0 = n_plain + QKV_PARTS * gi
        return pl.BlockSpec((tm // dil, dil * tn),
                            lambda i, j: (i, jnp.clip(j - j0, 0, QKV_PARTS - 1)))

    def a_shape(gi):
        dil = DILATION_PATTERNS[gi][1]
        return jax.ShapeDtypeStruct((n // dil, dil * QKV_PARTS * tn), BF16)

    return pl.pallas_call(
        functools.partial(_in_proj_kernel, n_plain=n_plain),
        grid=(n // tm, n_blocks),
        in_specs=[
            pl.BlockSpec((tm, d), lambda i, j: (i, 0)),
            pl.BlockSpec((1, d), lambda i, j: (0, 0)),
            pl.BlockSpec((d, tn), lambda i, j: (0, j)),
            pl.BlockSpec((d, nh), lambda i, j: (0, 0)),
            pl.BlockSpec((nh, d), lambda i, j: (0, 0)),
        ],
        out_specs=[
            pl.BlockSpec((tm, tn), lambda i, j: (i, jnp.minimum(j, n_plain - 1))),
            a_spec(0), a_spec(1), a_spec(2),
            pl.BlockSpec((tm, nh), lambda i, j: (i, 0)),
            pl.BlockSpec((nh, tm), lambda i, j: (0, i)),
        ],
        out_shape=[
            jax.ShapeDtypeStruct((n, n_plain * tn), BF16),
            a_shape(0), a_shape(1), a_shape(2),
            jax.ShapeDtypeStruct((n, nh), F32),
            jax.ShapeDtypeStruct((nh, n), F32),
        ],
        scratch_shapes=[pltpu.VMEM((tm, d), BF16), pltpu.VMEM((tn // LANES, tm, LANES), F32)],
        compiler_params=_params("arbitrary", "arbitrary"),
        name="in_proj",
    )(x2, norm_w.reshape(1, d), w_main, w_dt.astype(BF16), w_dt.T.astype(BF16))


def _ssd_kernel(xbc_ref, z_ref, dt_ref, dtt_ref, cw_ref, cb_ref, dtb_ref, dtbt_ref, alog_ref,
                alogt_ref, dskip_ref, nw_ref, expand_ref, out_ref,
                xbuf_ref, state_ref, y_ref, *, n_heads, d_inner):
    L = SSD_CHUNK
    P = SSD_HEAD_DIM
    NS = SSD_D_STATE
    G = SSD_N_GROUPS
    R = n_heads // G
    GW = R * P
    W = SSD_CONV_WIDTH
    c = pl.program_id(1)

    @pl.when(c == 0)
    def _():
        xbuf_ref[0:SUBLANES, :] = jnp.zeros((SUBLANES, xbuf_ref.shape[1]), F32)
        state_ref[...] = jnp.zeros(state_ref.shape, F32)

    @pl.when(c > 0)
    def _():
        xbuf_ref[0:SUBLANES, :] = xbuf_ref[L:L + SUBLANES, :]

    xbuf_ref[SUBLANES:SUBLANES + L, :] = xbc_ref[...].astype(F32)

    def conv(col0, width):
        acc = cb_ref[:, col0:col0 + width]
        for w in range(W):
            r0 = SUBLANES - (W - 1) + w
            acc = acc + cw_ref[w:w + 1, col0:col0 + width] * xbuf_ref[r0:r0 + L, col0:col0 + width]
        return _silu(acc)

    def softplus(v):
        return jnp.maximum(v, 0.0) + jnp.log1p(jnp.exp(-jnp.abs(v)))

    dt = softplus(dt_ref[...] + dtb_ref[...])
    dtt = softplus(dtt_ref[...] + dtbt_ref[...])
    da = dt * (-jnp.exp(alog_ref[...]))
    dat = dtt * (-jnp.exp(alogt_ref[...]))
    row = lax.broadcasted_iota(jnp.int32, (L, L), 0)
    col = lax.broadcasted_iota(jnp.int32, (L, L), 1)
    causal = row >= col
    tri = jnp.where(causal, 1.0, 0.0).astype(BF16)
    trit = jnp.where(row <= col, 1.0, 0.0).astype(BF16)

    def split3(v):
        a = v.astype(BF16)
        r1 = v - a.astype(F32)
        b = r1.astype(BF16)
        cc = (r1 - b.astype(F32)).astype(BF16)
        return a, b, cc

    a_cum = sum(jnp.dot(tri, p, preferred_element_type=F32) for p in split3(da))
    a_cumt = sum(jnp.dot(p, trit, preferred_element_type=F32) for p in split3(dat))
    a_last = a_cum[L - 1:L, :]

    expand = expand_ref[...]

    def expand_heads(v):
        hi, lo = _split_bf16(v)
        return (jnp.dot(hi, expand, preferred_element_type=F32)
                + jnp.dot(lo, expand, preferred_element_type=F32))

    dt_e = expand_heads(dt)
    in_scale_e = expand_heads(dt * jnp.exp(a_last - a_cum))
    out_scale_e = expand_heads(jnp.exp(a_cum))
    tail8 = jnp.concatenate([jnp.exp(a_last), dskip_ref[...],
                             jnp.zeros((SUBLANES - 2, n_heads), F32)], axis=0)
    tail8_e = expand_heads(tail8)
    chunk_decay_e = tail8_e[0:1, :]
    dskip_e = tail8_e[1:2, :]

    lane_head = lax.broadcasted_iota(jnp.int32, (L, GW), 1) // P

    for g in range(G):
        x0 = g * GW
        xs = conv(x0, GW)
        bm = conv(d_inner + g * NS, NS).astype(BF16)
        cm = conv(d_inner + G * NS + g * NS, NS).astype(BF16)
        cb = lax.dot_general(cm, bm, (((1,), (1,)), ((), ())), preferred_element_type=F32)
        xdt = (xs * dt_e[:, x0:x0 + GW]).astype(BF16)
        y = xs * dskip_e[:, x0:x0 + GW]
        for r in range(R):
            h = g * R + r
            seg = a_cum[:, h:h + 1] - a_cumt[h:h + 1, :]
            decay = jnp.exp(jnp.where(causal, seg, -jnp.inf))
            wgt = (cb * decay).astype(BF16)
            xh = jnp.where(lane_head == r, xdt, jnp.zeros_like(xdt))
            y = y + jnp.dot(wgt, xh, preferred_element_type=F32)
        st = state_ref[:, x0:x0 + GW]
        y = y + (jnp.dot(cm, st.astype(BF16), preferred_element_type=F32)
                 * out_scale_e[:, x0:x0 + GW])
        y_ref[:, x0:x0 + GW] = y
        xin = (xs * in_scale_e[:, x0:x0 + GW]).astype(BF16)
        st_new = lax.dot_general(bm, xin, (((0,), (0,)), ((), ())), preferred_element_type=F32)
        state_ref[:, x0:x0 + GW] = st * chunk_decay_e[:, x0:x0 + GW] + st_new

    yz = y_ref[...] * _silu(z_ref[...].astype(F32))
    ms = jnp.mean(yz * yz, axis=-1, keepdims=True)
    out_ref[...] = (yz * lax.rsqrt(ms + NORM_EPS) * nw_ref[...]).astype(out_ref.dtype)


def _ssd(proj, dt, dtt, conv_w, conv_b, dt_bias, a_log, d_skip, norm_w, *, batch, seq, d_inner,
         xbc_block, z_block):
    n = batch * seq
    n_heads = dt.shape[1]
    L = SSD_CHUNK
    nc = seq // L
    conv_dim = conv_w.shape[1]
    expand = (np.arange(d_inner)[None, :] // SSD_HEAD_DIM == np.arange(n_heads)[:, None])
    expand = jnp.asarray(expand, BF16)
    kern = functools.partial(_ssd_kernel, n_heads=n_heads, d_inner=d_inner)
    small = lambda shape: pl.BlockSpec(shape, lambda b, c: (0, 0))
    return pl.pallas_call(
        kern,
        grid=(batch, nc),
        in_specs=[
            pl.BlockSpec((L, conv_dim), lambda b, c: (b * nc + c, xbc_block)),
            pl.BlockSpec((L, d_inner), lambda b, c: (b * nc + c, z_block)),
            pl.BlockSpec((L, n_heads), lambda b, c: (b * nc + c, 0)),
            pl.BlockSpec((n_heads, L), lambda b, c: (0, b * nc + c)),
            small((SSD_CONV_WIDTH, conv_dim)),
            small((1, conv_dim)),
            small((1, n_heads)),
            small((n_heads, 1)),
            small((1, n_heads)),
            small((n_heads, 1)),
            small((1, n_heads)),
            small((1, d_inner)),
            small((n_heads, d_inner)),
        ],
        out_specs=pl.BlockSpec((L, d_inner), lambda b, c: (b * nc + c, 0)),
        out_shape=jax.ShapeDtypeStruct((n, d_inner), BF16),
        scratch_shapes=[
            pltpu.VMEM((L + SUBLANES, conv_dim), F32),
            pltpu.VMEM((SSD_D_STATE, d_inner), F32),
            pltpu.VMEM((L, d_inner), F32),
        ],
        compiler_params=_params("arbitrary", "arbitrary"),
        name="ssd",
    )(proj, proj, dt, dtt, conv_w, conv_b.reshape(1, -1), dt_bias.reshape(1, -1),
      dt_bias.reshape(-1, 1), a_log.reshape(1, -1), a_log.reshape(-1, 1), d_skip.reshape(1, -1),
      norm_w.reshape(1, -1), expand)


def _attn_kernel(q_ref, k_ref, v_ref, o_ref, lse_ref, kp_ref, vp_ref, *, slopes, dilation, hops):
    nb = pl.program_id(2)
    BLK = ATTN_BLOCK
    E = ATTN_HEAD_DIM

    @pl.when(nb == 0)
    def _():
        kp_ref[...] = jnp.zeros(kp_ref.shape, kp_ref.dtype)
        vp_ref[...] = jnp.zeros(vp_ref.shape, vp_ref.dtype)

    qi = lax.broadcasted_iota(jnp.int32, (BLK, BLK), 0)
    ki = lax.broadcasted_iota(jnp.int32, (BLK, BLK), 1)
    rel_cur = qi - ki
    rel_prev = rel_cur + BLK
    ok_cur = rel_cur >= 0
    ok_prev = rel_prev <= jnp.where(nb > 0, hops, -1)
    dist_cur = (rel_cur * dilation).astype(F32)
    dist_prev = (rel_prev * dilation).astype(F32)
    lane = lax.broadcasted_iota(jnp.int32, (BLK, LANES), 1)
    scale = E ** -0.5
    nt = (((1,), (1,)), ((), ()))

    q = q_ref[...]
    k = k_ref[...]
    v = v_ref[...]
    kp = kp_ref[...]
    vp = vp_ref[...]
    lse_tile = jnp.zeros((BLK, LANES), F32)
    for h, slope in enumerate(slopes):
        sl = slice(h * E, (h + 1) * E)
        qh = q[:, sl]
        s_cur = lax.dot_general(qh, k[:, sl], nt, preferred_element_type=F32)
        s_prev = lax.dot_general(qh, kp[:, sl], nt, preferred_element_type=F32)
        l_cur = jnp.where(ok_cur, s_cur * scale - slope * dist_cur, -jnp.inf)
        l_prev = jnp.where(ok_prev, s_prev * scale - slope * dist_prev, -jnp.inf)
        m = jnp.max(jnp.maximum(l_cur, l_prev), axis=-1, keepdims=True)
        p_cur = jnp.exp(l_cur - m)
        p_prev = jnp.exp(l_prev - m)
        den = jnp.sum(p_cur + p_prev, axis=-1, keepdims=True)
        acc = (jnp.dot(p_cur.astype(BF16), v[:, sl], preferred_element_type=F32)
               + jnp.dot(p_prev.astype(BF16), vp[:, sl], preferred_element_type=F32))
        o_ref[:, sl] = (acc / den).astype(o_ref.dtype)
        lse_tile = jnp.where(lane == h, m + jnp.log(den), lse_tile)
    lse_ref[...] = lse_tile
    kp_ref[...] = k
    vp_ref[...] = v


def _attn_group(qkv, gi, *, batch, seq):
    window, dilation = DILATION_PATTERNS[gi]
    hops = window // dilation
    n_heads_total = ATTN_HEADS_PER_GROUP * len(DILATION_PATTERNS)
    slopes = tuple(float(2.0 ** (-8.0 * (gi * ATTN_HEADS_PER_GROUP + h + 1) / n_heads_total))
                   for h in range(ATTN_HEADS_PER_GROUP))
    gw = ATTN_HEADS_PER_GROUP * ATTN_HEAD_DIM
    assert seq % (dilation * ATTN_BLOCK) == 0
    sub = seq // dilation
    nb = sub // ATTN_BLOCK
    kern = functools.partial(_attn_kernel, slopes=slopes, dilation=dilation, hops=hops)

    def part(p):
        return pl.BlockSpec((ATTN_BLOCK, gw), lambda b, r, n: (b * nb + n, p * dilation + r))

    o, lse = pl.pallas_call(
        kern,
        grid=(batch, dilation, nb),
        in_specs=[part(0), part(1), part(2)],
        out_specs=[
            pl.BlockSpec((ATTN_BLOCK, gw), lambda b, r, n: (b * nb + n, r)),
            pl.BlockSpec((ATTN_BLOCK, LANES), lambda b, r, n: (b * nb + n, r)),
        ],
        out_shape=[
            jax.ShapeDtypeStruct((batch * sub, dilation * gw), BF16),
            jax.ShapeDtypeStruct((batch * sub, dilation * LANES), F32),
        ],
        scratch_shapes=[pltpu.VMEM((ATTN_BLOCK, gw), BF16), pltpu.VMEM((ATTN_BLOCK, gw), BF16)],
        compiler_params=_params("arbitrary", "arbitrary", "arbitrary"),
        name=f"attn_g{gi}",
    )(qkv, qkv, qkv)
    return o.reshape(batch * seq, gw), lse.reshape(batch * seq, LANES)


def _mix_kernel(yn_ref, o0_ref, o1_ref, o2_ref, l0_ref, l1_ref, l2_ref, gate_ref, x_ref,
                wssd_ref, wattn_ref, wout_ref, bg_ref, fnw_ref, wrh_ref, wrl_ref, rb_ref,
                x1_ref, hn_ref, eid_ref, gcol_ref, rank_ref, cnt_ref, carry_ref, *, d_model):
    i = pl.program_id(0)
    tm = x_ref.shape[0]
    E = ATTN_HEAD_DIM

    @pl.when(i == 0)
    def _():
        carry_ref[...] = jnp.zeros(carry_ref.shape, F32)

    y_ssd = jnp.dot(yn_ref[...], wssd_ref[...], preferred_element_type=F32)

    l0, l1, l2 = l0_ref[...], l1_ref[...], l2_ref[...]
    lm = jnp.maximum(jnp.maximum(l0, l1), l2)
    e0, e1, e2 = jnp.exp(l0 - lm), jnp.exp(l1 - lm), jnp.exp(l2 - lm)
    inv = 1.0 / (e0 + e1 + e2)
    parts = []
    for h in range(ATTN_HEADS_PER_GROUP):
        sl = slice(h * E, (h + 1) * E)
        parts.append((e0[:, h:h + 1] * inv[:, h:h + 1]) * o0_ref[:, sl].astype(F32)
                     + (e1[:, h:h + 1] * inv[:, h:h + 1]) * o1_ref[:, sl].astype(F32)
                     + (e2[:, h:h + 1] * inv[:, h:h + 1]) * o2_ref[:, sl].astype(F32))
    o = jnp.concatenate(parts, axis=-1).astype(BF16)
    y_attn = jnp.dot(o, wattn_ref[...], preferred_element_type=F32)

    graw = gate_ref[...].astype(F32) + bg_ref[...]
    gates = 1.0 / (1.0 + jnp.exp(-graw))
    merged = (gates[:, :d_model] * y_ssd + gates[:, d_model:] * y_attn).astype(BF16)
    x1 = x_ref[...] + jnp.dot(merged, wout_ref[...], preferred_element_type=F32)
    x1_ref[...] = x1

    ms = jnp.mean(x1 * x1, axis=-1, keepdims=True)
    hn = x1 * lax.rsqrt(ms + NORM_EPS) * fnw_ref[...]
    hn_ref[...] = hn

    hn_hi, hn_lo = _split_bf16(hn)
    nt = (((1,), (1,)), ((), ()))
    logits = (lax.dot_general(wrh_ref[...], hn_hi, nt, preferred_element_type=F32)
              + lax.dot_general(wrh_ref[...], hn_lo, nt, preferred_element_type=F32)
              + lax.dot_general(wrl_ref[...], hn_hi, nt, preferred_element_type=F32)
              + rb_ref[...])

    grow = lax.broadcasted_iota(jnp.int32, (SUBLANES, tm), 0)
    gl = jnp.where(grow < N_EXPERT_GROUPS, logits[0:SUBLANES, :], -jnp.inf)
    gmax = jnp.max(gl, axis=0, keepdims=True)
    gidx = jnp.min(jnp.where(gl == gmax, grow, N_EXPERT_GROUPS), axis=0, keepdims=True)
    group_gate = 1.0 / jnp.sum(jnp.exp(gl - gmax), axis=0, keepdims=True)

    in_group = jnp.zeros((EXPERTS_PER_GROUP, tm), F32)
    for g in range(N_EXPERT_GROUPS):
        r0 = ROUTER_EXPERT_ROW0 + g * EXPERTS_PER_GROUP
        in_group = jnp.where(gidx == g, logits[r0:r0 + EXPERTS_PER_GROUP, :], in_group)
    erow = lax.broadcasted_iota(jnp.int32, in_group.shape, 0)
    v1 = jnp.max(in_group, axis=0, keepdims=True)
    i1 = jnp.min(jnp.where(in_group == v1, erow, EXPERTS_PER_GROUP), axis=0, keepdims=True)
    rest = jnp.where(erow == i1, -jnp.inf, in_group)
    v2 = jnp.max(rest, axis=0, keepdims=True)
    i2 = jnp.min(jnp.where(rest == v2, erow, EXPERTS_PER_GROUP), axis=0, keepdims=True)
    t = jnp.exp(v2 - v1)
    g1 = group_gate / (1.0 + t)
    g2 = group_gate * t / (1.0 + t)
    eid1 = gidx * EXPERTS_PER_GROUP + i1
    eid2 = gidx * EXPERTS_PER_GROUP + i2
    slot = lax.broadcasted_iota(jnp.int32, (SUBLANES, tm), 0)
    eid_ref[...] = jnp.where(slot == 0, eid1, jnp.where(slot == 1, eid2, 0))

    grow8 = lax.broadcasted_iota(jnp.int32, (LANES, tm), 0)
    gt = jnp.where(grow8 == 0, g1, jnp.where(grow8 == 1, g2, 0.0))
    gcol_ref[...] = gt.T

    xrow = lax.broadcasted_iota(jnp.int32, (N_EXPERTS, tm), 0)
    oh1 = xrow == eid1
    oh2 = xrow == eid2
    oh = jnp.where(oh1 | oh2, 1.0, 0.0)
    ti = lax.broadcasted_iota(jnp.int32, (tm, tm), 0)
    tj = lax.broadcasted_iota(jnp.int32, (tm, tm), 1)
    before = jnp.where(ti < tj, 1.0, 0.0).astype(BF16)
    prior = jnp.dot(oh.astype(BF16), before, preferred_element_type=F32) + carry_ref[:, 0:1]
    r1 = jnp.sum(jnp.where(oh1, prior, 0.0), axis=0, keepdims=True)
    r2 = jnp.sum(jnp.where(oh2, prior, 0.0), axis=0, keepdims=True)
    rank_ref[...] = jnp.where(slot == 0, r1, jnp.where(slot == 1, r2, 0.0)).astype(jnp.int32)
    carry_ref[...] = carry_ref[...] + jnp.sum(oh, axis=1, keepdims=True)
    cnt_ref[...] = carry_ref[...].astype(jnp.int32)


def _mix(yn, outs, lses, proj, x2, w_ssd_out, w_attn_out, w_out, b_gate, ffn_norm_w, wr, rb, *,
         tm, gate_block):
    n, d = x2.shape
    aw = outs[0].shape[1]
    wrt = jnp.zeros((ROUTER_ROWS, d), F32)
    wrt = wrt.at[0:N_EXPERT_GROUPS].set(wr[0].T).at[ROUTER_EXPERT_ROW0:].set(wr[1].T)
    rbc = jnp.zeros((ROUTER_ROWS, 1), F32)
    rbc = rbc.at[0:N_EXPERT_GROUPS, 0].set(rb[0]).at[ROUTER_EXPERT_ROW0:, 0].set(rb[1])
    wrh, wrl = _split_bf16(wrt)
    row = lambda w: pl.BlockSpec((tm, w), lambda i: (i, 0))
    const = lambda shape: pl.BlockSpec(shape, lambda i: (0, 0), pipeline_mode=pl.Buffered(1))
    kern = functools.partial(_mix_kernel, d_model=d)
    return pl.pallas_call(
        kern,
        grid=(n // tm,),
        in_specs=[
            row(d), row(aw), row(aw), row(aw), row(LANES), row(LANES), row(LANES),
            pl.BlockSpec((tm, 2 * d), lambda i: (i, gate_block)),
            row(d),
            const((d, d)), const((aw, d)), const((d, d)), const((1, 2 * d)), const((1, d)),
            const((ROUTER_ROWS, d)), const((ROUTER_ROWS, d)), const((ROUTER_ROWS, 1)),
        ],
        out_specs=[
            row(d), row(d),
            pl.BlockSpec((SUBLANES, tm), lambda i: (0, i)),
            row(LANES),
            pl.BlockSpec((SUBLANES, tm), lambda i: (0, i)),
            pl.BlockSpec((N_EXPERTS, LANES), lambda i: (0, 0)),
        ],
        out_shape=[
            jax.ShapeDtypeStruct((n, d), F32),
            jax.ShapeDtypeStruct((n, d), F32),
            jax.ShapeDtypeStruct((SUBLANES, n), jnp.int32),
            jax.ShapeDtypeStruct((n, LANES), F32),
            jax.ShapeDtypeStruct((SUBLANES, n), jnp.int32),
            jax.ShapeDtypeStruct((N_EXPERTS, LANES), jnp.int32),
        ],
        scratch_shapes=[pltpu.VMEM((N_EXPERTS, LANES), F32)],
        compiler_params=_params("arbitrary"),
        name="mix",
    )(yn, outs[0], outs[1], outs[2], lses[0], lses[1], lses[2], proj, x2,
      w_ssd_out.astype(BF16), w_attn_out.astype(BF16), w_out.astype(BF16),
      b_gate.reshape(1, -1), ffn_norm_w.reshape(1, -1), wrh, wrl, rbc)


def _scatter_kernel(pad_start_ref, pad_len_ref, n_active_ref, pos_ref, hn_ref, xs_ref, zero_ref,
                    sem, zsem, *, tme, n_tiles):
    tm = hn_ref.shape[0]
    zrows = zero_ref.shape[0]

    @pl.when(pl.program_id(0) == 0)
    def _():
        zero_ref[...] = jnp.zeros(zero_ref.shape, zero_ref.dtype)

        def fills(act):
            def per_expert(e, carry):
                off = pad_start_ref[e]
                left = pad_len_ref[e]
                head = left & (SUBLANES - 1)
                for r in range(SUBLANES - 1):
                    @pl.when(r < head)
                    def _(r=r):
                        act(pltpu.make_async_copy(zero_ref.at[pl.ds(0, 1), :],
                                                  xs_ref.at[pl.ds(off + r, 1), :], zsem))

                off = off + head
                bit = zrows
                while bit >= SUBLANES:
                    take = left & bit

                    @pl.when(take != 0)
                    def _(off=off, bit=bit):
                        act(pltpu.make_async_copy(
                            zero_ref.at[pl.ds(0, bit), :],
                            xs_ref.at[pl.ds(pl.multiple_of(off, SUBLANES), bit), :], zsem))

                    off = off + take
                    bit //= 2
                return carry

            def per_tile(t, carry):
                @pl.when(t >= n_active_ref[0])
                def _():
                    for part in range(tme // zrows):
                        row0 = pl.multiple_of(t * tme + part * zrows, zrows)
                        act(pltpu.make_async_copy(zero_ref, xs_ref.at[pl.ds(row0, zrows), :], zsem))

                return carry

            lax.fori_loop(0, N_EXPERTS, per_expert, 0)
            lax.fori_loop(0, n_tiles, per_tile, 0)

        fills(lambda cp: cp.start())
        fills(lambda cp: cp.wait())

    def start(tb, carry):
        for u in range(ROW_DMA_UNROLL):
            t = tb * ROW_DMA_UNROLL + u
            for k in range(EXPERT_TOP_K):
                dst = pos_ref[0, 0, k * tm + t]
                pltpu.make_async_copy(hn_ref.at[pl.ds(t, 1), :], xs_ref.at[pl.ds(dst, 1), :],
                                      sem).start(priority=(u + k) % 2)
        return carry

    lax.fori_loop(0, tm // ROW_DMA_UNROLL, start, 0)
    for k in range(EXPERT_TOP_K):
        pltpu.make_async_copy(hn_ref, xs_ref.at[pl.ds(0, tm), :], sem).wait()


def _scatter(hn, pos_tiles, pad_start, pad_len, n_active, *, tm, tme, n_tiles):
    n, d = hn.shape
    assert tme % 2 == 0 and (tme // 2) & (tme // 2 - 1) == 0
    grid_spec = pltpu.PrefetchScalarGridSpec(
        num_scalar_prefetch=3,
        grid=(n // tm,),
        in_specs=[
            pl.BlockSpec((1, 1, EXPERT_TOP_K * tm), lambda i, *_: (i, 0, 0),
                         memory_space=pltpu.SMEM),
            pl.BlockSpec((tm, d), lambda i, *_: (i, 0)),
        ],
        out_specs=pl.BlockSpec(memory_space=pl.ANY),
        scratch_shapes=[pltpu.VMEM((tme // 2, d), hn.dtype), pltpu.SemaphoreType.DMA(()),
                        pltpu.SemaphoreType.DMA(())],
    )
    return pl.pallas_call(
        functools.partial(_scatter_kernel, tme=tme, n_tiles=n_tiles),
        grid_spec=grid_spec,
        out_shape=jax.ShapeDtypeStruct((n_tiles * tme, d), hn.dtype),
        compiler_params=_params("arbitrary"),
        name="scatter",
    )(pad_start, pad_len, n_active, pos_tiles, hn)


W_CHUNK_ROWS = 256
W_RING = 4


def _experts_kernel(te_ref, slot_ref, nxt_ref, lo_ref, hi_ref, na_ref, xs_ref, wg_hbm, wu_hbm,
                    wd_hbm, ys_ref, wres_ref, stage_ref, sem, *, n_gu, n_r, n_h):
    i = pl.program_id(0)
    n_chunks = 2 * n_gu + n_h * n_r
    ring, cr, f = stage_ref.shape
    d = n_gu * cr

    def stage_copy(src, c):
        return pltpu.make_async_copy(src, stage_ref.at[c % ring], sem.at[c % ring])

    def start_chunk(e, c):
        @pl.when(c < n_gu)
        def _():
            stage_copy(wg_hbm.at[e, pl.ds(pl.multiple_of(c * cr, cr), cr), :], c).start()

        @pl.when((c >= n_gu) & (c < 2 * n_gu))
        def _():
            stage_copy(wu_hbm.at[e, pl.ds(pl.multiple_of((c - n_gu) * cr, cr), cr), :], c).start()

        @pl.when(c >= 2 * n_gu)
        def _():
            q = c - 2 * n_gu
            h = q // n_r
            r = q - h * n_r
            stage_copy(wd_hbm.at[e, pl.ds(pl.multiple_of(r * cr, cr), cr),
                                 pl.ds(pl.multiple_of(h * f, f), f)], c).start()

    def start_first(e):
        def body(c, carry):
            start_chunk(e, c)
            return carry

        lax.fori_loop(0, ring, body, 0)

    def convert(e, lo, hi, dst_slot):
        def body(c, carry):
            stage_copy(wg_hbm.at[0, pl.ds(0, cr), :], c).wait()
            wres_ref[dst_slot, pl.ds(pl.multiple_of(c * cr, cr), cr), :] = (
                stage_ref[c % ring].astype(BF16))

            @pl.when(c + ring < n_chunks)
            def _():
                start_chunk(e, c + ring)

            return carry

        lax.fori_loop(lo, hi, body, 0)

    @pl.when(i == 0)
    def _():
        start_first(te_ref[0])
        convert(te_ref[0], 0, n_chunks, slot_ref[0])

    @pl.when(i < na_ref[0])
    def _():
        slot = slot_ref[i]
        nxt = nxt_ref[i]

        @pl.when((nxt >= 0) & (lo_ref[i] == 0))
        def _():
            start_first(nxt)

        xb = xs_ref[...].astype(BF16)
        hg = jnp.dot(xb, wres_ref[slot, 0:d, :], preferred_element_type=F32)
        hu = jnp.dot(xb, wres_ref[slot, d:2 * d, :], preferred_element_type=F32)
        hmid = (_silu(hg) * hu).astype(BF16)
        for h in range(n_h):
            r0 = 2 * d + h * n_r * cr
            ys_ref[:, h * f:(h + 1) * f] = jnp.dot(hmid, wres_ref[slot, r0:r0 + n_r * cr, :],
                                                    preferred_element_type=F32)

        @pl.when(nxt >= 0)
        def _():
            convert(nxt, lo_ref[i], hi_ref[i], 1 - slot)

    @pl.when(i >= na_ref[0])
    def _():
        ys_ref[...] = jnp.zeros(ys_ref.shape, F32)


def _experts(xs, tile_expert, n_active, cnt, starts, padded, w_g, w_u, w_d, *, tm):
    p, d = xs.shape
    ne, _, f = w_g.shape
    n_tiles = p // tm
    cr = W_CHUNK_ROWS
    assert d % cr == 0 and f % cr == 0 and d % f == 0
    n_gu, n_r, n_h = d // cr, f // cr, d // f
    n_chunks = 2 * n_gu + n_h * n_r

    ids = jnp.arange(ne, dtype=jnp.int32)
    present = cnt > 0
    later = jnp.where(present[None, :] & (ids[None, :] > ids[:, None]), ids[None, :], ne)
    nxt_e = jnp.min(later, axis=1)
    nxt_e = jnp.where(nxt_e >= ne, -1, nxt_e)
    run_e = jnp.cumsum(present.astype(jnp.int32)) - 1
    onehot = (tile_expert[:, None] == ids[None, :]).astype(jnp.int32)
    pick = lambda v: jnp.sum(onehot * v[None, :].astype(jnp.int32), axis=1)
    tiles = jnp.arange(n_tiles, dtype=jnp.int32)
    active = tiles < n_active[0]
    j = tiles - pick(starts) // tm
    k = jnp.maximum(pick(padded) // tm, 1)
    nxt_t = jnp.where(active, pick(nxt_e), -1).astype(jnp.int32)
    lo_t = jnp.where(active, (n_chunks * j) // k, 0).astype(jnp.int32)
    hi_t = jnp.where(active, (n_chunks * (j + 1)) // k, 0).astype(jnp.int32)
    slot_t = (pick(run_e) % 2).astype(jnp.int32)

    def rows(i, te, sl, nx, lo, hi, na):
        return (jnp.minimum(i, na[0] - 1), 0)

    grid_spec = pltpu.PrefetchScalarGridSpec(
        num_scalar_prefetch=6,
        grid=(n_tiles,),
        in_specs=[
            pl.BlockSpec((tm, d), rows),
            pl.BlockSpec(memory_space=pl.ANY),
            pl.BlockSpec(memory_space=pl.ANY),
            pl.BlockSpec(memory_space=pl.ANY),
        ],
        out_specs=pl.BlockSpec((tm, d), lambda i, *_: (i, 0)),
        scratch_shapes=[
            pltpu.VMEM((2, n_chunks * cr, f), BF16),
            pltpu.VMEM((W_RING, cr, f), F32),
            pltpu.SemaphoreType.DMA((W_RING,)),
        ],
    )
    return pl.pallas_call(
        functools.partial(_experts_kernel, n_gu=n_gu, n_r=n_r, n_h=n_h),
        grid_spec=grid_spec,
        out_shape=jax.ShapeDtypeStruct((p, d), F32),
        compiler_params=_params("arbitrary"),
        name="experts",
    )(tile_expert, slot_t, nxt_t, lo_t, hi_t, n_active, xs, w_g, w_u, w_d)


def _combine_kernel(pos_ref, pos_next_ref, ys_ref, x1_ref, gcol_ref, nw_ref, out_ref, buf_ref, sem,
                    *, final):
    i = pl.program_id(0)
    n_steps = pl.num_programs(0)
    tm = x1_ref.shape[0]

    def issue(p_ref, slot):
        def start(tb, carry):
            for u in range(ROW_DMA_UNROLL):
                t = tb * ROW_DMA_UNROLL + u
                for k in range(EXPERT_TOP_K):
                    src = p_ref[0, 0, k * tm + t]
                    pltpu.make_async_copy(ys_ref.at[pl.ds(src, 1), :],
                                          buf_ref.at[slot, k, pl.ds(t, 1), :],
                                          sem.at[slot]).start(priority=(u + k) % 2)
            return carry

        lax.fori_loop(0, tm // ROW_DMA_UNROLL, start, 0)

    slot = i % 2

    @pl.when(i == 0)
    def _():
        issue(pos_ref, 0)

    @pl.when(i + 1 < n_steps)
    def _():
        issue(pos_next_ref, 1 - slot)

    for k in range(EXPERT_TOP_K):
        pltpu.make_async_copy(ys_ref.at[pl.ds(0, tm), :], buf_ref.at[slot, k], sem.at[slot]).wait()
    g = gcol_ref[...]
    xo = x1_ref[...] + g[:, 0:1] * buf_ref[slot, 0] + g[:, 1:2] * buf_ref[slot, 1]
    if final:
        ms = jnp.mean(xo * xo, axis=-1, keepdims=True)
        xo = xo * lax.rsqrt(ms + NORM_EPS) * nw_ref[...]
    out_ref[...] = xo


def _combine(ys, pos_tiles, x1, gcol, norm_w, *, tm, final):
    n, d = x1.shape
    last = n // tm - 1
    return pl.pallas_call(
        functools.partial(_combine_kernel, final=final),
        grid=(n // tm,),
        in_specs=[
            pl.BlockSpec((1, 1, EXPERT_TOP_K * tm), lambda i: (i, 0, 0), memory_space=pltpu.SMEM),
            pl.BlockSpec((1, 1, EXPERT_TOP_K * tm), lambda i: (jnp.minimum(i + 1, last), 0, 0),
                         memory_space=pltpu.SMEM),
            pl.BlockSpec(memory_space=pl.ANY),
            pl.BlockSpec((tm, d), lambda i: (i, 0)),
            pl.BlockSpec((tm, LANES), lambda i: (i, 0)),
            pl.BlockSpec((1, d), lambda i: (0, 0)),
        ],
        out_specs=pl.BlockSpec((tm, d), lambda i: (i, 0)),
        out_shape=jax.ShapeDtypeStruct((n, d), F32),
        scratch_shapes=[pltpu.VMEM((2, EXPERT_TOP_K, tm, d), F32), pltpu.SemaphoreType.DMA((2,))],
        compiler_params=_params("arbitrary"),
        name="combine",
    )(pos_tiles, pos_tiles, ys, x1, gcol, norm_w.reshape(1, d))


def _tiles(n):
    return dict(proj_tm=math.gcd(n, 1024), mix_tm=math.gcd(n, 256), moe_tm=256,
                route_tm=math.gcd(n, 256))


def _layer(x2, batch, seq, attn_norm_w, w_in, b_gate, conv_w, conv_b, dt_bias, a_log, d_skip,
           ssd_norm_w, w_ssd_out, w_attn_out, w_out, ffn_norm_w, w_gr, b_gr, w_er, b_er,
           w_g, w_u, w_d):
    n, d = x2.shape
    n_heads = dt_bias.shape[0]
    d_inner = ssd_norm_w.shape[0]
    conv_dim = conv_w.shape[1]
    aw_total = ATTN_HEADS_PER_GROUP * len(DILATION_PATTERNS) * ATTN_HEAD_DIM
    gw = ATTN_HEADS_PER_GROUP * ATTN_HEAD_DIM
    tiles = _tiles(n)

    c_z, c_xbc, c_dt = d_inner, d_inner + conv_dim, d_inner + conv_dim + n_heads
    c_gate = c_dt + QKV_PARTS * aw_total
    segments = [(c_gate, 2 * d), (c_z, conv_dim), (0, d_inner)]
    segments += [(c_dt + p * aw_total + gi * gw, gw)
                 for gi in range(len(DILATION_PATTERNS)) for p in range(QKV_PARTS)]
    w_main = _regroup_weight(w_in, tuple(segments))
    w_dt = w_in[:, c_xbc:c_dt]
    off_gate, off_xbc, off_z = 0, 2 * d, 2 * d + conv_dim
    plain_cols = off_z + d_inner
    assert off_xbc % conv_dim == 0 and off_z % d_inner == 0 and plain_cols % gw == 0

    proj, qkv0, qkv1, qkv2, dt, dtt = _in_proj(x2, attn_norm_w, w_main, w_dt, tm=tiles["proj_tm"],
                                               tn=gw, n_plain=plain_cols // gw)

    yn = _ssd(proj, dt, dtt, conv_w, conv_b, dt_bias, a_log, d_skip, ssd_norm_w, batch=batch,
              seq=seq, d_inner=d_inner, xbc_block=off_xbc // conv_dim, z_block=off_z // d_inner)

    outs, lses = [], []
    for gi, qkv in enumerate((qkv0, qkv1, qkv2)):
        o_g, lse_g = _attn_group(qkv, gi, batch=batch, seq=seq)
        outs.append(o_g)
        lses.append(lse_g)

    x1, hn, eid, gcol, rank, counts = _mix(
        yn, outs, lses, proj, x2, w_ssd_out, w_attn_out, w_out, b_gate, ffn_norm_w,
        (w_gr, w_er), (b_gr, b_er), tm=tiles["mix_tm"], gate_block=off_gate // (2 * d))

    tme = tiles["moe_tm"]
    cnt = counts[:, 0]
    padded = ((cnt + tme - 1) // tme) * tme
    ends = jnp.cumsum(padded)
    starts = ends - padded
    experts = jnp.arange(N_EXPERTS, dtype=jnp.int32)[:, None, None]
    pos = rank[:EXPERT_TOP_K] + jnp.sum(
        jnp.where(eid[None, :EXPERT_TOP_K] == experts, starts[:, None, None], 0), axis=0)
    n_tiles = EXPERT_TOP_K * n // tme + N_EXPERTS
    tile_start = jnp.arange(n_tiles, dtype=jnp.int32) * tme
    tile_expert = jnp.minimum(jnp.sum(ends[None, :] <= tile_start[:, None], axis=1),
                              N_EXPERTS - 1).astype(jnp.int32)
    n_active = (ends[-1:] // tme).astype(jnp.int32)

    rtm = tiles["route_tm"]
    pos_tiles = pos.reshape(EXPERT_TOP_K, n // rtm, rtm).transpose(1, 0, 2).reshape(
        n // rtm, 1, EXPERT_TOP_K * rtm)
    xs = _scatter(hn, pos_tiles, (starts + cnt).astype(jnp.int32), (padded - cnt).astype(jnp.int32),
                  n_active, tm=rtm, tme=tme, n_tiles=n_tiles)
    ys = _experts(xs, tile_expert, n_active, cnt, starts, padded, w_g, w_u, w_d, tm=tme)
    return ys, pos_tiles, x1, gcol, rtm


def kernel(x, attn_norm_w, w_in, b_gate, conv_w, conv_b, dt_bias, a_log, d_skip, ssd_norm_w,
           w_ssd_out, w_attn_out, w_out, ffn_norm_w, w_group_router, b_group_router,
           w_expert_router, b_expert_router, w_exp_gate, w_exp_up, w_exp_down, final_norm_w):
    batch, seq, d = x.shape
    depth = w_in.shape[0]
    x2 = x.reshape(batch * seq, d)
    for layer in range(depth):
        ys, pos_tiles, x1, gcol, rtm = _layer(
            x2, batch, seq, attn_norm_w[layer], w_in[layer], b_gate[layer], conv_w[layer],
            conv_b[layer], dt_bias[layer], a_log[layer], d_skip[layer], ssd_norm_w[layer],
            w_ssd_out[layer], w_attn_out[layer], w_out[layer], ffn_norm_w[layer],
            w_group_router[layer], b_group_router[layer], w_expert_router[layer],
            b_expert_router[layer], w_exp_gate[layer], w_exp_up[layer], w_exp_down[layer])
        x2 = _combine(ys, pos_tiles, x1, gcol, final_norm_w, tm=rtm, final=layer == depth - 1)
    return x2.reshape(batch, seq, d)
```

```python
import functools
import math

import jax
import jax.numpy as jnp
import numpy as np
from jax import lax
from jax.experimental import pallas as pl
from jax.experimental.pallas import tpu as pltpu

F32 = jnp.float32
BF16 = jnp.bfloat16

NORM_EPS = 1e-6
SSD_HEAD_DIM = 64
SSD_N_GROUPS = 8
SSD_D_STATE = 128
SSD_CONV_WIDTH = 4
SSD_CHUNK = 128
ATTN_HEAD_DIM = 128
DILATION_PATTERNS = ((128, 1), (512, 4), (2048, 16))
ATTN_HEADS_PER_GROUP = 4
ATTN_BLOCK = 128
N_EXPERT_GROUPS = 4
EXPERTS_PER_GROUP = 8
N_EXPERTS = N_EXPERT_GROUPS * EXPERTS_PER_GROUP
EXPERT_TOP_K = 2

LANES = 128
SUBLANES = 8
VMEM_LIMIT_BYTES = 56 * 1024 * 1024
ROW_DMA_UNROLL = 8

ROUTER_EXPERT_ROW0 = SUBLANES
ROUTER_ROWS = ROUTER_EXPERT_ROW0 + N_EXPERTS


def _params(*semantics):
    return pltpu.CompilerParams(dimension_semantics=semantics, vmem_limit_bytes=VMEM_LIMIT_BYTES)


def _split_bf16(v):
    hi = v.astype(BF16)
    lo = (v - hi.astype(F32)).astype(BF16)
    return hi, lo


def _silu(v):
    return v * (1.0 / (1.0 + jnp.exp(-v)))


def _regroup_weight_kernel(w_ref, out_ref, *, segments):
    off = 0
    for start, width in segments:
        out_ref[:, off:off + width] = w_ref[0, :, start:start + width].astype(out_ref.dtype)
        off += width


def _regroup_weight(w_stack, layer, segments, *, rows=256):
    _, k, cols = w_stack.shape
    out_cols = sum(width for _, width in segments)
    assert k % rows == 0 and out_cols % LANES == 0
    return pl.pallas_call(
        functools.partial(_regroup_weight_kernel, segments=segments),
        grid=(k // rows,),
        in_specs=[pl.BlockSpec((1, rows, cols), lambda i: (layer, i, 0))],
        out_specs=pl.BlockSpec((rows, out_cols), lambda i: (i, 0)),
        out_shape=jax.ShapeDtypeStruct((k, out_cols), BF16),
        compiler_params=_params("arbitrary"),
        name="regroup_weight",
    )(w_stack)


QKV_PARTS = 3


def _in_proj_kernel(x_ref, nw_ref, w_ref, wdt_ref, wdtt_ref, proj_ref, a0_ref, a1_ref, a2_ref,
                    dt_ref, dtt_ref, h_ref, stage_ref, *, n_plain):
    j = pl.program_id(1)
    tm, tn = proj_ref.shape

    @pl.when(j == 0)
    def _():
        xf = x_ref[...]
        ms = jnp.mean(xf * xf, axis=-1, keepdims=True)
        h = (xf * lax.rsqrt(ms + NORM_EPS) * nw_ref[...]).astype(BF16)
        h_ref[...] = h
        dt_ref[...] = jnp.dot(h, wdt_ref[...], preferred_element_type=F32)
        dtt_ref[...] = lax.dot_general(wdtt_ref[...], h, (((1,), (1,)), ((), ())),
                                       preferred_element_type=F32)

    @pl.when(j < n_plain)
    def _():
        proj_ref[...] = jnp.dot(h_ref[...], w_ref[...], preferred_element_type=F32).astype(BF16)

    for gi, a_ref in enumerate((a0_ref, a1_ref, a2_ref)):
        dil = DILATION_PATTERNS[gi][1]
        j0 = n_plain + QKV_PARTS * gi

        @pl.when((j >= j0) & (j < j0 + QKV_PARTS))
        def _(a_ref=a_ref, dil=dil):
            res = jnp.dot(h_ref[...], w_ref[...], preferred_element_type=F32)
            if dil == 1:
                a_ref[...] = res.astype(BF16)
            else:
                for s in range(tn // LANES):
                    stage_ref[s] = res[:, s * LANES:(s + 1) * LANES]
                for r in range(dil):
                    for s in range(tn // LANES):
                        c0 = r * tn + s * LANES
                        a_ref[:, c0:c0 + LANES] = stage_ref[
                            s, pl.ds(r, tm // dil, stride=dil), :].astype(BF16)


def _in_proj(x2, norm_w, w_main, w_dt, *, tm, tn, n_plain):
    n, d = x2.shape
    nh = w_dt.shape[1]
    n_blocks = w_main.shape[1] // tn
    assert n_blocks == n_plain + QKV_PARTS * len(DILATION_PATTERNS)

    def a_spec(gi):
        dil = DILATION_PATTERNS[gi][1]
        j0 = n_plain + QKV_PARTS * gi
        return pl.BlockSpec((tm // dil, dil * tn),
                            lambda i, j: (i, jnp.clip(j - j0, 0, QKV_PARTS - 1)))

    def a_shape(gi):
        dil = DILATION_PATTERNS[gi][1]
        return jax.ShapeDtypeStruct((n // dil, dil * QKV_PARTS * tn), BF16)

    return pl.pallas_call(
        functools.partial(_in_proj_kernel, n_plain=n_plain),
        grid=(n // tm, n_blocks),
        in_specs=[
            pl.BlockSpec((tm, d), lambda i, j: (i, 0)),
            pl.BlockSpec((1, d), lambda i, j: (0, 0)),
            pl.BlockSpec((d, tn), lambda i, j: (0, j)),
            pl.BlockSpec((d, nh), lambda i, j: (0, 0)),
            pl.BlockSpec((nh, d), lambda i, j: (0, 0)),
        ],
        out_specs=[
            pl.BlockSpec((tm, tn), lambda i, j: (i, jnp.minimum(j, n_plain - 1))),
            a_spec(0), a_spec(1), a_spec(2),
            pl.BlockSpec((tm, nh), lambda i, j: (i, 0)),
            pl.BlockSpec((nh, tm), lambda i, j: (0, i)),
        ],
        out_shape=[
            jax.ShapeDtypeStruct((n, n_plain * tn), BF16),
            a_shape(0), a_shape(1), a_shape(2),
            jax.ShapeDtypeStruct((n, nh), F32),
            jax.ShapeDtypeStruct((nh, n), F32),
        ],
        scratch_shapes=[pltpu.VMEM((tm, d), BF16), pltpu.VMEM((tn // LANES, tm, LANES), F32)],
        compiler_params=_params("arbitrary", "arbitrary"),
        name="in_proj",
    )(x2, norm_w.reshape(1, d), w_main, w_dt.astype(BF16), w_dt.T.astype(BF16))


LOG2E = math.log2(math.e)
CONV_ROW_PITCH = 2


def _ssd_kernel(xbc_ref, z_ref, dt_ref, dtt_ref, cw_ref, cb_ref, dtb_ref, dtbt_ref, alog_ref,
                alogt_ref, dskip_ref, nw_ref, expand_ref, out_ref,
                xbuf_ref, state_ref, y_ref, *, n_heads, d_inner):
    L = SSD_CHUNK
    P = SSD_HEAD_DIM
    NS = SSD_D_STATE
    G = SSD_N_GROUPS
    R = n_heads // G
    GW = R * P
    W = SSD_CONV_WIDTH
    RP = CONV_ROW_PITCH
    n_slabs = xbuf_ref.shape[0]
    c = pl.program_id(1)

    def rows(first, count):
        return pl.ds(first * RP, count, stride=RP)

    @pl.when(c == 0)
    def _():
        state_ref[...] = jnp.zeros(state_ref.shape, F32)
        for s in range(n_slabs):
            xbuf_ref[s, rows(0, SUBLANES), :] = jnp.zeros((SUBLANES, LANES), F32)

    @pl.when(c > 0)
    def _():
        for s in range(n_slabs):
            xbuf_ref[s, rows(0, SUBLANES), :] = xbuf_ref[s, rows(L, SUBLANES), :]

    for s in range(n_slabs):
        xbuf_ref[s, rows(SUBLANES, L), :] = xbc_ref[:, s * LANES:(s + 1) * LANES].astype(F32)

    def conv(col0, width):
        slabs = []
        for s in range(col0 // LANES, (col0 + width) // LANES):
            acc = cb_ref[:, s * LANES:(s + 1) * LANES]
            for w in range(W):
                acc = acc + (cw_ref[w:w + 1, s * LANES:(s + 1) * LANES]
                             * xbuf_ref[s, rows(SUBLANES - (W - 1) + w, L), :])
            slabs.append(acc)
        return _silu(jnp.concatenate(slabs, axis=1))

    def softplus(v):
        return jnp.maximum(v, 0.0) + jnp.log1p(jnp.exp(-jnp.abs(v)))

    dt = softplus(dt_ref[...] + dtb_ref[...])
    dtt = softplus(dtt_ref[...] + dtbt_ref[...])
    da = dt * (-LOG2E * jnp.exp(alog_ref[...]))
    dat = dtt * (-LOG2E * jnp.exp(alogt_ref[...]))
    row = lax.broadcasted_iota(jnp.int32, (L, L), 0)
    col = lax.broadcasted_iota(jnp.int32, (L, L), 1)
    causal = row >= col
    tri = jnp.where(causal, 1.0, 0.0).astype(BF16)
    trit = jnp.where(row <= col, 1.0, 0.0).astype(BF16)

    def split3(v):
        a = v.astype(BF16)
        r1 = v - a.astype(F32)
        b = r1.astype(BF16)
        cc = (r1 - b.astype(F32)).astype(BF16)
        return a, b, cc

    a2 = sum(jnp.dot(tri, p, preferred_element_type=F32) for p in split3(da))
    a2t = sum(jnp.dot(p, trit, preferred_element_type=F32) for p in split3(dat))
    a2_last = a2[L - 1:L, :]

    expand = expand_ref[...]

    def expand_heads(v):
        hi, lo = _split_bf16(v)
        return jnp.dot(jnp.concatenate([hi, lo], axis=1), expand, preferred_element_type=F32)

    in_scale_e = expand_heads(dt * jnp.exp2(a2_last - a2))
    tail8 = jnp.concatenate([jnp.exp2(a2_last), dskip_ref[...],
                             jnp.zeros((SUBLANES - 2, n_heads), F32)], axis=0)
    out_e = expand_heads(jnp.concatenate([jnp.exp2(a2), tail8], axis=0))
    out_scale_e = out_e[0:L, :]
    chunk_decay_e = out_e[L:L + 1, :]
    dskip_e = out_e[L + 1:L + 2, :]

    first_head = lax.broadcasted_iota(jnp.int32, (L, LANES), 1) < P

    for g in range(G):
        x0 = g * GW
        xs = conv(x0, GW)
        bm = conv(d_inner + g * NS, NS).astype(BF16)
        cm = conv(d_inner + G * NS + g * NS, NS).astype(BF16)
        cb = lax.dot_general(cm, bm, (((1,), (1,)), ((), ())), preferred_element_type=F32)
        cb = jnp.where(causal, cb, 0.0)
        y_parts = []
        for pr in range(GW // LANES):
            wgts = []
            for q in range(LANES // P):
                h = g * R + pr * (LANES // P) + q
                seg = a2[:, h:h + 1] - a2t[h:h + 1, :]
                decay = jnp.exp2(jnp.minimum(seg, 0.0))
                wgts.append((cb * decay * dtt[h:h + 1, :]).astype(BF16))
            slab = xs[:, pr * LANES:(pr + 1) * LANES]
            rhs = jnp.concatenate([jnp.where(first_head, slab, 0.0).astype(BF16),
                                   jnp.where(first_head, 0.0, slab).astype(BF16)], axis=0)
            y_parts.append(jnp.dot(jnp.concatenate(wgts, axis=1), rhs, preferred_element_type=F32))
        y = jnp.concatenate(y_parts, axis=1) + xs * dskip_e[:, x0:x0 + GW]
        st = state_ref[:, x0:x0 + GW]
        y = y + (jnp.dot(cm, st.astype(BF16), preferred_element_type=F32)
                 * out_scale_e[:, x0:x0 + GW])
        y_ref[:, x0:x0 + GW] = y
        xin = (xs * in_scale_e[:, x0:x0 + GW]).astype(BF16)
        st_new = lax.dot_general(bm, xin, (((0,), (0,)), ((), ())), preferred_element_type=F32)
        state_ref[:, x0:x0 + GW] = st * chunk_decay_e[:, x0:x0 + GW] + st_new

    yz = y_ref[...] * _silu(z_ref[...].astype(F32))
    ms = jnp.mean(yz * yz, axis=-1, keepdims=True)
    out_ref[...] = (yz * lax.rsqrt(ms + NORM_EPS) * nw_ref[...]).astype(out_ref.dtype)


def _ssd(proj, dt, dtt, conv_w, conv_b, dt_bias, a_log, d_skip, norm_w, *, batch, seq, d_inner,
         xbc_block, z_block):
    n = batch * seq
    n_heads = dt.shape[1]
    L = SSD_CHUNK
    nc = seq // L
    conv_dim = conv_w.shape[1]
    expand = (np.arange(d_inner)[None, :] // SSD_HEAD_DIM == np.arange(n_heads)[:, None])
    expand = jnp.asarray(np.concatenate([expand, expand], axis=0), BF16)
    assert conv_dim % LANES == 0
    kern = functools.partial(_ssd_kernel, n_heads=n_heads, d_inner=d_inner)
    small = lambda shape: pl.BlockSpec(shape, lambda b, c: (0, 0))
    return pl.pallas_call(
        kern,
        grid=(batch, nc),
        in_specs=[
            pl.BlockSpec((L, conv_dim), lambda b, c: (b * nc + c, xbc_block)),
            pl.BlockSpec((L, d_inner), lambda b, c: (b * nc + c, z_block)),
            pl.BlockSpec((L, n_heads), lambda b, c: (b * nc + c, 0)),
            pl.BlockSpec((n_heads, L), lambda b, c: (0, b * nc + c)),
            small((SSD_CONV_WIDTH, conv_dim)),
            small((1, conv_dim)),
            small((1, n_heads)),
            small((n_heads, 1)),
            small((1, n_heads)),
            small((n_heads, 1)),
            small((1, n_heads)),
            small((1, d_inner)),
            small((2 * n_heads, d_inner)),
        ],
        out_specs=pl.BlockSpec((L, d_inner), lambda b, c: (b * nc + c, 0)),
        out_shape=jax.ShapeDtypeStruct((n, d_inner), BF16),
        scratch_shapes=[
            pltpu.VMEM((conv_dim // LANES, CONV_ROW_PITCH * (L + SUBLANES), LANES), F32),
            pltpu.VMEM((SSD_D_STATE, d_inner), F32),
            pltpu.VMEM((L, d_inner), F32),
        ],
        compiler_params=_params("arbitrary", "arbitrary"),
        name="ssd",
    )(proj, proj, dt, dtt, conv_w, conv_b.reshape(1, -1), dt_bias.reshape(1, -1),
      dt_bias.reshape(-1, 1), a_log.reshape(1, -1), a_log.reshape(-1, 1), d_skip.reshape(1, -1),
      norm_w.reshape(1, -1), expand)


def _attn_kernel(q_ref, k_ref, v_ref, o_ref, lse_ref, kp_ref, vp_ref, *, slopes, dilation, hops):
    nb = pl.program_id(2)
    BLK = ATTN_BLOCK
    E = ATTN_HEAD_DIM

    @pl.when(nb == 0)
    def _():
        kp_ref[...] = jnp.zeros(kp_ref.shape, kp_ref.dtype)
        vp_ref[...] = jnp.zeros(vp_ref.shape, vp_ref.dtype)

    qi = lax.broadcasted_iota(jnp.int32, (BLK, BLK), 0)
    ki = lax.broadcasted_iota(jnp.int32, (BLK, BLK), 1)
    rel_cur = qi - ki
    rel_prev = rel_cur + BLK
    ok_cur = rel_cur >= 0
    ok_prev = rel_prev <= jnp.where(nb > 0, hops, -1)
    dist_cur = (rel_cur * dilation).astype(F32)
    dist_prev = (rel_prev * dilation).astype(F32)
    lane = lax.broadcasted_iota(jnp.int32, (BLK, LANES), 1)
    scale = E ** -0.5
    nt = (((1,), (1,)), ((), ()))

    q = q_ref[...]
    k = k_ref[...]
    v = v_ref[...]
    kp = kp_ref[...]
    vp = vp_ref[...]
    lse_tile = jnp.zeros((BLK, LANES), F32)
    for h, slope in enumerate(slopes):
        sl = slice(h * E, (h + 1) * E)
        qh = q[:, sl]
        s_cur = lax.dot_general(qh, k[:, sl], nt, preferred_element_type=F32)
        s_prev = lax.dot_general(qh, kp[:, sl], nt, preferred_element_type=F32)
        l_cur = jnp.where(ok_cur, s_cur * scale - slope * dist_cur, -jnp.inf)
        l_prev = jnp.where(ok_prev, s_prev * scale - slope * dist_prev, -jnp.inf)
        m = jnp.max(jnp.maximum(l_cur, l_prev), axis=-1, keepdims=True)
        p_cur = jnp.exp(l_cur - m)
        p_prev = jnp.exp(l_prev - m)
        den = jnp.sum(p_cur + p_prev, axis=-1, keepdims=True)
        acc = (jnp.dot(p_cur.astype(BF16), v[:, sl], preferred_element_type=F32)
               + jnp.dot(p_prev.astype(BF16), vp[:, sl], preferred_element_type=F32))
        o_ref[:, sl] = (acc / den).astype(o_ref.dtype)
        lse_tile = jnp.where(lane == h, m + jnp.log(den), lse_tile)
    lse_ref[...] = lse_tile
    kp_ref[...] = k
    vp_ref[...] = v


def _attn_group(qkv, gi, *, batch, seq):
    window, dilation = DILATION_PATTERNS[gi]
    hops = window // dilation
    n_heads_total = ATTN_HEADS_PER_GROUP * len(DILATION_PATTERNS)
    slopes = tuple(float(2.0 ** (-8.0 * (gi * ATTN_HEADS_PER_GROUP + h + 1) / n_heads_total))
                   for h in range(ATTN_HEADS_PER_GROUP))
    gw = ATTN_HEADS_PER_GROUP * ATTN_HEAD_DIM
    assert seq % (dilation * ATTN_BLOCK) == 0
    sub = seq // dilation
    nb = sub // ATTN_BLOCK
    kern = functools.partial(_attn_kernel, slopes=slopes, dilation=dilation, hops=hops)

    def part(p):
        return pl.BlockSpec((ATTN_BLOCK, gw), lambda b, r, n: (b * nb + n, p * dilation + r))

    o, lse = pl.pallas_call(
        kern,
        grid=(batch, dilation, nb),
        in_specs=[part(0), part(1), part(2)],
        out_specs=[
            pl.BlockSpec((ATTN_BLOCK, gw), lambda b, r, n: (b * nb + n, r)),
            pl.BlockSpec((ATTN_BLOCK, LANES), lambda b, r, n: (b * nb + n, r)),
        ],
        out_shape=[
            jax.ShapeDtypeStruct((batch * sub, dilation * gw), BF16),
            jax.ShapeDtypeStruct((batch * sub, dilation * LANES), F32),
        ],
        scratch_shapes=[pltpu.VMEM((ATTN_BLOCK, gw), BF16), pltpu.VMEM((ATTN_BLOCK, gw), BF16)],
        compiler_params=_params("arbitrary", "arbitrary", "arbitrary"),
        name=f"attn_g{gi}",
    )(qkv, qkv, qkv)
    return o.reshape(batch * seq, gw), lse.reshape(batch * seq, LANES)


def _mix_kernel(yn_ref, o0_ref, o1_ref, o2_ref, l0_ref, l1_ref, l2_ref, gate_ref, x_ref,
                wssd_ref, wattn_ref, wout_ref, bg_ref, fnw_ref, wrh_ref, wrl_ref, rb_ref,
                x1_ref, hn_ref, eid_ref, gcol_ref, rank_ref, cnt_ref, carry_ref, *, d_model):
    i = pl.program_id(0)
    tm = x_ref.shape[0]
    E = ATTN_HEAD_DIM

    @pl.when(i == 0)
    def _():
        carry_ref[...] = jnp.zeros(carry_ref.shape, F32)

    y_ssd = jnp.dot(yn_ref[...], wssd_ref[...], preferred_element_type=F32)

    l0, l1, l2 = l0_ref[...], l1_ref[...], l2_ref[...]
    lm = jnp.maximum(jnp.maximum(l0, l1), l2)
    e0, e1, e2 = jnp.exp(l0 - lm), jnp.exp(l1 - lm), jnp.exp(l2 - lm)
    inv = 1.0 / (e0 + e1 + e2)
    parts = []
    for h in range(ATTN_HEADS_PER_GROUP):
        sl = slice(h * E, (h + 1) * E)
        parts.append((e0[:, h:h + 1] * inv[:, h:h + 1]) * o0_ref[:, sl].astype(F32)
                     + (e1[:, h:h + 1] * inv[:, h:h + 1]) * o1_ref[:, sl].astype(F32)
                     + (e2[:, h:h + 1] * inv[:, h:h + 1]) * o2_ref[:, sl].astype(F32))
    o = jnp.concatenate(parts, axis=-1).astype(BF16)
    y_attn = jnp.dot(o, wattn_ref[...], preferred_element_type=F32)

    graw = gate_ref[...].astype(F32) + bg_ref[...]
    gates = 1.0 / (1.0 + jnp.exp(-graw))
    merged = (gates[:, :d_model] * y_ssd + gates[:, d_model:] * y_attn).astype(BF16)
    x1 = x_ref[...] + jnp.dot(merged, wout_ref[...], preferred_element_type=F32)
    x1_ref[...] = x1

    ms = jnp.mean(x1 * x1, axis=-1, keepdims=True)
    hn = x1 * lax.rsqrt(ms + NORM_EPS) * fnw_ref[...]
    hn_ref[...] = hn

    hn_hi, hn_lo = _split_bf16(hn)
    nt = (((1,), (1,)), ((), ()))
    logits = (lax.dot_general(wrh_ref[...], hn_hi, nt, preferred_element_type=F32)
              + lax.dot_general(wrh_ref[...], hn_lo, nt, preferred_element_type=F32)
              + lax.dot_general(wrl_ref[...], hn_hi, nt, preferred_element_type=F32)
              + rb_ref[...])

    grow = lax.broadcasted_iota(jnp.int32, (SUBLANES, tm), 0)
    gl = jnp.where(grow < N_EXPERT_GROUPS, logits[0:SUBLANES, :], -jnp.inf)
    gmax = jnp.max(gl, axis=0, keepdims=True)
    gidx = jnp.min(jnp.where(gl == gmax, grow, N_EXPERT_GROUPS), axis=0, keepdims=True)
    group_gate = 1.0 / jnp.sum(jnp.exp(gl - gmax), axis=0, keepdims=True)

    in_group = jnp.zeros((EXPERTS_PER_GROUP, tm), F32)
    for g in range(N_EXPERT_GROUPS):
        r0 = ROUTER_EXPERT_ROW0 + g * EXPERTS_PER_GROUP
        in_group = jnp.where(gidx == g, logits[r0:r0 + EXPERTS_PER_GROUP, :], in_group)
    erow = lax.broadcasted_iota(jnp.int32, in_group.shape, 0)
    v1 = jnp.max(in_group, axis=0, keepdims=True)
    i1 = jnp.min(jnp.where(in_group == v1, erow, EXPERTS_PER_GROUP), axis=0, keepdims=True)
    rest = jnp.where(erow == i1, -jnp.inf, in_group)
    v2 = jnp.max(rest, axis=0, keepdims=True)
    i2 = jnp.min(jnp.where(rest == v2, erow, EXPERTS_PER_GROUP), axis=0, keepdims=True)
    t = jnp.exp(v2 - v1)
    g1 = group_gate / (1.0 + t)
    g2 = group_gate * t / (1.0 + t)
    eid1 = gidx * EXPERTS_PER_GROUP + i1
    eid2 = gidx * EXPERTS_PER_GROUP + i2
    slot = lax.broadcasted_iota(jnp.int32, (SUBLANES, tm), 0)
    eid_ref[...] = jnp.where(slot == 0, eid1, jnp.where(slot == 1, eid2, 0))

    grow8 = lax.broadcasted_iota(jnp.int32, (LANES, tm), 0)
    gt = jnp.where(grow8 == 0, g1, jnp.where(grow8 == 1, g2, 0.0))
    gcol_ref[...] = gt.T

    xrow = lax.broadcasted_iota(jnp.int32, (N_EXPERTS, tm), 0)
    oh1 = xrow == eid1
    oh2 = xrow == eid2
    oh = jnp.where(oh1 | oh2, 1.0, 0.0)
    ti = lax.broadcasted_iota(jnp.int32, (tm, tm), 0)
    tj = lax.broadcasted_iota(jnp.int32, (tm, tm), 1)
    before = jnp.where(ti < tj, 1.0, 0.0).astype(BF16)
    prior = jnp.dot(oh.astype(BF16), before, preferred_element_type=F32) + carry_ref[:, 0:1]
    r1 = jnp.sum(jnp.where(oh1, prior, 0.0), axis=0, keepdims=True)
    r2 = jnp.sum(jnp.where(oh2, prior, 0.0), axis=0, keepdims=True)
    rank_ref[...] = jnp.where(slot == 0, r1, jnp.where(slot == 1, r2, 0.0)).astype(jnp.int32)
    carry_ref[...] = carry_ref[...] + jnp.sum(oh, axis=1, keepdims=True)
    cnt_ref[...] = carry_ref[...].astype(jnp.int32)


def _mix(yn, outs, lses, proj, x2, w_ssd_out, w_attn_out, w_out, b_gate, ffn_norm_w, wr, rb, *,
         tm, gate_block):
    n, d = x2.shape
    aw = outs[0].shape[1]
    wrt = jnp.zeros((ROUTER_ROWS, d), F32)
    wrt = wrt.at[0:N_EXPERT_GROUPS].set(wr[0].T).at[ROUTER_EXPERT_ROW0:].set(wr[1].T)
    rbc = jnp.zeros((ROUTER_ROWS, 1), F32)
    rbc = rbc.at[0:N_EXPERT_GROUPS, 0].set(rb[0]).at[ROUTER_EXPERT_ROW0:, 0].set(rb[1])
    wrh, wrl = _split_bf16(wrt)
    row = lambda w: pl.BlockSpec((tm, w), lambda i: (i, 0))
    const = lambda shape: pl.BlockSpec(shape, lambda i: (0, 0), pipeline_mode=pl.Buffered(1))
    kern = functools.partial(_mix_kernel, d_model=d)
    return pl.pallas_call(
        kern,
        grid=(n // tm,),
        in_specs=[
            row(d), row(aw), row(aw), row(aw), row(LANES), row(LANES), row(LANES),
            pl.BlockSpec((tm, 2 * d), lambda i: (i, gate_block)),
            row(d),
            const((d, d)), const((aw, d)), const((d, d)), const((1, 2 * d)), const((1, d)),
            const((ROUTER_ROWS, d)), const((ROUTER_ROWS, d)), const((ROUTER_ROWS, 1)),
        ],
        out_specs=[
            row(d), row(d),
            pl.BlockSpec((SUBLANES, tm), lambda i: (0, i)),
            row(LANES),
            pl.BlockSpec((SUBLANES, tm), lambda i: (0, i)),
            pl.BlockSpec((N_EXPERTS, LANES), lambda i: (0, 0)),
        ],
        out_shape=[
            jax.ShapeDtypeStruct((n, d), F32),
            jax.ShapeDtypeStruct((n, d), F32),
            jax.ShapeDtypeStruct((SUBLANES, n), jnp.int32),
            jax.ShapeDtypeStruct((n, LANES), F32),
            jax.ShapeDtypeStruct((SUBLANES, n), jnp.int32),
            jax.ShapeDtypeStruct((N_EXPERTS, LANES), jnp.int32),
        ],
        scratch_shapes=[pltpu.VMEM((N_EXPERTS, LANES), F32)],
        compiler_params=_params("arbitrary"),
        name="mix",
    )(yn, outs[0], outs[1], outs[2], lses[0], lses[1], lses[2], proj, x2,
      w_ssd_out.astype(BF16), w_attn_out.astype(BF16), w_out.astype(BF16),
      b_gate.reshape(1, -1), ffn_norm_w.reshape(1, -1), wrh, wrl, rbc)


def _scatter_kernel(pad_start_ref, pad_len_ref, n_active_ref, pos_ref, hn_ref, xs_ref, zero_ref,
                    sem, zsem, *, tme, n_tiles):
    tm = hn_ref.shape[0]
    zrows = zero_ref.shape[0]

    @pl.when(pl.program_id(0) == 0)
    def _():
        zero_ref[...] = jnp.zeros(zero_ref.shape, zero_ref.dtype)

        def fills(act):
            def per_expert(e, carry):
                off = pad_start_ref[e]
                left = pad_len_ref[e]
                head = left & (SUBLANES - 1)
                for r in range(SUBLANES - 1):
                    @pl.when(r < head)
                    def _(r=r):
                        act(pltpu.make_async_copy(zero_ref.at[pl.ds(0, 1), :],
                                                  xs_ref.at[pl.ds(off + r, 1), :], zsem))

                off = off + head
                bit = zrows
                while bit >= SUBLANES:
                    take = left & bit

                    @pl.when(take != 0)
                    def _(off=off, bit=bit):
                        act(pltpu.make_async_copy(
                            zero_ref.at[pl.ds(0, bit), :],
                            xs_ref.at[pl.ds(pl.multiple_of(off, SUBLANES), bit), :], zsem))

                    off = off + take
                    bit //= 2
                return carry

            def per_tile(t, carry):
                @pl.when(t >= n_active_ref[0])
                def _():
                    for part in range(tme // zrows):
                        row0 = pl.multiple_of(t * tme + part * zrows, zrows)
                        act(pltpu.make_async_copy(zero_ref, xs_ref.at[pl.ds(row0, zrows), :], zsem))

                return carry

            lax.fori_loop(0, N_EXPERTS, per_expert, 0)
            lax.fori_loop(0, n_tiles, per_tile, 0)

        fills(lambda cp: cp.start())
        fills(lambda cp: cp.wait())

    def start(tb, carry):
        for u in range(ROW_DMA_UNROLL):
            t = tb * ROW_DMA_UNROLL + u
            for k in range(EXPERT_TOP_K):
                dst = pos_ref[0, 0, k * tm + t]
                pltpu.make_async_copy(hn_ref.at[pl.ds(t, 1), :], xs_ref.at[pl.ds(dst, 1), :],
                                      sem).start(priority=(u + k) % 2)
        return carry

    lax.fori_loop(0, tm // ROW_DMA_UNROLL, start, 0)
    for k in range(EXPERT_TOP_K):
        pltpu.make_async_copy(hn_ref, xs_ref.at[pl.ds(0, tm), :], sem).wait()


def _scatter(hn, pos_tiles, pad_start, pad_len, n_active, *, tm, tme, n_tiles):
    n, d = hn.shape
    assert tme % 2 == 0 and (tme // 2) & (tme // 2 - 1) == 0
    grid_spec = pltpu.PrefetchScalarGridSpec(
        num_scalar_prefetch=3,
        grid=(n // tm,),
        in_specs=[
            pl.BlockSpec((1, 1, EXPERT_TOP_K * tm), lambda i, *_: (i, 0, 0),
                         memory_space=pltpu.SMEM),
            pl.BlockSpec((tm, d), lambda i, *_: (i, 0)),
        ],
        out_specs=pl.BlockSpec(memory_space=pl.ANY),
        scratch_shapes=[pltpu.VMEM((tme // 2, d), hn.dtype), pltpu.SemaphoreType.DMA(()),
                        pltpu.SemaphoreType.DMA(())],
    )
    return pl.pallas_call(
        functools.partial(_scatter_kernel, tme=tme, n_tiles=n_tiles),
        grid_spec=grid_spec,
        out_shape=jax.ShapeDtypeStruct((n_tiles * tme, d), hn.dtype),
        compiler_params=_params("arbitrary"),
        name="scatter",
    )(pad_start, pad_len, n_active, pos_tiles, hn)


W_CHUNK_ROWS = 256
W_RING = 8


def _experts_kernel(te_ref, slot_ref, nxt_ref, lo_ref, hi_ref, na_ref, xs_ref, wg_hbm, wu_hbm,
                    wd_hbm, ys_ref, wres_ref, stage_ref, sem, *, n_gu, n_r, n_h):
    i = pl.program_id(0)
    n_chunks = 2 * n_gu + n_h * n_r
    ring, cr, f = stage_ref.shape
    d = n_gu * cr

    def stage_copy(src, c):
        return pltpu.make_async_copy(src, stage_ref.at[c % ring], sem.at[c % ring])

    def start_chunk(e, c):
        @pl.when(c < n_gu)
        def _():
            stage_copy(wg_hbm.at[e, pl.ds(pl.multiple_of(c * cr, cr), cr), :], c).start()

        @pl.when((c >= n_gu) & (c < 2 * n_gu))
        def _():
            stage_copy(wu_hbm.at[e, pl.ds(pl.multiple_of((c - n_gu) * cr, cr), cr), :], c).start()

        @pl.when(c >= 2 * n_gu)
        def _():
            q = c - 2 * n_gu
            h = q // n_r
            r = q - h * n_r
            stage_copy(wd_hbm.at[e, pl.ds(pl.multiple_of(r * cr, cr), cr),
                                 pl.ds(pl.multiple_of(h * f, f), f)], c).start()

    def start_first(e):
        def body(c, carry):
            start_chunk(e, c)
            return carry

        lax.fori_loop(0, ring, body, 0)

    def convert(e, lo, hi, dst_slot):
        def body(c, carry):
            stage_copy(wg_hbm.at[0, pl.ds(0, cr), :], c).wait()
            wres_ref[dst_slot, pl.ds(pl.multiple_of(c * cr, cr), cr), :] = (
                stage_ref[c % ring].astype(BF16))

            @pl.when(c + ring < n_chunks)
            def _():
                start_chunk(e, c + ring)

            return carry

        lax.fori_loop(lo, hi, body, 0)

    @pl.when(i == 0)
    def _():
        start_first(te_ref[0])
        convert(te_ref[0], 0, n_chunks, slot_ref[0])

    @pl.when(i < na_ref[0])
    def _():
        slot = slot_ref[i]
        nxt = nxt_ref[i]

        @pl.when((nxt >= 0) & (lo_ref[i] == 0))
        def _():
            start_first(nxt)

        xb = xs_ref[...].astype(BF16)
        hg = jnp.dot(xb, wres_ref[slot, 0:d, :], preferred_element_type=F32)
        hu = jnp.dot(xb, wres_ref[slot, d:2 * d, :], preferred_element_type=F32)
        hmid = (_silu(hg) * hu).astype(BF16)
        for h in range(n_h):
            r0 = 2 * d + h * n_r * cr
            ys_ref[:, h * f:(h + 1) * f] = jnp.dot(hmid, wres_ref[slot, r0:r0 + n_r * cr, :],
                                                    preferred_element_type=F32)

        @pl.when(nxt >= 0)
        def _():
            convert(nxt, lo_ref[i], hi_ref[i], 1 - slot)

    @pl.when(i >= na_ref[0])
    def _():
        ys_ref[...] = jnp.zeros(ys_ref.shape, F32)


def _experts(xs, tile_expert, n_active, cnt, starts, padded, w_g, w_u, w_d, *, tm):
    p, d = xs.shape
    ne, _, f = w_g.shape
    n_tiles = p // tm
    cr = W_CHUNK_ROWS
    assert d % cr == 0 and f % cr == 0 and d % f == 0
    n_gu, n_r, n_h = d // cr, f // cr, d // f
    n_chunks = 2 * n_gu + n_h * n_r

    ids = jnp.arange(ne, dtype=jnp.int32)
    present = cnt > 0
    later = jnp.where(present[None, :] & (ids[None, :] > ids[:, None]), ids[None, :], ne)
    nxt_e = jnp.min(later, axis=1)
    nxt_e = jnp.where(nxt_e >= ne, -1, nxt_e)
    run_e = jnp.cumsum(present.astype(jnp.int32)) - 1
    onehot = (tile_expert[:, None] == ids[None, :]).astype(jnp.int32)
    pick = lambda v: jnp.sum(onehot * v[None, :].astype(jnp.int32), axis=1)
    tiles = jnp.arange(n_tiles, dtype=jnp.int32)
    active = tiles < n_active[0]
    j = tiles - pick(starts) // tm
    k = jnp.maximum(pick(padded) // tm, 1)
    nxt_t = jnp.where(active, pick(nxt_e), -1).astype(jnp.int32)
    lo_t = jnp.where(active, (n_chunks * j) // k, 0).astype(jnp.int32)
    hi_t = jnp.where(active, (n_chunks * (j + 1)) // k, 0).astype(jnp.int32)
    slot_t = (pick(run_e) % 2).astype(jnp.int32)

    def rows(i, te, sl, nx, lo, hi, na):
        return (jnp.minimum(i, na[0] - 1), 0)

    grid_spec = pltpu.PrefetchScalarGridSpec(
        num_scalar_prefetch=6,
        grid=(n_tiles,),
        in_specs=[
            pl.BlockSpec((tm, d), rows),
            pl.BlockSpec(memory_space=pl.ANY),
            pl.BlockSpec(memory_space=pl.ANY),
            pl.BlockSpec(memory_space=pl.ANY),
        ],
        out_specs=pl.BlockSpec((tm, d), lambda i, *_: (i, 0)),
        scratch_shapes=[
            pltpu.VMEM((2, n_chunks * cr, f), BF16),
            pltpu.VMEM((W_RING, cr, f), F32),
            pltpu.SemaphoreType.DMA((W_RING,)),
        ],
    )
    return pl.pallas_call(
        functools.partial(_experts_kernel, n_gu=n_gu, n_r=n_r, n_h=n_h),
        grid_spec=grid_spec,
        out_shape=jax.ShapeDtypeStruct((p, d), F32),
        compiler_params=_params("arbitrary"),
        name="experts",
    )(tile_expert, slot_t, nxt_t, lo_t, hi_t, n_active, xs, w_g, w_u, w_d)


def _combine_kernel(pos_ref, pos_next_ref, ys_ref, x1_ref, gcol_ref, nw_ref, out_ref, buf_ref, sem,
                    *, final):
    i = pl.program_id(0)
    n_steps = pl.num_programs(0)
    tm = x1_ref.shape[0]

    def issue(p_ref, slot):
        def start(tb, carry):
            for u in range(ROW_DMA_UNROLL):
                t = tb * ROW_DMA_UNROLL + u
                for k in range(EXPERT_TOP_K):
                    src = p_ref[0, 0, k * tm + t]
                    pltpu.make_async_copy(ys_ref.at[pl.ds(src, 1), :],
                                          buf_ref.at[slot, k, pl.ds(t, 1), :],
                                          sem.at[slot]).start(priority=(u + k) % 2)
            return carry

        lax.fori_loop(0, tm // ROW_DMA_UNROLL, start, 0)

    slot = i % 2

    @pl.when(i == 0)
    def _():
        issue(pos_ref, 0)

    @pl.when(i + 1 < n_steps)
    def _():
        issue(pos_next_ref, 1 - slot)

    for k in range(EXPERT_TOP_K):
        pltpu.make_async_copy(ys_ref.at[pl.ds(0, tm), :], buf_ref.at[slot, k], sem.at[slot]).wait()
    g = gcol_ref[...]
    xo = x1_ref[...] + g[:, 0:1] * buf_ref[slot, 0] + g[:, 1:2] * buf_ref[slot, 1]
    if final:
        ms = jnp.mean(xo * xo, axis=-1, keepdims=True)
        xo = xo * lax.rsqrt(ms + NORM_EPS) * nw_ref[...]
    out_ref[...] = xo


def _combine(ys, pos_tiles, x1, gcol, norm_w, *, tm, final):
    n, d = x1.shape
    last = n // tm - 1
    return pl.pallas_call(
        functools.partial(_combine_kernel, final=final),
        grid=(n // tm,),
        in_specs=[
            pl.BlockSpec((1, 1, EXPERT_TOP_K * tm), lambda i: (i, 0, 0), memory_space=pltpu.SMEM),
            pl.BlockSpec((1, 1, EXPERT_TOP_K * tm), lambda i: (jnp.minimum(i + 1, last), 0, 0),
                         memory_space=pltpu.SMEM),
            pl.BlockSpec(memory_space=pl.ANY),
            pl.BlockSpec((tm, d), lambda i: (i, 0)),
            pl.BlockSpec((tm, LANES), lambda i: (i, 0)),
            pl.BlockSpec((1, d), lambda i: (0, 0)),
        ],
        out_specs=pl.BlockSpec((tm, d), lambda i: (i, 0)),
        out_shape=jax.ShapeDtypeStruct((n, d), F32),
        scratch_shapes=[pltpu.VMEM((2, EXPERT_TOP_K, tm, d), F32), pltpu.SemaphoreType.DMA((2,))],
        compiler_params=_params("arbitrary"),
        name="combine",
    )(pos_tiles, pos_tiles, ys, x1, gcol, norm_w.reshape(1, d))


def _tiles(n):
    return dict(proj_tm=math.gcd(n, 1024), mix_tm=math.gcd(n, 256), moe_tm=256,
                route_tm=math.gcd(n, 256))


def _layer(x2, batch, seq, layer, attn_norm_w, w_in_stack, b_gate, conv_w, conv_b, dt_bias, a_log,
           d_skip, ssd_norm_w, w_ssd_out, w_attn_out, w_out, ffn_norm_w, w_gr, b_gr, w_er, b_er,
           w_g, w_u, w_d):
    n, d = x2.shape
    n_heads = dt_bias.shape[0]
    d_inner = ssd_norm_w.shape[0]
    conv_dim = conv_w.shape[1]
    aw_total = ATTN_HEADS_PER_GROUP * len(DILATION_PATTERNS) * ATTN_HEAD_DIM
    gw = ATTN_HEADS_PER_GROUP * ATTN_HEAD_DIM
    tiles = _tiles(n)

    c_z, c_xbc, c_dt = d_inner, d_inner + conv_dim, d_inner + conv_dim + n_heads
    c_gate = c_dt + QKV_PARTS * aw_total
    segments = [(c_gate, 2 * d), (c_z, conv_dim), (0, d_inner)]
    segments += [(c_dt + p * aw_total + gi * gw, gw)
                 for gi in range(len(DILATION_PATTERNS)) for p in range(QKV_PARTS)]
    w_main = _regroup_weight(w_in_stack, layer, tuple(segments))
    w_dt = w_in_stack[layer, :, c_xbc:c_dt]
    off_gate, off_xbc, off_z = 0, 2 * d, 2 * d + conv_dim
    plain_cols = off_z + d_inner
    assert off_xbc % conv_dim == 0 and off_z % d_inner == 0 and plain_cols % gw == 0

    proj, qkv0, qkv1, qkv2, dt, dtt = _in_proj(x2, attn_norm_w, w_main, w_dt, tm=tiles["proj_tm"],
                                               tn=gw, n_plain=plain_cols // gw)

    yn = _ssd(proj, dt, dtt, conv_w, conv_b, dt_bias, a_log, d_skip, ssd_norm_w, batch=batch,
              seq=seq, d_inner=d_inner, xbc_block=off_xbc // conv_dim, z_block=off_z // d_inner)

    outs, lses = [], []
    for gi, qkv in enumerate((qkv0, qkv1, qkv2)):
        o_g, lse_g = _attn_group(qkv, gi, batch=batch, seq=seq)
        outs.append(o_g)
        lses.append(lse_g)

    x1, hn, eid, gcol, rank, counts = _mix(
        yn, outs, lses, proj, x2, w_ssd_out, w_attn_out, w_out, b_gate, ffn_norm_w,
        (w_gr, w_er), (b_gr, b_er), tm=tiles["mix_tm"], gate_block=off_gate // (2 * d))

    tme = tiles["moe_tm"]
    cnt = counts[:, 0]
    padded = ((cnt + tme - 1) // tme) * tme
    ends = jnp.cumsum(padded)
    starts = ends - padded
    experts = jnp.arange(N_EXPERTS, dtype=jnp.int32)[:, None, None]
    pos = rank[:EXPERT_TOP_K] + jnp.sum(
        jnp.where(eid[None, :EXPERT_TOP_K] == experts, starts[:, None, None], 0), axis=0)
    n_tiles = EXPERT_TOP_K * n // tme + N_EXPERTS
    tile_start = jnp.arange(n_tiles, dtype=jnp.int32) * tme
    tile_expert = jnp.minimum(jnp.sum(ends[None, :] <= tile_start[:, None], axis=1),
                              N_EXPERTS - 1).astype(jnp.int32)
    n_active = (ends[-1:] // tme).astype(jnp.int32)

    rtm = tiles["route_tm"]
    pos_tiles = pos.reshape(EXPERT_TOP_K, n // rtm, rtm).transpose(1, 0, 2).reshape(
        n // rtm, 1, EXPERT_TOP_K * rtm)
    xs = _scatter(hn, pos_tiles, (starts + cnt).astype(jnp.int32), (padded - cnt).astype(jnp.int32),
                  n_active, tm=rtm, tme=tme, n_tiles=n_tiles)
    ys = _experts(xs, tile_expert, n_active, cnt, starts, padded, w_g, w_u, w_d, tm=tme)
    return ys, pos_tiles, x1, gcol, rtm


def kernel(x, attn_norm_w, w_in, b_gate, conv_w, conv_b, dt_bias, a_log, d_skip, ssd_norm_w,
           w_ssd_out, w_attn_out, w_out, ffn_norm_w, w_group_router, b_group_router,
           w_expert_router, b_expert_router, w_exp_gate, w_exp_up, w_exp_down, final_norm_w):
    batch, seq, d = x.shape
    depth = w_in.shape[0]
    x2 = x.reshape(batch * seq, d)
    for layer in range(depth):
        ys, pos_tiles, x1, gcol, rtm = _layer(
            x2, batch, seq, layer, attn_norm_w[layer], w_in, b_gate[layer], conv_w[layer],
            conv_b[layer], dt_bias[layer], a_log[layer], d_skip[layer], ssd_norm_w[layer],
            w_ssd_out[layer], w_attn_out[layer], w_out[layer], ffn_norm_w[layer],
            w_group_router[layer], b_group_router[layer], w_expert_router[layer],
            b_expert_router[layer], w_exp_gate[layer], w_exp_up[layer], w_exp_down[layer])
        x2 = _combine(ys, pos_tiles, x1, gcol, final_norm_w, tm=rtm, final=layer == depth - 1)
    return x2.reshape(batch, seq, d)
```

```python
import functools
import math

import jax
import jax.numpy as jnp
import numpy as np
from jax import lax
from jax.experimental import pallas as pl
from jax.experimental.pallas import tpu as pltpu

F32 = jnp.float32
BF16 = jnp.bfloat16

NORM_EPS = 1e-6
SSD_HEAD_DIM = 64
SSD_N_GROUPS = 8
SSD_D_STATE = 128
SSD_CONV_WIDTH = 4
SSD_CHUNK = 128
ATTN_HEAD_DIM = 128
DILATION_PATTERNS = ((128, 1), (512, 4), (2048, 16))
ATTN_HEADS_PER_GROUP = 4
ATTN_BLOCK = 128
ATTN_BLOCKS_PER_STEP = 4
N_EXPERT_GROUPS = 4
EXPERTS_PER_GROUP = 8
N_EXPERTS = N_EXPERT_GROUPS * EXPERTS_PER_GROUP
EXPERT_TOP_K = 2

LANES = 128
SUBLANES = 8
VMEM_LIMIT_BYTES = 56 * 1024 * 1024
ROW_DMA_UNROLL = 8

ROUTER_EXPERT_ROW0 = SUBLANES
ROUTER_ROWS = ROUTER_EXPERT_ROW0 + N_EXPERTS


def _params(*semantics):
    return pltpu.CompilerParams(dimension_semantics=semantics, vmem_limit_bytes=VMEM_LIMIT_BYTES)


def _split_bf16(v):
    hi = v.astype(BF16)
    lo = (v - hi.astype(F32)).astype(BF16)
    return hi, lo


def _silu(v):
    return v * (1.0 / (1.0 + jnp.exp(-v)))


def _regroup_weight_kernel(starts_ref, wt_hbm, out_ref, narrow_ref, buf_ref, nbuf_ref, sem, nsem,
                           *, layer, narrow_start, nh):
    i = pl.program_id(0)
    n_steps = pl.num_programs(0)
    tn = buf_ref.shape[1]

    def fetch(step, slot):
        rows = pl.ds(pl.multiple_of(starts_ref[step], SUBLANES), tn)
        return pltpu.make_async_copy(wt_hbm.at[layer, rows, :], buf_ref.at[slot], sem.at[slot])

    narrow = pltpu.make_async_copy(wt_hbm.at[layer, pl.ds(narrow_start, LANES), :], nbuf_ref, nsem)

    @pl.when(i == 0)
    def _():
        fetch(0, 0).start()
        narrow.start()

    @pl.when(i + 1 < n_steps)
    def _():
        fetch(i + 1, (i + 1) % 2).start()

    fetch(i, i % 2).wait()
    out_ref[...] = buf_ref[i % 2].T.astype(out_ref.dtype)

    @pl.when(i == n_steps - 1)
    def _():
        narrow.wait()
        lane = lax.broadcasted_iota(jnp.int32, narrow_ref.shape, 1)
        narrow_ref[...] = jnp.where(lane < nh, nbuf_ref[...].T, 0.0).astype(narrow_ref.dtype)


def _regroup_weight(w_stack, layer, starts, tn, narrow_start, nh):
    _, k, cols = w_stack.shape
    assert all(s % SUBLANES == 0 and s + tn <= cols for s in starts)
    assert narrow_start % SUBLANES == 0 and narrow_start + LANES <= cols and nh <= LANES
    wt = jnp.swapaxes(w_stack, 1, 2)
    grid_spec = pltpu.PrefetchScalarGridSpec(
        num_scalar_prefetch=1,
        grid=(len(starts),),
        in_specs=[pl.BlockSpec(memory_space=pl.ANY)],
        out_specs=[pl.BlockSpec((k, tn), lambda i, *_: (0, i)),
                   pl.BlockSpec((k, LANES), lambda i, *_: (0, 0))],
        scratch_shapes=[pltpu.VMEM((2, tn, k), F32), pltpu.VMEM((LANES, k), F32),
                        pltpu.SemaphoreType.DMA((2,)), pltpu.SemaphoreType.DMA(())],
    )
    return pl.pallas_call(
        functools.partial(_regroup_weight_kernel, layer=layer, narrow_start=narrow_start, nh=nh),
        grid_spec=grid_spec,
        out_shape=[jax.ShapeDtypeStruct((k, len(starts) * tn), BF16),
                   jax.ShapeDtypeStruct((k, LANES), BF16)],
        compiler_params=_params("arbitrary"),
        name="regroup_weight",
    )(jnp.asarray(starts, jnp.int32), wt)


QKV_PARTS = 3


def _in_proj_kernel(x_ref, nw_ref, w_ref, wdt_ref, proj_ref, a0_ref, a1_ref, a2_ref,
                    dt_ref, dtt_ref, h_ref, stage_ref, *, n_plain):
    j = pl.program_id(1)
    tm, tn = proj_ref.shape

    @pl.when(j == 0)
    def _():
        xf = x_ref[...]
        ms = jnp.mean(xf * xf, axis=-1, keepdims=True)
        h = (xf * lax.rsqrt(ms + NORM_EPS) * nw_ref[...]).astype(BF16)
        h_ref[...] = h
        nh = dt_ref.shape[1]
        dt_wide = jnp.dot(h, wdt_ref[...], preferred_element_type=F32)
        dt_ref[...] = dt_wide[:, :nh]
        dtt_ref[...] = dt_wide.T[:nh, :]

    @pl.when(j < n_plain)
    def _():
        proj_ref[...] = jnp.dot(h_ref[...], w_ref[...], preferred_element_type=F32).astype(BF16)

    for gi, a_ref in enumerate((a0_ref, a1_ref, a2_ref)):
        dil = DILATION_PATTERNS[gi][1]
        j0 = n_plain + QKV_PARTS * gi

        @pl.when((j >= j0) & (j < j0 + QKV_PARTS))
        def _(a_ref=a_ref, dil=dil):
            res = jnp.dot(h_ref[...], w_ref[...], preferred_element_type=F32)
            if dil == 1:
                a_ref[...] = res.astype(BF16)
            else:
                for s in range(tn // LANES):
                    stage_ref[s] = res[:, s * LANES:(s + 1) * LANES]
                for r in range(dil):
                    for s in range(tn // LANES):
                        c0 = r * tn + s * LANES
                        a_ref[:, c0:c0 + LANES] = stage_ref[
                            s, pl.ds(r, tm // dil, stride=dil), :].astype(BF16)


def _in_proj(x2, norm_w, w_main, w_dt, *, tm, tn, n_plain, nh):
    n, d = x2.shape
    n_blocks = w_main.shape[1] // tn
    assert n_blocks == n_plain + QKV_PARTS * len(DILATION_PATTERNS)

    def a_spec(gi):
        dil = DILATION_PATTERNS[gi][1]
        j0 = n_plain + QKV_PARTS * gi
        return pl.BlockSpec((tm // dil, dil * tn),
                            lambda i, j: (i, jnp.clip(j - j0, 0, QKV_PARTS - 1)))

    def a_shape(gi):
        dil = DILATION_PATTERNS[gi][1]
        return jax.ShapeDtypeStruct((n // dil, dil * QKV_PARTS * tn), BF16)

    return pl.pallas_call(
        functools.partial(_in_proj_kernel, n_plain=n_plain),
        grid=(n // tm, n_blocks),
        in_specs=[
            pl.BlockSpec((tm, d), lambda i, j: (i, 0)),
            pl.BlockSpec((1, d), lambda i, j: (0, 0)),
            pl.BlockSpec((d, tn), lambda i, j: (0, j)),
            pl.BlockSpec((d, LANES), lambda i, j: (0, 0)),
        ],
        out_specs=[
            pl.BlockSpec((tm, tn), lambda i, j: (i, jnp.minimum(j, n_plain - 1))),
            a_spec(0), a_spec(1), a_spec(2),
            pl.BlockSpec((tm, nh), lambda i, j: (i, 0)),
            pl.BlockSpec((nh, tm), lambda i, j: (0, i)),
        ],
        out_shape=[
            jax.ShapeDtypeStruct((n, n_plain * tn), BF16),
            a_shape(0), a_shape(1), a_shape(2),
            jax.ShapeDtypeStruct((n, nh), F32),
            jax.ShapeDtypeStruct((nh, n), F32),
        ],
        scratch_shapes=[pltpu.VMEM((tm, d), BF16), pltpu.VMEM((tn // LANES, tm, LANES), F32)],
        compiler_params=_params("arbitrary", "arbitrary"),
        name="in_proj",
    )(x2, norm_w.reshape(1, d), w_main, w_dt)


LOG2E = math.log2(math.e)
CONV_ROW_PITCH = 2


def _ssd_kernel(xbc_ref, z_ref, dt_ref, dtt_ref, cw_ref, cb_ref, dtb_ref, dtbt_ref, alog_ref,
                alogt_ref, dskip_ref, nw_ref, expand_ref, out_ref,
                xbuf_ref, state_ref, y_ref, *, n_heads, d_inner):
    L = SSD_CHUNK
    P = SSD_HEAD_DIM
    NS = SSD_D_STATE
    G = SSD_N_GROUPS
    R = n_heads // G
    GW = R * P
    W = SSD_CONV_WIDTH
    RP = CONV_ROW_PITCH
    n_slabs = xbuf_ref.shape[0]
    c = pl.program_id(1)

    def rows(first, count):
        return pl.ds(first * RP, count, stride=RP)

    @pl.when(c == 0)
    def _():
        state_ref[...] = jnp.zeros(state_ref.shape, F32)
        for s in range(n_slabs):
            xbuf_ref[s, rows(0, SUBLANES), :] = jnp.zeros((SUBLANES, LANES), F32)

    @pl.when(c > 0)
    def _():
        for s in range(n_slabs):
            xbuf_ref[s, rows(0, SUBLANES), :] = xbuf_ref[s, rows(L, SUBLANES), :]

    for s in range(n_slabs):
        xbuf_ref[s, rows(SUBLANES, L), :] = xbc_ref[:, s * LANES:(s + 1) * LANES].astype(F32)

    def conv(col0, width):
        slabs = []
        for s in range(col0 // LANES, (col0 + width) // LANES):
            acc = cb_ref[:, s * LANES:(s + 1) * LANES]
            for w in range(W):
                acc = acc + (cw_ref[w:w + 1, s * LANES:(s + 1) * LANES]
                             * xbuf_ref[s, rows(SUBLANES - (W - 1) + w, L), :])
            slabs.append(acc)
        return _silu(jnp.concatenate(slabs, axis=1))

    def softplus(v):
        return jnp.maximum(v, 0.0) + jnp.log1p(jnp.exp(-jnp.abs(v)))

    dt = softplus(dt_ref[...] + dtb_ref[...])
    dtt = softplus(dtt_ref[...] + dtbt_ref[...])
    da = dt * (-LOG2E * jnp.exp(alog_ref[...]))
    dat = dtt * (-LOG2E * jnp.exp(alogt_ref[...]))
    row = lax.broadcasted_iota(jnp.int32, (L, L), 0)
    col = lax.broadcasted_iota(jnp.int32, (L, L), 1)
    causal = row >= col
    tri = jnp.where(causal, 1.0, 0.0).astype(BF16)
    trit = jnp.where(row <= col, 1.0, 0.0).astype(BF16)

    def split3(v):
        a = v.astype(BF16)
        r1 = v - a.astype(F32)
        b = r1.astype(BF16)
        cc = (r1 - b.astype(F32)).astype(BF16)
        return a, b, cc

    a2 = sum(jnp.dot(tri, p, preferred_element_type=F32) for p in split3(da))
    a2t = sum(jnp.dot(p, trit, preferred_element_type=F32) for p in split3(dat))
    a2_last = a2[L - 1:L, :]

    expand = expand_ref[...]

    def expand_heads(v):
        hi, lo = _split_bf16(v)
        return jnp.dot(jnp.concatenate([hi, lo], axis=1), expand, preferred_element_type=F32)

    in_scale_e = expand_heads(dt * jnp.exp2(a2_last - a2))
    tail8 = jnp.concatenate([jnp.exp2(a2_last), dskip_ref[...],
                             jnp.zeros((SUBLANES - 2, n_heads), F32)], axis=0)
    out_e = expand_heads(jnp.concatenate([jnp.exp2(a2), tail8], axis=0))
    out_scale_e = out_e[0:L, :]
    chunk_decay_e = out_e[L:L + 1, :]
    dskip_e = out_e[L + 1:L + 2, :]

    first_head = lax.broadcasted_iota(jnp.int32, (L, LANES), 1) < P

    for g in range(G):
        x0 = g * GW
        xs = conv(x0, GW)
        bm = conv(d_inner + g * NS, NS).astype(BF16)
        cm = conv(d_inner + G * NS + g * NS, NS).astype(BF16)
        cb = lax.dot_general(cm, bm, (((1,), (1,)), ((), ())), preferred_element_type=F32)
        cb = jnp.where(causal, cb, 0.0)
        y_parts = []
        for pr in range(GW // LANES):
            wgts = []
            for q in range(LANES // P):
                h = g * R + pr * (LANES // P) + q
                seg = a2[:, h:h + 1] - a2t[h:h + 1, :]
                decay = jnp.exp2(jnp.minimum(seg, 0.0))
                wgts.append((cb * decay * dtt[h:h + 1, :]).astype(BF16))
            slab = xs[:, pr * LANES:(pr + 1) * LANES]
            rhs = jnp.concatenate([jnp.where(first_head, slab, 0.0).astype(BF16),
                                   jnp.where(first_head, 0.0, slab).astype(BF16)], axis=0)
            y_parts.append(jnp.dot(jnp.concatenate(wgts, axis=1), rhs, preferred_element_type=F32))
        y = jnp.concatenate(y_parts, axis=1) + xs * dskip_e[:, x0:x0 + GW]
        st = state_ref[:, x0:x0 + GW]
        y = y + (jnp.dot(cm, st.astype(BF16), preferred_element_type=F32)
                 * out_scale_e[:, x0:x0 + GW])
        y_ref[:, x0:x0 + GW] = y
        xin = (xs * in_scale_e[:, x0:x0 + GW]).astype(BF16)
        st_new = lax.dot_general(bm, xin, (((0,), (0,)), ((), ())), preferred_element_type=F32)
        state_ref[:, x0:x0 + GW] = st * chunk_decay_e[:, x0:x0 + GW] + st_new

    yz = y_ref[...] * _silu(z_ref[...].astype(F32))
    ms = jnp.mean(yz * yz, axis=-1, keepdims=True)
    out_ref[...] = (yz * lax.rsqrt(ms + NORM_EPS) * nw_ref[...]).astype(out_ref.dtype)


def _ssd(proj, dt, dtt, conv_w, conv_b, dt_bias, a_log, d_skip, norm_w, *, batch, seq, d_inner,
         xbc_block, z_block):
    n = batch * seq
    n_heads = dt.shape[1]
    L = SSD_CHUNK
    nc = seq // L
    conv_dim = conv_w.shape[1]
    expand = (np.arange(d_inner)[None, :] // SSD_HEAD_DIM == np.arange(n_heads)[:, None])
    expand = jnp.asarray(np.concatenate([expand, expand], axis=0), BF16)
    assert conv_dim % LANES == 0
    kern = functools.partial(_ssd_kernel, n_heads=n_heads, d_inner=d_inner)
    small = lambda shape: pl.BlockSpec(shape, lambda b, c: (0, 0))
    return pl.pallas_call(
        kern,
        grid=(batch, nc),
        in_specs=[
            pl.BlockSpec((L, conv_dim), lambda b, c: (b * nc + c, xbc_block)),
            pl.BlockSpec((L, d_inner), lambda b, c: (b * nc + c, z_block)),
            pl.BlockSpec((L, n_heads), lambda b, c: (b * nc + c, 0)),
            pl.BlockSpec((n_heads, L), lambda b, c: (0, b * nc + c)),
            small((SSD_CONV_WIDTH, conv_dim)),
            small((1, conv_dim)),
            small((1, n_heads)),
            small((n_heads, 1)),
            small((1, n_heads)),
            small((n_heads, 1)),
            small((1, n_heads)),
            small((1, d_inner)),
            small((2 * n_heads, d_inner)),
        ],
        out_specs=pl.BlockSpec((L, d_inner), lambda b, c: (b * nc + c, 0)),
        out_shape=jax.ShapeDtypeStruct((n, d_inner), BF16),
        scratch_shapes=[
            pltpu.VMEM((conv_dim // LANES, CONV_ROW_PITCH * (L + SUBLANES), LANES), F32),
            pltpu.VMEM((SSD_D_STATE, d_inner), F32),
            pltpu.VMEM((L, d_inner), F32),
        ],
        compiler_params=_params("arbitrary", "arbitrary"),
        name="ssd",
    )(proj, proj, dt, dtt, conv_w, conv_b.reshape(1, -1), dt_bias.reshape(1, -1),
      dt_bias.reshape(-1, 1), a_log.reshape(1, -1), a_log.reshape(-1, 1), d_skip.reshape(1, -1),
      norm_w.reshape(1, -1), expand)


def _attn_kernel(q_ref, k_ref, v_ref, o_ref, lse_ref, kp_ref, vp_ref, *, slopes, dilation, hops):
    nb = pl.program_id(2)
    BLK = ATTN_BLOCK
    E = ATTN_HEAD_DIM

    @pl.when(nb == 0)
    def _():
        kp_ref[...] = jnp.zeros(kp_ref.shape, kp_ref.dtype)
        vp_ref[...] = jnp.zeros(vp_ref.shape, vp_ref.dtype)

    scale = E ** -0.5
    nt = (((1,), (1,)), ((), ()))
    nblk = q_ref.shape[0] // BLK
    q = q_ref[...]
    kk = jnp.concatenate([kp_ref[...], k_ref[...]], axis=0)
    vv = jnp.concatenate([vp_ref[...], v_ref[...]], axis=0)
    units = [(j, h) for j in range(nblk) for h in range(len(slopes))]
    cols = lambda h: slice(h * E, (h + 1) * E)
    rows = lambda j: slice(j * BLK, (j + 1) * BLK)
    qi = lax.broadcasted_iota(jnp.int32, (BLK, BLK), 0)
    ki = lax.broadcasted_iota(jnp.int32, (BLK, BLK), 1)
    rel_cur = qi - ki
    rel_prev = rel_cur + BLK
    first = rel_prev <= jnp.where(nb > 0, hops, -1)
    later = rel_prev <= hops
    ok_cur = jnp.concatenate([rel_cur >= 0 for _ in units], axis=0)
    ok_prev = jnp.concatenate([first if j == 0 else later for j, _ in units], axis=0)
    dist_cur = (rel_cur * dilation).astype(F32)
    dist_prev = (rel_prev * dilation).astype(F32)
    bias_cur = jnp.concatenate([slopes[h] * dist_cur for _, h in units], axis=0)
    bias_prev = jnp.concatenate([slopes[h] * dist_prev for _, h in units], axis=0)
    s_cur = jnp.concatenate(
        [lax.dot_general(q[rows(j), cols(h)], kk[rows(j + 1), cols(h)], nt,
                         preferred_element_type=F32) for j, h in units], axis=0)
    s_prev = jnp.concatenate(
        [lax.dot_general(q[rows(j), cols(h)], kk[rows(j), cols(h)], nt,
                         preferred_element_type=F32) for j, h in units], axis=0)
    l_cur = jnp.where(ok_cur, s_cur * scale - bias_cur, -jnp.inf)
    l_prev = jnp.where(ok_prev, s_prev * scale - bias_prev, -jnp.inf)
    m = jnp.max(jnp.maximum(l_cur, l_prev), axis=-1, keepdims=True)
    p_cur = jnp.exp(l_cur - m)
    p_prev = jnp.exp(l_prev - m)
    den = jnp.sum(p_cur + p_prev, axis=-1, keepdims=True)
    p_cur = p_cur.astype(BF16)
    p_prev = p_prev.astype(BF16)
    inv = 1.0 / den
    lse = m + jnp.log(den)
    lane = lax.broadcasted_iota(jnp.int32, (BLK, LANES), 1)
    for j in range(nblk):
        lse_tile = jnp.zeros((BLK, LANES), F32)
        for h in range(len(slopes)):
            u = rows(units.index((j, h)))
            acc = (jnp.dot(p_cur[u], vv[rows(j + 1), cols(h)], preferred_element_type=F32)
                   + jnp.dot(p_prev[u], vv[rows(j), cols(h)], preferred_element_type=F32))
            o_ref[rows(j), cols(h)] = (acc * inv[u]).astype(o_ref.dtype)
            lse_tile = jnp.where(lane == h, lse[u], lse_tile)
        lse_ref[rows(j), :] = lse_tile
    kp_ref[...] = k_ref[rows(nblk - 1), :]
    vp_ref[...] = v_ref[rows(nblk - 1), :]


def _attn_group(qkv, gi, *, batch, seq):
    window, dilation = DILATION_PATTERNS[gi]
    hops = window // dilation
    n_heads_total = ATTN_HEADS_PER_GROUP * len(DILATION_PATTERNS)
    slopes = tuple(float(2.0 ** (-8.0 * (gi * ATTN_HEADS_PER_GROUP + h + 1) / n_heads_total))
                   for h in range(ATTN_HEADS_PER_GROUP))
    gw = ATTN_HEADS_PER_GROUP * ATTN_HEAD_DIM
    assert seq % (dilation * ATTN_BLOCK) == 0
    sub = seq // dilation
    rows = ATTN_BLOCK * math.gcd(sub // ATTN_BLOCK, ATTN_BLOCKS_PER_STEP)
    nb = sub // rows
    kern = functools.partial(_attn_kernel, slopes=slopes, dilation=dilation, hops=hops)

    def part(p):
        return pl.BlockSpec((rows, gw), lambda b, r, n: (b * nb + n, p * dilation + r))

    o, lse = pl.pallas_call(
        kern,
        grid=(batch, dilation, nb),
        in_specs=[part(0), part(1), part(2)],
        out_specs=[
            pl.BlockSpec((rows, gw), lambda b, r, n: (b * nb + n, r)),
            pl.BlockSpec((rows, LANES), lambda b, r, n: (b * nb + n, r)),
        ],
        out_shape=[
            jax.ShapeDtypeStruct((batch * sub, dilation * gw), BF16),
            jax.ShapeDtypeStruct((batch * sub, dilation * LANES), F32),
        ],
        scratch_shapes=[pltpu.VMEM((ATTN_BLOCK, gw), BF16), pltpu.VMEM((ATTN_BLOCK, gw), BF16)],
        compiler_params=_params("arbitrary", "arbitrary", "arbitrary"),
        name=f"attn_g{gi}",
    )(qkv, qkv, qkv)
    return o.reshape(batch * seq, gw), lse.reshape(batch * seq, LANES)


def _mix_kernel(yn_ref, o0_ref, o1_ref, o2_ref, l0_ref, l1_ref, l2_ref, gate_ref, x_ref,
                wssd_ref, wattn_ref, wout_ref, bg_ref, fnw_ref, wrh_ref, wrl_ref, rb_ref,
                x1_ref, hn_ref, eid_ref, gcol_ref, rank_ref, cnt_ref, carry_ref, *, d_model):
    i = pl.program_id(0)
    tm = x_ref.shape[0]
    E = ATTN_HEAD_DIM

    @pl.when(i == 0)
    def _():
        carry_ref[...] = jnp.zeros(carry_ref.shape, F32)

    y_ssd = jnp.dot(yn_ref[...], wssd_ref[...], preferred_element_type=F32)

    l0, l1, l2 = l0_ref[...], l1_ref[...], l2_ref[...]
    lm = jnp.maximum(jnp.maximum(l0, l1), l2)
    e0, e1, e2 = jnp.exp(l0 - lm), jnp.exp(l1 - lm), jnp.exp(l2 - lm)
    inv = 1.0 / (e0 + e1 + e2)
    parts = []
    for h in range(ATTN_HEADS_PER_GROUP):
        sl = slice(h * E, (h + 1) * E)
        parts.append((e0[:, h:h + 1] * inv[:, h:h + 1]) * o0_ref[:, sl].astype(F32)
                     + (e1[:, h:h + 1] * inv[:, h:h + 1]) * o1_ref[:, sl].astype(F32)
                     + (e2[:, h:h + 1] * inv[:, h:h + 1]) * o2_ref[:, sl].astype(F32))
    o = jnp.concatenate(parts, axis=-1).astype(BF16)
    y_attn = jnp.dot(o, wattn_ref[...], preferred_element_type=F32)

    graw = gate_ref[...].astype(F32) + bg_ref[...]
    gates = 1.0 / (1.0 + jnp.exp(-graw))
    merged = (gates[:, :d_model] * y_ssd + gates[:, d_model:] * y_attn).astype(BF16)
    x1 = x_ref[...] + jnp.dot(merged, wout_ref[...], preferred_element_type=F32)
    x1_ref[...] = x1

    ms = jnp.mean(x1 * x1, axis=-1, keepdims=True)
    hn = x1 * lax.rsqrt(ms + NORM_EPS) * fnw_ref[...]
    hn_ref[...] = hn

    hn_hi, hn_lo = _split_bf16(hn)
    nt = (((1,), (1,)), ((), ()))
    logits = (lax.dot_general(wrh_ref[...], hn_hi, nt, preferred_element_type=F32)
              + lax.dot_general(wrh_ref[...], hn_lo, nt, preferred_element_type=F32)
              + lax.dot_general(wrl_ref[...], hn_hi, nt, preferred_element_type=F32)
              + rb_ref[...])

    grow = lax.broadcasted_iota(jnp.int32, (SUBLANES, tm), 0)
    gl = jnp.where(grow < N_EXPERT_GROUPS, logits[0:SUBLANES, :], -jnp.inf)
    gmax = jnp.max(gl, axis=0, keepdims=True)
    gidx = jnp.min(jnp.where(gl == gmax, grow, N_EXPERT_GROUPS), axis=0, keepdims=True)
    group_gate = 1.0 / jnp.sum(jnp.exp(gl - gmax), axis=0, keepdims=True)

    in_group = jnp.zeros((EXPERTS_PER_GROUP, tm), F32)
    for g in range(N_EXPERT_GROUPS):
        r0 = ROUTER_EXPERT_ROW0 + g * EXPERTS_PER_GROUP
        in_group = jnp.where(gidx == g, logits[r0:r0 + EXPERTS_PER_GROUP, :], in_group)
    erow = lax.broadcasted_iota(jnp.int32, in_group.shape, 0)
    v1 = jnp.max(in_group, axis=0, keepdims=True)
    i1 = jnp.min(jnp.where(in_group == v1, erow, EXPERTS_PER_GROUP), axis=0, keepdims=True)
    rest = jnp.where(erow == i1, -jnp.inf, in_group)
    v2 = jnp.max(rest, axis=0, keepdims=True)
    i2 = jnp.min(jnp.where(rest == v2, erow, EXPERTS_PER_GROUP), axis=0, keepdims=True)
    t = jnp.exp(v2 - v1)
    g1 = group_gate / (1.0 + t)
    g2 = group_gate * t / (1.0 + t)
    eid1 = gidx * EXPERTS_PER_GROUP + i1
    eid2 = gidx * EXPERTS_PER_GROUP + i2
    slot = lax.broadcasted_iota(jnp.int32, (SUBLANES, tm), 0)
    eid_ref[...] = jnp.where(slot == 0, eid1, jnp.where(slot == 1, eid2, 0))

    grow8 = lax.broadcasted_iota(jnp.int32, (LANES, tm), 0)
    gt = jnp.where(grow8 == 0, g1, jnp.where(grow8 == 1, g2, 0.0))
    gcol_ref[...] = gt.T

    xrow = lax.broadcasted_iota(jnp.int32, (N_EXPERTS, tm), 0)
    oh1 = xrow == eid1
    oh2 = xrow == eid2
    oh = jnp.where(oh1 | oh2, 1.0, 0.0)
    ti = lax.broadcasted_iota(jnp.int32, (tm, tm), 0)
    tj = lax.broadcasted_iota(jnp.int32, (tm, tm), 1)
    before = jnp.where(ti < tj, 1.0, 0.0).astype(BF16)
    prior = jnp.dot(oh.astype(BF16), before, preferred_element_type=F32) + carry_ref[:, 0:1]
    r1 = jnp.sum(jnp.where(oh1, prior, 0.0), axis=0, keepdims=True)
    r2 = jnp.sum(jnp.where(oh2, prior, 0.0), axis=0, keepdims=True)
    rank_ref[...] = jnp.where(slot == 0, r1, jnp.where(slot == 1, r2, 0.0)).astype(jnp.int32)
    carry_ref[...] = carry_ref[...] + jnp.sum(oh, axis=1, keepdims=True)
    cnt_ref[...] = carry_ref[...].astype(jnp.int32)


def _mix(yn, outs, lses, proj, x2, w_ssd_out, w_attn_out, w_out, b_gate, ffn_norm_w, wr, rb, *,
         tm, gate_block):
    n, d = x2.shape
    aw = outs[0].shape[1]
    wrt = jnp.zeros((ROUTER_ROWS, d), F32)
    wrt = wrt.at[0:N_EXPERT_GROUPS].set(wr[0].T).at[ROUTER_EXPERT_ROW0:].set(wr[1].T)
    rbc = jnp.zeros((ROUTER_ROWS, 1), F32)
    rbc = rbc.at[0:N_EXPERT_GROUPS, 0].set(rb[0]).at[ROUTER_EXPERT_ROW0:, 0].set(rb[1])
    wrh, wrl = _split_bf16(wrt)
    row = lambda w: pl.BlockSpec((tm, w), lambda i: (i, 0))
    const = lambda shape: pl.BlockSpec(shape, lambda i: (0, 0), pipeline_mode=pl.Buffered(1))
    kern = functools.partial(_mix_kernel, d_model=d)
    return pl.pallas_call(
        kern,
        grid=(n // tm,),
        in_specs=[
            row(d), row(aw), row(aw), row(aw), row(LANES), row(LANES), row(LANES),
            pl.BlockSpec((tm, 2 * d), lambda i: (i, gate_block)),
            row(d),
            const((d, d)), const((aw, d)), const((d, d)), const((1, 2 * d)), const((1, d)),
            const((ROUTER_ROWS, d)), const((ROUTER_ROWS, d)), const((ROUTER_ROWS, 1)),
        ],
        out_specs=[
            row(d), row(d),
            pl.BlockSpec((SUBLANES, tm), lambda i: (0, i)),
            row(LANES),
            pl.BlockSpec((SUBLANES, tm), lambda i: (0, i)),
            pl.BlockSpec((N_EXPERTS, LANES), lambda i: (0, 0)),
        ],
        out_shape=[
            jax.ShapeDtypeStruct((n, d), F32),
            jax.ShapeDtypeStruct((n, d), F32),
            jax.ShapeDtypeStruct((SUBLANES, n), jnp.int32),
            jax.ShapeDtypeStruct((n, LANES), F32),
            jax.ShapeDtypeStruct((SUBLANES, n), jnp.int32),
            jax.ShapeDtypeStruct((N_EXPERTS, LANES), jnp.int32),
        ],
        scratch_shapes=[pltpu.VMEM((N_EXPERTS, LANES), F32)],
        compiler_params=_params("arbitrary"),
        name="mix",
    )(yn, outs[0], outs[1], outs[2], lses[0], lses[1], lses[2], proj, x2,
      w_ssd_out.astype(BF16), w_attn_out.astype(BF16), w_out.astype(BF16),
      b_gate.reshape(1, -1), ffn_norm_w.reshape(1, -1), wrh, wrl, rbc)


def _scatter_kernel(pad_start_ref, pad_len_ref, n_active_ref, pos_ref, hn_ref, xs_ref, zero_ref,
                    sem, zsem, *, tme, n_tiles):
    tm = hn_ref.shape[0]
    zrows = zero_ref.shape[0]

    @pl.when(pl.program_id(0) == 0)
    def _():
        zero_ref[...] = jnp.zeros(zero_ref.shape, zero_ref.dtype)

        def fills(act):
            def per_expert(e, carry):
                off = pad_start_ref[e]
                left = pad_len_ref[e]
                head = left & (SUBLANES - 1)
                for r in range(SUBLANES - 1):
                    @pl.when(r < head)
                    def _(r=r):
                        act(pltpu.make_async_copy(zero_ref.at[pl.ds(0, 1), :],
                                                  xs_ref.at[pl.ds(off + r, 1), :], zsem))

                off = off + head
                bit = zrows
                while bit >= SUBLANES:
                    take = left & bit

                    @pl.when(take != 0)
                    def _(off=off, bit=bit):
                        act(pltpu.make_async_copy(
                            zero_ref.at[pl.ds(0, bit), :],
                            xs_ref.at[pl.ds(pl.multiple_of(off, SUBLANES), bit), :], zsem))

                    off = off + take
                    bit //= 2
                return carry

            def per_tile(t, carry):
                @pl.when(t >= n_active_ref[0])
                def _():
                    for part in range(tme // zrows):
                        row0 = pl.multiple_of(t * tme + part * zrows, zrows)
                        act(pltpu.make_async_copy(zero_ref, xs_ref.at[pl.ds(row0, zrows), :], zsem))

                return carry

            lax.fori_loop(0, N_EXPERTS, per_expert, 0)
            lax.fori_loop(0, n_tiles, per_tile, 0)

        fills(lambda cp: cp.start())
        fills(lambda cp: cp.wait())

    def start(tb, carry):
        for u in range(ROW_DMA_UNROLL):
            t = tb * ROW_DMA_UNROLL + u
            for k in range(EXPERT_TOP_K):
                dst = pos_ref[0, 0, k * tm + t]
                pltpu.make_async_copy(hn_ref.at[pl.ds(t, 1), :], xs_ref.at[pl.ds(dst, 1), :],
                                      sem).start(priority=(u + k) % 2)
        return carry

    lax.fori_loop(0, tm // ROW_DMA_UNROLL, start, 0)
    for k in range(EXPERT_TOP_K):
        pltpu.make_async_copy(hn_ref, xs_ref.at[pl.ds(0, tm), :], sem).wait()


def _scatter(hn, pos_tiles, pad_start, pad_len, n_active, *, tm, tme, n_tiles):
    n, d = hn.shape
    assert tme % 2 == 0 and (tme // 2) & (tme // 2 - 1) == 0
    grid_spec = pltpu.PrefetchScalarGridSpec(
        num_scalar_prefetch=3,
        grid=(n // tm,),
        in_specs=[
            pl.BlockSpec((1, 1, EXPERT_TOP_K * tm), lambda i, *_: (i, 0, 0),
                         memory_space=pltpu.SMEM),
            pl.BlockSpec((tm, d), lambda i, *_: (i, 0)),
        ],
        out_specs=pl.BlockSpec(memory_space=pl.ANY),
        scratch_shapes=[pltpu.VMEM((tme // 2, d), hn.dtype), pltpu.SemaphoreType.DMA(()),
                        pltpu.SemaphoreType.DMA(())],
    )
    return pl.pallas_call(
        functools.partial(_scatter_kernel, tme=tme, n_tiles=n_tiles),
        grid_spec=grid_spec,
        out_shape=jax.ShapeDtypeStruct((n_tiles * tme, d), hn.dtype),
        compiler_params=_params("arbitrary"),
        name="scatter",
    )(pad_start, pad_len, n_active, pos_tiles, hn)


W_CHUNK_ROWS = 256
W_RING = 8


def _experts_kernel(te_ref, slot_ref, nxt_ref, lo_ref, hi_ref, na_ref, xs_ref, wg_hbm, wu_hbm,
                    wd_hbm, ys_ref, wres_ref, stage_ref, sem, *, n_gu, n_r, n_h):
    i = pl.program_id(0)
    n_chunks = 2 * n_gu + n_h * n_r
    ring, cr, f = stage_ref.shape
    d = n_gu * cr

    def stage_copy(src, c):
        return pltpu.make_async_copy(src, stage_ref.at[c % ring], sem.at[c % ring])

    def start_chunk(e, c):
        @pl.when(c < n_gu)
        def _():
            stage_copy(wg_hbm.at[e, pl.ds(pl.multiple_of(c * cr, cr), cr), :], c).start()

        @pl.when((c >= n_gu) & (c < 2 * n_gu))
        def _():
            stage_copy(wu_hbm.at[e, pl.ds(pl.multiple_of((c - n_gu) * cr, cr), cr), :], c).start()

        @pl.when(c >= 2 * n_gu)
        def _():
            q = c - 2 * n_gu
            h = q // n_r
            r = q - h * n_r
            stage_copy(wd_hbm.at[e, pl.ds(pl.multiple_of(r * cr, cr), cr),
                                 pl.ds(pl.multiple_of(h * f, f), f)], c).start()

    def start_first(e):
        def body(c, carry):
            start_chunk(e, c)
            return carry

        lax.fori_loop(0, ring, body, 0)

    def convert(e, lo, hi, dst_slot):
        def body(c, carry):
            stage_copy(wg_hbm.at[0, pl.ds(0, cr), :], c).wait()
            wres_ref[dst_slot, pl.ds(pl.multiple_of(c * cr, cr), cr), :] = (
                stage_ref[c % ring].astype(BF16))

            @pl.when(c + ring < n_chunks)
            def _():
                start_chunk(e, c + ring)

            return carry

        lax.fori_loop(lo, hi, body, 0)

    @pl.when(i == 0)
    def _():
        start_first(te_ref[0])
        convert(te_ref[0], 0, n_chunks, slot_ref[0])

    @pl.when(i < na_ref[0])
    def _():
        slot = slot_ref[i]
        nxt = nxt_ref[i]

        @pl.when((nxt >= 0) & (lo_ref[i] == 0))
        def _():
            start_first(nxt)

        xb = xs_ref[...].astype(BF16)
        hg = jnp.dot(xb, wres_ref[slot, 0:d, :], preferred_element_type=F32)
        hu = jnp.dot(xb, wres_ref[slot, d:2 * d, :], preferred_element_type=F32)
        hmid = (_silu(hg) * hu).astype(BF16)
        for h in range(n_h):
            r0 = 2 * d + h * n_r * cr
            ys_ref[:, h * f:(h + 1) * f] = jnp.dot(hmid, wres_ref[slot, r0:r0 + n_r * cr, :],
                                                    preferred_element_type=F32)

        @pl.when(nxt >= 0)
        def _():
            convert(nxt, lo_ref[i], hi_ref[i], 1 - slot)

    @pl.when(i >= na_ref[0])
    def _():
        ys_ref[...] = jnp.zeros(ys_ref.shape, F32)


def _experts(xs, tile_expert, n_active, cnt, starts, padded, w_g, w_u, w_d, *, tm):
    p, d = xs.shape
    ne, _, f = w_g.shape
    n_tiles = p // tm
    cr = W_CHUNK_ROWS
    assert d % cr == 0 and f % cr == 0 and d % f == 0
    n_gu, n_r, n_h = d // cr, f // cr, d // f
    n_chunks = 2 * n_gu + n_h * n_r

    ids = jnp.arange(ne, dtype=jnp.int32)
    present = cnt > 0
    later = jnp.where(present[None, :] & (ids[None, :] > ids[:, None]), ids[None, :], ne)
    nxt_e = jnp.min(later, axis=1)
    nxt_e = jnp.where(nxt_e >= ne, -1, nxt_e)
    run_e = jnp.cumsum(present.astype(jnp.int32)) - 1
    onehot = (tile_expert[:, None] == ids[None, :]).astype(jnp.int32)
    pick = lambda v: jnp.sum(onehot * v[None, :].astype(jnp.int32), axis=1)
    tiles = jnp.arange(n_tiles, dtype=jnp.int32)
    active = tiles < n_active[0]
    j = tiles - pick(starts) // tm
    k = jnp.maximum(pick(padded) // tm, 1)
    nxt_t = jnp.where(active, pick(nxt_e), -1).astype(jnp.int32)
    lo_t = jnp.where(active, (n_chunks * j) // k, 0).astype(jnp.int32)
    hi_t = jnp.where(active, (n_chunks * (j + 1)) // k, 0).astype(jnp.int32)
    slot_t = (pick(run_e) % 2).astype(jnp.int32)

    def rows(i, te, sl, nx, lo, hi, na):
        return (jnp.minimum(i, na[0] - 1), 0)

    grid_spec = pltpu.PrefetchScalarGridSpec(
        num_scalar_prefetch=6,
        grid=(n_tiles,),
        in_specs=[
            pl.BlockSpec((tm, d), rows),
            pl.BlockSpec(memory_space=pl.ANY),
            pl.BlockSpec(memory_space=pl.ANY),
            pl.BlockSpec(memory_space=pl.ANY),
        ],
        out_specs=pl.BlockSpec((tm, d), lambda i, *_: (i, 0)),
        scratch_shapes=[
            pltpu.VMEM((2, n_chunks * cr, f), BF16),
            pltpu.VMEM((W_RING, cr, f), F32),
            pltpu.SemaphoreType.DMA((W_RING,)),
        ],
    )
    return pl.pallas_call(
        functools.partial(_experts_kernel, n_gu=n_gu, n_r=n_r, n_h=n_h),
        grid_spec=grid_spec,
        out_shape=jax.ShapeDtypeStruct((p, d), F32),
        compiler_params=_params("arbitrary"),
        name="experts",
    )(tile_expert, slot_t, nxt_t, lo_t, hi_t, n_active, xs, w_g, w_u, w_d)


def _combine_kernel(pos_ref, pos_next_ref, ys_ref, x1_ref, gcol_ref, nw_ref, out_ref, buf_ref, sem,
                    *, final):
    i = pl.program_id(0)
    n_steps = pl.num_programs(0)
    tm = x1_ref.shape[0]

    def issue(p_ref, slot):
        def start(tb, carry):
            for u in range(ROW_DMA_UNROLL):
                t = tb * ROW_DMA_UNROLL + u
                for k in range(EXPERT_TOP_K):
                    src = p_ref[0, 0, k * tm + t]
                    pltpu.make_async_copy(ys_ref.at[pl.ds(src, 1), :],
                                          buf_ref.at[slot, k, pl.ds(t, 1), :],
                                          sem.at[slot]).start(priority=(u + k) % 2)
            return carry

        lax.fori_loop(0, tm // ROW_DMA_UNROLL, start, 0)

    slot = i % 2

    @pl.when(i == 0)
    def _():
        issue(pos_ref, 0)

    @pl.when(i + 1 < n_steps)
    def _():
        issue(pos_next_ref, 1 - slot)

    for k in range(EXPERT_TOP_K):
        pltpu.make_async_copy(ys_ref.at[pl.ds(0, tm), :], buf_ref.at[slot, k], sem.at[slot]).wait()
    g = gcol_ref[...]
    xo = x1_ref[...] + g[:, 0:1] * buf_ref[slot, 0] + g[:, 1:2] * buf_ref[slot, 1]
    if final:
        ms = jnp.mean(xo * xo, axis=-1, keepdims=True)
        xo = xo * lax.rsqrt(ms + NORM_EPS) * nw_ref[...]
    out_ref[...] = xo


def _combine(ys, pos_tiles, x1, gcol, norm_w, *, tm, final):
    n, d = x1.shape
    last = n // tm - 1
    return pl.pallas_call(
        functools.partial(_combine_kernel, final=final),
        grid=(n // tm,),
        in_specs=[
            pl.BlockSpec((1, 1, EXPERT_TOP_K * tm), lambda i: (i, 0, 0), memory_space=pltpu.SMEM),
            pl.BlockSpec((1, 1, EXPERT_TOP_K * tm), lambda i: (jnp.minimum(i + 1, last), 0, 0),
                         memory_space=pltpu.SMEM),
            pl.BlockSpec(memory_space=pl.ANY),
            pl.BlockSpec((tm, d), lambda i: (i, 0)),
            pl.BlockSpec((tm, LANES), lambda i: (i, 0)),
            pl.BlockSpec((1, d), lambda i: (0, 0)),
        ],
        out_specs=pl.BlockSpec((tm, d), lambda i: (i, 0)),
        out_shape=jax.ShapeDtypeStruct((n, d), F32),
        scratch_shapes=[pltpu.VMEM((2, EXPERT_TOP_K, tm, d), F32), pltpu.SemaphoreType.DMA((2,))],
        compiler_params=_params("arbitrary"),
        name="combine",
    )(pos_tiles, pos_tiles, ys, x1, gcol, norm_w.reshape(1, d))


def _tiles(n):
    return dict(proj_tm=math.gcd(n, 1024), mix_tm=math.gcd(n, 256), moe_tm=256,
                route_tm=math.gcd(n, 256))


def _layer(x2, batch, seq, layer, attn_norm_w, w_in_stack, b_gate, conv_w, conv_b, dt_bias, a_log,
           d_skip, ssd_norm_w, w_ssd_out, w_attn_out, w_out, ffn_norm_w, w_gr, b_gr, w_er, b_er,
           w_g, w_u, w_d):
    n, d = x2.shape
    n_heads = dt_bias.shape[0]
    d_inner = ssd_norm_w.shape[0]
    conv_dim = conv_w.shape[1]
    aw_total = ATTN_HEADS_PER_GROUP * len(DILATION_PATTERNS) * ATTN_HEAD_DIM
    gw = ATTN_HEADS_PER_GROUP * ATTN_HEAD_DIM
    tiles = _tiles(n)

    c_z, c_xbc, c_dt = d_inner, d_inner + conv_dim, d_inner + conv_dim + n_heads
    c_gate = c_dt + QKV_PARTS * aw_total
    segments = [(c_gate, 2 * d), (c_z, conv_dim), (0, d_inner)]
    segments += [(c_dt + p * aw_total + gi * gw, gw)
                 for gi in range(len(DILATION_PATTERNS)) for p in range(QKV_PARTS)]
    assert all(width % gw == 0 for _, width in segments)
    starts = tuple(start + b * gw for start, width in segments for b in range(width // gw))
    w_main, w_dt = _regroup_weight(w_in_stack, layer, starts, gw, c_xbc, n_heads)
    off_gate, off_xbc, off_z = 0, 2 * d, 2 * d + conv_dim
    plain_cols = off_z + d_inner
    assert off_xbc % conv_dim == 0 and off_z % d_inner == 0 and plain_cols % gw == 0

    proj, qkv0, qkv1, qkv2, dt, dtt = _in_proj(x2, attn_norm_w, w_main, w_dt, tm=tiles["proj_tm"],
                                               tn=gw, n_plain=plain_cols // gw, nh=n_heads)

    yn = _ssd(proj, dt, dtt, conv_w, conv_b, dt_bias, a_log, d_skip, ssd_norm_w, batch=batch,
              seq=seq, d_inner=d_inner, xbc_block=off_xbc // conv_dim, z_block=off_z // d_inner)

    outs, lses = [], []
    for gi, qkv in enumerate((qkv0, qkv1, qkv2)):
        o_g, lse_g = _attn_group(qkv, gi, batch=batch, seq=seq)
        outs.append(o_g)
        lses.append(lse_g)

    x1, hn, eid, gcol, rank, counts = _mix(
        yn, outs, lses, proj, x2, w_ssd_out, w_attn_out, w_out, b_gate, ffn_norm_w,
        (w_gr, w_er), (b_gr, b_er), tm=tiles["mix_tm"], gate_block=off_gate // (2 * d))

    tme = tiles["moe_tm"]
    cnt = counts[:, 0]
    padded = ((cnt + tme - 1) // tme) * tme
    ends = jnp.cumsum(padded)
    starts = ends - padded
    experts = jnp.arange(N_EXPERTS, dtype=jnp.int32)[:, None, None]
    pos = rank[:EXPERT_TOP_K] + jnp.sum(
        jnp.where(eid[None, :EXPERT_TOP_K] == experts, starts[:, None, None], 0), axis=0)
    n_tiles = EXPERT_TOP_K * n // tme + N_EXPERTS
    tile_start = jnp.arange(n_tiles, dtype=jnp.int32) * tme
    tile_expert = jnp.minimum(jnp.sum(ends[None, :] <= tile_start[:, None], axis=1),
                              N_EXPERTS - 1).astype(jnp.int32)
    n_active = (ends[-1:] // tme).astype(jnp.int32)

    rtm = tiles["route_tm"]
    pos_tiles = pos.reshape(EXPERT_TOP_K, n // rtm, rtm).transpose(1, 0, 2).reshape(
        n // rtm, 1, EXPERT_TOP_K * rtm)
    xs = _scatter(hn, pos_tiles, (starts + cnt).astype(jnp.int32), (padded - cnt).astype(jnp.int32),
                  n_active, tm=rtm, tme=tme, n_tiles=n_tiles)
    ys = _experts(xs, tile_expert, n_active, cnt, starts, padded, w_g, w_u, w_d, tm=tme)
    return ys, pos_tiles, x1, gcol, rtm


def kernel(x, attn_norm_w, w_in, b_gate, conv_w, conv_b, dt_bias, a_log, d_skip, ssd_norm_w,
           w_ssd_out, w_attn_out, w_out, ffn_norm_w, w_group_router, b_group_router,
           w_expert_router, b_expert_router, w_exp_gate, w_exp_up, w_exp_down, final_norm_w):
    batch, seq, d = x.shape
    depth = w_in.shape[0]
    x2 = x.reshape(batch * seq, d)
    for layer in range(depth):
        ys, pos_tiles, x1, gcol, rtm = _layer(
            x2, batch, seq, layer, attn_norm_w[layer], w_in, b_gate[layer], conv_w[layer],
            conv_b[layer], dt_bias[layer], a_log[layer], d_skip[layer], ssd_norm_w[layer],
            w_ssd_out[layer], w_attn_out[layer], w_out[layer], ffn_norm_w[layer],
            w_group_router[layer], b_group_router[layer], w_expert_router[layer],
            b_expert_router[layer], w_exp_gate[layer], w_exp_up[layer], w_exp_down[layer])
        x2 = _combine(ys, pos_tiles, x1, gcol, final_norm_w, tm=rtm, final=layer == depth - 1)
    return x2.reshape(batch, seq, d)
```

```python
import functools
import math

import jax
import jax.numpy as jnp
import numpy as np
from jax import lax
from jax.experimental import pallas as pl
from jax.experimental.pallas import tpu as pltpu

F32 = jnp.float32
BF16 = jnp.bfloat16

NORM_EPS = 1e-6
SSD_HEAD_DIM = 64
SSD_N_GROUPS = 8
SSD_D_STATE = 128
SSD_CONV_WIDTH = 4
SSD_CHUNK = 128
ATTN_HEAD_DIM = 128
DILATION_PATTERNS = ((128, 1), (512, 4), (2048, 16))
ATTN_HEADS_PER_GROUP = 4
ATTN_BLOCK = 128
ATTN_BLOCKS_PER_STEP = 4
N_EXPERT_GROUPS = 4
EXPERTS_PER_GROUP = 8
N_EXPERTS = N_EXPERT_GROUPS * EXPERTS_PER_GROUP
EXPERT_TOP_K = 2

LANES = 128
SUBLANES = 8
VMEM_LIMIT_BYTES = 56 * 1024 * 1024
ROW_DMA_UNROLL = 8

ROUTER_EXPERT_ROW0 = SUBLANES
ROUTER_ROWS = ROUTER_EXPERT_ROW0 + N_EXPERTS


def _params(*semantics):
    return pltpu.CompilerParams(dimension_semantics=semantics, vmem_limit_bytes=VMEM_LIMIT_BYTES)


def _split_bf16(v):
    hi = v.astype(BF16)
    lo = (v - hi.astype(F32)).astype(BF16)
    return hi, lo


def _silu(v):
    return v * (1.0 / (1.0 + jnp.exp(-v)))


def _regroup_weight_kernel(starts_ref, wt_hbm, out_ref, narrow_ref, buf_ref, nbuf_ref, sem, nsem,
                           *, layer, narrow_start, nh):
    i = pl.program_id(0)
    n_steps = pl.num_programs(0)
    tn = buf_ref.shape[1]

    def fetch(step, slot):
        rows = pl.ds(pl.multiple_of(starts_ref[step], SUBLANES), tn)
        return pltpu.make_async_copy(wt_hbm.at[layer, rows, :], buf_ref.at[slot], sem.at[slot])

    narrow = pltpu.make_async_copy(wt_hbm.at[layer, pl.ds(narrow_start, LANES), :], nbuf_ref, nsem)

    @pl.when(i == 0)
    def _():
        fetch(0, 0).start()
        narrow.start()

    @pl.when(i + 1 < n_steps)
    def _():
        fetch(i + 1, (i + 1) % 2).start()

    fetch(i, i % 2).wait()
    out_ref[...] = buf_ref[i % 2].T.astype(out_ref.dtype)

    @pl.when(i == n_steps - 1)
    def _():
        narrow.wait()
        lane = lax.broadcasted_iota(jnp.int32, narrow_ref.shape, 1)
        narrow_ref[...] = jnp.where(lane < nh, nbuf_ref[...].T, 0.0).astype(narrow_ref.dtype)


def _regroup_weight(w_stack, layer, starts, tn, narrow_start, nh):
    _, k, cols = w_stack.shape
    assert all(s % SUBLANES == 0 and s + tn <= cols for s in starts)
    assert narrow_start % SUBLANES == 0 and narrow_start + LANES <= cols and nh <= LANES
    wt = jnp.swapaxes(w_stack, 1, 2)
    grid_spec = pltpu.PrefetchScalarGridSpec(
        num_scalar_prefetch=1,
        grid=(len(starts),),
        in_specs=[pl.BlockSpec(memory_space=pl.ANY)],
        out_specs=[pl.BlockSpec((k, tn), lambda i, *_: (0, i)),
                   pl.BlockSpec((k, LANES), lambda i, *_: (0, 0))],
        scratch_shapes=[pltpu.VMEM((2, tn, k), F32), pltpu.VMEM((LANES, k), F32),
                        pltpu.SemaphoreType.DMA((2,)), pltpu.SemaphoreType.DMA(())],
    )
    return pl.pallas_call(
        functools.partial(_regroup_weight_kernel, layer=layer, narrow_start=narrow_start, nh=nh),
        grid_spec=grid_spec,
        out_shape=[jax.ShapeDtypeStruct((k, len(starts) * tn), BF16),
                   jax.ShapeDtypeStruct((k, LANES), BF16)],
        compiler_params=_params("arbitrary"),
        name="regroup_weight",
    )(jnp.asarray(starts, jnp.int32), wt)


QKV_PARTS = 3


def _in_proj_kernel(x_ref, nw_ref, wp_ref, w_ref, wdt_ref, proj_ref, a0_ref, a1_ref, a2_ref,
                    dt_ref, dtt_ref, h_ref, stage_ref, *, n_plain):
    j = pl.program_id(1)
    tm = proj_ref.shape[0]
    tn = w_ref.shape[1]

    @pl.when(j == 0)
    def _():
        xf = x_ref[...]
        ms = jnp.mean(xf * xf, axis=-1, keepdims=True)
        h = (xf * lax.rsqrt(ms + NORM_EPS) * nw_ref[...]).astype(BF16)
        h_ref[...] = h
        nh = dt_ref.shape[1]
        dt_wide = jnp.dot(h, wdt_ref[...], preferred_element_type=F32)
        dt_ref[...] = dt_wide[:, :nh]
        dtt_ref[...] = dt_wide.T[:nh, :]

    @pl.when(j < n_plain)
    def _():
        proj_ref[...] = jnp.dot(h_ref[...], wp_ref[...], preferred_element_type=F32).astype(BF16)

    for gi, a_ref in enumerate((a0_ref, a1_ref, a2_ref)):
        dil = DILATION_PATTERNS[gi][1]
        j0 = n_plain + QKV_PARTS * gi

        @pl.when((j >= j0) & (j < j0 + QKV_PARTS))
        def _(a_ref=a_ref, dil=dil):
            res = jnp.dot(h_ref[...], w_ref[...], preferred_element_type=F32)
            if dil == 1:
                a_ref[...] = res.astype(BF16)
            else:
                for s in range(tn // LANES):
                    stage_ref[s] = res[:, s * LANES:(s + 1) * LANES]
                for r in range(dil):
                    for s in range(tn // LANES):
                        c0 = r * tn + s * LANES
                        a_ref[:, c0:c0 + LANES] = stage_ref[
                            s, pl.ds(r, tm // dil, stride=dil), :].astype(BF16)


def _in_proj(x2, norm_w, w_main, w_dt, *, tm, tn, tp, plain_cols, nh):
    n, d = x2.shape
    assert plain_cols % tp == 0 and plain_cols % tn == 0
    n_plain = plain_cols // tp
    qkv_block0 = plain_cols // tn
    n_blocks = n_plain + QKV_PARTS * len(DILATION_PATTERNS)
    assert w_main.shape[1] == plain_cols + QKV_PARTS * len(DILATION_PATTERNS) * tn

    def a_spec(gi):
        dil = DILATION_PATTERNS[gi][1]
        j0 = n_plain + QKV_PARTS * gi
        return pl.BlockSpec((tm // dil, dil * tn),
                            lambda i, j: (i, jnp.clip(j - j0, 0, QKV_PARTS - 1)))

    def a_shape(gi):
        dil = DILATION_PATTERNS[gi][1]
        return jax.ShapeDtypeStruct((n // dil, dil * QKV_PARTS * tn), BF16)

    return pl.pallas_call(
        functools.partial(_in_proj_kernel, n_plain=n_plain),
        grid=(n // tm, n_blocks),
        in_specs=[
            pl.BlockSpec((tm, d), lambda i, j: (i, 0)),
            pl.BlockSpec((1, d), lambda i, j: (0, 0)),
            pl.BlockSpec((d, tp), lambda i, j: (0, jnp.minimum(j, n_plain - 1))),
            pl.BlockSpec((d, tn), lambda i, j: (0, qkv_block0 + jnp.maximum(j - n_plain, 0))),
            pl.BlockSpec((d, LANES), lambda i, j: (0, 0)),
        ],
        out_specs=[
            pl.BlockSpec((tm, tp), lambda i, j: (i, jnp.minimum(j, n_plain - 1))),
            a_spec(0), a_spec(1), a_spec(2),
            pl.BlockSpec((tm, nh), lambda i, j: (i, 0)),
            pl.BlockSpec((nh, tm), lambda i, j: (0, i)),
        ],
        out_shape=[
            jax.ShapeDtypeStruct((n, plain_cols), BF16),
            a_shape(0), a_shape(1), a_shape(2),
            jax.ShapeDtypeStruct((n, nh), F32),
            jax.ShapeDtypeStruct((nh, n), F32),
        ],
        scratch_shapes=[pltpu.VMEM((tm, d), BF16), pltpu.VMEM((tn // LANES, tm, LANES), F32)],
        compiler_params=_params("arbitrary", "arbitrary"),
        name="in_proj",
    )(x2, norm_w.reshape(1, d), w_main, w_main, w_dt)


LOG2E = math.log2(math.e)
CONV_ROW_PITCH = 2


def _ssd_kernel(xbc_ref, z_ref, dt_ref, dtt_ref, cw_ref, cb_ref, dtb_ref, dtbt_ref, alog_ref,
                alogt_ref, dskip_ref, nw_ref, expand_ref, out_ref,
                xbuf_ref, state_ref, y_ref, *, n_heads, d_inner):
    L = SSD_CHUNK
    P = SSD_HEAD_DIM
    NS = SSD_D_STATE
    G = SSD_N_GROUPS
    R = n_heads // G
    GW = R * P
    W = SSD_CONV_WIDTH
    RP = CONV_ROW_PITCH
    n_slabs = xbuf_ref.shape[0]
    c = pl.program_id(1)

    def rows(first, count):
        return pl.ds(first * RP, count, stride=RP)

    @pl.when(c == 0)
    def _():
        state_ref[...] = jnp.zeros(state_ref.shape, F32)
        for s in range(n_slabs):
            xbuf_ref[s, rows(0, SUBLANES), :] = jnp.zeros((SUBLANES, LANES), F32)

    @pl.when(c > 0)
    def _():
        for s in range(n_slabs):
            xbuf_ref[s, rows(0, SUBLANES), :] = xbuf_ref[s, rows(L, SUBLANES), :]

    for s in range(n_slabs):
        xbuf_ref[s, rows(SUBLANES, L), :] = xbc_ref[:, s * LANES:(s + 1) * LANES].astype(F32)

    def conv(col0, width):
        slabs = []
        for s in range(col0 // LANES, (col0 + width) // LANES):
            acc = cb_ref[:, s * LANES:(s + 1) * LANES]
            for w in range(W):
                acc = acc + (cw_ref[w:w + 1, s * LANES:(s + 1) * LANES]
                             * xbuf_ref[s, rows(SUBLANES - (W - 1) + w, L), :])
            slabs.append(acc)
        return _silu(jnp.concatenate(slabs, axis=1))

    def softplus(v):
        return jnp.maximum(v, 0.0) + jnp.log1p(jnp.exp(-jnp.abs(v)))

    dt = softplus(dt_ref[...] + dtb_ref[...])
    dtt = softplus(dtt_ref[...] + dtbt_ref[...])
    da = dt * (-LOG2E * jnp.exp(alog_ref[...]))
    dat = dtt * (-LOG2E * jnp.exp(alogt_ref[...]))
    row = lax.broadcasted_iota(jnp.int32, (L, L), 0)
    col = lax.broadcasted_iota(jnp.int32, (L, L), 1)
    causal = row >= col
    tri = jnp.where(causal, 1.0, 0.0).astype(BF16)
    trit = jnp.where(row <= col, 1.0, 0.0).astype(BF16)

    def split3(v):
        a = v.astype(BF16)
        r1 = v - a.astype(F32)
        b = r1.astype(BF16)
        cc = (r1 - b.astype(F32)).astype(BF16)
        return a, b, cc

    a2 = sum(jnp.dot(tri, p, preferred_element_type=F32) for p in split3(da))
    a2t = sum(jnp.dot(p, trit, preferred_element_type=F32) for p in split3(dat))
    a2_last = a2[L - 1:L, :]

    expand = expand_ref[...]

    def expand_heads(v):
        hi, lo = _split_bf16(v)
        return jnp.dot(jnp.concatenate([hi, lo], axis=1), expand, preferred_element_type=F32)

    in_scale_e = expand_heads(dt * jnp.exp2(a2_last - a2))
    tail8 = jnp.concatenate([jnp.exp2(a2_last), dskip_ref[...],
                             jnp.zeros((SUBLANES - 2, n_heads), F32)], axis=0)
    out_e = expand_heads(jnp.concatenate([jnp.exp2(a2), tail8], axis=0))
    out_scale_e = out_e[0:L, :]
    chunk_decay_e = out_e[L:L + 1, :]
    dskip_e = out_e[L + 1:L + 2, :]

    first_head = lax.broadcasted_iota(jnp.int32, (L, LANES), 1) < P

    for g in range(G):
        x0 = g * GW
        xs = conv(x0, GW)
        bm = conv(d_inner + g * NS, NS).astype(BF16)
        cm = conv(d_inner + G * NS + g * NS, NS).astype(BF16)
        cb = lax.dot_general(cm, bm, (((1,), (1,)), ((), ())), preferred_element_type=F32)
        cb = jnp.where(causal, cb, 0.0)
        y_parts = []
        for pr in range(GW // LANES):
            wgts = []
            for q in range(LANES // P):
                h = g * R + pr * (LANES // P) + q
                seg = a2[:, h:h + 1] - a2t[h:h + 1, :]
                decay = jnp.exp2(jnp.minimum(seg, 0.0))
                wgts.append((cb * decay * dtt[h:h + 1, :]).astype(BF16))
            slab = xs[:, pr * LANES:(pr + 1) * LANES]
            rhs = jnp.concatenate([jnp.where(first_head, slab, 0.0).astype(BF16),
                                   jnp.where(first_head, 0.0, slab).astype(BF16)], axis=0)
            y_parts.append(jnp.dot(jnp.concatenate(wgts, axis=1), rhs, preferred_element_type=F32))
        y = jnp.concatenate(y_parts, axis=1) + xs * dskip_e[:, x0:x0 + GW]
        st = state_ref[:, x0:x0 + GW]
        y = y + (jnp.dot(cm, st.astype(BF16), preferred_element_type=F32)
                 * out_scale_e[:, x0:x0 + GW])
        y_ref[:, x0:x0 + GW] = y
        xin = (xs * in_scale_e[:, x0:x0 + GW]).astype(BF16)
        st_new = lax.dot_general(bm, xin, (((0,), (0,)), ((), ())), preferred_element_type=F32)
        state_ref[:, x0:x0 + GW] = st * chunk_decay_e[:, x0:x0 + GW] + st_new

    yz = y_ref[...] * _silu(z_ref[...].astype(F32))
    ms = jnp.mean(yz * yz, axis=-1, keepdims=True)
    out_ref[...] = (yz * lax.rsqrt(ms + NORM_EPS) * nw_ref[...]).astype(out_ref.dtype)


def _ssd(proj, dt, dtt, conv_w, conv_b, dt_bias, a_log, d_skip, norm_w, *, batch, seq, d_inner,
         xbc_block, z_block):
    n = batch * seq
    n_heads = dt.shape[1]
    L = SSD_CHUNK
    nc = seq // L
    conv_dim = conv_w.shape[1]
    expand = (np.arange(d_inner)[None, :] // SSD_HEAD_DIM == np.arange(n_heads)[:, None])
    expand = jnp.asarray(np.concatenate([expand, expand], axis=0), BF16)
    assert conv_dim % LANES == 0
    kern = functools.partial(_ssd_kernel, n_heads=n_heads, d_inner=d_inner)
    small = lambda shape: pl.BlockSpec(shape, lambda b, c: (0, 0))
    return pl.pallas_call(
        kern,
        grid=(batch, nc),
        in_specs=[
            pl.BlockSpec((L, conv_dim), lambda b, c: (b * nc + c, xbc_block)),
            pl.BlockSpec((L, d_inner), lambda b, c: (b * nc + c, z_block)),
            pl.BlockSpec((L, n_heads), lambda b, c: (b * nc + c, 0)),
            pl.BlockSpec((n_heads, L), lambda b, c: (0, b * nc + c)),
            small((SSD_CONV_WIDTH, conv_dim)),
            small((1, conv_dim)),
            small((1, n_heads)),
            small((n_heads, 1)),
            small((1, n_heads)),
            small((n_heads, 1)),
            small((1, n_heads)),
            small((1, d_inner)),
            small((2 * n_heads, d_inner)),
        ],
        out_specs=pl.BlockSpec((L, d_inner), lambda b, c: (b * nc + c, 0)),
        out_shape=jax.ShapeDtypeStruct((n, d_inner), BF16),
        scratch_shapes=[
            pltpu.VMEM((conv_dim // LANES, CONV_ROW_PITCH * (L + SUBLANES), LANES), F32),
            pltpu.VMEM((SSD_D_STATE, d_inner), F32),
            pltpu.VMEM((L, d_inner), F32),
        ],
        compiler_params=_params("arbitrary", "arbitrary"),
        name="ssd",
    )(proj, proj, dt, dtt, conv_w, conv_b.reshape(1, -1), dt_bias.reshape(1, -1),
      dt_bias.reshape(-1, 1), a_log.reshape(1, -1), a_log.reshape(-1, 1), d_skip.reshape(1, -1),
      norm_w.reshape(1, -1), expand)


def _attn_kernel(q_ref, k_ref, v_ref, o_ref, lse_ref, kp_ref, vp_ref, *, slopes, dilation, hops):
    nb = pl.program_id(2)
    BLK = ATTN_BLOCK
    E = ATTN_HEAD_DIM

    @pl.when(nb == 0)
    def _():
        kp_ref[...] = jnp.zeros(kp_ref.shape, kp_ref.dtype)
        vp_ref[...] = jnp.zeros(vp_ref.shape, vp_ref.dtype)

    scale = E ** -0.5
    nt = (((1,), (1,)), ((), ()))
    nblk = q_ref.shape[0] // BLK
    q = q_ref[...]
    kk = jnp.concatenate([kp_ref[...], k_ref[...]], axis=0)
    vv = jnp.concatenate([vp_ref[...], v_ref[...]], axis=0)
    units = [(j, h) for j in range(nblk) for h in range(len(slopes))]
    cols = lambda h: slice(h * E, (h + 1) * E)
    rows = lambda j: slice(j * BLK, (j + 1) * BLK)
    qi = lax.broadcasted_iota(jnp.int32, (BLK, BLK), 0)
    ki = lax.broadcasted_iota(jnp.int32, (BLK, BLK), 1)
    rel_cur = qi - ki
    rel_prev = rel_cur + BLK
    first = rel_prev <= jnp.where(nb > 0, hops, -1)
    later = rel_prev <= hops
    ok_cur = jnp.concatenate([rel_cur >= 0 for _ in units], axis=0)
    ok_prev = jnp.concatenate([first if j == 0 else later for j, _ in units], axis=0)
    dist_cur = (rel_cur * dilation).astype(F32)
    dist_prev = (rel_prev * dilation).astype(F32)
    bias_cur = jnp.concatenate([slopes[h] * dist_cur for _, h in units], axis=0)
    bias_prev = jnp.concatenate([slopes[h] * dist_prev for _, h in units], axis=0)
    s_cur = jnp.concatenate(
        [lax.dot_general(q[rows(j), cols(h)], kk[rows(j + 1), cols(h)], nt,
                         preferred_element_type=F32) for j, h in units], axis=0)
    s_prev = jnp.concatenate(
        [lax.dot_general(q[rows(j), cols(h)], kk[rows(j), cols(h)], nt,
                         preferred_element_type=F32) for j, h in units], axis=0)
    l_cur = jnp.where(ok_cur, s_cur * scale - bias_cur, -jnp.inf)
    l_prev = jnp.where(ok_prev, s_prev * scale - bias_prev, -jnp.inf)
    m = jnp.max(jnp.maximum(l_cur, l_prev), axis=-1, keepdims=True)
    p_cur = jnp.exp(l_cur - m)
    p_prev = jnp.exp(l_prev - m)
    den = jnp.sum(p_cur + p_prev, axis=-1, keepdims=True)
    p_cur = p_cur.astype(BF16)
    p_prev = p_prev.astype(BF16)
    inv = 1.0 / den
    lse = m + jnp.log(den)
    lane = lax.broadcasted_iota(jnp.int32, (BLK, LANES), 1)
    for j in range(nblk):
        lse_tile = jnp.zeros((BLK, LANES), F32)
        for h in range(len(slopes)):
            u = rows(units.index((j, h)))
            acc = (jnp.dot(p_cur[u], vv[rows(j + 1), cols(h)], preferred_element_type=F32)
                   + jnp.dot(p_prev[u], vv[rows(j), cols(h)], preferred_element_type=F32))
            o_ref[rows(j), cols(h)] = (acc * inv[u]).astype(o_ref.dtype)
            lse_tile = jnp.where(lane == h, lse[u], lse_tile)
        lse_ref[rows(j), :] = lse_tile
    kp_ref[...] = k_ref[rows(nblk - 1), :]
    vp_ref[...] = v_ref[rows(nblk - 1), :]


def _attn_group(qkv, gi, *, batch, seq):
    window, dilation = DILATION_PATTERNS[gi]
    hops = window // dilation
    n_heads_total = ATTN_HEADS_PER_GROUP * len(DILATION_PATTERNS)
    slopes = tuple(float(2.0 ** (-8.0 * (gi * ATTN_HEADS_PER_GROUP + h + 1) / n_heads_total))
                   for h in range(ATTN_HEADS_PER_GROUP))
    gw = ATTN_HEADS_PER_GROUP * ATTN_HEAD_DIM
    assert seq % (dilation * ATTN_BLOCK) == 0
    sub = seq // dilation
    rows = ATTN_BLOCK * math.gcd(sub // ATTN_BLOCK, ATTN_BLOCKS_PER_STEP)
    nb = sub // rows
    kern = functools.partial(_attn_kernel, slopes=slopes, dilation=dilation, hops=hops)

    def part(p):
        return pl.BlockSpec((rows, gw), lambda b, r, n: (b * nb + n, p * dilation + r))

    o, lse = pl.pallas_call(
        kern,
        grid=(batch, dilation, nb),
        in_specs=[part(0), part(1), part(2)],
        out_specs=[
            pl.BlockSpec((rows, gw), lambda b, r, n: (b * nb + n, r)),
            pl.BlockSpec((rows, LANES), lambda b, r, n: (b * nb + n, r)),
        ],
        out_shape=[
            jax.ShapeDtypeStruct((batch * sub, dilation * gw), BF16),
            jax.ShapeDtypeStruct((batch * sub, dilation * LANES), F32),
        ],
        scratch_shapes=[pltpu.VMEM((ATTN_BLOCK, gw), BF16), pltpu.VMEM((ATTN_BLOCK, gw), BF16)],
        compiler_params=_params("arbitrary", "arbitrary", "arbitrary"),
        name=f"attn_g{gi}",
    )(qkv, qkv, qkv)
    return o.reshape(batch * seq, gw), lse.reshape(batch * seq, LANES)


def _merge_kernel(yn_ref, o0_ref, o1_ref, o2_ref, l0_ref, l1_ref, l2_ref, gate_ref,
                  wssd_ref, wattn_ref, bg_ref, merged_ref, *, d_model):
    E = ATTN_HEAD_DIM
    y_ssd = jnp.dot(yn_ref[...], wssd_ref[...], preferred_element_type=F32)

    l0, l1, l2 = l0_ref[...], l1_ref[...], l2_ref[...]
    lm = jnp.maximum(jnp.maximum(l0, l1), l2)
    e0, e1, e2 = jnp.exp(l0 - lm), jnp.exp(l1 - lm), jnp.exp(l2 - lm)
    inv = 1.0 / (e0 + e1 + e2)
    parts = []
    for h in range(ATTN_HEADS_PER_GROUP):
        sl = slice(h * E, (h + 1) * E)
        parts.append((e0[:, h:h + 1] * inv[:, h:h + 1]) * o0_ref[:, sl].astype(F32)
                     + (e1[:, h:h + 1] * inv[:, h:h + 1]) * o1_ref[:, sl].astype(F32)
                     + (e2[:, h:h + 1] * inv[:, h:h + 1]) * o2_ref[:, sl].astype(F32))
    o = jnp.concatenate(parts, axis=-1).astype(BF16)
    y_attn = jnp.dot(o, wattn_ref[...], preferred_element_type=F32)

    graw = gate_ref[...].astype(F32) + bg_ref[...]
    gates = 1.0 / (1.0 + jnp.exp(-graw))
    merged_ref[...] = (gates[:, :d_model] * y_ssd + gates[:, d_model:] * y_attn).astype(BF16)


def _merge(yn, outs, lses, proj, w_ssd_out, w_attn_out, b_gate, *, tm, gate_block):
    n, d = yn.shape
    aw = outs[0].shape[1]
    row = lambda w: pl.BlockSpec((tm, w), lambda i: (i, 0))
    const = lambda shape: pl.BlockSpec(shape, lambda i: (0, 0), pipeline_mode=pl.Buffered(1))
    return pl.pallas_call(
        functools.partial(_merge_kernel, d_model=d),
        grid=(n // tm,),
        in_specs=[
            row(d), row(aw), row(aw), row(aw), row(LANES), row(LANES), row(LANES),
            pl.BlockSpec((tm, 2 * d), lambda i: (i, gate_block)),
            const((d, d)), const((aw, d)), const((1, 2 * d)),
        ],
        out_specs=row(d),
        out_shape=jax.ShapeDtypeStruct((n, d), BF16),
        compiler_params=_params("arbitrary"),
        name="merge",
    )(yn, outs[0], outs[1], outs[2], lses[0], lses[1], lses[2], proj,
      w_ssd_out.astype(BF16), w_attn_out.astype(BF16), b_gate.reshape(1, -1))


def _route_kernel(merged_ref, x_ref, wout_ref, fnw_ref, wrh_ref, wrl_ref, rb_ref,
                  x1_ref, eid_ref, gcol_ref, rank_ref, cnt_ref, carry_ref):
    i = pl.program_id(0)
    tm = x_ref.shape[0]

    @pl.when(i == 0)
    def _():
        carry_ref[...] = jnp.zeros(carry_ref.shape, F32)

    x1 = x_ref[...] + jnp.dot(merged_ref[...], wout_ref[...], preferred_element_type=F32)
    x1_ref[...] = x1

    ms = jnp.mean(x1 * x1, axis=-1, keepdims=True)
    hn = x1 * lax.rsqrt(ms + NORM_EPS) * fnw_ref[...]

    hn_hi, hn_lo = _split_bf16(hn)
    nt = (((1,), (1,)), ((), ()))
    logits = (lax.dot_general(wrh_ref[...], hn_hi, nt, preferred_element_type=F32)
              + lax.dot_general(wrh_ref[...], hn_lo, nt, preferred_element_type=F32)
              + lax.dot_general(wrl_ref[...], hn_hi, nt, preferred_element_type=F32)
              + rb_ref[...])

    grow = lax.broadcasted_iota(jnp.int32, (SUBLANES, tm), 0)
    gl = jnp.where(grow < N_EXPERT_GROUPS, logits[0:SUBLANES, :], -jnp.inf)
    gmax = jnp.max(gl, axis=0, keepdims=True)
    gidx = jnp.min(jnp.where(gl == gmax, grow, N_EXPERT_GROUPS), axis=0, keepdims=True)
    group_gate = 1.0 / jnp.sum(jnp.exp(gl - gmax), axis=0, keepdims=True)

    in_group = jnp.zeros((EXPERTS_PER_GROUP, tm), F32)
    for g in range(N_EXPERT_GROUPS):
        r0 = ROUTER_EXPERT_ROW0 + g * EXPERTS_PER_GROUP
        in_group = jnp.where(gidx == g, logits[r0:r0 + EXPERTS_PER_GROUP, :], in_group)
    erow = lax.broadcasted_iota(jnp.int32, in_group.shape, 0)
    v1 = jnp.max(in_group, axis=0, keepdims=True)
    i1 = jnp.min(jnp.where(in_group == v1, erow, EXPERTS_PER_GROUP), axis=0, keepdims=True)
    rest = jnp.where(erow == i1, -jnp.inf, in_group)
    v2 = jnp.max(rest, axis=0, keepdims=True)
    i2 = jnp.min(jnp.where(rest == v2, erow, EXPERTS_PER_GROUP), axis=0, keepdims=True)
    t = jnp.exp(v2 - v1)
    g1 = group_gate / (1.0 + t)
    g2 = group_gate * t / (1.0 + t)
    eid1 = gidx * EXPERTS_PER_GROUP + i1
    eid2 = gidx * EXPERTS_PER_GROUP + i2
    slot = lax.broadcasted_iota(jnp.int32, (SUBLANES, tm), 0)
    eid_ref[...] = jnp.where(slot == 0, eid1, jnp.where(slot == 1, eid2, 0))

    grow8 = lax.broadcasted_iota(jnp.int32, (LANES, tm), 0)
    gt = jnp.where(grow8 == 0, g1, jnp.where(grow8 == 1, g2, 0.0))
    gcol_ref[...] = gt.T

    xrow = lax.broadcasted_iota(jnp.int32, (N_EXPERTS, tm), 0)
    oh1 = xrow == eid1
    oh2 = xrow == eid2
    oh = jnp.where(oh1 | oh2, 1.0, 0.0)
    ti = lax.broadcasted_iota(jnp.int32, (tm, tm), 0)
    tj = lax.broadcasted_iota(jnp.int32, (tm, tm), 1)
    before = jnp.where(ti < tj, 1.0, 0.0).astype(BF16)
    prior = jnp.dot(oh.astype(BF16), before, preferred_element_type=F32) + carry_ref[:, 0:1]
    r1 = jnp.sum(jnp.where(oh1, prior, 0.0), axis=0, keepdims=True)
    r2 = jnp.sum(jnp.where(oh2, prior, 0.0), axis=0, keepdims=True)
    rank_ref[...] = jnp.where(slot == 0, r1, jnp.where(slot == 1, r2, 0.0)).astype(jnp.int32)
    carry_ref[...] = carry_ref[...] + jnp.sum(oh, axis=1, keepdims=True)
    cnt_ref[...] = carry_ref[...].astype(jnp.int32)


def _route(merged, x2, w_out, ffn_norm_w, wr, rb, *, tm):
    n, d = x2.shape
    wrt = jnp.zeros((ROUTER_ROWS, d), F32)
    wrt = wrt.at[0:N_EXPERT_GROUPS].set(wr[0].T).at[ROUTER_EXPERT_ROW0:].set(wr[1].T)
    rbc = jnp.zeros((ROUTER_ROWS, 1), F32)
    rbc = rbc.at[0:N_EXPERT_GROUPS, 0].set(rb[0]).at[ROUTER_EXPERT_ROW0:, 0].set(rb[1])
    wrh, wrl = _split_bf16(wrt)
    row = lambda w: pl.BlockSpec((tm, w), lambda i: (i, 0))
    const = lambda shape: pl.BlockSpec(shape, lambda i: (0, 0), pipeline_mode=pl.Buffered(1))
    return pl.pallas_call(
        _route_kernel,
        grid=(n // tm,),
        in_specs=[
            row(d), row(d),
            const((d, d)), const((1, d)),
            const((ROUTER_ROWS, d)), const((ROUTER_ROWS, d)), const((ROUTER_ROWS, 1)),
        ],
        out_specs=[
            row(d),
            pl.BlockSpec((SUBLANES, tm), lambda i: (0, i)),
            row(LANES),
            pl.BlockSpec((SUBLANES, tm), lambda i: (0, i)),
            pl.BlockSpec((N_EXPERTS, LANES), lambda i: (0, 0)),
        ],
        out_shape=[
            jax.ShapeDtypeStruct((n, d), F32),
            jax.ShapeDtypeStruct((SUBLANES, n), jnp.int32),
            jax.ShapeDtypeStruct((n, LANES), F32),
            jax.ShapeDtypeStruct((SUBLANES, n), jnp.int32),
            jax.ShapeDtypeStruct((N_EXPERTS, LANES), jnp.int32),
        ],
        scratch_shapes=[pltpu.VMEM((N_EXPERTS, LANES), F32)],
        compiler_params=_params("arbitrary"),
        name="route",
    )(merged, x2, w_out.astype(BF16), ffn_norm_w.reshape(1, -1), wrh, wrl, rbc)


def _scatter_kernel(pad_start_ref, pad_len_ref, n_active_ref, pos_ref, hn_ref, xs_ref, zero_ref,
                    sem, zsem, *, tme, n_tiles):
    tm = hn_ref.shape[0]
    zrows = zero_ref.shape[0]

    @pl.when(pl.program_id(0) == 0)
    def _():
        zero_ref[...] = jnp.zeros(zero_ref.shape, zero_ref.dtype)

        def fills(act):
            def per_expert(e, carry):
                off = pad_start_ref[e]
                left = pad_len_ref[e]
                head = left & (SUBLANES - 1)
                for r in range(SUBLANES - 1):
                    @pl.when(r < head)
                    def _(r=r):
                        act(pltpu.make_async_copy(zero_ref.at[pl.ds(0, 1), :],
                                                  xs_ref.at[pl.ds(off + r, 1), :], zsem))

                off = off + head
                bit = zrows
                while bit >= SUBLANES:
                    take = left & bit

                    @pl.when(take != 0)
                    def _(off=off, bit=bit):
                        act(pltpu.make_async_copy(
                            zero_ref.at[pl.ds(0, bit), :],
                            xs_ref.at[pl.ds(pl.multiple_of(off, SUBLANES), bit), :], zsem))

                    off = off + take
                    bit //= 2
                return carry

            def per_tile(t, carry):
                @pl.when(t >= n_active_ref[0])
                def _():
                    for part in range(tme // zrows):
                        row0 = pl.multiple_of(t * tme + part * zrows, zrows)
                        act(pltpu.make_async_copy(zero_ref, xs_ref.at[pl.ds(row0, zrows), :], zsem))

                return carry

            lax.fori_loop(0, N_EXPERTS, per_expert, 0)
            lax.fori_loop(0, n_tiles, per_tile, 0)

        fills(lambda cp: cp.start())
        fills(lambda cp: cp.wait())

    def start(tb, carry):
        for u in range(ROW_DMA_UNROLL):
            t = tb * ROW_DMA_UNROLL + u
            for k in range(EXPERT_TOP_K):
                dst = pos_ref[0, 0, k * tm + t]
                pltpu.make_async_copy(hn_ref.at[pl.ds(t, 1), :], xs_ref.at[pl.ds(dst, 1), :],
                                      sem).start(priority=(u + k) % 2)
        return carry

    lax.fori_loop(0, tm // ROW_DMA_UNROLL, start, 0)
    for k in range(EXPERT_TOP_K):
        pltpu.make_async_copy(hn_ref, xs_ref.at[pl.ds(0, tm), :], sem).wait()


def _scatter(hn, pos_tiles, pad_start, pad_len, n_active, *, tm, tme, n_tiles):
    n, d = hn.shape
    assert tme % 2 == 0 and (tme // 2) & (tme // 2 - 1) == 0
    grid_spec = pltpu.PrefetchScalarGridSpec(
        num_scalar_prefetch=3,
        grid=(n // tm,),
        in_specs=[
            pl.BlockSpec((1, 1, EXPERT_TOP_K * tm), lambda i, *_: (i, 0, 0),
                         memory_space=pltpu.SMEM),
            pl.BlockSpec((tm, d), lambda i, *_: (i, 0)),
        ],
        out_specs=pl.BlockSpec(memory_space=pl.ANY),
        scratch_shapes=[pltpu.VMEM((tme // 2, d), hn.dtype), pltpu.SemaphoreType.DMA(()),
                        pltpu.SemaphoreType.DMA(())],
    )
    return pl.pallas_call(
        functools.partial(_scatter_kernel, tme=tme, n_tiles=n_tiles),
        grid_spec=grid_spec,
        out_shape=jax.ShapeDtypeStruct((n_tiles * tme, d), hn.dtype),
        compiler_params=_params("arbitrary"),
        name="scatter",
    )(pad_start, pad_len, n_active, pos_tiles, hn)


W_CHUNK_ROWS = 256
W_RING = 8


def _experts_kernel(te_ref, slot_ref, nxt_ref, lo_ref, hi_ref, na_ref, xs_ref, fnw_ref, wg_hbm,
                    wu_hbm, wd_hbm, ys_ref, wres_ref, stage_ref, sem, *, n_gu, n_r, n_h):
    i = pl.program_id(0)
    n_chunks = 2 * n_gu + n_h * n_r
    ring, cr, f = stage_ref.shape
    d = n_gu * cr

    def stage_copy(src, c):
        return pltpu.make_async_copy(src, stage_ref.at[c % ring], sem.at[c % ring])

    def start_chunk(e, c):
        @pl.when(c < n_gu)
        def _():
            stage_copy(wg_hbm.at[e, pl.ds(pl.multiple_of(c * cr, cr), cr), :], c).start()

        @pl.when((c >= n_gu) & (c < 2 * n_gu))
        def _():
            stage_copy(wu_hbm.at[e, pl.ds(pl.multiple_of((c - n_gu) * cr, cr), cr), :], c).start()

        @pl.when(c >= 2 * n_gu)
        def _():
            q = c - 2 * n_gu
            h = q // n_r
            r = q - h * n_r
            stage_copy(wd_hbm.at[e, pl.ds(pl.multiple_of(r * cr, cr), cr),
                                 pl.ds(pl.multiple_of(h * f, f), f)], c).start()

    def start_first(e):
        def body(c, carry):
            start_chunk(e, c)
            return carry

        lax.fori_loop(0, ring, body, 0)

    def convert(e, lo, hi, dst_slot):
        def body(c, carry):
            stage_copy(wg_hbm.at[0, pl.ds(0, cr), :], c).wait()
            wres_ref[dst_slot, pl.ds(pl.multiple_of(c * cr, cr), cr), :] = (
                stage_ref[c % ring].astype(BF16))

            @pl.when(c + ring < n_chunks)
            def _():
                start_chunk(e, c + ring)

            return carry

        lax.fori_loop(lo, hi, body, 0)

    @pl.when(i == 0)
    def _():
        start_first(te_ref[0])
        convert(te_ref[0], 0, n_chunks, slot_ref[0])

    @pl.when(i < na_ref[0])
    def _():
        slot = slot_ref[i]
        nxt = nxt_ref[i]

        @pl.when((nxt >= 0) & (lo_ref[i] == 0))
        def _():
            start_first(nxt)

        xr = xs_ref[...]
        ms = jnp.mean(xr * xr, axis=-1, keepdims=True)
        xb = (xr * lax.rsqrt(ms + NORM_EPS) * fnw_ref[...]).astype(BF16)
        hg = jnp.dot(xb, wres_ref[slot, 0:d, :], preferred_element_type=F32)
        hu = jnp.dot(xb, wres_ref[slot, d:2 * d, :], preferred_element_type=F32)
        hmid = (_silu(hg) * hu).astype(BF16)
        for h in range(n_h):
            r0 = 2 * d + h * n_r * cr
            ys_ref[:, h * f:(h + 1) * f] = jnp.dot(hmid, wres_ref[slot, r0:r0 + n_r * cr, :],
                                                    preferred_element_type=F32)

        @pl.when(nxt >= 0)
        def _():
            convert(nxt, lo_ref[i], hi_ref[i], 1 - slot)

    @pl.when(i >= na_ref[0])
    def _():
        ys_ref[...] = jnp.zeros(ys_ref.shape, F32)


def _experts(xs, norm_w, tile_expert, n_active, cnt, starts, padded, w_g, w_u, w_d, *, tm):
    p, d = xs.shape
    ne, _, f = w_g.shape
    n_tiles = p // tm
    cr = W_CHUNK_ROWS
    assert d % cr == 0 and f % cr == 0 and d % f == 0
    n_gu, n_r, n_h = d // cr, f // cr, d // f
    n_chunks = 2 * n_gu + n_h * n_r

    ids = jnp.arange(ne, dtype=jnp.int32)
    present = cnt > 0
    later = jnp.where(present[None, :] & (ids[None, :] > ids[:, None]), ids[None, :], ne)
    nxt_e = jnp.min(later, axis=1)
    nxt_e = jnp.where(nxt_e >= ne, -1, nxt_e)
    run_e = jnp.cumsum(present.astype(jnp.int32)) - 1
    onehot = (tile_expert[:, None] == ids[None, :]).astype(jnp.int32)
    pick = lambda v: jnp.sum(onehot * v[None, :].astype(jnp.int32), axis=1)
    tiles = jnp.arange(n_tiles, dtype=jnp.int32)
    active = tiles < n_active[0]
    j = tiles - pick(starts) // tm
    k = jnp.maximum(pick(padded) // tm, 1)
    nxt_t = jnp.where(active, pick(nxt_e), -1).astype(jnp.int32)
    lo_t = jnp.where(active, (n_chunks * j) // k, 0).astype(jnp.int32)
    hi_t = jnp.where(active, (n_chunks * (j + 1)) // k, 0).astype(jnp.int32)
    slot_t = (pick(run_e) % 2).astype(jnp.int32)

    def rows(i, te, sl, nx, lo, hi, na):
        return (jnp.minimum(i, na[0] - 1), 0)

    grid_spec = pltpu.PrefetchScalarGridSpec(
        num_scalar_prefetch=6,
        grid=(n_tiles,),
        in_specs=[
            pl.BlockSpec((tm, d), rows),
            pl.BlockSpec((1, d), lambda i, *_: (0, 0)),
            pl.BlockSpec(memory_space=pl.ANY),
            pl.BlockSpec(memory_space=pl.ANY),
            pl.BlockSpec(memory_space=pl.ANY),
        ],
        out_specs=pl.BlockSpec((tm, d), lambda i, *_: (i, 0)),
        scratch_shapes=[
            pltpu.VMEM((2, n_chunks * cr, f), BF16),
            pltpu.VMEM((W_RING, cr, f), F32),
            pltpu.SemaphoreType.DMA((W_RING,)),
        ],
    )
    return pl.pallas_call(
        functools.partial(_experts_kernel, n_gu=n_gu, n_r=n_r, n_h=n_h),
        grid_spec=grid_spec,
        out_shape=jax.ShapeDtypeStruct((p, d), F32),
        compiler_params=_params("arbitrary"),
        name="experts",
    )(tile_expert, slot_t, nxt_t, lo_t, hi_t, n_active, xs, norm_w.reshape(1, d), w_g, w_u, w_d)


def _combine_kernel(pos_ref, pos_next_ref, ys_ref, x1_ref, gcol_ref, nw_ref, out_ref, buf_ref, sem,
                    *, final):
    i = pl.program_id(0)
    n_steps = pl.num_programs(0)
    tm = x1_ref.shape[0]

    def issue(p_ref, slot):
        def start(tb, carry):
            for u in range(ROW_DMA_UNROLL):
                t = tb * ROW_DMA_UNROLL + u
                for k in range(EXPERT_TOP_K):
                    src = p_ref[0, 0, k * tm + t]
                    pltpu.make_async_copy(ys_ref.at[pl.ds(src, 1), :],
                                          buf_ref.at[slot, k, pl.ds(t, 1), :],
                                          sem.at[slot]).start(priority=(u + k) % 2)
            return carry

        lax.fori_loop(0, tm // ROW_DMA_UNROLL, start, 0)

    slot = i % 2

    @pl.when(i == 0)
    def _():
        issue(pos_ref, 0)

    @pl.when(i + 1 < n_steps)
    def _():
        issue(pos_next_ref, 1 - slot)

    for k in range(EXPERT_TOP_K):
        pltpu.make_async_copy(ys_ref.at[pl.ds(0, tm), :], buf_ref.at[slot, k], sem.at[slot]).wait()
    g = gcol_ref[...]
    xo = x1_ref[...] + g[:, 0:1] * buf_ref[slot, 0] + g[:, 1:2] * buf_ref[slot, 1]
    if final:
        ms = jnp.mean(xo * xo, axis=-1, keepdims=True)
        xo = xo * lax.rsqrt(ms + NORM_EPS) * nw_ref[...]
    out_ref[...] = xo


def _combine(ys, pos_tiles, x1, gcol, norm_w, *, tm, final):
    n, d = x1.shape
    last = n // tm - 1
    return pl.pallas_call(
        functools.partial(_combine_kernel, final=final),
        grid=(n // tm,),
        in_specs=[
            pl.BlockSpec((1, 1, EXPERT_TOP_K * tm), lambda i: (i, 0, 0), memory_space=pltpu.SMEM),
            pl.BlockSpec((1, 1, EXPERT_TOP_K * tm), lambda i: (jnp.minimum(i + 1, last), 0, 0),
                         memory_space=pltpu.SMEM),
            pl.BlockSpec(memory_space=pl.ANY),
            pl.BlockSpec((tm, d), lambda i: (i, 0)),
            pl.BlockSpec((tm, LANES), lambda i: (i, 0)),
            pl.BlockSpec((1, d), lambda i: (0, 0)),
        ],
        out_specs=pl.BlockSpec((tm, d), lambda i: (i, 0)),
        out_shape=jax.ShapeDtypeStruct((n, d), F32),
        scratch_shapes=[pltpu.VMEM((2, EXPERT_TOP_K, tm, d), F32), pltpu.SemaphoreType.DMA((2,))],
        compiler_params=_params("arbitrary"),
        name="combine",
    )(pos_tiles, pos_tiles, ys, x1, gcol, norm_w.reshape(1, d))


def _tiles(n):
    return dict(proj_tm=math.gcd(n, 1024), mix_tm=math.gcd(n, 512), moe_tm=256,
                route_tm=math.gcd(n, 256))


def _layer(x2, batch, seq, layer, attn_norm_w, w_in_stack, b_gate, conv_w, conv_b, dt_bias, a_log,
           d_skip, ssd_norm_w, w_ssd_out, w_attn_out, w_out, ffn_norm_w, w_gr, b_gr, w_er, b_er,
           w_g, w_u, w_d):
    n, d = x2.shape
    n_heads = dt_bias.shape[0]
    d_inner = ssd_norm_w.shape[0]
    conv_dim = conv_w.shape[1]
    aw_total = ATTN_HEADS_PER_GROUP * len(DILATION_PATTERNS) * ATTN_HEAD_DIM
    gw = ATTN_HEADS_PER_GROUP * ATTN_HEAD_DIM
    tiles = _tiles(n)

    c_z, c_xbc, c_dt = d_inner, d_inner + conv_dim, d_inner + conv_dim + n_heads
    c_gate = c_dt + QKV_PARTS * aw_total
    segments = [(c_gate, 2 * d), (c_z, conv_dim), (0, d_inner)]
    segments += [(c_dt + p * aw_total + gi * gw, gw)
                 for gi in range(len(DILATION_PATTERNS)) for p in range(QKV_PARTS)]
    assert all(width % gw == 0 for _, width in segments)
    starts = tuple(start + b * gw for start, width in segments for b in range(width // gw))
    w_main, w_dt = _regroup_weight(w_in_stack, layer, starts, gw, c_xbc, n_heads)
    off_gate, off_xbc, off_z = 0, 2 * d, 2 * d + conv_dim
    plain_cols = off_z + d_inner
    assert off_xbc % conv_dim == 0 and off_z % d_inner == 0 and plain_cols % gw == 0

    proj, qkv0, qkv1, qkv2, dt, dtt = _in_proj(x2, attn_norm_w, w_main, w_dt, tm=tiles["proj_tm"],
                                               tn=gw, tp=2 * gw, plain_cols=plain_cols, nh=n_heads)

    yn = _ssd(proj, dt, dtt, conv_w, conv_b, dt_bias, a_log, d_skip, ssd_norm_w, batch=batch,
              seq=seq, d_inner=d_inner, xbc_block=off_xbc // conv_dim, z_block=off_z // d_inner)

    outs, lses = [], []
    for gi, qkv in enumerate((qkv0, qkv1, qkv2)):
        o_g, lse_g = _attn_group(qkv, gi, batch=batch, seq=seq)
        outs.append(o_g)
        lses.append(lse_g)

    merged = _merge(yn, outs, lses, proj, w_ssd_out, w_attn_out, b_gate, tm=tiles["mix_tm"],
                    gate_block=off_gate // (2 * d))
    x1, eid, gcol, rank, counts = _route(merged, x2, w_out, ffn_norm_w, (w_gr, w_er),
                                         (b_gr, b_er), tm=tiles["mix_tm"])

    tme = tiles["moe_tm"]
    cnt = counts[:, 0]
    padded = ((cnt + tme - 1) // tme) * tme
    ends = jnp.cumsum(padded)
    starts = ends - padded
    experts = jnp.arange(N_EXPERTS, dtype=jnp.int32)[:, None, None]
    pos = rank[:EXPERT_TOP_K] + jnp.sum(
        jnp.where(eid[None, :EXPERT_TOP_K] == experts, starts[:, None, None], 0), axis=0)
    n_tiles = EXPERT_TOP_K * n // tme + N_EXPERTS
    tile_start = jnp.arange(n_tiles, dtype=jnp.int32) * tme
    tile_expert = jnp.minimum(jnp.sum(ends[None, :] <= tile_start[:, None], axis=1),
                              N_EXPERTS - 1).astype(jnp.int32)
    n_active = (ends[-1:] // tme).astype(jnp.int32)

    rtm = tiles["route_tm"]
    pos_tiles = pos.reshape(EXPERT_TOP_K, n // rtm, rtm).transpose(1, 0, 2).reshape(
        n // rtm, 1, EXPERT_TOP_K * rtm)
    xs = _scatter(x1, pos_tiles, (starts + cnt).astype(jnp.int32), (padded - cnt).astype(jnp.int32),
                  n_active, tm=rtm, tme=tme, n_tiles=n_tiles)
    ys = _experts(xs, ffn_norm_w, tile_expert, n_active, cnt, starts, padded, w_g, w_u, w_d, tm=tme)
    return ys, pos_tiles, x1, gcol, rtm


def kernel(x, attn_norm_w, w_in, b_gate, conv_w, conv_b, dt_bias, a_log, d_skip, ssd_norm_w,
           w_ssd_out, w_attn_out, w_out, ffn_norm_w, w_group_router, b_group_router,
           w_expert_router, b_expert_router, w_exp_gate, w_exp_up, w_exp_down, final_norm_w):
    batch, seq, d = x.shape
    depth = w_in.shape[0]
    x2 = x.reshape(batch * seq, d)
    for layer in range(depth):
        ys, pos_tiles, x1, gcol, rtm = _layer(
            x2, batch, seq, layer, attn_norm_w[layer], w_in, b_gate[layer], conv_w[layer],
            conv_b[layer], dt_bias[layer], a_log[layer], d_skip[layer], ssd_norm_w[layer],
            w_ssd_out[layer], w_attn_out[layer], w_out[layer], ffn_norm_w[layer],
            w_group_router[layer], b_group_router[layer], w_expert_router[layer],
            b_expert_router[layer], w_exp_gate[layer], w_exp_up[layer], w_exp_down[layer])
        x2 = _combine(ys, pos_tiles, x1, gcol, final_norm_w, tm=rtm, final=layer == depth - 1)
    return x2.reshape(batch, seq, d)
```

```python
import functools
import math

import jax
import jax.numpy as jnp
import numpy as np
from jax import lax
from jax.experimental import pallas as pl
from jax.experimental.pallas import tpu as pltpu

F32 = jnp.float32
BF16 = jnp.bfloat16

NORM_EPS = 1e-6
SSD_HEAD_DIM = 64
SSD_N_GROUPS = 8
SSD_D_STATE = 128
SSD_CONV_WIDTH = 4
SSD_CHUNK = 128
ATTN_HEAD_DIM = 128
DILATION_PATTERNS = ((128, 1), (512, 4), (2048, 16))
ATTN_HEADS_PER_GROUP = 4
ATTN_BLOCK = 128
ATTN_BLOCKS_PER_STEP = 4
N_EXPERT_GROUPS = 4
EXPERTS_PER_GROUP = 8
N_EXPERTS = N_EXPERT_GROUPS * EXPERTS_PER_GROUP
EXPERT_TOP_K = 2

LANES = 128
SUBLANES = 8
VMEM_LIMIT_BYTES = 56 * 1024 * 1024
ROW_DMA_UNROLL = 8

ROUTER_EXPERT_ROW0 = SUBLANES
ROUTER_ROWS = ROUTER_EXPERT_ROW0 + N_EXPERTS


def _params(*semantics):
    return pltpu.CompilerParams(dimension_semantics=semantics, vmem_limit_bytes=VMEM_LIMIT_BYTES)


def _split_bf16(v):
    hi = v.astype(BF16)
    lo = (v - hi.astype(F32)).astype(BF16)
    return hi, lo


def _silu(v):
    return v * (1.0 / (1.0 + jnp.exp(-v)))


def _regroup_weight_kernel(starts_ref, wt_hbm, out_ref, narrow_ref, buf_ref, nbuf_ref, sem, nsem,
                           *, layer, narrow_start, nh):
    i = pl.program_id(0)
    n_steps = pl.num_programs(0)
    tn = buf_ref.shape[1]

    def fetch(step, slot):
        rows = pl.ds(pl.multiple_of(starts_ref[step], SUBLANES), tn)
        return pltpu.make_async_copy(wt_hbm.at[layer, rows, :], buf_ref.at[slot], sem.at[slot])

    narrow = pltpu.make_async_copy(wt_hbm.at[layer, pl.ds(narrow_start, LANES), :], nbuf_ref, nsem)

    @pl.when(i == 0)
    def _():
        fetch(0, 0).start()
        narrow.start()

    @pl.when(i + 1 < n_steps)
    def _():
        fetch(i + 1, (i + 1) % 2).start()

    fetch(i, i % 2).wait()
    out_ref[...] = buf_ref[i % 2].T.astype(out_ref.dtype)

    @pl.when(i == n_steps - 1)
    def _():
        narrow.wait()
        lane = lax.broadcasted_iota(jnp.int32, narrow_ref.shape, 1)
        narrow_ref[...] = jnp.where(lane < nh, nbuf_ref[...].T, 0.0).astype(narrow_ref.dtype)


def _regroup_weight(w_stack, layer, starts, tn, narrow_start, nh):
    _, k, cols = w_stack.shape
    assert all(s % SUBLANES == 0 and s + tn <= cols for s in starts)
    assert narrow_start % SUBLANES == 0 and narrow_start + LANES <= cols and nh <= LANES
    wt = jnp.swapaxes(w_stack, 1, 2)
    grid_spec = pltpu.PrefetchScalarGridSpec(
        num_scalar_prefetch=1,
        grid=(len(starts),),
        in_specs=[pl.BlockSpec(memory_space=pl.ANY)],
        out_specs=[pl.BlockSpec((k, tn), lambda i, *_: (0, i)),
                   pl.BlockSpec((k, LANES), lambda i, *_: (0, 0))],
        scratch_shapes=[pltpu.VMEM((2, tn, k), F32), pltpu.VMEM((LANES, k), F32),
                        pltpu.SemaphoreType.DMA((2,)), pltpu.SemaphoreType.DMA(())],
    )
    return pl.pallas_call(
        functools.partial(_regroup_weight_kernel, layer=layer, narrow_start=narrow_start, nh=nh),
        grid_spec=grid_spec,
        out_shape=[jax.ShapeDtypeStruct((k, len(starts) * tn), BF16),
                   jax.ShapeDtypeStruct((k, LANES), BF16)],
        compiler_params=_params("arbitrary"),
        name="regroup_weight",
    )(jnp.asarray(starts, jnp.int32), wt)


QKV_PARTS = 3


def _in_proj_kernel(x_ref, nw_ref, wp_ref, w_ref, wdt_ref, proj_ref, a0_ref, a1_ref, a2_ref,
                    dt_ref, dtt_ref, h_ref, stage_ref, *, n_plain):
    j = pl.program_id(1)
    tm = proj_ref.shape[0]
    tn = w_ref.shape[1]

    @pl.when(j == 0)
    def _():
        xf = x_ref[...]
        ms = jnp.mean(xf * xf, axis=-1, keepdims=True)
        h = (xf * lax.rsqrt(ms + NORM_EPS) * nw_ref[...]).astype(BF16)
        h_ref[...] = h
        nh = dt_ref.shape[1]
        dt_wide = jnp.dot(h, wdt_ref[...], preferred_element_type=F32)
        dt_ref[...] = dt_wide[:, :nh]
        dtt_ref[...] = dt_wide.T[:nh, :]

    @pl.when(j < n_plain)
    def _():
        proj_ref[...] = jnp.dot(h_ref[...], wp_ref[...], preferred_element_type=F32).astype(BF16)

    for gi, a_ref in enumerate((a0_ref, a1_ref, a2_ref)):
        dil = DILATION_PATTERNS[gi][1]
        j0 = n_plain + QKV_PARTS * gi

        @pl.when((j >= j0) & (j < j0 + QKV_PARTS))
        def _(a_ref=a_ref, dil=dil):
            res = jnp.dot(h_ref[...], w_ref[...], preferred_element_type=F32)
            if dil == 1:
                a_ref[...] = res.astype(BF16)
            else:
                for s in range(tn // LANES):
                    stage_ref[s] = res[:, s * LANES:(s + 1) * LANES]
                for r in range(dil):
                    for s in range(tn // LANES):
                        c0 = r * tn + s * LANES
                        a_ref[:, c0:c0 + LANES] = stage_ref[
                            s, pl.ds(r, tm // dil, stride=dil), :].astype(BF16)


def _in_proj(x2, norm_w, w_main, w_dt, *, tm, tn, tp, plain_cols, nh):
    n, d = x2.shape
    assert plain_cols % tp == 0 and plain_cols % tn == 0
    n_plain = plain_cols // tp
    qkv_block0 = plain_cols // tn
    n_blocks = n_plain + QKV_PARTS * len(DILATION_PATTERNS)
    assert w_main.shape[1] == plain_cols + QKV_PARTS * len(DILATION_PATTERNS) * tn

    def a_spec(gi):
        dil = DILATION_PATTERNS[gi][1]
        j0 = n_plain + QKV_PARTS * gi
        return pl.BlockSpec((tm // dil, dil * tn),
                            lambda i, j: (i, jnp.clip(j - j0, 0, QKV_PARTS - 1)))

    def a_shape(gi):
        dil = DILATION_PATTERNS[gi][1]
        return jax.ShapeDtypeStruct((n // dil, dil * QKV_PARTS * tn), BF16)

    return pl.pallas_call(
        functools.partial(_in_proj_kernel, n_plain=n_plain),
        grid=(n // tm, n_blocks),
        in_specs=[
            pl.BlockSpec((tm, d), lambda i, j: (i, 0)),
            pl.BlockSpec((1, d), lambda i, j: (0, 0)),
            pl.BlockSpec((d, tp), lambda i, j: (0, jnp.minimum(j, n_plain - 1))),
            pl.BlockSpec((d, tn), lambda i, j: (0, qkv_block0 + jnp.maximum(j - n_plain, 0))),
            pl.BlockSpec((d, LANES), lambda i, j: (0, 0)),
        ],
        out_specs=[
            pl.BlockSpec((tm, tp), lambda i, j: (i, jnp.minimum(j, n_plain - 1))),
            a_spec(0), a_spec(1), a_spec(2),
            pl.BlockSpec((tm, nh), lambda i, j: (i, 0)),
            pl.BlockSpec((nh, tm), lambda i, j: (0, i)),
        ],
        out_shape=[
            jax.ShapeDtypeStruct((n, plain_cols), BF16),
            a_shape(0), a_shape(1), a_shape(2),
            jax.ShapeDtypeStruct((n, nh), F32),
            jax.ShapeDtypeStruct((nh, n), F32),
        ],
        scratch_shapes=[pltpu.VMEM((tm, d), BF16), pltpu.VMEM((tn // LANES, tm, LANES), F32)],
        compiler_params=_params("arbitrary", "arbitrary"),
        name="in_proj",
    )(x2, norm_w.reshape(1, d), w_main, w_main, w_dt)


LOG2E = math.log2(math.e)
CONV_ROW_PITCH = 2


def _ssd_kernel(xbc_ref, z_ref, dt_ref, dtt_ref, cw_ref, cb_ref, dtb_ref, dtbt_ref, alog_ref,
                alogt_ref, dskip_ref, nw_ref, expand_ref, out_ref,
                xbuf_ref, state_ref, y_ref, *, n_heads, d_inner):
    L = SSD_CHUNK
    P = SSD_HEAD_DIM
    NS = SSD_D_STATE
    G = SSD_N_GROUPS
    R = n_heads // G
    GW = R * P
    W = SSD_CONV_WIDTH
    RP = CONV_ROW_PITCH
    n_slabs = xbuf_ref.shape[0]
    c = pl.program_id(1)

    def rows(first, count):
        return pl.ds(first * RP, count, stride=RP)

    @pl.when(c == 0)
    def _():
        state_ref[...] = jnp.zeros(state_ref.shape, F32)
        for s in range(n_slabs):
            xbuf_ref[s, rows(0, SUBLANES), :] = jnp.zeros((SUBLANES, LANES), F32)

    @pl.when(c > 0)
    def _():
        for s in range(n_slabs):
            xbuf_ref[s, rows(0, SUBLANES), :] = xbuf_ref[s, rows(L, SUBLANES), :]

    for s in range(n_slabs):
        xbuf_ref[s, rows(SUBLANES, L), :] = xbc_ref[:, s * LANES:(s + 1) * LANES].astype(F32)

    def conv(col0, width):
        slabs = []
        for s in range(col0 // LANES, (col0 + width) // LANES):
            acc = cb_ref[:, s * LANES:(s + 1) * LANES]
            for w in range(W):
                acc = acc + (cw_ref[w:w + 1, s * LANES:(s + 1) * LANES]
                             * xbuf_ref[s, rows(SUBLANES - (W - 1) + w, L), :])
            slabs.append(acc)
        return _silu(jnp.concatenate(slabs, axis=1))

    def softplus(v):
        return jnp.maximum(v, 0.0) + jnp.log1p(jnp.exp(-jnp.abs(v)))

    dt = softplus(dt_ref[...] + dtb_ref[...])
    dtt = softplus(dtt_ref[...] + dtbt_ref[...])
    da = dt * (-LOG2E * jnp.exp(alog_ref[...]))
    dat = dtt * (-LOG2E * jnp.exp(alogt_ref[...]))
    row = lax.broadcasted_iota(jnp.int32, (L, L), 0)
    col = lax.broadcasted_iota(jnp.int32, (L, L), 1)
    causal = row >= col
    tri = jnp.where(causal, 1.0, 0.0).astype(BF16)
    trit = jnp.where(row <= col, 1.0, 0.0).astype(BF16)

    def split3(v):
        a = v.astype(BF16)
        r1 = v - a.astype(F32)
        b = r1.astype(BF16)
        cc = (r1 - b.astype(F32)).astype(BF16)
        return a, b, cc

    a2 = sum(jnp.dot(tri, p, preferred_element_type=F32) for p in split3(da))
    a2t = sum(jnp.dot(p, trit, preferred_element_type=F32) for p in split3(dat))
    a2_last = a2[L - 1:L, :]

    expand = expand_ref[...]

    def expand_heads(v):
        hi, lo = _split_bf16(v)
        return jnp.dot(jnp.concatenate([hi, lo], axis=1), expand, preferred_element_type=F32)

    in_scale_e = expand_heads(dt * jnp.exp2(a2_last - a2))
    tail8 = jnp.concatenate([jnp.exp2(a2_last), dskip_ref[...],
                             jnp.zeros((SUBLANES - 2, n_heads), F32)], axis=0)
    out_e = expand_heads(jnp.concatenate([jnp.exp2(a2), tail8], axis=0))
    out_scale_e = out_e[0:L, :]
    chunk_decay_e = out_e[L:L + 1, :]
    dskip_e = out_e[L + 1:L + 2, :]

    first_head = lax.broadcasted_iota(jnp.int32, (L, LANES), 1) < P

    for g in range(G):
        x0 = g * GW
        xs = conv(x0, GW)
        bm = conv(d_inner + g * NS, NS).astype(BF16)
        cm = conv(d_inner + G * NS + g * NS, NS).astype(BF16)
        cb = lax.dot_general(cm, bm, (((1,), (1,)), ((), ())), preferred_element_type=F32)
        cb = jnp.where(causal, cb, 0.0)
        y_parts = []
        for pr in range(GW // LANES):
            wgts = []
            for q in range(LANES // P):
                h = g * R + pr * (LANES // P) + q
                seg = a2[:, h:h + 1] - a2t[h:h + 1, :]
                decay = jnp.exp2(jnp.minimum(seg, 0.0))
                wgts.append((cb * decay * dtt[h:h + 1, :]).astype(BF16))
            slab = xs[:, pr * LANES:(pr + 1) * LANES]
            rhs = jnp.concatenate([jnp.where(first_head, slab, 0.0).astype(BF16),
                                   jnp.where(first_head, 0.0, slab).astype(BF16)], axis=0)
            y_parts.append(jnp.dot(jnp.concatenate(wgts, axis=1), rhs, preferred_element_type=F32))
        y = jnp.concatenate(y_parts, axis=1) + xs * dskip_e[:, x0:x0 + GW]
        st = state_ref[:, x0:x0 + GW]
        y = y + (jnp.dot(cm, st.astype(BF16), preferred_element_type=F32)
                 * out_scale_e[:, x0:x0 + GW])
        y_ref[:, x0:x0 + GW] = y
        xin = (xs * in_scale_e[:, x0:x0 + GW]).astype(BF16)
        st_new = lax.dot_general(bm, xin, (((0,), (0,)), ((), ())), preferred_element_type=F32)
        state_ref[:, x0:x0 + GW] = st * chunk_decay_e[:, x0:x0 + GW] + st_new

    yz = y_ref[...] * _silu(z_ref[...].astype(F32))
    ms = jnp.mean(yz * yz, axis=-1, keepdims=True)
    out_ref[...] = (yz * lax.rsqrt(ms + NORM_EPS) * nw_ref[...]).astype(out_ref.dtype)


def _ssd(proj, dt, dtt, conv_w, conv_b, dt_bias, a_log, d_skip, norm_w, *, batch, seq, d_inner,
         xbc_block, z_block):
    n = batch * seq
    n_heads = dt.shape[1]
    L = SSD_CHUNK
    nc = seq // L
    conv_dim = conv_w.shape[1]
    expand = (np.arange(d_inner)[None, :] // SSD_HEAD_DIM == np.arange(n_heads)[:, None])
    expand = jnp.asarray(np.concatenate([expand, expand], axis=0), BF16)
    assert conv_dim % LANES == 0
    kern = functools.partial(_ssd_kernel, n_heads=n_heads, d_inner=d_inner)
    small = lambda shape: pl.BlockSpec(shape, lambda b, c: (0, 0))
    return pl.pallas_call(
        kern,
        grid=(batch, nc),
        in_specs=[
            pl.BlockSpec((L, conv_dim), lambda b, c: (b * nc + c, xbc_block)),
            pl.BlockSpec((L, d_inner), lambda b, c: (b * nc + c, z_block)),
            pl.BlockSpec((L, n_heads), lambda b, c: (b * nc + c, 0)),
            pl.BlockSpec((n_heads, L), lambda b, c: (0, b * nc + c)),
            small((SSD_CONV_WIDTH, conv_dim)),
            small((1, conv_dim)),
            small((1, n_heads)),
            small((n_heads, 1)),
            small((1, n_heads)),
            small((n_heads, 1)),
            small((1, n_heads)),
            small((1, d_inner)),
            small((2 * n_heads, d_inner)),
        ],
        out_specs=pl.BlockSpec((L, d_inner), lambda b, c: (b * nc + c, 0)),
        out_shape=jax.ShapeDtypeStruct((n, d_inner), BF16),
        scratch_shapes=[
            pltpu.VMEM((conv_dim // LANES, CONV_ROW_PITCH * (L + SUBLANES), LANES), F32),
            pltpu.VMEM((SSD_D_STATE, d_inner), F32),
            pltpu.VMEM((L, d_inner), F32),
        ],
        compiler_params=_params("arbitrary", "arbitrary"),
        name="ssd",
    )(proj, proj, dt, dtt, conv_w, conv_b.reshape(1, -1), dt_bias.reshape(1, -1),
      dt_bias.reshape(-1, 1), a_log.reshape(1, -1), a_log.reshape(-1, 1), d_skip.reshape(1, -1),
      norm_w.reshape(1, -1), expand)


def _attn_kernel(q_ref, k_ref, v_ref, o_ref, lse_ref, kp_ref, vp_ref, *, slopes, dilation, hops):
    nb = pl.program_id(2)
    BLK = ATTN_BLOCK
    E = ATTN_HEAD_DIM

    @pl.when(nb == 0)
    def _():
        kp_ref[...] = jnp.zeros(kp_ref.shape, kp_ref.dtype)
        vp_ref[...] = jnp.zeros(vp_ref.shape, vp_ref.dtype)

    scale = E ** -0.5
    nt = (((1,), (1,)), ((), ()))
    nblk = q_ref.shape[0] // BLK
    q = q_ref[...]
    kk = jnp.concatenate([kp_ref[...], k_ref[...]], axis=0)
    vv = jnp.concatenate([vp_ref[...], v_ref[...]], axis=0)
    units = [(j, h) for j in range(nblk) for h in range(len(slopes))]
    cols = lambda h: slice(h * E, (h + 1) * E)
    rows = lambda j: slice(j * BLK, (j + 1) * BLK)
    qi = lax.broadcasted_iota(jnp.int32, (BLK, BLK), 0)
    ki = lax.broadcasted_iota(jnp.int32, (BLK, BLK), 1)
    rel_cur = qi - ki
    rel_prev = rel_cur + BLK
    first = rel_prev <= jnp.where(nb > 0, hops, -1)
    later = rel_prev <= hops
    ok_cur = jnp.concatenate([rel_cur >= 0 for _ in units], axis=0)
    ok_prev = jnp.concatenate([first if j == 0 else later for j, _ in units], axis=0)
    dist_cur = (rel_cur * dilation).astype(F32)
    dist_prev = (rel_prev * dilation).astype(F32)
    bias_cur = jnp.concatenate([slopes[h] * dist_cur for _, h in units], axis=0)
    bias_prev = jnp.concatenate([slopes[h] * dist_prev for _, h in units], axis=0)
    s_cur = jnp.concatenate(
        [lax.dot_general(q[rows(j), cols(h)], kk[rows(j + 1), cols(h)], nt,
                         preferred_element_type=F32) for j, h in units], axis=0)
    s_prev = jnp.concatenate(
        [lax.dot_general(q[rows(j), cols(h)], kk[rows(j), cols(h)], nt,
                         preferred_element_type=F32) for j, h in units], axis=0)
    l_cur = jnp.where(ok_cur, s_cur * scale - bias_cur, -jnp.inf)
    l_prev = jnp.where(ok_prev, s_prev * scale - bias_prev, -jnp.inf)
    m = jnp.max(jnp.maximum(l_cur, l_prev), axis=-1, keepdims=True)
    p_cur = jnp.exp(l_cur - m)
    p_prev = jnp.exp(l_prev - m)
    den = jnp.sum(p_cur + p_prev, axis=-1, keepdims=True)
    p_cur = p_cur.astype(BF16)
    p_prev = p_prev.astype(BF16)
    inv = 1.0 / den
    lse = m + jnp.log(den)
    lane = lax.broadcasted_iota(jnp.int32, (BLK, LANES), 1)
    for j in range(nblk):
        lse_tile = jnp.zeros((BLK, LANES), F32)
        for h in range(len(slopes)):
            u = rows(units.index((j, h)))
            acc = (jnp.dot(p_cur[u], vv[rows(j + 1), cols(h)], preferred_element_type=F32)
                   + jnp.dot(p_prev[u], vv[rows(j), cols(h)], preferred_element_type=F32))
            o_ref[rows(j), cols(h)] = (acc * inv[u]).astype(o_ref.dtype)
            lse_tile = jnp.where(lane == h, lse[u], lse_tile)
        lse_ref[rows(j), :] = lse_tile
    kp_ref[...] = k_ref[rows(nblk - 1), :]
    vp_ref[...] = v_ref[rows(nblk - 1), :]


def _attn_group(qkv, gi, *, batch, seq):
    window, dilation = DILATION_PATTERNS[gi]
    hops = window // dilation
    n_heads_total = ATTN_HEADS_PER_GROUP * len(DILATION_PATTERNS)
    slopes = tuple(float(2.0 ** (-8.0 * (gi * ATTN_HEADS_PER_GROUP + h + 1) / n_heads_total))
                   for h in range(ATTN_HEADS_PER_GROUP))
    gw = ATTN_HEADS_PER_GROUP * ATTN_HEAD_DIM
    assert seq % (dilation * ATTN_BLOCK) == 0
    sub = seq // dilation
    rows = ATTN_BLOCK * math.gcd(sub // ATTN_BLOCK, ATTN_BLOCKS_PER_STEP)
    nb = sub // rows
    kern = functools.partial(_attn_kernel, slopes=slopes, dilation=dilation, hops=hops)

    def part(p):
        return pl.BlockSpec((rows, gw), lambda b, r, n: (b * nb + n, p * dilation + r))

    o, lse = pl.pallas_call(
        kern,
        grid=(batch, dilation, nb),
        in_specs=[part(0), part(1), part(2)],
        out_specs=[
            pl.BlockSpec((rows, gw), lambda b, r, n: (b * nb + n, r)),
            pl.BlockSpec((rows, LANES), lambda b, r, n: (b * nb + n, r)),
        ],
        out_shape=[
            jax.ShapeDtypeStruct((batch * sub, dilation * gw), BF16),
            jax.ShapeDtypeStruct((batch * sub, dilation * LANES), F32),
        ],
        scratch_shapes=[pltpu.VMEM((ATTN_BLOCK, gw), BF16), pltpu.VMEM((ATTN_BLOCK, gw), BF16)],
        compiler_params=_params("arbitrary", "arbitrary", "arbitrary"),
        name=f"attn_g{gi}",
    )(qkv, qkv, qkv)
    return o, lse


def _merge_kernel(yn_ref, o0_ref, o1_ref, o2_ref, l0_ref, l1_ref, l2_ref, gate_ref,
                  wssd_ref, wattn_ref, bg_ref, merged_ref, ostage_ref, lstage_ref, *, d_model):
    E = ATTN_HEAD_DIM
    tm = merged_ref.shape[0]
    n_slabs = o0_ref.shape[1] // LANES
    y_ssd = jnp.dot(yn_ref[...], wssd_ref[...], preferred_element_type=F32)

    def token_major(gi, o_ref, l_ref):
        dil = DILATION_PATTERNS[gi][1]
        if dil == 1:
            return o_ref[...].astype(F32), l_ref[...]
        for r in range(dil):
            rows = pl.ds(r, tm // dil, stride=dil)
            lstage_ref[gi - 1, rows, :] = l_ref[:, r * LANES:(r + 1) * LANES]
            for s in range(n_slabs):
                c0 = (r * n_slabs + s) * LANES
                ostage_ref[gi - 1, s, rows, :] = o_ref[:, c0:c0 + LANES].astype(F32)
        out = jnp.concatenate([ostage_ref[gi - 1, s] for s in range(n_slabs)], axis=1)
        return out, lstage_ref[gi - 1]

    (o0, l0), (o1, l1), (o2, l2) = (token_major(gi, o_ref, l_ref) for gi, (o_ref, l_ref) in
                                    enumerate(((o0_ref, l0_ref), (o1_ref, l1_ref), (o2_ref, l2_ref))))

    lm = jnp.maximum(jnp.maximum(l0, l1), l2)
    e0, e1, e2 = jnp.exp(l0 - lm), jnp.exp(l1 - lm), jnp.exp(l2 - lm)
    inv = 1.0 / (e0 + e1 + e2)
    parts = []
    for h in range(ATTN_HEADS_PER_GROUP):
        sl = slice(h * E, (h + 1) * E)
        parts.append((e0[:, h:h + 1] * inv[:, h:h + 1]) * o0[:, sl]
                     + (e1[:, h:h + 1] * inv[:, h:h + 1]) * o1[:, sl]
                     + (e2[:, h:h + 1] * inv[:, h:h + 1]) * o2[:, sl])
    o = jnp.concatenate(parts, axis=-1).astype(BF16)
    y_attn = jnp.dot(o, wattn_ref[...], preferred_element_type=F32)

    graw = gate_ref[...].astype(F32) + bg_ref[...]
    gates = 1.0 / (1.0 + jnp.exp(-graw))
    merged_ref[...] = (gates[:, :d_model] * y_ssd + gates[:, d_model:] * y_attn).astype(BF16)


def _merge(yn, outs, lses, proj, w_ssd_out, w_attn_out, b_gate, *, tm, gate_block):
    n, d = yn.shape
    aw = outs[0].shape[1]
    dils = [dil for _, dil in DILATION_PATTERNS]
    assert dils[0] == 1 and all(tm % (dil * SUBLANES) == 0 for dil in dils)
    row = lambda w: pl.BlockSpec((tm, w), lambda i: (i, 0))
    grouped = lambda dil, w: pl.BlockSpec((tm // dil, dil * w), lambda i: (i, 0))
    const = lambda shape: pl.BlockSpec(shape, lambda i: (0, 0), pipeline_mode=pl.Buffered(1))
    return pl.pallas_call(
        functools.partial(_merge_kernel, d_model=d),
        grid=(n // tm,),
        in_specs=[
            row(d), *[grouped(dil, aw) for dil in dils], *[grouped(dil, LANES) for dil in dils],
            pl.BlockSpec((tm, 2 * d), lambda i: (i, gate_block)),
            const((d, d)), const((aw, d)), const((1, 2 * d)),
        ],
        out_specs=row(d),
        out_shape=jax.ShapeDtypeStruct((n, d), BF16),
        scratch_shapes=[pltpu.VMEM((len(dils) - 1, aw // LANES, tm, LANES), F32),
                        pltpu.VMEM((len(dils) - 1, tm, LANES), F32)],
        compiler_params=_params("arbitrary"),
        name="merge",
    )(yn, outs[0], outs[1], outs[2], lses[0], lses[1], lses[2], proj,
      w_ssd_out.astype(BF16), w_attn_out.astype(BF16), b_gate.reshape(1, -1))


def _route_kernel(merged_ref, x_ref, wout_ref, fnw_ref, wrh_ref, wrl_ref, rb_ref,
                  x1_ref, eid_ref, gcol_ref, rank_ref, cnt_ref, carry_ref):
    i = pl.program_id(0)
    tm = x_ref.shape[0]

    @pl.when(i == 0)
    def _():
        carry_ref[...] = jnp.zeros(carry_ref.shape, F32)

    x1 = x_ref[...] + jnp.dot(merged_ref[...], wout_ref[...], preferred_element_type=F32)
    x1_ref[...] = x1

    ms = jnp.mean(x1 * x1, axis=-1, keepdims=True)
    hn = x1 * lax.rsqrt(ms + NORM_EPS) * fnw_ref[...]

    hn_hi, hn_lo = _split_bf16(hn)
    nt = (((1,), (1,)), ((), ()))
    logits = (lax.dot_general(wrh_ref[...], hn_hi, nt, preferred_element_type=F32)
              + lax.dot_general(wrh_ref[...], hn_lo, nt, preferred_element_type=F32)
              + lax.dot_general(wrl_ref[...], hn_hi, nt, preferred_element_type=F32)
              + rb_ref[...])

    grow = lax.broadcasted_iota(jnp.int32, (SUBLANES, tm), 0)
    gl = jnp.where(grow < N_EXPERT_GROUPS, logits[0:SUBLANES, :], -jnp.inf)
    gmax = jnp.max(gl, axis=0, keepdims=True)
    gidx = jnp.min(jnp.where(gl == gmax, grow, N_EXPERT_GROUPS), axis=0, keepdims=True)
    group_gate = 1.0 / jnp.sum(jnp.exp(gl - gmax), axis=0, keepdims=True)

    in_group = jnp.zeros((EXPERTS_PER_GROUP, tm), F32)
    for g in range(N_EXPERT_GROUPS):
        r0 = ROUTER_EXPERT_ROW0 + g * EXPERTS_PER_GROUP
        in_group = jnp.where(gidx == g, logits[r0:r0 + EXPERTS_PER_GROUP, :], in_group)
    erow = lax.broadcasted_iota(jnp.int32, in_group.shape, 0)
    v1 = jnp.max(in_group, axis=0, keepdims=True)
    i1 = jnp.min(jnp.where(in_group == v1, erow, EXPERTS_PER_GROUP), axis=0, keepdims=True)
    rest = jnp.where(erow == i1, -jnp.inf, in_group)
    v2 = jnp.max(rest, axis=0, keepdims=True)
    i2 = jnp.min(jnp.where(rest == v2, erow, EXPERTS_PER_GROUP), axis=0, keepdims=True)
    t = jnp.exp(v2 - v1)
    g1 = group_gate / (1.0 + t)
    g2 = group_gate * t / (1.0 + t)
    eid1 = gidx * EXPERTS_PER_GROUP + i1
    eid2 = gidx * EXPERTS_PER_GROUP + i2
    slot = lax.broadcasted_iota(jnp.int32, (SUBLANES, tm), 0)
    eid_ref[...] = jnp.where(slot == 0, eid1, jnp.where(slot == 1, eid2, 0))

    grow8 = lax.broadcasted_iota(jnp.int32, (LANES, tm), 0)
    gt = jnp.where(grow8 == 0, g1, jnp.where(grow8 == 1, g2, 0.0))
    gcol_ref[...] = gt.T

    xrow = lax.broadcasted_iota(jnp.int32, (N_EXPERTS, tm), 0)
    oh1 = xrow == eid1
    oh2 = xrow == eid2
    oh = jnp.where(oh1 | oh2, 1.0, 0.0)
    ti = lax.broadcasted_iota(jnp.int32, (tm, tm), 0)
    tj = lax.broadcasted_iota(jnp.int32, (tm, tm), 1)
    before = jnp.where(ti < tj, 1.0, 0.0).astype(BF16)
    prior = jnp.dot(oh.astype(BF16), before, preferred_element_type=F32) + carry_ref[:, 0:1]
    r1 = jnp.sum(jnp.where(oh1, prior, 0.0), axis=0, keepdims=True)
    r2 = jnp.sum(jnp.where(oh2, prior, 0.0), axis=0, keepdims=True)
    rank_ref[...] = jnp.where(slot == 0, r1, jnp.where(slot == 1, r2, 0.0)).astype(jnp.int32)
    carry_ref[...] = carry_ref[...] + jnp.sum(oh, axis=1, keepdims=True)
    cnt_ref[...] = carry_ref[...].astype(jnp.int32)


def _route(merged, x2, w_out, ffn_norm_w, wr, rb, *, tm):
    n, d = x2.shape
    wrt = jnp.zeros((ROUTER_ROWS, d), F32)
    wrt = wrt.at[0:N_EXPERT_GROUPS].set(wr[0].T).at[ROUTER_EXPERT_ROW0:].set(wr[1].T)
    rbc = jnp.zeros((ROUTER_ROWS, 1), F32)
    rbc = rbc.at[0:N_EXPERT_GROUPS, 0].set(rb[0]).at[ROUTER_EXPERT_ROW0:, 0].set(rb[1])
    wrh, wrl = _split_bf16(wrt)
    row = lambda w: pl.BlockSpec((tm, w), lambda i: (i, 0))
    const = lambda shape: pl.BlockSpec(shape, lambda i: (0, 0), pipeline_mode=pl.Buffered(1))
    return pl.pallas_call(
        _route_kernel,
        grid=(n // tm,),
        in_specs=[
            row(d), row(d),
            const((d, d)), const((1, d)),
            const((ROUTER_ROWS, d)), const((ROUTER_ROWS, d)), const((ROUTER_ROWS, 1)),
        ],
        out_specs=[
            row(d),
            pl.BlockSpec((SUBLANES, tm), lambda i: (0, i)),
            row(LANES),
            pl.BlockSpec((SUBLANES, tm), lambda i: (0, i)),
            pl.BlockSpec((N_EXPERTS, LANES), lambda i: (0, 0)),
        ],
        out_shape=[
            jax.ShapeDtypeStruct((n, d), F32),
            jax.ShapeDtypeStruct((SUBLANES, n), jnp.int32),
            jax.ShapeDtypeStruct((n, LANES), F32),
            jax.ShapeDtypeStruct((SUBLANES, n), jnp.int32),
            jax.ShapeDtypeStruct((N_EXPERTS, LANES), jnp.int32),
        ],
        scratch_shapes=[pltpu.VMEM((N_EXPERTS, LANES), F32)],
        compiler_params=_params("arbitrary"),
        name="route",
    )(merged, x2, w_out.astype(BF16), ffn_norm_w.reshape(1, -1), wrh, wrl, rbc)


def _scatter_kernel(pad_start_ref, pad_len_ref, n_active_ref, pos_ref, hn_ref, xs_ref, zero_ref,
                    sem, zsem, *, tme, n_tiles):
    tm = hn_ref.shape[0]
    zrows = zero_ref.shape[0]

    @pl.when(pl.program_id(0) == 0)
    def _():
        zero_ref[...] = jnp.zeros(zero_ref.shape, zero_ref.dtype)

        def fills(act):
            def per_expert(e, carry):
                off = pad_start_ref[e]
                left = pad_len_ref[e]
                head = left & (SUBLANES - 1)
                for r in range(SUBLANES - 1):
                    @pl.when(r < head)
                    def _(r=r):
                        act(pltpu.make_async_copy(zero_ref.at[pl.ds(0, 1), :],
                                                  xs_ref.at[pl.ds(off + r, 1), :], zsem))

                off = off + head
                bit = zrows
                while bit >= SUBLANES:
                    take = left & bit

                    @pl.when(take != 0)
                    def _(off=off, bit=bit):
                        act(pltpu.make_async_copy(
                            zero_ref.at[pl.ds(0, bit), :],
                            xs_ref.at[pl.ds(pl.multiple_of(off, SUBLANES), bit), :], zsem))

                    off = off + take
                    bit //= 2
                return carry

            def per_tile(t, carry):
                @pl.when(t >= n_active_ref[0])
                def _():
                    for part in range(tme // zrows):
                        row0 = pl.multiple_of(t * tme + part * zrows, zrows)
                        act(pltpu.make_async_copy(zero_ref, xs_ref.at[pl.ds(row0, zrows), :], zsem))

                return carry

            lax.fori_loop(0, N_EXPERTS, per_expert, 0)
            lax.fori_loop(0, n_tiles, per_tile, 0)

        fills(lambda cp: cp.start())
        fills(lambda cp: cp.wait())

    def start(tb, carry):
        for u in range(ROW_DMA_UNROLL):
            t = tb * ROW_DMA_UNROLL + u
            for k in range(EXPERT_TOP_K):
                dst = pos_ref[0, 0, k * tm + t]
                pltpu.make_async_copy(hn_ref.at[pl.ds(t, 1), :], xs_ref.at[pl.ds(dst, 1), :],
                                      sem).start(priority=(u + k) % 2)
        return carry

    lax.fori_loop(0, tm // ROW_DMA_UNROLL, start, 0)
    for k in range(EXPERT_TOP_K):
        pltpu.make_async_copy(hn_ref, xs_ref.at[pl.ds(0, tm), :], sem).wait()


def _scatter(hn, pos_tiles, pad_start, pad_len, n_active, *, tm, tme, n_tiles):
    n, d = hn.shape
    assert tme % 2 == 0 and (tme // 2) & (tme // 2 - 1) == 0
    grid_spec = pltpu.PrefetchScalarGridSpec(
        num_scalar_prefetch=3,
        grid=(n // tm,),
        in_specs=[
            pl.BlockSpec((1, 1, EXPERT_TOP_K * tm), lambda i, *_: (i, 0, 0),
                         memory_space=pltpu.SMEM),
            pl.BlockSpec((tm, d), lambda i, *_: (i, 0)),
        ],
        out_specs=pl.BlockSpec(memory_space=pl.ANY),
        scratch_shapes=[pltpu.VMEM((tme // 2, d), hn.dtype), pltpu.SemaphoreType.DMA(()),
                        pltpu.SemaphoreType.DMA(())],
    )
    return pl.pallas_call(
        functools.partial(_scatter_kernel, tme=tme, n_tiles=n_tiles),
        grid_spec=grid_spec,
        out_shape=jax.ShapeDtypeStruct((n_tiles * tme, d), hn.dtype),
        compiler_params=_params("arbitrary"),
        name="scatter",
    )(pad_start, pad_len, n_active, pos_tiles, hn)


W_CHUNK_ROWS = 256
W_RING = 8


def _experts_kernel(te_ref, slot_ref, nxt_ref, lo_ref, hi_ref, na_ref, xs_ref, fnw_ref, wg_hbm,
                    wu_hbm, wd_hbm, ys_ref, wres_ref, stage_ref, sem, *, n_gu, n_r, n_h):
    i = pl.program_id(0)
    n_chunks = 2 * n_gu + n_h * n_r
    ring, cr, f = stage_ref.shape
    d = n_gu * cr

    def stage_copy(src, c):
        return pltpu.make_async_copy(src, stage_ref.at[c % ring], sem.at[c % ring])

    def start_chunk(e, c):
        @pl.when(c < n_gu)
        def _():
            stage_copy(wg_hbm.at[e, pl.ds(pl.multiple_of(c * cr, cr), cr), :], c).start()

        @pl.when((c >= n_gu) & (c < 2 * n_gu))
        def _():
            stage_copy(wu_hbm.at[e, pl.ds(pl.multiple_of((c - n_gu) * cr, cr), cr), :], c).start()

        @pl.when(c >= 2 * n_gu)
        def _():
            q = c - 2 * n_gu
            h = q // n_r
            r = q - h * n_r
            stage_copy(wd_hbm.at[e, pl.ds(pl.multiple_of(r * cr, cr), cr),
                                 pl.ds(pl.multiple_of(h * f, f), f)], c).start()

    def start_first(e):
        def body(c, carry):
            start_chunk(e, c)
            return carry

        lax.fori_loop(0, ring, body, 0)

    def convert(e, lo, hi, dst_slot):
        def body(c, carry):
            stage_copy(wg_hbm.at[0, pl.ds(0, cr), :], c).wait()
            wres_ref[dst_slot, pl.ds(pl.multiple_of(c * cr, cr), cr), :] = (
                stage_ref[c % ring].astype(BF16))

            @pl.when(c + ring < n_chunks)
            def _():
                start_chunk(e, c + ring)

            return carry

        lax.fori_loop(lo, hi, body, 0)

    @pl.when(i == 0)
    def _():
        start_first(te_ref[0])
        convert(te_ref[0], 0, n_chunks, slot_ref[0])

    @pl.when(i < na_ref[0])
    def _():
        slot = slot_ref[i]
        nxt = nxt_ref[i]

        @pl.when((nxt >= 0) & (lo_ref[i] == 0))
        def _():
            start_first(nxt)

        xr = xs_ref[...]
        ms = jnp.mean(xr * xr, axis=-1, keepdims=True)
        xb = (xr * lax.rsqrt(ms + NORM_EPS) * fnw_ref[...]).astype(BF16)
        hg = jnp.dot(xb, wres_ref[slot, 0:d, :], preferred_element_type=F32)
        hu = jnp.dot(xb, wres_ref[slot, d:2 * d, :], preferred_element_type=F32)
        hmid = (_silu(hg) * hu).astype(BF16)
        for h in range(n_h):
            r0 = 2 * d + h * n_r * cr
            ys_ref[:, h * f:(h + 1) * f] = jnp.dot(hmid, wres_ref[slot, r0:r0 + n_r * cr, :],
                                                    preferred_element_type=F32)

        @pl.when(nxt >= 0)
        def _():
            convert(nxt, lo_ref[i], hi_ref[i], 1 - slot)

    @pl.when(i >= na_ref[0])
    def _():
        ys_ref[...] = jnp.zeros(ys_ref.shape, F32)


def _experts(xs, norm_w, tile_expert, n_active, cnt, starts, padded, w_g, w_u, w_d, *, tm):
    p, d = xs.shape
    ne, _, f = w_g.shape
    n_tiles = p // tm
    cr = W_CHUNK_ROWS
    assert d % cr == 0 and f % cr == 0 and d % f == 0
    n_gu, n_r, n_h = d // cr, f // cr, d // f
    n_chunks = 2 * n_gu + n_h * n_r

    ids = jnp.arange(ne, dtype=jnp.int32)
    present = cnt > 0
    later = jnp.where(present[None, :] & (ids[None, :] > ids[:, None]), ids[None, :], ne)
    nxt_e = jnp.min(later, axis=1)
    nxt_e = jnp.where(nxt_e >= ne, -1, nxt_e)
    run_e = jnp.cumsum(present.astype(jnp.int32)) - 1
    onehot = (tile_expert[:, None] == ids[None, :]).astype(jnp.int32)
    pick = lambda v: jnp.sum(onehot * v[None, :].astype(jnp.int32), axis=1)
    tiles = jnp.arange(n_tiles, dtype=jnp.int32)
    active = tiles < n_active[0]
    j = tiles - pick(starts) // tm
    k = jnp.maximum(pick(padded) // tm, 1)
    nxt_t = jnp.where(active, pick(nxt_e), -1).astype(jnp.int32)
    lo_t = jnp.where(active, (n_chunks * j) // k, 0).astype(jnp.int32)
    hi_t = jnp.where(active, (n_chunks * (j + 1)) // k, 0).astype(jnp.int32)
    slot_t = (pick(run_e) % 2).astype(jnp.int32)

    def rows(i, te, sl, nx, lo, hi, na):
        return (jnp.minimum(i, na[0] - 1), 0)

    grid_spec = pltpu.PrefetchScalarGridSpec(
        num_scalar_prefetch=6,
        grid=(n_tiles,),
        in_specs=[
            pl.BlockSpec((tm, d), rows),
            pl.BlockSpec((1, d), lambda i, *_: (0, 0)),
            pl.BlockSpec(memory_space=pl.ANY),
            pl.BlockSpec(memory_space=pl.ANY),
            pl.BlockSpec(memory_space=pl.ANY),
        ],
        out_specs=pl.BlockSpec((tm, d), lambda i, *_: (i, 0)),
        scratch_shapes=[
            pltpu.VMEM((2, n_chunks * cr, f), BF16),
            pltpu.VMEM((W_RING, cr, f), F32),
            pltpu.SemaphoreType.DMA((W_RING,)),
        ],
    )
    return pl.pallas_call(
        functools.partial(_experts_kernel, n_gu=n_gu, n_r=n_r, n_h=n_h),
        grid_spec=grid_spec,
        out_shape=jax.ShapeDtypeStruct((p, d), F32),
        compiler_params=_params("arbitrary"),
        name="experts",
    )(tile_expert, slot_t, nxt_t, lo_t, hi_t, n_active, xs, norm_w.reshape(1, d), w_g, w_u, w_d)


def _combine_kernel(pos_ref, pos_next_ref, ys_ref, x1_ref, gcol_ref, nw_ref, out_ref, buf_ref, sem,
                    *, final):
    i = pl.program_id(0)
    n_steps = pl.num_programs(0)
    tm = x1_ref.shape[0]

    def issue(p_ref, slot):
        def start(tb, carry):
            for u in range(ROW_DMA_UNROLL):
                t = tb * ROW_DMA_UNROLL + u
                for k in range(EXPERT_TOP_K):
                    src = p_ref[0, 0, k * tm + t]
                    pltpu.make_async_copy(ys_ref.at[pl.ds(src, 1), :],
                                          buf_ref.at[slot, k, pl.ds(t, 1), :],
                                          sem.at[slot]).start(priority=(u + k) % 2)
            return carry

        lax.fori_loop(0, tm // ROW_DMA_UNROLL, start, 0)

    slot = i % 2

    @pl.when(i == 0)
    def _():
        issue(pos_ref, 0)

    @pl.when(i + 1 < n_steps)
    def _():
        issue(pos_next_ref, 1 - slot)

    for k in range(EXPERT_TOP_K):
        pltpu.make_async_copy(ys_ref.at[pl.ds(0, tm), :], buf_ref.at[slot, k], sem.at[slot]).wait()
    g = gcol_ref[...]
    xo = x1_ref[...] + g[:, 0:1] * buf_ref[slot, 0] + g[:, 1:2] * buf_ref[slot, 1]
    if final:
        ms = jnp.mean(xo * xo, axis=-1, keepdims=True)
        xo = xo * lax.rsqrt(ms + NORM_EPS) * nw_ref[...]
    out_ref[...] = xo


def _combine(ys, pos_tiles, x1, gcol, norm_w, *, tm, final):
    n, d = x1.shape
    last = n // tm - 1
    return pl.pallas_call(
        functools.partial(_combine_kernel, final=final),
        grid=(n // tm,),
        in_specs=[
            pl.BlockSpec((1, 1, EXPERT_TOP_K * tm), lambda i: (i, 0, 0), memory_space=pltpu.SMEM),
            pl.BlockSpec((1, 1, EXPERT_TOP_K * tm), lambda i: (jnp.minimum(i + 1, last), 0, 0),
                         memory_space=pltpu.SMEM),
            pl.BlockSpec(memory_space=pl.ANY),
            pl.BlockSpec((tm, d), lambda i: (i, 0)),
            pl.BlockSpec((tm, LANES), lambda i: (i, 0)),
            pl.BlockSpec((1, d), lambda i: (0, 0)),
        ],
        out_specs=pl.BlockSpec((tm, d), lambda i: (i, 0)),
        out_shape=jax.ShapeDtypeStruct((n, d), F32),
        scratch_shapes=[pltpu.VMEM((2, EXPERT_TOP_K, tm, d), F32), pltpu.SemaphoreType.DMA((2,))],
        compiler_params=_params("arbitrary"),
        name="combine",
    )(pos_tiles, pos_tiles, ys, x1, gcol, norm_w.reshape(1, d))


def _tiles(n):
    return dict(proj_tm=math.gcd(n, 1024), mix_tm=math.gcd(n, 512), moe_tm=256,
                route_tm=math.gcd(n, 512))


def _layer(x2, batch, seq, layer, attn_norm_w, w_in_stack, b_gate, conv_w, conv_b, dt_bias, a_log,
           d_skip, ssd_norm_w, w_ssd_out, w_attn_out, w_out, ffn_norm_w, w_gr, b_gr, w_er, b_er,
           w_g, w_u, w_d):
    n, d = x2.shape
    n_heads = dt_bias.shape[0]
    d_inner = ssd_norm_w.shape[0]
    conv_dim = conv_w.shape[1]
    aw_total = ATTN_HEADS_PER_GROUP * len(DILATION_PATTERNS) * ATTN_HEAD_DIM
    gw = ATTN_HEADS_PER_GROUP * ATTN_HEAD_DIM
    tiles = _tiles(n)

    c_z, c_xbc, c_dt = d_inner, d_inner + conv_dim, d_inner + conv_dim + n_heads
    c_gate = c_dt + QKV_PARTS * aw_total
    segments = [(c_gate, 2 * d), (c_z, conv_dim), (0, d_inner)]
    segments += [(c_dt + p * aw_total + gi * gw, gw)
                 for gi in range(len(DILATION_PATTERNS)) for p in range(QKV_PARTS)]
    assert all(width % gw == 0 for _, width in segments)
    starts = tuple(start + b * gw for start, width in segments for b in range(width // gw))
    w_main, w_dt = _regroup_weight(w_in_stack, layer, starts, gw, c_xbc, n_heads)
    off_gate, off_xbc, off_z = 0, 2 * d, 2 * d + conv_dim
    plain_cols = off_z + d_inner
    assert off_xbc % conv_dim == 0 and off_z % d_inner == 0 and plain_cols % gw == 0

    proj, qkv0, qkv1, qkv2, dt, dtt = _in_proj(x2, attn_norm_w, w_main, w_dt, tm=tiles["proj_tm"],
                                               tn=gw, tp=2 * gw, plain_cols=plain_cols, nh=n_heads)

    yn = _ssd(proj, dt, dtt, conv_w, conv_b, dt_bias, a_log, d_skip, ssd_norm_w, batch=batch,
              seq=seq, d_inner=d_inner, xbc_block=off_xbc // conv_dim, z_block=off_z // d_inner)

    outs, lses = [], []
    for gi, qkv in enumerate((qkv0, qkv1, qkv2)):
        o_g, lse_g = _attn_group(qkv, gi, batch=batch, seq=seq)
        outs.append(o_g)
        lses.append(lse_g)

    merged = _merge(yn, outs, lses, proj, w_ssd_out, w_attn_out, b_gate, tm=tiles["mix_tm"],
                    gate_block=off_gate // (2 * d))
    x1, eid, gcol, rank, counts = _route(merged, x2, w_out, ffn_norm_w, (w_gr, w_er),
                                         (b_gr, b_er), tm=tiles["mix_tm"])

    tme = tiles["moe_tm"]
    cnt = counts[:, 0]
    padded = ((cnt + tme - 1) // tme) * tme
    ends = jnp.cumsum(padded)
    starts = ends - padded
    experts = jnp.arange(N_EXPERTS, dtype=jnp.int32)[:, None, None]
    pos = rank[:EXPERT_TOP_K] + jnp.sum(
        jnp.where(eid[None, :EXPERT_TOP_K] == experts, starts[:, None, None], 0), axis=0)
    n_tiles = EXPERT_TOP_K * n // tme + N_EXPERTS
    tile_start = jnp.arange(n_tiles, dtype=jnp.int32) * tme
    tile_expert = jnp.minimum(jnp.sum(ends[None, :] <= tile_start[:, None], axis=1),
                              N_EXPERTS - 1).astype(jnp.int32)
    n_active = (ends[-1:] // tme).astype(jnp.int32)

    rtm = tiles["route_tm"]
    pos_tiles = pos.reshape(EXPERT_TOP_K, n // rtm, rtm).transpose(1, 0, 2).reshape(
        n // rtm, 1, EXPERT_TOP_K * rtm)
    xs = _scatter(x1, pos_tiles, (starts + cnt).astype(jnp.int32), (padded - cnt).astype(jnp.int32),
                  n_active, tm=rtm, tme=tme, n_tiles=n_tiles)
    ys = _experts(xs, ffn_norm_w, tile_expert, n_active, cnt, starts, padded, w_g, w_u, w_d, tm=tme)
    return ys, pos_tiles, x1, gcol, rtm


def kernel(x, attn_norm_w, w_in, b_gate, conv_w, conv_b, dt_bias, a_log, d_skip, ssd_norm_w,
           w_ssd_out, w_attn_out, w_out, ffn_norm_w, w_group_router, b_group_router,
           w_expert_router, b_expert_router, w_exp_gate, w_exp_up, w_exp_down, final_norm_w):
    batch, seq, d = x.shape
    depth = w_in.shape[0]
    x2 = x.reshape(batch * seq, d)
    for layer in range(depth):
        ys, pos_tiles, x1, gcol, rtm = _layer(
            x2, batch, seq, layer, attn_norm_w[layer], w_in, b_gate[layer], conv_w[layer],
            conv_b[layer], dt_bias[layer], a_log[layer], d_skip[layer], ssd_norm_w[layer],
            w_ssd_out[layer], w_attn_out[layer], w_out[layer], ffn_norm_w[layer],
            w_group_router[layer], b_group_router[layer], w_expert_router[layer],
            b_expert_router[layer], w_exp_gate[layer], w_exp_up[layer], w_exp_down[layer])
        x2 = _combine(ys, pos_tiles, x1, gcol, final_norm_w, tm=rtm, final=layer == depth - 1)
    return x2.reshape(batch, seq, d)
```

```python
import functools
import math

import jax
import jax.numpy as jnp
import numpy as np
from jax import lax
from jax.experimental import pallas as pl
from jax.experimental.pallas import tpu as pltpu

F32 = jnp.float32
BF16 = jnp.bfloat16

NORM_EPS = 1e-6
SSD_HEAD_DIM = 64
SSD_N_GROUPS = 8
SSD_D_STATE = 128
SSD_CONV_WIDTH = 4
SSD_CHUNK = 128
ATTN_HEAD_DIM = 128
DILATION_PATTERNS = ((128, 1), (512, 4), (2048, 16))
ATTN_HEADS_PER_GROUP = 4
ATTN_BLOCK = 128
ATTN_BLOCKS_PER_STEP = 4
N_EXPERT_GROUPS = 4
EXPERTS_PER_GROUP = 8
N_EXPERTS = N_EXPERT_GROUPS * EXPERTS_PER_GROUP
EXPERT_TOP_K = 2

LANES = 128
SUBLANES = 8
VMEM_LIMIT_BYTES = 56 * 1024 * 1024
ROW_DMA_UNROLL = 8

ROUTER_EXPERT_ROW0 = SUBLANES
ROUTER_ROWS = ROUTER_EXPERT_ROW0 + N_EXPERTS


def _params(*semantics):
    return pltpu.CompilerParams(dimension_semantics=semantics, vmem_limit_bytes=VMEM_LIMIT_BYTES)


def _split_bf16(v):
    hi = v.astype(BF16)
    lo = (v - hi.astype(F32)).astype(BF16)
    return hi, lo


def _sigmoid(v):
    return 0.5 + 0.5 * jnp.tanh(0.5 * v)


def _silu(v):
    h = 0.5 * v
    return h + h * jnp.tanh(h)


def _regroup_weight_kernel(starts_ref, wt_hbm, out_ref, narrow_ref, buf_ref, nbuf_ref, sem, nsem,
                           *, layer, narrow_start, nh):
    i = pl.program_id(0)
    n_steps = pl.num_programs(0)
    tn = buf_ref.shape[1]

    def fetch(step, slot):
        rows = pl.ds(pl.multiple_of(starts_ref[step], SUBLANES), tn)
        return pltpu.make_async_copy(wt_hbm.at[layer, rows, :], buf_ref.at[slot], sem.at[slot])

    narrow = pltpu.make_async_copy(wt_hbm.at[layer, pl.ds(narrow_start, LANES), :], nbuf_ref, nsem)

    @pl.when(i == 0)
    def _():
        fetch(0, 0).start()
        narrow.start()

    @pl.when(i + 1 < n_steps)
    def _():
        fetch(i + 1, (i + 1) % 2).start()

    fetch(i, i % 2).wait()
    out_ref[...] = buf_ref[i % 2].T.astype(out_ref.dtype)

    @pl.when(i == n_steps - 1)
    def _():
        narrow.wait()
        lane = lax.broadcasted_iota(jnp.int32, narrow_ref.shape, 1)
        narrow_ref[...] = jnp.where(lane < nh, nbuf_ref[...].T, 0.0).astype(narrow_ref.dtype)


def _regroup_weight(w_stack, layer, starts, tn, narrow_start, nh):
    _, k, cols = w_stack.shape
    assert all(s % SUBLANES == 0 and s + tn <= cols for s in starts)
    assert narrow_start % SUBLANES == 0 and narrow_start + LANES <= cols and nh <= LANES
    wt = jnp.swapaxes(w_stack, 1, 2)
    grid_spec = pltpu.PrefetchScalarGridSpec(
        num_scalar_prefetch=1,
        grid=(len(starts),),
        in_specs=[pl.BlockSpec(memory_space=pl.ANY)],
        out_specs=[pl.BlockSpec((k, tn), lambda i, *_: (0, i)),
                   pl.BlockSpec((k, LANES), lambda i, *_: (0, 0))],
        scratch_shapes=[pltpu.VMEM((2, tn, k), F32), pltpu.VMEM((LANES, k), F32),
                        pltpu.SemaphoreType.DMA((2,)), pltpu.SemaphoreType.DMA(())],
    )
    return pl.pallas_call(
        functools.partial(_regroup_weight_kernel, layer=layer, narrow_start=narrow_start, nh=nh),
        grid_spec=grid_spec,
        out_shape=[jax.ShapeDtypeStruct((k, len(starts) * tn), BF16),
                   jax.ShapeDtypeStruct((k, LANES), BF16)],
        compiler_params=_params("arbitrary"),
        name="regroup_weight",
    )(jnp.asarray(starts, jnp.int32), wt)


QKV_PARTS = 3


def _in_proj_kernel(x_ref, nw_ref, wp_ref, w_ref, wdt_ref, proj_ref, a0_ref, a1_ref, a2_ref,
                    dt_ref, dtt_ref, h_ref, stage_ref, *, n_plain):
    j = pl.program_id(1)
    tm = proj_ref.shape[0]
    tn = w_ref.shape[1]

    @pl.when(j == 0)
    def _():
        xf = x_ref[...]
        ms = jnp.mean(xf * xf, axis=-1, keepdims=True)
        h = (xf * lax.rsqrt(ms + NORM_EPS) * nw_ref[...]).astype(BF16)
        h_ref[...] = h
        nh = dt_ref.shape[1]
        dt_wide = jnp.dot(h, wdt_ref[...], preferred_element_type=F32)
        dt_ref[...] = dt_wide[:, :nh]
        dtt_ref[...] = dt_wide.T[:nh, :]

    @pl.when(j < n_plain)
    def _():
        proj_ref[...] = jnp.dot(h_ref[...], wp_ref[...], preferred_element_type=F32).astype(BF16)

    for gi, a_ref in enumerate((a0_ref, a1_ref, a2_ref)):
        dil = DILATION_PATTERNS[gi][1]
        j0 = n_plain + QKV_PARTS * gi

        @pl.when((j >= j0) & (j < j0 + QKV_PARTS))
        def _(a_ref=a_ref, dil=dil):
            res = jnp.dot(h_ref[...], w_ref[...], preferred_element_type=F32)
            if dil == 1:
                a_ref[...] = res.astype(BF16)
            else:
                for s in range(tn // LANES):
                    stage_ref[s] = res[:, s * LANES:(s + 1) * LANES]
                for r in range(dil):
                    for s in range(tn // LANES):
                        c0 = r * tn + s * LANES
                        a_ref[:, c0:c0 + LANES] = stage_ref[
                            s, pl.ds(r, tm // dil, stride=dil), :].astype(BF16)


def _in_proj(x2, norm_w, w_main, w_dt, *, tm, tn, tp, plain_cols, nh):
    n, d = x2.shape
    assert plain_cols % tp == 0 and plain_cols % tn == 0
    n_plain = plain_cols // tp
    qkv_block0 = plain_cols // tn
    n_blocks = n_plain + QKV_PARTS * len(DILATION_PATTERNS)
    assert w_main.shape[1] == plain_cols + QKV_PARTS * len(DILATION_PATTERNS) * tn

    def a_spec(gi):
        dil = DILATION_PATTERNS[gi][1]
        j0 = n_plain + QKV_PARTS * gi
        return pl.BlockSpec((tm // dil, dil * tn),
                            lambda i, j: (i, jnp.clip(j - j0, 0, QKV_PARTS - 1)))

    def a_shape(gi):
        dil = DILATION_PATTERNS[gi][1]
        return jax.ShapeDtypeStruct((n // dil, dil * QKV_PARTS * tn), BF16)

    return pl.pallas_call(
        functools.partial(_in_proj_kernel, n_plain=n_plain),
        grid=(n // tm, n_blocks),
        in_specs=[
            pl.BlockSpec((tm, d), lambda i, j: (i, 0)),
            pl.BlockSpec((1, d), lambda i, j: (0, 0)),
            pl.BlockSpec((d, tp), lambda i, j: (0, jnp.minimum(j, n_plain - 1))),
            pl.BlockSpec((d, tn), lambda i, j: (0, qkv_block0 + jnp.maximum(j - n_plain, 0))),
            pl.BlockSpec((d, LANES), lambda i, j: (0, 0)),
        ],
        out_specs=[
            pl.BlockSpec((tm, tp), lambda i, j: (i, jnp.minimum(j, n_plain - 1))),
            a_spec(0), a_spec(1), a_spec(2),
            pl.BlockSpec((tm, nh), lambda i, j: (i, 0)),
            pl.BlockSpec((nh, tm), lambda i, j: (0, i)),
        ],
        out_shape=[
            jax.ShapeDtypeStruct((n, plain_cols), BF16),
            a_shape(0), a_shape(1), a_shape(2),
            jax.ShapeDtypeStruct((n, nh), F32),
            jax.ShapeDtypeStruct((nh, n), F32),
        ],
        scratch_shapes=[pltpu.VMEM((tm, d), BF16), pltpu.VMEM((tn // LANES, tm, LANES), F32)],
        compiler_params=_params("arbitrary", "arbitrary"),
        name="in_proj",
    )(x2, norm_w.reshape(1, d), w_main, w_main, w_dt)


LOG2E = math.log2(math.e)
CONV_ROW_PITCH = 2


def _ssd_kernel(xbc_ref, z_ref, dt_ref, dtt_ref, cw_ref, cb_ref, dtb_ref, dtbt_ref, alog_ref,
                alogt_ref, dskip_ref, nw_ref, expand_ref, out_ref,
                xbuf_ref, state_ref, y_ref, *, n_heads, d_inner):
    L = SSD_CHUNK
    P = SSD_HEAD_DIM
    NS = SSD_D_STATE
    G = SSD_N_GROUPS
    R = n_heads // G
    GW = R * P
    W = SSD_CONV_WIDTH
    RP = CONV_ROW_PITCH
    n_slabs = xbuf_ref.shape[0]
    c = pl.program_id(1)

    def rows(first, count):
        return pl.ds(first * RP, count, stride=RP)

    @pl.when(c == 0)
    def _():
        state_ref[...] = jnp.zeros(state_ref.shape, F32)
        for s in range(n_slabs):
            xbuf_ref[s, rows(0, SUBLANES), :] = jnp.zeros((SUBLANES, LANES), F32)

    @pl.when(c > 0)
    def _():
        for s in range(n_slabs):
            xbuf_ref[s, rows(0, SUBLANES), :] = xbuf_ref[s, rows(L, SUBLANES), :]

    for s in range(n_slabs):
        xbuf_ref[s, rows(SUBLANES, L), :] = xbc_ref[:, s * LANES:(s + 1) * LANES].astype(F32)

    def conv(col0, width):
        slabs = []
        for s in range(col0 // LANES, (col0 + width) // LANES):
            acc = cb_ref[:, s * LANES:(s + 1) * LANES]
            for w in range(W):
                acc = acc + (cw_ref[w:w + 1, s * LANES:(s + 1) * LANES]
                             * xbuf_ref[s, rows(SUBLANES - (W - 1) + w, L), :])
            slabs.append(acc)
        return _silu(jnp.concatenate(slabs, axis=1))

    def softplus(v):
        return jnp.maximum(v, 0.0) + jnp.log1p(jnp.exp(-jnp.abs(v)))

    dt = softplus(dt_ref[...] + dtb_ref[...])
    dtt = softplus(dtt_ref[...] + dtbt_ref[...])
    da = dt * (-LOG2E * jnp.exp(alog_ref[...]))
    dat = dtt * (-LOG2E * jnp.exp(alogt_ref[...]))
    row = lax.broadcasted_iota(jnp.int32, (L, L), 0)
    col = lax.broadcasted_iota(jnp.int32, (L, L), 1)
    causal = row >= col
    tri = jnp.where(causal, 1.0, 0.0).astype(BF16)
    trit = jnp.where(row <= col, 1.0, 0.0).astype(BF16)

    def split3(v):
        a = v.astype(BF16)
        r1 = v - a.astype(F32)
        b = r1.astype(BF16)
        cc = (r1 - b.astype(F32)).astype(BF16)
        return a, b, cc

    a2 = sum(jnp.dot(tri, p, preferred_element_type=F32) for p in split3(da))
    a2t = sum(jnp.dot(p, trit, preferred_element_type=F32) for p in split3(dat))
    a2_last = a2[L - 1:L, :]

    expand = expand_ref[...]

    def expand_heads(v):
        hi, lo = _split_bf16(v)
        return jnp.dot(jnp.concatenate([hi, lo], axis=1), expand, preferred_element_type=F32)

    in_scale_e = expand_heads(dt * jnp.exp2(a2_last - a2))
    tail8 = jnp.concatenate([jnp.exp2(a2_last), dskip_ref[...],
                             jnp.zeros((SUBLANES - 2, n_heads), F32)], axis=0)
    out_e = expand_heads(jnp.concatenate([jnp.exp2(a2), tail8], axis=0))
    out_scale_e = out_e[0:L, :]
    chunk_decay_e = out_e[L:L + 1, :]
    dskip_e = out_e[L + 1:L + 2, :]

    first_head = lax.broadcasted_iota(jnp.int32, (L, LANES), 1) < P

    for g in range(G):
        x0 = g * GW
        xs = conv(x0, GW)
        bm = conv(d_inner + g * NS, NS).astype(BF16)
        cm = conv(d_inner + G * NS + g * NS, NS).astype(BF16)
        cb = lax.dot_general(cm, bm, (((1,), (1,)), ((), ())), preferred_element_type=F32)
        cb = jnp.where(causal, cb, 0.0)
        y_parts = []
        for pr in range(GW // LANES):
            wgts = []
            for q in range(LANES // P):
                h = g * R + pr * (LANES // P) + q
                seg = a2[:, h:h + 1] - a2t[h:h + 1, :]
                decay = jnp.exp2(jnp.minimum(seg, 0.0))
                wgts.append((cb * decay * dtt[h:h + 1, :]).astype(BF16))
            slab = xs[:, pr * LANES:(pr + 1) * LANES]
            rhs = jnp.concatenate([jnp.where(first_head, slab, 0.0).astype(BF16),
                                   jnp.where(first_head, 0.0, slab).astype(BF16)], axis=0)
            y_parts.append(jnp.dot(jnp.concatenate(wgts, axis=1), rhs, preferred_element_type=F32))
        y = jnp.concatenate(y_parts, axis=1) + xs * dskip_e[:, x0:x0 + GW]
        st = state_ref[:, x0:x0 + GW]
        y = y + (jnp.dot(cm, st.astype(BF16), preferred_element_type=F32)
                 * out_scale_e[:, x0:x0 + GW])
        y_ref[:, x0:x0 + GW] = y
        xin = (xs * in_scale_e[:, x0:x0 + GW]).astype(BF16)
        st_new = lax.dot_general(bm, xin, (((0,), (0,)), ((), ())), preferred_element_type=F32)
        state_ref[:, x0:x0 + GW] = st * chunk_decay_e[:, x0:x0 + GW] + st_new

    yz = y_ref[...] * _silu(z_ref[...].astype(F32))
    ms = jnp.mean(yz * yz, axis=-1, keepdims=True)
    out_ref[...] = (yz * lax.rsqrt(ms + NORM_EPS) * nw_ref[...]).astype(out_ref.dtype)


def _ssd(proj, dt, dtt, conv_w, conv_b, dt_bias, a_log, d_skip, norm_w, *, batch, seq, d_inner,
         xbc_block, z_block):
    n = batch * seq
    n_heads = dt.shape[1]
    L = SSD_CHUNK
    nc = seq // L
    conv_dim = conv_w.shape[1]
    expand = (np.arange(d_inner)[None, :] // SSD_HEAD_DIM == np.arange(n_heads)[:, None])
    expand = jnp.asarray(np.concatenate([expand, expand], axis=0), BF16)
    assert conv_dim % LANES == 0
    kern = functools.partial(_ssd_kernel, n_heads=n_heads, d_inner=d_inner)
    small = lambda shape: pl.BlockSpec(shape, lambda b, c: (0, 0))
    return pl.pallas_call(
        kern,
        grid=(batch, nc),
        in_specs=[
            pl.BlockSpec((L, conv_dim), lambda b, c: (b * nc + c, xbc_block)),
            pl.BlockSpec((L, d_inner), lambda b, c: (b * nc + c, z_block)),
            pl.BlockSpec((L, n_heads), lambda b, c: (b * nc + c, 0)),
            pl.BlockSpec((n_heads, L), lambda b, c: (0, b * nc + c)),
            small((SSD_CONV_WIDTH, conv_dim)),
            small((1, conv_dim)),
            small((1, n_heads)),
            small((n_heads, 1)),
            small((1, n_heads)),
            small((n_heads, 1)),
            small((1, n_heads)),
            small((1, d_inner)),
            small((2 * n_heads, d_inner)),
        ],
        out_specs=pl.BlockSpec((L, d_inner), lambda b, c: (b * nc + c, 0)),
        out_shape=jax.ShapeDtypeStruct((n, d_inner), BF16),
        scratch_shapes=[
            pltpu.VMEM((conv_dim // LANES, CONV_ROW_PITCH * (L + SUBLANES), LANES), F32),
            pltpu.VMEM((SSD_D_STATE, d_inner), F32),
            pltpu.VMEM((L, d_inner), F32),
        ],
        compiler_params=_params("arbitrary", "arbitrary"),
        name="ssd",
    )(proj, proj, dt, dtt, conv_w, conv_b.reshape(1, -1), dt_bias.reshape(1, -1),
      dt_bias.reshape(-1, 1), a_log.reshape(1, -1), a_log.reshape(-1, 1), d_skip.reshape(1, -1),
      norm_w.reshape(1, -1), expand)


def _attn_kernel(q_ref, k_ref, v_ref, o_ref, lse_ref, kp_ref, vp_ref, *, slopes, dilation, hops):
    nb = pl.program_id(2)
    BLK = ATTN_BLOCK
    E = ATTN_HEAD_DIM

    @pl.when(nb == 0)
    def _():
        kp_ref[...] = jnp.zeros(kp_ref.shape, kp_ref.dtype)
        vp_ref[...] = jnp.zeros(vp_ref.shape, vp_ref.dtype)

    scale = E ** -0.5
    nt = (((1,), (1,)), ((), ()))
    nblk = q_ref.shape[0] // BLK
    q = q_ref[...]
    kk = jnp.concatenate([kp_ref[...], k_ref[...]], axis=0)
    vv = jnp.concatenate([vp_ref[...], v_ref[...]], axis=0)
    units = [(j, h) for j in range(nblk) for h in range(len(slopes))]
    cols = lambda h: slice(h * E, (h + 1) * E)
    rows = lambda j: slice(j * BLK, (j + 1) * BLK)
    qi = lax.broadcasted_iota(jnp.int32, (BLK, BLK), 0)
    ki = lax.broadcasted_iota(jnp.int32, (BLK, BLK), 1)
    rel_cur = qi - ki
    rel_prev = rel_cur + BLK
    first = rel_prev <= jnp.where(nb > 0, hops, -1)
    later = rel_prev <= hops
    ok_cur = jnp.concatenate([rel_cur >= 0 for _ in units], axis=0)
    ok_prev = jnp.concatenate([first if j == 0 else later for j, _ in units], axis=0)
    dist_cur = (rel_cur * dilation).astype(F32)
    dist_prev = (rel_prev * dilation).astype(F32)
    bias_cur = jnp.concatenate([slopes[h] * dist_cur for _, h in units], axis=0)
    bias_prev = jnp.concatenate([slopes[h] * dist_prev for _, h in units], axis=0)
    s_cur = jnp.concatenate(
        [lax.dot_general(q[rows(j), cols(h)], kk[rows(j + 1), cols(h)], nt,
                         preferred_element_type=F32) for j, h in units], axis=0)
    s_prev = jnp.concatenate(
        [lax.dot_general(q[rows(j), cols(h)], kk[rows(j), cols(h)], nt,
                         preferred_element_type=F32) for j, h in units], axis=0)
    l_cur = jnp.where(ok_cur, s_cur * scale - bias_cur, -jnp.inf)
    l_prev = jnp.where(ok_prev, s_prev * scale - bias_prev, -jnp.inf)
    m = jnp.max(jnp.maximum(l_cur, l_prev), axis=-1, keepdims=True)
    p_cur = jnp.exp(l_cur - m)
    p_prev = jnp.exp(l_prev - m)
    den = jnp.sum(p_cur + p_prev, axis=-1, keepdims=True)
    p_cur = p_cur.astype(BF16)
    p_prev = p_prev.astype(BF16)
    inv = 1.0 / den
    lse = m + jnp.log(den)
    lane = lax.broadcasted_iota(jnp.int32, (BLK, LANES), 1)
    for j in range(nblk):
        lse_tile = jnp.zeros((BLK, LANES), F32)
        for h in range(len(slopes)):
            u = rows(units.index((j, h)))
            acc = (jnp.dot(p_cur[u], vv[rows(j + 1), cols(h)], preferred_element_type=F32)
                   + jnp.dot(p_prev[u], vv[rows(j), cols(h)], preferred_element_type=F32))
            o_ref[rows(j), cols(h)] = (acc * inv[u]).astype(o_ref.dtype)
            lse_tile = jnp.where(lane == h, lse[u], lse_tile)
        lse_ref[rows(j), :] = lse_tile
    kp_ref[...] = k_ref[rows(nblk - 1), :]
    vp_ref[...] = v_ref[rows(nblk - 1), :]


def _attn_group(qkv, gi, *, batch, seq):
    window, dilation = DILATION_PATTERNS[gi]
    hops = window // dilation
    n_heads_total = ATTN_HEADS_PER_GROUP * len(DILATION_PATTERNS)
    slopes = tuple(float(2.0 ** (-8.0 * (gi * ATTN_HEADS_PER_GROUP + h + 1) / n_heads_total))
                   for h in range(ATTN_HEADS_PER_GROUP))
    gw = ATTN_HEADS_PER_GROUP * ATTN_HEAD_DIM
    assert seq % (dilation * ATTN_BLOCK) == 0
    sub = seq // dilation
    rows = ATTN_BLOCK * math.gcd(sub // ATTN_BLOCK, ATTN_BLOCKS_PER_STEP)
    nb = sub // rows
    kern = functools.partial(_attn_kernel, slopes=slopes, dilation=dilation, hops=hops)

    def part(p):
        return pl.BlockSpec((rows, gw), lambda b, r, n: (b * nb + n, p * dilation + r))

    o, lse = pl.pallas_call(
        kern,
        grid=(batch, dilation, nb),
        in_specs=[part(0), part(1), part(2)],
        out_specs=[
            pl.BlockSpec((rows, gw), lambda b, r, n: (b * nb + n, r)),
            pl.BlockSpec((rows, LANES), lambda b, r, n: (b * nb + n, r)),
        ],
        out_shape=[
            jax.ShapeDtypeStruct((batch * sub, dilation * gw), BF16),
            jax.ShapeDtypeStruct((batch * sub, dilation * LANES), F32),
        ],
        scratch_shapes=[pltpu.VMEM((ATTN_BLOCK, gw), BF16), pltpu.VMEM((ATTN_BLOCK, gw), BF16)],
        compiler_params=_params("arbitrary", "arbitrary", "arbitrary"),
        name=f"attn_g{gi}",
    )(qkv, qkv, qkv)
    return o, lse


def _merge_kernel(yn_ref, o0_ref, o1_ref, o2_ref, l0_ref, l1_ref, l2_ref, gate_ref,
                  wssd_ref, wattn_ref, bg_ref, merged_ref, ostage_ref, lstage_ref, *, d_model):
    E = ATTN_HEAD_DIM
    tm = merged_ref.shape[0]
    n_slabs = o0_ref.shape[1] // LANES
    y_ssd = jnp.dot(yn_ref[...], wssd_ref[...], preferred_element_type=F32)

    def token_major(gi, o_ref, l_ref):
        dil = DILATION_PATTERNS[gi][1]
        if dil == 1:
            return o_ref[...].astype(F32), l_ref[...]
        for r in range(dil):
            rows = pl.ds(r, tm // dil, stride=dil)
            lstage_ref[gi - 1, rows, :] = l_ref[:, r * LANES:(r + 1) * LANES]
            for s in range(n_slabs):
                c0 = (r * n_slabs + s) * LANES
                ostage_ref[gi - 1, s, rows, :] = o_ref[:, c0:c0 + LANES].astype(F32)
        out = jnp.concatenate([ostage_ref[gi - 1, s] for s in range(n_slabs)], axis=1)
        return out, lstage_ref[gi - 1]

    (o0, l0), (o1, l1), (o2, l2) = (token_major(gi, o_ref, l_ref) for gi, (o_ref, l_ref) in
                                    enumerate(((o0_ref, l0_ref), (o1_ref, l1_ref), (o2_ref, l2_ref))))

    lm = jnp.maximum(jnp.maximum(l0, l1), l2)
    e0, e1, e2 = jnp.exp(l0 - lm), jnp.exp(l1 - lm), jnp.exp(l2 - lm)
    inv = 1.0 / (e0 + e1 + e2)
    parts = []
    for h in range(ATTN_HEADS_PER_GROUP):
        sl = slice(h * E, (h + 1) * E)
        parts.append((e0[:, h:h + 1] * inv[:, h:h + 1]) * o0[:, sl]
                     + (e1[:, h:h + 1] * inv[:, h:h + 1]) * o1[:, sl]
                     + (e2[:, h:h + 1] * inv[:, h:h + 1]) * o2[:, sl])
    o = jnp.concatenate(parts, axis=-1).astype(BF16)
    y_attn = jnp.dot(o, wattn_ref[...], preferred_element_type=F32)

    gates = _sigmoid(gate_ref[...].astype(F32) + bg_ref[...])
    merged_ref[...] = (gates[:, :d_model] * y_ssd + gates[:, d_model:] * y_attn).astype(BF16)


def _merge(yn, outs, lses, proj, w_ssd_out, w_attn_out, b_gate, *, tm, gate_block):
    n, d = yn.shape
    aw = outs[0].shape[1]
    dils = [dil for _, dil in DILATION_PATTERNS]
    assert dils[0] == 1 and all(tm % (dil * SUBLANES) == 0 for dil in dils)
    row = lambda w: pl.BlockSpec((tm, w), lambda i: (i, 0))
    grouped = lambda dil, w: pl.BlockSpec((tm // dil, dil * w), lambda i: (i, 0))
    const = lambda shape: pl.BlockSpec(shape, lambda i: (0, 0), pipeline_mode=pl.Buffered(1))
    return pl.pallas_call(
        functools.partial(_merge_kernel, d_model=d),
        grid=(n // tm,),
        in_specs=[
            row(d), *[grouped(dil, aw) for dil in dils], *[grouped(dil, LANES) for dil in dils],
            pl.BlockSpec((tm, 2 * d), lambda i: (i, gate_block)),
            const((d, d)), const((aw, d)), const((1, 2 * d)),
        ],
        out_specs=row(d),
        out_shape=jax.ShapeDtypeStruct((n, d), BF16),
        scratch_shapes=[pltpu.VMEM((len(dils) - 1, aw // LANES, tm, LANES), F32),
                        pltpu.VMEM((len(dils) - 1, tm, LANES), F32)],
        compiler_params=_params("arbitrary"),
        name="merge",
    )(yn, outs[0], outs[1], outs[2], lses[0], lses[1], lses[2], proj,
      w_ssd_out.astype(BF16), w_attn_out.astype(BF16), b_gate.reshape(1, -1))


def _route_kernel(merged_ref, x_ref, wout_ref, fnw_ref, wrh_ref, wrl_ref, rb_ref,
                  x1_ref, eid_ref, gcol_ref, rank_ref, cnt_ref, carry_ref):
    i = pl.program_id(0)
    tm = x_ref.shape[0]

    @pl.when(i == 0)
    def _():
        carry_ref[...] = jnp.zeros(carry_ref.shape, F32)

    x1 = x_ref[...] + jnp.dot(merged_ref[...], wout_ref[...], preferred_element_type=F32)
    x1_ref[...] = x1

    ms = jnp.mean(x1 * x1, axis=-1, keepdims=True)
    hn = x1 * lax.rsqrt(ms + NORM_EPS) * fnw_ref[...]

    hn_hi, hn_lo = _split_bf16(hn)
    nt = (((1,), (1,)), ((), ()))
    logits = (lax.dot_general(wrh_ref[...], hn_hi, nt, preferred_element_type=F32)
              + lax.dot_general(wrh_ref[...], hn_lo, nt, preferred_element_type=F32)
              + lax.dot_general(wrl_ref[...], hn_hi, nt, preferred_element_type=F32)
              + rb_ref[...])

    grow = lax.broadcasted_iota(jnp.int32, (SUBLANES, tm), 0)
    gl = jnp.where(grow < N_EXPERT_GROUPS, logits[0:SUBLANES, :], -jnp.inf)
    gmax = jnp.max(gl, axis=0, keepdims=True)
    gidx = jnp.min(jnp.where(gl == gmax, grow, N_EXPERT_GROUPS), axis=0, keepdims=True)
    group_gate = 1.0 / jnp.sum(jnp.exp(gl - gmax), axis=0, keepdims=True)

    in_group = jnp.zeros((EXPERTS_PER_GROUP, tm), F32)
    for g in range(N_EXPERT_GROUPS):
        r0 = ROUTER_EXPERT_ROW0 + g * EXPERTS_PER_GROUP
        in_group = jnp.where(gidx == g, logits[r0:r0 + EXPERTS_PER_GROUP, :], in_group)
    erow = lax.broadcasted_iota(jnp.int32, in_group.shape, 0)
    v1 = jnp.max(in_group, axis=0, keepdims=True)
    i1 = jnp.min(jnp.where(in_group == v1, erow, EXPERTS_PER_GROUP), axis=0, keepdims=True)
    rest = jnp.where(erow == i1, -jnp.inf, in_group)
    v2 = jnp.max(rest, axis=0, keepdims=True)
    i2 = jnp.min(jnp.where(rest == v2, erow, EXPERTS_PER_GROUP), axis=0, keepdims=True)
    t = jnp.exp(v2 - v1)
    g1 = group_gate / (1.0 + t)
    g2 = group_gate * t / (1.0 + t)
    eid1 = gidx * EXPERTS_PER_GROUP + i1
    eid2 = gidx * EXPERTS_PER_GROUP + i2
    slot = lax.broadcasted_iota(jnp.int32, (SUBLANES, tm), 0)
    eid_ref[...] = jnp.where(slot == 0, eid1, jnp.where(slot == 1, eid2, 0))

    grow8 = lax.broadcasted_iota(jnp.int32, (LANES, tm), 0)
    gt = jnp.where(grow8 == 0, g1, jnp.where(grow8 == 1, g2, 0.0))
    gcol_ref[...] = gt.T

    xrow = lax.broadcasted_iota(jnp.int32, (N_EXPERTS, tm), 0)
    oh1 = xrow == eid1
    oh2 = xrow == eid2
    oh = jnp.where(oh1 | oh2, 1.0, 0.0)
    ti = lax.broadcasted_iota(jnp.int32, (tm, tm), 0)
    tj = lax.broadcasted_iota(jnp.int32, (tm, tm), 1)
    before = jnp.where(ti < tj, 1.0, 0.0).astype(BF16)
    prior = jnp.dot(oh.astype(BF16), before, preferred_element_type=F32) + carry_ref[:, 0:1]
    r1 = jnp.sum(jnp.where(oh1, prior, 0.0), axis=0, keepdims=True)
    r2 = jnp.sum(jnp.where(oh2, prior, 0.0), axis=0, keepdims=True)
    rank_ref[...] = jnp.where(slot == 0, r1, jnp.where(slot == 1, r2, 0.0)).astype(jnp.int32)
    carry_ref[...] = carry_ref[...] + jnp.sum(oh, axis=1, keepdims=True)
    cnt_ref[...] = carry_ref[...].astype(jnp.int32)


def _route(merged, x2, w_out, ffn_norm_w, wr, rb, *, tm):
    n, d = x2.shape
    wrt = jnp.zeros((ROUTER_ROWS, d), F32)
    wrt = wrt.at[0:N_EXPERT_GROUPS].set(wr[0].T).at[ROUTER_EXPERT_ROW0:].set(wr[1].T)
    rbc = jnp.zeros((ROUTER_ROWS, 1), F32)
    rbc = rbc.at[0:N_EXPERT_GROUPS, 0].set(rb[0]).at[ROUTER_EXPERT_ROW0:, 0].set(rb[1])
    wrh, wrl = _split_bf16(wrt)
    row = lambda w: pl.BlockSpec((tm, w), lambda i: (i, 0))
    const = lambda shape: pl.BlockSpec(shape, lambda i: (0, 0), pipeline_mode=pl.Buffered(1))
    return pl.pallas_call(
        _route_kernel,
        grid=(n // tm,),
        in_specs=[
            row(d), row(d),
            const((d, d)), const((1, d)),
            const((ROUTER_ROWS, d)), const((ROUTER_ROWS, d)), const((ROUTER_ROWS, 1)),
        ],
        out_specs=[
            row(d),
            pl.BlockSpec((SUBLANES, tm), lambda i: (0, i)),
            row(LANES),
            pl.BlockSpec((SUBLANES, tm), lambda i: (0, i)),
            pl.BlockSpec((N_EXPERTS, LANES), lambda i: (0, 0)),
        ],
        out_shape=[
            jax.ShapeDtypeStruct((n, d), F32),
            jax.ShapeDtypeStruct((SUBLANES, n), jnp.int32),
            jax.ShapeDtypeStruct((n, LANES), F32),
            jax.ShapeDtypeStruct((SUBLANES, n), jnp.int32),
            jax.ShapeDtypeStruct((N_EXPERTS, LANES), jnp.int32),
        ],
        scratch_shapes=[pltpu.VMEM((N_EXPERTS, LANES), F32)],
        compiler_params=_params("arbitrary"),
        name="route",
    )(merged, x2, w_out.astype(BF16), ffn_norm_w.reshape(1, -1), wrh, wrl, rbc)


INVERT_UNROLL = 8


def _invert_kernel(pos_ref, tok_ref, *, n):
    def clear(b, carry):
        for u in range(INVERT_UNROLL):
            tok_ref[b * INVERT_UNROLL + u] = 0
        return carry

    lax.fori_loop(0, tok_ref.shape[0] // INVERT_UNROLL, clear, 0)

    def place(b, carry):
        for u in range(INVERT_UNROLL):
            t = b * INVERT_UNROLL + u
            for k in range(EXPERT_TOP_K):
                tok_ref[pos_ref[k * n + t]] = t
        return carry

    lax.fori_loop(0, n // INVERT_UNROLL, place, 0)


def _invert(pos, n_rows):
    n = pos.shape[1]
    assert n % INVERT_UNROLL == 0 and n_rows % INVERT_UNROLL == 0
    return pl.pallas_call(
        functools.partial(_invert_kernel, n=n),
        in_specs=[pl.BlockSpec(memory_space=pltpu.SMEM)],
        out_specs=pl.BlockSpec(memory_space=pltpu.SMEM),
        out_shape=jax.ShapeDtypeStruct((n_rows,), jnp.int32),
        name="invert",
    )(pos.reshape(-1))


W_CHUNK_ROWS = 256
W_RING = 8


def _experts_kernel(te_ref, slot_ref, nxt_ref, lo_ref, hi_ref, na_ref, tok_ref, tok_next_ref,
                    x1_hbm, fnw_ref, wg_hbm, wu_hbm, wd_hbm, ys_ref, wres_ref, stage_ref, xs_ref,
                    sem, gsem, *, n_gu, n_r, n_h):
    i = pl.program_id(0)
    n_chunks = 2 * n_gu + n_h * n_r
    ring, cr, f = stage_ref.shape
    d = n_gu * cr
    tm = ys_ref.shape[0]

    def gather_rows(t_ref, buf):
        for r in range(tm):
            pltpu.make_async_copy(x1_hbm.at[pl.ds(t_ref[0, 0, r], 1), :],
                                  xs_ref.at[buf, pl.ds(r, 1), :],
                                  gsem.at[buf]).start(priority=r % 2)

    def wait_rows(buf):
        pltpu.make_async_copy(x1_hbm.at[pl.ds(0, tm), :], xs_ref.at[buf], gsem.at[buf]).wait()

    def stage_copy(src, c):
        return pltpu.make_async_copy(src, stage_ref.at[c % ring], sem.at[c % ring])

    def start_chunk(e, c):
        @pl.when(c < n_gu)
        def _():
            stage_copy(wg_hbm.at[e, pl.ds(pl.multiple_of(c * cr, cr), cr), :], c).start()

        @pl.when((c >= n_gu) & (c < 2 * n_gu))
        def _():
            stage_copy(wu_hbm.at[e, pl.ds(pl.multiple_of((c - n_gu) * cr, cr), cr), :], c).start()

        @pl.when(c >= 2 * n_gu)
        def _():
            q = c - 2 * n_gu
            h = q // n_r
            r = q - h * n_r
            stage_copy(wd_hbm.at[e, pl.ds(pl.multiple_of(r * cr, cr), cr),
                                 pl.ds(pl.multiple_of(h * f, f), f)], c).start()

    def start_first(e):
        def body(c, carry):
            start_chunk(e, c)
            return carry

        lax.fori_loop(0, ring, body, 0)

    def convert(e, lo, hi, dst_slot):
        def body(c, carry):
            stage_copy(wg_hbm.at[0, pl.ds(0, cr), :], c).wait()
            wres_ref[dst_slot, pl.ds(pl.multiple_of(c * cr, cr), cr), :] = (
                stage_ref[c % ring].astype(BF16))

            @pl.when(c + ring < n_chunks)
            def _():
                start_chunk(e, c + ring)

            return carry

        lax.fori_loop(lo, hi, body, 0)

    buf = i % 2

    @pl.when(i == 0)
    def _():
        gather_rows(tok_ref, 0)
        start_first(te_ref[0])
        convert(te_ref[0], 0, n_chunks, slot_ref[0])

    @pl.when(i < na_ref[0])
    def _():
        slot = slot_ref[i]
        nxt = nxt_ref[i]

        @pl.when((nxt >= 0) & (lo_ref[i] == 0))
        def _():
            start_first(nxt)

        wait_rows(buf)

        xr = xs_ref[buf]
        ms = jnp.mean(xr * xr, axis=-1, keepdims=True)
        xb = (xr * lax.rsqrt(ms + NORM_EPS) * fnw_ref[...]).astype(BF16)
        gather_rows(tok_next_ref, 1 - buf)
        hg = jnp.dot(xb, wres_ref[slot, 0:d, :], preferred_element_type=F32)
        hu = jnp.dot(xb, wres_ref[slot, d:2 * d, :], preferred_element_type=F32)
        hmid = (_silu(hg) * hu).astype(BF16)
        for h in range(n_h):
            r0 = 2 * d + h * n_r * cr
            ys_ref[:, h * f:(h + 1) * f] = jnp.dot(hmid, wres_ref[slot, r0:r0 + n_r * cr, :],
                                                    preferred_element_type=F32)

        @pl.when(nxt >= 0)
        def _():
            convert(nxt, lo_ref[i], hi_ref[i], 1 - slot)

    @pl.when(i == na_ref[0])
    def _():
        wait_rows(buf)

    @pl.when(i >= na_ref[0])
    def _():
        ys_ref[...] = jnp.zeros(ys_ref.shape, F32)


def _experts(x1, tok_tiles, norm_w, tile_expert, n_active, cnt, starts, padded, w_g, w_u, w_d, *,
             tm):
    _, d = x1.shape
    ne, _, f = w_g.shape
    n_tiles = tok_tiles.shape[0]
    p = n_tiles * tm
    cr = W_CHUNK_ROWS
    assert d % cr == 0 and f % cr == 0 and d % f == 0
    n_gu, n_r, n_h = d // cr, f // cr, d // f
    n_chunks = 2 * n_gu + n_h * n_r

    ids = jnp.arange(ne, dtype=jnp.int32)
    present = cnt > 0
    later = jnp.where(present[None, :] & (ids[None, :] > ids[:, None]), ids[None, :], ne)
    nxt_e = jnp.min(later, axis=1)
    nxt_e = jnp.where(nxt_e >= ne, -1, nxt_e)
    run_e = jnp.cumsum(present.astype(jnp.int32)) - 1
    onehot = (tile_expert[:, None] == ids[None, :]).astype(jnp.int32)
    pick = lambda v: jnp.sum(onehot * v[None, :].astype(jnp.int32), axis=1)
    tiles = jnp.arange(n_tiles, dtype=jnp.int32)
    active = tiles < n_active[0]
    j = tiles - pick(starts) // tm
    k = jnp.maximum(pick(padded) // tm, 1)
    nxt_t = jnp.where(active, pick(nxt_e), -1).astype(jnp.int32)
    lo_t = jnp.where(active, (n_chunks * j) // k, 0).astype(jnp.int32)
    hi_t = jnp.where(active, (n_chunks * (j + 1)) // k, 0).astype(jnp.int32)
    slot_t = (pick(run_e) % 2).astype(jnp.int32)

    grid_spec = pltpu.PrefetchScalarGridSpec(
        num_scalar_prefetch=6,
        grid=(n_tiles,),
        in_specs=[
            pl.BlockSpec((1, 1, tm), lambda i, *_: (i, 0, 0), memory_space=pltpu.SMEM),
            pl.BlockSpec((1, 1, tm), lambda i, *_: (jnp.minimum(i + 1, n_tiles - 1), 0, 0),
                         memory_space=pltpu.SMEM),
            pl.BlockSpec(memory_space=pl.ANY),
            pl.BlockSpec((1, d), lambda i, *_: (0, 0)),
            pl.BlockSpec(memory_space=pl.ANY),
            pl.BlockSpec(memory_space=pl.ANY),
            pl.BlockSpec(memory_space=pl.ANY),
        ],
        out_specs=pl.BlockSpec((tm, d), lambda i, *_: (i, 0)),
        scratch_shapes=[
            pltpu.VMEM((2, n_chunks * cr, f), BF16),
            pltpu.VMEM((W_RING, cr, f), F32),
            pltpu.VMEM((2, tm, d), F32),
            pltpu.SemaphoreType.DMA((W_RING,)),
            pltpu.SemaphoreType.DMA((2,)),
        ],
    )
    return pl.pallas_call(
        functools.partial(_experts_kernel, n_gu=n_gu, n_r=n_r, n_h=n_h),
        grid_spec=grid_spec,
        out_shape=jax.ShapeDtypeStruct((p, d), F32),
        compiler_params=_params("arbitrary"),
        name="experts",
    )(tile_expert, slot_t, nxt_t, lo_t, hi_t, n_active, tok_tiles, tok_tiles, x1,
      norm_w.reshape(1, d), w_g, w_u, w_d)


def _combine_kernel(pos_ref, pos_next_ref, ys_ref, x1_ref, gcol_ref, nw_ref, out_ref, buf_ref, sem,
                    *, final):
    i = pl.program_id(0)
    n_steps = pl.num_programs(0)
    tm = x1_ref.shape[0]

    def issue(p_ref, slot):
        def start(tb, carry):
            for u in range(ROW_DMA_UNROLL):
                t = tb * ROW_DMA_UNROLL + u
                for k in range(EXPERT_TOP_K):
                    src = p_ref[0, 0, k * tm + t]
                    pltpu.make_async_copy(ys_ref.at[pl.ds(src, 1), :],
                                          buf_ref.at[slot, k, pl.ds(t, 1), :],
                                          sem.at[slot]).start(priority=(u + k) % 2)
            return carry

        lax.fori_loop(0, tm // ROW_DMA_UNROLL, start, 0)

    slot = i % 2

    @pl.when(i == 0)
    def _():
        issue(pos_ref, 0)

    @pl.when(i + 1 < n_steps)
    def _():
        issue(pos_next_ref, 1 - slot)

    for k in range(EXPERT_TOP_K):
        pltpu.make_async_copy(ys_ref.at[pl.ds(0, tm), :], buf_ref.at[slot, k], sem.at[slot]).wait()
    g = gcol_ref[...]
    xo = x1_ref[...] + g[:, 0:1] * buf_ref[slot, 0] + g[:, 1:2] * buf_ref[slot, 1]
    if final:
        ms = jnp.mean(xo * xo, axis=-1, keepdims=True)
        xo = xo * lax.rsqrt(ms + NORM_EPS) * nw_ref[...]
    out_ref[...] = xo


def _combine(ys, pos_tiles, x1, gcol, norm_w, *, tm, final):
    n, d = x1.shape
    last = n // tm - 1
    return pl.pallas_call(
        functools.partial(_combine_kernel, final=final),
        grid=(n // tm,),
        in_specs=[
            pl.BlockSpec((1, 1, EXPERT_TOP_K * tm), lambda i: (i, 0, 0), memory_space=pltpu.SMEM),
            pl.BlockSpec((1, 1, EXPERT_TOP_K * tm), lambda i: (jnp.minimum(i + 1, last), 0, 0),
                         memory_space=pltpu.SMEM),
            pl.BlockSpec(memory_space=pl.ANY),
            pl.BlockSpec((tm, d), lambda i: (i, 0)),
            pl.BlockSpec((tm, LANES), lambda i: (i, 0)),
            pl.BlockSpec((1, d), lambda i: (0, 0)),
        ],
        out_specs=pl.BlockSpec((tm, d), lambda i: (i, 0)),
        out_shape=jax.ShapeDtypeStruct((n, d), F32),
        scratch_shapes=[pltpu.VMEM((2, EXPERT_TOP_K, tm, d), F32), pltpu.SemaphoreType.DMA((2,))],
        compiler_params=_params("arbitrary"),
        name="combine",
    )(pos_tiles, pos_tiles, ys, x1, gcol, norm_w.reshape(1, d))


def _tiles(n):
    return dict(proj_tm=math.gcd(n, 1024), mix_tm=math.gcd(n, 512), moe_tm=256,
                route_tm=math.gcd(n, 512))


def _layer(x2, batch, seq, layer, attn_norm_w, w_in_stack, b_gate, conv_w, conv_b, dt_bias, a_log,
           d_skip, ssd_norm_w, w_ssd_out, w_attn_out, w_out, ffn_norm_w, w_gr, b_gr, w_er, b_er,
           w_g, w_u, w_d):
    n, d = x2.shape
    n_heads = dt_bias.shape[0]
    d_inner = ssd_norm_w.shape[0]
    conv_dim = conv_w.shape[1]
    aw_total = ATTN_HEADS_PER_GROUP * len(DILATION_PATTERNS) * ATTN_HEAD_DIM
    gw = ATTN_HEADS_PER_GROUP * ATTN_HEAD_DIM
    tiles = _tiles(n)

    c_z, c_xbc, c_dt = d_inner, d_inner + conv_dim, d_inner + conv_dim + n_heads
    c_gate = c_dt + QKV_PARTS * aw_total
    segments = [(c_gate, 2 * d), (c_z, conv_dim), (0, d_inner)]
    segments += [(c_dt + p * aw_total + gi * gw, gw)
                 for gi in range(len(DILATION_PATTERNS)) for p in range(QKV_PARTS)]
    assert all(width % gw == 0 for _, width in segments)
    starts = tuple(start + b * gw for start, width in segments for b in range(width // gw))
    w_main, w_dt = _regroup_weight(w_in_stack, layer, starts, gw, c_xbc, n_heads)
    off_gate, off_xbc, off_z = 0, 2 * d, 2 * d + conv_dim
    plain_cols = off_z + d_inner
    assert off_xbc % conv_dim == 0 and off_z % d_inner == 0 and plain_cols % gw == 0

    proj, qkv0, qkv1, qkv2, dt, dtt = _in_proj(x2, attn_norm_w, w_main, w_dt, tm=tiles["proj_tm"],
                                               tn=gw, tp=2 * gw, plain_cols=plain_cols, nh=n_heads)

    yn = _ssd(proj, dt, dtt, conv_w, conv_b, dt_bias, a_log, d_skip, ssd_norm_w, batch=batch,
              seq=seq, d_inner=d_inner, xbc_block=off_xbc // conv_dim, z_block=off_z // d_inner)

    outs, lses = [], []
    for gi, qkv in enumerate((qkv0, qkv1, qkv2)):
        o_g, lse_g = _attn_group(qkv, gi, batch=batch, seq=seq)
        outs.append(o_g)
        lses.append(lse_g)

    merged = _merge(yn, outs, lses, proj, w_ssd_out, w_attn_out, b_gate, tm=tiles["mix_tm"],
                    gate_block=off_gate // (2 * d))
    x1, eid, gcol, rank, counts = _route(merged, x2, w_out, ffn_norm_w, (w_gr, w_er),
                                         (b_gr, b_er), tm=tiles["mix_tm"])

    tme = tiles["moe_tm"]
    cnt = counts[:, 0]
    padded = ((cnt + tme - 1) // tme) * tme
    ends = jnp.cumsum(padded)
    starts = ends - padded
    experts = jnp.arange(N_EXPERTS, dtype=jnp.int32)[:, None, None]
    pos = rank[:EXPERT_TOP_K] + jnp.sum(
        jnp.where(eid[None, :EXPERT_TOP_K] == experts, starts[:, None, None], 0), axis=0)
    n_tiles = EXPERT_TOP_K * n // tme + N_EXPERTS
    tile_start = jnp.arange(n_tiles, dtype=jnp.int32) * tme
    tile_expert = jnp.minimum(jnp.sum(ends[None, :] <= tile_start[:, None], axis=1),
                              N_EXPERTS - 1).astype(jnp.int32)
    n_active = (ends[-1:] // tme).astype(jnp.int32)

    rtm = tiles["route_tm"]
    pos_tiles = pos.reshape(EXPERT_TOP_K, n // rtm, rtm).transpose(1, 0, 2).reshape(
        n // rtm, 1, EXPERT_TOP_K * rtm)
    tok_tiles = _invert(pos, n_tiles * tme).reshape(n_tiles, 1, tme)
    ys = _experts(x1, tok_tiles, ffn_norm_w, tile_expert, n_active, cnt, starts, padded,
                  w_g, w_u, w_d, tm=tme)
    return ys, pos_tiles, x1, gcol, rtm


def kernel(x, attn_norm_w, w_in, b_gate, conv_w, conv_b, dt_bias, a_log, d_skip, ssd_norm_w,
           w_ssd_out, w_attn_out, w_out, ffn_norm_w, w_group_router, b_group_router,
           w_expert_router, b_expert_router, w_exp_gate, w_exp_up, w_exp_down, final_norm_w):
    batch, seq, d = x.shape
    depth = w_in.shape[0]
    x2 = x.reshape(batch * seq, d)
    for layer in range(depth):
        ys, pos_tiles, x1, gcol, rtm = _layer(
            x2, batch, seq, layer, attn_norm_w[layer], w_in, b_gate[layer], conv_w[layer],
            conv_b[layer], dt_bias[layer], a_log[layer], d_skip[layer], ssd_norm_w[layer],
            w_ssd_out[layer], w_attn_out[layer], w_out[layer], ffn_norm_w[layer],
            w_group_router[layer], b_group_router[layer], w_expert_router[layer],
            b_expert_router[layer], w_exp_gate[layer], w_exp_up[layer], w_exp_down[layer])
        x2 = _combine(ys, pos_tiles, x1, gcol, final_norm_w, tm=rtm, final=layer == depth - 1)
    return x2.reshape(batch, seq, d)
```

```python
import functools
import math

import jax
import jax.numpy as jnp
import numpy as np
from jax import lax
from jax.experimental import pallas as pl
from jax.experimental.pallas import tpu as pltpu

F32 = jnp.float32
BF16 = jnp.bfloat16

NORM_EPS = 1e-6
SSD_HEAD_DIM = 64
SSD_N_GROUPS = 8
SSD_D_STATE = 128
SSD_CONV_WIDTH = 4
SSD_CHUNK = 128
ATTN_HEAD_DIM = 128
DILATION_PATTERNS = ((128, 1), (512, 4), (2048, 16))
ATTN_HEADS_PER_GROUP = 4
ATTN_BLOCK = 128
ATTN_BLOCKS_PER_STEP = 4
N_EXPERT_GROUPS = 4
EXPERTS_PER_GROUP = 8
N_EXPERTS = N_EXPERT_GROUPS * EXPERTS_PER_GROUP
EXPERT_TOP_K = 2

LANES = 128
SUBLANES = 8
VMEM_LIMIT_BYTES = 56 * 1024 * 1024
ROW_DMA_UNROLL = 8

ROUTER_EXPERT_ROW0 = SUBLANES
ROUTER_ROWS = ROUTER_EXPERT_ROW0 + N_EXPERTS


def _params(*semantics):
    return pltpu.CompilerParams(dimension_semantics=semantics, vmem_limit_bytes=VMEM_LIMIT_BYTES)


def _split_bf16(v):
    hi = v.astype(BF16)
    lo = (v - hi.astype(F32)).astype(BF16)
    return hi, lo


def _sigmoid(v):
    return 0.5 + 0.5 * jnp.tanh(0.5 * v)


def _silu(v):
    h = 0.5 * v
    return h + h * jnp.tanh(h)


def _regroup_weight_kernel(starts_ref, wt_hbm, out_ref, narrow_ref, buf_ref, nbuf_ref, sem, nsem,
                           *, layer, narrow_start, nh):
    i = pl.program_id(0)
    n_steps = pl.num_programs(0)
    tn = buf_ref.shape[1]

    def fetch(step, slot):
        rows = pl.ds(pl.multiple_of(starts_ref[step], SUBLANES), tn)
        return pltpu.make_async_copy(wt_hbm.at[layer, rows, :], buf_ref.at[slot], sem.at[slot])

    narrow = pltpu.make_async_copy(wt_hbm.at[layer, pl.ds(narrow_start, LANES), :], nbuf_ref, nsem)

    @pl.when(i == 0)
    def _():
        fetch(0, 0).start()
        narrow.start()

    @pl.when(i + 1 < n_steps)
    def _():
        fetch(i + 1, (i + 1) % 2).start()

    fetch(i, i % 2).wait()
    out_ref[...] = buf_ref[i % 2].T.astype(out_ref.dtype)

    @pl.when(i == n_steps - 1)
    def _():
        narrow.wait()
        lane = lax.broadcasted_iota(jnp.int32, narrow_ref.shape, 1)
        narrow_ref[...] = jnp.where(lane < nh, nbuf_ref[...].T, 0.0).astype(narrow_ref.dtype)


def _regroup_weight(w_stack, layer, starts, tn, narrow_start, nh):
    _, k, cols = w_stack.shape
    assert all(s % SUBLANES == 0 and s + tn <= cols for s in starts)
    assert narrow_start % SUBLANES == 0 and narrow_start + LANES <= cols and nh <= LANES
    wt = jnp.swapaxes(w_stack, 1, 2)
    grid_spec = pltpu.PrefetchScalarGridSpec(
        num_scalar_prefetch=1,
        grid=(len(starts),),
        in_specs=[pl.BlockSpec(memory_space=pl.ANY)],
        out_specs=[pl.BlockSpec((k, tn), lambda i, *_: (0, i)),
                   pl.BlockSpec((k, LANES), lambda i, *_: (0, 0))],
        scratch_shapes=[pltpu.VMEM((2, tn, k), F32), pltpu.VMEM((LANES, k), F32),
                        pltpu.SemaphoreType.DMA((2,)), pltpu.SemaphoreType.DMA(())],
    )
    return pl.pallas_call(
        functools.partial(_regroup_weight_kernel, layer=layer, narrow_start=narrow_start, nh=nh),
        grid_spec=grid_spec,
        out_shape=[jax.ShapeDtypeStruct((k, len(starts) * tn), BF16),
                   jax.ShapeDtypeStruct((k, LANES), BF16)],
        compiler_params=_params("arbitrary"),
        name="regroup_weight",
    )(jnp.asarray(starts, jnp.int32), wt)


QKV_PARTS = 3


def _in_proj_kernel(x_hbm, nw_ref, wp_ref, w_ref, wdt_ref, proj_ref, a0_ref, a1_ref, a2_ref,
                    dt_ref, dtt_ref, h_ref, stage_ref, x_ref, xsem, *, n_plain):
    i = pl.program_id(0)
    j = pl.program_id(1)
    tm = proj_ref.shape[0]
    tn = w_ref.shape[1]

    def fetch_x(tile):
        return pltpu.make_async_copy(x_hbm.at[pl.ds(pl.multiple_of(tile * tm, tm), tm), :], x_ref, xsem)

    @pl.when((i == 0) & (j == 0))
    def _():
        fetch_x(0).start()

    @pl.when(j == 0)
    def _():
        fetch_x(i).wait()
        xf = x_ref[...]
        ms = jnp.mean(xf * xf, axis=-1, keepdims=True)
        h = (xf * lax.rsqrt(ms + NORM_EPS) * nw_ref[...]).astype(BF16)
        h_ref[...] = h
        nh = dt_ref.shape[1]
        dt_wide = jnp.dot(h, wdt_ref[...], preferred_element_type=F32)
        dt_ref[...] = dt_wide[:, :nh]
        dtt_ref[...] = dt_wide.T[:nh, :]

        @pl.when(i + 1 < pl.num_programs(0))
        def _():
            fetch_x(i + 1).start()

    @pl.when(j < n_plain)
    def _():
        proj_ref[...] = jnp.dot(h_ref[...], wp_ref[...], preferred_element_type=F32).astype(BF16)

    for gi, a_ref in enumerate((a0_ref, a1_ref, a2_ref)):
        dil = DILATION_PATTERNS[gi][1]
        j0 = n_plain + QKV_PARTS * gi

        @pl.when((j >= j0) & (j < j0 + QKV_PARTS))
        def _(a_ref=a_ref, dil=dil):
            res = jnp.dot(h_ref[...], w_ref[...], preferred_element_type=F32)
            if dil == 1:
                a_ref[...] = res.astype(BF16)
            else:
                for s in range(tn // LANES):
                    stage_ref[s] = res[:, s * LANES:(s + 1) * LANES]
                for r in range(dil):
                    for s in range(tn // LANES):
                        c0 = r * tn + s * LANES
                        a_ref[:, c0:c0 + LANES] = stage_ref[
                            s, pl.ds(r, tm // dil, stride=dil), :].astype(BF16)


def _in_proj(x2, norm_w, w_main, w_dt, *, tm, tn, tp, plain_cols, nh):
    n, d = x2.shape
    assert plain_cols % tp == 0 and plain_cols % tn == 0
    n_plain = plain_cols // tp
    qkv_block0 = plain_cols // tn
    n_blocks = n_plain + QKV_PARTS * len(DILATION_PATTERNS)
    assert w_main.shape[1] == plain_cols + QKV_PARTS * len(DILATION_PATTERNS) * tn

    def a_spec(gi):
        dil = DILATION_PATTERNS[gi][1]
        j0 = n_plain + QKV_PARTS * gi
        return pl.BlockSpec((tm // dil, dil * tn),
                            lambda i, j: (i, jnp.clip(j - j0, 0, QKV_PARTS - 1)))

    def a_shape(gi):
        dil = DILATION_PATTERNS[gi][1]
        return jax.ShapeDtypeStruct((n // dil, dil * QKV_PARTS * tn), BF16)

    return pl.pallas_call(
        functools.partial(_in_proj_kernel, n_plain=n_plain),
        grid=(n // tm, n_blocks),
        in_specs=[
            pl.BlockSpec(memory_space=pl.ANY),
            pl.BlockSpec((1, d), lambda i, j: (0, 0)),
            pl.BlockSpec((d, tp), lambda i, j: (0, jnp.minimum(j, n_plain - 1))),
            pl.BlockSpec((d, tn), lambda i, j: (0, qkv_block0 + jnp.maximum(j - n_plain, 0))),
            pl.BlockSpec((d, LANES), lambda i, j: (0, 0)),
        ],
        out_specs=[
            pl.BlockSpec((tm, tp), lambda i, j: (i, jnp.minimum(j, n_plain - 1))),
            a_spec(0), a_spec(1), a_spec(2),
            pl.BlockSpec((tm, nh), lambda i, j: (i, 0)),
            pl.BlockSpec((nh, tm), lambda i, j: (0, i)),
        ],
        out_shape=[
            jax.ShapeDtypeStruct((n, plain_cols), BF16),
            a_shape(0), a_shape(1), a_shape(2),
            jax.ShapeDtypeStruct((n, nh), F32),
            jax.ShapeDtypeStruct((nh, n), F32),
        ],
        scratch_shapes=[pltpu.VMEM((tm, d), BF16), pltpu.VMEM((tn // LANES, tm, LANES), F32),
                        pltpu.VMEM((tm, d), x2.dtype), pltpu.SemaphoreType.DMA(())],
        compiler_params=_params("arbitrary", "arbitrary"),
        name="in_proj",
    )(x2, norm_w.reshape(1, d), w_main, w_main, w_dt)


LOG2E = math.log2(math.e)
CONV_ROW_PITCH = 2


def _ssd_kernel(xbc_ref, z_ref, dt_ref, dtt_ref, cw_ref, cb_ref, dtb_ref, dtbt_ref, alog_ref,
                alogt_ref, dskip_ref, nw_ref, expand_ref, out_ref,
                xbuf_ref, state_ref, y_ref, *, n_heads, d_inner):
    L = SSD_CHUNK
    P = SSD_HEAD_DIM
    NS = SSD_D_STATE
    G = SSD_N_GROUPS
    R = n_heads // G
    GW = R * P
    W = SSD_CONV_WIDTH
    RP = CONV_ROW_PITCH
    n_slabs = xbuf_ref.shape[0]
    c = pl.program_id(1)

    def rows(first, count):
        return pl.ds(first * RP, count, stride=RP)

    @pl.when(c == 0)
    def _():
        state_ref[...] = jnp.zeros(state_ref.shape, F32)
        for s in range(n_slabs):
            xbuf_ref[s, rows(0, SUBLANES), :] = jnp.zeros((SUBLANES, LANES), F32)

    @pl.when(c > 0)
    def _():
        for s in range(n_slabs):
            xbuf_ref[s, rows(0, SUBLANES), :] = xbuf_ref[s, rows(L, SUBLANES), :]

    for s in range(n_slabs):
        xbuf_ref[s, rows(SUBLANES, L), :] = xbc_ref[:, s * LANES:(s + 1) * LANES].astype(F32)

    def conv(col0, width):
        slabs = []
        for s in range(col0 // LANES, (col0 + width) // LANES):
            acc = cb_ref[:, s * LANES:(s + 1) * LANES]
            for w in range(W):
                acc = acc + (cw_ref[w:w + 1, s * LANES:(s + 1) * LANES]
                             * xbuf_ref[s, rows(SUBLANES - (W - 1) + w, L), :])
            slabs.append(acc)
        return _silu(jnp.concatenate(slabs, axis=1))

    def softplus(v):
        return jnp.maximum(v, 0.0) + jnp.log1p(jnp.exp(-jnp.abs(v)))

    dt = softplus(dt_ref[...] + dtb_ref[...])
    dtt = softplus(dtt_ref[...] + dtbt_ref[...])
    da = dt * (-LOG2E * jnp.exp(alog_ref[...]))
    dat = dtt * (-LOG2E * jnp.exp(alogt_ref[...]))
    row = lax.broadcasted_iota(jnp.int32, (L, L), 0)
    col = lax.broadcasted_iota(jnp.int32, (L, L), 1)
    causal = row >= col
    tri = jnp.where(causal, 1.0, 0.0).astype(BF16)
    trit = jnp.where(row <= col, 1.0, 0.0).astype(BF16)

    def split3(v):
        a = v.astype(BF16)
        r1 = v - a.astype(F32)
        b = r1.astype(BF16)
        cc = (r1 - b.astype(F32)).astype(BF16)
        return a, b, cc

    a2 = sum(jnp.dot(tri, p, preferred_element_type=F32) for p in split3(da))
    a2t = sum(jnp.dot(p, trit, preferred_element_type=F32) for p in split3(dat))
    a2_last = a2[L - 1:L, :]

    expand = expand_ref[...]

    def expand_heads(v):
        hi, lo = _split_bf16(v)
        return jnp.dot(jnp.concatenate([hi, lo], axis=1), expand, preferred_element_type=F32)

    in_scale_e = expand_heads(dt * jnp.exp2(a2_last - a2))
    tail8 = jnp.concatenate([jnp.exp2(a2_last), dskip_ref[...],
                             jnp.zeros((SUBLANES - 2, n_heads), F32)], axis=0)
    out_e = expand_heads(jnp.concatenate([jnp.exp2(a2), tail8], axis=0))
    out_scale_e = out_e[0:L, :]
    chunk_decay_e = out_e[L:L + 1, :]
    dskip_e = out_e[L + 1:L + 2, :]

    first_head = lax.broadcasted_iota(jnp.int32, (L, LANES), 1) < P

    for g in range(G):
        x0 = g * GW
        xs = conv(x0, GW)
        bm = conv(d_inner + g * NS, NS).astype(BF16)
        cm = conv(d_inner + G * NS + g * NS, NS).astype(BF16)
        cb = lax.dot_general(cm, bm, (((1,), (1,)), ((), ())), preferred_element_type=F32)
        cb = jnp.where(causal, cb, 0.0)
        y_parts = []
        for pr in range(GW // LANES):
            wgts = []
            for q in range(LANES // P):
                h = g * R + pr * (LANES // P) + q
                seg = a2[:, h:h + 1] - a2t[h:h + 1, :]
                decay = jnp.exp2(jnp.minimum(seg, 0.0))
                wgts.append((cb * decay * dtt[h:h + 1, :]).astype(BF16))
            slab = xs[:, pr * LANES:(pr + 1) * LANES]
            rhs = jnp.concatenate([jnp.where(first_head, slab, 0.0).astype(BF16),
                                   jnp.where(first_head, 0.0, slab).astype(BF16)], axis=0)
            y_parts.append(jnp.dot(jnp.concatenate(wgts, axis=1), rhs, preferred_element_type=F32))
        y = jnp.concatenate(y_parts, axis=1) + xs * dskip_e[:, x0:x0 + GW]
        st = state_ref[:, x0:x0 + GW]
        y = y + (jnp.dot(cm, st.astype(BF16), preferred_element_type=F32)
                 * out_scale_e[:, x0:x0 + GW])
        y_ref[:, x0:x0 + GW] = y
        xin = (xs * in_scale_e[:, x0:x0 + GW]).astype(BF16)
        st_new = lax.dot_general(bm, xin, (((0,), (0,)), ((), ())), preferred_element_type=F32)
        state_ref[:, x0:x0 + GW] = st * chunk_decay_e[:, x0:x0 + GW] + st_new

    yz = y_ref[...] * _silu(z_ref[...].astype(F32))
    ms = jnp.mean(yz * yz, axis=-1, keepdims=True)
    out_ref[...] = (yz * lax.rsqrt(ms + NORM_EPS) * nw_ref[...]).astype(out_ref.dtype)


def _ssd(proj, dt, dtt, conv_w, conv_b, dt_bias, a_log, d_skip, norm_w, *, batch, seq, d_inner,
         xbc_block, z_block):
    n = batch * seq
    n_heads = dt.shape[1]
    L = SSD_CHUNK
    nc = seq // L
    conv_dim = conv_w.shape[1]
    expand = (np.arange(d_inner)[None, :] // SSD_HEAD_DIM == np.arange(n_heads)[:, None])
    expand = jnp.asarray(np.concatenate([expand, expand], axis=0), BF16)
    assert conv_dim % LANES == 0
    kern = functools.partial(_ssd_kernel, n_heads=n_heads, d_inner=d_inner)
    small = lambda shape: pl.BlockSpec(shape, lambda b, c: (0, 0))
    return pl.pallas_call(
        kern,
        grid=(batch, nc),
        in_specs=[
            pl.BlockSpec((L, conv_dim), lambda b, c: (b * nc + c, xbc_block)),
            pl.BlockSpec((L, d_inner), lambda b, c: (b * nc + c, z_block)),
            pl.BlockSpec((L, n_heads), lambda b, c: (b * nc + c, 0)),
            pl.BlockSpec((n_heads, L), lambda b, c: (0, b * nc + c)),
            small((SSD_CONV_WIDTH, conv_dim)),
            small((1, conv_dim)),
            small((1, n_heads)),
            small((n_heads, 1)),
            small((1, n_heads)),
            small((n_heads, 1)),
            small((1, n_heads)),
            small((1, d_inner)),
            small((2 * n_heads, d_inner)),
        ],
        out_specs=pl.BlockSpec((L, d_inner), lambda b, c: (b * nc + c, 0)),
        out_shape=jax.ShapeDtypeStruct((n, d_inner), BF16),
        scratch_shapes=[
            pltpu.VMEM((conv_dim // LANES, CONV_ROW_PITCH * (L + SUBLANES), LANES), F32),
            pltpu.VMEM((SSD_D_STATE, d_inner), F32),
            pltpu.VMEM((L, d_inner), F32),
        ],
        compiler_params=_params("arbitrary", "arbitrary"),
        name="ssd",
    )(proj, proj, dt, dtt, conv_w, conv_b.reshape(1, -1), dt_bias.reshape(1, -1),
      dt_bias.reshape(-1, 1), a_log.reshape(1, -1), a_log.reshape(-1, 1), d_skip.reshape(1, -1),
      norm_w.reshape(1, -1), expand)


def _attn_kernel(q_ref, k_ref, v_ref, o_ref, lse_ref, kp_ref, vp_ref, *, slopes, dilation, hops):
    nb = pl.program_id(2)
    BLK = ATTN_BLOCK
    E = ATTN_HEAD_DIM

    @pl.when(nb == 0)
    def _():
        kp_ref[...] = jnp.zeros(kp_ref.shape, kp_ref.dtype)
        vp_ref[...] = jnp.zeros(vp_ref.shape, vp_ref.dtype)

    scale = E ** -0.5
    nt = (((1,), (1,)), ((), ()))
    nblk = q_ref.shape[0] // BLK
    q = q_ref[...]
    kk = jnp.concatenate([kp_ref[...], k_ref[...]], axis=0)
    vv = jnp.concatenate([vp_ref[...], v_ref[...]], axis=0)
    units = [(j, h) for j in range(nblk) for h in range(len(slopes))]
    cols = lambda h: slice(h * E, (h + 1) * E)
    rows = lambda j: slice(j * BLK, (j + 1) * BLK)
    qi = lax.broadcasted_iota(jnp.int32, (BLK, BLK), 0)
    ki = lax.broadcasted_iota(jnp.int32, (BLK, BLK), 1)
    rel_cur = qi - ki
    rel_prev = rel_cur + BLK
    first = rel_prev <= jnp.where(nb > 0, hops, -1)
    later = rel_prev <= hops
    ok_cur = jnp.concatenate([rel_cur >= 0 for _ in units], axis=0)
    ok_prev = jnp.concatenate([first if j == 0 else later for j, _ in units], axis=0)
    dist_cur = (rel_cur * dilation).astype(F32)
    dist_prev = (rel_prev * dilation).astype(F32)
    bias_cur = jnp.concatenate([slopes[h] * dist_cur for _, h in units], axis=0)
    bias_prev = jnp.concatenate([slopes[h] * dist_prev for _, h in units], axis=0)
    s_cur = jnp.concatenate(
        [lax.dot_general(q[rows(j), cols(h)], kk[rows(j + 1), cols(h)], nt,
                         preferred_element_type=F32) for j, h in units], axis=0)
    s_prev = jnp.concatenate(
        [lax.dot_general(q[rows(j), cols(h)], kk[rows(j), cols(h)], nt,
                         preferred_element_type=F32) for j, h in units], axis=0)
    l_cur = jnp.where(ok_cur, s_cur * scale - bias_cur, -jnp.inf)
    l_prev = jnp.where(ok_prev, s_prev * scale - bias_prev, -jnp.inf)
    m = jnp.max(jnp.maximum(l_cur, l_prev), axis=-1, keepdims=True)
    p_cur = jnp.exp(l_cur - m)
    p_prev = jnp.exp(l_prev - m)
    den = jnp.sum(p_cur + p_prev, axis=-1, keepdims=True)
    p_cur = p_cur.astype(BF16)
    p_prev = p_prev.astype(BF16)
    inv = 1.0 / den
    lse = m + jnp.log(den)
    lane = lax.broadcasted_iota(jnp.int32, (BLK, LANES), 1)
    for j in range(nblk):
        lse_tile = jnp.zeros((BLK, LANES), F32)
        for h in range(len(slopes)):
            u = rows(units.index((j, h)))
            acc = (jnp.dot(p_cur[u], vv[rows(j + 1), cols(h)], preferred_element_type=F32)
                   + jnp.dot(p_prev[u], vv[rows(j), cols(h)], preferred_element_type=F32))
            o_ref[rows(j), cols(h)] = (acc * inv[u]).astype(o_ref.dtype)
            lse_tile = jnp.where(lane == h, lse[u], lse_tile)
        lse_ref[rows(j), :] = lse_tile
    kp_ref[...] = k_ref[rows(nblk - 1), :]
    vp_ref[...] = v_ref[rows(nblk - 1), :]


def _attn_group(qkv, gi, *, batch, seq):
    window, dilation = DILATION_PATTERNS[gi]
    hops = window // dilation
    n_heads_total = ATTN_HEADS_PER_GROUP * len(DILATION_PATTERNS)
    slopes = tuple(float(2.0 ** (-8.0 * (gi * ATTN_HEADS_PER_GROUP + h + 1) / n_heads_total))
                   for h in range(ATTN_HEADS_PER_GROUP))
    gw = ATTN_HEADS_PER_GROUP * ATTN_HEAD_DIM
    assert seq % (dilation * ATTN_BLOCK) == 0
    sub = seq // dilation
    rows = ATTN_BLOCK * math.gcd(sub // ATTN_BLOCK, ATTN_BLOCKS_PER_STEP)
    nb = sub // rows
    kern = functools.partial(_attn_kernel, slopes=slopes, dilation=dilation, hops=hops)

    def part(p):
        return pl.BlockSpec((rows, gw), lambda b, r, n: (b * nb + n, p * dilation + r))

    o, lse = pl.pallas_call(
        kern,
        grid=(batch, dilation, nb),
        in_specs=[part(0), part(1), part(2)],
        out_specs=[
            pl.BlockSpec((rows, gw), lambda b, r, n: (b * nb + n, r)),
            pl.BlockSpec((rows, LANES), lambda b, r, n: (b * nb + n, r)),
        ],
        out_shape=[
            jax.ShapeDtypeStruct((batch * sub, dilation * gw), BF16),
            jax.ShapeDtypeStruct((batch * sub, dilation * LANES), F32),
        ],
        scratch_shapes=[pltpu.VMEM((ATTN_BLOCK, gw), BF16), pltpu.VMEM((ATTN_BLOCK, gw), BF16)],
        compiler_params=_params("arbitrary", "arbitrary", "arbitrary"),
        name=f"attn_g{gi}",
    )(qkv, qkv, qkv)
    return o, lse


def _merge_kernel(yn_ref, o0_ref, o1_ref, o2_ref, l0_ref, l1_ref, l2_ref, gate_ref,
                  wssd_ref, wattn_ref, bg_ref, merged_ref, ostage_ref, lstage_ref, *, d_model):
    E = ATTN_HEAD_DIM
    tm = merged_ref.shape[0]
    n_slabs = o0_ref.shape[1] // LANES
    y_ssd = jnp.dot(yn_ref[...], wssd_ref[...], preferred_element_type=F32)

    def token_major(gi, o_ref, l_ref):
        dil = DILATION_PATTERNS[gi][1]
        if dil == 1:
            return o_ref[...].astype(F32), l_ref[...]
        for r in range(dil):
            rows = pl.ds(r, tm // dil, stride=dil)
            lstage_ref[gi - 1, rows, :] = l_ref[:, r * LANES:(r + 1) * LANES]
            for s in range(n_slabs):
                c0 = (r * n_slabs + s) * LANES
                ostage_ref[gi - 1, s, rows, :] = o_ref[:, c0:c0 + LANES].astype(F32)
        out = jnp.concatenate([ostage_ref[gi - 1, s] for s in range(n_slabs)], axis=1)
        return out, lstage_ref[gi - 1]

    (o0, l0), (o1, l1), (o2, l2) = (token_major(gi, o_ref, l_ref) for gi, (o_ref, l_ref) in
                                    enumerate(((o0_ref, l0_ref), (o1_ref, l1_ref), (o2_ref, l2_ref))))

    lm = jnp.maximum(jnp.maximum(l0, l1), l2)
    e0, e1, e2 = jnp.exp(l0 - lm), jnp.exp(l1 - lm), jnp.exp(l2 - lm)
    inv = 1.0 / (e0 + e1 + e2)
    parts = []
    for h in range(ATTN_HEADS_PER_GROUP):
        sl = slice(h * E, (h + 1) * E)
        parts.append((e0[:, h:h + 1] * inv[:, h:h + 1]) * o0[:, sl]
                     + (e1[:, h:h + 1] * inv[:, h:h + 1]) * o1[:, sl]
                     + (e2[:, h:h + 1] * inv[:, h:h + 1]) * o2[:, sl])
    o = jnp.concatenate(parts, axis=-1).astype(BF16)
    y_attn = jnp.dot(o, wattn_ref[...], preferred_element_type=F32)

    gates = _sigmoid(gate_ref[...].astype(F32) + bg_ref[...])
    merged_ref[...] = (gates[:, :d_model] * y_ssd + gates[:, d_model:] * y_attn).astype(BF16)


def _merge(yn, outs, lses, proj, w_ssd_out, w_attn_out, b_gate, *, tm, gate_block):
    n, d = yn.shape
    aw = outs[0].shape[1]
    dils = [dil for _, dil in DILATION_PATTERNS]
    assert dils[0] == 1 and all(tm % (dil * SUBLANES) == 0 for dil in dils)
    row = lambda w: pl.BlockSpec((tm, w), lambda i: (i, 0))
    grouped = lambda dil, w: pl.BlockSpec((tm // dil, dil * w), lambda i: (i, 0))
    const = lambda shape: pl.BlockSpec(shape, lambda i: (0, 0), pipeline_mode=pl.Buffered(1))
    return pl.pallas_call(
        functools.partial(_merge_kernel, d_model=d),
        grid=(n // tm,),
        in_specs=[
            row(d), *[grouped(dil, aw) for dil in dils], *[grouped(dil, LANES) for dil in dils],
            pl.BlockSpec((tm, 2 * d), lambda i: (i, gate_block)),
            const((d, d)), const((aw, d)), const((1, 2 * d)),
        ],
        out_specs=row(d),
        out_shape=jax.ShapeDtypeStruct((n, d), BF16),
        scratch_shapes=[pltpu.VMEM((len(dils) - 1, aw // LANES, tm, LANES), F32),
                        pltpu.VMEM((len(dils) - 1, tm, LANES), F32)],
        compiler_params=_params("arbitrary"),
        name="merge",
    )(yn, outs[0], outs[1], outs[2], lses[0], lses[1], lses[2], proj,
      w_ssd_out.astype(BF16), w_attn_out.astype(BF16), b_gate.reshape(1, -1))


def _route_kernel(merged_ref, x_ref, wout_ref, fnw_ref, wrh_ref, wrl_ref, rb_ref,
                  x1_ref, eid_ref, gcol_ref, rank_ref, cnt_ref, carry_ref):
    i = pl.program_id(0)
    tm = x_ref.shape[0]

    @pl.when(i == 0)
    def _():
        carry_ref[...] = jnp.zeros(carry_ref.shape, F32)

    x1 = x_ref[...] + jnp.dot(merged_ref[...], wout_ref[...], preferred_element_type=F32)
    x1_ref[...] = x1

    ms = jnp.mean(x1 * x1, axis=-1, keepdims=True)
    hn = x1 * lax.rsqrt(ms + NORM_EPS) * fnw_ref[...]

    hn_hi, hn_lo = _split_bf16(hn)
    nt = (((1,), (1,)), ((), ()))
    logits = (lax.dot_general(wrh_ref[...], hn_hi, nt, preferred_element_type=F32)
              + lax.dot_general(wrh_ref[...], hn_lo, nt, preferred_element_type=F32)
              + lax.dot_general(wrl_ref[...], hn_hi, nt, preferred_element_type=F32)
              + rb_ref[...])

    grow = lax.broadcasted_iota(jnp.int32, (SUBLANES, tm), 0)
    gl = jnp.where(grow < N_EXPERT_GROUPS, logits[0:SUBLANES, :], -jnp.inf)
    gmax = jnp.max(gl, axis=0, keepdims=True)
    gidx = jnp.min(jnp.where(gl == gmax, grow, N_EXPERT_GROUPS), axis=0, keepdims=True)
    group_gate = 1.0 / jnp.sum(jnp.exp(gl - gmax), axis=0, keepdims=True)

    in_group = jnp.zeros((EXPERTS_PER_GROUP, tm), F32)
    for g in range(N_EXPERT_GROUPS):
        r0 = ROUTER_EXPERT_ROW0 + g * EXPERTS_PER_GROUP
        in_group = jnp.where(gidx == g, logits[r0:r0 + EXPERTS_PER_GROUP, :], in_group)
    erow = lax.broadcasted_iota(jnp.int32, in_group.shape, 0)
    v1 = jnp.max(in_group, axis=0, keepdims=True)
    i1 = jnp.min(jnp.where(in_group == v1, erow, EXPERTS_PER_GROUP), axis=0, keepdims=True)
    rest = jnp.where(erow == i1, -jnp.inf, in_group)
    v2 = jnp.max(rest, axis=0, keepdims=True)
    i2 = jnp.min(jnp.where(rest == v2, erow, EXPERTS_PER_GROUP), axis=0, keepdims=True)
    t = jnp.exp(v2 - v1)
    g1 = group_gate / (1.0 + t)
    g2 = group_gate * t / (1.0 + t)
    eid1 = gidx * EXPERTS_PER_GROUP + i1
    eid2 = gidx * EXPERTS_PER_GROUP + i2
    slot = lax.broadcasted_iota(jnp.int32, (SUBLANES, tm), 0)
    eid_ref[...] = jnp.where(slot == 0, eid1, jnp.where(slot == 1, eid2, 0))

    grow8 = lax.broadcasted_iota(jnp.int32, (LANES, tm), 0)
    gt = jnp.where(grow8 == 0, g1, jnp.where(grow8 == 1, g2, 0.0))
    gcol_ref[...] = gt.T

    xrow = lax.broadcasted_iota(jnp.int32, (N_EXPERTS, tm), 0)
    oh1 = xrow == eid1
    oh2 = xrow == eid2
    oh = jnp.where(oh1 | oh2, 1.0, 0.0)
    ti = lax.broadcasted_iota(jnp.int32, (tm, tm), 0)
    tj = lax.broadcasted_iota(jnp.int32, (tm, tm), 1)
    before = jnp.where(ti < tj, 1.0, 0.0).astype(BF16)
    prior = jnp.dot(oh.astype(BF16), before, preferred_element_type=F32) + carry_ref[:, 0:1]
    r1 = jnp.sum(jnp.where(oh1, prior, 0.0), axis=0, keepdims=True)
    r2 = jnp.sum(jnp.where(oh2, prior, 0.0), axis=0, keepdims=True)
    rank_ref[...] = jnp.where(slot == 0, r1, jnp.where(slot == 1, r2, 0.0)).astype(jnp.int32)
    carry_ref[...] = carry_ref[...] + jnp.sum(oh, axis=1, keepdims=True)
    cnt_ref[...] = carry_ref[...].astype(jnp.int32)


def _route(merged, x2, w_out, ffn_norm_w, wr, rb, *, tm):
    n, d = x2.shape
    wrt = jnp.zeros((ROUTER_ROWS, d), F32)
    wrt = wrt.at[0:N_EXPERT_GROUPS].set(wr[0].T).at[ROUTER_EXPERT_ROW0:].set(wr[1].T)
    rbc = jnp.zeros((ROUTER_ROWS, 1), F32)
    rbc = rbc.at[0:N_EXPERT_GROUPS, 0].set(rb[0]).at[ROUTER_EXPERT_ROW0:, 0].set(rb[1])
    wrh, wrl = _split_bf16(wrt)
    row = lambda w: pl.BlockSpec((tm, w), lambda i: (i, 0))
    const = lambda shape: pl.BlockSpec(shape, lambda i: (0, 0), pipeline_mode=pl.Buffered(1))
    return pl.pallas_call(
        _route_kernel,
        grid=(n // tm,),
        in_specs=[
            row(d), row(d),
            const((d, d)), const((1, d)),
            const((ROUTER_ROWS, d)), const((ROUTER_ROWS, d)), const((ROUTER_ROWS, 1)),
        ],
        out_specs=[
            row(d),
            pl.BlockSpec((SUBLANES, tm), lambda i: (0, i)),
            row(LANES),
            pl.BlockSpec((SUBLANES, tm), lambda i: (0, i)),
            pl.BlockSpec((N_EXPERTS, LANES), lambda i: (0, 0)),
        ],
        out_shape=[
            jax.ShapeDtypeStruct((n, d), F32),
            jax.ShapeDtypeStruct((SUBLANES, n), jnp.int32),
            jax.ShapeDtypeStruct((n, LANES), F32),
            jax.ShapeDtypeStruct((SUBLANES, n), jnp.int32),
            jax.ShapeDtypeStruct((N_EXPERTS, LANES), jnp.int32),
        ],
        scratch_shapes=[pltpu.VMEM((N_EXPERTS, LANES), F32)],
        compiler_params=_params("arbitrary"),
        name="route",
    )(merged, x2, w_out.astype(BF16), ffn_norm_w.reshape(1, -1), wrh, wrl, rbc)


def _scatter_kernel(pad_start_ref, pad_len_ref, n_active_ref, pos_ref, hn_ref, xs_ref, zero_ref,
                    sem, zsem, *, tme, n_tiles):
    tm = hn_ref.shape[0]
    zrows = zero_ref.shape[0]

    @pl.when(pl.program_id(0) == 0)
    def _():
        zero_ref[...] = jnp.zeros(zero_ref.shape, zero_ref.dtype)

        def fills(act):
            def per_expert(e, carry):
                off = pad_start_ref[e]
                left = pad_len_ref[e]
                head = left & (SUBLANES - 1)
                for r in range(SUBLANES - 1):
                    @pl.when(r < head)
                    def _(r=r):
                        act(pltpu.make_async_copy(zero_ref.at[pl.ds(0, 1), :],
                                                  xs_ref.at[pl.ds(off + r, 1), :], zsem))

                off = off + head
                bit = zrows
                while bit >= SUBLANES:
                    take = left & bit

                    @pl.when(take != 0)
                    def _(off=off, bit=bit):
                        act(pltpu.make_async_copy(
                            zero_ref.at[pl.ds(0, bit), :],
                            xs_ref.at[pl.ds(pl.multiple_of(off, SUBLANES), bit), :], zsem))

                    off = off + take
                    bit //= 2
                return carry

            def per_tile(t, carry):
                @pl.when(t >= n_active_ref[0])
                def _():
                    for part in range(tme // zrows):
                        row0 = pl.multiple_of(t * tme + part * zrows, zrows)
                        act(pltpu.make_async_copy(zero_ref, xs_ref.at[pl.ds(row0, zrows), :], zsem))

                return carry

            lax.fori_loop(0, N_EXPERTS, per_expert, 0)
            lax.fori_loop(0, n_tiles, per_tile, 0)

        fills(lambda cp: cp.start())
        fills(lambda cp: cp.wait())

    def start(tb, carry):
        for u in range(ROW_DMA_UNROLL):
            t = tb * ROW_DMA_UNROLL + u
            for k in range(EXPERT_TOP_K):
                dst = pos_ref[0, 0, k * tm + t]
                pltpu.make_async_copy(hn_ref.at[pl.ds(t, 1), :], xs_ref.at[pl.ds(dst, 1), :],
                                      sem).start(priority=(u + k) % 2)
        return carry

    lax.fori_loop(0, tm // ROW_DMA_UNROLL, start, 0)
    for k in range(EXPERT_TOP_K):
        pltpu.make_async_copy(hn_ref, xs_ref.at[pl.ds(0, tm), :], sem).wait()


def _scatter(hn, pos_tiles, pad_start, pad_len, n_active, *, tm, tme, n_tiles):
    n, d = hn.shape
    assert tme % 2 == 0 and (tme // 2) & (tme // 2 - 1) == 0
    grid_spec = pltpu.PrefetchScalarGridSpec(
        num_scalar_prefetch=3,
        grid=(n // tm,),
        in_specs=[
            pl.BlockSpec((1, 1, EXPERT_TOP_K * tm), lambda i, *_: (i, 0, 0),
                         memory_space=pltpu.SMEM),
            pl.BlockSpec((tm, d), lambda i, *_: (i, 0)),
        ],
        out_specs=pl.BlockSpec(memory_space=pl.ANY),
        scratch_shapes=[pltpu.VMEM((tme // 2, d), hn.dtype), pltpu.SemaphoreType.DMA(()),
                        pltpu.SemaphoreType.DMA(())],
    )
    return pl.pallas_call(
        functools.partial(_scatter_kernel, tme=tme, n_tiles=n_tiles),
        grid_spec=grid_spec,
        out_shape=jax.ShapeDtypeStruct((n_tiles * tme, d), hn.dtype),
        compiler_params=_params("arbitrary"),
        name="scatter",
    )(pad_start, pad_len, n_active, pos_tiles, hn)


W_CHUNK_ROWS = 256
W_RING = 8


def _experts_kernel(te_ref, slot_ref, nxt_ref, lo_ref, hi_ref, na_ref, xs_ref, fnw_ref, wg_hbm,
                    wu_hbm, wd_hbm, ys_ref, wres_ref, stage_ref, sem, *, n_gu, n_r, n_h):
    i = pl.program_id(0)
    n_chunks = 2 * n_gu + n_h * n_r
    ring, cr, f = stage_ref.shape
    d = n_gu * cr

    def stage_copy(src, c):
        return pltpu.make_async_copy(src, stage_ref.at[c % ring], sem.at[c % ring])

    def start_chunk(e, c):
        @pl.when(c < n_gu)
        def _():
            stage_copy(wg_hbm.at[e, pl.ds(pl.multiple_of(c * cr, cr), cr), :], c).start()

        @pl.when((c >= n_gu) & (c < 2 * n_gu))
        def _():
            stage_copy(wu_hbm.at[e, pl.ds(pl.multiple_of((c - n_gu) * cr, cr), cr), :], c).start()

        @pl.when(c >= 2 * n_gu)
        def _():
            q = c - 2 * n_gu
            h = q // n_r
            r = q - h * n_r
            stage_copy(wd_hbm.at[e, pl.ds(pl.multiple_of(r * cr, cr), cr),
                                 pl.ds(pl.multiple_of(h * f, f), f)], c).start()

    def start_first(e):
        def body(c, carry):
            start_chunk(e, c)
            return carry

        lax.fori_loop(0, ring, body, 0)

    def convert(e, lo, hi, dst_slot):
        def body(c, carry):
            stage_copy(wg_hbm.at[0, pl.ds(0, cr), :], c).wait()
            wres_ref[dst_slot, pl.ds(pl.multiple_of(c * cr, cr), cr), :] = (
                stage_ref[c % ring].astype(BF16))

            @pl.when(c + ring < n_chunks)
            def _():
                start_chunk(e, c + ring)

            return carry

        lax.fori_loop(lo, hi, body, 0)

    @pl.when(i == 0)
    def _():
        start_first(te_ref[0])
        convert(te_ref[0], 0, n_chunks, slot_ref[0])

    @pl.when(i < na_ref[0])
    def _():
        slot = slot_ref[i]
        nxt = nxt_ref[i]

        @pl.when((nxt >= 0) & (lo_ref[i] == 0))
        def _():
            start_first(nxt)

        xr = xs_ref[...]
        ms = jnp.mean(xr * xr, axis=-1, keepdims=True)
        xb = (xr * lax.rsqrt(ms + NORM_EPS) * fnw_ref[...]).astype(BF16)
        hg = jnp.dot(xb, wres_ref[slot, 0:d, :], preferred_element_type=F32)
        hu = jnp.dot(xb, wres_ref[slot, d:2 * d, :], preferred_element_type=F32)
        hmid = (_silu(hg) * hu).astype(BF16)
        for h in range(n_h):
            r0 = 2 * d + h * n_r * cr
            ys_ref[:, h * f:(h + 1) * f] = jnp.dot(hmid, wres_ref[slot, r0:r0 + n_r * cr, :],
                                                    preferred_element_type=F32)

        @pl.when(nxt >= 0)
        def _():
            convert(nxt, lo_ref[i], hi_ref[i], 1 - slot)

    @pl.when(i >= na_ref[0])
    def _():
        ys_ref[...] = jnp.zeros(ys_ref.shape, F32)


def _experts(xs, norm_w, tile_expert, n_active, cnt, starts, padded, w_g, w_u, w_d, *, tm):
    p, d = xs.shape
    ne, _, f = w_g.shape
    n_tiles = p // tm
    cr = W_CHUNK_ROWS
    assert d % cr == 0 and f % cr == 0 and d % f == 0
    n_gu, n_r, n_h = d // cr, f // cr, d // f
    n_chunks = 2 * n_gu + n_h * n_r

    ids = jnp.arange(ne, dtype=jnp.int32)
    present = cnt > 0
    later = jnp.where(present[None, :] & (ids[None, :] > ids[:, None]), ids[None, :], ne)
    nxt_e = jnp.min(later, axis=1)
    nxt_e = jnp.where(nxt_e >= ne, -1, nxt_e)
    run_e = jnp.cumsum(present.astype(jnp.int32)) - 1
    onehot = (tile_expert[:, None] == ids[None, :]).astype(jnp.int32)
    pick = lambda v: jnp.sum(onehot * v[None, :].astype(jnp.int32), axis=1)
    tiles = jnp.arange(n_tiles, dtype=jnp.int32)
    active = tiles < n_active[0]
    j = tiles - pick(starts) // tm
    k = jnp.maximum(pick(padded) // tm, 1)
    nxt_t = jnp.where(active, pick(nxt_e), -1).astype(jnp.int32)
    lo_t = jnp.where(active, (n_chunks * j) // k, 0).astype(jnp.int32)
    hi_t = jnp.where(active, (n_chunks * (j + 1)) // k, 0).astype(jnp.int32)
    slot_t = (pick(run_e) % 2).astype(jnp.int32)

    def rows(i, te, sl, nx, lo, hi, na):
        return (jnp.minimum(i, na[0] - 1), 0)

    grid_spec = pltpu.PrefetchScalarGridSpec(
        num_scalar_prefetch=6,
        grid=(n_tiles,),
        in_specs=[
            pl.BlockSpec((tm, d), rows),
            pl.BlockSpec((1, d), lambda i, *_: (0, 0)),
            pl.BlockSpec(memory_space=pl.ANY),
            pl.BlockSpec(memory_space=pl.ANY),
            pl.BlockSpec(memory_space=pl.ANY),
        ],
        out_specs=pl.BlockSpec((tm, d), lambda i, *_: (i, 0)),
        scratch_shapes=[
            pltpu.VMEM((2, n_chunks * cr, f), BF16),
            pltpu.VMEM((W_RING, cr, f), F32),
            pltpu.SemaphoreType.DMA((W_RING,)),
        ],
    )
    return pl.pallas_call(
        functools.partial(_experts_kernel, n_gu=n_gu, n_r=n_r, n_h=n_h),
        grid_spec=grid_spec,
        out_shape=jax.ShapeDtypeStruct((p, d), F32),
        compiler_params=_params("arbitrary"),
        name="experts",
    )(tile_expert, slot_t, nxt_t, lo_t, hi_t, n_active, xs, norm_w.reshape(1, d), w_g, w_u, w_d)


def _combine_kernel(pos_ref, pos_next_ref, ys_ref, x1_ref, gcol_ref, nw_ref, out_ref, buf_ref, sem,
                    *, final):
    i = pl.program_id(0)
    n_steps = pl.num_programs(0)
    tm = x1_ref.shape[0]

    def issue(p_ref, slot):
        def start(tb, carry):
            for u in range(ROW_DMA_UNROLL):
                t = tb * ROW_DMA_UNROLL + u
                for k in range(EXPERT_TOP_K):
                    src = p_ref[0, 0, k * tm + t]
                    pltpu.make_async_copy(ys_ref.at[pl.ds(src, 1), :],
                                          buf_ref.at[slot, k, pl.ds(t, 1), :],
                                          sem.at[slot]).start(priority=(u + k) % 2)
            return carry

        lax.fori_loop(0, tm // ROW_DMA_UNROLL, start, 0)

    slot = i % 2

    @pl.when(i == 0)
    def _():
        issue(pos_ref, 0)

    @pl.when(i + 1 < n_steps)
    def _():
        issue(pos_next_ref, 1 - slot)

    for k in range(EXPERT_TOP_K):
        pltpu.make_async_copy(ys_ref.at[pl.ds(0, tm), :], buf_ref.at[slot, k], sem.at[slot]).wait()
    g = gcol_ref[...]
    xo = x1_ref[...] + g[:, 0:1] * buf_ref[slot, 0] + g[:, 1:2] * buf_ref[slot, 1]
    if final:
        ms = jnp.mean(xo * xo, axis=-1, keepdims=True)
        xo = xo * lax.rsqrt(ms + NORM_EPS) * nw_ref[...]
    out_ref[...] = xo


def _combine(ys, pos_tiles, x1, gcol, norm_w, *, tm, final):
    n, d = x1.shape
    last = n // tm - 1
    return pl.pallas_call(
        functools.partial(_combine_kernel, final=final),
        grid=(n // tm,),
        in_specs=[
            pl.BlockSpec((1, 1, EXPERT_TOP_K * tm), lambda i: (i, 0, 0), memory_space=pltpu.SMEM),
            pl.BlockSpec((1, 1, EXPERT_TOP_K * tm), lambda i: (jnp.minimum(i + 1, last), 0, 0),
                         memory_space=pltpu.SMEM),
            pl.BlockSpec(memory_space=pl.ANY),
            pl.BlockSpec((tm, d), lambda i: (i, 0)),
            pl.BlockSpec((tm, LANES), lambda i: (i, 0)),
            pl.BlockSpec((1, d), lambda i: (0, 0)),
        ],
        out_specs=pl.BlockSpec((tm, d), lambda i: (i, 0)),
        out_shape=jax.ShapeDtypeStruct((n, d), F32),
        scratch_shapes=[pltpu.VMEM((2, EXPERT_TOP_K, tm, d), F32), pltpu.SemaphoreType.DMA((2,))],
        compiler_params=_params("arbitrary"),
        name="combine",
    )(pos_tiles, pos_tiles, ys, x1, gcol, norm_w.reshape(1, d))


def _tiles(n):
    return dict(proj_tm=math.gcd(n, 1024), mix_tm=math.gcd(n, 512), moe_tm=256,
                route_tm=math.gcd(n, 512))


def _layer(x2, batch, seq, layer, attn_norm_w, w_in_stack, b_gate, conv_w, conv_b, dt_bias, a_log,
           d_skip, ssd_norm_w, w_ssd_out, w_attn_out, w_out, ffn_norm_w, w_gr, b_gr, w_er, b_er,
           w_g, w_u, w_d):
    n, d = x2.shape
    n_heads = dt_bias.shape[0]
    d_inner = ssd_norm_w.shape[0]
    conv_dim = conv_w.shape[1]
    aw_total = ATTN_HEADS_PER_GROUP * len(DILATION_PATTERNS) * ATTN_HEAD_DIM
    gw = ATTN_HEADS_PER_GROUP * ATTN_HEAD_DIM
    tiles = _tiles(n)

    c_z, c_xbc, c_dt = d_inner, d_inner + conv_dim, d_inner + conv_dim + n_heads
    c_gate = c_dt + QKV_PARTS * aw_total
    segments = [(c_gate, 2 * d), (c_z, conv_dim), (0, d_inner)]
    segments += [(c_dt + p * aw_total + gi * gw, gw)
                 for gi in range(len(DILATION_PATTERNS)) for p in range(QKV_PARTS)]
    assert all(width % gw == 0 for _, width in segments)
    starts = tuple(start + b * gw for start, width in segments for b in range(width // gw))
    w_main, w_dt = _regroup_weight(w_in_stack, layer, starts, gw, c_xbc, n_heads)
    off_gate, off_xbc, off_z = 0, 2 * d, 2 * d + conv_dim
    plain_cols = off_z + d_inner
    assert off_xbc % conv_dim == 0 and off_z % d_inner == 0 and plain_cols % gw == 0

    proj, qkv0, qkv1, qkv2, dt, dtt = _in_proj(x2, attn_norm_w, w_main, w_dt, tm=tiles["proj_tm"],
                                               tn=gw, tp=4 * gw, plain_cols=plain_cols, nh=n_heads)

    yn = _ssd(proj, dt, dtt, conv_w, conv_b, dt_bias, a_log, d_skip, ssd_norm_w, batch=batch,
              seq=seq, d_inner=d_inner, xbc_block=off_xbc // conv_dim, z_block=off_z // d_inner)

    outs, lses = [], []
    for gi, qkv in enumerate((qkv0, qkv1, qkv2)):
        o_g, lse_g = _attn_group(qkv, gi, batch=batch, seq=seq)
        outs.append(o_g)
        lses.append(lse_g)

    merged = _merge(yn, outs, lses, proj, w_ssd_out, w_attn_out, b_gate, tm=tiles["mix_tm"],
                    gate_block=off_gate // (2 * d))
    x1, eid, gcol, rank, counts = _route(merged, x2, w_out, ffn_norm_w, (w_gr, w_er),
                                         (b_gr, b_er), tm=tiles["mix_tm"])

    tme = tiles["moe_tm"]
    cnt = counts[:, 0]
    padded = ((cnt + tme - 1) // tme) * tme
    ends = jnp.cumsum(padded)
    starts = ends - padded
    experts = jnp.arange(N_EXPERTS, dtype=jnp.int32)[:, None, None]
    pos = rank[:EXPERT_TOP_K] + jnp.sum(
        jnp.where(eid[None, :EXPERT_TOP_K] == experts, starts[:, None, None], 0), axis=0)
    n_tiles = EXPERT_TOP_K * n // tme + N_EXPERTS
    tile_start = jnp.arange(n_tiles, dtype=jnp.int32) * tme
    tile_expert = jnp.minimum(jnp.sum(ends[None, :] <= tile_start[:, None], axis=1),
                              N_EXPERTS - 1).astype(jnp.int32)
    n_active = (ends[-1:] // tme).astype(jnp.int32)

    rtm = tiles["route_tm"]
    pos_tiles = pos.reshape(EXPERT_TOP_K, n // rtm, rtm).transpose(1, 0, 2).reshape(
        n // rtm, 1, EXPERT_TOP_K * rtm)
    xs = _scatter(x1, pos_tiles, (starts + cnt).astype(jnp.int32), (padded - cnt).astype(jnp.int32),
                  n_active, tm=rtm, tme=tme, n_tiles=n_tiles)
    ys = _experts(xs, ffn_norm_w, tile_expert, n_active, cnt, starts, padded, w_g, w_u, w_d, tm=tme)
    return ys, pos_tiles, x1, gcol, rtm


def kernel(x, attn_norm_w, w_in, b_gate, conv_w, conv_b, dt_bias, a_log, d_skip, ssd_norm_w,
           w_ssd_out, w_attn_out, w_out, ffn_norm_w, w_group_router, b_group_router,
           w_expert_router, b_expert_router, w_exp_gate, w_exp_up, w_exp_down, final_norm_w):
    batch, seq, d = x.shape
    depth = w_in.shape[0]
    x2 = x.reshape(batch * seq, d)
    for layer in range(depth):
        ys, pos_tiles, x1, gcol, rtm = _layer(
            x2, batch, seq, layer, attn_norm_w[layer], w_in, b_gate[layer], conv_w[layer],
            conv_b[layer], dt_bias[layer], a_log[layer], d_skip[layer], ssd_norm_w[layer],
            w_ssd_out[layer], w_attn_out[layer], w_out[layer], ffn_norm_w[layer],
            w_group_router[layer], b_group_router[layer], w_expert_router[layer],
            b_expert_router[layer], w_exp_gate[layer], w_exp_up[layer], w_exp_down[layer])
        x2 = _combine(ys, pos_tiles, x1, gcol, final_norm_w, tm=rtm, final=layer == depth - 1)
    return x2.reshape(batch, seq, d)
```

```python
import functools
import math

import jax
import jax.numpy as jnp
import numpy as np
from jax import lax
from jax.experimental import pallas as pl
from jax.experimental.pallas import tpu as pltpu

F32 = jnp.float32
BF16 = jnp.bfloat16

NORM_EPS = 1e-6
SSD_HEAD_DIM = 64
SSD_N_GROUPS = 8
SSD_D_STATE = 128
SSD_CONV_WIDTH = 4
SSD_CHUNK = 128
SSD_CHUNKS_PER_STEP = 4
ATTN_HEAD_DIM = 128
DILATION_PATTERNS = ((128, 1), (512, 4), (2048, 16))
ATTN_HEADS_PER_GROUP = 4
ATTN_BLOCK = 128
ATTN_BLOCKS_PER_STEP = 4
N_EXPERT_GROUPS = 4
EXPERTS_PER_GROUP = 8
N_EXPERTS = N_EXPERT_GROUPS * EXPERTS_PER_GROUP
EXPERT_TOP_K = 2

LANES = 128
SUBLANES = 8
VMEM_LIMIT_BYTES = 56 * 1024 * 1024
ROW_DMA_UNROLL = 8

ROUTER_EXPERT_ROW0 = SUBLANES
ROUTER_ROWS = ROUTER_EXPERT_ROW0 + N_EXPERTS


def _params(*semantics):
    return pltpu.CompilerParams(dimension_semantics=semantics, vmem_limit_bytes=VMEM_LIMIT_BYTES)


def _split_bf16(v):
    hi = v.astype(BF16)
    lo = (v - hi.astype(F32)).astype(BF16)
    return hi, lo


def _sigmoid(v):
    return 0.5 + 0.5 * jnp.tanh(0.5 * v)


def _silu(v):
    h = 0.5 * v
    return h + h * jnp.tanh(h)


def _regroup_weight_kernel(starts_ref, wt_hbm, out_ref, narrow_ref, buf_ref, nbuf_ref, sem, nsem,
                           *, layer, narrow_start, nh):
    i = pl.program_id(0)
    n_steps = pl.num_programs(0)
    tn = buf_ref.shape[1]

    def fetch(step, slot):
        rows = pl.ds(pl.multiple_of(starts_ref[step], SUBLANES), tn)
        return pltpu.make_async_copy(wt_hbm.at[layer, rows, :], buf_ref.at[slot], sem.at[slot])

    narrow = pltpu.make_async_copy(wt_hbm.at[layer, pl.ds(narrow_start, LANES), :], nbuf_ref, nsem)

    @pl.when(i == 0)
    def _():
        fetch(0, 0).start()
        narrow.start()

    @pl.when(i + 1 < n_steps)
    def _():
        fetch(i + 1, (i + 1) % 2).start()

    fetch(i, i % 2).wait()
    out_ref[...] = buf_ref[i % 2].T.astype(out_ref.dtype)

    @pl.when(i == n_steps - 1)
    def _():
        narrow.wait()
        lane = lax.broadcasted_iota(jnp.int32, narrow_ref.shape, 1)
        narrow_ref[...] = jnp.where(lane < nh, nbuf_ref[...].T, 0.0).astype(narrow_ref.dtype)


def _regroup_weight(w_stack, layer, starts, tn, narrow_start, nh):
    _, k, cols = w_stack.shape
    assert all(s % SUBLANES == 0 and s + tn <= cols for s in starts)
    assert narrow_start % SUBLANES == 0 and narrow_start + LANES <= cols and nh <= LANES
    wt = jnp.swapaxes(w_stack, 1, 2)
    grid_spec = pltpu.PrefetchScalarGridSpec(
        num_scalar_prefetch=1,
        grid=(len(starts),),
        in_specs=[pl.BlockSpec(memory_space=pl.ANY)],
        out_specs=[pl.BlockSpec((k, tn), lambda i, *_: (0, i)),
                   pl.BlockSpec((k, LANES), lambda i, *_: (0, 0))],
        scratch_shapes=[pltpu.VMEM((2, tn, k), F32), pltpu.VMEM((LANES, k), F32),
                        pltpu.SemaphoreType.DMA((2,)), pltpu.SemaphoreType.DMA(())],
    )
    return pl.pallas_call(
        functools.partial(_regroup_weight_kernel, layer=layer, narrow_start=narrow_start, nh=nh),
        grid_spec=grid_spec,
        out_shape=[jax.ShapeDtypeStruct((k, len(starts) * tn), BF16),
                   jax.ShapeDtypeStruct((k, LANES), BF16)],
        compiler_params=_params("arbitrary"),
        name="regroup_weight",
    )(jnp.asarray(starts, jnp.int32), wt)


QKV_PARTS = 3


def _in_proj_kernel(x_hbm, nw_ref, wp_ref, w_ref, wdt_ref, proj_ref, a0_ref, a1_ref, a2_ref,
                    dt_ref, dtt_ref, h_ref, stage_ref, x_ref, xsem, *, n_plain):
    i = pl.program_id(0)
    j = pl.program_id(1)
    tm = proj_ref.shape[0]
    tn = w_ref.shape[1]

    def fetch_x(tile):
        return pltpu.make_async_copy(x_hbm.at[pl.ds(pl.multiple_of(tile * tm, tm), tm), :], x_ref, xsem)

    @pl.when((i == 0) & (j == 0))
    def _():
        fetch_x(0).start()

    @pl.when(j == 0)
    def _():
        fetch_x(i).wait()
        xf = x_ref[...]
        ms = jnp.mean(xf * xf, axis=-1, keepdims=True)
        h = (xf * lax.rsqrt(ms + NORM_EPS) * nw_ref[...]).astype(BF16)
        h_ref[...] = h
        nh = dt_ref.shape[1]
        dt_wide = jnp.dot(h, wdt_ref[...], preferred_element_type=F32)
        dt_ref[...] = dt_wide[:, :nh]
        dtt_ref[...] = dt_wide.T[:nh, :]

        @pl.when(i + 1 < pl.num_programs(0))
        def _():
            fetch_x(i + 1).start()

    @pl.when(j < n_plain)
    def _():
        proj_ref[...] = jnp.dot(h_ref[...], wp_ref[...], preferred_element_type=F32).astype(BF16)

    for gi, a_ref in enumerate((a0_ref, a1_ref, a2_ref)):
        dil = DILATION_PATTERNS[gi][1]
        j0 = n_plain + QKV_PARTS * gi

        @pl.when((j >= j0) & (j < j0 + QKV_PARTS))
        def _(a_ref=a_ref, dil=dil):
            res = jnp.dot(h_ref[...], w_ref[...], preferred_element_type=F32)
            if dil == 1:
                a_ref[...] = res.astype(BF16)
            else:
                for s in range(tn // LANES):
                    stage_ref[s] = res[:, s * LANES:(s + 1) * LANES]
                for r in range(dil):
                    for s in range(tn // LANES):
                        c0 = r * tn + s * LANES
                        a_ref[:, c0:c0 + LANES] = stage_ref[
                            s, pl.ds(r, tm // dil, stride=dil), :].astype(BF16)


def _in_proj(x2, norm_w, w_main, w_dt, *, tm, tn, tp, plain_cols, nh):
    n, d = x2.shape
    assert plain_cols % tp == 0 and plain_cols % tn == 0
    n_plain = plain_cols // tp
    qkv_block0 = plain_cols // tn
    n_blocks = n_plain + QKV_PARTS * len(DILATION_PATTERNS)
    assert w_main.shape[1] == plain_cols + QKV_PARTS * len(DILATION_PATTERNS) * tn

    def a_spec(gi):
        dil = DILATION_PATTERNS[gi][1]
        j0 = n_plain + QKV_PARTS * gi
        return pl.BlockSpec((tm // dil, dil * tn),
                            lambda i, j: (i, jnp.clip(j - j0, 0, QKV_PARTS - 1)))

    def a_shape(gi):
        dil = DILATION_PATTERNS[gi][1]
        return jax.ShapeDtypeStruct((n // dil, dil * QKV_PARTS * tn), BF16)

    return pl.pallas_call(
        functools.partial(_in_proj_kernel, n_plain=n_plain),
        grid=(n // tm, n_blocks),
        in_specs=[
            pl.BlockSpec(memory_space=pl.ANY),
            pl.BlockSpec((1, d), lambda i, j: (0, 0)),
            pl.BlockSpec((d, tp), lambda i, j: (0, jnp.minimum(j, n_plain - 1))),
            pl.BlockSpec((d, tn), lambda i, j: (0, qkv_block0 + jnp.maximum(j - n_plain, 0))),
            pl.BlockSpec((d, LANES), lambda i, j: (0, 0)),
        ],
        out_specs=[
            pl.BlockSpec((tm, tp), lambda i, j: (i, jnp.minimum(j, n_plain - 1))),
            a_spec(0), a_spec(1), a_spec(2),
            pl.BlockSpec((tm, nh), lambda i, j: (i, 0)),
            pl.BlockSpec((nh, tm), lambda i, j: (0, i)),
        ],
        out_shape=[
            jax.ShapeDtypeStruct((n, plain_cols), BF16),
            a_shape(0), a_shape(1), a_shape(2),
            jax.ShapeDtypeStruct((n, nh), F32),
            jax.ShapeDtypeStruct((nh, n), F32),
        ],
        scratch_shapes=[pltpu.VMEM((tm, d), BF16), pltpu.VMEM((tn // LANES, tm, LANES), F32),
                        pltpu.VMEM((tm, d), x2.dtype), pltpu.SemaphoreType.DMA(())],
        compiler_params=_params("arbitrary", "arbitrary"),
        name="in_proj",
    )(x2, norm_w.reshape(1, d), w_main, w_main, w_dt)


LOG2E = math.log2(math.e)
CONV_ROW_PITCH = 2


def _ssd_kernel(xbc_ref, z_ref, dt_ref, dtt_ref, cw_ref, cb_ref, dtb_ref, dtbt_ref, alog_ref,
                alogt_ref, dskip_ref, nw_ref, expand_ref, out_ref,
                xbuf_ref, state_ref, y_ref, *, n_heads, d_inner):
    L = SSD_CHUNK
    P = SSD_HEAD_DIM
    NS = SSD_D_STATE
    G = SSD_N_GROUPS
    R = n_heads // G
    GW = R * P
    W = SSD_CONV_WIDTH
    RP = CONV_ROW_PITCH
    n_slabs = xbuf_ref.shape[0]
    c = pl.program_id(1)

    def rows(first, count):
        return pl.ds(first * RP, count, stride=RP)

    @pl.when(c == 0)
    def _():
        state_ref[...] = jnp.zeros(state_ref.shape, F32)
        for s in range(n_slabs):
            xbuf_ref[s, rows(0, SUBLANES), :] = jnp.zeros((SUBLANES, LANES), F32)

    n_rows = xbc_ref.shape[0]

    @pl.when(c > 0)
    def _():
        for s in range(n_slabs):
            xbuf_ref[s, rows(0, SUBLANES), :] = xbuf_ref[s, rows(n_rows, SUBLANES), :]

    for s in range(n_slabs):
        xbuf_ref[s, rows(SUBLANES, n_rows), :] = xbc_ref[:, s * LANES:(s + 1) * LANES].astype(F32)

    for sub in range(n_rows // L):
        _ssd_chunk(sub * L, xbuf_ref, z_ref, dt_ref, dtt_ref, cw_ref, cb_ref, dtb_ref, dtbt_ref,
                   alog_ref, alogt_ref, dskip_ref, nw_ref, expand_ref, out_ref, state_ref, y_ref,
                   n_heads=n_heads, d_inner=d_inner)


def _ssd_chunk(r0, xbuf_ref, z_ref, dt_ref, dtt_ref, cw_ref, cb_ref, dtb_ref, dtbt_ref, alog_ref,
               alogt_ref, dskip_ref, nw_ref, expand_ref, out_ref, state_ref, y_ref, *, n_heads,
               d_inner):
    L = SSD_CHUNK
    P = SSD_HEAD_DIM
    NS = SSD_D_STATE
    G = SSD_N_GROUPS
    R = n_heads // G
    GW = R * P
    W = SSD_CONV_WIDTH
    RP = CONV_ROW_PITCH
    tok = slice(r0, r0 + L)

    def rows(first, count):
        return pl.ds(first * RP, count, stride=RP)

    def conv(col0, width):
        slabs = []
        for s in range(col0 // LANES, (col0 + width) // LANES):
            acc = cb_ref[:, s * LANES:(s + 1) * LANES]
            for w in range(W):
                acc = acc + (cw_ref[w:w + 1, s * LANES:(s + 1) * LANES]
                             * xbuf_ref[s, rows(r0 + SUBLANES - (W - 1) + w, L), :])
            slabs.append(acc)
        return _silu(jnp.concatenate(slabs, axis=1))

    def softplus(v):
        return jnp.maximum(v, 0.0) + jnp.log1p(jnp.exp(-jnp.abs(v)))

    dt = softplus(dt_ref[tok, :] + dtb_ref[...])
    dtt = softplus(dtt_ref[:, tok] + dtbt_ref[...])
    da = dt * (-LOG2E * jnp.exp(alog_ref[...]))
    dat = dtt * (-LOG2E * jnp.exp(alogt_ref[...]))
    row = lax.broadcasted_iota(jnp.int32, (L, L), 0)
    col = lax.broadcasted_iota(jnp.int32, (L, L), 1)
    causal = row >= col
    tri = jnp.where(causal, 1.0, 0.0).astype(BF16)
    trit = jnp.where(row <= col, 1.0, 0.0).astype(BF16)

    def split3(v):
        a = v.astype(BF16)
        r1 = v - a.astype(F32)
        b = r1.astype(BF16)
        cc = (r1 - b.astype(F32)).astype(BF16)
        return a, b, cc

    a2 = sum(jnp.dot(tri, p, preferred_element_type=F32) for p in split3(da))
    a2t = sum(jnp.dot(p, trit, preferred_element_type=F32) for p in split3(dat))
    a2_last = a2[L - 1:L, :]

    expand = expand_ref[...]

    def expand_heads(v):
        hi, lo = _split_bf16(v)
        return jnp.dot(jnp.concatenate([hi, lo], axis=1), expand, preferred_element_type=F32)

    in_scale_e = expand_heads(dt * jnp.exp2(a2_last - a2))
    tail8 = jnp.concatenate([jnp.exp2(a2_last), dskip_ref[...],
                             jnp.zeros((SUBLANES - 2, n_heads), F32)], axis=0)
    out_e = expand_heads(jnp.concatenate([jnp.exp2(a2), tail8], axis=0))
    out_scale_e = out_e[0:L, :]
    chunk_decay_e = out_e[L:L + 1, :]
    dskip_e = out_e[L + 1:L + 2, :]

    first_head = lax.broadcasted_iota(jnp.int32, (L, LANES), 1) < P

    for g in range(G):
        x0 = g * GW
        xs = conv(x0, GW)
        bm = conv(d_inner + g * NS, NS).astype(BF16)
        cm = conv(d_inner + G * NS + g * NS, NS).astype(BF16)
        cb = lax.dot_general(cm, bm, (((1,), (1,)), ((), ())), preferred_element_type=F32)
        cb = jnp.where(causal, cb, 0.0)
        y_parts = []
        for pr in range(GW // LANES):
            wgts = []
            for q in range(LANES // P):
                h = g * R + pr * (LANES // P) + q
                seg = a2[:, h:h + 1] - a2t[h:h + 1, :]
                decay = jnp.exp2(jnp.minimum(seg, 0.0))
                wgts.append((cb * decay * dtt[h:h + 1, :]).astype(BF16))
            slab = xs[:, pr * LANES:(pr + 1) * LANES]
            rhs = jnp.concatenate([jnp.where(first_head, slab, 0.0).astype(BF16),
                                   jnp.where(first_head, 0.0, slab).astype(BF16)], axis=0)
            y_parts.append(jnp.dot(jnp.concatenate(wgts, axis=1), rhs, preferred_element_type=F32))
        y = jnp.concatenate(y_parts, axis=1) + xs * dskip_e[:, x0:x0 + GW]
        st = state_ref[:, x0:x0 + GW]
        y = y + (jnp.dot(cm, st.astype(BF16), preferred_element_type=F32)
                 * out_scale_e[:, x0:x0 + GW])
        y_ref[tok, x0:x0 + GW] = y
        xin = (xs * in_scale_e[:, x0:x0 + GW]).astype(BF16)
        st_new = lax.dot_general(bm, xin, (((0,), (0,)), ((), ())), preferred_element_type=F32)
        state_ref[:, x0:x0 + GW] = st * chunk_decay_e[:, x0:x0 + GW] + st_new

    yz = y_ref[tok, :] * _silu(z_ref[tok, :].astype(F32))
    ms = jnp.mean(yz * yz, axis=-1, keepdims=True)
    out_ref[tok, :] = (yz * lax.rsqrt(ms + NORM_EPS) * nw_ref[...]).astype(out_ref.dtype)


def _ssd(proj, dt, dtt, conv_w, conv_b, dt_bias, a_log, d_skip, norm_w, *, batch, seq, d_inner,
         xbc_block, z_block):
    n = batch * seq
    n_heads = dt.shape[1]
    L = SSD_CHUNK * math.gcd(seq // SSD_CHUNK, SSD_CHUNKS_PER_STEP)
    nc = seq // L
    conv_dim = conv_w.shape[1]
    expand = (np.arange(d_inner)[None, :] // SSD_HEAD_DIM == np.arange(n_heads)[:, None])
    expand = jnp.asarray(np.concatenate([expand, expand], axis=0), BF16)
    assert conv_dim % LANES == 0
    kern = functools.partial(_ssd_kernel, n_heads=n_heads, d_inner=d_inner)
    small = lambda shape: pl.BlockSpec(shape, lambda b, c: (0, 0))
    return pl.pallas_call(
        kern,
        grid=(batch, nc),
        in_specs=[
            pl.BlockSpec((L, conv_dim), lambda b, c: (b * nc + c, xbc_block)),
            pl.BlockSpec((L, d_inner), lambda b, c: (b * nc + c, z_block)),
            pl.BlockSpec((L, n_heads), lambda b, c: (b * nc + c, 0)),
            pl.BlockSpec((n_heads, L), lambda b, c: (0, b * nc + c)),
            small((SSD_CONV_WIDTH, conv_dim)),
            small((1, conv_dim)),
            small((1, n_heads)),
            small((n_heads, 1)),
            small((1, n_heads)),
            small((n_heads, 1)),
            small((1, n_heads)),
            small((1, d_inner)),
            small((2 * n_heads, d_inner)),
        ],
        out_specs=pl.BlockSpec((L, d_inner), lambda b, c: (b * nc + c, 0)),
        out_shape=jax.ShapeDtypeStruct((n, d_inner), BF16),
        scratch_shapes=[
            pltpu.VMEM((conv_dim // LANES, CONV_ROW_PITCH * (L + SUBLANES), LANES), F32),
            pltpu.VMEM((SSD_D_STATE, d_inner), F32),
            pltpu.VMEM((L, d_inner), F32),
        ],
        compiler_params=_params("arbitrary", "arbitrary"),
        name="ssd",
    )(proj, proj, dt, dtt, conv_w, conv_b.reshape(1, -1), dt_bias.reshape(1, -1),
      dt_bias.reshape(-1, 1), a_log.reshape(1, -1), a_log.reshape(-1, 1), d_skip.reshape(1, -1),
      norm_w.reshape(1, -1), expand)


def _attn_kernel(q_ref, k_ref, v_ref, o_ref, lse_ref, kp_ref, vp_ref, *, slopes, dilation, hops):
    nb = pl.program_id(2)
    BLK = ATTN_BLOCK
    E = ATTN_HEAD_DIM

    @pl.when(nb == 0)
    def _():
        kp_ref[...] = jnp.zeros(kp_ref.shape, kp_ref.dtype)
        vp_ref[...] = jnp.zeros(vp_ref.shape, vp_ref.dtype)

    scale = E ** -0.5
    nt = (((1,), (1,)), ((), ()))
    nblk = q_ref.shape[0] // BLK
    q = q_ref[...]
    kk = jnp.concatenate([kp_ref[...], k_ref[...]], axis=0)
    vv = jnp.concatenate([vp_ref[...], v_ref[...]], axis=0)
    units = [(j, h) for j in range(nblk) for h in range(len(slopes))]
    cols = lambda h: slice(h * E, (h + 1) * E)
    rows = lambda j: slice(j * BLK, (j + 1) * BLK)
    qi = lax.broadcasted_iota(jnp.int32, (BLK, BLK), 0)
    ki = lax.broadcasted_iota(jnp.int32, (BLK, BLK), 1)
    rel_cur = qi - ki
    rel_prev = rel_cur + BLK
    first = rel_prev <= jnp.where(nb > 0, hops, -1)
    later = rel_prev <= hops
    ok_cur = jnp.concatenate([rel_cur >= 0 for _ in units], axis=0)
    ok_prev = jnp.concatenate([first if j == 0 else later for j, _ in units], axis=0)
    dist_cur = (rel_cur * dilation).astype(F32)
    dist_prev = (rel_prev * dilation).astype(F32)
    bias_cur = jnp.concatenate([slopes[h] * dist_cur for _, h in units], axis=0)
    bias_prev = jnp.concatenate([slopes[h] * dist_prev for _, h in units], axis=0)
    s_cur = jnp.concatenate(
        [lax.dot_general(q[rows(j), cols(h)], kk[rows(j + 1), cols(h)], nt,
                         preferred_element_type=F32) for j, h in units], axis=0)
    s_prev = jnp.concatenate(
        [lax.dot_general(q[rows(j), cols(h)], kk[rows(j), cols(h)], nt,
                         preferred_element_type=F32) for j, h in units], axis=0)
    l_cur = jnp.where(ok_cur, s_cur * scale - bias_cur, -jnp.inf)
    l_prev = jnp.where(ok_prev, s_prev * scale - bias_prev, -jnp.inf)
    m = jnp.max(jnp.maximum(l_cur, l_prev), axis=-1, keepdims=True)
    p_cur = jnp.exp(l_cur - m)
    p_prev = jnp.exp(l_prev - m)
    den = jnp.sum(p_cur + p_prev, axis=-1, keepdims=True)
    p_cur = p_cur.astype(BF16)
    p_prev = p_prev.astype(BF16)
    inv = 1.0 / den
    lse = m + jnp.log(den)
    lane = lax.broadcasted_iota(jnp.int32, (BLK, LANES), 1)
    for j in range(nblk):
        lse_tile = jnp.zeros((BLK, LANES), F32)
        for h in range(len(slopes)):
            u = rows(units.index((j, h)))
            acc = (jnp.dot(p_cur[u], vv[rows(j + 1), cols(h)], preferred_element_type=F32)
                   + jnp.dot(p_prev[u], vv[rows(j), cols(h)], preferred_element_type=F32))
            o_ref[rows(j), cols(h)] = (acc * inv[u]).astype(o_ref.dtype)
            lse_tile = jnp.where(lane == h, lse[u], lse_tile)
        lse_ref[rows(j), :] = lse_tile
    kp_ref[...] = k_ref[rows(nblk - 1), :]
    vp_ref[...] = v_ref[rows(nblk - 1), :]


def _attn_group(qkv, gi, *, batch, seq):
    window, dilation = DILATION_PATTERNS[gi]
    hops = window // dilation
    n_heads_total = ATTN_HEADS_PER_GROUP * len(DILATION_PATTERNS)
    slopes = tuple(float(2.0 ** (-8.0 * (gi * ATTN_HEADS_PER_GROUP + h + 1) / n_heads_total))
                   for h in range(ATTN_HEADS_PER_GROUP))
    gw = ATTN_HEADS_PER_GROUP * ATTN_HEAD_DIM
    assert seq % (dilation * ATTN_BLOCK) == 0
    sub = seq // dilation
    rows = ATTN_BLOCK * math.gcd(sub // ATTN_BLOCK, ATTN_BLOCKS_PER_STEP)
    nb = sub // rows
    kern = functools.partial(_attn_kernel, slopes=slopes, dilation=dilation, hops=hops)

    def part(p):
        return pl.BlockSpec((rows, gw), lambda b, r, n: (b * nb + n, p * dilation + r))

    o, lse = pl.pallas_call(
        kern,
        grid=(batch, dilation, nb),
        in_specs=[part(0), part(1), part(2)],
        out_specs=[
            pl.BlockSpec((rows, gw), lambda b, r, n: (b * nb + n, r)),
            pl.BlockSpec((rows, LANES), lambda b, r, n: (b * nb + n, r)),
        ],
        out_shape=[
            jax.ShapeDtypeStruct((batch * sub, dilation * gw), BF16),
            jax.ShapeDtypeStruct((batch * sub, dilation * LANES), F32),
        ],
        scratch_shapes=[pltpu.VMEM((ATTN_BLOCK, gw), BF16), pltpu.VMEM((ATTN_BLOCK, gw), BF16)],
        compiler_params=_params("arbitrary", "arbitrary", "arbitrary"),
        name=f"attn_g{gi}",
    )(qkv, qkv, qkv)
    return o, lse


def _merge_kernel(yn_ref, o0_ref, o1_ref, o2_ref, l0_ref, l1_ref, l2_ref, gate_ref,
                  wssd_ref, wattn_ref, bg_ref, merged_ref, ostage_ref, lstage_ref, *, d_model):
    E = ATTN_HEAD_DIM
    tm = merged_ref.shape[0]
    n_slabs = o0_ref.shape[1] // LANES
    y_ssd = jnp.dot(yn_ref[...], wssd_ref[...], preferred_element_type=F32)

    def token_major(gi, o_ref, l_ref):
        dil = DILATION_PATTERNS[gi][1]
        if dil == 1:
            return o_ref[...].astype(F32), l_ref[...]
        for r in range(dil):
            rows = pl.ds(r, tm // dil, stride=dil)
            lstage_ref[gi - 1, rows, :] = l_ref[:, r * LANES:(r + 1) * LANES]
            for s in range(n_slabs):
                c0 = (r * n_slabs + s) * LANES
                ostage_ref[gi - 1, s, rows, :] = o_ref[:, c0:c0 + LANES].astype(F32)
        out = jnp.concatenate([ostage_ref[gi - 1, s] for s in range(n_slabs)], axis=1)
        return out, lstage_ref[gi - 1]

    (o0, l0), (o1, l1), (o2, l2) = (token_major(gi, o_ref, l_ref) for gi, (o_ref, l_ref) in
                                    enumerate(((o0_ref, l0_ref), (o1_ref, l1_ref), (o2_ref, l2_ref))))

    lm = jnp.maximum(jnp.maximum(l0, l1), l2)
    e0, e1, e2 = jnp.exp(l0 - lm), jnp.exp(l1 - lm), jnp.exp(l2 - lm)
    inv = 1.0 / (e0 + e1 + e2)
    parts = []
    for h in range(ATTN_HEADS_PER_GROUP):
        sl = slice(h * E, (h + 1) * E)
        parts.append((e0[:, h:h + 1] * inv[:, h:h + 1]) * o0[:, sl]
                     + (e1[:, h:h + 1] * inv[:, h:h + 1]) * o1[:, sl]
                     + (e2[:, h:h + 1] * inv[:, h:h + 1]) * o2[:, sl])
    o = jnp.concatenate(parts, axis=-1).astype(BF16)
    y_attn = jnp.dot(o, wattn_ref[...], preferred_element_type=F32)

    gates = _sigmoid(gate_ref[...].astype(F32) + bg_ref[...])
    merged_ref[...] = (gates[:, :d_model] * y_ssd + gates[:, d_model:] * y_attn).astype(BF16)


def _merge(yn, outs, lses, proj, w_ssd_out, w_attn_out, b_gate, *, tm, gate_block):
    n, d = yn.shape
    aw = outs[0].shape[1]
    dils = [dil for _, dil in DILATION_PATTERNS]
    assert dils[0] == 1 and all(tm % (dil * SUBLANES) == 0 for dil in dils)
    row = lambda w: pl.BlockSpec((tm, w), lambda i: (i, 0))
    grouped = lambda dil, w: pl.BlockSpec((tm // dil, dil * w), lambda i: (i, 0))
    const = lambda shape: pl.BlockSpec(shape, lambda i: (0, 0), pipeline_mode=pl.Buffered(1))
    return pl.pallas_call(
        functools.partial(_merge_kernel, d_model=d),
        grid=(n // tm,),
        in_specs=[
            row(d), *[grouped(dil, aw) for dil in dils], *[grouped(dil, LANES) for dil in dils],
            pl.BlockSpec((tm, 2 * d), lambda i: (i, gate_block)),
            const((d, d)), const((aw, d)), const((1, 2 * d)),
        ],
        out_specs=row(d),
        out_shape=jax.ShapeDtypeStruct((n, d), BF16),
        scratch_shapes=[pltpu.VMEM((len(dils) - 1, aw // LANES, tm, LANES), F32),
                        pltpu.VMEM((len(dils) - 1, tm, LANES), F32)],
        compiler_params=_params("arbitrary"),
        name="merge",
    )(yn, outs[0], outs[1], outs[2], lses[0], lses[1], lses[2], proj,
      w_ssd_out.astype(BF16), w_attn_out.astype(BF16), b_gate.reshape(1, -1))


def _route_kernel(merged_ref, x_ref, wout_ref, fnw_ref, wrh_ref, wrl_ref, rb_ref,
                  x1_ref, eid_ref, gcol_ref, rank_ref, cnt_ref, carry_ref):
    i = pl.program_id(0)
    tm = x_ref.shape[0]

    @pl.when(i == 0)
    def _():
        carry_ref[...] = jnp.zeros(carry_ref.shape, F32)

    x1 = x_ref[...] + jnp.dot(merged_ref[...], wout_ref[...], preferred_element_type=F32)
    x1_ref[...] = x1

    ms = jnp.mean(x1 * x1, axis=-1, keepdims=True)
    hn = x1 * lax.rsqrt(ms + NORM_EPS) * fnw_ref[...]

    hn_hi, hn_lo = _split_bf16(hn)
    nt = (((1,), (1,)), ((), ()))
    logits = (lax.dot_general(wrh_ref[...], hn_hi, nt, preferred_element_type=F32)
              + lax.dot_general(wrh_ref[...], hn_lo, nt, preferred_element_type=F32)
              + lax.dot_general(wrl_ref[...], hn_hi, nt, preferred_element_type=F32)
              + rb_ref[...])

    grow = lax.broadcasted_iota(jnp.int32, (SUBLANES, tm), 0)
    gl = jnp.where(grow < N_EXPERT_GROUPS, logits[0:SUBLANES, :], -jnp.inf)
    gmax = jnp.max(gl, axis=0, keepdims=True)
    gidx = jnp.min(jnp.where(gl == gmax, grow, N_EXPERT_GROUPS), axis=0, keepdims=True)
    group_gate = 1.0 / jnp.sum(jnp.exp(gl - gmax), axis=0, keepdims=True)

    in_group = jnp.zeros((EXPERTS_PER_GROUP, tm), F32)
    for g in range(N_EXPERT_GROUPS):
        r0 = ROUTER_EXPERT_ROW0 + g * EXPERTS_PER_GROUP
        in_group = jnp.where(gidx == g, logits[r0:r0 + EXPERTS_PER_GROUP, :], in_group)
    erow = lax.broadcasted_iota(jnp.int32, in_group.shape, 0)
    v1 = jnp.max(in_group, axis=0, keepdims=True)
    i1 = jnp.min(jnp.where(in_group == v1, erow, EXPERTS_PER_GROUP), axis=0, keepdims=True)
    rest = jnp.where(erow == i1, -jnp.inf, in_group)
    v2 = jnp.max(rest, axis=0, keepdims=True)
    i2 = jnp.min(jnp.where(rest == v2, erow, EXPERTS_PER_GROUP), axis=0, keepdims=True)
    t = jnp.exp(v2 - v1)
    g1 = group_gate / (1.0 + t)
    g2 = group_gate * t / (1.0 + t)
    eid1 = gidx * EXPERTS_PER_GROUP + i1
    eid2 = gidx * EXPERTS_PER_GROUP + i2
    slot = lax.broadcasted_iota(jnp.int32, (SUBLANES, tm), 0)
    eid_ref[...] = jnp.where(slot == 0, eid1, jnp.where(slot == 1, eid2, 0))

    grow8 = lax.broadcasted_iota(jnp.int32, (LANES, tm), 0)
    gt = jnp.where(grow8 == 0, g1, jnp.where(grow8 == 1, g2, 0.0))
    gcol_ref[...] = gt.T

    xrow = lax.broadcasted_iota(jnp.int32, (N_EXPERTS, tm), 0)
    oh1 = xrow == eid1
    oh2 = xrow == eid2
    oh = jnp.where(oh1 | oh2, 1.0, 0.0)
    ti = lax.broadcasted_iota(jnp.int32, (tm, tm), 0)
    tj = lax.broadcasted_iota(jnp.int32, (tm, tm), 1)
    before = jnp.where(ti < tj, 1.0, 0.0).astype(BF16)
    prior = jnp.dot(oh.astype(BF16), before, preferred_element_type=F32) + carry_ref[:, 0:1]
    r1 = jnp.sum(jnp.where(oh1, prior, 0.0), axis=0, keepdims=True)
    r2 = jnp.sum(jnp.where(oh2, prior, 0.0), axis=0, keepdims=True)
    rank_ref[...] = jnp.where(slot == 0, r1, jnp.where(slot == 1, r2, 0.0)).astype(jnp.int32)
    carry_ref[...] = carry_ref[...] + jnp.sum(oh, axis=1, keepdims=True)
    cnt_ref[...] = carry_ref[...].astype(jnp.int32)


def _route(merged, x2, w_out, ffn_norm_w, wr, rb, *, tm):
    n, d = x2.shape
    wrt = jnp.zeros((ROUTER_ROWS, d), F32)
    wrt = wrt.at[0:N_EXPERT_GROUPS].set(wr[0].T).at[ROUTER_EXPERT_ROW0:].set(wr[1].T)
    rbc = jnp.zeros((ROUTER_ROWS, 1), F32)
    rbc = rbc.at[0:N_EXPERT_GROUPS, 0].set(rb[0]).at[ROUTER_EXPERT_ROW0:, 0].set(rb[1])
    wrh, wrl = _split_bf16(wrt)
    row = lambda w: pl.BlockSpec((tm, w), lambda i: (i, 0))
    const = lambda shape: pl.BlockSpec(shape, lambda i: (0, 0), pipeline_mode=pl.Buffered(1))
    return pl.pallas_call(
        _route_kernel,
        grid=(n // tm,),
        in_specs=[
            row(d), row(d),
            const((d, d)), const((1, d)),
            const((ROUTER_ROWS, d)), const((ROUTER_ROWS, d)), const((ROUTER_ROWS, 1)),
        ],
        out_specs=[
            row(d),
            pl.BlockSpec((SUBLANES, tm), lambda i: (0, i)),
            row(LANES),
            pl.BlockSpec((SUBLANES, tm), lambda i: (0, i)),
            pl.BlockSpec((N_EXPERTS, LANES), lambda i: (0, 0)),
        ],
        out_shape=[
            jax.ShapeDtypeStruct((n, d), F32),
            jax.ShapeDtypeStruct((SUBLANES, n), jnp.int32),
            jax.ShapeDtypeStruct((n, LANES), F32),
            jax.ShapeDtypeStruct((SUBLANES, n), jnp.int32),
            jax.ShapeDtypeStruct((N_EXPERTS, LANES), jnp.int32),
        ],
        scratch_shapes=[pltpu.VMEM((N_EXPERTS, LANES), F32)],
        compiler_params=_params("arbitrary"),
        name="route",
    )(merged, x2, w_out.astype(BF16), ffn_norm_w.reshape(1, -1), wrh, wrl, rbc)


def _scatter_kernel(pad_start_ref, pad_len_ref, n_active_ref, pos_ref, hn_ref, xs_ref, zero_ref,
                    sem, zsem, *, tme, n_tiles):
    tm = hn_ref.shape[0]
    zrows = zero_ref.shape[0]

    @pl.when(pl.program_id(0) == 0)
    def _():
        zero_ref[...] = jnp.zeros(zero_ref.shape, zero_ref.dtype)

        def fills(act):
            def per_expert(e, carry):
                off = pad_start_ref[e]
                left = pad_len_ref[e]
                head = left & (SUBLANES - 1)
                for r in range(SUBLANES - 1):
                    @pl.when(r < head)
                    def _(r=r):
                        act(pltpu.make_async_copy(zero_ref.at[pl.ds(0, 1), :],
                                                  xs_ref.at[pl.ds(off + r, 1), :], zsem))

                off = off + head
                bit = zrows
                while bit >= SUBLANES:
                    take = left & bit

                    @pl.when(take != 0)
                    def _(off=off, bit=bit):
                        act(pltpu.make_async_copy(
                            zero_ref.at[pl.ds(0, bit), :],
                            xs_ref.at[pl.ds(pl.multiple_of(off, SUBLANES), bit), :], zsem))

                    off = off + take
                    bit //= 2
                return carry

            def per_tile(t, carry):
                @pl.when(t >= n_active_ref[0])
                def _():
                    for part in range(tme // zrows):
                        row0 = pl.multiple_of(t * tme + part * zrows, zrows)
                        act(pltpu.make_async_copy(zero_ref, xs_ref.at[pl.ds(row0, zrows), :], zsem))

                return carry

            lax.fori_loop(0, N_EXPERTS, per_expert, 0)
            lax.fori_loop(0, n_tiles, per_tile, 0)

        fills(lambda cp: cp.start())
        fills(lambda cp: cp.wait())

    def start(tb, carry):
        for u in range(ROW_DMA_UNROLL):
            t = tb * ROW_DMA_UNROLL + u
            for k in range(EXPERT_TOP_K):
                dst = pos_ref[0, 0, k * tm + t]
                pltpu.make_async_copy(hn_ref.at[pl.ds(t, 1), :], xs_ref.at[pl.ds(dst, 1), :],
                                      sem).start(priority=(u + k) % 2)
        return carry

    lax.fori_loop(0, tm // ROW_DMA_UNROLL, start, 0)
    for k in range(EXPERT_TOP_K):
        pltpu.make_async_copy(hn_ref, xs_ref.at[pl.ds(0, tm), :], sem).wait()


def _scatter(hn, pos_tiles, pad_start, pad_len, n_active, *, tm, tme, n_tiles):
    n, d = hn.shape
    assert tme % 2 == 0 and (tme // 2) & (tme // 2 - 1) == 0
    grid_spec = pltpu.PrefetchScalarGridSpec(
        num_scalar_prefetch=3,
        grid=(n // tm,),
        in_specs=[
            pl.BlockSpec((1, 1, EXPERT_TOP_K * tm), lambda i, *_: (i, 0, 0),
                         memory_space=pltpu.SMEM),
            pl.BlockSpec((tm, d), lambda i, *_: (i, 0)),
        ],
        out_specs=pl.BlockSpec(memory_space=pl.ANY),
        scratch_shapes=[pltpu.VMEM((tme // 2, d), hn.dtype), pltpu.SemaphoreType.DMA(()),
                        pltpu.SemaphoreType.DMA(())],
    )
    return pl.pallas_call(
        functools.partial(_scatter_kernel, tme=tme, n_tiles=n_tiles),
        grid_spec=grid_spec,
        out_shape=jax.ShapeDtypeStruct((n_tiles * tme, d), hn.dtype),
        compiler_params=_params("arbitrary"),
        name="scatter",
    )(pad_start, pad_len, n_active, pos_tiles, hn)


W_CHUNK_ROWS = 256
W_RING = 8


def _experts_kernel(te_ref, slot_ref, nxt_ref, lo_ref, hi_ref, na_ref, xs_ref, fnw_ref, wg_hbm,
                    wu_hbm, wd_hbm, ys_ref, wres_ref, stage_ref, sem, *, n_gu, n_r, n_h):
    i = pl.program_id(0)
    n_chunks = 2 * n_gu + n_h * n_r
    ring, cr, f = stage_ref.shape
    d = n_gu * cr

    def stage_copy(src, c):
        return pltpu.make_async_copy(src, stage_ref.at[c % ring], sem.at[c % ring])

    def start_chunk(e, c):
        @pl.when(c < n_gu)
        def _():
            stage_copy(wg_hbm.at[e, pl.ds(pl.multiple_of(c * cr, cr), cr), :], c).start()

        @pl.when((c >= n_gu) & (c < 2 * n_gu))
        def _():
            stage_copy(wu_hbm.at[e, pl.ds(pl.multiple_of((c - n_gu) * cr, cr), cr), :], c).start()

        @pl.when(c >= 2 * n_gu)
        def _():
            q = c - 2 * n_gu
            h = q // n_r
            r = q - h * n_r
            stage_copy(wd_hbm.at[e, pl.ds(pl.multiple_of(r * cr, cr), cr),
                                 pl.ds(pl.multiple_of(h * f, f), f)], c).start()

    def start_first(e):
        def body(c, carry):
            start_chunk(e, c)
            return carry

        lax.fori_loop(0, ring, body, 0)

    def convert(e, lo, hi, dst_slot):
        def body(c, carry):
            stage_copy(wg_hbm.at[0, pl.ds(0, cr), :], c).wait()
            wres_ref[dst_slot, pl.ds(pl.multiple_of(c * cr, cr), cr), :] = (
                stage_ref[c % ring].astype(BF16))

            @pl.when(c + ring < n_chunks)
            def _():
                start_chunk(e, c + ring)

            return carry

        lax.fori_loop(lo, hi, body, 0)

    @pl.when(i == 0)
    def _():
        start_first(te_ref[0])
        convert(te_ref[0], 0, n_chunks, slot_ref[0])

    @pl.when(i < na_ref[0])
    def _():
        slot = slot_ref[i]
        nxt = nxt_ref[i]

        @pl.when((nxt >= 0) & (lo_ref[i] == 0))
        def _():
            start_first(nxt)

        xr = xs_ref[...]
        ms = jnp.mean(xr * xr, axis=-1, keepdims=True)
        xb = (xr * lax.rsqrt(ms + NORM_EPS) * fnw_ref[...]).astype(BF16)
        hg = jnp.dot(xb, wres_ref[slot, 0:d, :], preferred_element_type=F32)
        hu = jnp.dot(xb, wres_ref[slot, d:2 * d, :], preferred_element_type=F32)
        hmid = (_silu(hg) * hu).astype(BF16)
        for h in range(n_h):
            r0 = 2 * d + h * n_r * cr
            ys_ref[:, h * f:(h + 1) * f] = jnp.dot(hmid, wres_ref[slot, r0:r0 + n_r * cr, :],
                                                    preferred_element_type=F32)

        @pl.when(nxt >= 0)
        def _():
            convert(nxt, lo_ref[i], hi_ref[i], 1 - slot)

    @pl.when(i >= na_ref[0])
    def _():
        ys_ref[...] = jnp.zeros(ys_ref.shape, F32)


def _experts(xs, norm_w, tile_expert, n_active, cnt, starts, padded, w_g, w_u, w_d, *, tm):
    p, d = xs.shape
    ne, _, f = w_g.shape
    n_tiles = p // tm
    cr = W_CHUNK_ROWS
    assert d % cr == 0 and f % cr == 0 and d % f == 0
    n_gu, n_r, n_h = d // cr, f // cr, d // f
    n_chunks = 2 * n_gu + n_h * n_r

    ids = jnp.arange(ne, dtype=jnp.int32)
    present = cnt > 0
    later = jnp.where(present[None, :] & (ids[None, :] > ids[:, None]), ids[None, :], ne)
    nxt_e = jnp.min(later, axis=1)
    nxt_e = jnp.where(nxt_e >= ne, -1, nxt_e)
    run_e = jnp.cumsum(present.astype(jnp.int32)) - 1
    onehot = (tile_expert[:, None] == ids[None, :]).astype(jnp.int32)
    pick = lambda v: jnp.sum(onehot * v[None, :].astype(jnp.int32), axis=1)
    tiles = jnp.arange(n_tiles, dtype=jnp.int32)
    active = tiles < n_active[0]
    j = tiles - pick(starts) // tm
    k = jnp.maximum(pick(padded) // tm, 1)
    nxt_t = jnp.where(active, pick(nxt_e), -1).astype(jnp.int32)
    lo_t = jnp.where(active, (n_chunks * j) // k, 0).astype(jnp.int32)
    hi_t = jnp.where(active, (n_chunks * (j + 1)) // k, 0).astype(jnp.int32)
    slot_t = (pick(run_e) % 2).astype(jnp.int32)

    def rows(i, te, sl, nx, lo, hi, na):
        return (jnp.minimum(i, na[0] - 1), 0)

    grid_spec = pltpu.PrefetchScalarGridSpec(
        num_scalar_prefetch=6,
        grid=(n_tiles,),
        in_specs=[
            pl.BlockSpec((tm, d), rows),
            pl.BlockSpec((1, d), lambda i, *_: (0, 0)),
            pl.BlockSpec(memory_space=pl.ANY),
            pl.BlockSpec(memory_space=pl.ANY),
            pl.BlockSpec(memory_space=pl.ANY),
        ],
        out_specs=pl.BlockSpec((tm, d), lambda i, *_: (i, 0)),
        scratch_shapes=[
            pltpu.VMEM((2, n_chunks * cr, f), BF16),
            pltpu.VMEM((W_RING, cr, f), F32),
            pltpu.SemaphoreType.DMA((W_RING,)),
        ],
    )
    return pl.pallas_call(
        functools.partial(_experts_kernel, n_gu=n_gu, n_r=n_r, n_h=n_h),
        grid_spec=grid_spec,
        out_shape=jax.ShapeDtypeStruct((p, d), F32),
        compiler_params=_params("arbitrary"),
        name="experts",
    )(tile_expert, slot_t, nxt_t, lo_t, hi_t, n_active, xs, norm_w.reshape(1, d), w_g, w_u, w_d)


def _combine_kernel(pos_ref, pos_next_ref, ys_ref, x1_ref, gcol_ref, nw_ref, out_ref, buf_ref, sem,
                    *, final):
    i = pl.program_id(0)
    n_steps = pl.num_programs(0)
    tm = x1_ref.shape[0]

    def issue(p_ref, slot):
        def start(tb, carry):
            for u in range(ROW_DMA_UNROLL):
                t = tb * ROW_DMA_UNROLL + u
                for k in range(EXPERT_TOP_K):
                    src = p_ref[0, 0, k * tm + t]
                    pltpu.make_async_copy(ys_ref.at[pl.ds(src, 1), :],
                                          buf_ref.at[slot, k, pl.ds(t, 1), :],
                                          sem.at[slot]).start(priority=(u + k) % 2)
            return carry

        lax.fori_loop(0, tm // ROW_DMA_UNROLL, start, 0)

    slot = i % 2

    @pl.when(i == 0)
    def _():
        issue(pos_ref, 0)

    @pl.when(i + 1 < n_steps)
    def _():
        issue(pos_next_ref, 1 - slot)

    for k in range(EXPERT_TOP_K):
        pltpu.make_async_copy(ys_ref.at[pl.ds(0, tm), :], buf_ref.at[slot, k], sem.at[slot]).wait()
    g = gcol_ref[...]
    xo = x1_ref[...] + g[:, 0:1] * buf_ref[slot, 0] + g[:, 1:2] * buf_ref[slot, 1]
    if final:
        ms = jnp.mean(xo * xo, axis=-1, keepdims=True)
        xo = xo * lax.rsqrt(ms + NORM_EPS) * nw_ref[...]
    out_ref[...] = xo


def _combine(ys, pos_tiles, x1, gcol, norm_w, *, tm, final):
    n, d = x1.shape
    last = n // tm - 1
    return pl.pallas_call(
        functools.partial(_combine_kernel, final=final),
        grid=(n // tm,),
        in_specs=[
            pl.BlockSpec((1, 1, EXPERT_TOP_K * tm), lambda i: (i, 0, 0), memory_space=pltpu.SMEM),
            pl.BlockSpec((1, 1, EXPERT_TOP_K * tm), lambda i: (jnp.minimum(i + 1, last), 0, 0),
                         memory_space=pltpu.SMEM),
            pl.BlockSpec(memory_space=pl.ANY),
            pl.BlockSpec((tm, d), lambda i: (i, 0)),
            pl.BlockSpec((tm, LANES), lambda i: (i, 0)),
            pl.BlockSpec((1, d), lambda i: (0, 0)),
        ],
        out_specs=pl.BlockSpec((tm, d), lambda i: (i, 0)),
        out_shape=jax.ShapeDtypeStruct((n, d), F32),
        scratch_shapes=[pltpu.VMEM((2, EXPERT_TOP_K, tm, d), F32), pltpu.SemaphoreType.DMA((2,))],
        compiler_params=_params("arbitrary"),
        name="combine",
    )(pos_tiles, pos_tiles, ys, x1, gcol, norm_w.reshape(1, d))


def _tiles(n):
    return dict(proj_tm=math.gcd(n, 1024), mix_tm=math.gcd(n, 512), moe_tm=256,
                route_tm=math.gcd(n, 512))


def _layer(x2, batch, seq, layer, attn_norm_w, w_in_stack, b_gate, conv_w, conv_b, dt_bias, a_log,
           d_skip, ssd_norm_w, w_ssd_out, w_attn_out, w_out, ffn_norm_w, w_gr, b_gr, w_er, b_er,
           w_g, w_u, w_d):
    n, d = x2.shape
    n_heads = dt_bias.shape[0]
    d_inner = ssd_norm_w.shape[0]
    conv_dim = conv_w.shape[1]
    aw_total = ATTN_HEADS_PER_GROUP * len(DILATION_PATTERNS) * ATTN_HEAD_DIM
    gw = ATTN_HEADS_PER_GROUP * ATTN_HEAD_DIM
    tiles = _tiles(n)

    c_z, c_xbc, c_dt = d_inner, d_inner + conv_dim, d_inner + conv_dim + n_heads
    c_gate = c_dt + QKV_PARTS * aw_total
    segments = [(c_gate, 2 * d), (c_z, conv_dim), (0, d_inner)]
    segments += [(c_dt + p * aw_total + gi * gw, gw)
                 for gi in range(len(DILATION_PATTERNS)) for p in range(QKV_PARTS)]
    assert all(width % gw == 0 for _, width in segments)
    starts = tuple(start + b * gw for start, width in segments for b in range(width // gw))
    w_main, w_dt = _regroup_weight(w_in_stack, layer, starts, gw, c_xbc, n_heads)
    off_gate, off_xbc, off_z = 0, 2 * d, 2 * d + conv_dim
    plain_cols = off_z + d_inner
    assert off_xbc % conv_dim == 0 and off_z % d_inner == 0 and plain_cols % gw == 0

    proj, qkv0, qkv1, qkv2, dt, dtt = _in_proj(x2, attn_norm_w, w_main, w_dt, tm=tiles["proj_tm"],
                                               tn=gw, tp=4 * gw, plain_cols=plain_cols, nh=n_heads)

    yn = _ssd(proj, dt, dtt, conv_w, conv_b, dt_bias, a_log, d_skip, ssd_norm_w, batch=batch,
              seq=seq, d_inner=d_inner, xbc_block=off_xbc // conv_dim, z_block=off_z // d_inner)

    outs, lses = [], []
    for gi, qkv in enumerate((qkv0, qkv1, qkv2)):
        o_g, lse_g = _attn_group(qkv, gi, batch=batch, seq=seq)
        outs.append(o_g)
        lses.append(lse_g)

    merged = _merge(yn, outs, lses, proj, w_ssd_out, w_attn_out, b_gate, tm=tiles["mix_tm"],
                    gate_block=off_gate // (2 * d))
    x1, eid, gcol, rank, counts = _route(merged, x2, w_out, ffn_norm_w, (w_gr, w_er),
                                         (b_gr, b_er), tm=tiles["mix_tm"])

    tme = tiles["moe_tm"]
    cnt = counts[:, 0]
    padded = ((cnt + tme - 1) // tme) * tme
    ends = jnp.cumsum(padded)
    starts = ends - padded
    experts = jnp.arange(N_EXPERTS, dtype=jnp.int32)[:, None, None]
    pos = rank[:EXPERT_TOP_K] + jnp.sum(
        jnp.where(eid[None, :EXPERT_TOP_K] == experts, starts[:, None, None], 0), axis=0)
    n_tiles = EXPERT_TOP_K * n // tme + N_EXPERTS
    tile_start = jnp.arange(n_tiles, dtype=jnp.int32) * tme
    tile_expert = jnp.minimum(jnp.sum(ends[None, :] <= tile_start[:, None], axis=1),
                              N_EXPERTS - 1).astype(jnp.int32)
    n_active = (ends[-1:] // tme).astype(jnp.int32)

    rtm = tiles["route_tm"]
    pos_tiles = pos.reshape(EXPERT_TOP_K, n // rtm, rtm).transpose(1, 0, 2).reshape(
        n // rtm, 1, EXPERT_TOP_K * rtm)
    xs = _scatter(x1, pos_tiles, (starts + cnt).astype(jnp.int32), (padded - cnt).astype(jnp.int32),
                  n_active, tm=rtm, tme=tme, n_tiles=n_tiles)
    ys = _experts(xs, ffn_norm_w, tile_expert, n_active, cnt, starts, padded, w_g, w_u, w_d, tm=tme)
    return ys, pos_tiles, x1, gcol, rtm


def kernel(x, attn_norm_w, w_in, b_gate, conv_w, conv_b, dt_bias, a_log, d_skip, ssd_norm_w,
           w_ssd_out, w_attn_out, w_out, ffn_norm_w, w_group_router, b_group_router,
           w_expert_router, b_expert_router, w_exp_gate, w_exp_up, w_exp_down, final_norm_w):
    batch, seq, d = x.shape
    depth = w_in.shape[0]
    x2 = x.reshape(batch * seq, d)
    for layer in range(depth):
        ys, pos_tiles, x1, gcol, rtm = _layer(
            x2, batch, seq, layer, attn_norm_w[layer], w_in, b_gate[layer], conv_w[layer],
            conv_b[layer], dt_bias[layer], a_log[layer], d_skip[layer], ssd_norm_w[layer],
            w_ssd_out[layer], w_attn_out[layer], w_out[layer], ffn_norm_w[layer],
            w_group_router[layer], b_group_router[layer], w_expert_router[layer],
            b_expert_router[layer], w_exp_gate[layer], w_exp_up[layer], w_exp_down[layer])
        x2 = _combine(ys, pos_tiles, x1, gcol, final_norm_w, tm=rtm, final=layer == depth - 1)
    return x2.reshape(batch, seq, d)
```

```python
import functools
import math

import jax
import jax.numpy as jnp
import numpy as np
from jax import lax
from jax.experimental import pallas as pl
from jax.experimental.pallas import tpu as pltpu

F32 = jnp.float32
BF16 = jnp.bfloat16

NORM_EPS = 1e-6
SSD_HEAD_DIM = 64
SSD_N_GROUPS = 8
SSD_D_STATE = 128
SSD_CONV_WIDTH = 4
SSD_CHUNK = 128
SSD_CHUNKS_PER_STEP = 4
ATTN_HEAD_DIM = 128
DILATION_PATTERNS = ((128, 1), (512, 4), (2048, 16))
ATTN_HEADS_PER_GROUP = 4
ATTN_BLOCK = 128
ATTN_BLOCKS_PER_STEP = 4
N_EXPERT_GROUPS = 4
EXPERTS_PER_GROUP = 8
N_EXPERTS = N_EXPERT_GROUPS * EXPERTS_PER_GROUP
EXPERT_TOP_K = 2

LANES = 128
SUBLANES = 8
VMEM_LIMIT_BYTES = 56 * 1024 * 1024
ROW_DMA_UNROLL = 8

ROUTER_EXPERT_ROW0 = SUBLANES
ROUTER_ROWS = ROUTER_EXPERT_ROW0 + N_EXPERTS


def _params(*semantics):
    return pltpu.CompilerParams(dimension_semantics=semantics, vmem_limit_bytes=VMEM_LIMIT_BYTES)


def _split_bf16(v):
    hi = v.astype(BF16)
    lo = (v - hi.astype(F32)).astype(BF16)
    return hi, lo


def _sigmoid(v):
    return 0.5 + 0.5 * jnp.tanh(0.5 * v)


def _silu(v):
    h = 0.5 * v
    return h + h * jnp.tanh(h)


def _regroup_weight_kernel(starts_ref, wt_hbm, out_ref, narrow_ref, buf_ref, nbuf_ref, sem, nsem,
                           *, layer, narrow_start, nh):
    i = pl.program_id(0)
    n_steps = pl.num_programs(0)
    tn = buf_ref.shape[1]

    def fetch(step, slot):
        rows = pl.ds(pl.multiple_of(starts_ref[step], SUBLANES), tn)
        return pltpu.make_async_copy(wt_hbm.at[layer, rows, :], buf_ref.at[slot], sem.at[slot])

    narrow = pltpu.make_async_copy(wt_hbm.at[layer, pl.ds(narrow_start, LANES), :], nbuf_ref, nsem)

    @pl.when(i == 0)
    def _():
        fetch(0, 0).start()
        narrow.start()

    @pl.when(i + 1 < n_steps)
    def _():
        fetch(i + 1, (i + 1) % 2).start()

    fetch(i, i % 2).wait()
    out_ref[...] = buf_ref[i % 2].T.astype(out_ref.dtype)

    @pl.when(i == n_steps - 1)
    def _():
        narrow.wait()
        lane = lax.broadcasted_iota(jnp.int32, narrow_ref.shape, 1)
        narrow_ref[...] = jnp.where(lane < nh, nbuf_ref[...].T, 0.0).astype(narrow_ref.dtype)


def _regroup_weight(w_stack, layer, starts, tn, narrow_start, nh):
    _, k, cols = w_stack.shape
    assert all(s % SUBLANES == 0 and s + tn <= cols for s in starts)
    assert narrow_start % SUBLANES == 0 and narrow_start + LANES <= cols and nh <= LANES
    wt = jnp.swapaxes(w_stack, 1, 2)
    grid_spec = pltpu.PrefetchScalarGridSpec(
        num_scalar_prefetch=1,
        grid=(len(starts),),
        in_specs=[pl.BlockSpec(memory_space=pl.ANY)],
        out_specs=[pl.BlockSpec((k, tn), lambda i, *_: (0, i)),
                   pl.BlockSpec((k, LANES), lambda i, *_: (0, 0))],
        scratch_shapes=[pltpu.VMEM((2, tn, k), F32), pltpu.VMEM((LANES, k), F32),
                        pltpu.SemaphoreType.DMA((2,)), pltpu.SemaphoreType.DMA(())],
    )
    return pl.pallas_call(
        functools.partial(_regroup_weight_kernel, layer=layer, narrow_start=narrow_start, nh=nh),
        grid_spec=grid_spec,
        out_shape=[jax.ShapeDtypeStruct((k, len(starts) * tn), BF16),
                   jax.ShapeDtypeStruct((k, LANES), BF16)],
        compiler_params=_params("arbitrary"),
        name="regroup_weight",
    )(jnp.asarray(starts, jnp.int32), wt)


QKV_PARTS = 3


def _in_proj_kernel(x_hbm, nw_ref, wp_ref, w_ref, wdt_ref, proj_ref, a0_ref, a1_ref, a2_ref,
                    dt_ref, dtt_ref, h_ref, stage_ref, x_ref, xsem, *, n_plain):
    i = pl.program_id(0)
    j = pl.program_id(1)
    tm = proj_ref.shape[0]
    tn = w_ref.shape[1]

    def fetch_x(tile):
        return pltpu.make_async_copy(x_hbm.at[pl.ds(pl.multiple_of(tile * tm, tm), tm), :], x_ref, xsem)

    @pl.when((i == 0) & (j == 0))
    def _():
        fetch_x(0).start()

    @pl.when(j == 0)
    def _():
        fetch_x(i).wait()
        xf = x_ref[...]
        ms = jnp.mean(xf * xf, axis=-1, keepdims=True)
        h = (xf * lax.rsqrt(ms + NORM_EPS) * nw_ref[...]).astype(BF16)
        h_ref[...] = h
        nh = dt_ref.shape[1]
        dt_wide = jnp.dot(h, wdt_ref[...], preferred_element_type=F32)
        dt_ref[...] = dt_wide[:, :nh]
        dtt_ref[...] = dt_wide.T[:nh, :]

        @pl.when(i + 1 < pl.num_programs(0))
        def _():
            fetch_x(i + 1).start()

    @pl.when(j < n_plain)
    def _():
        proj_ref[...] = jnp.dot(h_ref[...], wp_ref[...], preferred_element_type=F32).astype(BF16)

    for gi, a_ref in enumerate((a0_ref, a1_ref, a2_ref)):
        dil = DILATION_PATTERNS[gi][1]
        j0 = n_plain + QKV_PARTS * gi

        @pl.when((j >= j0) & (j < j0 + QKV_PARTS))
        def _(a_ref=a_ref, dil=dil):
            res = jnp.dot(h_ref[...], w_ref[...], preferred_element_type=F32)
            if dil == 1:
                a_ref[...] = res.astype(BF16)
            else:
                for s in range(tn // LANES):
                    stage_ref[s] = res[:, s * LANES:(s + 1) * LANES]
                for r in range(dil):
                    for s in range(tn // LANES):
                        c0 = r * tn + s * LANES
                        a_ref[:, c0:c0 + LANES] = stage_ref[
                            s, pl.ds(r, tm // dil, stride=dil), :].astype(BF16)


def _in_proj(x2, norm_w, w_main, w_dt, *, tm, tn, tp, plain_cols, nh):
    n, d = x2.shape
    assert plain_cols % tp == 0 and plain_cols % tn == 0
    n_plain = plain_cols // tp
    qkv_block0 = plain_cols // tn
    n_blocks = n_plain + QKV_PARTS * len(DILATION_PATTERNS)
    assert w_main.shape[1] == plain_cols + QKV_PARTS * len(DILATION_PATTERNS) * tn

    def a_spec(gi):
        dil = DILATION_PATTERNS[gi][1]
        j0 = n_plain + QKV_PARTS * gi
        return pl.BlockSpec((tm // dil, dil * tn),
                            lambda i, j: (i, jnp.clip(j - j0, 0, QKV_PARTS - 1)))

    def a_shape(gi):
        dil = DILATION_PATTERNS[gi][1]
        return jax.ShapeDtypeStruct((n // dil, dil * QKV_PARTS * tn), BF16)

    return pl.pallas_call(
        functools.partial(_in_proj_kernel, n_plain=n_plain),
        grid=(n // tm, n_blocks),
        in_specs=[
            pl.BlockSpec(memory_space=pl.ANY),
            pl.BlockSpec((1, d), lambda i, j: (0, 0)),
            pl.BlockSpec((d, tp), lambda i, j: (0, jnp.minimum(j, n_plain - 1))),
            pl.BlockSpec((d, tn), lambda i, j: (0, qkv_block0 + jnp.maximum(j - n_plain, 0))),
            pl.BlockSpec((d, LANES), lambda i, j: (0, 0)),
        ],
        out_specs=[
            pl.BlockSpec((tm, tp), lambda i, j: (i, jnp.minimum(j, n_plain - 1))),
            a_spec(0), a_spec(1), a_spec(2),
            pl.BlockSpec((tm, nh), lambda i, j: (i, 0)),
            pl.BlockSpec((nh, tm), lambda i, j: (0, i)),
        ],
        out_shape=[
            jax.ShapeDtypeStruct((n, plain_cols), BF16),
            a_shape(0), a_shape(1), a_shape(2),
            jax.ShapeDtypeStruct((n, nh), F32),
            jax.ShapeDtypeStruct((nh, n), F32),
        ],
        scratch_shapes=[pltpu.VMEM((tm, d), BF16), pltpu.VMEM((tn // LANES, tm, LANES), F32),
                        pltpu.VMEM((tm, d), x2.dtype), pltpu.SemaphoreType.DMA(())],
        compiler_params=_params("arbitrary", "arbitrary"),
        name="in_proj",
    )(x2, norm_w.reshape(1, d), w_main, w_main, w_dt)


LOG2E = math.log2(math.e)
CONV_ROW_PITCH = 2


def _ssd_kernel(xbc_ref, z_ref, dt_ref, dtt_ref, cw_ref, cb_ref, dtb_ref, dtbt_ref, alog_ref,
                alogt_ref, dskip_ref, nw_ref, expand_ref, out_ref,
                xbuf_ref, state_ref, y_ref, *, n_heads, d_inner):
    L = SSD_CHUNK
    P = SSD_HEAD_DIM
    NS = SSD_D_STATE
    G = SSD_N_GROUPS
    R = n_heads // G
    GW = R * P
    W = SSD_CONV_WIDTH
    RP = CONV_ROW_PITCH
    n_slabs = xbuf_ref.shape[0]
    c = pl.program_id(1)

    def rows(first, count):
        return pl.ds(first * RP, count, stride=RP)

    @pl.when(c == 0)
    def _():
        state_ref[...] = jnp.zeros(state_ref.shape, F32)
        for s in range(n_slabs):
            xbuf_ref[s, rows(0, SUBLANES), :] = jnp.zeros((SUBLANES, LANES), F32)

    n_rows = xbc_ref.shape[0]

    @pl.when(c > 0)
    def _():
        for s in range(n_slabs):
            xbuf_ref[s, rows(0, SUBLANES), :] = xbuf_ref[s, rows(n_rows, SUBLANES), :]

    for s in range(n_slabs):
        xbuf_ref[s, rows(SUBLANES, n_rows), :] = xbc_ref[:, s * LANES:(s + 1) * LANES].astype(F32)

    for sub in range(n_rows // L):
        _ssd_chunk(sub * L, xbuf_ref, z_ref, dt_ref, dtt_ref, cw_ref, cb_ref, dtb_ref, dtbt_ref,
                   alog_ref, alogt_ref, dskip_ref, nw_ref, expand_ref, out_ref, state_ref, y_ref,
                   n_heads=n_heads, d_inner=d_inner)


def _ssd_chunk(r0, xbuf_ref, z_ref, dt_ref, dtt_ref, cw_ref, cb_ref, dtb_ref, dtbt_ref, alog_ref,
               alogt_ref, dskip_ref, nw_ref, expand_ref, out_ref, state_ref, y_ref, *, n_heads,
               d_inner):
    L = SSD_CHUNK
    P = SSD_HEAD_DIM
    NS = SSD_D_STATE
    G = SSD_N_GROUPS
    R = n_heads // G
    GW = R * P
    W = SSD_CONV_WIDTH
    RP = CONV_ROW_PITCH
    tok = slice(r0, r0 + L)

    def rows(first, count):
        return pl.ds(first * RP, count, stride=RP)

    def conv(col0, width):
        slabs = []
        for s in range(col0 // LANES, (col0 + width) // LANES):
            acc = cb_ref[:, s * LANES:(s + 1) * LANES]
            for w in range(W):
                acc = acc + (cw_ref[w:w + 1, s * LANES:(s + 1) * LANES]
                             * xbuf_ref[s, rows(r0 + SUBLANES - (W - 1) + w, L), :])
            slabs.append(acc)
        return _silu(jnp.concatenate(slabs, axis=1))

    def softplus(v):
        return jnp.maximum(v, 0.0) + jnp.log1p(jnp.exp(-jnp.abs(v)))

    dt = softplus(dt_ref[tok, :] + dtb_ref[...])
    dtt = softplus(dtt_ref[:, tok] + dtbt_ref[...])
    da = dt * (-LOG2E * jnp.exp(alog_ref[...]))
    dat = dtt * (-LOG2E * jnp.exp(alogt_ref[...]))
    row = lax.broadcasted_iota(jnp.int32, (L, L), 0)
    col = lax.broadcasted_iota(jnp.int32, (L, L), 1)
    causal = row >= col
    tri = jnp.where(causal, 1.0, 0.0).astype(BF16)
    trit = jnp.where(row <= col, 1.0, 0.0).astype(BF16)

    def split3(v):
        a = v.astype(BF16)
        r1 = v - a.astype(F32)
        b = r1.astype(BF16)
        cc = (r1 - b.astype(F32)).astype(BF16)
        return a, b, cc

    a2 = sum(jnp.dot(tri, p, preferred_element_type=F32) for p in split3(da))
    a2t = sum(jnp.dot(p, trit, preferred_element_type=F32) for p in split3(dat))
    a2_last = a2[L - 1:L, :]

    expand = expand_ref[...]

    def expand_heads(v):
        hi, lo = _split_bf16(v)
        return jnp.dot(jnp.concatenate([hi, lo], axis=1), expand, preferred_element_type=F32)

    in_scale_e = expand_heads(dt * jnp.exp2(a2_last - a2))
    tail8 = jnp.concatenate([jnp.exp2(a2_last), dskip_ref[...],
                             jnp.zeros((SUBLANES - 2, n_heads), F32)], axis=0)
    out_e = expand_heads(jnp.concatenate([jnp.exp2(a2), tail8], axis=0))
    out_scale_e = out_e[0:L, :]
    chunk_decay_e = out_e[L:L + 1, :]
    dskip_e = out_e[L + 1:L + 2, :]

    first_head = lax.broadcasted_iota(jnp.int32, (L, LANES), 1) < P

    for g in range(G):
        x0 = g * GW
        xs = conv(x0, GW)
        bm = conv(d_inner + g * NS, NS).astype(BF16)
        cm = conv(d_inner + G * NS + g * NS, NS).astype(BF16)
        cb = lax.dot_general(cm, bm, (((1,), (1,)), ((), ())), preferred_element_type=F32)
        cb = jnp.where(causal, cb, 0.0)
        y_parts = []
        for pr in range(GW // LANES):
            wgts = []
            for q in range(LANES // P):
                h = g * R + pr * (LANES // P) + q
                seg = a2[:, h:h + 1] - a2t[h:h + 1, :]
                decay = jnp.exp2(jnp.minimum(seg, 0.0))
                wgts.append((cb * decay * dtt[h:h + 1, :]).astype(BF16))
            slab = xs[:, pr * LANES:(pr + 1) * LANES]
            rhs = jnp.concatenate([jnp.where(first_head, slab, 0.0).astype(BF16),
                                   jnp.where(first_head, 0.0, slab).astype(BF16)], axis=0)
            y_parts.append(jnp.dot(jnp.concatenate(wgts, axis=1), rhs, preferred_element_type=F32))
        y = jnp.concatenate(y_parts, axis=1) + xs * dskip_e[:, x0:x0 + GW]
        st = state_ref[:, x0:x0 + GW]
        y = y + (jnp.dot(cm, st.astype(BF16), preferred_element_type=F32)
                 * out_scale_e[:, x0:x0 + GW])
        y_ref[tok, x0:x0 + GW] = y
        xin = (xs * in_scale_e[:, x0:x0 + GW]).astype(BF16)
        st_new = lax.dot_general(bm, xin, (((0,), (0,)), ((), ())), preferred_element_type=F32)
        state_ref[:, x0:x0 + GW] = st * chunk_decay_e[:, x0:x0 + GW] + st_new

    yz = y_ref[tok, :] * _silu(z_ref[tok, :].astype(F32))
    ms = jnp.mean(yz * yz, axis=-1, keepdims=True)
    out_ref[tok, :] = (yz * lax.rsqrt(ms + NORM_EPS) * nw_ref[...]).astype(out_ref.dtype)


def _ssd(proj, dt, dtt, conv_w, conv_b, dt_bias, a_log, d_skip, norm_w, *, batch, seq, d_inner,
         xbc_block, z_block):
    n = batch * seq
    n_heads = dt.shape[1]
    L = SSD_CHUNK * math.gcd(seq // SSD_CHUNK, SSD_CHUNKS_PER_STEP)
    nc = seq // L
    conv_dim = conv_w.shape[1]
    expand = (np.arange(d_inner)[None, :] // SSD_HEAD_DIM == np.arange(n_heads)[:, None])
    expand = jnp.asarray(np.concatenate([expand, expand], axis=0), BF16)
    assert conv_dim % LANES == 0
    kern = functools.partial(_ssd_kernel, n_heads=n_heads, d_inner=d_inner)
    small = lambda shape: pl.BlockSpec(shape, lambda b, c: (0, 0))
    return pl.pallas_call(
        kern,
        grid=(batch, nc),
        in_specs=[
            pl.BlockSpec((L, conv_dim), lambda b, c: (b * nc + c, xbc_block)),
            pl.BlockSpec((L, d_inner), lambda b, c: (b * nc + c, z_block)),
            pl.BlockSpec((L, n_heads), lambda b, c: (b * nc + c, 0)),
            pl.BlockSpec((n_heads, L), lambda b, c: (0, b * nc + c)),
            small((SSD_CONV_WIDTH, conv_dim)),
            small((1, conv_dim)),
            small((1, n_heads)),
            small((n_heads, 1)),
            small((1, n_heads)),
            small((n_heads, 1)),
            small((1, n_heads)),
            small((1, d_inner)),
            small((2 * n_heads, d_inner)),
        ],
        out_specs=pl.BlockSpec((L, d_inner), lambda b, c: (b * nc + c, 0)),
        out_shape=jax.ShapeDtypeStruct((n, d_inner), BF16),
        scratch_shapes=[
            pltpu.VMEM((conv_dim // LANES, CONV_ROW_PITCH * (L + SUBLANES), LANES), F32),
            pltpu.VMEM((SSD_D_STATE, d_inner), F32),
            pltpu.VMEM((L, d_inner), F32),
        ],
        compiler_params=_params("arbitrary", "arbitrary"),
        name="ssd",
    )(proj, proj, dt, dtt, conv_w, conv_b.reshape(1, -1), dt_bias.reshape(1, -1),
      dt_bias.reshape(-1, 1), a_log.reshape(1, -1), a_log.reshape(-1, 1), d_skip.reshape(1, -1),
      norm_w.reshape(1, -1), expand)


def _attn_kernel(q_ref, k_ref, v_ref, o_ref, lse_ref, kp_ref, vp_ref, *, slopes, dilation, hops):
    nb = pl.program_id(2)
    BLK = ATTN_BLOCK
    E = ATTN_HEAD_DIM

    @pl.when(nb == 0)
    def _():
        kp_ref[...] = jnp.zeros(kp_ref.shape, kp_ref.dtype)
        vp_ref[...] = jnp.zeros(vp_ref.shape, vp_ref.dtype)

    scale = E ** -0.5
    nt = (((1,), (1,)), ((), ()))
    nh = len(slopes)
    nblk = q_ref.shape[0] // BLK
    nres = q_ref.shape[1] // (nh * E)
    q = q_ref[...]
    kk = jnp.concatenate([kp_ref[...], k_ref[...]], axis=0)
    vv = jnp.concatenate([vp_ref[...], v_ref[...]], axis=0)
    units = [(j, c) for c in range(nres * nh) for j in range(nblk)]
    cols = lambda c: slice(c * E, (c + 1) * E)
    rows = lambda j: slice(j * BLK, (j + 1) * BLK)
    qi = lax.broadcasted_iota(jnp.int32, (BLK, BLK), 0)
    ki = lax.broadcasted_iota(jnp.int32, (BLK, BLK), 1)
    rel_cur = qi - ki
    rel_prev = rel_cur + BLK
    first = rel_prev <= jnp.where(nb > 0, hops, -1)
    later = rel_prev <= hops
    ok_cur = jnp.concatenate([rel_cur >= 0 for _ in units], axis=0)
    ok_prev = jnp.concatenate([first if j == 0 else later for j, _ in units], axis=0)
    dist_cur = (rel_cur * dilation).astype(F32)
    dist_prev = (rel_prev * dilation).astype(F32)
    bias_cur = jnp.concatenate([slopes[c % nh] * dist_cur for _, c in units], axis=0)
    bias_prev = jnp.concatenate([slopes[c % nh] * dist_prev for _, c in units], axis=0)
    s_cur = jnp.concatenate(
        [lax.dot_general(q[rows(j), cols(h)], kk[rows(j + 1), cols(h)], nt,
                         preferred_element_type=F32) for j, h in units], axis=0)
    s_prev = jnp.concatenate(
        [lax.dot_general(q[rows(j), cols(h)], kk[rows(j), cols(h)], nt,
                         preferred_element_type=F32) for j, h in units], axis=0)
    l_cur = jnp.where(ok_cur, s_cur * scale - bias_cur, -jnp.inf)
    l_prev = jnp.where(ok_prev, s_prev * scale - bias_prev, -jnp.inf)
    m = jnp.max(jnp.maximum(l_cur, l_prev), axis=-1, keepdims=True)
    p_cur = jnp.exp(l_cur - m)
    p_prev = jnp.exp(l_prev - m)
    den = jnp.sum(p_cur + p_prev, axis=-1, keepdims=True)
    p_cur = p_cur.astype(BF16)
    p_prev = p_prev.astype(BF16)
    inv = 1.0 / den
    lse = m + jnp.log(den)
    lane = lax.broadcasted_iota(jnp.int32, (BLK, LANES), 1)
    for res in range(nres):
        for j in range(nblk):
            lse_tile = jnp.zeros((BLK, LANES), F32)
            for h in range(nh):
                c = res * nh + h
                u = rows(units.index((j, c)))
                acc = (jnp.dot(p_cur[u], vv[rows(j + 1), cols(c)], preferred_element_type=F32)
                       + jnp.dot(p_prev[u], vv[rows(j), cols(c)], preferred_element_type=F32))
                o_ref[rows(j), cols(c)] = (acc * inv[u]).astype(o_ref.dtype)
                lse_tile = jnp.where(lane == h, lse[u], lse_tile)
            lse_ref[rows(j), res * LANES:(res + 1) * LANES] = lse_tile
    kp_ref[...] = k_ref[rows(nblk - 1), :]
    vp_ref[...] = v_ref[rows(nblk - 1), :]


def _attn_group(qkv, gi, *, batch, seq):
    window, dilation = DILATION_PATTERNS[gi]
    hops = window // dilation
    n_heads_total = ATTN_HEADS_PER_GROUP * len(DILATION_PATTERNS)
    slopes = tuple(float(2.0 ** (-8.0 * (gi * ATTN_HEADS_PER_GROUP + h + 1) / n_heads_total))
                   for h in range(ATTN_HEADS_PER_GROUP))
    gw = ATTN_HEADS_PER_GROUP * ATTN_HEAD_DIM
    assert seq % (dilation * ATTN_BLOCK) == 0
    sub = seq // dilation
    blocks = math.gcd(sub // ATTN_BLOCK, ATTN_BLOCKS_PER_STEP)
    nres = math.gcd(dilation, ATTN_BLOCKS_PER_STEP // blocks)
    rows = ATTN_BLOCK * blocks
    nb = sub // rows
    nr = dilation // nres
    kern = functools.partial(_attn_kernel, slopes=slopes, dilation=dilation, hops=hops)

    def part(p):
        return pl.BlockSpec((rows, nres * gw), lambda b, r, n: (b * nb + n, p * nr + r))

    o, lse = pl.pallas_call(
        kern,
        grid=(batch, nr, nb),
        in_specs=[part(0), part(1), part(2)],
        out_specs=[
            pl.BlockSpec((rows, nres * gw), lambda b, r, n: (b * nb + n, r)),
            pl.BlockSpec((rows, nres * LANES), lambda b, r, n: (b * nb + n, r)),
        ],
        out_shape=[
            jax.ShapeDtypeStruct((batch * sub, dilation * gw), BF16),
            jax.ShapeDtypeStruct((batch * sub, dilation * LANES), F32),
        ],
        scratch_shapes=[pltpu.VMEM((ATTN_BLOCK, nres * gw), BF16),
                        pltpu.VMEM((ATTN_BLOCK, nres * gw), BF16)],
        compiler_params=_params("arbitrary", "arbitrary", "arbitrary"),
        name=f"attn_g{gi}",
    )(qkv, qkv, qkv)
    return o, lse


def _merge_kernel(yn_ref, o0_ref, o1_ref, o2_ref, l0_ref, l1_ref, l2_ref, gate_ref,
                  wssd_ref, wattn_ref, bg_ref, merged_ref, ostage_ref, lstage_ref, *, d_model):
    E = ATTN_HEAD_DIM
    tm = merged_ref.shape[0]
    n_slabs = o0_ref.shape[1] // LANES
    y_ssd = jnp.dot(yn_ref[...], wssd_ref[...], preferred_element_type=F32)

    def token_major(gi, o_ref, l_ref):
        dil = DILATION_PATTERNS[gi][1]
        if dil == 1:
            return o_ref[...].astype(F32), l_ref[...]
        for r in range(dil):
            rows = pl.ds(r, tm // dil, stride=dil)
            lstage_ref[gi - 1, rows, :] = l_ref[:, r * LANES:(r + 1) * LANES]
            for s in range(n_slabs):
                c0 = (r * n_slabs + s) * LANES
                ostage_ref[gi - 1, s, rows, :] = o_ref[:, c0:c0 + LANES].astype(F32)
        out = jnp.concatenate([ostage_ref[gi - 1, s] for s in range(n_slabs)], axis=1)
        return out, lstage_ref[gi - 1]

    (o0, l0), (o1, l1), (o2, l2) = (token_major(gi, o_ref, l_ref) for gi, (o_ref, l_ref) in
                                    enumerate(((o0_ref, l0_ref), (o1_ref, l1_ref), (o2_ref, l2_ref))))

    lm = jnp.maximum(jnp.maximum(l0, l1), l2)
    e0, e1, e2 = jnp.exp(l0 - lm), jnp.exp(l1 - lm), jnp.exp(l2 - lm)
    inv = 1.0 / (e0 + e1 + e2)
    parts = []
    for h in range(ATTN_HEADS_PER_GROUP):
        sl = slice(h * E, (h + 1) * E)
        parts.append((e0[:, h:h + 1] * inv[:, h:h + 1]) * o0[:, sl]
                     + (e1[:, h:h + 1] * inv[:, h:h + 1]) * o1[:, sl]
                     + (e2[:, h:h + 1] * inv[:, h:h + 1]) * o2[:, sl])
    o = jnp.concatenate(parts, axis=-1).astype(BF16)
    y_attn = jnp.dot(o, wattn_ref[...], preferred_element_type=F32)

    gates = _sigmoid(gate_ref[...].astype(F32) + bg_ref[...])
    merged_ref[...] = (gates[:, :d_model] * y_ssd + gates[:, d_model:] * y_attn).astype(BF16)


def _merge(yn, outs, lses, proj, w_ssd_out, w_attn_out, b_gate, *, tm, gate_block):
    n, d = yn.shape
    aw = outs[0].shape[1]
    dils = [dil for _, dil in DILATION_PATTERNS]
    assert dils[0] == 1 and all(tm % (dil * SUBLANES) == 0 for dil in dils)
    row = lambda w: pl.BlockSpec((tm, w), lambda i: (i, 0))
    grouped = lambda dil, w: pl.BlockSpec((tm // dil, dil * w), lambda i: (i, 0))
    const = lambda shape: pl.BlockSpec(shape, lambda i: (0, 0), pipeline_mode=pl.Buffered(1))
    return pl.pallas_call(
        functools.partial(_merge_kernel, d_model=d),
        grid=(n // tm,),
        in_specs=[
            row(d), *[grouped(dil, aw) for dil in dils], *[grouped(dil, LANES) for dil in dils],
            pl.BlockSpec((tm, 2 * d), lambda i: (i, gate_block)),
            const((d, d)), const((aw, d)), const((1, 2 * d)),
        ],
        out_specs=row(d),
        out_shape=jax.ShapeDtypeStruct((n, d), BF16),
        scratch_shapes=[pltpu.VMEM((len(dils) - 1, aw // LANES, tm, LANES), F32),
                        pltpu.VMEM((len(dils) - 1, tm, LANES), F32)],
        compiler_params=_params("arbitrary"),
        name="merge",
    )(yn, outs[0], outs[1], outs[2], lses[0], lses[1], lses[2], proj,
      w_ssd_out.astype(BF16), w_attn_out.astype(BF16), b_gate.reshape(1, -1))


def _route_kernel(merged_ref, x_ref, wout_ref, fnw_ref, wr_ref, rb_ref,
                  x1_ref, eid_ref, gcol_ref, rank_ref, cnt_ref, carry_ref):
    i = pl.program_id(0)
    tm = x_ref.shape[0]

    @pl.when(i == 0)
    def _():
        carry_ref[...] = jnp.zeros(carry_ref.shape, F32)

    x1 = x_ref[...] + jnp.dot(merged_ref[...], wout_ref[...], preferred_element_type=F32)
    x1_ref[...] = x1

    ms = jnp.mean(x1 * x1, axis=-1, keepdims=True)
    hn = x1 * lax.rsqrt(ms + NORM_EPS) * fnw_ref[...]

    hn_hi, hn_lo = _split_bf16(hn)
    by_hi = jnp.dot(hn_hi, wr_ref[...], preferred_element_type=F32)
    by_lo = jnp.dot(hn_lo, wr_ref[:, 0:LANES], preferred_element_type=F32)
    logits_tok = by_hi[:, 0:LANES] + by_hi[:, LANES:] + by_lo
    logits = logits_tok.T[0:ROUTER_ROWS, :] + rb_ref[...]

    grow = lax.broadcasted_iota(jnp.int32, (SUBLANES, tm), 0)
    gl = jnp.where(grow < N_EXPERT_GROUPS, logits[0:SUBLANES, :], -jnp.inf)
    gmax = jnp.max(gl, axis=0, keepdims=True)
    gidx = jnp.min(jnp.where(gl == gmax, grow, N_EXPERT_GROUPS), axis=0, keepdims=True)
    group_gate = 1.0 / jnp.sum(jnp.exp(gl - gmax), axis=0, keepdims=True)

    in_group = jnp.zeros((EXPERTS_PER_GROUP, tm), F32)
    for g in range(N_EXPERT_GROUPS):
        r0 = ROUTER_EXPERT_ROW0 + g * EXPERTS_PER_GROUP
        in_group = jnp.where(gidx == g, logits[r0:r0 + EXPERTS_PER_GROUP, :], in_group)
    erow = lax.broadcasted_iota(jnp.int32, in_group.shape, 0)
    v1 = jnp.max(in_group, axis=0, keepdims=True)
    i1 = jnp.min(jnp.where(in_group == v1, erow, EXPERTS_PER_GROUP), axis=0, keepdims=True)
    rest = jnp.where(erow == i1, -jnp.inf, in_group)
    v2 = jnp.max(rest, axis=0, keepdims=True)
    i2 = jnp.min(jnp.where(rest == v2, erow, EXPERTS_PER_GROUP), axis=0, keepdims=True)
    t = jnp.exp(v2 - v1)
    g1 = group_gate / (1.0 + t)
    g2 = group_gate * t / (1.0 + t)
    eid1 = gidx * EXPERTS_PER_GROUP + i1
    eid2 = gidx * EXPERTS_PER_GROUP + i2
    slot = lax.broadcasted_iota(jnp.int32, (SUBLANES, tm), 0)
    eid_ref[...] = jnp.where(slot == 0, eid1, jnp.where(slot == 1, eid2, 0))

    grow8 = lax.broadcasted_iota(jnp.int32, (LANES, tm), 0)
    gt = jnp.where(grow8 == 0, g1, jnp.where(grow8 == 1, g2, 0.0))
    gcol_ref[...] = gt.T

    xrow = lax.broadcasted_iota(jnp.int32, (N_EXPERTS, tm), 0)
    oh1 = xrow == eid1
    oh2 = xrow == eid2
    oh = jnp.where(oh1 | oh2, 1.0, 0.0)
    ti = lax.broadcasted_iota(jnp.int32, (tm, tm), 0)
    tj = lax.broadcasted_iota(jnp.int32, (tm, tm), 1)
    before = jnp.where(ti < tj, 1.0, 0.0).astype(BF16)
    prior = jnp.dot(oh.astype(BF16), before, preferred_element_type=F32) + carry_ref[:, 0:1]
    r1 = jnp.sum(jnp.where(oh1, prior, 0.0), axis=0, keepdims=True)
    r2 = jnp.sum(jnp.where(oh2, prior, 0.0), axis=0, keepdims=True)
    rank_ref[...] = jnp.where(slot == 0, r1, jnp.where(slot == 1, r2, 0.0)).astype(jnp.int32)
    carry_ref[...] = carry_ref[...] + jnp.sum(oh, axis=1, keepdims=True)
    cnt_ref[...] = carry_ref[...].astype(jnp.int32)


def _route(merged, x2, w_out, ffn_norm_w, wr, rb, *, tm):
    n, d = x2.shape
    wrc = jnp.zeros((d, LANES), F32)
    wrc = wrc.at[:, 0:N_EXPERT_GROUPS].set(wr[0]).at[:, ROUTER_EXPERT_ROW0:ROUTER_ROWS].set(wr[1])
    rbc = jnp.zeros((ROUTER_ROWS, 1), F32)
    rbc = rbc.at[0:N_EXPERT_GROUPS, 0].set(rb[0]).at[ROUTER_EXPERT_ROW0:, 0].set(rb[1])
    wr_hi_lo = jnp.concatenate(_split_bf16(wrc), axis=1)
    row = lambda w: pl.BlockSpec((tm, w), lambda i: (i, 0))
    const = lambda shape: pl.BlockSpec(shape, lambda i: (0, 0), pipeline_mode=pl.Buffered(1))
    return pl.pallas_call(
        _route_kernel,
        grid=(n // tm,),
        in_specs=[
            row(d), row(d),
            const((d, d)), const((1, d)),
            const((d, 2 * LANES)), const((ROUTER_ROWS, 1)),
        ],
        out_specs=[
            row(d),
            pl.BlockSpec((SUBLANES, tm), lambda i: (0, i)),
            row(LANES),
            pl.BlockSpec((SUBLANES, tm), lambda i: (0, i)),
            pl.BlockSpec((N_EXPERTS, LANES), lambda i: (0, 0)),
        ],
        out_shape=[
            jax.ShapeDtypeStruct((n, d), F32),
            jax.ShapeDtypeStruct((SUBLANES, n), jnp.int32),
            jax.ShapeDtypeStruct((n, LANES), F32),
            jax.ShapeDtypeStruct((SUBLANES, n), jnp.int32),
            jax.ShapeDtypeStruct((N_EXPERTS, LANES), jnp.int32),
        ],
        scratch_shapes=[pltpu.VMEM((N_EXPERTS, LANES), F32)],
        compiler_params=_params("arbitrary"),
        name="route",
    )(merged, x2, w_out.astype(BF16), ffn_norm_w.reshape(1, -1), wr_hi_lo, rbc)


def _scatter_kernel(pad_start_ref, pad_len_ref, n_active_ref, pos_ref, hn_ref, xs_ref, zero_ref,
                    sem, zsem, *, tme, n_tiles):
    tm = hn_ref.shape[0]
    zrows = zero_ref.shape[0]

    @pl.when(pl.program_id(0) == 0)
    def _():
        zero_ref[...] = jnp.zeros(zero_ref.shape, zero_ref.dtype)

        def fills(act):
            def per_expert(e, carry):
                off = pad_start_ref[e]
                left = pad_len_ref[e]
                head = left & (SUBLANES - 1)
                for r in range(SUBLANES - 1):
                    @pl.when(r < head)
                    def _(r=r):
                        act(pltpu.make_async_copy(zero_ref.at[pl.ds(0, 1), :],
                                                  xs_ref.at[pl.ds(off + r, 1), :], zsem))

                off = off + head
                bit = zrows
                while bit >= SUBLANES:
                    take = left & bit

                    @pl.when(take != 0)
                    def _(off=off, bit=bit):
                        act(pltpu.make_async_copy(
                            zero_ref.at[pl.ds(0, bit), :],
                            xs_ref.at[pl.ds(pl.multiple_of(off, SUBLANES), bit), :], zsem))

                    off = off + take
                    bit //= 2
                return carry

            def per_tile(t, carry):
                @pl.when(t >= n_active_ref[0])
                def _():
                    for part in range(tme // zrows):
                        row0 = pl.multiple_of(t * tme + part * zrows, zrows)
                        act(pltpu.make_async_copy(zero_ref, xs_ref.at[pl.ds(row0, zrows), :], zsem))

                return carry

            lax.fori_loop(0, N_EXPERTS, per_expert, 0)
            lax.fori_loop(0, n_tiles, per_tile, 0)

        fills(lambda cp: cp.start())
        fills(lambda cp: cp.wait())

    def start(tb, carry):
        for u in range(ROW_DMA_UNROLL):
            t = tb * ROW_DMA_UNROLL + u
            for k in range(EXPERT_TOP_K):
                dst = pos_ref[0, 0, k * tm + t]
                pltpu.make_async_copy(hn_ref.at[pl.ds(t, 1), :], xs_ref.at[pl.ds(dst, 1), :],
                                      sem).start(priority=(u + k) % 2)
        return carry

    lax.fori_loop(0, tm // ROW_DMA_UNROLL, start, 0)
    for k in range(EXPERT_TOP_K):
        pltpu.make_async_copy(hn_ref, xs_ref.at[pl.ds(0, tm), :], sem).wait()


def _scatter(hn, pos_tiles, pad_start, pad_len, n_active, *, tm, tme, n_tiles):
    n, d = hn.shape
    assert tme % 2 == 0 and (tme // 2) & (tme // 2 - 1) == 0
    grid_spec = pltpu.PrefetchScalarGridSpec(
        num_scalar_prefetch=3,
        grid=(n // tm,),
        in_specs=[
            pl.BlockSpec((1, 1, EXPERT_TOP_K * tm), lambda i, *_: (i, 0, 0),
                         memory_space=pltpu.SMEM),
            pl.BlockSpec((tm, d), lambda i, *_: (i, 0)),
        ],
        out_specs=pl.BlockSpec(memory_space=pl.ANY),
        scratch_shapes=[pltpu.VMEM((tme // 2, d), hn.dtype), pltpu.SemaphoreType.DMA(()),
                        pltpu.SemaphoreType.DMA(())],
    )
    return pl.pallas_call(
        functools.partial(_scatter_kernel, tme=tme, n_tiles=n_tiles),
        grid_spec=grid_spec,
        out_shape=jax.ShapeDtypeStruct((n_tiles * tme, d), hn.dtype),
        compiler_params=_params("arbitrary"),
        name="scatter",
    )(pad_start, pad_len, n_active, pos_tiles, hn)


W_CHUNK_ROWS = 256
W_RING = 8


def _experts_kernel(te_ref, slot_ref, nxt_ref, lo_ref, hi_ref, na_ref, xs_ref, fnw_ref, wg_hbm,
                    wu_hbm, wd_hbm, ys_ref, wres_ref, stage_ref, sem, *, n_gu, n_r, n_h):
    i = pl.program_id(0)
    n_chunks = 2 * n_gu + n_h * n_r
    ring, cr, f = stage_ref.shape
    d = n_gu * cr

    def stage_copy(src, c):
        return pltpu.make_async_copy(src, stage_ref.at[c % ring], sem.at[c % ring])

    def start_chunk(e, c):
        @pl.when(c < n_gu)
        def _():
            stage_copy(wg_hbm.at[e, pl.ds(pl.multiple_of(c * cr, cr), cr), :], c).start()

        @pl.when((c >= n_gu) & (c < 2 * n_gu))
        def _():
            stage_copy(wu_hbm.at[e, pl.ds(pl.multiple_of((c - n_gu) * cr, cr), cr), :], c).start()

        @pl.when(c >= 2 * n_gu)
        def _():
            q = c - 2 * n_gu
            h = q // n_r
            r = q - h * n_r
            stage_copy(wd_hbm.at[e, pl.ds(pl.multiple_of(r * cr, cr), cr),
                                 pl.ds(pl.multiple_of(h * f, f), f)], c).start()

    def start_first(e):
        def body(c, carry):
            start_chunk(e, c)
            return carry

        lax.fori_loop(0, ring, body, 0)

    def convert(e, lo, hi, dst_slot):
        def body(c, carry):
            stage_copy(wg_hbm.at[0, pl.ds(0, cr), :], c).wait()
            wres_ref[dst_slot, pl.ds(pl.multiple_of(c * cr, cr), cr), :] = (
                stage_ref[c % ring].astype(BF16))

            @pl.when(c + ring < n_chunks)
            def _():
                start_chunk(e, c + ring)

            return carry

        lax.fori_loop(lo, hi, body, 0)

    @pl.when(i == 0)
    def _():
        start_first(te_ref[0])
        convert(te_ref[0], 0, n_chunks, slot_ref[0])

    @pl.when(i < na_ref[0])
    def _():
        slot = slot_ref[i]
        nxt = nxt_ref[i]

        @pl.when((nxt >= 0) & (lo_ref[i] == 0))
        def _():
            start_first(nxt)

        xr = xs_ref[...]
        ms = jnp.mean(xr * xr, axis=-1, keepdims=True)
        xb = (xr * lax.rsqrt(ms + NORM_EPS) * fnw_ref[...]).astype(BF16)
        hg = jnp.dot(xb, wres_ref[slot, 0:d, :], preferred_element_type=F32)
        hu = jnp.dot(xb, wres_ref[slot, d:2 * d, :], preferred_element_type=F32)
        hmid = (_silu(hg) * hu).astype(BF16)
        for h in range(n_h):
            r0 = 2 * d + h * n_r * cr
            ys_ref[:, h * f:(h + 1) * f] = jnp.dot(hmid, wres_ref[slot, r0:r0 + n_r * cr, :],
                                                    preferred_element_type=F32)

        @pl.when(nxt >= 0)
        def _():
            convert(nxt, lo_ref[i], hi_ref[i], 1 - slot)

    @pl.when(i >= na_ref[0])
    def _():
        ys_ref[...] = jnp.zeros(ys_ref.shape, F32)


def _experts(xs, norm_w, tile_expert, n_active, cnt, starts, padded, w_g, w_u, w_d, *, tm):
    p, d = xs.shape
    ne, _, f = w_g.shape
    n_tiles = p // tm
    cr = W_CHUNK_ROWS
    assert d % cr == 0 and f % cr == 0 and d % f == 0
    n_gu, n_r, n_h = d // cr, f // cr, d // f
    n_chunks = 2 * n_gu + n_h * n_r

    ids = jnp.arange(ne, dtype=jnp.int32)
    present = cnt > 0
    later = jnp.where(present[None, :] & (ids[None, :] > ids[:, None]), ids[None, :], ne)
    nxt_e = jnp.min(later, axis=1)
    nxt_e = jnp.where(nxt_e >= ne, -1, nxt_e)
    run_e = jnp.cumsum(present.astype(jnp.int32)) - 1
    onehot = (tile_expert[:, None] == ids[None, :]).astype(jnp.int32)
    pick = lambda v: jnp.sum(onehot * v[None, :].astype(jnp.int32), axis=1)
    tiles = jnp.arange(n_tiles, dtype=jnp.int32)
    active = tiles < n_active[0]
    j = tiles - pick(starts) // tm
    k = jnp.maximum(pick(padded) // tm, 1)
    nxt_t = jnp.where(active, pick(nxt_e), -1).astype(jnp.int32)
    lo_t = jnp.where(active, (n_chunks * j) // k, 0).astype(jnp.int32)
    hi_t = jnp.where(active, (n_chunks * (j + 1)) // k, 0).astype(jnp.int32)
    slot_t = (pick(run_e) % 2).astype(jnp.int32)

    def rows(i, te, sl, nx, lo, hi, na):
        return (jnp.minimum(i, na[0] - 1), 0)

    grid_spec = pltpu.PrefetchScalarGridSpec(
        num_scalar_prefetch=6,
        grid=(n_tiles,),
        in_specs=[
            pl.BlockSpec((tm, d), rows),
            pl.BlockSpec((1, d), lambda i, *_: (0, 0)),
            pl.BlockSpec(memory_space=pl.ANY),
            pl.BlockSpec(memory_space=pl.ANY),
            pl.BlockSpec(memory_space=pl.ANY),
        ],
        out_specs=pl.BlockSpec((tm, d), lambda i, *_: (i, 0)),
        scratch_shapes=[
            pltpu.VMEM((2, n_chunks * cr, f), BF16),
            pltpu.VMEM((W_RING, cr, f), F32),
            pltpu.SemaphoreType.DMA((W_RING,)),
        ],
    )
    return pl.pallas_call(
        functools.partial(_experts_kernel, n_gu=n_gu, n_r=n_r, n_h=n_h),
        grid_spec=grid_spec,
        out_shape=jax.ShapeDtypeStruct((p, d), F32),
        compiler_params=_params("arbitrary"),
        name="experts",
    )(tile_expert, slot_t, nxt_t, lo_t, hi_t, n_active, xs, norm_w.reshape(1, d), w_g, w_u, w_d)


def _combine_kernel(pos_ref, pos_next_ref, ys_ref, x1_ref, gcol_ref, nw_ref, out_ref, buf_ref, sem,
                    *, final):
    i = pl.program_id(0)
    n_steps = pl.num_programs(0)
    tm = x1_ref.shape[0]

    def issue(p_ref, slot):
        def start(tb, carry):
            for u in range(ROW_DMA_UNROLL):
                t = tb * ROW_DMA_UNROLL + u
                for k in range(EXPERT_TOP_K):
                    src = p_ref[0, 0, k * tm + t]
                    pltpu.make_async_copy(ys_ref.at[pl.ds(src, 1), :],
                                          buf_ref.at[slot, k, pl.ds(t, 1), :],
                                          sem.at[slot]).start(priority=(u + k) % 2)
            return carry

        lax.fori_loop(0, tm // ROW_DMA_UNROLL, start, 0)

    slot = i % 2

    @pl.when(i == 0)
    def _():
        issue(pos_ref, 0)

    @pl.when(i + 1 < n_steps)
    def _():
        issue(pos_next_ref, 1 - slot)

    for k in range(EXPERT_TOP_K):
        pltpu.make_async_copy(ys_ref.at[pl.ds(0, tm), :], buf_ref.at[slot, k], sem.at[slot]).wait()
    g = gcol_ref[...]
    xo = x1_ref[...] + g[:, 0:1] * buf_ref[slot, 0] + g[:, 1:2] * buf_ref[slot, 1]
    if final:
        ms = jnp.mean(xo * xo, axis=-1, keepdims=True)
        xo = xo * lax.rsqrt(ms + NORM_EPS) * nw_ref[...]
    out_ref[...] = xo


def _combine(ys, pos_tiles, x1, gcol, norm_w, *, tm, final):
    n, d = x1.shape
    last = n // tm - 1
    return pl.pallas_call(
        functools.partial(_combine_kernel, final=final),
        grid=(n // tm,),
        in_specs=[
            pl.BlockSpec((1, 1, EXPERT_TOP_K * tm), lambda i: (i, 0, 0), memory_space=pltpu.SMEM),
            pl.BlockSpec((1, 1, EXPERT_TOP_K * tm), lambda i: (jnp.minimum(i + 1, last), 0, 0),
                         memory_space=pltpu.SMEM),
            pl.BlockSpec(memory_space=pl.ANY),
            pl.BlockSpec((tm, d), lambda i: (i, 0)),
            pl.BlockSpec((tm, LANES), lambda i: (i, 0)),
            pl.BlockSpec((1, d), lambda i: (0, 0)),
        ],
        out_specs=pl.BlockSpec((tm, d), lambda i: (i, 0)),
        out_shape=jax.ShapeDtypeStruct((n, d), F32),
        scratch_shapes=[pltpu.VMEM((2, EXPERT_TOP_K, tm, d), F32), pltpu.SemaphoreType.DMA((2,))],
        compiler_params=_params("arbitrary"),
        name="combine",
    )(pos_tiles, pos_tiles, ys, x1, gcol, norm_w.reshape(1, d))


def _tiles(n):
    return dict(proj_tm=math.gcd(n, 1024), mix_tm=math.gcd(n, 512), moe_tm=256,
                route_tm=math.gcd(n, 512))


def _layer(x2, batch, seq, layer, attn_norm_w, w_in_stack, b_gate, conv_w, conv_b, dt_bias, a_log,
           d_skip, ssd_norm_w, w_ssd_out, w_attn_out, w_out, ffn_norm_w, w_gr, b_gr, w_er, b_er,
           w_g, w_u, w_d):
    n, d = x2.shape
    n_heads = dt_bias.shape[0]
    d_inner = ssd_norm_w.shape[0]
    conv_dim = conv_w.shape[1]
    aw_total = ATTN_HEADS_PER_GROUP * len(DILATION_PATTERNS) * ATTN_HEAD_DIM
    gw = ATTN_HEADS_PER_GROUP * ATTN_HEAD_DIM
    tiles = _tiles(n)

    c_z, c_xbc, c_dt = d_inner, d_inner + conv_dim, d_inner + conv_dim + n_heads
    c_gate = c_dt + QKV_PARTS * aw_total
    segments = [(c_gate, 2 * d), (c_z, conv_dim), (0, d_inner)]
    segments += [(c_dt + p * aw_total + gi * gw, gw)
                 for gi in range(len(DILATION_PATTERNS)) for p in range(QKV_PARTS)]
    assert all(width % gw == 0 for _, width in segments)
    starts = tuple(start + b * gw for start, width in segments for b in range(width // gw))
    w_main, w_dt = _regroup_weight(w_in_stack, layer, starts, gw, c_xbc, n_heads)
    off_gate, off_xbc, off_z = 0, 2 * d, 2 * d + conv_dim
    plain_cols = off_z + d_inner
    assert off_xbc % conv_dim == 0 and off_z % d_inner == 0 and plain_cols % gw == 0

    proj, qkv0, qkv1, qkv2, dt, dtt = _in_proj(x2, attn_norm_w, w_main, w_dt, tm=tiles["proj_tm"],
                                               tn=gw, tp=4 * gw, plain_cols=plain_cols, nh=n_heads)

    yn = _ssd(proj, dt, dtt, conv_w, conv_b, dt_bias, a_log, d_skip, ssd_norm_w, batch=batch,
              seq=seq, d_inner=d_inner, xbc_block=off_xbc // conv_dim, z_block=off_z // d_inner)

    outs, lses = [], []
    for gi, qkv in enumerate((qkv0, qkv1, qkv2)):
        o_g, lse_g = _attn_group(qkv, gi, batch=batch, seq=seq)
        outs.append(o_g)
        lses.append(lse_g)

    merged = _merge(yn, outs, lses, proj, w_ssd_out, w_attn_out, b_gate, tm=tiles["mix_tm"],
                    gate_block=off_gate // (2 * d))
    x1, eid, gcol, rank, counts = _route(merged, x2, w_out, ffn_norm_w, (w_gr, w_er),
                                         (b_gr, b_er), tm=tiles["mix_tm"])

    tme = tiles["moe_tm"]
    cnt = counts[:, 0]
    padded = ((cnt + tme - 1) // tme) * tme
    ends = jnp.cumsum(padded)
    starts = ends - padded
    experts = jnp.arange(N_EXPERTS, dtype=jnp.int32)[:, None, None]
    pos = rank[:EXPERT_TOP_K] + jnp.sum(
        jnp.where(eid[None, :EXPERT_TOP_K] == experts, starts[:, None, None], 0), axis=0)
    n_tiles = EXPERT_TOP_K * n // tme + N_EXPERTS
    tile_start = jnp.arange(n_tiles, dtype=jnp.int32) * tme
    tile_expert = jnp.minimum(jnp.sum(ends[None, :] <= tile_start[:, None], axis=1),
                              N_EXPERTS - 1).astype(jnp.int32)
    n_active = (ends[-1:] // tme).astype(jnp.int32)

    rtm = tiles["route_tm"]
    pos_tiles = pos.reshape(EXPERT_TOP_K, n // rtm, rtm).transpose(1, 0, 2).reshape(
        n // rtm, 1, EXPERT_TOP_K * rtm)
    xs = _scatter(x1, pos_tiles, (starts + cnt).astype(jnp.int32), (padded - cnt).astype(jnp.int32),
                  n_active, tm=rtm, tme=tme, n_tiles=n_tiles)
    ys = _experts(xs, ffn_norm_w, tile_expert, n_active, cnt, starts, padded, w_g, w_u, w_d, tm=tme)
    return ys, pos_tiles, x1, gcol, rtm


def kernel(x, attn_norm_w, w_in, b_gate, conv_w, conv_b, dt_bias, a_log, d_skip, ssd_norm_w,
           w_ssd_out, w_attn_out, w_out, ffn_norm_w, w_group_router, b_group_router,
           w_expert_router, b_expert_router, w_exp_gate, w_exp_up, w_exp_down, final_norm_w):
    batch, seq, d = x.shape
    depth = w_in.shape[0]
    x2 = x.reshape(batch * seq, d)
    for layer in range(depth):
        ys, pos_tiles, x1, gcol, rtm = _layer(
            x2, batch, seq, layer, attn_norm_w[layer], w_in, b_gate[layer], conv_w[layer],
            conv_b[layer], dt_bias[layer], a_log[layer], d_skip[layer], ssd_norm_w[layer],
            w_ssd_out[layer], w_attn_out[layer], w_out[layer], ffn_norm_w[layer],
            w_group_router[layer], b_group_router[layer], w_expert_router[layer],
            b_expert_router[layer], w_exp_gate[layer], w_exp_up[layer], w_exp_down[layer])
        x2 = _combine(ys, pos_tiles, x1, gcol, final_norm_w, tm=rtm, final=layer == depth - 1)
    return x2.reshape(batch, seq, d)
```

```python
import functools
import math

import jax
import jax.numpy as jnp
import numpy as np
from jax import lax
from jax.experimental import pallas as pl
from jax.experimental.pallas import tpu as pltpu

F32 = jnp.float32
BF16 = jnp.bfloat16

NORM_EPS = 1e-6
SSD_HEAD_DIM = 64
SSD_N_GROUPS = 8
SSD_D_STATE = 128
SSD_CONV_WIDTH = 4
SSD_CHUNK = 128
SSD_CHUNKS_PER_STEP = 4
ATTN_HEAD_DIM = 128
DILATION_PATTERNS = ((128, 1), (512, 4), (2048, 16))
ATTN_HEADS_PER_GROUP = 4
ATTN_BLOCK = 128
ATTN_BLOCKS_PER_STEP = 4
N_EXPERT_GROUPS = 4
EXPERTS_PER_GROUP = 8
N_EXPERTS = N_EXPERT_GROUPS * EXPERTS_PER_GROUP
EXPERT_TOP_K = 2

LANES = 128
SUBLANES = 8
VMEM_LIMIT_BYTES = 56 * 1024 * 1024
ROW_DMA_UNROLL = 8

ROUTER_EXPERT_ROW0 = SUBLANES
ROUTER_ROWS = ROUTER_EXPERT_ROW0 + N_EXPERTS


def _params(*semantics):
    return pltpu.CompilerParams(dimension_semantics=semantics, vmem_limit_bytes=VMEM_LIMIT_BYTES)


def _split_bf16(v):
    hi = v.astype(BF16)
    lo = (v - hi.astype(F32)).astype(BF16)
    return hi, lo


def _sigmoid(v):
    return 0.5 + 0.5 * jnp.tanh(0.5 * v)


def _silu(v):
    h = 0.5 * v
    return h + h * jnp.tanh(h)


def _regroup_weight_kernel(starts_ref, wt_hbm, out_ref, narrow_ref, buf_ref, nbuf_ref, sem, nsem,
                           *, layer, narrow_start, nh):
    i = pl.program_id(0)
    n_steps = pl.num_programs(0)
    tn = buf_ref.shape[1]

    def fetch(step, slot):
        rows = pl.ds(pl.multiple_of(starts_ref[step], SUBLANES), tn)
        return pltpu.make_async_copy(wt_hbm.at[layer, rows, :], buf_ref.at[slot], sem.at[slot])

    narrow = pltpu.make_async_copy(wt_hbm.at[layer, pl.ds(narrow_start, LANES), :], nbuf_ref, nsem)

    @pl.when(i == 0)
    def _():
        fetch(0, 0).start()
        narrow.start()

    @pl.when(i + 1 < n_steps)
    def _():
        fetch(i + 1, (i + 1) % 2).start()

    fetch(i, i % 2).wait()
    out_ref[...] = buf_ref[i % 2].T.astype(out_ref.dtype)

    @pl.when(i == n_steps - 1)
    def _():
        narrow.wait()
        lane = lax.broadcasted_iota(jnp.int32, narrow_ref.shape, 1)
        narrow_ref[...] = jnp.where(lane < nh, nbuf_ref[...].T, 0.0).astype(narrow_ref.dtype)


def _regroup_weight(w_stack, layer, starts, tn, narrow_start, nh):
    _, k, cols = w_stack.shape
    assert all(s % SUBLANES == 0 and s + tn <= cols for s in starts)
    assert narrow_start % SUBLANES == 0 and narrow_start + LANES <= cols and nh <= LANES
    wt = jnp.swapaxes(w_stack, 1, 2)
    grid_spec = pltpu.PrefetchScalarGridSpec(
        num_scalar_prefetch=1,
        grid=(len(starts),),
        in_specs=[pl.BlockSpec(memory_space=pl.ANY)],
        out_specs=[pl.BlockSpec((k, tn), lambda i, *_: (0, i)),
                   pl.BlockSpec((k, LANES), lambda i, *_: (0, 0))],
        scratch_shapes=[pltpu.VMEM((2, tn, k), F32), pltpu.VMEM((LANES, k), F32),
                        pltpu.SemaphoreType.DMA((2,)), pltpu.SemaphoreType.DMA(())],
    )
    return pl.pallas_call(
        functools.partial(_regroup_weight_kernel, layer=layer, narrow_start=narrow_start, nh=nh),
        grid_spec=grid_spec,
        out_shape=[jax.ShapeDtypeStruct((k, len(starts) * tn), BF16),
                   jax.ShapeDtypeStruct((k, LANES), BF16)],
        compiler_params=_params("arbitrary"),
        name="regroup_weight",
    )(jnp.asarray(starts, jnp.int32), wt)


QKV_PARTS = 3
RELAYOUT_STRIDE = 4


def _in_proj_kernel(x_hbm, nw_ref, wp_ref, w_ref, wdt_ref, proj_ref, a0_ref, a1_ref, a2_ref,
                    dt_ref, dtt_ref, h_ref, stage_ref, x_ref, xsem, *, n_plain):
    i = pl.program_id(0)
    j = pl.program_id(1)
    tm = proj_ref.shape[0]
    tn = w_ref.shape[1]

    def fetch_x(tile):
        return pltpu.make_async_copy(x_hbm.at[pl.ds(pl.multiple_of(tile * tm, tm), tm), :], x_ref, xsem)

    @pl.when((i == 0) & (j == 0))
    def _():
        fetch_x(0).start()

    @pl.when(j == 0)
    def _():
        fetch_x(i).wait()
        xf = x_ref[...]
        ms = jnp.mean(xf * xf, axis=-1, keepdims=True)
        h = (xf * lax.rsqrt(ms + NORM_EPS) * nw_ref[...]).astype(BF16)
        h_ref[...] = h
        nh = dt_ref.shape[1]
        dt_wide = jnp.dot(h, wdt_ref[...], preferred_element_type=F32)
        dt_ref[...] = dt_wide[:, :nh]
        dtt_ref[...] = dt_wide.T[:nh, :]

        @pl.when(i + 1 < pl.num_programs(0))
        def _():
            fetch_x(i + 1).start()

    @pl.when(j < n_plain)
    def _():
        proj_ref[...] = jnp.dot(h_ref[...], wp_ref[...], preferred_element_type=F32).astype(BF16)

    for gi, a_ref in enumerate((a0_ref, a1_ref, a2_ref)):
        dil = DILATION_PATTERNS[gi][1]
        j0 = n_plain + QKV_PARTS * gi

        @pl.when((j >= j0) & (j < j0 + QKV_PARTS))
        def _(a_ref=a_ref, dil=dil):
            res = jnp.dot(h_ref[...], w_ref[...], preferred_element_type=F32)
            if dil == 1:
                a_ref[...] = res.astype(BF16)
            else:
                two_pass = dil > RELAYOUT_STRIDE and dil % RELAYOUT_STRIDE == 0
                quarter = tm // RELAYOUT_STRIDE
                for s in range(tn // LANES):
                    stage_ref[0] = res[:, s * LANES:(s + 1) * LANES]
                    if two_pass:
                        for r1 in range(RELAYOUT_STRIDE):
                            stage_ref[1, r1 * quarter:(r1 + 1) * quarter, :] = stage_ref[
                                0, pl.ds(r1, quarter, stride=RELAYOUT_STRIDE), :]
                    for r in range(dil):
                        if two_pass:
                            rows = pl.ds((r % RELAYOUT_STRIDE) * quarter + r // RELAYOUT_STRIDE,
                                         tm // dil, stride=dil // RELAYOUT_STRIDE)
                        else:
                            rows = pl.ds(r, tm // dil, stride=dil)
                        c0 = r * tn + s * LANES
                        a_ref[:, c0:c0 + LANES] = stage_ref[int(two_pass), rows, :].astype(BF16)


def _in_proj(x2, norm_w, w_main, w_dt, *, tm, tn, tp, plain_cols, nh):
    n, d = x2.shape
    assert plain_cols % tp == 0 and plain_cols % tn == 0
    n_plain = plain_cols // tp
    qkv_block0 = plain_cols // tn
    n_blocks = n_plain + QKV_PARTS * len(DILATION_PATTERNS)
    assert w_main.shape[1] == plain_cols + QKV_PARTS * len(DILATION_PATTERNS) * tn

    def a_spec(gi):
        dil = DILATION_PATTERNS[gi][1]
        j0 = n_plain + QKV_PARTS * gi
        return pl.BlockSpec((tm // dil, dil * tn),
                            lambda i, j: (i, jnp.clip(j - j0, 0, QKV_PARTS - 1)))

    def a_shape(gi):
        dil = DILATION_PATTERNS[gi][1]
        return jax.ShapeDtypeStruct((n // dil, dil * QKV_PARTS * tn), BF16)

    return pl.pallas_call(
        functools.partial(_in_proj_kernel, n_plain=n_plain),
        grid=(n // tm, n_blocks),
        in_specs=[
            pl.BlockSpec(memory_space=pl.ANY),
            pl.BlockSpec((1, d), lambda i, j: (0, 0)),
            pl.BlockSpec((d, tp), lambda i, j: (0, jnp.minimum(j, n_plain - 1))),
            pl.BlockSpec((d, tn), lambda i, j: (0, qkv_block0 + jnp.maximum(j - n_plain, 0))),
            pl.BlockSpec((d, LANES), lambda i, j: (0, 0)),
        ],
        out_specs=[
            pl.BlockSpec((tm, tp), lambda i, j: (i, jnp.minimum(j, n_plain - 1))),
            a_spec(0), a_spec(1), a_spec(2),
            pl.BlockSpec((tm, nh), lambda i, j: (i, 0)),
            pl.BlockSpec((nh, tm), lambda i, j: (0, i)),
        ],
        out_shape=[
            jax.ShapeDtypeStruct((n, plain_cols), BF16),
            a_shape(0), a_shape(1), a_shape(2),
            jax.ShapeDtypeStruct((n, nh), F32),
            jax.ShapeDtypeStruct((nh, n), F32),
        ],
        scratch_shapes=[pltpu.VMEM((tm, d), BF16), pltpu.VMEM((2, tm, LANES), F32),
                        pltpu.VMEM((tm, d), x2.dtype), pltpu.SemaphoreType.DMA(())],
        compiler_params=_params("arbitrary", "arbitrary"),
        name="in_proj",
    )(x2, norm_w.reshape(1, d), w_main, w_main, w_dt)


LOG2E = math.log2(math.e)
CONV_ROW_PITCH = 2


def _ssd_kernel(xbc_ref, z_ref, dt_ref, dtt_ref, cw_ref, cb_ref, dtb_ref, dtbt_ref, alog_ref,
                alogt_ref, dskip_ref, nw_ref, expand_ref, out_ref,
                xbuf_ref, state_ref, y_ref, *, n_heads, d_inner):
    L = SSD_CHUNK
    P = SSD_HEAD_DIM
    NS = SSD_D_STATE
    G = SSD_N_GROUPS
    R = n_heads // G
    GW = R * P
    W = SSD_CONV_WIDTH
    RP = CONV_ROW_PITCH
    n_slabs = xbuf_ref.shape[0]
    c = pl.program_id(1)

    def rows(first, count):
        return pl.ds(first * RP, count, stride=RP)

    @pl.when(c == 0)
    def _():
        state_ref[...] = jnp.zeros(state_ref.shape, F32)
        for s in range(n_slabs):
            xbuf_ref[s, rows(0, SUBLANES), :] = jnp.zeros((SUBLANES, LANES), F32)

    n_rows = xbc_ref.shape[0]

    @pl.when(c > 0)
    def _():
        for s in range(n_slabs):
            xbuf_ref[s, rows(0, SUBLANES), :] = xbuf_ref[s, rows(n_rows, SUBLANES), :]

    for s in range(n_slabs):
        xbuf_ref[s, rows(SUBLANES, n_rows), :] = xbc_ref[:, s * LANES:(s + 1) * LANES].astype(F32)

    for sub in range(n_rows // L):
        _ssd_chunk(sub * L, xbuf_ref, z_ref, dt_ref, dtt_ref, cw_ref, cb_ref, dtb_ref, dtbt_ref,
                   alog_ref, alogt_ref, dskip_ref, nw_ref, expand_ref, out_ref, state_ref, y_ref,
                   n_heads=n_heads, d_inner=d_inner)


def _ssd_chunk(r0, xbuf_ref, z_ref, dt_ref, dtt_ref, cw_ref, cb_ref, dtb_ref, dtbt_ref, alog_ref,
               alogt_ref, dskip_ref, nw_ref, expand_ref, out_ref, state_ref, y_ref, *, n_heads,
               d_inner):
    L = SSD_CHUNK
    P = SSD_HEAD_DIM
    NS = SSD_D_STATE
    G = SSD_N_GROUPS
    R = n_heads // G
    GW = R * P
    W = SSD_CONV_WIDTH
    RP = CONV_ROW_PITCH
    tok = slice(r0, r0 + L)

    def rows(first, count):
        return pl.ds(first * RP, count, stride=RP)

    def conv(col0, width):
        slabs = []
        for s in range(col0 // LANES, (col0 + width) // LANES):
            acc = cb_ref[:, s * LANES:(s + 1) * LANES]
            for w in range(W):
                acc = acc + (cw_ref[w:w + 1, s * LANES:(s + 1) * LANES]
                             * xbuf_ref[s, rows(r0 + SUBLANES - (W - 1) + w, L), :])
            slabs.append(acc)
        return _silu(jnp.concatenate(slabs, axis=1))

    def softplus(v):
        return jnp.maximum(v, 0.0) + jnp.log1p(jnp.exp(-jnp.abs(v)))

    dt = softplus(dt_ref[tok, :] + dtb_ref[...])
    dtt = softplus(dtt_ref[:, tok] + dtbt_ref[...])
    da = dt * (-LOG2E * jnp.exp(alog_ref[...]))
    dat = dtt * (-LOG2E * jnp.exp(alogt_ref[...]))
    row = lax.broadcasted_iota(jnp.int32, (L, L), 0)
    col = lax.broadcasted_iota(jnp.int32, (L, L), 1)
    causal = row >= col
    tri = jnp.where(causal, 1.0, 0.0).astype(BF16)
    trit = jnp.where(row <= col, 1.0, 0.0).astype(BF16)

    def split3(v):
        a = v.astype(BF16)
        r1 = v - a.astype(F32)
        b = r1.astype(BF16)
        cc = (r1 - b.astype(F32)).astype(BF16)
        return a, b, cc

    a2 = sum(jnp.dot(tri, p, preferred_element_type=F32) for p in split3(da))
    a2t = sum(jnp.dot(p, trit, preferred_element_type=F32) for p in split3(dat))
    a2_last = a2[L - 1:L, :]

    expand = expand_ref[...]

    def expand_heads(v):
        hi, lo = _split_bf16(v)
        return jnp.dot(jnp.concatenate([hi, lo], axis=1), expand, preferred_element_type=F32)

    in_scale_e = expand_heads(dt * jnp.exp2(a2_last - a2))
    tail8 = jnp.concatenate([jnp.exp2(a2_last), dskip_ref[...],
                             jnp.zeros((SUBLANES - 2, n_heads), F32)], axis=0)
    out_e = expand_heads(jnp.concatenate([jnp.exp2(a2), tail8], axis=0))
    out_scale_e = out_e[0:L, :]
    chunk_decay_e = out_e[L:L + 1, :]
    dskip_e = out_e[L + 1:L + 2, :]

    first_head = lax.broadcasted_iota(jnp.int32, (L, LANES), 1) < P

    for g in range(G):
        x0 = g * GW
        xs = conv(x0, GW)
        bm = conv(d_inner + g * NS, NS).astype(BF16)
        cm = conv(d_inner + G * NS + g * NS, NS).astype(BF16)
        cb = lax.dot_general(cm, bm, (((1,), (1,)), ((), ())), preferred_element_type=F32)
        cb = jnp.where(causal, cb, 0.0)
        y_parts = []
        for pr in range(GW // LANES):
            wgts = []
            for q in range(LANES // P):
                h = g * R + pr * (LANES // P) + q
                seg = a2[:, h:h + 1] - a2t[h:h + 1, :]
                decay = jnp.exp2(jnp.minimum(seg, 0.0))
                wgts.append((cb * decay * dtt[h:h + 1, :]).astype(BF16))
            slab = xs[:, pr * LANES:(pr + 1) * LANES]
            rhs = jnp.concatenate([jnp.where(first_head, slab, 0.0).astype(BF16),
                                   jnp.where(first_head, 0.0, slab).astype(BF16)], axis=0)
            y_parts.append(jnp.dot(jnp.concatenate(wgts, axis=1), rhs, preferred_element_type=F32))
        y = jnp.concatenate(y_parts, axis=1) + xs * dskip_e[:, x0:x0 + GW]
        st = state_ref[:, x0:x0 + GW]
        y = y + (jnp.dot(cm, st.astype(BF16), preferred_element_type=F32)
                 * out_scale_e[:, x0:x0 + GW])
        y_ref[tok, x0:x0 + GW] = y
        xin = (xs * in_scale_e[:, x0:x0 + GW]).astype(BF16)
        st_new = lax.dot_general(bm, xin, (((0,), (0,)), ((), ())), preferred_element_type=F32)
        state_ref[:, x0:x0 + GW] = st * chunk_decay_e[:, x0:x0 + GW] + st_new

    yz = y_ref[tok, :] * _silu(z_ref[tok, :].astype(F32))
    ms = jnp.mean(yz * yz, axis=-1, keepdims=True)
    out_ref[tok, :] = (yz * lax.rsqrt(ms + NORM_EPS) * nw_ref[...]).astype(out_ref.dtype)


def _ssd(proj, dt, dtt, conv_w, conv_b, dt_bias, a_log, d_skip, norm_w, *, batch, seq, d_inner,
         xbc_block, z_block):
    n = batch * seq
    n_heads = dt.shape[1]
    L = SSD_CHUNK * math.gcd(seq // SSD_CHUNK, SSD_CHUNKS_PER_STEP)
    nc = seq // L
    conv_dim = conv_w.shape[1]
    expand = (np.arange(d_inner)[None, :] // SSD_HEAD_DIM == np.arange(n_heads)[:, None])
    expand = jnp.asarray(np.concatenate([expand, expand], axis=0), BF16)
    assert conv_dim % LANES == 0
    kern = functools.partial(_ssd_kernel, n_heads=n_heads, d_inner=d_inner)
    small = lambda shape: pl.BlockSpec(shape, lambda b, c: (0, 0))
    return pl.pallas_call(
        kern,
        grid=(batch, nc),
        in_specs=[
            pl.BlockSpec((L, conv_dim), lambda b, c: (b * nc + c, xbc_block)),
            pl.BlockSpec((L, d_inner), lambda b, c: (b * nc + c, z_block)),
            pl.BlockSpec((L, n_heads), lambda b, c: (b * nc + c, 0)),
            pl.BlockSpec((n_heads, L), lambda b, c: (0, b * nc + c)),
            small((SSD_CONV_WIDTH, conv_dim)),
            small((1, conv_dim)),
            small((1, n_heads)),
            small((n_heads, 1)),
            small((1, n_heads)),
            small((n_heads, 1)),
            small((1, n_heads)),
            small((1, d_inner)),
            small((2 * n_heads, d_inner)),
        ],
        out_specs=pl.BlockSpec((L, d_inner), lambda b, c: (b * nc + c, 0)),
        out_shape=jax.ShapeDtypeStruct((n, d_inner), BF16),
        scratch_shapes=[
            pltpu.VMEM((conv_dim // LANES, CONV_ROW_PITCH * (L + SUBLANES), LANES), F32),
            pltpu.VMEM((SSD_D_STATE, d_inner), F32),
            pltpu.VMEM((L, d_inner), F32),
        ],
        compiler_params=_params("arbitrary", "arbitrary"),
        name="ssd",
    )(proj, proj, dt, dtt, conv_w, conv_b.reshape(1, -1), dt_bias.reshape(1, -1),
      dt_bias.reshape(-1, 1), a_log.reshape(1, -1), a_log.reshape(-1, 1), d_skip.reshape(1, -1),
      norm_w.reshape(1, -1), expand)


def _attn_kernel(q_ref, k_ref, v_ref, o_ref, lse_ref, kp_ref, vp_ref, *, slopes, dilation, hops):
    nb = pl.program_id(2)
    BLK = ATTN_BLOCK
    E = ATTN_HEAD_DIM

    @pl.when(nb == 0)
    def _():
        kp_ref[...] = jnp.zeros(kp_ref.shape, kp_ref.dtype)
        vp_ref[...] = jnp.zeros(vp_ref.shape, vp_ref.dtype)

    scale = E ** -0.5
    nt = (((1,), (1,)), ((), ()))
    nh = len(slopes)
    nblk = q_ref.shape[0] // BLK
    nres = q_ref.shape[1] // (nh * E)
    q = q_ref[...]
    kk = jnp.concatenate([kp_ref[...], k_ref[...]], axis=0)
    vv = jnp.concatenate([vp_ref[...], v_ref[...]], axis=0)
    units = [(j, c) for c in range(nres * nh) for j in range(nblk)]
    cols = lambda c: slice(c * E, (c + 1) * E)
    rows = lambda j: slice(j * BLK, (j + 1) * BLK)
    qi = lax.broadcasted_iota(jnp.int32, (BLK, BLK), 0)
    ki = lax.broadcasted_iota(jnp.int32, (BLK, BLK), 1)
    rel_cur = qi - ki
    rel_prev = rel_cur + BLK
    first = rel_prev <= jnp.where(nb > 0, hops, -1)
    later = rel_prev <= hops
    ok_cur = jnp.concatenate([rel_cur >= 0 for _ in units], axis=0)
    ok_prev = jnp.concatenate([first if j == 0 else later for j, _ in units], axis=0)
    dist_cur = (rel_cur * dilation).astype(F32)
    dist_prev = (rel_prev * dilation).astype(F32)
    bias_cur = jnp.concatenate([slopes[c % nh] * dist_cur for _, c in units], axis=0)
    bias_prev = jnp.concatenate([slopes[c % nh] * dist_prev for _, c in units], axis=0)
    s_cur = jnp.concatenate(
        [lax.dot_general(q[rows(j), cols(h)], kk[rows(j + 1), cols(h)], nt,
                         preferred_element_type=F32) for j, h in units], axis=0)
    s_prev = jnp.concatenate(
        [lax.dot_general(q[rows(j), cols(h)], kk[rows(j), cols(h)], nt,
                         preferred_element_type=F32) for j, h in units], axis=0)
    l_cur = jnp.where(ok_cur, s_cur * scale - bias_cur, -jnp.inf)
    l_prev = jnp.where(ok_prev, s_prev * scale - bias_prev, -jnp.inf)
    m = jnp.max(jnp.maximum(l_cur, l_prev), axis=-1, keepdims=True)
    p_cur = jnp.exp(l_cur - m)
    p_prev = jnp.exp(l_prev - m)
    den = jnp.sum(p_cur + p_prev, axis=-1, keepdims=True)
    p_cur = p_cur.astype(BF16)
    p_prev = p_prev.astype(BF16)
    inv = 1.0 / den
    lse = m + jnp.log(den)
    lane = lax.broadcasted_iota(jnp.int32, (BLK, LANES), 1)
    for res in range(nres):
        for j in range(nblk):
            lse_tile = jnp.zeros((BLK, LANES), F32)
            for h in range(nh):
                c = res * nh + h
                u = rows(units.index((j, c)))
                acc = (jnp.dot(p_cur[u], vv[rows(j + 1), cols(c)], preferred_element_type=F32)
                       + jnp.dot(p_prev[u], vv[rows(j), cols(c)], preferred_element_type=F32))
                o_ref[rows(j), cols(c)] = (acc * inv[u]).astype(o_ref.dtype)
                lse_tile = jnp.where(lane == h, lse[u], lse_tile)
            lse_ref[rows(j), res * LANES:(res + 1) * LANES] = lse_tile
    kp_ref[...] = k_ref[rows(nblk - 1), :]
    vp_ref[...] = v_ref[rows(nblk - 1), :]


def _attn_group(qkv, gi, *, batch, seq):
    window, dilation = DILATION_PATTERNS[gi]
    hops = window // dilation
    n_heads_total = ATTN_HEADS_PER_GROUP * len(DILATION_PATTERNS)
    slopes = tuple(float(2.0 ** (-8.0 * (gi * ATTN_HEADS_PER_GROUP + h + 1) / n_heads_total))
                   for h in range(ATTN_HEADS_PER_GROUP))
    gw = ATTN_HEADS_PER_GROUP * ATTN_HEAD_DIM
    assert seq % (dilation * ATTN_BLOCK) == 0
    sub = seq // dilation
    blocks = math.gcd(sub // ATTN_BLOCK, ATTN_BLOCKS_PER_STEP)
    nres = math.gcd(dilation, ATTN_BLOCKS_PER_STEP // blocks)
    rows = ATTN_BLOCK * blocks
    nb = sub // rows
    nr = dilation // nres
    kern = functools.partial(_attn_kernel, slopes=slopes, dilation=dilation, hops=hops)

    def part(p):
        return pl.BlockSpec((rows, nres * gw), lambda b, r, n: (b * nb + n, p * nr + r))

    o, lse = pl.pallas_call(
        kern,
        grid=(batch, nr, nb),
        in_specs=[part(0), part(1), part(2)],
        out_specs=[
            pl.BlockSpec((rows, nres * gw), lambda b, r, n: (b * nb + n, r)),
            pl.BlockSpec((rows, nres * LANES), lambda b, r, n: (b * nb + n, r)),
        ],
        out_shape=[
            jax.ShapeDtypeStruct((batch * sub, dilation * gw), BF16),
            jax.ShapeDtypeStruct((batch * sub, dilation * LANES), F32),
        ],
        scratch_shapes=[pltpu.VMEM((ATTN_BLOCK, nres * gw), BF16),
                        pltpu.VMEM((ATTN_BLOCK, nres * gw), BF16)],
        compiler_params=_params("arbitrary", "arbitrary", "arbitrary"),
        name=f"attn_g{gi}",
    )(qkv, qkv, qkv)
    return o, lse


def _merge_kernel(yn_ref, o0_ref, o1_ref, o2_ref, l0_ref, l1_ref, l2_ref, gate_ref,
                  wssd_ref, wattn_ref, bg_ref, merged_ref, ostage_ref, lstage_ref, *, d_model):
    E = ATTN_HEAD_DIM
    tm = merged_ref.shape[0]
    n_slabs = o0_ref.shape[1] // LANES
    y_ssd = jnp.dot(yn_ref[...], wssd_ref[...], preferred_element_type=F32)

    def token_major(gi, o_ref, l_ref):
        dil = DILATION_PATTERNS[gi][1]
        if dil == 1:
            return o_ref[...].astype(F32), l_ref[...]
        for r in range(dil):
            rows = pl.ds(r, tm // dil, stride=dil)
            lstage_ref[gi - 1, rows, :] = l_ref[:, r * LANES:(r + 1) * LANES]
            for s in range(n_slabs):
                c0 = (r * n_slabs + s) * LANES
                ostage_ref[gi - 1, s, rows, :] = o_ref[:, c0:c0 + LANES].astype(F32)
        out = jnp.concatenate([ostage_ref[gi - 1, s] for s in range(n_slabs)], axis=1)
        return out, lstage_ref[gi - 1]

    (o0, l0), (o1, l1), (o2, l2) = (token_major(gi, o_ref, l_ref) for gi, (o_ref, l_ref) in
                                    enumerate(((o0_ref, l0_ref), (o1_ref, l1_ref), (o2_ref, l2_ref))))

    lm = jnp.maximum(jnp.maximum(l0, l1), l2)
    e0, e1, e2 = jnp.exp(l0 - lm), jnp.exp(l1 - lm), jnp.exp(l2 - lm)
    inv = 1.0 / (e0 + e1 + e2)
    parts = []
    for h in range(ATTN_HEADS_PER_GROUP):
        sl = slice(h * E, (h + 1) * E)
        parts.append((e0[:, h:h + 1] * inv[:, h:h + 1]) * o0[:, sl]
                     + (e1[:, h:h + 1] * inv[:, h:h + 1]) * o1[:, sl]
                     + (e2[:, h:h + 1] * inv[:, h:h + 1]) * o2[:, sl])
    o = jnp.concatenate(parts, axis=-1).astype(BF16)
    y_attn = jnp.dot(o, wattn_ref[...], preferred_element_type=F32)

    gates = _sigmoid(gate_ref[...].astype(F32) + bg_ref[...])
    merged_ref[...] = (gates[:, :d_model] * y_ssd + gates[:, d_model:] * y_attn).astype(BF16)


def _merge(yn, outs, lses, proj, w_ssd_out, w_attn_out, b_gate, *, tm, gate_block):
    n, d = yn.shape
    aw = outs[0].shape[1]
    dils = [dil for _, dil in DILATION_PATTERNS]
    assert dils[0] == 1 and all(tm % (dil * SUBLANES) == 0 for dil in dils)
    row = lambda w: pl.BlockSpec((tm, w), lambda i: (i, 0))
    grouped = lambda dil, w: pl.BlockSpec((tm // dil, dil * w), lambda i: (i, 0))
    const = lambda shape: pl.BlockSpec(shape, lambda i: (0, 0), pipeline_mode=pl.Buffered(1))
    return pl.pallas_call(
        functools.partial(_merge_kernel, d_model=d),
        grid=(n // tm,),
        in_specs=[
            row(d), *[grouped(dil, aw) for dil in dils], *[grouped(dil, LANES) for dil in dils],
            pl.BlockSpec((tm, 2 * d), lambda i: (i, gate_block)),
            const((d, d)), const((aw, d)), const((1, 2 * d)),
        ],
        out_specs=row(d),
        out_shape=jax.ShapeDtypeStruct((n, d), BF16),
        scratch_shapes=[pltpu.VMEM((len(dils) - 1, aw // LANES, tm, LANES), F32),
                        pltpu.VMEM((len(dils) - 1, tm, LANES), F32)],
        compiler_params=_params("arbitrary"),
        name="merge",
    )(yn, outs[0], outs[1], outs[2], lses[0], lses[1], lses[2], proj,
      w_ssd_out.astype(BF16), w_attn_out.astype(BF16), b_gate.reshape(1, -1))


def _route_kernel(merged_ref, x_ref, wout_ref, fnw_ref, wr_ref, rb_ref,
                  x1_ref, eid_ref, gcol_ref, rank_ref, cnt_ref, carry_ref):
    i = pl.program_id(0)
    tm = x_ref.shape[0]

    @pl.when(i == 0)
    def _():
        carry_ref[...] = jnp.zeros(carry_ref.shape, F32)

    x1 = x_ref[...] + jnp.dot(merged_ref[...], wout_ref[...], preferred_element_type=F32)
    x1_ref[...] = x1

    ms = jnp.mean(x1 * x1, axis=-1, keepdims=True)
    hn = x1 * lax.rsqrt(ms + NORM_EPS) * fnw_ref[...]

    hn_hi, hn_lo = _split_bf16(hn)
    by_hi = jnp.dot(hn_hi, wr_ref[...], preferred_element_type=F32)
    by_lo = jnp.dot(hn_lo, wr_ref[:, 0:LANES], preferred_element_type=F32)
    logits_tok = by_hi[:, 0:LANES] + by_hi[:, LANES:] + by_lo
    logits = logits_tok.T[0:ROUTER_ROWS, :] + rb_ref[...]

    grow = lax.broadcasted_iota(jnp.int32, (SUBLANES, tm), 0)
    gl = jnp.where(grow < N_EXPERT_GROUPS, logits[0:SUBLANES, :], -jnp.inf)
    gmax = jnp.max(gl, axis=0, keepdims=True)
    gidx = jnp.min(jnp.where(gl == gmax, grow, N_EXPERT_GROUPS), axis=0, keepdims=True)
    group_gate = 1.0 / jnp.sum(jnp.exp(gl - gmax), axis=0, keepdims=True)

    in_group = jnp.zeros((EXPERTS_PER_GROUP, tm), F32)
    for g in range(N_EXPERT_GROUPS):
        r0 = ROUTER_EXPERT_ROW0 + g * EXPERTS_PER_GROUP
        in_group = jnp.where(gidx == g, logits[r0:r0 + EXPERTS_PER_GROUP, :], in_group)
    erow = lax.broadcasted_iota(jnp.int32, in_group.shape, 0)
    v1 = jnp.max(in_group, axis=0, keepdims=True)
    i1 = jnp.min(jnp.where(in_group == v1, erow, EXPERTS_PER_GROUP), axis=0, keepdims=True)
    rest = jnp.where(erow == i1, -jnp.inf, in_group)
    v2 = jnp.max(rest, axis=0, keepdims=True)
    i2 = jnp.min(jnp.where(rest == v2, erow, EXPERTS_PER_GROUP), axis=0, keepdims=True)
    t = jnp.exp(v2 - v1)
    g1 = group_gate / (1.0 + t)
    g2 = group_gate * t / (1.0 + t)
    eid1 = gidx * EXPERTS_PER_GROUP + i1
    eid2 = gidx * EXPERTS_PER_GROUP + i2
    slot = lax.broadcasted_iota(jnp.int32, (SUBLANES, tm), 0)
    eid_ref[...] = jnp.where(slot == 0, eid1, jnp.where(slot == 1, eid2, 0))

    grow8 = lax.broadcasted_iota(jnp.int32, (LANES, tm), 0)
    gt = jnp.where(grow8 == 0, g1, jnp.where(grow8 == 1, g2, 0.0))
    gcol_ref[...] = gt.T

    xrow = lax.broadcasted_iota(jnp.int32, (N_EXPERTS, tm), 0)
    oh1 = xrow == eid1
    oh2 = xrow == eid2
    oh = jnp.where(oh1 | oh2, 1.0, 0.0)
    ti = lax.broadcasted_iota(jnp.int32, (tm, tm), 0)
    tj = lax.broadcasted_iota(jnp.int32, (tm, tm), 1)
    before = jnp.where(ti < tj, 1.0, 0.0).astype(BF16)
    prior = jnp.dot(oh.astype(BF16), before, preferred_element_type=F32) + carry_ref[:, 0:1]
    r1 = jnp.sum(jnp.where(oh1, prior, 0.0), axis=0, keepdims=True)
    r2 = jnp.sum(jnp.where(oh2, prior, 0.0), axis=0, keepdims=True)
    rank_ref[...] = jnp.where(slot == 0, r1, jnp.where(slot == 1, r2, 0.0)).astype(jnp.int32)
    carry_ref[...] = carry_ref[...] + jnp.sum(oh, axis=1, keepdims=True)
    cnt_ref[...] = carry_ref[...].astype(jnp.int32)


def _route(merged, x2, w_out, ffn_norm_w, wr, rb, *, tm):
    n, d = x2.shape
    wrc = jnp.zeros((d, LANES), F32)
    wrc = wrc.at[:, 0:N_EXPERT_GROUPS].set(wr[0]).at[:, ROUTER_EXPERT_ROW0:ROUTER_ROWS].set(wr[1])
    rbc = jnp.zeros((ROUTER_ROWS, 1), F32)
    rbc = rbc.at[0:N_EXPERT_GROUPS, 0].set(rb[0]).at[ROUTER_EXPERT_ROW0:, 0].set(rb[1])
    wr_hi_lo = jnp.concatenate(_split_bf16(wrc), axis=1)
    row = lambda w: pl.BlockSpec((tm, w), lambda i: (i, 0))
    const = lambda shape: pl.BlockSpec(shape, lambda i: (0, 0), pipeline_mode=pl.Buffered(1))
    return pl.pallas_call(
        _route_kernel,
        grid=(n // tm,),
        in_specs=[
            row(d), row(d),
            const((d, d)), const((1, d)),
            const((d, 2 * LANES)), const((ROUTER_ROWS, 1)),
        ],
        out_specs=[
            row(d),
            pl.BlockSpec((SUBLANES, tm), lambda i: (0, i)),
            row(LANES),
            pl.BlockSpec((SUBLANES, tm), lambda i: (0, i)),
            pl.BlockSpec((N_EXPERTS, LANES), lambda i: (0, 0)),
        ],
        out_shape=[
            jax.ShapeDtypeStruct((n, d), F32),
            jax.ShapeDtypeStruct((SUBLANES, n), jnp.int32),
            jax.ShapeDtypeStruct((n, LANES), F32),
            jax.ShapeDtypeStruct((SUBLANES, n), jnp.int32),
            jax.ShapeDtypeStruct((N_EXPERTS, LANES), jnp.int32),
        ],
        scratch_shapes=[pltpu.VMEM((N_EXPERTS, LANES), F32)],
        compiler_params=_params("arbitrary"),
        name="route",
    )(merged, x2, w_out.astype(BF16), ffn_norm_w.reshape(1, -1), wr_hi_lo, rbc)


def _scatter_kernel(pad_start_ref, pad_len_ref, n_active_ref, pos_ref, hn_ref, xs_ref, zero_ref,
                    sem, zsem, *, tme, n_tiles):
    tm = hn_ref.shape[0]
    zrows = zero_ref.shape[0]

    @pl.when(pl.program_id(0) == 0)
    def _():
        zero_ref[...] = jnp.zeros(zero_ref.shape, zero_ref.dtype)

        def fills(act):
            def per_expert(e, carry):
                off = pad_start_ref[e]
                left = pad_len_ref[e]
                head = left & (SUBLANES - 1)
                for r in range(SUBLANES - 1):
                    @pl.when(r < head)
                    def _(r=r):
                        act(pltpu.make_async_copy(zero_ref.at[pl.ds(0, 1), :],
                                                  xs_ref.at[pl.ds(off + r, 1), :], zsem))

                off = off + head
                bit = zrows
                while bit >= SUBLANES:
                    take = left & bit

                    @pl.when(take != 0)
                    def _(off=off, bit=bit):
                        act(pltpu.make_async_copy(
                            zero_ref.at[pl.ds(0, bit), :],
                            xs_ref.at[pl.ds(pl.multiple_of(off, SUBLANES), bit), :], zsem))

                    off = off + take
                    bit //= 2
                return carry

            def per_tile(t, carry):
                @pl.when(t >= n_active_ref[0])
                def _():
                    for part in range(tme // zrows):
                        row0 = pl.multiple_of(t * tme + part * zrows, zrows)
                        act(pltpu.make_async_copy(zero_ref, xs_ref.at[pl.ds(row0, zrows), :], zsem))

                return carry

            lax.fori_loop(0, N_EXPERTS, per_expert, 0)
            lax.fori_loop(0, n_tiles, per_tile, 0)

        fills(lambda cp: cp.start())
        fills(lambda cp: cp.wait())

    def start(tb, carry):
        for u in range(ROW_DMA_UNROLL):
            t = tb * ROW_DMA_UNROLL + u
            for k in range(EXPERT_TOP_K):
                dst = pos_ref[0, 0, k * tm + t]
                pltpu.make_async_copy(hn_ref.at[pl.ds(t, 1), :], xs_ref.at[pl.ds(dst, 1), :],
                                      sem).start(priority=(u + k) % 2)
        return carry

    lax.fori_loop(0, tm // ROW_DMA_UNROLL, start, 0)
    for k in range(EXPERT_TOP_K):
        pltpu.make_async_copy(hn_ref, xs_ref.at[pl.ds(0, tm), :], sem).wait()


def _scatter(hn, pos_tiles, pad_start, pad_len, n_active, *, tm, tme, n_tiles):
    n, d = hn.shape
    assert tme % 2 == 0 and (tme // 2) & (tme // 2 - 1) == 0
    grid_spec = pltpu.PrefetchScalarGridSpec(
        num_scalar_prefetch=3,
        grid=(n // tm,),
        in_specs=[
            pl.BlockSpec((1, 1, EXPERT_TOP_K * tm), lambda i, *_: (i, 0, 0),
                         memory_space=pltpu.SMEM),
            pl.BlockSpec((tm, d), lambda i, *_: (i, 0)),
        ],
        out_specs=pl.BlockSpec(memory_space=pl.ANY),
        scratch_shapes=[pltpu.VMEM((tme // 2, d), hn.dtype), pltpu.SemaphoreType.DMA(()),
                        pltpu.SemaphoreType.DMA(())],
    )
    return pl.pallas_call(
        functools.partial(_scatter_kernel, tme=tme, n_tiles=n_tiles),
        grid_spec=grid_spec,
        out_shape=jax.ShapeDtypeStruct((n_tiles * tme, d), hn.dtype),
        compiler_params=_params("arbitrary"),
        name="scatter",
    )(pad_start, pad_len, n_active, pos_tiles, hn)


W_CHUNK_ROWS = 256
W_RING = 8


def _experts_kernel(te_ref, slot_ref, nxt_ref, lo_ref, hi_ref, na_ref, xs_ref, fnw_ref, wg_hbm,
                    wu_hbm, wd_hbm, ys_ref, wres_ref, stage_ref, sem, *, n_gu, n_r, n_h):
    i = pl.program_id(0)
    n_chunks = 2 * n_gu + n_h * n_r
    ring, cr, f = stage_ref.shape
    d = n_gu * cr

    def stage_copy(src, c):
        return pltpu.make_async_copy(src, stage_ref.at[c % ring], sem.at[c % ring])

    def start_chunk(e, c):
        @pl.when(c < n_gu)
        def _():
            stage_copy(wg_hbm.at[e, pl.ds(pl.multiple_of(c * cr, cr), cr), :], c).start()

        @pl.when((c >= n_gu) & (c < 2 * n_gu))
        def _():
            stage_copy(wu_hbm.at[e, pl.ds(pl.multiple_of((c - n_gu) * cr, cr), cr), :], c).start()

        @pl.when(c >= 2 * n_gu)
        def _():
            q = c - 2 * n_gu
            h = q // n_r
            r = q - h * n_r
            stage_copy(wd_hbm.at[e, pl.ds(pl.multiple_of(r * cr, cr), cr),
                                 pl.ds(pl.multiple_of(h * f, f), f)], c).start()

    def start_first(e):
        def body(c, carry):
            start_chunk(e, c)
            return carry

        lax.fori_loop(0, ring, body, 0)

    def convert(e, lo, hi, dst_slot):
        def body(c, carry):
            stage_copy(wg_hbm.at[0, pl.ds(0, cr), :], c).wait()
            wres_ref[dst_slot, pl.ds(pl.multiple_of(c * cr, cr), cr), :] = (
                stage_ref[c % ring].astype(BF16))

            @pl.when(c + ring < n_chunks)
            def _():
                start_chunk(e, c + ring)

            return carry

        lax.fori_loop(lo, hi, body, 0)

    @pl.when(i == 0)
    def _():
        start_first(te_ref[0])
        convert(te_ref[0], 0, n_chunks, slot_ref[0])

    @pl.when(i < na_ref[0])
    def _():
        slot = slot_ref[i]
        nxt = nxt_ref[i]

        @pl.when((nxt >= 0) & (lo_ref[i] == 0))
        def _():
            start_first(nxt)

        xr = xs_ref[...]
        ms = jnp.mean(xr * xr, axis=-1, keepdims=True)
        xb = (xr * lax.rsqrt(ms + NORM_EPS) * fnw_ref[...]).astype(BF16)
        hg = jnp.dot(xb, wres_ref[slot, 0:d, :], preferred_element_type=F32)
        hu = jnp.dot(xb, wres_ref[slot, d:2 * d, :], preferred_element_type=F32)
        hmid = (_silu(hg) * hu).astype(BF16)
        for h in range(n_h):
            r0 = 2 * d + h * n_r * cr
            ys_ref[:, h * f:(h + 1) * f] = jnp.dot(hmid, wres_ref[slot, r0:r0 + n_r * cr, :],
                                                    preferred_element_type=F32)

        @pl.when(nxt >= 0)
        def _():
            convert(nxt, lo_ref[i], hi_ref[i], 1 - slot)

    @pl.when(i >= na_ref[0])
    def _():
        ys_ref[...] = jnp.zeros(ys_ref.shape, F32)


def _experts(xs, norm_w, tile_expert, n_active, cnt, starts, padded, w_g, w_u, w_d, *, tm):
    p, d = xs.shape
    ne, _, f = w_g.shape
    n_tiles = p // tm
    cr = W_CHUNK_ROWS
    assert d % cr == 0 and f % cr == 0 and d % f == 0
    n_gu, n_r, n_h = d // cr, f // cr, d // f
    n_chunks = 2 * n_gu + n_h * n_r

    ids = jnp.arange(ne, dtype=jnp.int32)
    present = cnt > 0
    later = jnp.where(present[None, :] & (ids[None, :] > ids[:, None]), ids[None, :], ne)
    nxt_e = jnp.min(later, axis=1)
    nxt_e = jnp.where(nxt_e >= ne, -1, nxt_e)
    run_e = jnp.cumsum(present.astype(jnp.int32)) - 1
    onehot = (tile_expert[:, None] == ids[None, :]).astype(jnp.int32)
    pick = lambda v: jnp.sum(onehot * v[None, :].astype(jnp.int32), axis=1)
    tiles = jnp.arange(n_tiles, dtype=jnp.int32)
    active = tiles < n_active[0]
    j = tiles - pick(starts) // tm
    k = jnp.maximum(pick(padded) // tm, 1)
    nxt_t = jnp.where(active, pick(nxt_e), -1).astype(jnp.int32)
    lo_t = jnp.where(active, (n_chunks * j) // k, 0).astype(jnp.int32)
    hi_t = jnp.where(active, (n_chunks * (j + 1)) // k, 0).astype(jnp.int32)
    slot_t = (pick(run_e) % 2).astype(jnp.int32)

    def rows(i, te, sl, nx, lo, hi, na):
        return (jnp.minimum(i, na[0] - 1), 0)

    grid_spec = pltpu.PrefetchScalarGridSpec(
        num_scalar_prefetch=6,
        grid=(n_tiles,),
        in_specs=[
            pl.BlockSpec((tm, d), rows),
            pl.BlockSpec((1, d), lambda i, *_: (0, 0)),
            pl.BlockSpec(memory_space=pl.ANY),
            pl.BlockSpec(memory_space=pl.ANY),
            pl.BlockSpec(memory_space=pl.ANY),
        ],
        out_specs=pl.BlockSpec((tm, d), lambda i, *_: (i, 0)),
        scratch_shapes=[
            pltpu.VMEM((2, n_chunks * cr, f), BF16),
            pltpu.VMEM((W_RING, cr, f), F32),
            pltpu.SemaphoreType.DMA((W_RING,)),
        ],
    )
    return pl.pallas_call(
        functools.partial(_experts_kernel, n_gu=n_gu, n_r=n_r, n_h=n_h),
        grid_spec=grid_spec,
        out_shape=jax.ShapeDtypeStruct((p, d), F32),
        compiler_params=_params("arbitrary"),
        name="experts",
    )(tile_expert, slot_t, nxt_t, lo_t, hi_t, n_active, xs, norm_w.reshape(1, d), w_g, w_u, w_d)


def _combine_kernel(pos_ref, pos_next_ref, ys_ref, x1_ref, gcol_ref, nw_ref, out_ref, buf_ref, sem,
                    *, final):
    i = pl.program_id(0)
    n_steps = pl.num_programs(0)
    tm = x1_ref.shape[0]

    def issue(p_ref, slot):
        def start(tb, carry):
            for u in range(ROW_DMA_UNROLL):
                t = tb * ROW_DMA_UNROLL + u
                for k in range(EXPERT_TOP_K):
                    src = p_ref[0, 0, k * tm + t]
                    pltpu.make_async_copy(ys_ref.at[pl.ds(src, 1), :],
                                          buf_ref.at[slot, k, pl.ds(t, 1), :],
                                          sem.at[slot]).start(priority=(u + k) % 2)
            return carry

        lax.fori_loop(0, tm // ROW_DMA_UNROLL, start, 0)

    slot = i % 2

    @pl.when(i == 0)
    def _():
        issue(pos_ref, 0)

    @pl.when(i + 1 < n_steps)
    def _():
        issue(pos_next_ref, 1 - slot)

    for k in range(EXPERT_TOP_K):
        pltpu.make_async_copy(ys_ref.at[pl.ds(0, tm), :], buf_ref.at[slot, k], sem.at[slot]).wait()
    g = gcol_ref[...]
    xo = x1_ref[...] + g[:, 0:1] * buf_ref[slot, 0] + g[:, 1:2] * buf_ref[slot, 1]
    if final:
        ms = jnp.mean(xo * xo, axis=-1, keepdims=True)
        xo = xo * lax.rsqrt(ms + NORM_EPS) * nw_ref[...]
    out_ref[...] = xo


def _combine(ys, pos_tiles, x1, gcol, norm_w, *, tm, final):
    n, d = x1.shape
    last = n // tm - 1
    return pl.pallas_call(
        functools.partial(_combine_kernel, final=final),
        grid=(n // tm,),
        in_specs=[
            pl.BlockSpec((1, 1, EXPERT_TOP_K * tm), lambda i: (i, 0, 0), memory_space=pltpu.SMEM),
            pl.BlockSpec((1, 1, EXPERT_TOP_K * tm), lambda i: (jnp.minimum(i + 1, last), 0, 0),
                         memory_space=pltpu.SMEM),
            pl.BlockSpec(memory_space=pl.ANY),
            pl.BlockSpec((tm, d), lambda i: (i, 0)),
            pl.BlockSpec((tm, LANES), lambda i: (i, 0)),
            pl.BlockSpec((1, d), lambda i: (0, 0)),
        ],
        out_specs=pl.BlockSpec((tm, d), lambda i: (i, 0)),
        out_shape=jax.ShapeDtypeStruct((n, d), F32),
        scratch_shapes=[pltpu.VMEM((2, EXPERT_TOP_K, tm, d), F32), pltpu.SemaphoreType.DMA((2,))],
        compiler_params=_params("arbitrary"),
        name="combine",
    )(pos_tiles, pos_tiles, ys, x1, gcol, norm_w.reshape(1, d))


def _tiles(n):
    return dict(proj_tm=math.gcd(n, 1024), mix_tm=math.gcd(n, 512), moe_tm=256,
                route_tm=math.gcd(n, 512))


def _layer(x2, batch, seq, layer, attn_norm_w, w_in_stack, b_gate, conv_w, conv_b, dt_bias, a_log,
           d_skip, ssd_norm_w, w_ssd_out, w_attn_out, w_out, ffn_norm_w, w_gr, b_gr, w_er, b_er,
           w_g, w_u, w_d):
    n, d = x2.shape
    n_heads = dt_bias.shape[0]
    d_inner = ssd_norm_w.shape[0]
    conv_dim = conv_w.shape[1]
    aw_total = ATTN_HEADS_PER_GROUP * len(DILATION_PATTERNS) * ATTN_HEAD_DIM
    gw = ATTN_HEADS_PER_GROUP * ATTN_HEAD_DIM
    tiles = _tiles(n)

    c_z, c_xbc, c_dt = d_inner, d_inner + conv_dim, d_inner + conv_dim + n_heads
    c_gate = c_dt + QKV_PARTS * aw_total
    segments = [(c_gate, 2 * d), (c_z, conv_dim), (0, d_inner)]
    segments += [(c_dt + p * aw_total + gi * gw, gw)
                 for gi in range(len(DILATION_PATTERNS)) for p in range(QKV_PARTS)]
    assert all(width % gw == 0 for _, width in segments)
    starts = tuple(start + b * gw for start, width in segments for b in range(width // gw))
    w_main, w_dt = _regroup_weight(w_in_stack, layer, starts, gw, c_xbc, n_heads)
    off_gate, off_xbc, off_z = 0, 2 * d, 2 * d + conv_dim
    plain_cols = off_z + d_inner
    assert off_xbc % conv_dim == 0 and off_z % d_inner == 0 and plain_cols % gw == 0

    proj, qkv0, qkv1, qkv2, dt, dtt = _in_proj(x2, attn_norm_w, w_main, w_dt, tm=tiles["proj_tm"],
                                               tn=gw, tp=4 * gw, plain_cols=plain_cols, nh=n_heads)

    yn = _ssd(proj, dt, dtt, conv_w, conv_b, dt_bias, a_log, d_skip, ssd_norm_w, batch=batch,
              seq=seq, d_inner=d_inner, xbc_block=off_xbc // conv_dim, z_block=off_z // d_inner)

    outs, lses = [], []
    for gi, qkv in enumerate((qkv0, qkv1, qkv2)):
        o_g, lse_g = _attn_group(qkv, gi, batch=batch, seq=seq)
        outs.append(o_g)
        lses.append(lse_g)

    merged = _merge(yn, outs, lses, proj, w_ssd_out, w_attn_out, b_gate, tm=tiles["mix_tm"],
                    gate_block=off_gate // (2 * d))
    x1, eid, gcol, rank, counts = _route(merged, x2, w_out, ffn_norm_w, (w_gr, w_er),
                                         (b_gr, b_er), tm=tiles["mix_tm"])

    tme = tiles["moe_tm"]
    cnt = counts[:, 0]
    padded = ((cnt + tme - 1) // tme) * tme
    ends = jnp.cumsum(padded)
    starts = ends - padded
    experts = jnp.arange(N_EXPERTS, dtype=jnp.int32)[:, None, None]
    pos = rank[:EXPERT_TOP_K] + jnp.sum(
        jnp.where(eid[None, :EXPERT_TOP_K] == experts, starts[:, None, None], 0), axis=0)
    n_tiles = EXPERT_TOP_K * n // tme + N_EXPERTS
    tile_start = jnp.arange(n_tiles, dtype=jnp.int32) * tme
    tile_expert = jnp.minimum(jnp.sum(ends[None, :] <= tile_start[:, None], axis=1),
                              N_EXPERTS - 1).astype(jnp.int32)
    n_active = (ends[-1:] // tme).astype(jnp.int32)

    rtm = tiles["route_tm"]
    pos_tiles = pos.reshape(EXPERT_TOP_K, n // rtm, rtm).transpose(1, 0, 2).reshape(
        n // rtm, 1, EXPERT_TOP_K * rtm)
    xs = _scatter(x1, pos_tiles, (starts + cnt).astype(jnp.int32), (padded - cnt).astype(jnp.int32),
                  n_active, tm=rtm, tme=tme, n_tiles=n_tiles)
    ys = _experts(xs, ffn_norm_w, tile_expert, n_active, cnt, starts, padded, w_g, w_u, w_d, tm=tme)
    return ys, pos_tiles, x1, gcol, rtm


def kernel(x, attn_norm_w, w_in, b_gate, conv_w, conv_b, dt_bias, a_log, d_skip, ssd_norm_w,
           w_ssd_out, w_attn_out, w_out, ffn_norm_w, w_group_router, b_group_router,
           w_expert_router, b_expert_router, w_exp_gate, w_exp_up, w_exp_down, final_norm_w):
    batch, seq, d = x.shape
    depth = w_in.shape[0]
    x2 = x.reshape(batch * seq, d)
    for layer in range(depth):
        ys, pos_tiles, x1, gcol, rtm = _layer(
            x2, batch, seq, layer, attn_norm_w[layer], w_in, b_gate[layer], conv_w[layer],
            conv_b[layer], dt_bias[layer], a_log[layer], d_skip[layer], ssd_norm_w[layer],
            w_ssd_out[layer], w_attn_out[layer], w_out[layer], ffn_norm_w[layer],
            w_group_router[layer], b_group_router[layer], w_expert_router[layer],
            b_expert_router[layer], w_exp_gate[layer], w_exp_up[layer], w_exp_down[layer])
        x2 = _combine(ys, pos_tiles, x1, gcol, final_norm_w, tm=rtm, final=layer == depth - 1)
    return x2.reshape(batch, seq, d)
```

```python
import functools
import math

import jax
import jax.numpy as jnp
import numpy as np
from jax import lax
from jax.experimental import pallas as pl
from jax.experimental.pallas import tpu as pltpu

F32 = jnp.float32
BF16 = jnp.bfloat16

NORM_EPS = 1e-6
SSD_HEAD_DIM = 64
SSD_N_GROUPS = 8
SSD_D_STATE = 128
SSD_CONV_WIDTH = 4
SSD_CHUNK = 128
SSD_CHUNKS_PER_STEP = 4
ATTN_HEAD_DIM = 128
DILATION_PATTERNS = ((128, 1), (512, 4), (2048, 16))
ATTN_HEADS_PER_GROUP = 4
ATTN_BLOCK = 128
ATTN_BLOCKS_PER_STEP = 8
N_EXPERT_GROUPS = 4
EXPERTS_PER_GROUP = 8
N_EXPERTS = N_EXPERT_GROUPS * EXPERTS_PER_GROUP
EXPERT_TOP_K = 2

LANES = 128
SUBLANES = 8
VMEM_LIMIT_BYTES = 56 * 1024 * 1024
ROW_DMA_UNROLL = 8

ROUTER_EXPERT_ROW0 = SUBLANES
ROUTER_ROWS = ROUTER_EXPERT_ROW0 + N_EXPERTS


def _params(*semantics):
    return pltpu.CompilerParams(dimension_semantics=semantics, vmem_limit_bytes=VMEM_LIMIT_BYTES)


def _split_bf16(v):
    hi = v.astype(BF16)
    lo = (v - hi.astype(F32)).astype(BF16)
    return hi, lo


def _sigmoid(v):
    return 0.5 + 0.5 * jnp.tanh(0.5 * v)


def _silu(v):
    h = 0.5 * v
    return h + h * jnp.tanh(h)


def _regroup_weight_kernel(starts_ref, wt_hbm, out_ref, narrow_ref, buf_ref, nbuf_ref, sem, nsem,
                           *, layer, narrow_start, nh):
    i = pl.program_id(0)
    n_steps = pl.num_programs(0)
    tn = buf_ref.shape[1]

    def fetch(step, slot):
        rows = pl.ds(pl.multiple_of(starts_ref[step], SUBLANES), tn)
        return pltpu.make_async_copy(wt_hbm.at[layer, rows, :], buf_ref.at[slot], sem.at[slot])

    narrow = pltpu.make_async_copy(wt_hbm.at[layer, pl.ds(narrow_start, LANES), :], nbuf_ref, nsem)

    @pl.when(i == 0)
    def _():
        fetch(0, 0).start()
        narrow.start()

    @pl.when(i + 1 < n_steps)
    def _():
        fetch(i + 1, (i + 1) % 2).start()

    fetch(i, i % 2).wait()
    out_ref[...] = buf_ref[i % 2].T.astype(out_ref.dtype)

    @pl.when(i == n_steps - 1)
    def _():
        narrow.wait()
        lane = lax.broadcasted_iota(jnp.int32, narrow_ref.shape, 1)
        narrow_ref[...] = jnp.where(lane < nh, nbuf_ref[...].T, 0.0).astype(narrow_ref.dtype)


def _regroup_weight(w_stack, layer, starts, tn, narrow_start, nh):
    _, k, cols = w_stack.shape
    assert all(s % SUBLANES == 0 and s + tn <= cols for s in starts)
    assert narrow_start % SUBLANES == 0 and narrow_start + LANES <= cols and nh <= LANES
    wt = jnp.swapaxes(w_stack, 1, 2)
    grid_spec = pltpu.PrefetchScalarGridSpec(
        num_scalar_prefetch=1,
        grid=(len(starts),),
        in_specs=[pl.BlockSpec(memory_space=pl.ANY)],
        out_specs=[pl.BlockSpec((k, tn), lambda i, *_: (0, i)),
                   pl.BlockSpec((k, LANES), lambda i, *_: (0, 0))],
        scratch_shapes=[pltpu.VMEM((2, tn, k), F32), pltpu.VMEM((LANES, k), F32),
                        pltpu.SemaphoreType.DMA((2,)), pltpu.SemaphoreType.DMA(())],
    )
    return pl.pallas_call(
        functools.partial(_regroup_weight_kernel, layer=layer, narrow_start=narrow_start, nh=nh),
        grid_spec=grid_spec,
        out_shape=[jax.ShapeDtypeStruct((k, len(starts) * tn), BF16),
                   jax.ShapeDtypeStruct((k, LANES), BF16)],
        compiler_params=_params("arbitrary"),
        name="regroup_weight",
    )(jnp.asarray(starts, jnp.int32), wt)


QKV_PARTS = 3
RELAYOUT_STRIDE = 4


def _in_proj_kernel(x_hbm, nw_ref, wp_ref, w_ref, wdt_ref, proj_ref, a0_ref, a1_ref, a2_ref,
                    dt_ref, dtt_ref, h_ref, stage_ref, x_ref, xsem, *, n_plain):
    i = pl.program_id(0)
    j = pl.program_id(1)
    tm = proj_ref.shape[0]
    tn = w_ref.shape[1]

    def fetch_x(tile):
        return pltpu.make_async_copy(x_hbm.at[pl.ds(pl.multiple_of(tile * tm, tm), tm), :], x_ref, xsem)

    @pl.when((i == 0) & (j == 0))
    def _():
        fetch_x(0).start()

    @pl.when(j == 0)
    def _():
        fetch_x(i).wait()
        xf = x_ref[...]
        ms = jnp.mean(xf * xf, axis=-1, keepdims=True)
        h = (xf * lax.rsqrt(ms + NORM_EPS) * nw_ref[...]).astype(BF16)
        h_ref[...] = h
        nh = dt_ref.shape[1]
        dt_wide = jnp.dot(h, wdt_ref[...], preferred_element_type=F32)
        dt_ref[...] = dt_wide[:, :nh]
        dtt_ref[...] = dt_wide.T[:nh, :]

        @pl.when(i + 1 < pl.num_programs(0))
        def _():
            fetch_x(i + 1).start()

    @pl.when(j < n_plain)
    def _():
        proj_ref[...] = jnp.dot(h_ref[...], wp_ref[...], preferred_element_type=F32).astype(BF16)

    for gi, a_ref in enumerate((a0_ref, a1_ref, a2_ref)):
        dil = DILATION_PATTERNS[gi][1]
        j0 = n_plain + QKV_PARTS * gi

        @pl.when((j >= j0) & (j < j0 + QKV_PARTS))
        def _(a_ref=a_ref, dil=dil):
            res = jnp.dot(h_ref[...], w_ref[...], preferred_element_type=F32)
            if dil == 1:
                a_ref[...] = res.astype(BF16)
            else:
                two_pass = dil > RELAYOUT_STRIDE and dil % RELAYOUT_STRIDE == 0
                quarter = tm // RELAYOUT_STRIDE
                for s in range(tn // LANES):
                    stage_ref[0] = res[:, s * LANES:(s + 1) * LANES]
                    if two_pass:
                        for r1 in range(RELAYOUT_STRIDE):
                            stage_ref[1, r1 * quarter:(r1 + 1) * quarter, :] = stage_ref[
                                0, pl.ds(r1, quarter, stride=RELAYOUT_STRIDE), :]
                    for r in range(dil):
                        if two_pass:
                            rows = pl.ds((r % RELAYOUT_STRIDE) * quarter + r // RELAYOUT_STRIDE,
                                         tm // dil, stride=dil // RELAYOUT_STRIDE)
                        else:
                            rows = pl.ds(r, tm // dil, stride=dil)
                        c0 = r * tn + s * LANES
                        a_ref[:, c0:c0 + LANES] = stage_ref[int(two_pass), rows, :].astype(BF16)


def _in_proj(x2, norm_w, w_main, w_dt, *, tm, tn, tp, plain_cols, nh):
    n, d = x2.shape
    assert plain_cols % tp == 0 and plain_cols % tn == 0
    n_plain = plain_cols // tp
    qkv_block0 = plain_cols // tn
    n_blocks = n_plain + QKV_PARTS * len(DILATION_PATTERNS)
    assert w_main.shape[1] == plain_cols + QKV_PARTS * len(DILATION_PATTERNS) * tn

    def a_spec(gi):
        dil = DILATION_PATTERNS[gi][1]
        j0 = n_plain + QKV_PARTS * gi
        return pl.BlockSpec((tm // dil, dil * tn),
                            lambda i, j: (i, jnp.clip(j - j0, 0, QKV_PARTS - 1)))

    def a_shape(gi):
        dil = DILATION_PATTERNS[gi][1]
        return jax.ShapeDtypeStruct((n // dil, dil * QKV_PARTS * tn), BF16)

    return pl.pallas_call(
        functools.partial(_in_proj_kernel, n_plain=n_plain),
        grid=(n // tm, n_blocks),
        in_specs=[
            pl.BlockSpec(memory_space=pl.ANY),
            pl.BlockSpec((1, d), lambda i, j: (0, 0)),
            pl.BlockSpec((d, tp), lambda i, j: (0, jnp.minimum(j, n_plain - 1))),
            pl.BlockSpec((d, tn), lambda i, j: (0, qkv_block0 + jnp.maximum(j - n_plain, 0))),
            pl.BlockSpec((d, LANES), lambda i, j: (0, 0)),
        ],
        out_specs=[
            pl.BlockSpec((tm, tp), lambda i, j: (i, jnp.minimum(j, n_plain - 1))),
            a_spec(0), a_spec(1), a_spec(2),
            pl.BlockSpec((tm, nh), lambda i, j: (i, 0)),
            pl.BlockSpec((nh, tm), lambda i, j: (0, i)),
        ],
        out_shape=[
            jax.ShapeDtypeStruct((n, plain_cols), BF16),
            a_shape(0), a_shape(1), a_shape(2),
            jax.ShapeDtypeStruct((n, nh), F32),
            jax.ShapeDtypeStruct((nh, n), F32),
        ],
        scratch_shapes=[pltpu.VMEM((tm, d), BF16), pltpu.VMEM((2, tm, LANES), F32),
                        pltpu.VMEM((tm, d), x2.dtype), pltpu.SemaphoreType.DMA(())],
        compiler_params=_params("arbitrary", "arbitrary"),
        name="in_proj",
    )(x2, norm_w.reshape(1, d), w_main, w_main, w_dt)


LOG2E = math.log2(math.e)
CONV_ROW_PITCH = 2


def _ssd_kernel(xbc_ref, z_ref, dt_ref, dtt_ref, cw_ref, cb_ref, dtb_ref, dtbt_ref, alog_ref,
                alogt_ref, dskip_ref, nw_ref, expand_ref, out_ref,
                xbuf_ref, state_ref, y_ref, *, n_heads, d_inner):
    L = SSD_CHUNK
    P = SSD_HEAD_DIM
    NS = SSD_D_STATE
    G = SSD_N_GROUPS
    R = n_heads // G
    GW = R * P
    W = SSD_CONV_WIDTH
    RP = CONV_ROW_PITCH
    n_slabs = xbuf_ref.shape[0]
    c = pl.program_id(1)

    def rows(first, count):
        return pl.ds(first * RP, count, stride=RP)

    @pl.when(c == 0)
    def _():
        state_ref[...] = jnp.zeros(state_ref.shape, F32)
        for s in range(n_slabs):
            xbuf_ref[s, rows(0, SUBLANES), :] = jnp.zeros((SUBLANES, LANES), F32)

    n_rows = xbc_ref.shape[0]

    @pl.when(c > 0)
    def _():
        for s in range(n_slabs):
            xbuf_ref[s, rows(0, SUBLANES), :] = xbuf_ref[s, rows(n_rows, SUBLANES), :]

    for s in range(n_slabs):
        xbuf_ref[s, rows(SUBLANES, n_rows), :] = xbc_ref[:, s * LANES:(s + 1) * LANES].astype(F32)

    for sub in range(n_rows // L):
        _ssd_chunk(sub * L, xbuf_ref, z_ref, dt_ref, dtt_ref, cw_ref, cb_ref, dtb_ref, dtbt_ref,
                   alog_ref, alogt_ref, dskip_ref, nw_ref, expand_ref, out_ref, state_ref, y_ref,
                   n_heads=n_heads, d_inner=d_inner)


def _ssd_chunk(r0, xbuf_ref, z_ref, dt_ref, dtt_ref, cw_ref, cb_ref, dtb_ref, dtbt_ref, alog_ref,
               alogt_ref, dskip_ref, nw_ref, expand_ref, out_ref, state_ref, y_ref, *, n_heads,
               d_inner):
    L = SSD_CHUNK
    P = SSD_HEAD_DIM
    NS = SSD_D_STATE
    G = SSD_N_GROUPS
    R = n_heads // G
    GW = R * P
    W = SSD_CONV_WIDTH
    RP = CONV_ROW_PITCH
    tok = slice(r0, r0 + L)

    def rows(first, count):
        return pl.ds(first * RP, count, stride=RP)

    def conv(col0, width):
        slabs = []
        for s in range(col0 // LANES, (col0 + width) // LANES):
            acc = cb_ref[:, s * LANES:(s + 1) * LANES]
            for w in range(W):
                acc = acc + (cw_ref[w:w + 1, s * LANES:(s + 1) * LANES]
                             * xbuf_ref[s, rows(r0 + SUBLANES - (W - 1) + w, L), :])
            slabs.append(acc)
        return _silu(jnp.concatenate(slabs, axis=1))

    def softplus(v):
        return jnp.maximum(v, 0.0) + jnp.log1p(jnp.exp(-jnp.abs(v)))

    dt = softplus(dt_ref[tok, :] + dtb_ref[...])
    dtt = softplus(dtt_ref[:, tok] + dtbt_ref[...])
    da = dt * (-LOG2E * jnp.exp(alog_ref[...]))
    dat = dtt * (-LOG2E * jnp.exp(alogt_ref[...]))
    row = lax.broadcasted_iota(jnp.int32, (L, L), 0)
    col = lax.broadcasted_iota(jnp.int32, (L, L), 1)
    causal = row >= col
    tri = jnp.where(causal, 1.0, 0.0).astype(BF16)
    trit = jnp.where(row <= col, 1.0, 0.0).astype(BF16)

    def split3(v):
        a = v.astype(BF16)
        r1 = v - a.astype(F32)
        b = r1.astype(BF16)
        cc = (r1 - b.astype(F32)).astype(BF16)
        return a, b, cc

    a2 = sum(jnp.dot(tri, p, preferred_element_type=F32) for p in split3(da))
    a2t = sum(jnp.dot(p, trit, preferred_element_type=F32) for p in split3(dat))
    a2_last = a2[L - 1:L, :]

    expand = expand_ref[...]

    def expand_heads(v):
        hi, lo = _split_bf16(v)
        return jnp.dot(jnp.concatenate([hi, lo], axis=1), expand, preferred_element_type=F32)

    in_scale_e = expand_heads(dt * jnp.exp2(a2_last - a2))
    tail8 = jnp.concatenate([jnp.exp2(a2_last), dskip_ref[...],
                             jnp.zeros((SUBLANES - 2, n_heads), F32)], axis=0)
    out_e = expand_heads(jnp.concatenate([jnp.exp2(a2), tail8], axis=0))
    out_scale_e = out_e[0:L, :]
    chunk_decay_e = out_e[L:L + 1, :]
    dskip_e = out_e[L + 1:L + 2, :]

    first_head = lax.broadcasted_iota(jnp.int32, (L, LANES), 1) < P

    for g in range(G):
        x0 = g * GW
        xs = conv(x0, GW)
        bm = conv(d_inner + g * NS, NS).astype(BF16)
        cm = conv(d_inner + G * NS + g * NS, NS).astype(BF16)
        cb = lax.dot_general(cm, bm, (((1,), (1,)), ((), ())), preferred_element_type=F32)
        cb = jnp.where(causal, cb, 0.0)
        y_parts = []
        for pr in range(GW // LANES):
            wgts = []
            for q in range(LANES // P):
                h = g * R + pr * (LANES // P) + q
                seg = a2[:, h:h + 1] - a2t[h:h + 1, :]
                decay = jnp.exp2(jnp.minimum(seg, 0.0))
                wgts.append((cb * decay * dtt[h:h + 1, :]).astype(BF16))
            slab = xs[:, pr * LANES:(pr + 1) * LANES]
            rhs = jnp.concatenate([jnp.where(first_head, slab, 0.0).astype(BF16),
                                   jnp.where(first_head, 0.0, slab).astype(BF16)], axis=0)
            y_parts.append(jnp.dot(jnp.concatenate(wgts, axis=1), rhs, preferred_element_type=F32))
        y = jnp.concatenate(y_parts, axis=1) + xs * dskip_e[:, x0:x0 + GW]
        st = state_ref[:, x0:x0 + GW]
        y = y + (jnp.dot(cm, st.astype(BF16), preferred_element_type=F32)
                 * out_scale_e[:, x0:x0 + GW])
        y_ref[tok, x0:x0 + GW] = y
        xin = (xs * in_scale_e[:, x0:x0 + GW]).astype(BF16)
        st_new = lax.dot_general(bm, xin, (((0,), (0,)), ((), ())), preferred_element_type=F32)
        state_ref[:, x0:x0 + GW] = st * chunk_decay_e[:, x0:x0 + GW] + st_new

    yz = y_ref[tok, :] * _silu(z_ref[tok, :].astype(F32))
    ms = jnp.mean(yz * yz, axis=-1, keepdims=True)
    out_ref[tok, :] = (yz * lax.rsqrt(ms + NORM_EPS) * nw_ref[...]).astype(out_ref.dtype)


def _ssd(proj, dt, dtt, conv_w, conv_b, dt_bias, a_log, d_skip, norm_w, *, batch, seq, d_inner,
         xbc_block, z_block):
    n = batch * seq
    n_heads = dt.shape[1]
    L = SSD_CHUNK * math.gcd(seq // SSD_CHUNK, SSD_CHUNKS_PER_STEP)
    nc = seq // L
    conv_dim = conv_w.shape[1]
    expand = (np.arange(d_inner)[None, :] // SSD_HEAD_DIM == np.arange(n_heads)[:, None])
    expand = jnp.asarray(np.concatenate([expand, expand], axis=0), BF16)
    assert conv_dim % LANES == 0
    kern = functools.partial(_ssd_kernel, n_heads=n_heads, d_inner=d_inner)
    small = lambda shape: pl.BlockSpec(shape, lambda b, c: (0, 0))
    return pl.pallas_call(
        kern,
        grid=(batch, nc),
        in_specs=[
            pl.BlockSpec((L, conv_dim), lambda b, c: (b * nc + c, xbc_block)),
            pl.BlockSpec((L, d_inner), lambda b, c: (b * nc + c, z_block)),
            pl.BlockSpec((L, n_heads), lambda b, c: (b * nc + c, 0)),
            pl.BlockSpec((n_heads, L), lambda b, c: (0, b * nc + c)),
            small((SSD_CONV_WIDTH, conv_dim)),
            small((1, conv_dim)),
            small((1, n_heads)),
            small((n_heads, 1)),
            small((1, n_heads)),
            small((n_heads, 1)),
            small((1, n_heads)),
            small((1, d_inner)),
            small((2 * n_heads, d_inner)),
        ],
        out_specs=pl.BlockSpec((L, d_inner), lambda b, c: (b * nc + c, 0)),
        out_shape=jax.ShapeDtypeStruct((n, d_inner), BF16),
        scratch_shapes=[
            pltpu.VMEM((conv_dim // LANES, CONV_ROW_PITCH * (L + SUBLANES), LANES), F32),
            pltpu.VMEM((SSD_D_STATE, d_inner), F32),
            pltpu.VMEM((L, d_inner), F32),
        ],
        compiler_params=_params("arbitrary", "arbitrary"),
        name="ssd",
    )(proj, proj, dt, dtt, conv_w, conv_b.reshape(1, -1), dt_bias.reshape(1, -1),
      dt_bias.reshape(-1, 1), a_log.reshape(1, -1), a_log.reshape(-1, 1), d_skip.reshape(1, -1),
      norm_w.reshape(1, -1), expand)


def _attn_kernel(q_ref, k_ref, v_ref, o_ref, lse_ref, kp_ref, vp_ref, *, slopes, dilation, hops):
    nb = pl.program_id(2)
    BLK = ATTN_BLOCK
    E = ATTN_HEAD_DIM

    @pl.when(nb == 0)
    def _():
        kp_ref[...] = jnp.zeros(kp_ref.shape, kp_ref.dtype)
        vp_ref[...] = jnp.zeros(vp_ref.shape, vp_ref.dtype)

    scale = E ** -0.5
    nt = (((1,), (1,)), ((), ()))
    nh = len(slopes)
    nblk = q_ref.shape[0] // BLK
    nres = q_ref.shape[1] // (nh * E)
    q = q_ref[...]
    kk = jnp.concatenate([kp_ref[...], k_ref[...]], axis=0)
    vv = jnp.concatenate([vp_ref[...], v_ref[...]], axis=0)
    units = [(j, c) for c in range(nres * nh) for j in range(nblk)]
    cols = lambda c: slice(c * E, (c + 1) * E)
    rows = lambda j: slice(j * BLK, (j + 1) * BLK)
    qi = lax.broadcasted_iota(jnp.int32, (BLK, BLK), 0)
    ki = lax.broadcasted_iota(jnp.int32, (BLK, BLK), 1)
    rel_cur = qi - ki
    rel_prev = rel_cur + BLK
    first = rel_prev <= jnp.where(nb > 0, hops, -1)
    later = rel_prev <= hops
    ok_cur = jnp.concatenate([rel_cur >= 0 for _ in units], axis=0)
    ok_prev = jnp.concatenate([first if j == 0 else later for j, _ in units], axis=0)
    dist_cur = (rel_cur * dilation).astype(F32)
    dist_prev = (rel_prev * dilation).astype(F32)
    bias_cur = jnp.concatenate([slopes[c % nh] * dist_cur for _, c in units], axis=0)
    bias_prev = jnp.concatenate([slopes[c % nh] * dist_prev for _, c in units], axis=0)
    s_cur = jnp.concatenate(
        [lax.dot_general(q[rows(j), cols(h)], kk[rows(j + 1), cols(h)], nt,
                         preferred_element_type=F32) for j, h in units], axis=0)
    s_prev = jnp.concatenate(
        [lax.dot_general(q[rows(j), cols(h)], kk[rows(j), cols(h)], nt,
                         preferred_element_type=F32) for j, h in units], axis=0)
    l_cur = jnp.where(ok_cur, s_cur * scale - bias_cur, -jnp.inf)
    l_prev = jnp.where(ok_prev, s_prev * scale - bias_prev, -jnp.inf)
    m = jnp.max(jnp.maximum(l_cur, l_prev), axis=-1, keepdims=True)
    p_cur = jnp.exp(l_cur - m)
    p_prev = jnp.exp(l_prev - m)
    den = jnp.sum(p_cur + p_prev, axis=-1, keepdims=True)
    p_cur = p_cur.astype(BF16)
    p_prev = p_prev.astype(BF16)
    inv = 1.0 / den
    lse = m + jnp.log(den)
    lane = lax.broadcasted_iota(jnp.int32, (BLK, LANES), 1)
    for res in range(nres):
        for j in range(nblk):
            lse_tile = jnp.zeros((BLK, LANES), F32)
            for h in range(nh):
                c = res * nh + h
                u = rows(units.index((j, c)))
                acc = (jnp.dot(p_cur[u], vv[rows(j + 1), cols(c)], preferred_element_type=F32)
                       + jnp.dot(p_prev[u], vv[rows(j), cols(c)], preferred_element_type=F32))
                o_ref[rows(j), cols(c)] = (acc * inv[u]).astype(o_ref.dtype)
                lse_tile = jnp.where(lane == h, lse[u], lse_tile)
            lse_ref[rows(j), res * LANES:(res + 1) * LANES] = lse_tile
    kp_ref[...] = k_ref[rows(nblk - 1), :]
    vp_ref[...] = v_ref[rows(nblk - 1), :]


def _attn_group(qkv, gi, *, batch, seq):
    window, dilation = DILATION_PATTERNS[gi]
    hops = window // dilation
    n_heads_total = ATTN_HEADS_PER_GROUP * len(DILATION_PATTERNS)
    slopes = tuple(float(2.0 ** (-8.0 * (gi * ATTN_HEADS_PER_GROUP + h + 1) / n_heads_total))
                   for h in range(ATTN_HEADS_PER_GROUP))
    gw = ATTN_HEADS_PER_GROUP * ATTN_HEAD_DIM
    assert seq % (dilation * ATTN_BLOCK) == 0
    sub = seq // dilation
    blocks = math.gcd(sub // ATTN_BLOCK, ATTN_BLOCKS_PER_STEP)
    nres = math.gcd(dilation, ATTN_BLOCKS_PER_STEP // blocks)
    rows = ATTN_BLOCK * blocks
    nb = sub // rows
    nr = dilation // nres
    kern = functools.partial(_attn_kernel, slopes=slopes, dilation=dilation, hops=hops)

    def part(p):
        return pl.BlockSpec((rows, nres * gw), lambda b, r, n: (b * nb + n, p * nr + r))

    o, lse = pl.pallas_call(
        kern,
        grid=(batch, nr, nb),
        in_specs=[part(0), part(1), part(2)],
        out_specs=[
            pl.BlockSpec((rows, nres * gw), lambda b, r, n: (b * nb + n, r)),
            pl.BlockSpec((rows, nres * LANES), lambda b, r, n: (b * nb + n, r)),
        ],
        out_shape=[
            jax.ShapeDtypeStruct((batch * sub, dilation * gw), BF16),
            jax.ShapeDtypeStruct((batch * sub, dilation * LANES), F32),
        ],
        scratch_shapes=[pltpu.VMEM((ATTN_BLOCK, nres * gw), BF16),
                        pltpu.VMEM((ATTN_BLOCK, nres * gw), BF16)],
        compiler_params=_params("arbitrary", "arbitrary", "arbitrary"),
        name=f"attn_g{gi}",
    )(qkv, qkv, qkv)
    return o, lse


def _merge_kernel(yn_ref, o0_ref, o1_ref, o2_ref, l0_ref, l1_ref, l2_ref, gate_ref,
                  wssd_ref, wattn_ref, bg_ref, merged_ref, ostage_ref, lstage_ref, *, d_model):
    E = ATTN_HEAD_DIM
    tm = merged_ref.shape[0]
    n_slabs = o0_ref.shape[1] // LANES
    y_ssd = jnp.dot(yn_ref[...], wssd_ref[...], preferred_element_type=F32)

    def token_major(gi, o_ref, l_ref):
        dil = DILATION_PATTERNS[gi][1]
        if dil == 1:
            return o_ref[...].astype(F32), l_ref[...]
        for r in range(dil):
            rows = pl.ds(r, tm // dil, stride=dil)
            lstage_ref[gi - 1, rows, :] = l_ref[:, r * LANES:(r + 1) * LANES]
            for s in range(n_slabs):
                c0 = (r * n_slabs + s) * LANES
                ostage_ref[gi - 1, s, rows, :] = o_ref[:, c0:c0 + LANES].astype(F32)
        out = jnp.concatenate([ostage_ref[gi - 1, s] for s in range(n_slabs)], axis=1)
        return out, lstage_ref[gi - 1]

    (o0, l0), (o1, l1), (o2, l2) = (token_major(gi, o_ref, l_ref) for gi, (o_ref, l_ref) in
                                    enumerate(((o0_ref, l0_ref), (o1_ref, l1_ref), (o2_ref, l2_ref))))

    lm = jnp.maximum(jnp.maximum(l0, l1), l2)
    e0, e1, e2 = jnp.exp(l0 - lm), jnp.exp(l1 - lm), jnp.exp(l2 - lm)
    inv = 1.0 / (e0 + e1 + e2)
    parts = []
    for h in range(ATTN_HEADS_PER_GROUP):
        sl = slice(h * E, (h + 1) * E)
        parts.append((e0[:, h:h + 1] * inv[:, h:h + 1]) * o0[:, sl]
                     + (e1[:, h:h + 1] * inv[:, h:h + 1]) * o1[:, sl]
                     + (e2[:, h:h + 1] * inv[:, h:h + 1]) * o2[:, sl])
    o = jnp.concatenate(parts, axis=-1).astype(BF16)
    y_attn = jnp.dot(o, wattn_ref[...], preferred_element_type=F32)

    gates = _sigmoid(gate_ref[...].astype(F32) + bg_ref[...])
    merged_ref[...] = (gates[:, :d_model] * y_ssd + gates[:, d_model:] * y_attn).astype(BF16)


def _merge(yn, outs, lses, proj, w_ssd_out, w_attn_out, b_gate, *, tm, gate_block):
    n, d = yn.shape
    aw = outs[0].shape[1]
    dils = [dil for _, dil in DILATION_PATTERNS]
    assert dils[0] == 1 and all(tm % (dil * SUBLANES) == 0 for dil in dils)
    row = lambda w: pl.BlockSpec((tm, w), lambda i: (i, 0))
    grouped = lambda dil, w: pl.BlockSpec((tm // dil, dil * w), lambda i: (i, 0))
    const = lambda shape: pl.BlockSpec(shape, lambda i: (0, 0), pipeline_mode=pl.Buffered(1))
    return pl.pallas_call(
        functools.partial(_merge_kernel, d_model=d),
        grid=(n // tm,),
        in_specs=[
            row(d), *[grouped(dil, aw) for dil in dils], *[grouped(dil, LANES) for dil in dils],
            pl.BlockSpec((tm, 2 * d), lambda i: (i, gate_block)),
            const((d, d)), const((aw, d)), const((1, 2 * d)),
        ],
        out_specs=row(d),
        out_shape=jax.ShapeDtypeStruct((n, d), BF16),
        scratch_shapes=[pltpu.VMEM((len(dils) - 1, aw // LANES, tm, LANES), F32),
                        pltpu.VMEM((len(dils) - 1, tm, LANES), F32)],
        compiler_params=_params("arbitrary"),
        name="merge",
    )(yn, outs[0], outs[1], outs[2], lses[0], lses[1], lses[2], proj,
      w_ssd_out.astype(BF16), w_attn_out.astype(BF16), b_gate.reshape(1, -1))


def _route_kernel(merged_ref, x_ref, wout_ref, fnw_ref, wr_ref, rb_ref,
                  x1_ref, eid_ref, gcol_ref, rank_ref, cnt_ref, carry_ref):
    i = pl.program_id(0)
    tm = x_ref.shape[0]

    @pl.when(i == 0)
    def _():
        carry_ref[...] = jnp.zeros(carry_ref.shape, F32)

    x1 = x_ref[...] + jnp.dot(merged_ref[...], wout_ref[...], preferred_element_type=F32)
    x1_ref[...] = x1

    ms = jnp.mean(x1 * x1, axis=-1, keepdims=True)
    hn = x1 * lax.rsqrt(ms + NORM_EPS) * fnw_ref[...]

    hn_hi, hn_lo = _split_bf16(hn)
    by_hi = jnp.dot(hn_hi, wr_ref[...], preferred_element_type=F32)
    by_lo = jnp.dot(hn_lo, wr_ref[:, 0:LANES], preferred_element_type=F32)
    logits_tok = by_hi[:, 0:LANES] + by_hi[:, LANES:] + by_lo
    logits = logits_tok.T[0:ROUTER_ROWS, :] + rb_ref[...]

    grow = lax.broadcasted_iota(jnp.int32, (SUBLANES, tm), 0)
    gl = jnp.where(grow < N_EXPERT_GROUPS, logits[0:SUBLANES, :], -jnp.inf)
    gmax = jnp.max(gl, axis=0, keepdims=True)
    gidx = jnp.min(jnp.where(gl == gmax, grow, N_EXPERT_GROUPS), axis=0, keepdims=True)
    group_gate = 1.0 / jnp.sum(jnp.exp(gl - gmax), axis=0, keepdims=True)

    in_group = jnp.zeros((EXPERTS_PER_GROUP, tm), F32)
    for g in range(N_EXPERT_GROUPS):
        r0 = ROUTER_EXPERT_ROW0 + g * EXPERTS_PER_GROUP
        in_group = jnp.where(gidx == g, logits[r0:r0 + EXPERTS_PER_GROUP, :], in_group)
    erow = lax.broadcasted_iota(jnp.int32, in_group.shape, 0)
    v1 = jnp.max(in_group, axis=0, keepdims=True)
    i1 = jnp.min(jnp.where(in_group == v1, erow, EXPERTS_PER_GROUP), axis=0, keepdims=True)
    rest = jnp.where(erow == i1, -jnp.inf, in_group)
    v2 = jnp.max(rest, axis=0, keepdims=True)
    i2 = jnp.min(jnp.where(rest == v2, erow, EXPERTS_PER_GROUP), axis=0, keepdims=True)
    t = jnp.exp(v2 - v1)
    g1 = group_gate / (1.0 + t)
    g2 = group_gate * t / (1.0 + t)
    eid1 = gidx * EXPERTS_PER_GROUP + i1
    eid2 = gidx * EXPERTS_PER_GROUP + i2
    slot = lax.broadcasted_iota(jnp.int32, (SUBLANES, tm), 0)
    eid_ref[...] = jnp.where(slot == 0, eid1, jnp.where(slot == 1, eid2, 0))

    grow8 = lax.broadcasted_iota(jnp.int32, (LANES, tm), 0)
    gt = jnp.where(grow8 == 0, g1, jnp.where(grow8 == 1, g2, 0.0))
    gcol_ref[...] = gt.T

    xrow = lax.broadcasted_iota(jnp.int32, (N_EXPERTS, tm), 0)
    oh1 = xrow == eid1
    oh2 = xrow == eid2
    oh = jnp.where(oh1 | oh2, 1.0, 0.0)
    ti = lax.broadcasted_iota(jnp.int32, (tm, tm), 0)
    tj = lax.broadcasted_iota(jnp.int32, (tm, tm), 1)
    before = jnp.where(ti < tj, 1.0, 0.0).astype(BF16)
    prior = jnp.dot(oh.astype(BF16), before, preferred_element_type=F32) + carry_ref[:, 0:1]
    r1 = jnp.sum(jnp.where(oh1, prior, 0.0), axis=0, keepdims=True)
    r2 = jnp.sum(jnp.where(oh2, prior, 0.0), axis=0, keepdims=True)
    rank_ref[...] = jnp.where(slot == 0, r1, jnp.where(slot == 1, r2, 0.0)).astype(jnp.int32)
    carry_ref[...] = carry_ref[...] + jnp.sum(oh, axis=1, keepdims=True)
    cnt_ref[...] = carry_ref[...].astype(jnp.int32)


def _route(merged, x2, w_out, ffn_norm_w, wr, rb, *, tm):
    n, d = x2.shape
    wrc = jnp.zeros((d, LANES), F32)
    wrc = wrc.at[:, 0:N_EXPERT_GROUPS].set(wr[0]).at[:, ROUTER_EXPERT_ROW0:ROUTER_ROWS].set(wr[1])
    rbc = jnp.zeros((ROUTER_ROWS, 1), F32)
    rbc = rbc.at[0:N_EXPERT_GROUPS, 0].set(rb[0]).at[ROUTER_EXPERT_ROW0:, 0].set(rb[1])
    wr_hi_lo = jnp.concatenate(_split_bf16(wrc), axis=1)
    row = lambda w: pl.BlockSpec((tm, w), lambda i: (i, 0))
    const = lambda shape: pl.BlockSpec(shape, lambda i: (0, 0), pipeline_mode=pl.Buffered(1))
    return pl.pallas_call(
        _route_kernel,
        grid=(n // tm,),
        in_specs=[
            row(d), row(d),
            const((d, d)), const((1, d)),
            const((d, 2 * LANES)), const((ROUTER_ROWS, 1)),
        ],
        out_specs=[
            row(d),
            pl.BlockSpec((SUBLANES, tm), lambda i: (0, i)),
            row(LANES),
            pl.BlockSpec((SUBLANES, tm), lambda i: (0, i)),
            pl.BlockSpec((N_EXPERTS, LANES), lambda i: (0, 0)),
        ],
        out_shape=[
            jax.ShapeDtypeStruct((n, d), F32),
            jax.ShapeDtypeStruct((SUBLANES, n), jnp.int32),
            jax.ShapeDtypeStruct((n, LANES), F32),
            jax.ShapeDtypeStruct((SUBLANES, n), jnp.int32),
            jax.ShapeDtypeStruct((N_EXPERTS, LANES), jnp.int32),
        ],
        scratch_shapes=[pltpu.VMEM((N_EXPERTS, LANES), F32)],
        compiler_params=_params("arbitrary"),
        name="route",
    )(merged, x2, w_out.astype(BF16), ffn_norm_w.reshape(1, -1), wr_hi_lo, rbc)


def _scatter_kernel(pad_start_ref, pad_len_ref, n_active_ref, pos_ref, hn_ref, xs_ref, zero_ref,
                    sem, zsem, *, tme, n_tiles):
    tm = hn_ref.shape[0]
    zrows = zero_ref.shape[0]

    @pl.when(pl.program_id(0) == 0)
    def _():
        zero_ref[...] = jnp.zeros(zero_ref.shape, zero_ref.dtype)

        def fills(act):
            def per_expert(e, carry):
                off = pad_start_ref[e]
                left = pad_len_ref[e]
                head = left & (SUBLANES - 1)
                for r in range(SUBLANES - 1):
                    @pl.when(r < head)
                    def _(r=r):
                        act(pltpu.make_async_copy(zero_ref.at[pl.ds(0, 1), :],
                                                  xs_ref.at[pl.ds(off + r, 1), :], zsem))

                off = off + head
                bit = zrows
                while bit >= SUBLANES:
                    take = left & bit

                    @pl.when(take != 0)
                    def _(off=off, bit=bit):
                        act(pltpu.make_async_copy(
                            zero_ref.at[pl.ds(0, bit), :],
                            xs_ref.at[pl.ds(pl.multiple_of(off, SUBLANES), bit), :], zsem))

                    off = off + take
                    bit //= 2
                return carry

            def per_tile(t, carry):
                @pl.when(t >= n_active_ref[0])
                def _():
                    for part in range(tme // zrows):
                        row0 = pl.multiple_of(t * tme + part * zrows, zrows)
                        act(pltpu.make_async_copy(zero_ref, xs_ref.at[pl.ds(row0, zrows), :], zsem))

                return carry

            lax.fori_loop(0, N_EXPERTS, per_expert, 0)
            lax.fori_loop(0, n_tiles, per_tile, 0)

        fills(lambda cp: cp.start())
        fills(lambda cp: cp.wait())

    def start(tb, carry):
        for u in range(ROW_DMA_UNROLL):
            t = tb * ROW_DMA_UNROLL + u
            for k in range(EXPERT_TOP_K):
                dst = pos_ref[0, 0, k * tm + t]
                pltpu.make_async_copy(hn_ref.at[pl.ds(t, 1), :], xs_ref.at[pl.ds(dst, 1), :],
                                      sem).start(priority=(u + k) % 2)
        return carry

    lax.fori_loop(0, tm // ROW_DMA_UNROLL, start, 0)
    for k in range(EXPERT_TOP_K):
        pltpu.make_async_copy(hn_ref, xs_ref.at[pl.ds(0, tm), :], sem).wait()


def _scatter(hn, pos_tiles, pad_start, pad_len, n_active, *, tm, tme, n_tiles):
    n, d = hn.shape
    assert tme % 2 == 0 and (tme // 2) & (tme // 2 - 1) == 0
    grid_spec = pltpu.PrefetchScalarGridSpec(
        num_scalar_prefetch=3,
        grid=(n // tm,),
        in_specs=[
            pl.BlockSpec((1, 1, EXPERT_TOP_K * tm), lambda i, *_: (i, 0, 0),
                         memory_space=pltpu.SMEM),
            pl.BlockSpec((tm, d), lambda i, *_: (i, 0)),
        ],
        out_specs=pl.BlockSpec(memory_space=pl.ANY),
        scratch_shapes=[pltpu.VMEM((tme // 2, d), hn.dtype), pltpu.SemaphoreType.DMA(()),
                        pltpu.SemaphoreType.DMA(())],
    )
    return pl.pallas_call(
        functools.partial(_scatter_kernel, tme=tme, n_tiles=n_tiles),
        grid_spec=grid_spec,
        out_shape=jax.ShapeDtypeStruct((n_tiles * tme, d), hn.dtype),
        compiler_params=_params("arbitrary"),
        name="scatter",
    )(pad_start, pad_len, n_active, pos_tiles, hn)


W_CHUNK_ROWS = 256
W_RING = 8


def _experts_kernel(te_ref, slot_ref, nxt_ref, lo_ref, hi_ref, na_ref, xs_ref, fnw_ref, wg_hbm,
                    wu_hbm, wd_hbm, ys_ref, wres_ref, stage_ref, sem, *, n_gu, n_r, n_h):
    i = pl.program_id(0)
    n_chunks = 2 * n_gu + n_h * n_r
    ring, cr, f = stage_ref.shape
    d = n_gu * cr

    def stage_copy(src, c):
        return pltpu.make_async_copy(src, stage_ref.at[c % ring], sem.at[c % ring])

    def start_chunk(e, c):
        @pl.when(c < n_gu)
        def _():
            stage_copy(wg_hbm.at[e, pl.ds(pl.multiple_of(c * cr, cr), cr), :], c).start()

        @pl.when((c >= n_gu) & (c < 2 * n_gu))
        def _():
            stage_copy(wu_hbm.at[e, pl.ds(pl.multiple_of((c - n_gu) * cr, cr), cr), :], c).start()

        @pl.when(c >= 2 * n_gu)
        def _():
            q = c - 2 * n_gu
            h = q // n_r
            r = q - h * n_r
            stage_copy(wd_hbm.at[e, pl.ds(pl.multiple_of(r * cr, cr), cr),
                                 pl.ds(pl.multiple_of(h * f, f), f)], c).start()

    def start_first(e):
        def body(c, carry):
            start_chunk(e, c)
            return carry

        lax.fori_loop(0, ring, body, 0)

    def convert(e, lo, hi, dst_slot):
        def body(c, carry):
            stage_copy(wg_hbm.at[0, pl.ds(0, cr), :], c).wait()
            wres_ref[dst_slot, pl.ds(pl.multiple_of(c * cr, cr), cr), :] = (
                stage_ref[c % ring].astype(BF16))

            @pl.when(c + ring < n_chunks)
            def _():
                start_chunk(e, c + ring)

            return carry

        lax.fori_loop(lo, hi, body, 0)

    @pl.when(i == 0)
    def _():
        start_first(te_ref[0])
        convert(te_ref[0], 0, n_chunks, slot_ref[0])

    @pl.when(i < na_ref[0])
    def _():
        slot = slot_ref[i]
        nxt = nxt_ref[i]

        @pl.when((nxt >= 0) & (lo_ref[i] == 0))
        def _():
            start_first(nxt)

        xr = xs_ref[...]
        ms = jnp.mean(xr * xr, axis=-1, keepdims=True)
        xb = (xr * lax.rsqrt(ms + NORM_EPS) * fnw_ref[...]).astype(BF16)
        hg = jnp.dot(xb, wres_ref[slot, 0:d, :], preferred_element_type=F32)
        hu = jnp.dot(xb, wres_ref[slot, d:2 * d, :], preferred_element_type=F32)
        hmid = (_silu(hg) * hu).astype(BF16)
        for h in range(n_h):
            r0 = 2 * d + h * n_r * cr
            ys_ref[:, h * f:(h + 1) * f] = jnp.dot(hmid, wres_ref[slot, r0:r0 + n_r * cr, :],
                                                    preferred_element_type=F32)

        @pl.when(nxt >= 0)
        def _():
            convert(nxt, lo_ref[i], hi_ref[i], 1 - slot)

    @pl.when(i >= na_ref[0])
    def _():
        ys_ref[...] = jnp.zeros(ys_ref.shape, F32)


def _experts(xs, norm_w, tile_expert, n_active, cnt, starts, padded, w_g, w_u, w_d, *, tm):
    p, d = xs.shape
    ne, _, f = w_g.shape
    n_tiles = p // tm
    cr = W_CHUNK_ROWS
    assert d % cr == 0 and f % cr == 0 and d % f == 0
    n_gu, n_r, n_h = d // cr, f // cr, d // f
    n_chunks = 2 * n_gu + n_h * n_r

    ids = jnp.arange(ne, dtype=jnp.int32)
    present = cnt > 0
    later = jnp.where(present[None, :] & (ids[None, :] > ids[:, None]), ids[None, :], ne)
    nxt_e = jnp.min(later, axis=1)
    nxt_e = jnp.where(nxt_e >= ne, -1, nxt_e)
    run_e = jnp.cumsum(present.astype(jnp.int32)) - 1
    onehot = (tile_expert[:, None] == ids[None, :]).astype(jnp.int32)
    pick = lambda v: jnp.sum(onehot * v[None, :].astype(jnp.int32), axis=1)
    tiles = jnp.arange(n_tiles, dtype=jnp.int32)
    active = tiles < n_active[0]
    j = tiles - pick(starts) // tm
    k = jnp.maximum(pick(padded) // tm, 1)
    nxt_t = jnp.where(active, pick(nxt_e), -1).astype(jnp.int32)
    lo_t = jnp.where(active, (n_chunks * j) // k, 0).astype(jnp.int32)
    hi_t = jnp.where(active, (n_chunks * (j + 1)) // k, 0).astype(jnp.int32)
    slot_t = (pick(run_e) % 2).astype(jnp.int32)

    def rows(i, te, sl, nx, lo, hi, na):
        return (jnp.minimum(i, na[0] - 1), 0)

    grid_spec = pltpu.PrefetchScalarGridSpec(
        num_scalar_prefetch=6,
        grid=(n_tiles,),
        in_specs=[
            pl.BlockSpec((tm, d), rows),
            pl.BlockSpec((1, d), lambda i, *_: (0, 0)),
            pl.BlockSpec(memory_space=pl.ANY),
            pl.BlockSpec(memory_space=pl.ANY),
            pl.BlockSpec(memory_space=pl.ANY),
        ],
        out_specs=pl.BlockSpec((tm, d), lambda i, *_: (i, 0)),
        scratch_shapes=[
            pltpu.VMEM((2, n_chunks * cr, f), BF16),
            pltpu.VMEM((W_RING, cr, f), F32),
            pltpu.SemaphoreType.DMA((W_RING,)),
        ],
    )
    return pl.pallas_call(
        functools.partial(_experts_kernel, n_gu=n_gu, n_r=n_r, n_h=n_h),
        grid_spec=grid_spec,
        out_shape=jax.ShapeDtypeStruct((p, d), F32),
        compiler_params=_params("arbitrary"),
        name="experts",
    )(tile_expert, slot_t, nxt_t, lo_t, hi_t, n_active, xs, norm_w.reshape(1, d), w_g, w_u, w_d)


def _combine_kernel(pos_ref, pos_next_ref, ys_ref, x1_ref, gcol_ref, nw_ref, out_ref, buf_ref, sem,
                    *, final):
    i = pl.program_id(0)
    n_steps = pl.num_programs(0)
    tm = x1_ref.shape[0]

    def issue(p_ref, slot):
        def start(tb, carry):
            for u in range(ROW_DMA_UNROLL):
                t = tb * ROW_DMA_UNROLL + u
                for k in range(EXPERT_TOP_K):
                    src = p_ref[0, 0, k * tm + t]
                    pltpu.make_async_copy(ys_ref.at[pl.ds(src, 1), :],
                                          buf_ref.at[slot, k, pl.ds(t, 1), :],
                                          sem.at[slot]).start(priority=(u + k) % 2)
            return carry

        lax.fori_loop(0, tm // ROW_DMA_UNROLL, start, 0)

    slot = i % 2

    @pl.when(i == 0)
    def _():
        issue(pos_ref, 0)

    @pl.when(i + 1 < n_steps)
    def _():
        issue(pos_next_ref, 1 - slot)

    for k in range(EXPERT_TOP_K):
        pltpu.make_async_copy(ys_ref.at[pl.ds(0, tm), :], buf_ref.at[slot, k], sem.at[slot]).wait()
    g = gcol_ref[...]
    xo = x1_ref[...] + g[:, 0:1] * buf_ref[slot, 0] + g[:, 1:2] * buf_ref[slot, 1]
    if final:
        ms = jnp.mean(xo * xo, axis=-1, keepdims=True)
        xo = xo * lax.rsqrt(ms + NORM_EPS) * nw_ref[...]
    out_ref[...] = xo


def _combine(ys, pos_tiles, x1, gcol, norm_w, *, tm, final):
    n, d = x1.shape
    last = n // tm - 1
    return pl.pallas_call(
        functools.partial(_combine_kernel, final=final),
        grid=(n // tm,),
        in_specs=[
            pl.BlockSpec((1, 1, EXPERT_TOP_K * tm), lambda i: (i, 0, 0), memory_space=pltpu.SMEM),
            pl.BlockSpec((1, 1, EXPERT_TOP_K * tm), lambda i: (jnp.minimum(i + 1, last), 0, 0),
                         memory_space=pltpu.SMEM),
            pl.BlockSpec(memory_space=pl.ANY),
            pl.BlockSpec((tm, d), lambda i: (i, 0)),
            pl.BlockSpec((tm, LANES), lambda i: (i, 0)),
            pl.BlockSpec((1, d), lambda i: (0, 0)),
        ],
        out_specs=pl.BlockSpec((tm, d), lambda i: (i, 0)),
        out_shape=jax.ShapeDtypeStruct((n, d), F32),
        scratch_shapes=[pltpu.VMEM((2, EXPERT_TOP_K, tm, d), F32), pltpu.SemaphoreType.DMA((2,))],
        compiler_params=_params("arbitrary"),
        name="combine",
    )(pos_tiles, pos_tiles, ys, x1, gcol, norm_w.reshape(1, d))


def _tiles(n):
    return dict(proj_tm=math.gcd(n, 1024), mix_tm=math.gcd(n, 512), moe_tm=256,
                route_tm=math.gcd(n, 512))


def _layer(x2, batch, seq, layer, attn_norm_w, w_in_stack, b_gate, conv_w, conv_b, dt_bias, a_log,
           d_skip, ssd_norm_w, w_ssd_out, w_attn_out, w_out, ffn_norm_w, w_gr, b_gr, w_er, b_er,
           w_g, w_u, w_d):
    n, d = x2.shape
    n_heads = dt_bias.shape[0]
    d_inner = ssd_norm_w.shape[0]
    conv_dim = conv_w.shape[1]
    aw_total = ATTN_HEADS_PER_GROUP * len(DILATION_PATTERNS) * ATTN_HEAD_DIM
    gw = ATTN_HEADS_PER_GROUP * ATTN_HEAD_DIM
    tiles = _tiles(n)

    c_z, c_xbc, c_dt = d_inner, d_inner + conv_dim, d_inner + conv_dim + n_heads
    c_gate = c_dt + QKV_PARTS * aw_total
    segments = [(c_gate, 2 * d), (c_z, conv_dim), (0, d_inner)]
    segments += [(c_dt + p * aw_total + gi * gw, gw)
                 for gi in range(len(DILATION_PATTERNS)) for p in range(QKV_PARTS)]
    assert all(width % gw == 0 for _, width in segments)
    starts = tuple(start + b * gw for start, width in segments for b in range(width // gw))
    w_main, w_dt = _regroup_weight(w_in_stack, layer, starts, gw, c_xbc, n_heads)
    off_gate, off_xbc, off_z = 0, 2 * d, 2 * d + conv_dim
    plain_cols = off_z + d_inner
    assert off_xbc % conv_dim == 0 and off_z % d_inner == 0 and plain_cols % gw == 0

    proj, qkv0, qkv1, qkv2, dt, dtt = _in_proj(x2, attn_norm_w, w_main, w_dt, tm=tiles["proj_tm"],
                                               tn=gw, tp=4 * gw, plain_cols=plain_cols, nh=n_heads)

    yn = _ssd(proj, dt, dtt, conv_w, conv_b, dt_bias, a_log, d_skip, ssd_norm_w, batch=batch,
              seq=seq, d_inner=d_inner, xbc_block=off_xbc // conv_dim, z_block=off_z // d_inner)

    outs, lses = [], []
    for gi, qkv in enumerate((qkv0, qkv1, qkv2)):
        o_g, lse_g = _attn_group(qkv, gi, batch=batch, seq=seq)
        outs.append(o_g)
        lses.append(lse_g)

    merged = _merge(yn, outs, lses, proj, w_ssd_out, w_attn_out, b_gate, tm=tiles["mix_tm"],
                    gate_block=off_gate // (2 * d))
    x1, eid, gcol, rank, counts = _route(merged, x2, w_out, ffn_norm_w, (w_gr, w_er),
                                         (b_gr, b_er), tm=tiles["mix_tm"])

    tme = tiles["moe_tm"]
    cnt = counts[:, 0]
    padded = ((cnt + tme - 1) // tme) * tme
    ends = jnp.cumsum(padded)
    starts = ends - padded
    experts = jnp.arange(N_EXPERTS, dtype=jnp.int32)[:, None, None]
    pos = rank[:EXPERT_TOP_K] + jnp.sum(
        jnp.where(eid[None, :EXPERT_TOP_K] == experts, starts[:, None, None], 0), axis=0)
    n_tiles = EXPERT_TOP_K * n // tme + N_EXPERTS
    tile_start = jnp.arange(n_tiles, dtype=jnp.int32) * tme
    tile_expert = jnp.minimum(jnp.sum(ends[None, :] <= tile_start[:, None], axis=1),
                              N_EXPERTS - 1).astype(jnp.int32)
    n_active = (ends[-1:] // tme).astype(jnp.int32)

    rtm = tiles["route_tm"]
    pos_tiles = pos.reshape(EXPERT_TOP_K, n // rtm, rtm).transpose(1, 0, 2).reshape(
        n // rtm, 1, EXPERT_TOP_K * rtm)
    xs = _scatter(x1, pos_tiles, (starts + cnt).astype(jnp.int32), (padded - cnt).astype(jnp.int32),
                  n_active, tm=rtm, tme=tme, n_tiles=n_tiles)
    ys = _experts(xs, ffn_norm_w, tile_expert, n_active, cnt, starts, padded, w_g, w_u, w_d, tm=tme)
    return ys, pos_tiles, x1, gcol, rtm


def kernel(x, attn_norm_w, w_in, b_gate, conv_w, conv_b, dt_bias, a_log, d_skip, ssd_norm_w,
           w_ssd_out, w_attn_out, w_out, ffn_norm_w, w_group_router, b_group_router,
           w_expert_router, b_expert_router, w_exp_gate, w_exp_up, w_exp_down, final_norm_w):
    batch, seq, d = x.shape
    depth = w_in.shape[0]
    x2 = x.reshape(batch * seq, d)
    for layer in range(depth):
        ys, pos_tiles, x1, gcol, rtm = _layer(
            x2, batch, seq, layer, attn_norm_w[layer], w_in, b_gate[layer], conv_w[layer],
            conv_b[layer], dt_bias[layer], a_log[layer], d_skip[layer], ssd_norm_w[layer],
            w_ssd_out[layer], w_attn_out[layer], w_out[layer], ffn_norm_w[layer],
            w_group_router[layer], b_group_router[layer], w_expert_router[layer],
            b_expert_router[layer], w_exp_gate[layer], w_exp_up[layer], w_exp_down[layer])
        x2 = _combine(ys, pos_tiles, x1, gcol, final_norm_w, tm=rtm, final=layer == depth - 1)
    return x2.reshape(batch, seq, d)
```

```python
import functools
import math

import jax
import jax.numpy as jnp
import numpy as np
from jax import lax
from jax.experimental import pallas as pl
from jax.experimental.pallas import tpu as pltpu

F32 = jnp.float32
BF16 = jnp.bfloat16

NORM_EPS = 1e-6
SSD_HEAD_DIM = 64
SSD_N_GROUPS = 8
SSD_D_STATE = 128
SSD_CONV_WIDTH = 4
SSD_CHUNK = 128
SSD_CHUNKS_PER_STEP = 4
ATTN_HEAD_DIM = 128
DILATION_PATTERNS = ((128, 1), (512, 4), (2048, 16))
ATTN_HEADS_PER_GROUP = 4
ATTN_BLOCK = 128
ATTN_BLOCKS_PER_STEP = 8
N_EXPERT_GROUPS = 4
EXPERTS_PER_GROUP = 8
N_EXPERTS = N_EXPERT_GROUPS * EXPERTS_PER_GROUP
EXPERT_TOP_K = 2

LANES = 128
SUBLANES = 8
VMEM_LIMIT_BYTES = 56 * 1024 * 1024
ROW_DMA_UNROLL = 512

ROUTER_EXPERT_ROW0 = SUBLANES
ROUTER_ROWS = ROUTER_EXPERT_ROW0 + N_EXPERTS


def _params(*semantics):
    return pltpu.CompilerParams(dimension_semantics=semantics, vmem_limit_bytes=VMEM_LIMIT_BYTES)


def _split_bf16(v):
    hi = v.astype(BF16)
    lo = (v - hi.astype(F32)).astype(BF16)
    return hi, lo


def _sigmoid(v):
    return 0.5 + 0.5 * jnp.tanh(0.5 * v)


def _silu(v):
    h = 0.5 * v
    return h + h * jnp.tanh(h)


def _regroup_weight_kernel(starts_ref, wt_hbm, out_ref, narrow_ref, buf_ref, nbuf_ref, sem, nsem,
                           *, layer, narrow_start, nh):
    i = pl.program_id(0)
    n_steps = pl.num_programs(0)
    tn = buf_ref.shape[1]

    def fetch(step, slot):
        rows = pl.ds(pl.multiple_of(starts_ref[step], SUBLANES), tn)
        return pltpu.make_async_copy(wt_hbm.at[layer, rows, :], buf_ref.at[slot], sem.at[slot])

    narrow = pltpu.make_async_copy(wt_hbm.at[layer, pl.ds(narrow_start, LANES), :], nbuf_ref, nsem)

    @pl.when(i == 0)
    def _():
        fetch(0, 0).start()
        narrow.start()

    @pl.when(i + 1 < n_steps)
    def _():
        fetch(i + 1, (i + 1) % 2).start()

    fetch(i, i % 2).wait()
    out_ref[...] = buf_ref[i % 2].T.astype(out_ref.dtype)

    @pl.when(i == n_steps - 1)
    def _():
        narrow.wait()
        lane = lax.broadcasted_iota(jnp.int32, narrow_ref.shape, 1)
        narrow_ref[...] = jnp.where(lane < nh, nbuf_ref[...].T, 0.0).astype(narrow_ref.dtype)


def _regroup_weight(w_stack, layer, starts, tn, narrow_start, nh):
    _, k, cols = w_stack.shape
    assert all(s % SUBLANES == 0 and s + tn <= cols for s in starts)
    assert narrow_start % SUBLANES == 0 and narrow_start + LANES <= cols and nh <= LANES
    wt = jnp.swapaxes(w_stack, 1, 2)
    grid_spec = pltpu.PrefetchScalarGridSpec(
        num_scalar_prefetch=1,
        grid=(len(starts),),
        in_specs=[pl.BlockSpec(memory_space=pl.ANY)],
        out_specs=[pl.BlockSpec((k, tn), lambda i, *_: (0, i)),
                   pl.BlockSpec((k, LANES), lambda i, *_: (0, 0))],
        scratch_shapes=[pltpu.VMEM((2, tn, k), F32), pltpu.VMEM((LANES, k), F32),
                        pltpu.SemaphoreType.DMA((2,)), pltpu.SemaphoreType.DMA(())],
    )
    return pl.pallas_call(
        functools.partial(_regroup_weight_kernel, layer=layer, narrow_start=narrow_start, nh=nh),
        grid_spec=grid_spec,
        out_shape=[jax.ShapeDtypeStruct((k, len(starts) * tn), BF16),
                   jax.ShapeDtypeStruct((k, LANES), BF16)],
        compiler_params=_params("arbitrary"),
        name="regroup_weight",
    )(jnp.asarray(starts, jnp.int32), wt)


QKV_PARTS = 3
RELAYOUT_STRIDE = 4


def _in_proj_kernel(x_hbm, nw_ref, wp_ref, w_ref, wdt_ref, proj_ref, a0_ref, a1_ref, a2_ref,
                    dt_ref, dtt_ref, h_ref, stage_ref, x_ref, xsem, *, n_plain):
    i = pl.program_id(0)
    j = pl.program_id(1)
    tm = proj_ref.shape[0]
    tn = w_ref.shape[1]

    def fetch_x(tile):
        return pltpu.make_async_copy(x_hbm.at[pl.ds(pl.multiple_of(tile * tm, tm), tm), :], x_ref, xsem)

    @pl.when((i == 0) & (j == 0))
    def _():
        fetch_x(0).start()

    @pl.when(j == 0)
    def _():
        fetch_x(i).wait()
        xf = x_ref[...]
        ms = jnp.mean(xf * xf, axis=-1, keepdims=True)
        h = (xf * lax.rsqrt(ms + NORM_EPS) * nw_ref[...]).astype(BF16)
        h_ref[...] = h
        nh = dt_ref.shape[1]
        dt_wide = jnp.dot(h, wdt_ref[...], preferred_element_type=F32)
        dt_ref[...] = dt_wide[:, :nh]
        dtt_ref[...] = dt_wide.T[:nh, :]

        @pl.when(i + 1 < pl.num_programs(0))
        def _():
            fetch_x(i + 1).start()

    @pl.when(j < n_plain)
    def _():
        proj_ref[...] = jnp.dot(h_ref[...], wp_ref[...], preferred_element_type=F32).astype(BF16)

    for gi, a_ref in enumerate((a0_ref, a1_ref, a2_ref)):
        dil = DILATION_PATTERNS[gi][1]
        j0 = n_plain + QKV_PARTS * gi

        @pl.when((j >= j0) & (j < j0 + QKV_PARTS))
        def _(a_ref=a_ref, dil=dil):
            res = jnp.dot(h_ref[...], w_ref[...], preferred_element_type=F32)
            if dil == 1:
                a_ref[...] = res.astype(BF16)
            else:
                two_pass = dil > RELAYOUT_STRIDE and dil % RELAYOUT_STRIDE == 0
                quarter = tm // RELAYOUT_STRIDE
                for s in range(tn // LANES):
                    stage_ref[0] = res[:, s * LANES:(s + 1) * LANES]
                    if two_pass:
                        for r1 in range(RELAYOUT_STRIDE):
                            stage_ref[1, r1 * quarter:(r1 + 1) * quarter, :] = stage_ref[
                                0, pl.ds(r1, quarter, stride=RELAYOUT_STRIDE), :]
                    for r in range(dil):
                        if two_pass:
                            rows = pl.ds((r % RELAYOUT_STRIDE) * quarter + r // RELAYOUT_STRIDE,
                                         tm // dil, stride=dil // RELAYOUT_STRIDE)
                        else:
                            rows = pl.ds(r, tm // dil, stride=dil)
                        c0 = r * tn + s * LANES
                        a_ref[:, c0:c0 + LANES] = stage_ref[int(two_pass), rows, :].astype(BF16)


def _in_proj(x2, norm_w, w_main, w_dt, *, tm, tn, tp, plain_cols, nh):
    n, d = x2.shape
    assert plain_cols % tp == 0 and plain_cols % tn == 0
    n_plain = plain_cols // tp
    qkv_block0 = plain_cols // tn
    n_blocks = n_plain + QKV_PARTS * len(DILATION_PATTERNS)
    assert w_main.shape[1] == plain_cols + QKV_PARTS * len(DILATION_PATTERNS) * tn

    def a_spec(gi):
        dil = DILATION_PATTERNS[gi][1]
        j0 = n_plain + QKV_PARTS * gi
        return pl.BlockSpec((tm // dil, dil * tn),
                            lambda i, j: (i, jnp.clip(j - j0, 0, QKV_PARTS - 1)))

    def a_shape(gi):
        dil = DILATION_PATTERNS[gi][1]
        return jax.ShapeDtypeStruct((n // dil, dil * QKV_PARTS * tn), BF16)

    return pl.pallas_call(
        functools.partial(_in_proj_kernel, n_plain=n_plain),
        grid=(n // tm, n_blocks),
        in_specs=[
            pl.BlockSpec(memory_space=pl.ANY),
            pl.BlockSpec((1, d), lambda i, j: (0, 0)),
            pl.BlockSpec((d, tp), lambda i, j: (0, jnp.minimum(j, n_plain - 1))),
            pl.BlockSpec((d, tn), lambda i, j: (0, qkv_block0 + jnp.maximum(j - n_plain, 0))),
            pl.BlockSpec((d, LANES), lambda i, j: (0, 0)),
        ],
        out_specs=[
            pl.BlockSpec((tm, tp), lambda i, j: (i, jnp.minimum(j, n_plain - 1))),
            a_spec(0), a_spec(1), a_spec(2),
            pl.BlockSpec((tm, nh), lambda i, j: (i, 0)),
            pl.BlockSpec((nh, tm), lambda i, j: (0, i)),
        ],
        out_shape=[
            jax.ShapeDtypeStruct((n, plain_cols), BF16),
            a_shape(0), a_shape(1), a_shape(2),
            jax.ShapeDtypeStruct((n, nh), F32),
            jax.ShapeDtypeStruct((nh, n), F32),
        ],
        scratch_shapes=[pltpu.VMEM((tm, d), BF16), pltpu.VMEM((2, tm, LANES), F32),
                        pltpu.VMEM((tm, d), x2.dtype), pltpu.SemaphoreType.DMA(())],
        compiler_params=_params("arbitrary", "arbitrary"),
        name="in_proj",
    )(x2, norm_w.reshape(1, d), w_main, w_main, w_dt)


LOG2E = math.log2(math.e)
CONV_ROW_PITCH = 2


def _ssd_kernel(xbc_ref, z_ref, dt_ref, dtt_ref, cw_ref, cb_ref, dtb_ref, dtbt_ref, alog_ref,
                alogt_ref, dskip_ref, nw_ref, expand_ref, out_ref,
                xbuf_ref, state_ref, y_ref, *, n_heads, d_inner):
    L = SSD_CHUNK
    P = SSD_HEAD_DIM
    NS = SSD_D_STATE
    G = SSD_N_GROUPS
    R = n_heads // G
    GW = R * P
    W = SSD_CONV_WIDTH
    RP = CONV_ROW_PITCH
    n_slabs = xbuf_ref.shape[0]
    c = pl.program_id(1)

    def rows(first, count):
        return pl.ds(first * RP, count, stride=RP)

    @pl.when(c == 0)
    def _():
        state_ref[...] = jnp.zeros(state_ref.shape, F32)
        for s in range(n_slabs):
            xbuf_ref[s, rows(0, SUBLANES), :] = jnp.zeros((SUBLANES, LANES), F32)

    n_rows = xbc_ref.shape[0]

    @pl.when(c > 0)
    def _():
        for s in range(n_slabs):
            xbuf_ref[s, rows(0, SUBLANES), :] = xbuf_ref[s, rows(n_rows, SUBLANES), :]

    for s in range(n_slabs):
        xbuf_ref[s, rows(SUBLANES, n_rows), :] = xbc_ref[:, s * LANES:(s + 1) * LANES].astype(F32)

    for sub in range(n_rows // L):
        _ssd_chunk(sub * L, xbuf_ref, z_ref, dt_ref, dtt_ref, cw_ref, cb_ref, dtb_ref, dtbt_ref,
                   alog_ref, alogt_ref, dskip_ref, nw_ref, expand_ref, out_ref, state_ref, y_ref,
                   n_heads=n_heads, d_inner=d_inner)


def _ssd_chunk(r0, xbuf_ref, z_ref, dt_ref, dtt_ref, cw_ref, cb_ref, dtb_ref, dtbt_ref, alog_ref,
               alogt_ref, dskip_ref, nw_ref, expand_ref, out_ref, state_ref, y_ref, *, n_heads,
               d_inner):
    L = SSD_CHUNK
    P = SSD_HEAD_DIM
    NS = SSD_D_STATE
    G = SSD_N_GROUPS
    R = n_heads // G
    GW = R * P
    W = SSD_CONV_WIDTH
    RP = CONV_ROW_PITCH
    tok = slice(r0, r0 + L)

    def rows(first, count):
        return pl.ds(first * RP, count, stride=RP)

    def conv(col0, width):
        slabs = []
        for s in range(col0 // LANES, (col0 + width) // LANES):
            acc = cb_ref[:, s * LANES:(s + 1) * LANES]
            for w in range(W):
                acc = acc + (cw_ref[w:w + 1, s * LANES:(s + 1) * LANES]
                             * xbuf_ref[s, rows(r0 + SUBLANES - (W - 1) + w, L), :])
            slabs.append(acc)
        return _silu(jnp.concatenate(slabs, axis=1))

    def softplus(v):
        return jnp.maximum(v, 0.0) + jnp.log1p(jnp.exp(-jnp.abs(v)))

    dt = softplus(dt_ref[tok, :] + dtb_ref[...])
    dtt = softplus(dtt_ref[:, tok] + dtbt_ref[...])
    da = dt * (-LOG2E * jnp.exp(alog_ref[...]))
    dat = dtt * (-LOG2E * jnp.exp(alogt_ref[...]))
    row = lax.broadcasted_iota(jnp.int32, (L, L), 0)
    col = lax.broadcasted_iota(jnp.int32, (L, L), 1)
    causal = row >= col
    tri = jnp.where(causal, 1.0, 0.0).astype(BF16)
    trit = jnp.where(row <= col, 1.0, 0.0).astype(BF16)

    def split3(v):
        a = v.astype(BF16)
        r1 = v - a.astype(F32)
        b = r1.astype(BF16)
        cc = (r1 - b.astype(F32)).astype(BF16)
        return a, b, cc

    a2 = sum(jnp.dot(tri, p, preferred_element_type=F32) for p in split3(da))
    a2t = sum(jnp.dot(p, trit, preferred_element_type=F32) for p in split3(dat))
    a2_last = a2[L - 1:L, :]

    expand = expand_ref[...]

    def expand_heads(v):
        hi, lo = _split_bf16(v)
        return jnp.dot(jnp.concatenate([hi, lo], axis=1), expand, preferred_element_type=F32)

    in_scale_e = expand_heads(dt * jnp.exp2(a2_last - a2))
    tail8 = jnp.concatenate([jnp.exp2(a2_last), dskip_ref[...],
                             jnp.zeros((SUBLANES - 2, n_heads), F32)], axis=0)
    out_e = expand_heads(jnp.concatenate([jnp.exp2(a2), tail8], axis=0))
    out_scale_e = out_e[0:L, :]
    chunk_decay_e = out_e[L:L + 1, :]
    dskip_e = out_e[L + 1:L + 2, :]

    first_head = lax.broadcasted_iota(jnp.int32, (L, LANES), 1) < P

    for g in range(G):
        x0 = g * GW
        xs = conv(x0, GW)
        bm = conv(d_inner + g * NS, NS).astype(BF16)
        cm = conv(d_inner + G * NS + g * NS, NS).astype(BF16)
        cb = lax.dot_general(cm, bm, (((1,), (1,)), ((), ())), preferred_element_type=F32)
        cb = jnp.where(causal, cb, 0.0)
        y_parts = []
        for pr in range(GW // LANES):
            wgts = []
            for q in range(LANES // P):
                h = g * R + pr * (LANES // P) + q
                seg = a2[:, h:h + 1] - a2t[h:h + 1, :]
                decay = jnp.exp2(jnp.minimum(seg, 0.0))
                wgts.append((cb * decay * dtt[h:h + 1, :]).astype(BF16))
            slab = xs[:, pr * LANES:(pr + 1) * LANES]
            rhs = jnp.concatenate([jnp.where(first_head, slab, 0.0).astype(BF16),
                                   jnp.where(first_head, 0.0, slab).astype(BF16)], axis=0)
            y_parts.append(jnp.dot(jnp.concatenate(wgts, axis=1), rhs, preferred_element_type=F32))
        y = jnp.concatenate(y_parts, axis=1) + xs * dskip_e[:, x0:x0 + GW]
        st = state_ref[:, x0:x0 + GW]
        y = y + (jnp.dot(cm, st.astype(BF16), preferred_element_type=F32)
                 * out_scale_e[:, x0:x0 + GW])
        y_ref[tok, x0:x0 + GW] = y
        xin = (xs * in_scale_e[:, x0:x0 + GW]).astype(BF16)
        st_new = lax.dot_general(bm, xin, (((0,), (0,)), ((), ())), preferred_element_type=F32)
        state_ref[:, x0:x0 + GW] = st * chunk_decay_e[:, x0:x0 + GW] + st_new

    yz = y_ref[tok, :] * _silu(z_ref[tok, :].astype(F32))
    ms = jnp.mean(yz * yz, axis=-1, keepdims=True)
    out_ref[tok, :] = (yz * lax.rsqrt(ms + NORM_EPS) * nw_ref[...]).astype(out_ref.dtype)


def _ssd(proj, dt, dtt, conv_w, conv_b, dt_bias, a_log, d_skip, norm_w, *, batch, seq, d_inner,
         xbc_block, z_block):
    n = batch * seq
    n_heads = dt.shape[1]
    L = SSD_CHUNK * math.gcd(seq // SSD_CHUNK, SSD_CHUNKS_PER_STEP)
    nc = seq // L
    conv_dim = conv_w.shape[1]
    expand = (np.arange(d_inner)[None, :] // SSD_HEAD_DIM == np.arange(n_heads)[:, None])
    expand = jnp.asarray(np.concatenate([expand, expand], axis=0), BF16)
    assert conv_dim % LANES == 0
    kern = functools.partial(_ssd_kernel, n_heads=n_heads, d_inner=d_inner)
    small = lambda shape: pl.BlockSpec(shape, lambda b, c: (0, 0))
    return pl.pallas_call(
        kern,
        grid=(batch, nc),
        in_specs=[
            pl.BlockSpec((L, conv_dim), lambda b, c: (b * nc + c, xbc_block)),
            pl.BlockSpec((L, d_inner), lambda b, c: (b * nc + c, z_block)),
            pl.BlockSpec((L, n_heads), lambda b, c: (b * nc + c, 0)),
            pl.BlockSpec((n_heads, L), lambda b, c: (0, b * nc + c)),
            small((SSD_CONV_WIDTH, conv_dim)),
            small((1, conv_dim)),
            small((1, n_heads)),
            small((n_heads, 1)),
            small((1, n_heads)),
            small((n_heads, 1)),
            small((1, n_heads)),
            small((1, d_inner)),
            small((2 * n_heads, d_inner)),
        ],
        out_specs=pl.BlockSpec((L, d_inner), lambda b, c: (b * nc + c, 0)),
        out_shape=jax.ShapeDtypeStruct((n, d_inner), BF16),
        scratch_shapes=[
            pltpu.VMEM((conv_dim // LANES, CONV_ROW_PITCH * (L + SUBLANES), LANES), F32),
            pltpu.VMEM((SSD_D_STATE, d_inner), F32),
            pltpu.VMEM((L, d_inner), F32),
        ],
        compiler_params=_params("arbitrary", "arbitrary"),
        name="ssd",
    )(proj, proj, dt, dtt, conv_w, conv_b.reshape(1, -1), dt_bias.reshape(1, -1),
      dt_bias.reshape(-1, 1), a_log.reshape(1, -1), a_log.reshape(-1, 1), d_skip.reshape(1, -1),
      norm_w.reshape(1, -1), expand)


def _attn_kernel(q_ref, k_ref, v_ref, o_ref, lse_ref, kp_ref, vp_ref, *, slopes, dilation, hops):
    nb = pl.program_id(2)
    BLK = ATTN_BLOCK
    E = ATTN_HEAD_DIM

    @pl.when(nb == 0)
    def _():
        kp_ref[...] = jnp.zeros(kp_ref.shape, kp_ref.dtype)
        vp_ref[...] = jnp.zeros(vp_ref.shape, vp_ref.dtype)

    scale = E ** -0.5
    nt = (((1,), (1,)), ((), ()))
    nh = len(slopes)
    nblk = q_ref.shape[0] // BLK
    nres = q_ref.shape[1] // (nh * E)
    q = q_ref[...]
    kk = jnp.concatenate([kp_ref[...], k_ref[...]], axis=0)
    vv = jnp.concatenate([vp_ref[...], v_ref[...]], axis=0)
    units = [(j, c) for c in range(nres * nh) for j in range(nblk)]
    cols = lambda c: slice(c * E, (c + 1) * E)
    rows = lambda j: slice(j * BLK, (j + 1) * BLK)
    qi = lax.broadcasted_iota(jnp.int32, (BLK, BLK), 0)
    ki = lax.broadcasted_iota(jnp.int32, (BLK, BLK), 1)
    rel_cur = qi - ki
    rel_prev = rel_cur + BLK
    first = rel_prev <= jnp.where(nb > 0, hops, -1)
    later = rel_prev <= hops
    ok_cur = jnp.concatenate([rel_cur >= 0 for _ in units], axis=0)
    ok_prev = jnp.concatenate([first if j == 0 else later for j, _ in units], axis=0)
    dist_cur = (rel_cur * dilation).astype(F32)
    dist_prev = (rel_prev * dilation).astype(F32)
    bias_cur = jnp.concatenate([slopes[c % nh] * dist_cur for _, c in units], axis=0)
    bias_prev = jnp.concatenate([slopes[c % nh] * dist_prev for _, c in units], axis=0)
    s_cur = jnp.concatenate(
        [lax.dot_general(q[rows(j), cols(h)], kk[rows(j + 1), cols(h)], nt,
                         preferred_element_type=F32) for j, h in units], axis=0)
    s_prev = jnp.concatenate(
        [lax.dot_general(q[rows(j), cols(h)], kk[rows(j), cols(h)], nt,
                         preferred_element_type=F32) for j, h in units], axis=0)
    l_cur = jnp.where(ok_cur, s_cur * scale - bias_cur, -jnp.inf)
    l_prev = jnp.where(ok_prev, s_prev * scale - bias_prev, -jnp.inf)
    m = jnp.max(jnp.maximum(l_cur, l_prev), axis=-1, keepdims=True)
    p_cur = jnp.exp(l_cur - m)
    p_prev = jnp.exp(l_prev - m)
    den = jnp.sum(p_cur + p_prev, axis=-1, keepdims=True)
    p_cur = p_cur.astype(BF16)
    p_prev = p_prev.astype(BF16)
    inv = 1.0 / den
    lse = m + jnp.log(den)
    lane = lax.broadcasted_iota(jnp.int32, (BLK, LANES), 1)
    for res in range(nres):
        for j in range(nblk):
            lse_tile = jnp.zeros((BLK, LANES), F32)
            for h in range(nh):
                c = res * nh + h
                u = rows(units.index((j, c)))
                acc = (jnp.dot(p_cur[u], vv[rows(j + 1), cols(c)], preferred_element_type=F32)
                       + jnp.dot(p_prev[u], vv[rows(j), cols(c)], preferred_element_type=F32))
                o_ref[rows(j), cols(c)] = (acc * inv[u]).astype(o_ref.dtype)
                lse_tile = jnp.where(lane == h, lse[u], lse_tile)
            lse_ref[rows(j), res * LANES:(res + 1) * LANES] = lse_tile
    kp_ref[...] = k_ref[rows(nblk - 1), :]
    vp_ref[...] = v_ref[rows(nblk - 1), :]


def _attn_group(qkv, gi, *, batch, seq):
    window, dilation = DILATION_PATTERNS[gi]
    hops = window // dilation
    n_heads_total = ATTN_HEADS_PER_GROUP * len(DILATION_PATTERNS)
    slopes = tuple(float(2.0 ** (-8.0 * (gi * ATTN_HEADS_PER_GROUP + h + 1) / n_heads_total))
                   for h in range(ATTN_HEADS_PER_GROUP))
    gw = ATTN_HEADS_PER_GROUP * ATTN_HEAD_DIM
    assert seq % (dilation * ATTN_BLOCK) == 0
    sub = seq // dilation
    blocks = math.gcd(sub // ATTN_BLOCK, ATTN_BLOCKS_PER_STEP)
    nres = math.gcd(dilation, ATTN_BLOCKS_PER_STEP // blocks)
    rows = ATTN_BLOCK * blocks
    nb = sub // rows
    nr = dilation // nres
    kern = functools.partial(_attn_kernel, slopes=slopes, dilation=dilation, hops=hops)

    def part(p):
        return pl.BlockSpec((rows, nres * gw), lambda b, r, n: (b * nb + n, p * nr + r))

    o, lse = pl.pallas_call(
        kern,
        grid=(batch, nr, nb),
        in_specs=[part(0), part(1), part(2)],
        out_specs=[
            pl.BlockSpec((rows, nres * gw), lambda b, r, n: (b * nb + n, r)),
            pl.BlockSpec((rows, nres * LANES), lambda b, r, n: (b * nb + n, r)),
        ],
        out_shape=[
            jax.ShapeDtypeStruct((batch * sub, dilation * gw), BF16),
            jax.ShapeDtypeStruct((batch * sub, dilation * LANES), F32),
        ],
        scratch_shapes=[pltpu.VMEM((ATTN_BLOCK, nres * gw), BF16),
                        pltpu.VMEM((ATTN_BLOCK, nres * gw), BF16)],
        compiler_params=_params("arbitrary", "arbitrary", "arbitrary"),
        name=f"attn_g{gi}",
    )(qkv, qkv, qkv)
    return o, lse


def _merge_kernel(yn_ref, o0_ref, o1_ref, o2_ref, l0_ref, l1_ref, l2_ref, gate_ref,
                  wssd_ref, wattn_ref, bg_ref, merged_ref, ostage_ref, lstage_ref, *, d_model):
    E = ATTN_HEAD_DIM
    tm = merged_ref.shape[0]
    n_slabs = o0_ref.shape[1] // LANES
    y_ssd = jnp.dot(yn_ref[...], wssd_ref[...], preferred_element_type=F32)

    def token_major(gi, o_ref, l_ref):
        dil = DILATION_PATTERNS[gi][1]
        if dil == 1:
            return o_ref[...].astype(F32), l_ref[...]
        for r in range(dil):
            rows = pl.ds(r, tm // dil, stride=dil)
            lstage_ref[gi - 1, rows, :] = l_ref[:, r * LANES:(r + 1) * LANES]
            for s in range(n_slabs):
                c0 = (r * n_slabs + s) * LANES
                ostage_ref[gi - 1, s, rows, :] = o_ref[:, c0:c0 + LANES].astype(F32)
        out = jnp.concatenate([ostage_ref[gi - 1, s] for s in range(n_slabs)], axis=1)
        return out, lstage_ref[gi - 1]

    (o0, l0), (o1, l1), (o2, l2) = (token_major(gi, o_ref, l_ref) for gi, (o_ref, l_ref) in
                                    enumerate(((o0_ref, l0_ref), (o1_ref, l1_ref), (o2_ref, l2_ref))))

    lm = jnp.maximum(jnp.maximum(l0, l1), l2)
    e0, e1, e2 = jnp.exp(l0 - lm), jnp.exp(l1 - lm), jnp.exp(l2 - lm)
    inv = 1.0 / (e0 + e1 + e2)
    parts = []
    for h in range(ATTN_HEADS_PER_GROUP):
        sl = slice(h * E, (h + 1) * E)
        parts.append((e0[:, h:h + 1] * inv[:, h:h + 1]) * o0[:, sl]
                     + (e1[:, h:h + 1] * inv[:, h:h + 1]) * o1[:, sl]
                     + (e2[:, h:h + 1] * inv[:, h:h + 1]) * o2[:, sl])
    o = jnp.concatenate(parts, axis=-1).astype(BF16)
    y_attn = jnp.dot(o, wattn_ref[...], preferred_element_type=F32)

    gates = _sigmoid(gate_ref[...].astype(F32) + bg_ref[...])
    merged_ref[...] = (gates[:, :d_model] * y_ssd + gates[:, d_model:] * y_attn).astype(BF16)


def _merge(yn, outs, lses, proj, w_ssd_out, w_attn_out, b_gate, *, tm, gate_block):
    n, d = yn.shape
    aw = outs[0].shape[1]
    dils = [dil for _, dil in DILATION_PATTERNS]
    assert dils[0] == 1 and all(tm % (dil * SUBLANES) == 0 for dil in dils)
    row = lambda w: pl.BlockSpec((tm, w), lambda i: (i, 0))
    grouped = lambda dil, w: pl.BlockSpec((tm // dil, dil * w), lambda i: (i, 0))
    const = lambda shape: pl.BlockSpec(shape, lambda i: (0, 0), pipeline_mode=pl.Buffered(1))
    return pl.pallas_call(
        functools.partial(_merge_kernel, d_model=d),
        grid=(n // tm,),
        in_specs=[
            row(d), *[grouped(dil, aw) for dil in dils], *[grouped(dil, LANES) for dil in dils],
            pl.BlockSpec((tm, 2 * d), lambda i: (i, gate_block)),
            const((d, d)), const((aw, d)), const((1, 2 * d)),
        ],
        out_specs=row(d),
        out_shape=jax.ShapeDtypeStruct((n, d), BF16),
        scratch_shapes=[pltpu.VMEM((len(dils) - 1, aw // LANES, tm, LANES), F32),
                        pltpu.VMEM((len(dils) - 1, tm, LANES), F32)],
        compiler_params=_params("arbitrary"),
        name="merge",
    )(yn, outs[0], outs[1], outs[2], lses[0], lses[1], lses[2], proj,
      w_ssd_out.astype(BF16), w_attn_out.astype(BF16), b_gate.reshape(1, -1))


def _route_kernel(merged_ref, x_ref, wout_ref, fnw_ref, wr_ref, rb_ref,
                  x1_ref, eid_ref, gcol_ref, rank_ref, cnt_ref, carry_ref):
    i = pl.program_id(0)
    tm = x_ref.shape[0]

    @pl.when(i == 0)
    def _():
        carry_ref[...] = jnp.zeros(carry_ref.shape, F32)

    x1 = x_ref[...] + jnp.dot(merged_ref[...], wout_ref[...], preferred_element_type=F32)
    x1_ref[...] = x1

    ms = jnp.mean(x1 * x1, axis=-1, keepdims=True)
    hn = x1 * lax.rsqrt(ms + NORM_EPS) * fnw_ref[...]

    hn_hi, hn_lo = _split_bf16(hn)
    by_hi = jnp.dot(hn_hi, wr_ref[...], preferred_element_type=F32)
    by_lo = jnp.dot(hn_lo, wr_ref[:, 0:LANES], preferred_element_type=F32)
    logits_tok = by_hi[:, 0:LANES] + by_hi[:, LANES:] + by_lo
    logits = logits_tok.T[0:ROUTER_ROWS, :] + rb_ref[...]

    grow = lax.broadcasted_iota(jnp.int32, (SUBLANES, tm), 0)
    gl = jnp.where(grow < N_EXPERT_GROUPS, logits[0:SUBLANES, :], -jnp.inf)
    gmax = jnp.max(gl, axis=0, keepdims=True)
    gidx = jnp.min(jnp.where(gl == gmax, grow, N_EXPERT_GROUPS), axis=0, keepdims=True)
    group_gate = 1.0 / jnp.sum(jnp.exp(gl - gmax), axis=0, keepdims=True)

    in_group = jnp.zeros((EXPERTS_PER_GROUP, tm), F32)
    for g in range(N_EXPERT_GROUPS):
        r0 = ROUTER_EXPERT_ROW0 + g * EXPERTS_PER_GROUP
        in_group = jnp.where(gidx == g, logits[r0:r0 + EXPERTS_PER_GROUP, :], in_group)
    erow = lax.broadcasted_iota(jnp.int32, in_group.shape, 0)
    v1 = jnp.max(in_group, axis=0, keepdims=True)
    i1 = jnp.min(jnp.where(in_group == v1, erow, EXPERTS_PER_GROUP), axis=0, keepdims=True)
    rest = jnp.where(erow == i1, -jnp.inf, in_group)
    v2 = jnp.max(rest, axis=0, keepdims=True)
    i2 = jnp.min(jnp.where(rest == v2, erow, EXPERTS_PER_GROUP), axis=0, keepdims=True)
    t = jnp.exp(v2 - v1)
    g1 = group_gate / (1.0 + t)
    g2 = group_gate * t / (1.0 + t)
    eid1 = gidx * EXPERTS_PER_GROUP + i1
    eid2 = gidx * EXPERTS_PER_GROUP + i2
    slot = lax.broadcasted_iota(jnp.int32, (SUBLANES, tm), 0)
    eid_ref[...] = jnp.where(slot == 0, eid1, jnp.where(slot == 1, eid2, 0))

    grow8 = lax.broadcasted_iota(jnp.int32, (LANES, tm), 0)
    gt = jnp.where(grow8 == 0, g1, jnp.where(grow8 == 1, g2, 0.0))
    gcol_ref[...] = gt.T

    xrow = lax.broadcasted_iota(jnp.int32, (N_EXPERTS, tm), 0)
    oh1 = xrow == eid1
    oh2 = xrow == eid2
    oh = jnp.where(oh1 | oh2, 1.0, 0.0)
    ti = lax.broadcasted_iota(jnp.int32, (tm, tm), 0)
    tj = lax.broadcasted_iota(jnp.int32, (tm, tm), 1)
    before = jnp.where(ti < tj, 1.0, 0.0).astype(BF16)
    prior = jnp.dot(oh.astype(BF16), before, preferred_element_type=F32) + carry_ref[:, 0:1]
    r1 = jnp.sum(jnp.where(oh1, prior, 0.0), axis=0, keepdims=True)
    r2 = jnp.sum(jnp.where(oh2, prior, 0.0), axis=0, keepdims=True)
    rank_ref[...] = jnp.where(slot == 0, r1, jnp.where(slot == 1, r2, 0.0)).astype(jnp.int32)
    carry_ref[...] = carry_ref[...] + jnp.sum(oh, axis=1, keepdims=True)
    cnt_ref[...] = carry_ref[...].astype(jnp.int32)


def _route(merged, x2, w_out, ffn_norm_w, wr, rb, *, tm):
    n, d = x2.shape
    wrc = jnp.zeros((d, LANES), F32)
    wrc = wrc.at[:, 0:N_EXPERT_GROUPS].set(wr[0]).at[:, ROUTER_EXPERT_ROW0:ROUTER_ROWS].set(wr[1])
    rbc = jnp.zeros((ROUTER_ROWS, 1), F32)
    rbc = rbc.at[0:N_EXPERT_GROUPS, 0].set(rb[0]).at[ROUTER_EXPERT_ROW0:, 0].set(rb[1])
    wr_hi_lo = jnp.concatenate(_split_bf16(wrc), axis=1)
    row = lambda w: pl.BlockSpec((tm, w), lambda i: (i, 0))
    const = lambda shape: pl.BlockSpec(shape, lambda i: (0, 0), pipeline_mode=pl.Buffered(1))
    return pl.pallas_call(
        _route_kernel,
        grid=(n // tm,),
        in_specs=[
            row(d), row(d),
            const((d, d)), const((1, d)),
            const((d, 2 * LANES)), const((ROUTER_ROWS, 1)),
        ],
        out_specs=[
            row(d),
            pl.BlockSpec((SUBLANES, tm), lambda i: (0, i)),
            row(LANES),
            pl.BlockSpec((SUBLANES, tm), lambda i: (0, i)),
            pl.BlockSpec((N_EXPERTS, LANES), lambda i: (0, 0)),
        ],
        out_shape=[
            jax.ShapeDtypeStruct((n, d), F32),
            jax.ShapeDtypeStruct((SUBLANES, n), jnp.int32),
            jax.ShapeDtypeStruct((n, LANES), F32),
            jax.ShapeDtypeStruct((SUBLANES, n), jnp.int32),
            jax.ShapeDtypeStruct((N_EXPERTS, LANES), jnp.int32),
        ],
        scratch_shapes=[pltpu.VMEM((N_EXPERTS, LANES), F32)],
        compiler_params=_params("arbitrary"),
        name="route",
    )(merged, x2, w_out.astype(BF16), ffn_norm_w.reshape(1, -1), wr_hi_lo, rbc)


def _scatter_kernel(pad_start_ref, pad_len_ref, n_active_ref, pos_ref, hn_ref, xs_ref, zero_ref,
                    sem, zsem, *, tme, n_tiles):
    tm = hn_ref.shape[0]
    zrows = zero_ref.shape[0]

    @pl.when(pl.program_id(0) == 0)
    def _():
        zero_ref[...] = jnp.zeros(zero_ref.shape, zero_ref.dtype)

        def fills(act):
            def per_expert(e, carry):
                off = pad_start_ref[e]
                left = pad_len_ref[e]
                head = left & (SUBLANES - 1)
                for r in range(SUBLANES - 1):
                    @pl.when(r < head)
                    def _(r=r):
                        act(pltpu.make_async_copy(zero_ref.at[pl.ds(0, 1), :],
                                                  xs_ref.at[pl.ds(off + r, 1), :], zsem))

                off = off + head
                bit = zrows
                while bit >= SUBLANES:
                    take = left & bit

                    @pl.when(take != 0)
                    def _(off=off, bit=bit):
                        act(pltpu.make_async_copy(
                            zero_ref.at[pl.ds(0, bit), :],
                            xs_ref.at[pl.ds(pl.multiple_of(off, SUBLANES), bit), :], zsem))

                    off = off + take
                    bit //= 2
                return carry

            def per_tile(t, carry):
                @pl.when(t >= n_active_ref[0])
                def _():
                    for part in range(tme // zrows):
                        row0 = pl.multiple_of(t * tme + part * zrows, zrows)
                        act(pltpu.make_async_copy(zero_ref, xs_ref.at[pl.ds(row0, zrows), :], zsem))

                return carry

            lax.fori_loop(0, N_EXPERTS, per_expert, 0)
            lax.fori_loop(0, n_tiles, per_tile, 0)

        fills(lambda cp: cp.start())
        fills(lambda cp: cp.wait())

    unroll = math.gcd(tm, ROW_DMA_UNROLL)

    def start(tb, carry):
        for u in range(unroll):
            t = tb * unroll + u
            for k in range(EXPERT_TOP_K):
                dst = pos_ref[0, 0, k * tm + t]
                pltpu.make_async_copy(hn_ref.at[pl.ds(t, 1), :], xs_ref.at[pl.ds(dst, 1), :],
                                      sem).start(priority=(u + k) % 2)
        return carry

    lax.fori_loop(0, tm // unroll, start, 0)
    for k in range(EXPERT_TOP_K):
        pltpu.make_async_copy(hn_ref, xs_ref.at[pl.ds(0, tm), :], sem).wait()


def _scatter(hn, pos_tiles, pad_start, pad_len, n_active, *, tm, tme, n_tiles):
    n, d = hn.shape
    assert tme % 2 == 0 and (tme // 2) & (tme // 2 - 1) == 0
    grid_spec = pltpu.PrefetchScalarGridSpec(
        num_scalar_prefetch=3,
        grid=(n // tm,),
        in_specs=[
            pl.BlockSpec((1, 1, EXPERT_TOP_K * tm), lambda i, *_: (i, 0, 0),
                         memory_space=pltpu.SMEM),
            pl.BlockSpec((tm, d), lambda i, *_: (i, 0)),
        ],
        out_specs=pl.BlockSpec(memory_space=pl.ANY),
        scratch_shapes=[pltpu.VMEM((tme // 2, d), hn.dtype), pltpu.SemaphoreType.DMA(()),
                        pltpu.SemaphoreType.DMA(())],
    )
    return pl.pallas_call(
        functools.partial(_scatter_kernel, tme=tme, n_tiles=n_tiles),
        grid_spec=grid_spec,
        out_shape=jax.ShapeDtypeStruct((n_tiles * tme, d), hn.dtype),
        compiler_params=_params("arbitrary"),
        name="scatter",
    )(pad_start, pad_len, n_active, pos_tiles, hn)


W_CHUNK_ROWS = 256
W_RING = 8


def _experts_kernel(te_ref, slot_ref, nxt_ref, lo_ref, hi_ref, na_ref, xs_ref, fnw_ref, wg_hbm,
                    wu_hbm, wd_hbm, ys_ref, wres_ref, stage_ref, sem, *, n_gu, n_r, n_h):
    i = pl.program_id(0)
    n_chunks = 2 * n_gu + n_h * n_r
    ring, cr, f = stage_ref.shape
    d = n_gu * cr

    def stage_copy(src, c):
        return pltpu.make_async_copy(src, stage_ref.at[c % ring], sem.at[c % ring])

    def start_chunk(e, c):
        @pl.when(c < n_gu)
        def _():
            stage_copy(wg_hbm.at[e, pl.ds(pl.multiple_of(c * cr, cr), cr), :], c).start()

        @pl.when((c >= n_gu) & (c < 2 * n_gu))
        def _():
            stage_copy(wu_hbm.at[e, pl.ds(pl.multiple_of((c - n_gu) * cr, cr), cr), :], c).start()

        @pl.when(c >= 2 * n_gu)
        def _():
            q = c - 2 * n_gu
            h = q // n_r
            r = q - h * n_r
            stage_copy(wd_hbm.at[e, pl.ds(pl.multiple_of(r * cr, cr), cr),
                                 pl.ds(pl.multiple_of(h * f, f), f)], c).start()

    def start_first(e):
        def body(c, carry):
            start_chunk(e, c)
            return carry

        lax.fori_loop(0, ring, body, 0)

    def convert(e, lo, hi, dst_slot):
        def body(c, carry):
            stage_copy(wg_hbm.at[0, pl.ds(0, cr), :], c).wait()
            wres_ref[dst_slot, pl.ds(pl.multiple_of(c * cr, cr), cr), :] = (
                stage_ref[c % ring].astype(BF16))

            @pl.when(c + ring < n_chunks)
            def _():
                start_chunk(e, c + ring)

            return carry

        lax.fori_loop(lo, hi, body, 0)

    @pl.when(i == 0)
    def _():
        start_first(te_ref[0])
        convert(te_ref[0], 0, n_chunks, slot_ref[0])

    @pl.when(i < na_ref[0])
    def _():
        slot = slot_ref[i]
        nxt = nxt_ref[i]

        @pl.when((nxt >= 0) & (lo_ref[i] == 0))
        def _():
            start_first(nxt)

        xr = xs_ref[...]
        ms = jnp.mean(xr * xr, axis=-1, keepdims=True)
        xb = (xr * lax.rsqrt(ms + NORM_EPS) * fnw_ref[...]).astype(BF16)
        hg = jnp.dot(xb, wres_ref[slot, 0:d, :], preferred_element_type=F32)
        hu = jnp.dot(xb, wres_ref[slot, d:2 * d, :], preferred_element_type=F32)
        hmid = (_silu(hg) * hu).astype(BF16)
        for h in range(n_h):
            r0 = 2 * d + h * n_r * cr
            ys_ref[:, h * f:(h + 1) * f] = jnp.dot(hmid, wres_ref[slot, r0:r0 + n_r * cr, :],
                                                    preferred_element_type=F32)

        @pl.when(nxt >= 0)
        def _():
            convert(nxt, lo_ref[i], hi_ref[i], 1 - slot)

    @pl.when(i >= na_ref[0])
    def _():
        ys_ref[...] = jnp.zeros(ys_ref.shape, F32)


def _experts(xs, norm_w, tile_expert, n_active, cnt, starts, padded, w_g, w_u, w_d, *, tm):
    p, d = xs.shape
    ne, _, f = w_g.shape
    n_tiles = p // tm
    cr = W_CHUNK_ROWS
    assert d % cr == 0 and f % cr == 0 and d % f == 0
    n_gu, n_r, n_h = d // cr, f // cr, d // f
    n_chunks = 2 * n_gu + n_h * n_r

    ids = jnp.arange(ne, dtype=jnp.int32)
    present = cnt > 0
    later = jnp.where(present[None, :] & (ids[None, :] > ids[:, None]), ids[None, :], ne)
    nxt_e = jnp.min(later, axis=1)
    nxt_e = jnp.where(nxt_e >= ne, -1, nxt_e)
    run_e = jnp.cumsum(present.astype(jnp.int32)) - 1
    onehot = (tile_expert[:, None] == ids[None, :]).astype(jnp.int32)
    pick = lambda v: jnp.sum(onehot * v[None, :].astype(jnp.int32), axis=1)
    tiles = jnp.arange(n_tiles, dtype=jnp.int32)
    active = tiles < n_active[0]
    j = tiles - pick(starts) // tm
    k = jnp.maximum(pick(padded) // tm, 1)
    nxt_t = jnp.where(active, pick(nxt_e), -1).astype(jnp.int32)
    lo_t = jnp.where(active, (n_chunks * j) // k, 0).astype(jnp.int32)
    hi_t = jnp.where(active, (n_chunks * (j + 1)) // k, 0).astype(jnp.int32)
    slot_t = (pick(run_e) % 2).astype(jnp.int32)

    def rows(i, te, sl, nx, lo, hi, na):
        return (jnp.minimum(i, na[0] - 1), 0)

    grid_spec = pltpu.PrefetchScalarGridSpec(
        num_scalar_prefetch=6,
        grid=(n_tiles,),
        in_specs=[
            pl.BlockSpec((tm, d), rows),
            pl.BlockSpec((1, d), lambda i, *_: (0, 0)),
            pl.BlockSpec(memory_space=pl.ANY),
            pl.BlockSpec(memory_space=pl.ANY),
            pl.BlockSpec(memory_space=pl.ANY),
        ],
        out_specs=pl.BlockSpec((tm, d), lambda i, *_: (i, 0)),
        scratch_shapes=[
            pltpu.VMEM((2, n_chunks * cr, f), BF16),
            pltpu.VMEM((W_RING, cr, f), F32),
            pltpu.SemaphoreType.DMA((W_RING,)),
        ],
    )
    return pl.pallas_call(
        functools.partial(_experts_kernel, n_gu=n_gu, n_r=n_r, n_h=n_h),
        grid_spec=grid_spec,
        out_shape=jax.ShapeDtypeStruct((p, d), F32),
        compiler_params=_params("arbitrary"),
        name="experts",
    )(tile_expert, slot_t, nxt_t, lo_t, hi_t, n_active, xs, norm_w.reshape(1, d), w_g, w_u, w_d)


def _combine_kernel(pos_ref, pos_next_ref, ys_ref, x1_ref, gcol_ref, nw_ref, out_ref, buf_ref, sem,
                    *, final):
    i = pl.program_id(0)
    n_steps = pl.num_programs(0)
    tm = x1_ref.shape[0]

    unroll = math.gcd(tm, ROW_DMA_UNROLL)

    def issue(p_ref, slot):
        def start(tb, carry):
            for u in range(unroll):
                t = tb * unroll + u
                for k in range(EXPERT_TOP_K):
                    src = p_ref[0, 0, k * tm + t]
                    pltpu.make_async_copy(ys_ref.at[pl.ds(src, 1), :],
                                          buf_ref.at[slot, k, pl.ds(t, 1), :],
                                          sem.at[slot]).start(priority=(u + k) % 2)
            return carry

        lax.fori_loop(0, tm // unroll, start, 0)

    slot = i % 2

    @pl.when(i == 0)
    def _():
        issue(pos_ref, 0)

    @pl.when(i + 1 < n_steps)
    def _():
        issue(pos_next_ref, 1 - slot)

    for k in range(EXPERT_TOP_K):
        pltpu.make_async_copy(ys_ref.at[pl.ds(0, tm), :], buf_ref.at[slot, k], sem.at[slot]).wait()
    g = gcol_ref[...]
    xo = x1_ref[...] + g[:, 0:1] * buf_ref[slot, 0] + g[:, 1:2] * buf_ref[slot, 1]
    if final:
        ms = jnp.mean(xo * xo, axis=-1, keepdims=True)
        xo = xo * lax.rsqrt(ms + NORM_EPS) * nw_ref[...]
    out_ref[...] = xo


def _combine(ys, pos_tiles, x1, gcol, norm_w, *, tm, final):
    n, d = x1.shape
    last = n // tm - 1
    return pl.pallas_call(
        functools.partial(_combine_kernel, final=final),
        grid=(n // tm,),
        in_specs=[
            pl.BlockSpec((1, 1, EXPERT_TOP_K * tm), lambda i: (i, 0, 0), memory_space=pltpu.SMEM),
            pl.BlockSpec((1, 1, EXPERT_TOP_K * tm), lambda i: (jnp.minimum(i + 1, last), 0, 0),
                         memory_space=pltpu.SMEM),
            pl.BlockSpec(memory_space=pl.ANY),
            pl.BlockSpec((tm, d), lambda i: (i, 0)),
            pl.BlockSpec((tm, LANES), lambda i: (i, 0)),
            pl.BlockSpec((1, d), lambda i: (0, 0)),
        ],
        out_specs=pl.BlockSpec((tm, d), lambda i: (i, 0)),
        out_shape=jax.ShapeDtypeStruct((n, d), F32),
        scratch_shapes=[pltpu.VMEM((2, EXPERT_TOP_K, tm, d), F32), pltpu.SemaphoreType.DMA((2,))],
        compiler_params=_params("arbitrary"),
        name="combine",
    )(pos_tiles, pos_tiles, ys, x1, gcol, norm_w.reshape(1, d))


def _tiles(n):
    return dict(proj_tm=math.gcd(n, 1024), mix_tm=math.gcd(n, 512), moe_tm=256,
                route_tm=math.gcd(n, 512))


def _layer(x2, batch, seq, layer, attn_norm_w, w_in_stack, b_gate, conv_w, conv_b, dt_bias, a_log,
           d_skip, ssd_norm_w, w_ssd_out, w_attn_out, w_out, ffn_norm_w, w_gr, b_gr, w_er, b_er,
           w_g, w_u, w_d):
    n, d = x2.shape
    n_heads = dt_bias.shape[0]
    d_inner = ssd_norm_w.shape[0]
    conv_dim = conv_w.shape[1]
    aw_total = ATTN_HEADS_PER_GROUP * len(DILATION_PATTERNS) * ATTN_HEAD_DIM
    gw = ATTN_HEADS_PER_GROUP * ATTN_HEAD_DIM
    tiles = _tiles(n)

    c_z, c_xbc, c_dt = d_inner, d_inner + conv_dim, d_inner + conv_dim + n_heads
    c_gate = c_dt + QKV_PARTS * aw_total
    segments = [(c_gate, 2 * d), (c_z, conv_dim), (0, d_inner)]
    segments += [(c_dt + p * aw_total + gi * gw, gw)
                 for gi in range(len(DILATION_PATTERNS)) for p in range(QKV_PARTS)]
    assert all(width % gw == 0 for _, width in segments)
    starts = tuple(start + b * gw for start, width in segments for b in range(width // gw))
    w_main, w_dt = _regroup_weight(w_in_stack, layer, starts, gw, c_xbc, n_heads)
    off_gate, off_xbc, off_z = 0, 2 * d, 2 * d + conv_dim
    plain_cols = off_z + d_inner
    assert off_xbc % conv_dim == 0 and off_z % d_inner == 0 and plain_cols % gw == 0

    proj, qkv0, qkv1, qkv2, dt, dtt = _in_proj(x2, attn_norm_w, w_main, w_dt, tm=tiles["proj_tm"],
                                               tn=gw, tp=4 * gw, plain_cols=plain_cols, nh=n_heads)

    yn = _ssd(proj, dt, dtt, conv_w, conv_b, dt_bias, a_log, d_skip, ssd_norm_w, batch=batch,
              seq=seq, d_inner=d_inner, xbc_block=off_xbc // conv_dim, z_block=off_z // d_inner)

    outs, lses = [], []
    for gi, qkv in enumerate((qkv0, qkv1, qkv2)):
        o_g, lse_g = _attn_group(qkv, gi, batch=batch, seq=seq)
        outs.append(o_g)
        lses.append(lse_g)

    merged = _merge(yn, outs, lses, proj, w_ssd_out, w_attn_out, b_gate, tm=tiles["mix_tm"],
                    gate_block=off_gate // (2 * d))
    x1, eid, gcol, rank, counts = _route(merged, x2, w_out, ffn_norm_w, (w_gr, w_er),
                                         (b_gr, b_er), tm=tiles["mix_tm"])

    tme = tiles["moe_tm"]
    cnt = counts[:, 0]
    padded = ((cnt + tme - 1) // tme) * tme
    ends = jnp.cumsum(padded)
    starts = ends - padded
    experts = jnp.arange(N_EXPERTS, dtype=jnp.int32)[:, None, None]
    pos = rank[:EXPERT_TOP_K] + jnp.sum(
        jnp.where(eid[None, :EXPERT_TOP_K] == experts, starts[:, None, None], 0), axis=0)
    n_tiles = EXPERT_TOP_K * n // tme + N_EXPERTS
    tile_start = jnp.arange(n_tiles, dtype=jnp.int32) * tme
    tile_expert = jnp.minimum(jnp.sum(ends[None, :] <= tile_start[:, None], axis=1),
                              N_EXPERTS - 1).astype(jnp.int32)
    n_active = (ends[-1:] // tme).astype(jnp.int32)

    rtm = tiles["route_tm"]
    pos_tiles = pos.reshape(EXPERT_TOP_K, n // rtm, rtm).transpose(1, 0, 2).reshape(
        n // rtm, 1, EXPERT_TOP_K * rtm)
    xs = _scatter(x1, pos_tiles, (starts + cnt).astype(jnp.int32), (padded - cnt).astype(jnp.int32),
                  n_active, tm=rtm, tme=tme, n_tiles=n_tiles)
    ys = _experts(xs, ffn_norm_w, tile_expert, n_active, cnt, starts, padded, w_g, w_u, w_d, tm=tme)
    return ys, pos_tiles, x1, gcol, rtm


def kernel(x, attn_norm_w, w_in, b_gate, conv_w, conv_b, dt_bias, a_log, d_skip, ssd_norm_w,
           w_ssd_out, w_attn_out, w_out, ffn_norm_w, w_group_router, b_group_router,
           w_expert_router, b_expert_router, w_exp_gate, w_exp_up, w_exp_down, final_norm_w):
    batch, seq, d = x.shape
    depth = w_in.shape[0]
    x2 = x.reshape(batch * seq, d)
    for layer in range(depth):
        ys, pos_tiles, x1, gcol, rtm = _layer(
            x2, batch, seq, layer, attn_norm_w[layer], w_in, b_gate[layer], conv_w[layer],
            conv_b[layer], dt_bias[layer], a_log[layer], d_skip[layer], ssd_norm_w[layer],
            w_ssd_out[layer], w_attn_out[layer], w_out[layer], ffn_norm_w[layer],
            w_group_router[layer], b_group_router[layer], w_expert_router[layer],
            b_expert_router[layer], w_exp_gate[layer], w_exp_up[layer], w_exp_down[layer])
        x2 = _combine(ys, pos_tiles, x1, gcol, final_norm_w, tm=rtm, final=layer == depth - 1)
    return x2.reshape(batch, seq, d)
```

```python
import functools
import math

import jax
import jax.numpy as jnp
import numpy as np
from jax import lax
from jax.experimental import pallas as pl
from jax.experimental.pallas import tpu as pltpu

F32 = jnp.float32
BF16 = jnp.bfloat16

NORM_EPS = 1e-6
SSD_HEAD_DIM = 64
SSD_N_GROUPS = 8
SSD_D_STATE = 128
SSD_CONV_WIDTH = 4
SSD_CHUNK = 128
SSD_CHUNKS_PER_STEP = 4
ATTN_HEAD_DIM = 128
DILATION_PATTERNS = ((128, 1), (512, 4), (2048, 16))
ATTN_HEADS_PER_GROUP = 4
ATTN_BLOCK = 128
ATTN_BLOCKS_PER_STEP = 8
N_EXPERT_GROUPS = 4
EXPERTS_PER_GROUP = 8
N_EXPERTS = N_EXPERT_GROUPS * EXPERTS_PER_GROUP
EXPERT_TOP_K = 2

LANES = 128
SUBLANES = 8
VMEM_LIMIT_BYTES = 56 * 1024 * 1024
ROW_DMA_UNROLL = 512

ROUTER_EXPERT_ROW0 = SUBLANES
ROUTER_ROWS = ROUTER_EXPERT_ROW0 + N_EXPERTS


def _params(*semantics):
    return pltpu.CompilerParams(dimension_semantics=semantics, vmem_limit_bytes=VMEM_LIMIT_BYTES)


def _split_bf16(v):
    hi = v.astype(BF16)
    lo = (v - hi.astype(F32)).astype(BF16)
    return hi, lo


def _pack_bf16_pair(a, b):
    hi = pltpu.bitcast(a.astype(BF16).astype(F32), jnp.uint32)
    lo = pltpu.bitcast(b.astype(BF16).astype(F32), jnp.uint32)
    return hi | (lo >> 16)


def _unpack_bf16_pair(w):
    return (pltpu.bitcast(w & jnp.uint32(0xFFFF0000), F32), pltpu.bitcast(w << 16, F32))


def _sigmoid(v):
    return 0.5 + 0.5 * jnp.tanh(0.5 * v)


def _silu(v):
    h = 0.5 * v
    return h + h * jnp.tanh(h)


def _regroup_weight_kernel(starts_ref, wt_hbm, out_ref, narrow_ref, buf_ref, nbuf_ref, sem, nsem,
                           *, layer, narrow_start, nh):
    i = pl.program_id(0)
    n_steps = pl.num_programs(0)
    tn = buf_ref.shape[1]

    def fetch(step, slot):
        rows = pl.ds(pl.multiple_of(starts_ref[step], SUBLANES), tn)
        return pltpu.make_async_copy(wt_hbm.at[layer, rows, :], buf_ref.at[slot], sem.at[slot])

    narrow = pltpu.make_async_copy(wt_hbm.at[layer, pl.ds(narrow_start, LANES), :], nbuf_ref, nsem)

    @pl.when(i == 0)
    def _():
        fetch(0, 0).start()
        narrow.start()

    @pl.when(i + 1 < n_steps)
    def _():
        fetch(i + 1, (i + 1) % 2).start()

    fetch(i, i % 2).wait()
    out_ref[...] = buf_ref[i % 2].T.astype(out_ref.dtype)

    @pl.when(i == n_steps - 1)
    def _():
        narrow.wait()
        lane = lax.broadcasted_iota(jnp.int32, narrow_ref.shape, 1)
        narrow_ref[...] = jnp.where(lane < nh, nbuf_ref[...].T, 0.0).astype(narrow_ref.dtype)


def _regroup_weight(w_stack, layer, starts, tn, narrow_start, nh):
    _, k, cols = w_stack.shape
    assert all(s % SUBLANES == 0 and s + tn <= cols for s in starts)
    assert narrow_start % SUBLANES == 0 and narrow_start + LANES <= cols and nh <= LANES
    wt = jnp.swapaxes(w_stack, 1, 2)
    grid_spec = pltpu.PrefetchScalarGridSpec(
        num_scalar_prefetch=1,
        grid=(len(starts),),
        in_specs=[pl.BlockSpec(memory_space=pl.ANY)],
        out_specs=[pl.BlockSpec((k, tn), lambda i, *_: (0, i)),
                   pl.BlockSpec((k, LANES), lambda i, *_: (0, 0))],
        scratch_shapes=[pltpu.VMEM((2, tn, k), F32), pltpu.VMEM((LANES, k), F32),
                        pltpu.SemaphoreType.DMA((2,)), pltpu.SemaphoreType.DMA(())],
    )
    return pl.pallas_call(
        functools.partial(_regroup_weight_kernel, layer=layer, narrow_start=narrow_start, nh=nh),
        grid_spec=grid_spec,
        out_shape=[jax.ShapeDtypeStruct((k, len(starts) * tn), BF16),
                   jax.ShapeDtypeStruct((k, LANES), BF16)],
        compiler_params=_params("arbitrary"),
        name="regroup_weight",
    )(jnp.asarray(starts, jnp.int32), wt)


QKV_PARTS = 3
RELAYOUT_STRIDE = 4


def _in_proj_kernel(x_hbm, nw_ref, wp_ref, w_ref, wdt_ref, proj_ref, a0_ref, a1_ref, a2_ref,
                    dt_ref, dtt_ref, h_ref, stage_ref, x_ref, xsem, *, n_plain):
    i = pl.program_id(0)
    j = pl.program_id(1)
    tm = proj_ref.shape[0]
    tn = w_ref.shape[1]

    def fetch_x(tile):
        return pltpu.make_async_copy(x_hbm.at[pl.ds(pl.multiple_of(tile * tm, tm), tm), :], x_ref, xsem)

    @pl.when((i == 0) & (j == 0))
    def _():
        fetch_x(0).start()

    @pl.when(j == 0)
    def _():
        fetch_x(i).wait()
        xf = x_ref[...]
        ms = jnp.mean(xf * xf, axis=-1, keepdims=True)
        h = (xf * lax.rsqrt(ms + NORM_EPS) * nw_ref[...]).astype(BF16)
        h_ref[...] = h
        nh = dt_ref.shape[1]
        dt_wide = jnp.dot(h, wdt_ref[...], preferred_element_type=F32)
        dt_ref[...] = dt_wide[:, :nh]
        dtt_ref[...] = dt_wide.T[:nh, :]

        @pl.when(i + 1 < pl.num_programs(0))
        def _():
            fetch_x(i + 1).start()

    @pl.when(j < n_plain)
    def _():
        proj_ref[...] = jnp.dot(h_ref[...], wp_ref[...], preferred_element_type=F32).astype(BF16)

    for gi, a_ref in enumerate((a0_ref, a1_ref, a2_ref)):
        dil = DILATION_PATTERNS[gi][1]
        j0 = n_plain + QKV_PARTS * gi

        @pl.when((j >= j0) & (j < j0 + QKV_PARTS))
        def _(a_ref=a_ref, dil=dil):
            res = jnp.dot(h_ref[...], w_ref[...], preferred_element_type=F32)
            if dil == 1:
                a_ref[...] = res.astype(BF16)
            else:
                two_pass = dil > RELAYOUT_STRIDE and dil % RELAYOUT_STRIDE == 0
                quarter = tm // RELAYOUT_STRIDE
                for s in range(tn // LANES):
                    stage_ref[0] = res[:, s * LANES:(s + 1) * LANES]
                    if two_pass:
                        for r1 in range(RELAYOUT_STRIDE):
                            stage_ref[1, r1 * quarter:(r1 + 1) * quarter, :] = stage_ref[
                                0, pl.ds(r1, quarter, stride=RELAYOUT_STRIDE), :]
                    for r in range(dil):
                        if two_pass:
                            rows = pl.ds((r % RELAYOUT_STRIDE) * quarter + r // RELAYOUT_STRIDE,
                                         tm // dil, stride=dil // RELAYOUT_STRIDE)
                        else:
                            rows = pl.ds(r, tm // dil, stride=dil)
                        c0 = r * tn + s * LANES
                        a_ref[:, c0:c0 + LANES] = stage_ref[int(two_pass), rows, :].astype(BF16)


def _in_proj(x2, norm_w, w_main, w_dt, *, tm, tn, tp, plain_cols, nh):
    n, d = x2.shape
    assert plain_cols % tp == 0 and plain_cols % tn == 0
    n_plain = plain_cols // tp
    qkv_block0 = plain_cols // tn
    n_blocks = n_plain + QKV_PARTS * len(DILATION_PATTERNS)
    assert w_main.shape[1] == plain_cols + QKV_PARTS * len(DILATION_PATTERNS) * tn

    def a_spec(gi):
        dil = DILATION_PATTERNS[gi][1]
        j0 = n_plain + QKV_PARTS * gi
        return pl.BlockSpec((tm // dil, dil * tn),
                            lambda i, j: (i, jnp.clip(j - j0, 0, QKV_PARTS - 1)))

    def a_shape(gi):
        dil = DILATION_PATTERNS[gi][1]
        return jax.ShapeDtypeStruct((n // dil, dil * QKV_PARTS * tn), BF16)

    return pl.pallas_call(
        functools.partial(_in_proj_kernel, n_plain=n_plain),
        grid=(n // tm, n_blocks),
        in_specs=[
            pl.BlockSpec(memory_space=pl.ANY),
            pl.BlockSpec((1, d), lambda i, j: (0, 0)),
            pl.BlockSpec((d, tp), lambda i, j: (0, jnp.minimum(j, n_plain - 1))),
            pl.BlockSpec((d, tn), lambda i, j: (0, qkv_block0 + jnp.maximum(j - n_plain, 0))),
            pl.BlockSpec((d, LANES), lambda i, j: (0, 0)),
        ],
        out_specs=[
            pl.BlockSpec((tm, tp), lambda i, j: (i, jnp.minimum(j, n_plain - 1))),
            a_spec(0), a_spec(1), a_spec(2),
            pl.BlockSpec((tm, nh), lambda i, j: (i, 0)),
            pl.BlockSpec((nh, tm), lambda i, j: (0, i)),
        ],
        out_shape=[
            jax.ShapeDtypeStruct((n, plain_cols), BF16),
            a_shape(0), a_shape(1), a_shape(2),
            jax.ShapeDtypeStruct((n, nh), F32),
            jax.ShapeDtypeStruct((nh, n), F32),
        ],
        scratch_shapes=[pltpu.VMEM((tm, d), BF16), pltpu.VMEM((2, tm, LANES), F32),
                        pltpu.VMEM((tm, d), x2.dtype), pltpu.SemaphoreType.DMA(())],
        compiler_params=_params("arbitrary", "arbitrary"),
        name="in_proj",
    )(x2, norm_w.reshape(1, d), w_main, w_main, w_dt)


LOG2E = math.log2(math.e)
CONV_ROW_PITCH = 2


def _ssd_kernel(xbc_ref, z_ref, dt_ref, dtt_ref, cw_ref, cb_ref, dtb_ref, dtbt_ref, alog_ref,
                alogt_ref, dskip_ref, nw_ref, expand_ref, out_ref,
                xbuf_ref, state_ref, y_ref, *, n_heads, d_inner):
    L = SSD_CHUNK
    P = SSD_HEAD_DIM
    NS = SSD_D_STATE
    G = SSD_N_GROUPS
    R = n_heads // G
    GW = R * P
    W = SSD_CONV_WIDTH
    RP = CONV_ROW_PITCH
    n_slabs = xbuf_ref.shape[0]
    c = pl.program_id(1)

    def rows(first, count):
        return pl.ds(first * RP, count, stride=RP)

    @pl.when(c == 0)
    def _():
        state_ref[...] = jnp.zeros(state_ref.shape, F32)
        for s in range(n_slabs):
            xbuf_ref[s, rows(0, SUBLANES), :] = jnp.zeros((SUBLANES, LANES), F32)

    n_rows = xbc_ref.shape[0]

    @pl.when(c > 0)
    def _():
        for s in range(n_slabs):
            xbuf_ref[s, rows(0, SUBLANES), :] = xbuf_ref[s, rows(n_rows, SUBLANES), :]

    for s in range(n_slabs):
        xbuf_ref[s, rows(SUBLANES, n_rows), :] = xbc_ref[:, s * LANES:(s + 1) * LANES].astype(F32)

    for sub in range(n_rows // L):
        _ssd_chunk(sub * L, xbuf_ref, z_ref, dt_ref, dtt_ref, cw_ref, cb_ref, dtb_ref, dtbt_ref,
                   alog_ref, alogt_ref, dskip_ref, nw_ref, expand_ref, out_ref, state_ref, y_ref,
                   n_heads=n_heads, d_inner=d_inner)


def _ssd_chunk(r0, xbuf_ref, z_ref, dt_ref, dtt_ref, cw_ref, cb_ref, dtb_ref, dtbt_ref, alog_ref,
               alogt_ref, dskip_ref, nw_ref, expand_ref, out_ref, state_ref, y_ref, *, n_heads,
               d_inner):
    L = SSD_CHUNK
    P = SSD_HEAD_DIM
    NS = SSD_D_STATE
    G = SSD_N_GROUPS
    R = n_heads // G
    GW = R * P
    W = SSD_CONV_WIDTH
    RP = CONV_ROW_PITCH
    tok = slice(r0, r0 + L)

    def rows(first, count):
        return pl.ds(first * RP, count, stride=RP)

    def conv(col0, width):
        slabs = []
        for s in range(col0 // LANES, (col0 + width) // LANES):
            acc = cb_ref[:, s * LANES:(s + 1) * LANES]
            for w in range(W):
                acc = acc + (cw_ref[w:w + 1, s * LANES:(s + 1) * LANES]
                             * xbuf_ref[s, rows(r0 + SUBLANES - (W - 1) + w, L), :])
            slabs.append(acc)
        return _silu(jnp.concatenate(slabs, axis=1))

    def softplus(v):
        return jnp.maximum(v, 0.0) + jnp.log1p(jnp.exp(-jnp.abs(v)))

    dt = softplus(dt_ref[tok, :] + dtb_ref[...])
    dtt = softplus(dtt_ref[:, tok] + dtbt_ref[...])
    da = dt * (-LOG2E * jnp.exp(alog_ref[...]))
    dat = dtt * (-LOG2E * jnp.exp(alogt_ref[...]))
    row = lax.broadcasted_iota(jnp.int32, (L, L), 0)
    col = lax.broadcasted_iota(jnp.int32, (L, L), 1)
    causal = row >= col
    tri = jnp.where(causal, 1.0, 0.0).astype(BF16)
    trit = jnp.where(row <= col, 1.0, 0.0).astype(BF16)

    def split3(v):
        a = v.astype(BF16)
        r1 = v - a.astype(F32)
        b = r1.astype(BF16)
        cc = (r1 - b.astype(F32)).astype(BF16)
        return a, b, cc

    a2 = sum(jnp.dot(tri, p, preferred_element_type=F32) for p in split3(da))
    a2t = sum(jnp.dot(p, trit, preferred_element_type=F32) for p in split3(dat))
    a2_last = a2[L - 1:L, :]

    expand = expand_ref[...]

    def expand_heads(v):
        hi, lo = _split_bf16(v)
        return jnp.dot(jnp.concatenate([hi, lo], axis=1), expand, preferred_element_type=F32)

    in_scale_e = expand_heads(dt * jnp.exp2(a2_last - a2))
    tail8 = jnp.concatenate([jnp.exp2(a2_last), dskip_ref[...],
                             jnp.zeros((SUBLANES - 2, n_heads), F32)], axis=0)
    out_e = expand_heads(jnp.concatenate([jnp.exp2(a2), tail8], axis=0))
    out_scale_e = out_e[0:L, :]
    chunk_decay_e = out_e[L:L + 1, :]
    dskip_e = out_e[L + 1:L + 2, :]

    first_head = lax.broadcasted_iota(jnp.int32, (L, LANES), 1) < P

    for g in range(G):
        x0 = g * GW
        xs = conv(x0, GW)
        bm = conv(d_inner + g * NS, NS).astype(BF16)
        cm = conv(d_inner + G * NS + g * NS, NS).astype(BF16)
        cb = lax.dot_general(cm, bm, (((1,), (1,)), ((), ())), preferred_element_type=F32)
        cb = jnp.where(causal, cb, 0.0)
        y_parts = []
        for pr in range(GW // LANES):
            wgts = []
            for q in range(LANES // P):
                h = g * R + pr * (LANES // P) + q
                seg = a2[:, h:h + 1] - a2t[h:h + 1, :]
                decay = jnp.exp2(jnp.minimum(seg, 0.0))
                wgts.append((cb * decay * dtt[h:h + 1, :]).astype(BF16))
            slab = xs[:, pr * LANES:(pr + 1) * LANES]
            rhs = jnp.concatenate([jnp.where(first_head, slab, 0.0).astype(BF16),
                                   jnp.where(first_head, 0.0, slab).astype(BF16)], axis=0)
            y_parts.append(jnp.dot(jnp.concatenate(wgts, axis=1), rhs, preferred_element_type=F32))
        y = jnp.concatenate(y_parts, axis=1) + xs * dskip_e[:, x0:x0 + GW]
        st = state_ref[:, x0:x0 + GW]
        y = y + (jnp.dot(cm, st.astype(BF16), preferred_element_type=F32)
                 * out_scale_e[:, x0:x0 + GW])
        y_ref[tok, x0:x0 + GW] = y
        xin = (xs * in_scale_e[:, x0:x0 + GW]).astype(BF16)
        st_new = lax.dot_general(bm, xin, (((0,), (0,)), ((), ())), preferred_element_type=F32)
        state_ref[:, x0:x0 + GW] = st * chunk_decay_e[:, x0:x0 + GW] + st_new

    yz = y_ref[tok, :] * _silu(z_ref[tok, :].astype(F32))
    ms = jnp.mean(yz * yz, axis=-1, keepdims=True)
    out_ref[tok, :] = (yz * lax.rsqrt(ms + NORM_EPS) * nw_ref[...]).astype(out_ref.dtype)


def _ssd(proj, dt, dtt, conv_w, conv_b, dt_bias, a_log, d_skip, norm_w, *, batch, seq, d_inner,
         xbc_block, z_block):
    n = batch * seq
    n_heads = dt.shape[1]
    L = SSD_CHUNK * math.gcd(seq // SSD_CHUNK, SSD_CHUNKS_PER_STEP)
    nc = seq // L
    conv_dim = conv_w.shape[1]
    expand = (np.arange(d_inner)[None, :] // SSD_HEAD_DIM == np.arange(n_heads)[:, None])
    expand = jnp.asarray(np.concatenate([expand, expand], axis=0), BF16)
    assert conv_dim % LANES == 0
    kern = functools.partial(_ssd_kernel, n_heads=n_heads, d_inner=d_inner)
    small = lambda shape: pl.BlockSpec(shape, lambda b, c: (0, 0))
    return pl.pallas_call(
        kern,
        grid=(batch, nc),
        in_specs=[
            pl.BlockSpec((L, conv_dim), lambda b, c: (b * nc + c, xbc_block)),
            pl.BlockSpec((L, d_inner), lambda b, c: (b * nc + c, z_block)),
            pl.BlockSpec((L, n_heads), lambda b, c: (b * nc + c, 0)),
            pl.BlockSpec((n_heads, L), lambda b, c: (0, b * nc + c)),
            small((SSD_CONV_WIDTH, conv_dim)),
            small((1, conv_dim)),
            small((1, n_heads)),
            small((n_heads, 1)),
            small((1, n_heads)),
            small((n_heads, 1)),
            small((1, n_heads)),
            small((1, d_inner)),
            small((2 * n_heads, d_inner)),
        ],
        out_specs=pl.BlockSpec((L, d_inner), lambda b, c: (b * nc + c, 0)),
        out_shape=jax.ShapeDtypeStruct((n, d_inner), BF16),
        scratch_shapes=[
            pltpu.VMEM((conv_dim // LANES, CONV_ROW_PITCH * (L + SUBLANES), LANES), F32),
            pltpu.VMEM((SSD_D_STATE, d_inner), F32),
            pltpu.VMEM((L, d_inner), F32),
        ],
        compiler_params=_params("arbitrary", "arbitrary"),
        name="ssd",
    )(proj, proj, dt, dtt, conv_w, conv_b.reshape(1, -1), dt_bias.reshape(1, -1),
      dt_bias.reshape(-1, 1), a_log.reshape(1, -1), a_log.reshape(-1, 1), d_skip.reshape(1, -1),
      norm_w.reshape(1, -1), expand)


def _attn_kernel(q_ref, k_ref, v_ref, o_ref, lse_ref, kp_ref, vp_ref, *, slopes, dilation, hops):
    nb = pl.program_id(2)
    BLK = ATTN_BLOCK
    E = ATTN_HEAD_DIM

    @pl.when(nb == 0)
    def _():
        kp_ref[...] = jnp.zeros(kp_ref.shape, kp_ref.dtype)
        vp_ref[...] = jnp.zeros(vp_ref.shape, vp_ref.dtype)

    scale = E ** -0.5
    nt = (((1,), (1,)), ((), ()))
    nh = len(slopes)
    nblk = q_ref.shape[0] // BLK
    nres = q_ref.shape[1] // (nh * E)
    q = q_ref[...]
    kk = jnp.concatenate([kp_ref[...], k_ref[...]], axis=0)
    vv = jnp.concatenate([vp_ref[...], v_ref[...]], axis=0)
    units = [(j, c) for c in range(nres * nh) for j in range(nblk)]
    cols = lambda c: slice(c * E, (c + 1) * E)
    rows = lambda j: slice(j * BLK, (j + 1) * BLK)
    qi = lax.broadcasted_iota(jnp.int32, (BLK, BLK), 0)
    ki = lax.broadcasted_iota(jnp.int32, (BLK, BLK), 1)
    rel_cur = qi - ki
    rel_prev = rel_cur + BLK
    first = rel_prev <= jnp.where(nb > 0, hops, -1)
    later = rel_prev <= hops
    ok_cur = jnp.concatenate([rel_cur >= 0 for _ in units], axis=0)
    ok_prev = jnp.concatenate([first if j == 0 else later for j, _ in units], axis=0)
    dist_cur = (rel_cur * dilation).astype(F32)
    dist_prev = (rel_prev * dilation).astype(F32)
    bias_cur = jnp.concatenate([slopes[c % nh] * dist_cur for _, c in units], axis=0)
    bias_prev = jnp.concatenate([slopes[c % nh] * dist_prev for _, c in units], axis=0)
    s_cur = jnp.concatenate(
        [lax.dot_general(q[rows(j), cols(h)], kk[rows(j + 1), cols(h)], nt,
                         preferred_element_type=F32) for j, h in units], axis=0)
    s_prev = jnp.concatenate(
        [lax.dot_general(q[rows(j), cols(h)], kk[rows(j), cols(h)], nt,
                         preferred_element_type=F32) for j, h in units], axis=0)
    l_cur = jnp.where(ok_cur, s_cur * scale - bias_cur, -jnp.inf)
    l_prev = jnp.where(ok_prev, s_prev * scale - bias_prev, -jnp.inf)
    m = jnp.max(jnp.maximum(l_cur, l_prev), axis=-1, keepdims=True)
    p_cur = jnp.exp(l_cur - m)
    p_prev = jnp.exp(l_prev - m)
    den = jnp.sum(p_cur + p_prev, axis=-1, keepdims=True)
    p_cur = p_cur.astype(BF16)
    p_prev = p_prev.astype(BF16)
    inv = 1.0 / den
    lse = m + jnp.log(den)
    lane = lax.broadcasted_iota(jnp.int32, (BLK, LANES), 1)
    for res in range(nres):
        for j in range(nblk):
            lse_tile = jnp.zeros((BLK, LANES), F32)
            for h in range(nh):
                c = res * nh + h
                u = rows(units.index((j, c)))
                acc = (jnp.dot(p_cur[u], vv[rows(j + 1), cols(c)], preferred_element_type=F32)
                       + jnp.dot(p_prev[u], vv[rows(j), cols(c)], preferred_element_type=F32))
                o_ref[rows(j), cols(c)] = (acc * inv[u]).astype(o_ref.dtype)
                lse_tile = jnp.where(lane == h, lse[u], lse_tile)
            lse_ref[rows(j), res * LANES:(res + 1) * LANES] = lse_tile
    kp_ref[...] = k_ref[rows(nblk - 1), :]
    vp_ref[...] = v_ref[rows(nblk - 1), :]


def _attn_group(qkv, gi, *, batch, seq):
    window, dilation = DILATION_PATTERNS[gi]
    hops = window // dilation
    n_heads_total = ATTN_HEADS_PER_GROUP * len(DILATION_PATTERNS)
    slopes = tuple(float(2.0 ** (-8.0 * (gi * ATTN_HEADS_PER_GROUP + h + 1) / n_heads_total))
                   for h in range(ATTN_HEADS_PER_GROUP))
    gw = ATTN_HEADS_PER_GROUP * ATTN_HEAD_DIM
    assert seq % (dilation * ATTN_BLOCK) == 0
    sub = seq // dilation
    blocks = math.gcd(sub // ATTN_BLOCK, ATTN_BLOCKS_PER_STEP)
    nres = math.gcd(dilation, ATTN_BLOCKS_PER_STEP // blocks)
    rows = ATTN_BLOCK * blocks
    nb = sub // rows
    nr = dilation // nres
    kern = functools.partial(_attn_kernel, slopes=slopes, dilation=dilation, hops=hops)

    def part(p):
        return pl.BlockSpec((rows, nres * gw), lambda b, r, n: (b * nb + n, p * nr + r))

    o, lse = pl.pallas_call(
        kern,
        grid=(batch, nr, nb),
        in_specs=[part(0), part(1), part(2)],
        out_specs=[
            pl.BlockSpec((rows, nres * gw), lambda b, r, n: (b * nb + n, r)),
            pl.BlockSpec((rows, nres * LANES), lambda b, r, n: (b * nb + n, r)),
        ],
        out_shape=[
            jax.ShapeDtypeStruct((batch * sub, dilation * gw), BF16),
            jax.ShapeDtypeStruct((batch * sub, dilation * LANES), F32),
        ],
        scratch_shapes=[pltpu.VMEM((ATTN_BLOCK, nres * gw), BF16),
                        pltpu.VMEM((ATTN_BLOCK, nres * gw), BF16)],
        compiler_params=_params("arbitrary", "arbitrary", "arbitrary"),
        name=f"attn_g{gi}",
    )(qkv, qkv, qkv)
    return o, lse


def _merge_kernel(yn_ref, o0_ref, o1_ref, o2_ref, l0_ref, l1_ref, l2_ref, gate_ref,
                  wssd_ref, wattn_ref, bg_ref, merged_ref, ostage_ref, lstage_ref, *, d_model):
    E = ATTN_HEAD_DIM
    tm = merged_ref.shape[0]
    n_slabs = o0_ref.shape[1] // LANES
    y_ssd = jnp.dot(yn_ref[...], wssd_ref[...], preferred_element_type=F32)

    def token_major(gi, o_ref, l_ref):
        dil = DILATION_PATTERNS[gi][1]
        if dil == 1:
            return o_ref[...].astype(F32), l_ref[...]
        for r in range(dil):
            rows = pl.ds(r, tm // dil, stride=dil)
            lstage_ref[gi - 1, rows, :] = l_ref[:, r * LANES:(r + 1) * LANES]
            for s in range(n_slabs):
                c0 = (r * n_slabs + s) * LANES
                ostage_ref[gi - 1, s, rows, :] = o_ref[:, c0:c0 + LANES].astype(F32)
        out = jnp.concatenate([ostage_ref[gi - 1, s] for s in range(n_slabs)], axis=1)
        return out, lstage_ref[gi - 1]

    (o0, l0), (o1, l1), (o2, l2) = (token_major(gi, o_ref, l_ref) for gi, (o_ref, l_ref) in
                                    enumerate(((o0_ref, l0_ref), (o1_ref, l1_ref), (o2_ref, l2_ref))))

    lm = jnp.maximum(jnp.maximum(l0, l1), l2)
    e0, e1, e2 = jnp.exp(l0 - lm), jnp.exp(l1 - lm), jnp.exp(l2 - lm)
    inv = 1.0 / (e0 + e1 + e2)
    parts = []
    for h in range(ATTN_HEADS_PER_GROUP):
        sl = slice(h * E, (h + 1) * E)
        parts.append((e0[:, h:h + 1] * inv[:, h:h + 1]) * o0[:, sl]
                     + (e1[:, h:h + 1] * inv[:, h:h + 1]) * o1[:, sl]
                     + (e2[:, h:h + 1] * inv[:, h:h + 1]) * o2[:, sl])
    o = jnp.concatenate(parts, axis=-1).astype(BF16)
    y_attn = jnp.dot(o, wattn_ref[...], preferred_element_type=F32)

    gates = _sigmoid(gate_ref[...].astype(F32) + bg_ref[...])
    merged_ref[...] = (gates[:, :d_model] * y_ssd + gates[:, d_model:] * y_attn).astype(BF16)


def _merge(yn, outs, lses, proj, w_ssd_out, w_attn_out, b_gate, *, tm, gate_block):
    n, d = yn.shape
    aw = outs[0].shape[1]
    dils = [dil for _, dil in DILATION_PATTERNS]
    assert dils[0] == 1 and all(tm % (dil * SUBLANES) == 0 for dil in dils)
    row = lambda w: pl.BlockSpec((tm, w), lambda i: (i, 0))
    grouped = lambda dil, w: pl.BlockSpec((tm // dil, dil * w), lambda i: (i, 0))
    const = lambda shape: pl.BlockSpec(shape, lambda i: (0, 0), pipeline_mode=pl.Buffered(1))
    return pl.pallas_call(
        functools.partial(_merge_kernel, d_model=d),
        grid=(n // tm,),
        in_specs=[
            row(d), *[grouped(dil, aw) for dil in dils], *[grouped(dil, LANES) for dil in dils],
            pl.BlockSpec((tm, 2 * d), lambda i: (i, gate_block)),
            const((d, d)), const((aw, d)), const((1, 2 * d)),
        ],
        out_specs=row(d),
        out_shape=jax.ShapeDtypeStruct((n, d), BF16),
        scratch_shapes=[pltpu.VMEM((len(dils) - 1, aw // LANES, tm, LANES), F32),
                        pltpu.VMEM((len(dils) - 1, tm, LANES), F32)],
        compiler_params=_params("arbitrary"),
        name="merge",
    )(yn, outs[0], outs[1], outs[2], lses[0], lses[1], lses[2], proj,
      w_ssd_out.astype(BF16), w_attn_out.astype(BF16), b_gate.reshape(1, -1))


def _route_kernel(merged_ref, x_ref, wout_ref, fnw_ref, wr_ref, rb_ref,
                  x1_ref, eid_ref, gcol_ref, rank_ref, cnt_ref, carry_ref):
    i = pl.program_id(0)
    tm = x_ref.shape[0]

    @pl.when(i == 0)
    def _():
        carry_ref[...] = jnp.zeros(carry_ref.shape, F32)

    x1 = x_ref[...] + jnp.dot(merged_ref[...], wout_ref[...], preferred_element_type=F32)
    x1_ref[...] = x1

    ms = jnp.mean(x1 * x1, axis=-1, keepdims=True)
    hn = x1 * lax.rsqrt(ms + NORM_EPS) * fnw_ref[...]

    hn_hi, hn_lo = _split_bf16(hn)
    by_hi = jnp.dot(hn_hi, wr_ref[...], preferred_element_type=F32)
    by_lo = jnp.dot(hn_lo, wr_ref[:, 0:LANES], preferred_element_type=F32)
    logits_tok = by_hi[:, 0:LANES] + by_hi[:, LANES:] + by_lo
    logits = logits_tok.T[0:ROUTER_ROWS, :] + rb_ref[...]

    grow = lax.broadcasted_iota(jnp.int32, (SUBLANES, tm), 0)
    gl = jnp.where(grow < N_EXPERT_GROUPS, logits[0:SUBLANES, :], -jnp.inf)
    gmax = jnp.max(gl, axis=0, keepdims=True)
    gidx = jnp.min(jnp.where(gl == gmax, grow, N_EXPERT_GROUPS), axis=0, keepdims=True)
    group_gate = 1.0 / jnp.sum(jnp.exp(gl - gmax), axis=0, keepdims=True)

    in_group = jnp.zeros((EXPERTS_PER_GROUP, tm), F32)
    for g in range(N_EXPERT_GROUPS):
        r0 = ROUTER_EXPERT_ROW0 + g * EXPERTS_PER_GROUP
        in_group = jnp.where(gidx == g, logits[r0:r0 + EXPERTS_PER_GROUP, :], in_group)
    erow = lax.broadcasted_iota(jnp.int32, in_group.shape, 0)
    v1 = jnp.max(in_group, axis=0, keepdims=True)
    i1 = jnp.min(jnp.where(in_group == v1, erow, EXPERTS_PER_GROUP), axis=0, keepdims=True)
    rest = jnp.where(erow == i1, -jnp.inf, in_group)
    v2 = jnp.max(rest, axis=0, keepdims=True)
    i2 = jnp.min(jnp.where(rest == v2, erow, EXPERTS_PER_GROUP), axis=0, keepdims=True)
    t = jnp.exp(v2 - v1)
    g1 = group_gate / (1.0 + t)
    g2 = group_gate * t / (1.0 + t)
    eid1 = gidx * EXPERTS_PER_GROUP + i1
    eid2 = gidx * EXPERTS_PER_GROUP + i2
    slot = lax.broadcasted_iota(jnp.int32, (SUBLANES, tm), 0)
    eid_ref[...] = jnp.where(slot == 0, eid1, jnp.where(slot == 1, eid2, 0))

    grow8 = lax.broadcasted_iota(jnp.int32, (LANES, tm), 0)
    gt = jnp.where(grow8 == 0, g1, jnp.where(grow8 == 1, g2, 0.0))
    gcol_ref[...] = gt.T

    xrow = lax.broadcasted_iota(jnp.int32, (N_EXPERTS, tm), 0)
    oh1 = xrow == eid1
    oh2 = xrow == eid2
    oh = jnp.where(oh1 | oh2, 1.0, 0.0)
    ti = lax.broadcasted_iota(jnp.int32, (tm, tm), 0)
    tj = lax.broadcasted_iota(jnp.int32, (tm, tm), 1)
    before = jnp.where(ti < tj, 1.0, 0.0).astype(BF16)
    prior = jnp.dot(oh.astype(BF16), before, preferred_element_type=F32) + carry_ref[:, 0:1]
    r1 = jnp.sum(jnp.where(oh1, prior, 0.0), axis=0, keepdims=True)
    r2 = jnp.sum(jnp.where(oh2, prior, 0.0), axis=0, keepdims=True)
    rank_ref[...] = jnp.where(slot == 0, r1, jnp.where(slot == 1, r2, 0.0)).astype(jnp.int32)
    carry_ref[...] = carry_ref[...] + jnp.sum(oh, axis=1, keepdims=True)
    cnt_ref[...] = carry_ref[...].astype(jnp.int32)


def _route(merged, x2, w_out, ffn_norm_w, wr, rb, *, tm):
    n, d = x2.shape
    wrc = jnp.zeros((d, LANES), F32)
    wrc = wrc.at[:, 0:N_EXPERT_GROUPS].set(wr[0]).at[:, ROUTER_EXPERT_ROW0:ROUTER_ROWS].set(wr[1])
    rbc = jnp.zeros((ROUTER_ROWS, 1), F32)
    rbc = rbc.at[0:N_EXPERT_GROUPS, 0].set(rb[0]).at[ROUTER_EXPERT_ROW0:, 0].set(rb[1])
    wr_hi_lo = jnp.concatenate(_split_bf16(wrc), axis=1)
    row = lambda w: pl.BlockSpec((tm, w), lambda i: (i, 0))
    const = lambda shape: pl.BlockSpec(shape, lambda i: (0, 0), pipeline_mode=pl.Buffered(1))
    return pl.pallas_call(
        _route_kernel,
        grid=(n // tm,),
        in_specs=[
            row(d), row(d),
            const((d, d)), const((1, d)),
            const((d, 2 * LANES)), const((ROUTER_ROWS, 1)),
        ],
        out_specs=[
            row(d),
            pl.BlockSpec((SUBLANES, tm), lambda i: (0, i)),
            row(LANES),
            pl.BlockSpec((SUBLANES, tm), lambda i: (0, i)),
            pl.BlockSpec((N_EXPERTS, LANES), lambda i: (0, 0)),
        ],
        out_shape=[
            jax.ShapeDtypeStruct((n, d), F32),
            jax.ShapeDtypeStruct((SUBLANES, n), jnp.int32),
            jax.ShapeDtypeStruct((n, LANES), F32),
            jax.ShapeDtypeStruct((SUBLANES, n), jnp.int32),
            jax.ShapeDtypeStruct((N_EXPERTS, LANES), jnp.int32),
        ],
        scratch_shapes=[pltpu.VMEM((N_EXPERTS, LANES), F32)],
        compiler_params=_params("arbitrary"),
        name="route",
    )(merged, x2, w_out.astype(BF16), ffn_norm_w.reshape(1, -1), wr_hi_lo, rbc)


def _scatter_kernel(pad_start_ref, pad_len_ref, n_active_ref, pos_ref, hn_ref, xs_ref, zero_ref,
                    sem, zsem, *, tme, n_tiles):
    tm = hn_ref.shape[0]
    zrows = zero_ref.shape[0]

    @pl.when(pl.program_id(0) == 0)
    def _():
        zero_ref[...] = jnp.zeros(zero_ref.shape, zero_ref.dtype)

        def fills(act):
            def per_expert(e, carry):
                off = pad_start_ref[e]
                left = pad_len_ref[e]
                head = left & (SUBLANES - 1)
                for r in range(SUBLANES - 1):
                    @pl.when(r < head)
                    def _(r=r):
                        act(pltpu.make_async_copy(zero_ref.at[pl.ds(0, 1), :],
                                                  xs_ref.at[pl.ds(off + r, 1), :], zsem))

                off = off + head
                bit = zrows
                while bit >= SUBLANES:
                    take = left & bit

                    @pl.when(take != 0)
                    def _(off=off, bit=bit):
                        act(pltpu.make_async_copy(
                            zero_ref.at[pl.ds(0, bit), :],
                            xs_ref.at[pl.ds(pl.multiple_of(off, SUBLANES), bit), :], zsem))

                    off = off + take
                    bit //= 2
                return carry

            def per_tile(t, carry):
                @pl.when(t >= n_active_ref[0])
                def _():
                    for part in range(tme // zrows):
                        row0 = pl.multiple_of(t * tme + part * zrows, zrows)
                        act(pltpu.make_async_copy(zero_ref, xs_ref.at[pl.ds(row0, zrows), :], zsem))

                return carry

            lax.fori_loop(0, N_EXPERTS, per_expert, 0)
            lax.fori_loop(0, n_tiles, per_tile, 0)

        fills(lambda cp: cp.start())
        fills(lambda cp: cp.wait())

    unroll = math.gcd(tm, ROW_DMA_UNROLL)

    def start(tb, carry):
        for u in range(unroll):
            t = tb * unroll + u
            for k in range(EXPERT_TOP_K):
                dst = pos_ref[0, 0, k * tm + t]
                pltpu.make_async_copy(hn_ref.at[pl.ds(t, 1), :], xs_ref.at[pl.ds(dst, 1), :],
                                      sem).start(priority=(u + k) % 2)
        return carry

    lax.fori_loop(0, tm // unroll, start, 0)
    for k in range(EXPERT_TOP_K):
        pltpu.make_async_copy(hn_ref, xs_ref.at[pl.ds(0, tm), :], sem).wait()


def _scatter(hn, pos_tiles, pad_start, pad_len, n_active, *, tm, tme, n_tiles):
    n, d = hn.shape
    assert tme % 2 == 0 and (tme // 2) & (tme // 2 - 1) == 0
    grid_spec = pltpu.PrefetchScalarGridSpec(
        num_scalar_prefetch=3,
        grid=(n // tm,),
        in_specs=[
            pl.BlockSpec((1, 1, EXPERT_TOP_K * tm), lambda i, *_: (i, 0, 0),
                         memory_space=pltpu.SMEM),
            pl.BlockSpec((tm, d), lambda i, *_: (i, 0)),
        ],
        out_specs=pl.BlockSpec(memory_space=pl.ANY),
        scratch_shapes=[pltpu.VMEM((tme // 2, d), hn.dtype), pltpu.SemaphoreType.DMA(()),
                        pltpu.SemaphoreType.DMA(())],
    )
    return pl.pallas_call(
        functools.partial(_scatter_kernel, tme=tme, n_tiles=n_tiles),
        grid_spec=grid_spec,
        out_shape=jax.ShapeDtypeStruct((n_tiles * tme, d), hn.dtype),
        compiler_params=_params("arbitrary"),
        name="scatter",
    )(pad_start, pad_len, n_active, pos_tiles, hn)


W_CHUNK_ROWS = 256
W_RING = 8


def _experts_kernel(te_ref, slot_ref, nxt_ref, lo_ref, hi_ref, na_ref, xs_ref, fnw_ref, wg_hbm,
                    wu_hbm, wd_hbm, ys_ref, wres_ref, stage_ref, sem, *, n_gu, n_r, n_h):
    i = pl.program_id(0)
    n_chunks = 2 * n_gu + n_h * n_r
    ring, cr, f = stage_ref.shape
    d = n_gu * cr

    def stage_copy(src, c):
        return pltpu.make_async_copy(src, stage_ref.at[c % ring], sem.at[c % ring])

    def start_chunk(e, c):
        @pl.when(c < n_gu)
        def _():
            stage_copy(wg_hbm.at[e, pl.ds(pl.multiple_of(c * cr, cr), cr), :], c).start()

        @pl.when((c >= n_gu) & (c < 2 * n_gu))
        def _():
            stage_copy(wu_hbm.at[e, pl.ds(pl.multiple_of((c - n_gu) * cr, cr), cr), :], c).start()

        @pl.when(c >= 2 * n_gu)
        def _():
            q = c - 2 * n_gu
            h = q // n_r
            r = q - h * n_r
            stage_copy(wd_hbm.at[e, pl.ds(pl.multiple_of(r * cr, cr), cr),
                                 pl.ds(pl.multiple_of(h * f, f), f)], c).start()

    def start_first(e):
        def body(c, carry):
            start_chunk(e, c)
            return carry

        lax.fori_loop(0, ring, body, 0)

    def convert(e, lo, hi, dst_slot):
        def body(c, carry):
            stage_copy(wg_hbm.at[0, pl.ds(0, cr), :], c).wait()
            wres_ref[dst_slot, pl.ds(pl.multiple_of(c * cr, cr), cr), :] = (
                stage_ref[c % ring].astype(BF16))

            @pl.when(c + ring < n_chunks)
            def _():
                start_chunk(e, c + ring)

            return carry

        lax.fori_loop(lo, hi, body, 0)

    @pl.when(i == 0)
    def _():
        start_first(te_ref[0])
        convert(te_ref[0], 0, n_chunks, slot_ref[0])

    @pl.when(i < na_ref[0])
    def _():
        slot = slot_ref[i]
        nxt = nxt_ref[i]

        @pl.when((nxt >= 0) & (lo_ref[i] == 0))
        def _():
            start_first(nxt)

        xr = xs_ref[...]
        ms = jnp.mean(xr * xr, axis=-1, keepdims=True)
        xb = (xr * lax.rsqrt(ms + NORM_EPS) * fnw_ref[...]).astype(BF16)
        hg = jnp.dot(xb, wres_ref[slot, 0:d, :], preferred_element_type=F32)
        hu = jnp.dot(xb, wres_ref[slot, d:2 * d, :], preferred_element_type=F32)
        hmid = (_silu(hg) * hu).astype(BF16)
        halves = []
        for h in range(n_h):
            r0 = 2 * d + h * n_r * cr
            halves.append(jnp.dot(hmid, wres_ref[slot, r0:r0 + n_r * cr, :],
                                  preferred_element_type=F32))
        for m in range(n_h // 2):
            ys_ref[:, m * f:(m + 1) * f] = _pack_bf16_pair(halves[2 * m], halves[2 * m + 1])

        @pl.when(nxt >= 0)
        def _():
            convert(nxt, lo_ref[i], hi_ref[i], 1 - slot)

    @pl.when(i >= na_ref[0])
    def _():
        ys_ref[...] = jnp.zeros(ys_ref.shape, ys_ref.dtype)


def _experts(xs, norm_w, tile_expert, n_active, cnt, starts, padded, w_g, w_u, w_d, *, tm):
    p, d = xs.shape
    ne, _, f = w_g.shape
    n_tiles = p // tm
    cr = W_CHUNK_ROWS
    assert d % cr == 0 and f % cr == 0 and d % (2 * f) == 0
    n_gu, n_r, n_h = d // cr, f // cr, d // f
    n_chunks = 2 * n_gu + n_h * n_r

    ids = jnp.arange(ne, dtype=jnp.int32)
    present = cnt > 0
    later = jnp.where(present[None, :] & (ids[None, :] > ids[:, None]), ids[None, :], ne)
    nxt_e = jnp.min(later, axis=1)
    nxt_e = jnp.where(nxt_e >= ne, -1, nxt_e)
    run_e = jnp.cumsum(present.astype(jnp.int32)) - 1
    onehot = (tile_expert[:, None] == ids[None, :]).astype(jnp.int32)
    pick = lambda v: jnp.sum(onehot * v[None, :].astype(jnp.int32), axis=1)
    tiles = jnp.arange(n_tiles, dtype=jnp.int32)
    active = tiles < n_active[0]
    j = tiles - pick(starts) // tm
    k = jnp.maximum(pick(padded) // tm, 1)
    nxt_t = jnp.where(active, pick(nxt_e), -1).astype(jnp.int32)
    lo_t = jnp.where(active, (n_chunks * j) // k, 0).astype(jnp.int32)
    hi_t = jnp.where(active, (n_chunks * (j + 1)) // k, 0).astype(jnp.int32)
    slot_t = (pick(run_e) % 2).astype(jnp.int32)

    def rows(i, te, sl, nx, lo, hi, na):
        return (jnp.minimum(i, na[0] - 1), 0)

    grid_spec = pltpu.PrefetchScalarGridSpec(
        num_scalar_prefetch=6,
        grid=(n_tiles,),
        in_specs=[
            pl.BlockSpec((tm, d), rows),
            pl.BlockSpec((1, d), lambda i, *_: (0, 0)),
            pl.BlockSpec(memory_space=pl.ANY),
            pl.BlockSpec(memory_space=pl.ANY),
            pl.BlockSpec(memory_space=pl.ANY),
        ],
        out_specs=pl.BlockSpec((tm, d // 2), lambda i, *_: (i, 0)),
        scratch_shapes=[
            pltpu.VMEM((2, n_chunks * cr, f), BF16),
            pltpu.VMEM((W_RING, cr, f), F32),
            pltpu.SemaphoreType.DMA((W_RING,)),
        ],
    )
    return pl.pallas_call(
        functools.partial(_experts_kernel, n_gu=n_gu, n_r=n_r, n_h=n_h),
        grid_spec=grid_spec,
        out_shape=jax.ShapeDtypeStruct((p, d // 2), jnp.uint32),
        compiler_params=_params("arbitrary"),
        name="experts",
    )(tile_expert, slot_t, nxt_t, lo_t, hi_t, n_active, xs, norm_w.reshape(1, d), w_g, w_u, w_d)


def _combine_kernel(pos_ref, pos_next_ref, ys_ref, x1_ref, gcol_ref, nw_ref, out_ref, buf_ref, sem,
                    *, final, pack_block):
    i = pl.program_id(0)
    n_steps = pl.num_programs(0)
    tm = x1_ref.shape[0]

    unroll = math.gcd(tm, ROW_DMA_UNROLL)

    def issue(p_ref, slot):
        def start(tb, carry):
            for u in range(unroll):
                t = tb * unroll + u
                for k in range(EXPERT_TOP_K):
                    src = p_ref[0, 0, k * tm + t]
                    pltpu.make_async_copy(ys_ref.at[pl.ds(src, 1), :],
                                          buf_ref.at[slot, k, pl.ds(t, 1), :],
                                          sem.at[slot]).start(priority=(u + k) % 2)
            return carry

        lax.fori_loop(0, tm // unroll, start, 0)

    slot = i % 2

    @pl.when(i == 0)
    def _():
        issue(pos_ref, 0)

    @pl.when(i + 1 < n_steps)
    def _():
        issue(pos_next_ref, 1 - slot)

    for k in range(EXPERT_TOP_K):
        pltpu.make_async_copy(ys_ref.at[pl.ds(0, tm), :], buf_ref.at[slot, k], sem.at[slot]).wait()
    g = gcol_ref[...]

    def expert_rows(k):
        w = buf_ref[slot, k]
        blocks = []
        for m in range(w.shape[1] // pack_block):
            blocks.extend(_unpack_bf16_pair(w[:, m * pack_block:(m + 1) * pack_block]))
        return jnp.concatenate(blocks, axis=1)

    xo = x1_ref[...] + g[:, 0:1] * expert_rows(0) + g[:, 1:2] * expert_rows(1)
    if final:
        ms = jnp.mean(xo * xo, axis=-1, keepdims=True)
        xo = xo * lax.rsqrt(ms + NORM_EPS) * nw_ref[...]
    out_ref[...] = xo


def _combine(ys, pos_tiles, x1, gcol, norm_w, *, tm, final, pack_block):
    n, d = x1.shape
    last = n // tm - 1
    return pl.pallas_call(
        functools.partial(_combine_kernel, final=final, pack_block=pack_block),
        grid=(n // tm,),
        in_specs=[
            pl.BlockSpec((1, 1, EXPERT_TOP_K * tm), lambda i: (i, 0, 0), memory_space=pltpu.SMEM),
            pl.BlockSpec((1, 1, EXPERT_TOP_K * tm), lambda i: (jnp.minimum(i + 1, last), 0, 0),
                         memory_space=pltpu.SMEM),
            pl.BlockSpec(memory_space=pl.ANY),
            pl.BlockSpec((tm, d), lambda i: (i, 0)),
            pl.BlockSpec((tm, LANES), lambda i: (i, 0)),
            pl.BlockSpec((1, d), lambda i: (0, 0)),
        ],
        out_specs=pl.BlockSpec((tm, d), lambda i: (i, 0)),
        out_shape=jax.ShapeDtypeStruct((n, d), F32),
        scratch_shapes=[pltpu.VMEM((2, EXPERT_TOP_K, tm, d // 2), ys.dtype),
                        pltpu.SemaphoreType.DMA((2,))],
        compiler_params=_params("arbitrary"),
        name="combine",
    )(pos_tiles, pos_tiles, ys, x1, gcol, norm_w.reshape(1, d))


def _tiles(n):
    return dict(proj_tm=math.gcd(n, 1024), mix_tm=math.gcd(n, 512), moe_tm=256,
                route_tm=math.gcd(n, 512))


def _layer(x2, batch, seq, layer, attn_norm_w, w_in_stack, b_gate, conv_w, conv_b, dt_bias, a_log,
           d_skip, ssd_norm_w, w_ssd_out, w_attn_out, w_out, ffn_norm_w, w_gr, b_gr, w_er, b_er,
           w_g, w_u, w_d):
    n, d = x2.shape
    n_heads = dt_bias.shape[0]
    d_inner = ssd_norm_w.shape[0]
    conv_dim = conv_w.shape[1]
    aw_total = ATTN_HEADS_PER_GROUP * len(DILATION_PATTERNS) * ATTN_HEAD_DIM
    gw = ATTN_HEADS_PER_GROUP * ATTN_HEAD_DIM
    tiles = _tiles(n)

    c_z, c_xbc, c_dt = d_inner, d_inner + conv_dim, d_inner + conv_dim + n_heads
    c_gate = c_dt + QKV_PARTS * aw_total
    segments = [(c_gate, 2 * d), (c_z, conv_dim), (0, d_inner)]
    segments += [(c_dt + p * aw_total + gi * gw, gw)
                 for gi in range(len(DILATION_PATTERNS)) for p in range(QKV_PARTS)]
    assert all(width % gw == 0 for _, width in segments)
    starts = tuple(start + b * gw for start, width in segments for b in range(width // gw))
    w_main, w_dt = _regroup_weight(w_in_stack, layer, starts, gw, c_xbc, n_heads)
    off_gate, off_xbc, off_z = 0, 2 * d, 2 * d + conv_dim
    plain_cols = off_z + d_inner
    assert off_xbc % conv_dim == 0 and off_z % d_inner == 0 and plain_cols % gw == 0

    proj, qkv0, qkv1, qkv2, dt, dtt = _in_proj(x2, attn_norm_w, w_main, w_dt, tm=tiles["proj_tm"],
                                               tn=gw, tp=4 * gw, plain_cols=plain_cols, nh=n_heads)

    yn = _ssd(proj, dt, dtt, conv_w, conv_b, dt_bias, a_log, d_skip, ssd_norm_w, batch=batch,
              seq=seq, d_inner=d_inner, xbc_block=off_xbc // conv_dim, z_block=off_z // d_inner)

    outs, lses = [], []
    for gi, qkv in enumerate((qkv0, qkv1, qkv2)):
        o_g, lse_g = _attn_group(qkv, gi, batch=batch, seq=seq)
        outs.append(o_g)
        lses.append(lse_g)

    merged = _merge(yn, outs, lses, proj, w_ssd_out, w_attn_out, b_gate, tm=tiles["mix_tm"],
                    gate_block=off_gate // (2 * d))
    x1, eid, gcol, rank, counts = _route(merged, x2, w_out, ffn_norm_w, (w_gr, w_er),
                                         (b_gr, b_er), tm=tiles["mix_tm"])

    tme = tiles["moe_tm"]
    cnt = counts[:, 0]
    padded = ((cnt + tme - 1) // tme) * tme
    ends = jnp.cumsum(padded)
    starts = ends - padded
    experts = jnp.arange(N_EXPERTS, dtype=jnp.int32)[:, None, None]
    pos = rank[:EXPERT_TOP_K] + jnp.sum(
        jnp.where(eid[None, :EXPERT_TOP_K] == experts, starts[:, None, None], 0), axis=0)
    n_tiles = EXPERT_TOP_K * n // tme + N_EXPERTS
    tile_start = jnp.arange(n_tiles, dtype=jnp.int32) * tme
    tile_expert = jnp.minimum(jnp.sum(ends[None, :] <= tile_start[:, None], axis=1),
                              N_EXPERTS - 1).astype(jnp.int32)
    n_active = (ends[-1:] // tme).astype(jnp.int32)

    rtm = tiles["route_tm"]
    pos_tiles = pos.reshape(EXPERT_TOP_K, n // rtm, rtm).transpose(1, 0, 2).reshape(
        n // rtm, 1, EXPERT_TOP_K * rtm)
    xs = _scatter(x1, pos_tiles, (starts + cnt).astype(jnp.int32), (padded - cnt).astype(jnp.int32),
                  n_active, tm=rtm, tme=tme, n_tiles=n_tiles)
    ys = _experts(xs, ffn_norm_w, tile_expert, n_active, cnt, starts, padded, w_g, w_u, w_d, tm=tme)
    return ys, pos_tiles, x1, gcol, rtm


def kernel(x, attn_norm_w, w_in, b_gate, conv_w, conv_b, dt_bias, a_log, d_skip, ssd_norm_w,
           w_ssd_out, w_attn_out, w_out, ffn_norm_w, w_group_router, b_group_router,
           w_expert_router, b_expert_router, w_exp_gate, w_exp_up, w_exp_down, final_norm_w):
    batch, seq, d = x.shape
    depth = w_in.shape[0]
    x2 = x.reshape(batch * seq, d)
    for layer in range(depth):
        ys, pos_tiles, x1, gcol, rtm = _layer(
            x2, batch, seq, layer, attn_norm_w[layer], w_in, b_gate[layer], conv_w[layer],
            conv_b[layer], dt_bias[layer], a_log[layer], d_skip[layer], ssd_norm_w[layer],
            w_ssd_out[layer], w_attn_out[layer], w_out[layer], ffn_norm_w[layer],
            w_group_router[layer], b_group_router[layer], w_expert_router[layer],
            b_expert_router[layer], w_exp_gate[layer], w_exp_up[layer], w_exp_down[layer])
        x2 = _combine(ys, pos_tiles, x1, gcol, final_norm_w, tm=rtm, final=layer == depth - 1,
                      pack_block=w_exp_gate.shape[-1])
    return x2.reshape(batch, seq, d)
```

```python
import functools
import math

import jax
import jax.numpy as jnp
import numpy as np
from jax import lax
from jax.experimental import pallas as pl
from jax.experimental.pallas import tpu as pltpu

F32 = jnp.float32
BF16 = jnp.bfloat16

NORM_EPS = 1e-6
SSD_HEAD_DIM = 64
SSD_N_GROUPS = 8
SSD_D_STATE = 128
SSD_CONV_WIDTH = 4
SSD_CHUNK = 128
SSD_CHUNKS_PER_STEP = 4
ATTN_HEAD_DIM = 128
DILATION_PATTERNS = ((128, 1), (512, 4), (2048, 16))
ATTN_HEADS_PER_GROUP = 4
ATTN_BLOCK = 128
ATTN_BLOCKS_PER_STEP = 8
N_EXPERT_GROUPS = 4
EXPERTS_PER_GROUP = 8
N_EXPERTS = N_EXPERT_GROUPS * EXPERTS_PER_GROUP
EXPERT_TOP_K = 2

LANES = 128
SUBLANES = 8
VMEM_LIMIT_BYTES = 56 * 1024 * 1024
ROW_DMA_UNROLL = 512

ROUTER_EXPERT_ROW0 = SUBLANES
ROUTER_ROWS = ROUTER_EXPERT_ROW0 + N_EXPERTS


def _params(*semantics):
    return pltpu.CompilerParams(dimension_semantics=semantics, vmem_limit_bytes=VMEM_LIMIT_BYTES)


def _split_bf16(v):
    hi = v.astype(BF16)
    lo = (v - hi.astype(F32)).astype(BF16)
    return hi, lo


def _sigmoid(v):
    return 0.5 + 0.5 * jnp.tanh(0.5 * v)


def _silu(v):
    h = 0.5 * v
    return h + h * jnp.tanh(h)


def _regroup_weight_kernel(starts_ref, wt_hbm, out_ref, narrow_ref, buf_ref, nbuf_ref, sem, nsem,
                           *, layer, narrow_start, nh):
    i = pl.program_id(0)
    n_steps = pl.num_programs(0)
    tn = buf_ref.shape[1]

    def fetch(step, slot):
        rows = pl.ds(pl.multiple_of(starts_ref[step], SUBLANES), tn)
        return pltpu.make_async_copy(wt_hbm.at[layer, rows, :], buf_ref.at[slot], sem.at[slot])

    narrow = pltpu.make_async_copy(wt_hbm.at[layer, pl.ds(narrow_start, LANES), :], nbuf_ref, nsem)

    @pl.when(i == 0)
    def _():
        fetch(0, 0).start()
        narrow.start()

    @pl.when(i + 1 < n_steps)
    def _():
        fetch(i + 1, (i + 1) % 2).start()

    fetch(i, i % 2).wait()
    out_ref[...] = buf_ref[i % 2].T.astype(out_ref.dtype)

    @pl.when(i == n_steps - 1)
    def _():
        narrow.wait()
        lane = lax.broadcasted_iota(jnp.int32, narrow_ref.shape, 1)
        narrow_ref[...] = jnp.where(lane < nh, nbuf_ref[...].T, 0.0).astype(narrow_ref.dtype)


def _regroup_weight(w_stack, layer, starts, tn, narrow_start, nh):
    _, k, cols = w_stack.shape
    assert all(s % SUBLANES == 0 and s + tn <= cols for s in starts)
    assert narrow_start % SUBLANES == 0 and narrow_start + LANES <= cols and nh <= LANES
    wt = jnp.swapaxes(w_stack, 1, 2)
    grid_spec = pltpu.PrefetchScalarGridSpec(
        num_scalar_prefetch=1,
        grid=(len(starts),),
        in_specs=[pl.BlockSpec(memory_space=pl.ANY)],
        out_specs=[pl.BlockSpec((k, tn), lambda i, *_: (0, i)),
                   pl.BlockSpec((k, LANES), lambda i, *_: (0, 0))],
        scratch_shapes=[pltpu.VMEM((2, tn, k), F32), pltpu.VMEM((LANES, k), F32),
                        pltpu.SemaphoreType.DMA((2,)), pltpu.SemaphoreType.DMA(())],
    )
    return pl.pallas_call(
        functools.partial(_regroup_weight_kernel, layer=layer, narrow_start=narrow_start, nh=nh),
        grid_spec=grid_spec,
        out_shape=[jax.ShapeDtypeStruct((k, len(starts) * tn), BF16),
                   jax.ShapeDtypeStruct((k, LANES), BF16)],
        compiler_params=_params("arbitrary"),
        name="regroup_weight",
    )(jnp.asarray(starts, jnp.int32), wt)


QKV_PARTS = 3
RELAYOUT_STRIDE = 4


def _in_proj_kernel(x_hbm, nw_ref, wp_ref, w_ref, wdt_ref, proj_ref, a0_ref, a1_ref, a2_ref,
                    dt_ref, dtt_ref, h_ref, stage_ref, x_ref, xsem, *, n_plain):
    i = pl.program_id(0)
    j = pl.program_id(1)
    tm = proj_ref.shape[0]
    tn = w_ref.shape[1]

    def fetch_x(tile):
        return pltpu.make_async_copy(x_hbm.at[pl.ds(pl.multiple_of(tile * tm, tm), tm), :], x_ref, xsem)

    @pl.when((i == 0) & (j == 0))
    def _():
        fetch_x(0).start()

    @pl.when(j == 0)
    def _():
        fetch_x(i).wait()
        xf = x_ref[...]
        ms = jnp.mean(xf * xf, axis=-1, keepdims=True)
        h = (xf * lax.rsqrt(ms + NORM_EPS) * nw_ref[...]).astype(BF16)
        h_ref[...] = h
        nh = dt_ref.shape[1]
        dt_wide = jnp.dot(h, wdt_ref[...], preferred_element_type=F32)
        dt_ref[...] = dt_wide[:, :nh]
        dtt_ref[...] = dt_wide.T[:nh, :]

        @pl.when(i + 1 < pl.num_programs(0))
        def _():
            fetch_x(i + 1).start()

    @pl.when(j < n_plain)
    def _():
        proj_ref[...] = jnp.dot(h_ref[...], wp_ref[...], preferred_element_type=F32).astype(BF16)

    for gi, a_ref in enumerate((a0_ref, a1_ref, a2_ref)):
        dil = DILATION_PATTERNS[gi][1]
        j0 = n_plain + QKV_PARTS * gi

        @pl.when((j >= j0) & (j < j0 + QKV_PARTS))
        def _(a_ref=a_ref, dil=dil):
            res = jnp.dot(h_ref[...], w_ref[...], preferred_element_type=F32)
            if dil == 1:
                a_ref[...] = res.astype(BF16)
            else:
                two_pass = dil > RELAYOUT_STRIDE and dil % RELAYOUT_STRIDE == 0
                quarter = tm // RELAYOUT_STRIDE
                for s in range(tn // LANES):
                    stage_ref[0] = res[:, s * LANES:(s + 1) * LANES]
                    if two_pass:
                        for r1 in range(RELAYOUT_STRIDE):
                            stage_ref[1, r1 * quarter:(r1 + 1) * quarter, :] = stage_ref[
                                0, pl.ds(r1, quarter, stride=RELAYOUT_STRIDE), :]
                    for r in range(dil):
                        if two_pass:
                            rows = pl.ds((r % RELAYOUT_STRIDE) * quarter + r // RELAYOUT_STRIDE,
                                         tm // dil, stride=dil // RELAYOUT_STRIDE)
                        else:
                            rows = pl.ds(r, tm // dil, stride=dil)
                        c0 = r * tn + s * LANES
                        a_ref[:, c0:c0 + LANES] = stage_ref[int(two_pass), rows, :].astype(BF16)


def _in_proj(x2, norm_w, w_main, w_dt, *, tm, tn, tp, plain_cols, nh):
    n, d = x2.shape
    assert plain_cols % tp == 0 and plain_cols % tn == 0
    n_plain = plain_cols // tp
    qkv_block0 = plain_cols // tn
    n_blocks = n_plain + QKV_PARTS * len(DILATION_PATTERNS)
    assert w_main.shape[1] == plain_cols + QKV_PARTS * len(DILATION_PATTERNS) * tn

    def a_spec(gi):
        dil = DILATION_PATTERNS[gi][1]
        j0 = n_plain + QKV_PARTS * gi
        return pl.BlockSpec((tm // dil, dil * tn),
                            lambda i, j: (i, jnp.clip(j - j0, 0, QKV_PARTS - 1)))

    def a_shape(gi):
        dil = DILATION_PATTERNS[gi][1]
        return jax.ShapeDtypeStruct((n // dil, dil * QKV_PARTS * tn), BF16)

    return pl.pallas_call(
        functools.partial(_in_proj_kernel, n_plain=n_plain),
        grid=(n // tm, n_blocks),
        in_specs=[
            pl.BlockSpec(memory_space=pl.ANY),
            pl.BlockSpec((1, d), lambda i, j: (0, 0)),
            pl.BlockSpec((d, tp), lambda i, j: (0, jnp.minimum(j, n_plain - 1))),
            pl.BlockSpec((d, tn), lambda i, j: (0, qkv_block0 + jnp.maximum(j - n_plain, 0))),
            pl.BlockSpec((d, LANES), lambda i, j: (0, 0)),
        ],
        out_specs=[
            pl.BlockSpec((tm, tp), lambda i, j: (i, jnp.minimum(j, n_plain - 1))),
            a_spec(0), a_spec(1), a_spec(2),
            pl.BlockSpec((tm, nh), lambda i, j: (i, 0)),
            pl.BlockSpec((nh, tm), lambda i, j: (0, i)),
        ],
        out_shape=[
            jax.ShapeDtypeStruct((n, plain_cols), BF16),
            a_shape(0), a_shape(1), a_shape(2),
            jax.ShapeDtypeStruct((n, nh), F32),
            jax.ShapeDtypeStruct((nh, n), F32),
        ],
        scratch_shapes=[pltpu.VMEM((tm, d), BF16), pltpu.VMEM((2, tm, LANES), F32),
                        pltpu.VMEM((tm, d), x2.dtype), pltpu.SemaphoreType.DMA(())],
        compiler_params=_params("arbitrary", "arbitrary"),
        name="in_proj",
    )(x2, norm_w.reshape(1, d), w_main, w_main, w_dt)


LOG2E = math.log2(math.e)
CONV_ROW_PITCH = 2


def _ssd_kernel(xbc_ref, z_ref, dt_ref, dtt_ref, cw_ref, cb_ref, dtb_ref, dtbt_ref, alog_ref,
                alogt_ref, dskip_ref, nw_ref, expand_ref, out_ref,
                xbuf_ref, state_ref, y_ref, *, n_heads, d_inner):
    L = SSD_CHUNK
    P = SSD_HEAD_DIM
    NS = SSD_D_STATE
    G = SSD_N_GROUPS
    R = n_heads // G
    GW = R * P
    W = SSD_CONV_WIDTH
    RP = CONV_ROW_PITCH
    n_slabs = xbuf_ref.shape[0]
    c = pl.program_id(1)

    def rows(first, count):
        return pl.ds(first * RP, count, stride=RP)

    @pl.when(c == 0)
    def _():
        state_ref[...] = jnp.zeros(state_ref.shape, F32)
        for s in range(n_slabs):
            xbuf_ref[s, rows(0, SUBLANES), :] = jnp.zeros((SUBLANES, LANES), F32)

    n_rows = xbc_ref.shape[0]

    @pl.when(c > 0)
    def _():
        for s in range(n_slabs):
            xbuf_ref[s, rows(0, SUBLANES), :] = xbuf_ref[s, rows(n_rows, SUBLANES), :]

    for s in range(n_slabs):
        xbuf_ref[s, rows(SUBLANES, n_rows), :] = xbc_ref[:, s * LANES:(s + 1) * LANES].astype(F32)

    for sub in range(n_rows // L):
        _ssd_chunk(sub * L, xbuf_ref, z_ref, dt_ref, dtt_ref, cw_ref, cb_ref, dtb_ref, dtbt_ref,
                   alog_ref, alogt_ref, dskip_ref, nw_ref, expand_ref, out_ref, state_ref, y_ref,
                   n_heads=n_heads, d_inner=d_inner)


def _ssd_chunk(r0, xbuf_ref, z_ref, dt_ref, dtt_ref, cw_ref, cb_ref, dtb_ref, dtbt_ref, alog_ref,
               alogt_ref, dskip_ref, nw_ref, expand_ref, out_ref, state_ref, y_ref, *, n_heads,
               d_inner):
    L = SSD_CHUNK
    P = SSD_HEAD_DIM
    NS = SSD_D_STATE
    G = SSD_N_GROUPS
    R = n_heads // G
    GW = R * P
    W = SSD_CONV_WIDTH
    RP = CONV_ROW_PITCH
    tok = slice(r0, r0 + L)

    def rows(first, count):
        return pl.ds(first * RP, count, stride=RP)

    def conv(col0, width):
        slabs = []
        for s in range(col0 // LANES, (col0 + width) // LANES):
            acc = cb_ref[:, s * LANES:(s + 1) * LANES]
            for w in range(W):
                acc = acc + (cw_ref[w:w + 1, s * LANES:(s + 1) * LANES]
                             * xbuf_ref[s, rows(r0 + SUBLANES - (W - 1) + w, L), :])
            slabs.append(acc)
        return _silu(jnp.concatenate(slabs, axis=1))

    def softplus(v):
        return jnp.maximum(v, 0.0) + jnp.log1p(jnp.exp(-jnp.abs(v)))

    dt = softplus(dt_ref[tok, :] + dtb_ref[...])
    dtt = softplus(dtt_ref[:, tok] + dtbt_ref[...])
    da = dt * (-LOG2E * jnp.exp(alog_ref[...]))
    dat = dtt * (-LOG2E * jnp.exp(alogt_ref[...]))
    row = lax.broadcasted_iota(jnp.int32, (L, L), 0)
    col = lax.broadcasted_iota(jnp.int32, (L, L), 1)
    causal = row >= col
    tri = jnp.where(causal, 1.0, 0.0).astype(BF16)
    trit = jnp.where(row <= col, 1.0, 0.0).astype(BF16)

    def split3(v):
        a = v.astype(BF16)
        r1 = v - a.astype(F32)
        b = r1.astype(BF16)
        cc = (r1 - b.astype(F32)).astype(BF16)
        return a, b, cc

    a2 = sum(jnp.dot(tri, p, preferred_element_type=F32) for p in split3(da))
    a2t = sum(jnp.dot(p, trit, preferred_element_type=F32) for p in split3(dat))
    a2_last = a2[L - 1:L, :]

    expand = expand_ref[...]

    def expand_heads(v):
        hi, lo = _split_bf16(v)
        return jnp.dot(jnp.concatenate([hi, lo], axis=1), expand, preferred_element_type=F32)

    in_scale_e = expand_heads(dt * jnp.exp2(a2_last - a2))
    tail8 = jnp.concatenate([jnp.exp2(a2_last), dskip_ref[...],
                             jnp.zeros((SUBLANES - 2, n_heads), F32)], axis=0)
    out_e = expand_heads(jnp.concatenate([jnp.exp2(a2), tail8], axis=0))
    out_scale_e = out_e[0:L, :]
    chunk_decay_e = out_e[L:L + 1, :]
    dskip_e = out_e[L + 1:L + 2, :]

    first_head = lax.broadcasted_iota(jnp.int32, (L, LANES), 1) < P

    for g in range(G):
        x0 = g * GW
        xs = conv(x0, GW)
        bm = conv(d_inner + g * NS, NS).astype(BF16)
        cm = conv(d_inner + G * NS + g * NS, NS).astype(BF16)
        cb = lax.dot_general(cm, bm, (((1,), (1,)), ((), ())), preferred_element_type=F32)
        cb = jnp.where(causal, cb, 0.0)
        y_parts = []
        for pr in range(GW // LANES):
            wgts = []
            for q in range(LANES // P):
                h = g * R + pr * (LANES // P) + q
                seg = a2[:, h:h + 1] - a2t[h:h + 1, :]
                decay = jnp.exp2(jnp.minimum(seg, 0.0))
                wgts.append((cb * decay * dtt[h:h + 1, :]).astype(BF16))
            slab = xs[:, pr * LANES:(pr + 1) * LANES]
            rhs = jnp.concatenate([jnp.where(first_head, slab, 0.0).astype(BF16),
                                   jnp.where(first_head, 0.0, slab).astype(BF16)], axis=0)
            y_parts.append(jnp.dot(jnp.concatenate(wgts, axis=1), rhs, preferred_element_type=F32))
        y = jnp.concatenate(y_parts, axis=1) + xs * dskip_e[:, x0:x0 + GW]
        st = state_ref[:, x0:x0 + GW]
        y = y + (jnp.dot(cm, st.astype(BF16), preferred_element_type=F32)
                 * out_scale_e[:, x0:x0 + GW])
        y_ref[tok, x0:x0 + GW] = y
        xin = (xs * in_scale_e[:, x0:x0 + GW]).astype(BF16)
        st_new = lax.dot_general(bm, xin, (((0,), (0,)), ((), ())), preferred_element_type=F32)
        state_ref[:, x0:x0 + GW] = st * chunk_decay_e[:, x0:x0 + GW] + st_new

    yz = y_ref[tok, :] * _silu(z_ref[tok, :].astype(F32))
    ms = jnp.mean(yz * yz, axis=-1, keepdims=True)
    out_ref[tok, :] = (yz * lax.rsqrt(ms + NORM_EPS) * nw_ref[...]).astype(out_ref.dtype)


def _ssd(proj, dt, dtt, conv_w, conv_b, dt_bias, a_log, d_skip, norm_w, *, batch, seq, d_inner,
         xbc_block, z_block):
    n = batch * seq
    n_heads = dt.shape[1]
    L = SSD_CHUNK * math.gcd(seq // SSD_CHUNK, SSD_CHUNKS_PER_STEP)
    nc = seq // L
    conv_dim = conv_w.shape[1]
    expand = (np.arange(d_inner)[None, :] // SSD_HEAD_DIM == np.arange(n_heads)[:, None])
    expand = jnp.asarray(np.concatenate([expand, expand], axis=0), BF16)
    assert conv_dim % LANES == 0
    kern = functools.partial(_ssd_kernel, n_heads=n_heads, d_inner=d_inner)
    small = lambda shape: pl.BlockSpec(shape, lambda b, c: (0, 0))
    return pl.pallas_call(
        kern,
        grid=(batch, nc),
        in_specs=[
            pl.BlockSpec((L, conv_dim), lambda b, c: (b * nc + c, xbc_block)),
            pl.BlockSpec((L, d_inner), lambda b, c: (b * nc + c, z_block)),
            pl.BlockSpec((L, n_heads), lambda b, c: (b * nc + c, 0)),
            pl.BlockSpec((n_heads, L), lambda b, c: (0, b * nc + c)),
            small((SSD_CONV_WIDTH, conv_dim)),
            small((1, conv_dim)),
            small((1, n_heads)),
            small((n_heads, 1)),
            small((1, n_heads)),
            small((n_heads, 1)),
            small((1, n_heads)),
            small((1, d_inner)),
            small((2 * n_heads, d_inner)),
        ],
        out_specs=pl.BlockSpec((L, d_inner), lambda b, c: (b * nc + c, 0)),
        out_shape=jax.ShapeDtypeStruct((n, d_inner), BF16),
        scratch_shapes=[
            pltpu.VMEM((conv_dim // LANES, CONV_ROW_PITCH * (L + SUBLANES), LANES), F32),
            pltpu.VMEM((SSD_D_STATE, d_inner), F32),
            pltpu.VMEM((L, d_inner), F32),
        ],
        compiler_params=_params("arbitrary", "arbitrary"),
        name="ssd",
    )(proj, proj, dt, dtt, conv_w, conv_b.reshape(1, -1), dt_bias.reshape(1, -1),
      dt_bias.reshape(-1, 1), a_log.reshape(1, -1), a_log.reshape(-1, 1), d_skip.reshape(1, -1),
      norm_w.reshape(1, -1), expand)


def _attn_kernel(q_ref, k_ref, v_ref, o_ref, lse_ref, kp_ref, vp_ref, *, slopes, dilation, hops):
    nb = pl.program_id(2)
    BLK = ATTN_BLOCK
    E = ATTN_HEAD_DIM

    @pl.when(nb == 0)
    def _():
        kp_ref[...] = jnp.zeros(kp_ref.shape, kp_ref.dtype)
        vp_ref[...] = jnp.zeros(vp_ref.shape, vp_ref.dtype)

    scale = E ** -0.5
    nt = (((1,), (1,)), ((), ()))
    nh = len(slopes)
    nblk = q_ref.shape[0] // BLK
    nres = q_ref.shape[1] // (nh * E)
    q = q_ref[...]
    kk = jnp.concatenate([kp_ref[...], k_ref[...]], axis=0)
    vv = jnp.concatenate([vp_ref[...], v_ref[...]], axis=0)
    units = [(j, c) for c in range(nres * nh) for j in range(nblk)]
    cols = lambda c: slice(c * E, (c + 1) * E)
    rows = lambda j: slice(j * BLK, (j + 1) * BLK)
    qi = lax.broadcasted_iota(jnp.int32, (BLK, BLK), 0)
    ki = lax.broadcasted_iota(jnp.int32, (BLK, BLK), 1)
    rel_cur = qi - ki
    rel_prev = rel_cur + BLK
    first = rel_prev <= jnp.where(nb > 0, hops, -1)
    later = rel_prev <= hops
    ok_cur = jnp.concatenate([rel_cur >= 0 for _ in units], axis=0)
    ok_prev = jnp.concatenate([first if j == 0 else later for j, _ in units], axis=0)
    dist_cur = (rel_cur * dilation).astype(F32)
    dist_prev = (rel_prev * dilation).astype(F32)
    bias_cur = jnp.concatenate([slopes[c % nh] * dist_cur for _, c in units], axis=0)
    bias_prev = jnp.concatenate([slopes[c % nh] * dist_prev for _, c in units], axis=0)
    s_cur = jnp.concatenate(
        [lax.dot_general(q[rows(j), cols(h)], kk[rows(j + 1), cols(h)], nt,
                         preferred_element_type=F32) for j, h in units], axis=0)
    s_prev = jnp.concatenate(
        [lax.dot_general(q[rows(j), cols(h)], kk[rows(j), cols(h)], nt,
                         preferred_element_type=F32) for j, h in units], axis=0)
    l_cur = jnp.where(ok_cur, s_cur * scale - bias_cur, -jnp.inf)
    l_prev = jnp.where(ok_prev, s_prev * scale - bias_prev, -jnp.inf)
    m = jnp.max(jnp.maximum(l_cur, l_prev), axis=-1, keepdims=True)
    p_cur = jnp.exp(l_cur - m)
    p_prev = jnp.exp(l_prev - m)
    den = jnp.sum(p_cur + p_prev, axis=-1, keepdims=True)
    p_cur = p_cur.astype(BF16)
    p_prev = p_prev.astype(BF16)
    inv = 1.0 / den
    lse = m + jnp.log(den)
    lane = lax.broadcasted_iota(jnp.int32, (BLK, LANES), 1)
    for res in range(nres):
        for j in range(nblk):
            lse_tile = jnp.zeros((BLK, LANES), F32)
            for h in range(nh):
                c = res * nh + h
                u = rows(units.index((j, c)))
                acc = (jnp.dot(p_cur[u], vv[rows(j + 1), cols(c)], preferred_element_type=F32)
                       + jnp.dot(p_prev[u], vv[rows(j), cols(c)], preferred_element_type=F32))
                o_ref[rows(j), cols(c)] = (acc * inv[u]).astype(o_ref.dtype)
                lse_tile = jnp.where(lane == h, lse[u], lse_tile)
            lse_ref[rows(j), res * LANES:(res + 1) * LANES] = lse_tile
    kp_ref[...] = k_ref[rows(nblk - 1), :]
    vp_ref[...] = v_ref[rows(nblk - 1), :]


def _attn_group(qkv, gi, *, batch, seq):
    window, dilation = DILATION_PATTERNS[gi]
    hops = window // dilation
    n_heads_total = ATTN_HEADS_PER_GROUP * len(DILATION_PATTERNS)
    slopes = tuple(float(2.0 ** (-8.0 * (gi * ATTN_HEADS_PER_GROUP + h + 1) / n_heads_total))
                   for h in range(ATTN_HEADS_PER_GROUP))
    gw = ATTN_HEADS_PER_GROUP * ATTN_HEAD_DIM
    assert seq % (dilation * ATTN_BLOCK) == 0
    sub = seq // dilation
    blocks = math.gcd(sub // ATTN_BLOCK, ATTN_BLOCKS_PER_STEP)
    nres = math.gcd(dilation, ATTN_BLOCKS_PER_STEP // blocks)
    rows = ATTN_BLOCK * blocks
    nb = sub // rows
    nr = dilation // nres
    kern = functools.partial(_attn_kernel, slopes=slopes, dilation=dilation, hops=hops)

    def part(p):
        return pl.BlockSpec((rows, nres * gw), lambda b, r, n: (b * nb + n, p * nr + r))

    o, lse = pl.pallas_call(
        kern,
        grid=(batch, nr, nb),
        in_specs=[part(0), part(1), part(2)],
        out_specs=[
            pl.BlockSpec((rows, nres * gw), lambda b, r, n: (b * nb + n, r)),
            pl.BlockSpec((rows, nres * LANES), lambda b, r, n: (b * nb + n, r)),
        ],
        out_shape=[
            jax.ShapeDtypeStruct((batch * sub, dilation * gw), BF16),
            jax.ShapeDtypeStruct((batch * sub, dilation * LANES), F32),
        ],
        scratch_shapes=[pltpu.VMEM((ATTN_BLOCK, nres * gw), BF16),
                        pltpu.VMEM((ATTN_BLOCK, nres * gw), BF16)],
        compiler_params=_params("arbitrary", "arbitrary", "arbitrary"),
        name=f"attn_g{gi}",
    )(qkv, qkv, qkv)
    return o, lse


def _merge_kernel(yn_ref, o0_ref, o1_ref, o2_ref, l0_ref, l1_ref, l2_ref, gate_ref,
                  wssd_ref, wattn_ref, bg_ref, merged_ref, ostage_ref, lstage_ref, *, d_model):
    E = ATTN_HEAD_DIM
    tm = merged_ref.shape[0]
    n_slabs = o0_ref.shape[1] // LANES
    y_ssd = jnp.dot(yn_ref[...], wssd_ref[...], preferred_element_type=F32)

    def token_major(gi, o_ref, l_ref):
        dil = DILATION_PATTERNS[gi][1]
        if dil == 1:
            return o_ref[...].astype(F32), l_ref[...]
        for r in range(dil):
            rows = pl.ds(r, tm // dil, stride=dil)
            lstage_ref[gi - 1, rows, :] = l_ref[:, r * LANES:(r + 1) * LANES]
            for s in range(n_slabs):
                c0 = (r * n_slabs + s) * LANES
                ostage_ref[gi - 1, s, rows, :] = o_ref[:, c0:c0 + LANES].astype(F32)
        out = jnp.concatenate([ostage_ref[gi - 1, s] for s in range(n_slabs)], axis=1)
        return out, lstage_ref[gi - 1]

    (o0, l0), (o1, l1), (o2, l2) = (token_major(gi, o_ref, l_ref) for gi, (o_ref, l_ref) in
                                    enumerate(((o0_ref, l0_ref), (o1_ref, l1_ref), (o2_ref, l2_ref))))

    lm = jnp.maximum(jnp.maximum(l0, l1), l2)
    e0, e1, e2 = jnp.exp(l0 - lm), jnp.exp(l1 - lm), jnp.exp(l2 - lm)
    inv = 1.0 / (e0 + e1 + e2)
    parts = []
    for h in range(ATTN_HEADS_PER_GROUP):
        sl = slice(h * E, (h + 1) * E)
        parts.append((e0[:, h:h + 1] * inv[:, h:h + 1]) * o0[:, sl]
                     + (e1[:, h:h + 1] * inv[:, h:h + 1]) * o1[:, sl]
                     + (e2[:, h:h + 1] * inv[:, h:h + 1]) * o2[:, sl])
    o = jnp.concatenate(parts, axis=-1).astype(BF16)
    y_attn = jnp.dot(o, wattn_ref[...], preferred_element_type=F32)

    gates = _sigmoid(gate_ref[...].astype(F32) + bg_ref[...])
    merged_ref[...] = (gates[:, :d_model] * y_ssd + gates[:, d_model:] * y_attn).astype(BF16)


def _merge(yn, outs, lses, proj, w_ssd_out, w_attn_out, b_gate, *, tm, gate_block):
    n, d = yn.shape
    aw = outs[0].shape[1]
    dils = [dil for _, dil in DILATION_PATTERNS]
    assert dils[0] == 1 and all(tm % (dil * SUBLANES) == 0 for dil in dils)
    row = lambda w: pl.BlockSpec((tm, w), lambda i: (i, 0))
    grouped = lambda dil, w: pl.BlockSpec((tm // dil, dil * w), lambda i: (i, 0))
    const = lambda shape: pl.BlockSpec(shape, lambda i: (0, 0), pipeline_mode=pl.Buffered(1))
    return pl.pallas_call(
        functools.partial(_merge_kernel, d_model=d),
        grid=(n // tm,),
        in_specs=[
            row(d), *[grouped(dil, aw) for dil in dils], *[grouped(dil, LANES) for dil in dils],
            pl.BlockSpec((tm, 2 * d), lambda i: (i, gate_block)),
            const((d, d)), const((aw, d)), const((1, 2 * d)),
        ],
        out_specs=row(d),
        out_shape=jax.ShapeDtypeStruct((n, d), BF16),
        scratch_shapes=[pltpu.VMEM((len(dils) - 1, aw // LANES, tm, LANES), F32),
                        pltpu.VMEM((len(dils) - 1, tm, LANES), F32)],
        compiler_params=_params("arbitrary"),
        name="merge",
    )(yn, outs[0], outs[1], outs[2], lses[0], lses[1], lses[2], proj,
      w_ssd_out.astype(BF16), w_attn_out.astype(BF16), b_gate.reshape(1, -1))


def _route_kernel(merged_ref, x_ref, wout_ref, fnw_ref, wr_ref, rb_ref,
                  x1_ref, eid_ref, gcol_ref, rank_ref, cnt_ref, carry_ref):
    i = pl.program_id(0)
    tm = x_ref.shape[0]

    @pl.when(i == 0)
    def _():
        carry_ref[...] = jnp.zeros(carry_ref.shape, F32)

    x1 = x_ref[...] + jnp.dot(merged_ref[...], wout_ref[...], preferred_element_type=F32)
    x1_ref[...] = x1

    ms = jnp.mean(x1 * x1, axis=-1, keepdims=True)
    hn = x1 * lax.rsqrt(ms + NORM_EPS) * fnw_ref[...]

    hn_hi, hn_lo = _split_bf16(hn)
    by_hi = jnp.dot(hn_hi, wr_ref[...], preferred_element_type=F32)
    by_lo = jnp.dot(hn_lo, wr_ref[:, 0:LANES], preferred_element_type=F32)
    logits_tok = by_hi[:, 0:LANES] + by_hi[:, LANES:] + by_lo
    logits = logits_tok.T[0:ROUTER_ROWS, :] + rb_ref[...]

    grow = lax.broadcasted_iota(jnp.int32, (SUBLANES, tm), 0)
    gl = jnp.where(grow < N_EXPERT_GROUPS, logits[0:SUBLANES, :], -jnp.inf)
    gmax = jnp.max(gl, axis=0, keepdims=True)
    gidx = jnp.min(jnp.where(gl == gmax, grow, N_EXPERT_GROUPS), axis=0, keepdims=True)
    group_gate = 1.0 / jnp.sum(jnp.exp(gl - gmax), axis=0, keepdims=True)

    in_group = jnp.zeros((EXPERTS_PER_GROUP, tm), F32)
    for g in range(N_EXPERT_GROUPS):
        r0 = ROUTER_EXPERT_ROW0 + g * EXPERTS_PER_GROUP
        in_group = jnp.where(gidx == g, logits[r0:r0 + EXPERTS_PER_GROUP, :], in_group)
    erow = lax.broadcasted_iota(jnp.int32, in_group.shape, 0)
    v1 = jnp.max(in_group, axis=0, keepdims=True)
    i1 = jnp.min(jnp.where(in_group == v1, erow, EXPERTS_PER_GROUP), axis=0, keepdims=True)
    rest = jnp.where(erow == i1, -jnp.inf, in_group)
    v2 = jnp.max(rest, axis=0, keepdims=True)
    i2 = jnp.min(jnp.where(rest == v2, erow, EXPERTS_PER_GROUP), axis=0, keepdims=True)
    t = jnp.exp(v2 - v1)
    g1 = group_gate / (1.0 + t)
    g2 = group_gate * t / (1.0 + t)
    eid1 = gidx * EXPERTS_PER_GROUP + i1
    eid2 = gidx * EXPERTS_PER_GROUP + i2
    slot = lax.broadcasted_iota(jnp.int32, (SUBLANES, tm), 0)
    eid_ref[...] = jnp.where(slot == 0, eid1, jnp.where(slot == 1, eid2, 0))

    grow8 = lax.broadcasted_iota(jnp.int32, (LANES, tm), 0)
    gt = jnp.where(grow8 == 0, g1, jnp.where(grow8 == 1, g2, 0.0))
    gcol_ref[...] = gt.T

    xrow = lax.broadcasted_iota(jnp.int32, (N_EXPERTS, tm), 0)
    oh1 = xrow == eid1
    oh2 = xrow == eid2
    oh = jnp.where(oh1 | oh2, 1.0, 0.0)
    ti = lax.broadcasted_iota(jnp.int32, (tm, tm), 0)
    tj = lax.broadcasted_iota(jnp.int32, (tm, tm), 1)
    before = jnp.where(ti < tj, 1.0, 0.0).astype(BF16)
    prior = jnp.dot(oh.astype(BF16), before, preferred_element_type=F32) + carry_ref[:, 0:1]
    r1 = jnp.sum(jnp.where(oh1, prior, 0.0), axis=0, keepdims=True)
    r2 = jnp.sum(jnp.where(oh2, prior, 0.0), axis=0, keepdims=True)
    rank_ref[...] = jnp.where(slot == 0, r1, jnp.where(slot == 1, r2, 0.0)).astype(jnp.int32)
    carry_ref[...] = carry_ref[...] + jnp.sum(oh, axis=1, keepdims=True)
    cnt_ref[...] = carry_ref[...].astype(jnp.int32)


def _route(merged, x2, w_out, ffn_norm_w, wr, rb, *, tm):
    n, d = x2.shape
    wrc = jnp.zeros((d, LANES), F32)
    wrc = wrc.at[:, 0:N_EXPERT_GROUPS].set(wr[0]).at[:, ROUTER_EXPERT_ROW0:ROUTER_ROWS].set(wr[1])
    rbc = jnp.zeros((ROUTER_ROWS, 1), F32)
    rbc = rbc.at[0:N_EXPERT_GROUPS, 0].set(rb[0]).at[ROUTER_EXPERT_ROW0:, 0].set(rb[1])
    wr_hi_lo = jnp.concatenate(_split_bf16(wrc), axis=1)
    row = lambda w: pl.BlockSpec((tm, w), lambda i: (i, 0))
    const = lambda shape: pl.BlockSpec(shape, lambda i: (0, 0), pipeline_mode=pl.Buffered(1))
    return pl.pallas_call(
        _route_kernel,
        grid=(n // tm,),
        in_specs=[
            row(d), row(d),
            const((d, d)), const((1, d)),
            const((d, 2 * LANES)), const((ROUTER_ROWS, 1)),
        ],
        out_specs=[
            row(d),
            pl.BlockSpec((SUBLANES, tm), lambda i: (0, i)),
            row(LANES),
            pl.BlockSpec((SUBLANES, tm), lambda i: (0, i)),
            pl.BlockSpec((N_EXPERTS, LANES), lambda i: (0, 0)),
        ],
        out_shape=[
            jax.ShapeDtypeStruct((n, d), F32),
            jax.ShapeDtypeStruct((SUBLANES, n), jnp.int32),
            jax.ShapeDtypeStruct((n, LANES), F32),
            jax.ShapeDtypeStruct((SUBLANES, n), jnp.int32),
            jax.ShapeDtypeStruct((N_EXPERTS, LANES), jnp.int32),
        ],
        scratch_shapes=[pltpu.VMEM((N_EXPERTS, LANES), F32)],
        compiler_params=_params("arbitrary"),
        name="route",
    )(merged, x2, w_out.astype(BF16), ffn_norm_w.reshape(1, -1), wr_hi_lo, rbc)


def _scatter_kernel(pad_start_ref, pad_len_ref, n_active_ref, pos_ref, hn_ref, xs_ref, zero_ref,
                    sem, zsem, *, tme, n_tiles):
    tm = hn_ref.shape[0]
    zrows = zero_ref.shape[0]

    @pl.when(pl.program_id(0) == 0)
    def _():
        zero_ref[...] = jnp.zeros(zero_ref.shape, zero_ref.dtype)

        def fills(act):
            def per_expert(e, carry):
                off = pad_start_ref[e]
                left = pad_len_ref[e]
                head = left & (SUBLANES - 1)
                for r in range(SUBLANES - 1):
                    @pl.when(r < head)
                    def _(r=r):
                        act(pltpu.make_async_copy(zero_ref.at[pl.ds(0, 1), :],
                                                  xs_ref.at[pl.ds(off + r, 1), :], zsem))

                off = off + head
                bit = zrows
                while bit >= SUBLANES:
                    take = left & bit

                    @pl.when(take != 0)
                    def _(off=off, bit=bit):
                        act(pltpu.make_async_copy(
                            zero_ref.at[pl.ds(0, bit), :],
                            xs_ref.at[pl.ds(pl.multiple_of(off, SUBLANES), bit), :], zsem))

                    off = off + take
                    bit //= 2
                return carry

            def per_tile(t, carry):
                @pl.when(t >= n_active_ref[0])
                def _():
                    for part in range(tme // zrows):
                        row0 = pl.multiple_of(t * tme + part * zrows, zrows)
                        act(pltpu.make_async_copy(zero_ref, xs_ref.at[pl.ds(row0, zrows), :], zsem))

                return carry

            lax.fori_loop(0, N_EXPERTS, per_expert, 0)
            lax.fori_loop(0, n_tiles, per_tile, 0)

        fills(lambda cp: cp.start())
        fills(lambda cp: cp.wait())

    unroll = math.gcd(tm, ROW_DMA_UNROLL)

    def start(tb, carry):
        for u in range(unroll):
            t = tb * unroll + u
            for k in range(EXPERT_TOP_K):
                dst = pos_ref[0, 0, k * tm + t]
                pltpu.make_async_copy(hn_ref.at[pl.ds(t, 1), :], xs_ref.at[pl.ds(dst, 1), :],
                                      sem).start(priority=(u + k) % 2)
        return carry

    lax.fori_loop(0, tm // unroll, start, 0)
    for k in range(EXPERT_TOP_K):
        pltpu.make_async_copy(hn_ref, xs_ref.at[pl.ds(0, tm), :], sem).wait()


def _scatter(hn, pos_tiles, pad_start, pad_len, n_active, *, tm, tme, n_tiles):
    n, d = hn.shape
    assert tme % 2 == 0 and (tme // 2) & (tme // 2 - 1) == 0
    grid_spec = pltpu.PrefetchScalarGridSpec(
        num_scalar_prefetch=3,
        grid=(n // tm,),
        in_specs=[
            pl.BlockSpec((1, 1, EXPERT_TOP_K * tm), lambda i, *_: (i, 0, 0),
                         memory_space=pltpu.SMEM),
            pl.BlockSpec((tm, d), lambda i, *_: (i, 0)),
        ],
        out_specs=pl.BlockSpec(memory_space=pl.ANY),
        scratch_shapes=[pltpu.VMEM((tme // 2, d), hn.dtype), pltpu.SemaphoreType.DMA(()),
                        pltpu.SemaphoreType.DMA(())],
    )
    return pl.pallas_call(
        functools.partial(_scatter_kernel, tme=tme, n_tiles=n_tiles),
        grid_spec=grid_spec,
        out_shape=jax.ShapeDtypeStruct((n_tiles * tme, d), hn.dtype),
        compiler_params=_params("arbitrary"),
        name="scatter",
    )(pad_start, pad_len, n_active, pos_tiles, hn)


W_CHUNK_ROWS = 256
W_RING = 8


def _experts_kernel(te_ref, slot_ref, nxt_ref, lo_ref, hi_ref, na_ref, xs_ref, fnw_ref, wg_hbm,
                    wu_hbm, wd_hbm, ys_ref, wres_ref, stage_ref, sem, *, n_gu, n_r, n_h):
    i = pl.program_id(0)
    n_chunks = 2 * n_gu + n_h * n_r
    ring, cr, f = stage_ref.shape
    d = n_gu * cr

    def stage_copy(src, c):
        return pltpu.make_async_copy(src, stage_ref.at[c % ring], sem.at[c % ring])

    def start_chunk(e, c):
        @pl.when(c < n_gu)
        def _():
            stage_copy(wg_hbm.at[e, pl.ds(pl.multiple_of(c * cr, cr), cr), :], c).start()

        @pl.when((c >= n_gu) & (c < 2 * n_gu))
        def _():
            stage_copy(wu_hbm.at[e, pl.ds(pl.multiple_of((c - n_gu) * cr, cr), cr), :], c).start()

        @pl.when(c >= 2 * n_gu)
        def _():
            q = c - 2 * n_gu
            h = q // n_r
            r = q - h * n_r
            stage_copy(wd_hbm.at[e, pl.ds(pl.multiple_of(r * cr, cr), cr),
                                 pl.ds(pl.multiple_of(h * f, f), f)], c).start()

    def start_first(e):
        def body(c, carry):
            start_chunk(e, c)
            return carry

        lax.fori_loop(0, ring, body, 0)

    def convert(e, lo, hi, dst_slot):
        def body(c, carry):
            stage_copy(wg_hbm.at[0, pl.ds(0, cr), :], c).wait()
            wres_ref[dst_slot, pl.ds(pl.multiple_of(c * cr, cr), cr), :] = (
                stage_ref[c % ring].astype(BF16))

            @pl.when(c + ring < n_chunks)
            def _():
                start_chunk(e, c + ring)

            return carry

        lax.fori_loop(lo, hi, body, 0)

    @pl.when(i == 0)
    def _():
        start_first(te_ref[0])
        convert(te_ref[0], 0, n_chunks, slot_ref[0])

    @pl.when(i < na_ref[0])
    def _():
        slot = slot_ref[i]
        nxt = nxt_ref[i]

        @pl.when((nxt >= 0) & (lo_ref[i] == 0))
        def _():
            start_first(nxt)

        xr = xs_ref[...]
        ms = jnp.mean(xr * xr, axis=-1, keepdims=True)
        xb = (xr * lax.rsqrt(ms + NORM_EPS) * fnw_ref[...]).astype(BF16)
        hg = jnp.dot(xb, wres_ref[slot, 0:d, :], preferred_element_type=F32)
        hu = jnp.dot(xb, wres_ref[slot, d:2 * d, :], preferred_element_type=F32)
        hmid = (_silu(hg) * hu).astype(BF16)
        for h in range(n_h):
            r0 = 2 * d + h * n_r * cr
            ys_ref[:, h * f:(h + 1) * f] = jnp.dot(hmid, wres_ref[slot, r0:r0 + n_r * cr, :],
                                                    preferred_element_type=F32)

        @pl.when(nxt >= 0)
        def _():
            convert(nxt, lo_ref[i], hi_ref[i], 1 - slot)

    @pl.when(i >= na_ref[0])
    def _():
        ys_ref[...] = jnp.zeros(ys_ref.shape, F32)


def _experts(xs, norm_w, tile_expert, n_active, cnt, starts, padded, w_g, w_u, w_d, *, tm):
    p, d = xs.shape
    ne, _, f = w_g.shape
    n_tiles = p // tm
    cr = W_CHUNK_ROWS
    assert d % cr == 0 and f % cr == 0 and d % f == 0
    n_gu, n_r, n_h = d // cr, f // cr, d // f
    n_chunks = 2 * n_gu + n_h * n_r

    ids = jnp.arange(ne, dtype=jnp.int32)
    present = cnt > 0
    later = jnp.where(present[None, :] & (ids[None, :] > ids[:, None]), ids[None, :], ne)
    nxt_e = jnp.min(later, axis=1)
    nxt_e = jnp.where(nxt_e >= ne, -1, nxt_e)
    run_e = jnp.cumsum(present.astype(jnp.int32)) - 1
    onehot = (tile_expert[:, None] == ids[None, :]).astype(jnp.int32)
    pick = lambda v: jnp.sum(onehot * v[None, :].astype(jnp.int32), axis=1)
    tiles = jnp.arange(n_tiles, dtype=jnp.int32)
    active = tiles < n_active[0]
    j = tiles - pick(starts) // tm
    k = jnp.maximum(pick(padded) // tm, 1)
    nxt_t = jnp.where(active, pick(nxt_e), -1).astype(jnp.int32)
    lo_t = jnp.where(active, (n_chunks * j) // k, 0).astype(jnp.int32)
    hi_t = jnp.where(active, (n_chunks * (j + 1)) // k, 0).astype(jnp.int32)
    slot_t = (pick(run_e) % 2).astype(jnp.int32)

    def rows(i, te, sl, nx, lo, hi, na):
        return (jnp.minimum(i, na[0] - 1), 0)

    grid_spec = pltpu.PrefetchScalarGridSpec(
        num_scalar_prefetch=6,
        grid=(n_tiles,),
        in_specs=[
            pl.BlockSpec((tm, d), rows),
            pl.BlockSpec((1, d), lambda i, *_: (0, 0)),
            pl.BlockSpec(memory_space=pl.ANY),
            pl.BlockSpec(memory_space=pl.ANY),
            pl.BlockSpec(memory_space=pl.ANY),
        ],
        out_specs=pl.BlockSpec((tm, d), lambda i, *_: (i, 0)),
        scratch_shapes=[
            pltpu.VMEM((2, n_chunks * cr, f), BF16),
            pltpu.VMEM((W_RING, cr, f), F32),
            pltpu.SemaphoreType.DMA((W_RING,)),
        ],
    )
    return pl.pallas_call(
        functools.partial(_experts_kernel, n_gu=n_gu, n_r=n_r, n_h=n_h),
        grid_spec=grid_spec,
        out_shape=jax.ShapeDtypeStruct((p, d), F32),
        compiler_params=_params("arbitrary"),
        name="experts",
    )(tile_expert, slot_t, nxt_t, lo_t, hi_t, n_active, xs, norm_w.reshape(1, d), w_g, w_u, w_d)


def _combine_kernel(pos_ref, pos_next_ref, ys_ref, x1_ref, gcol_ref, nw_ref, out_ref, buf_ref, sem,
                    *, final):
    i = pl.program_id(0)
    n_steps = pl.num_programs(0)
    tm = x1_ref.shape[0]

    unroll = math.gcd(tm, ROW_DMA_UNROLL)

    def issue(p_ref, slot):
        def start(tb, carry):
            for u in range(unroll):
                t = tb * unroll + u
                for k in range(EXPERT_TOP_K):
                    src = p_ref[0, 0, k * tm + t]
                    pltpu.make_async_copy(ys_ref.at[pl.ds(src, 1), :],
                                          buf_ref.at[slot, k, pl.ds(t, 1), :],
                                          sem.at[slot]).start(priority=(u + k) % 2)
            return carry

        lax.fori_loop(0, tm // unroll, start, 0)

    slot = i % 2

    @pl.when(i == 0)
    def _():
        issue(pos_ref, 0)

    @pl.when(i + 1 < n_steps)
    def _():
        issue(pos_next_ref, 1 - slot)

    for k in range(EXPERT_TOP_K):
        pltpu.make_async_copy(ys_ref.at[pl.ds(0, tm), :], buf_ref.at[slot, k], sem.at[slot]).wait()
    g = gcol_ref[...]
    xo = x1_ref[...] + g[:, 0:1] * buf_ref[slot, 0] + g[:, 1:2] * buf_ref[slot, 1]
    if final:
        ms = jnp.mean(xo * xo, axis=-1, keepdims=True)
        xo = xo * lax.rsqrt(ms + NORM_EPS) * nw_ref[...]
    out_ref[...] = xo


def _combine(ys, pos_tiles, x1, gcol, norm_w, *, tm, final):
    n, d = x1.shape
    last = n // tm - 1
    return pl.pallas_call(
        functools.partial(_combine_kernel, final=final),
        grid=(n // tm,),
        in_specs=[
            pl.BlockSpec((1, 1, EXPERT_TOP_K * tm), lambda i: (i, 0, 0), memory_space=pltpu.SMEM),
            pl.BlockSpec((1, 1, EXPERT_TOP_K * tm), lambda i: (jnp.minimum(i + 1, last), 0, 0),
                         memory_space=pltpu.SMEM),
            pl.BlockSpec(memory_space=pl.ANY),
            pl.BlockSpec((tm, d), lambda i: (i, 0)),
            pl.BlockSpec((tm, LANES), lambda i: (i, 0)),
            pl.BlockSpec((1, d), lambda i: (0, 0)),
        ],
        out_specs=pl.BlockSpec((tm, d), lambda i: (i, 0)),
        out_shape=jax.ShapeDtypeStruct((n, d), F32),
        scratch_shapes=[pltpu.VMEM((2, EXPERT_TOP_K, tm, d), F32), pltpu.SemaphoreType.DMA((2,))],
        compiler_params=_params("arbitrary"),
        name="combine",
    )(pos_tiles, pos_tiles, ys, x1, gcol, norm_w.reshape(1, d))


def _tiles(n):
    return dict(proj_tm=math.gcd(n, 1024), mix_tm=math.gcd(n, 512), moe_tm=256,
                route_tm=math.gcd(n, 512))


def _layer(x2, batch, seq, layer, attn_norm_w, w_in_stack, b_gate, conv_w, conv_b, dt_bias, a_log,
           d_skip, ssd_norm_w, w_ssd_out, w_attn_out, w_out, ffn_norm_w, w_gr, b_gr, w_er, b_er,
           w_g, w_u, w_d):
    n, d = x2.shape
    n_heads = dt_bias.shape[0]
    d_inner = ssd_norm_w.shape[0]
    conv_dim = conv_w.shape[1]
    aw_total = ATTN_HEADS_PER_GROUP * len(DILATION_PATTERNS) * ATTN_HEAD_DIM
    gw = ATTN_HEADS_PER_GROUP * ATTN_HEAD_DIM
    tiles = _tiles(n)

    c_z, c_xbc, c_dt = d_inner, d_inner + conv_dim, d_inner + conv_dim + n_heads
    c_gate = c_dt + QKV_PARTS * aw_total
    segments = [(c_gate, 2 * d), (c_z, conv_dim), (0, d_inner)]
    segments += [(c_dt + p * aw_total + gi * gw, gw)
                 for gi in range(len(DILATION_PATTERNS)) for p in range(QKV_PARTS)]
    assert all(width % gw == 0 for _, width in segments)
    starts = tuple(start + b * gw for start, width in segments for b in range(width // gw))
    w_main, w_dt = _regroup_weight(w_in_stack, layer, starts, gw, c_xbc, n_heads)
    off_gate, off_xbc, off_z = 0, 2 * d, 2 * d + conv_dim
    plain_cols = off_z + d_inner
    assert off_xbc % conv_dim == 0 and off_z % d_inner == 0 and plain_cols % gw == 0

    proj, qkv0, qkv1, qkv2, dt, dtt = _in_proj(x2, attn_norm_w, w_main, w_dt, tm=tiles["proj_tm"],
                                               tn=gw, tp=4 * gw, plain_cols=plain_cols, nh=n_heads)

    yn = _ssd(proj, dt, dtt, conv_w, conv_b, dt_bias, a_log, d_skip, ssd_norm_w, batch=batch,
              seq=seq, d_inner=d_inner, xbc_block=off_xbc // conv_dim, z_block=off_z // d_inner)

    outs, lses = [], []
    for gi, qkv in enumerate((qkv0, qkv1, qkv2)):
        o_g, lse_g = _attn_group(qkv, gi, batch=batch, seq=seq)
        outs.append(o_g)
        lses.append(lse_g)

    merged = _merge(yn, outs, lses, proj, w_ssd_out, w_attn_out, b_gate, tm=tiles["mix_tm"],
                    gate_block=off_gate // (2 * d))
    x1, eid, gcol, rank, counts = _route(merged, x2, w_out, ffn_norm_w, (w_gr, w_er),
                                         (b_gr, b_er), tm=tiles["mix_tm"])

    tme = tiles["moe_tm"]
    cnt = counts[:, 0]
    padded = ((cnt + tme - 1) // tme) * tme
    ends = jnp.cumsum(padded)
    starts = ends - padded
    experts = jnp.arange(N_EXPERTS, dtype=jnp.int32)[:, None, None]
    pos = rank[:EXPERT_TOP_K] + jnp.sum(
        jnp.where(eid[None, :EXPERT_TOP_K] == experts, starts[:, None, None], 0), axis=0)
    n_tiles = EXPERT_TOP_K * n // tme + N_EXPERTS
    tile_start = jnp.arange(n_tiles, dtype=jnp.int32) * tme
    tile_expert = jnp.minimum(jnp.sum(ends[None, :] <= tile_start[:, None], axis=1),
                              N_EXPERTS - 1).astype(jnp.int32)
    n_active = (ends[-1:] // tme).astype(jnp.int32)

    rtm = tiles["route_tm"]
    pos_tiles = pos.reshape(EXPERT_TOP_K, n // rtm, rtm).transpose(1, 0, 2).reshape(
        n // rtm, 1, EXPERT_TOP_K * rtm)
    xs = _scatter(x1, pos_tiles, (starts + cnt).astype(jnp.int32), (padded - cnt).astype(jnp.int32),
                  n_active, tm=rtm, tme=tme, n_tiles=n_tiles)
    ys = _experts(xs, ffn_norm_w, tile_expert, n_active, cnt, starts, padded, w_g, w_u, w_d, tm=tme)
    return ys, pos_tiles, x1, gcol, rtm


def kernel(x, attn_norm_w, w_in, b_gate, conv_w, conv_b, dt_bias, a_log, d_skip, ssd_norm_w,
           w_ssd_out, w_attn_out, w_out, ffn_norm_w, w_group_router, b_group_router,
           w_expert_router, b_expert_router, w_exp_gate, w_exp_up, w_exp_down, final_norm_w):
    batch, seq, d = x.shape
    depth = w_in.shape[0]
    x2 = x.reshape(batch * seq, d)
    for layer in range(depth):
        ys, pos_tiles, x1, gcol, rtm = _layer(
            x2, batch, seq, layer, attn_norm_w[layer], w_in, b_gate[layer], conv_w[layer],
            conv_b[layer], dt_bias[layer], a_log[layer], d_skip[layer], ssd_norm_w[layer],
            w_ssd_out[layer], w_attn_out[layer], w_out[layer], ffn_norm_w[layer],
            w_group_router[layer], b_group_router[layer], w_expert_router[layer],
            b_expert_router[layer], w_exp_gate[layer], w_exp_up[layer], w_exp_down[layer])
        x2 = _combine(ys, pos_tiles, x1, gcol, final_norm_w, tm=rtm, final=layer == depth - 1)
    return x2.reshape(batch, seq, d)
```

```python
import functools
import math

import jax
import jax.numpy as jnp
import numpy as np
from jax import lax
from jax.experimental import pallas as pl
from jax.experimental.pallas import tpu as pltpu

F32 = jnp.float32
BF16 = jnp.bfloat16

NORM_EPS = 1e-6
SSD_HEAD_DIM = 64
SSD_N_GROUPS = 8
SSD_D_STATE = 128
SSD_CONV_WIDTH = 4
SSD_CHUNK = 128
SSD_CHUNKS_PER_STEP = 4
ATTN_HEAD_DIM = 128
DILATION_PATTERNS = ((128, 1), (512, 4), (2048, 16))
ATTN_HEADS_PER_GROUP = 4
ATTN_BLOCK = 128
ATTN_BLOCKS_PER_STEP = 8
N_EXPERT_GROUPS = 4
EXPERTS_PER_GROUP = 8
N_EXPERTS = N_EXPERT_GROUPS * EXPERTS_PER_GROUP
EXPERT_TOP_K = 2

LANES = 128
SUBLANES = 8
VMEM_LIMIT_BYTES = 56 * 1024 * 1024
ROW_DMA_UNROLL = 512

ROUTER_EXPERT_ROW0 = SUBLANES
ROUTER_ROWS = ROUTER_EXPERT_ROW0 + N_EXPERTS


def _params(*semantics):
    return pltpu.CompilerParams(dimension_semantics=semantics, vmem_limit_bytes=VMEM_LIMIT_BYTES)


def _split_bf16(v):
    hi = v.astype(BF16)
    lo = (v - hi.astype(F32)).astype(BF16)
    return hi, lo


def _sigmoid(v):
    return 0.5 + 0.5 * jnp.tanh(0.5 * v)


def _silu(v):
    h = 0.5 * v
    return h + h * jnp.tanh(h)


def _regroup_weight_kernel(starts_ref, wt_hbm, out_ref, narrow_ref, buf_ref, nbuf_ref, sem, nsem,
                           *, layer, narrow_start, nh):
    i = pl.program_id(0)
    n_steps = pl.num_programs(0)
    tn = buf_ref.shape[1]

    def fetch(step, slot):
        rows = pl.ds(pl.multiple_of(starts_ref[step], SUBLANES), tn)
        return pltpu.make_async_copy(wt_hbm.at[layer, rows, :], buf_ref.at[slot], sem.at[slot])

    narrow = pltpu.make_async_copy(wt_hbm.at[layer, pl.ds(narrow_start, LANES), :], nbuf_ref, nsem)

    @pl.when(i == 0)
    def _():
        fetch(0, 0).start()
        narrow.start()

    @pl.when(i + 1 < n_steps)
    def _():
        fetch(i + 1, (i + 1) % 2).start()

    fetch(i, i % 2).wait()
    out_ref[...] = buf_ref[i % 2].T.astype(out_ref.dtype)

    @pl.when(i == n_steps - 1)
    def _():
        narrow.wait()
        lane = lax.broadcasted_iota(jnp.int32, narrow_ref.shape, 1)
        narrow_ref[...] = jnp.where(lane < nh, nbuf_ref[...].T, 0.0).astype(narrow_ref.dtype)


def _regroup_weight(w_stack, layer, starts, tn, narrow_start, nh):
    _, k, cols = w_stack.shape
    assert all(s % SUBLANES == 0 and s + tn <= cols for s in starts)
    assert narrow_start % SUBLANES == 0 and narrow_start + LANES <= cols and nh <= LANES
    wt = jnp.swapaxes(w_stack, 1, 2)
    grid_spec = pltpu.PrefetchScalarGridSpec(
        num_scalar_prefetch=1,
        grid=(len(starts),),
        in_specs=[pl.BlockSpec(memory_space=pl.ANY)],
        out_specs=[pl.BlockSpec((k, tn), lambda i, *_: (0, i)),
                   pl.BlockSpec((k, LANES), lambda i, *_: (0, 0))],
        scratch_shapes=[pltpu.VMEM((2, tn, k), F32), pltpu.VMEM((LANES, k), F32),
                        pltpu.SemaphoreType.DMA((2,)), pltpu.SemaphoreType.DMA(())],
    )
    return pl.pallas_call(
        functools.partial(_regroup_weight_kernel, layer=layer, narrow_start=narrow_start, nh=nh),
        grid_spec=grid_spec,
        out_shape=[jax.ShapeDtypeStruct((k, len(starts) * tn), BF16),
                   jax.ShapeDtypeStruct((k, LANES), BF16)],
        compiler_params=_params("arbitrary"),
        name="regroup_weight",
    )(jnp.asarray(starts, jnp.int32), wt)


QKV_PARTS = 3
RELAYOUT_STRIDE = 4


def _in_proj_kernel(x_hbm, nw_ref, wp_ref, w_ref, wdt_ref, proj_ref, a0_ref, a1_ref, a2_ref,
                    dt_ref, dtt_ref, h_ref, stage_ref, x_ref, xsem, *, n_plain):
    i = pl.program_id(0)
    j = pl.program_id(1)
    tm = proj_ref.shape[0]
    tn = w_ref.shape[1]

    def fetch_x(tile):
        return pltpu.make_async_copy(x_hbm.at[pl.ds(pl.multiple_of(tile * tm, tm), tm), :], x_ref, xsem)

    @pl.when((i == 0) & (j == 0))
    def _():
        fetch_x(0).start()

    @pl.when(j == 0)
    def _():
        fetch_x(i).wait()
        xf = x_ref[...]
        ms = jnp.mean(xf * xf, axis=-1, keepdims=True)
        h = (xf * lax.rsqrt(ms + NORM_EPS) * nw_ref[...]).astype(BF16)
        h_ref[...] = h
        nh = dt_ref.shape[1]
        dt_wide = jnp.dot(h, wdt_ref[...], preferred_element_type=F32)
        dt_ref[...] = dt_wide[:, :nh]
        dtt_ref[...] = dt_wide.T[:nh, :]

        @pl.when(i + 1 < pl.num_programs(0))
        def _():
            fetch_x(i + 1).start()

    @pl.when(j < n_plain)
    def _():
        proj_ref[...] = jnp.dot(h_ref[...], wp_ref[...], preferred_element_type=F32).astype(BF16)

    for gi, a_ref in enumerate((a0_ref, a1_ref, a2_ref)):
        dil = DILATION_PATTERNS[gi][1]
        j0 = n_plain + QKV_PARTS * gi

        @pl.when((j >= j0) & (j < j0 + QKV_PARTS))
        def _(a_ref=a_ref, dil=dil):
            res = jnp.dot(h_ref[...], w_ref[...], preferred_element_type=F32)
            if dil == 1:
                a_ref[...] = res.astype(BF16)
            else:
                two_pass = dil > RELAYOUT_STRIDE and dil % RELAYOUT_STRIDE == 0
                quarter = tm // RELAYOUT_STRIDE
                for s in range(tn // LANES):
                    stage_ref[0] = res[:, s * LANES:(s + 1) * LANES]
                    if two_pass:
                        for r1 in range(RELAYOUT_STRIDE):
                            stage_ref[1, r1 * quarter:(r1 + 1) * quarter, :] = stage_ref[
                                0, pl.ds(r1, quarter, stride=RELAYOUT_STRIDE), :]
                    for r in range(dil):
                        if two_pass:
                            rows = pl.ds((r % RELAYOUT_STRIDE) * quarter + r // RELAYOUT_STRIDE,
                                         tm // dil, stride=dil // RELAYOUT_STRIDE)
                        else:
                            rows = pl.ds(r, tm // dil, stride=dil)
                        c0 = r * tn + s * LANES
                        a_ref[:, c0:c0 + LANES] = stage_ref[int(two_pass), rows, :].astype(BF16)


def _in_proj(x2, norm_w, w_main, w_dt, *, tm, tn, tp, plain_cols, nh):
    n, d = x2.shape
    assert plain_cols % tp == 0 and plain_cols % tn == 0
    n_plain = plain_cols // tp
    qkv_block0 = plain_cols // tn
    n_blocks = n_plain + QKV_PARTS * len(DILATION_PATTERNS)
    assert w_main.shape[1] == plain_cols + QKV_PARTS * len(DILATION_PATTERNS) * tn

    def a_spec(gi):
        dil = DILATION_PATTERNS[gi][1]
        j0 = n_plain + QKV_PARTS * gi
        return pl.BlockSpec((tm // dil, dil * tn),
                            lambda i, j: (i, jnp.clip(j - j0, 0, QKV_PARTS - 1)))

    def a_shape(gi):
        dil = DILATION_PATTERNS[gi][1]
        return jax.ShapeDtypeStruct((n // dil, dil * QKV_PARTS * tn), BF16)

    return pl.pallas_call(
        functools.partial(_in_proj_kernel, n_plain=n_plain),
        grid=(n // tm, n_blocks),
        in_specs=[
            pl.BlockSpec(memory_space=pl.ANY),
            pl.BlockSpec((1, d), lambda i, j: (0, 0)),
            pl.BlockSpec((d, tp), lambda i, j: (0, jnp.minimum(j, n_plain - 1))),
            pl.BlockSpec((d, tn), lambda i, j: (0, qkv_block0 + jnp.maximum(j - n_plain, 0))),
            pl.BlockSpec((d, LANES), lambda i, j: (0, 0)),
        ],
        out_specs=[
            pl.BlockSpec((tm, tp), lambda i, j: (i, jnp.minimum(j, n_plain - 1))),
            a_spec(0), a_spec(1), a_spec(2),
            pl.BlockSpec((tm, nh), lambda i, j: (i, 0)),
            pl.BlockSpec((nh, tm), lambda i, j: (0, i)),
        ],
        out_shape=[
            jax.ShapeDtypeStruct((n, plain_cols), BF16),
            a_shape(0), a_shape(1), a_shape(2),
            jax.ShapeDtypeStruct((n, nh), F32),
            jax.ShapeDtypeStruct((nh, n), F32),
        ],
        scratch_shapes=[pltpu.VMEM((tm, d), BF16), pltpu.VMEM((2, tm, LANES), F32),
                        pltpu.VMEM((tm, d), x2.dtype), pltpu.SemaphoreType.DMA(())],
        compiler_params=_params("arbitrary", "arbitrary"),
        name="in_proj",
    )(x2, norm_w.reshape(1, d), w_main, w_main, w_dt)


LOG2E = math.log2(math.e)
CONV_ROW_PITCH = 2


def _ssd_kernel(xbc_ref, z_ref, dt_ref, dtt_ref, cw_ref, cb_ref, dtb_ref, dtbt_ref, alog_ref,
                alogt_ref, dskip_ref, nw_ref, expand_ref, out_ref,
                xbuf_ref, state_ref, y_ref, *, n_heads, d_inner):
    L = SSD_CHUNK
    P = SSD_HEAD_DIM
    NS = SSD_D_STATE
    G = SSD_N_GROUPS
    R = n_heads // G
    GW = R * P
    W = SSD_CONV_WIDTH
    RP = CONV_ROW_PITCH
    n_slabs = xbuf_ref.shape[0]
    c = pl.program_id(1)

    def rows(first, count):
        return pl.ds(first * RP, count, stride=RP)

    @pl.when(c == 0)
    def _():
        state_ref[...] = jnp.zeros(state_ref.shape, F32)
        for s in range(n_slabs):
            xbuf_ref[s, rows(0, SUBLANES), :] = jnp.zeros((SUBLANES, LANES), F32)

    n_rows = xbc_ref.shape[0]

    @pl.when(c > 0)
    def _():
        for s in range(n_slabs):
            xbuf_ref[s, rows(0, SUBLANES), :] = xbuf_ref[s, rows(n_rows, SUBLANES), :]

    for s in range(n_slabs):
        xbuf_ref[s, rows(SUBLANES, n_rows), :] = xbc_ref[:, s * LANES:(s + 1) * LANES].astype(F32)

    for sub in range(n_rows // L):
        _ssd_chunk(sub * L, xbuf_ref, z_ref, dt_ref, dtt_ref, cw_ref, cb_ref, dtb_ref, dtbt_ref,
                   alog_ref, alogt_ref, dskip_ref, nw_ref, expand_ref, out_ref, state_ref, y_ref,
                   n_heads=n_heads, d_inner=d_inner)


def _ssd_chunk(r0, xbuf_ref, z_ref, dt_ref, dtt_ref, cw_ref, cb_ref, dtb_ref, dtbt_ref, alog_ref,
               alogt_ref, dskip_ref, nw_ref, expand_ref, out_ref, state_ref, y_ref, *, n_heads,
               d_inner):
    L = SSD_CHUNK
    P = SSD_HEAD_DIM
    NS = SSD_D_STATE
    G = SSD_N_GROUPS
    R = n_heads // G
    GW = R * P
    W = SSD_CONV_WIDTH
    RP = CONV_ROW_PITCH
    tok = slice(r0, r0 + L)

    def rows(first, count):
        return pl.ds(first * RP, count, stride=RP)

    def conv(col0, width):
        slabs = []
        for s in range(col0 // LANES, (col0 + width) // LANES):
            acc = cb_ref[:, s * LANES:(s + 1) * LANES]
            for w in range(W):
                acc = acc + (cw_ref[w:w + 1, s * LANES:(s + 1) * LANES]
                             * xbuf_ref[s, rows(r0 + SUBLANES - (W - 1) + w, L), :])
            slabs.append(acc)
        return _silu(jnp.concatenate(slabs, axis=1))

    def softplus(v):
        return jnp.maximum(v, 0.0) + jnp.log1p(jnp.exp(-jnp.abs(v)))

    dt = softplus(dt_ref[tok, :] + dtb_ref[...])
    dtt = softplus(dtt_ref[:, tok] + dtbt_ref[...])
    da = dt * (-LOG2E * jnp.exp(alog_ref[...]))
    dat = dtt * (-LOG2E * jnp.exp(alogt_ref[...]))
    row = lax.broadcasted_iota(jnp.int32, (L, L), 0)
    col = lax.broadcasted_iota(jnp.int32, (L, L), 1)
    causal = row >= col
    tri = jnp.where(causal, 1.0, 0.0).astype(BF16)
    trit = jnp.where(row <= col, 1.0, 0.0).astype(BF16)

    def split3(v):
        a = v.astype(BF16)
        r1 = v - a.astype(F32)
        b = r1.astype(BF16)
        cc = (r1 - b.astype(F32)).astype(BF16)
        return a, b, cc

    a2 = sum(jnp.dot(tri, p, preferred_element_type=F32) for p in split3(da))
    a2t = sum(jnp.dot(p, trit, preferred_element_type=F32) for p in split3(dat))
    a2_last = a2[L - 1:L, :]

    expand = expand_ref[...]

    def expand_heads(v):
        hi, lo = _split_bf16(v)
        return jnp.dot(jnp.concatenate([hi, lo], axis=1), expand, preferred_element_type=F32)

    in_scale_e = expand_heads(dt * jnp.exp2(a2_last - a2))
    tail8 = jnp.concatenate([jnp.exp2(a2_last), dskip_ref[...],
                             jnp.zeros((SUBLANES - 2, n_heads), F32)], axis=0)
    out_e = expand_heads(jnp.concatenate([jnp.exp2(a2), tail8], axis=0))
    out_scale_e = out_e[0:L, :]
    chunk_decay_e = out_e[L:L + 1, :]
    dskip_e = out_e[L + 1:L + 2, :]

    first_head = lax.broadcasted_iota(jnp.int32, (L, LANES), 1) < P

    for g in range(G):
        x0 = g * GW
        xs = conv(x0, GW)
        bm = conv(d_inner + g * NS, NS).astype(BF16)
        cm = conv(d_inner + G * NS + g * NS, NS).astype(BF16)
        cb = lax.dot_general(cm, bm, (((1,), (1,)), ((), ())), preferred_element_type=F32)
        cb = jnp.where(causal, cb, 0.0)
        y_parts = []
        for pr in range(GW // LANES):
            wgts = []
            for q in range(LANES // P):
                h = g * R + pr * (LANES // P) + q
                seg = a2[:, h:h + 1] - a2t[h:h + 1, :]
                decay = jnp.exp2(jnp.minimum(seg, 0.0))
                wgts.append((cb * decay * dtt[h:h + 1, :]).astype(BF16))
            slab = xs[:, pr * LANES:(pr + 1) * LANES]
            rhs = jnp.concatenate([jnp.where(first_head, slab, 0.0).astype(BF16),
                                   jnp.where(first_head, 0.0, slab).astype(BF16)], axis=0)
            y_parts.append(jnp.dot(jnp.concatenate(wgts, axis=1), rhs, preferred_element_type=F32))
        y = jnp.concatenate(y_parts, axis=1) + xs * dskip_e[:, x0:x0 + GW]
        st = state_ref[:, x0:x0 + GW]
        y = y + (jnp.dot(cm, st.astype(BF16), preferred_element_type=F32)
                 * out_scale_e[:, x0:x0 + GW])
        y_ref[tok, x0:x0 + GW] = y
        xin = (xs * in_scale_e[:, x0:x0 + GW]).astype(BF16)
        st_new = lax.dot_general(bm, xin, (((0,), (0,)), ((), ())), preferred_element_type=F32)
        state_ref[:, x0:x0 + GW] = st * chunk_decay_e[:, x0:x0 + GW] + st_new

    yz = y_ref[tok, :] * _silu(z_ref[tok, :].astype(F32))
    ms = jnp.mean(yz * yz, axis=-1, keepdims=True)
    out_ref[tok, :] = (yz * lax.rsqrt(ms + NORM_EPS) * nw_ref[...]).astype(out_ref.dtype)


def _ssd(proj, dt, dtt, conv_w, conv_b, dt_bias, a_log, d_skip, norm_w, *, batch, seq, d_inner,
         xbc_block, z_block):
    n = batch * seq
    n_heads = dt.shape[1]
    L = SSD_CHUNK * math.gcd(seq // SSD_CHUNK, SSD_CHUNKS_PER_STEP)
    nc = seq // L
    conv_dim = conv_w.shape[1]
    expand = (np.arange(d_inner)[None, :] // SSD_HEAD_DIM == np.arange(n_heads)[:, None])
    expand = jnp.asarray(np.concatenate([expand, expand], axis=0), BF16)
    assert conv_dim % LANES == 0
    kern = functools.partial(_ssd_kernel, n_heads=n_heads, d_inner=d_inner)
    small = lambda shape: pl.BlockSpec(shape, lambda b, c: (0, 0))
    return pl.pallas_call(
        kern,
        grid=(batch, nc),
        in_specs=[
            pl.BlockSpec((L, conv_dim), lambda b, c: (b * nc + c, xbc_block)),
            pl.BlockSpec((L, d_inner), lambda b, c: (b * nc + c, z_block)),
            pl.BlockSpec((L, n_heads), lambda b, c: (b * nc + c, 0)),
            pl.BlockSpec((n_heads, L), lambda b, c: (0, b * nc + c)),
            small((SSD_CONV_WIDTH, conv_dim)),
            small((1, conv_dim)),
            small((1, n_heads)),
            small((n_heads, 1)),
            small((1, n_heads)),
            small((n_heads, 1)),
            small((1, n_heads)),
            small((1, d_inner)),
            small((2 * n_heads, d_inner)),
        ],
        out_specs=pl.BlockSpec((L, d_inner), lambda b, c: (b * nc + c, 0)),
        out_shape=jax.ShapeDtypeStruct((n, d_inner), BF16),
        scratch_shapes=[
            pltpu.VMEM((conv_dim // LANES, CONV_ROW_PITCH * (L + SUBLANES), LANES), F32),
            pltpu.VMEM((SSD_D_STATE, d_inner), F32),
            pltpu.VMEM((L, d_inner), F32),
        ],
        compiler_params=_params("arbitrary", "arbitrary"),
        name="ssd",
    )(proj, proj, dt, dtt, conv_w, conv_b.reshape(1, -1), dt_bias.reshape(1, -1),
      dt_bias.reshape(-1, 1), a_log.reshape(1, -1), a_log.reshape(-1, 1), d_skip.reshape(1, -1),
      norm_w.reshape(1, -1), expand)


def _attn_kernel(q_ref, k_ref, v_ref, o_ref, lse_ref, kp_ref, vp_ref, *, slopes, dilation, hops):
    nb = pl.program_id(2)
    BLK = ATTN_BLOCK
    E = ATTN_HEAD_DIM

    @pl.when(nb == 0)
    def _():
        kp_ref[...] = jnp.zeros(kp_ref.shape, kp_ref.dtype)
        vp_ref[...] = jnp.zeros(vp_ref.shape, vp_ref.dtype)

    scale = E ** -0.5
    nt = (((1,), (1,)), ((), ()))
    nh = len(slopes)
    nblk = q_ref.shape[0] // BLK
    nres = q_ref.shape[1] // (nh * E)
    q = q_ref[...]
    kk = jnp.concatenate([kp_ref[...], k_ref[...]], axis=0)
    vv = jnp.concatenate([vp_ref[...], v_ref[...]], axis=0)
    units = [(j, c) for c in range(nres * nh) for j in range(nblk)]
    cols = lambda c: slice(c * E, (c + 1) * E)
    rows = lambda j: slice(j * BLK, (j + 1) * BLK)
    qi = lax.broadcasted_iota(jnp.int32, (BLK, BLK), 0)
    ki = lax.broadcasted_iota(jnp.int32, (BLK, BLK), 1)
    rel_cur = qi - ki
    rel_prev = rel_cur + BLK
    first = rel_prev <= jnp.where(nb > 0, hops, -1)
    later = rel_prev <= hops
    ok_cur = jnp.concatenate([rel_cur >= 0 for _ in units], axis=0)
    ok_prev = jnp.concatenate([first if j == 0 else later for j, _ in units], axis=0)
    dist_cur = (rel_cur * dilation).astype(F32)
    dist_prev = (rel_prev * dilation).astype(F32)
    bias_cur = jnp.concatenate([slopes[c % nh] * dist_cur for _, c in units], axis=0)
    bias_prev = jnp.concatenate([slopes[c % nh] * dist_prev for _, c in units], axis=0)
    s_cur = jnp.concatenate(
        [lax.dot_general(q[rows(j), cols(h)], kk[rows(j + 1), cols(h)], nt,
                         preferred_element_type=F32) for j, h in units], axis=0)
    s_prev = jnp.concatenate(
        [lax.dot_general(q[rows(j), cols(h)], kk[rows(j), cols(h)], nt,
                         preferred_element_type=F32) for j, h in units], axis=0)
    l_cur = jnp.where(ok_cur, s_cur * scale - bias_cur, -jnp.inf)
    l_prev = jnp.where(ok_prev, s_prev * scale - bias_prev, -jnp.inf)
    m = jnp.max(jnp.maximum(l_cur, l_prev), axis=-1, keepdims=True)
    p_cur = jnp.exp(l_cur - m)
    p_prev = jnp.exp(l_prev - m)
    den = jnp.sum(p_cur + p_prev, axis=-1, keepdims=True)
    p_cur = p_cur.astype(BF16)
    p_prev = p_prev.astype(BF16)
    inv = 1.0 / den
    lse = m + jnp.log(den)
    lane = lax.broadcasted_iota(jnp.int32, (BLK, LANES), 1)
    for res in range(nres):
        for j in range(nblk):
            lse_tile = jnp.zeros((BLK, LANES), F32)
            for h in range(nh):
                c = res * nh + h
                u = rows(units.index((j, c)))
                acc = (jnp.dot(p_cur[u], vv[rows(j + 1), cols(c)], preferred_element_type=F32)
                       + jnp.dot(p_prev[u], vv[rows(j), cols(c)], preferred_element_type=F32))
                o_ref[rows(j), cols(c)] = (acc * inv[u]).astype(o_ref.dtype)
                lse_tile = jnp.where(lane == h, lse[u], lse_tile)
            lse_ref[rows(j), res * LANES:(res + 1) * LANES] = lse_tile
    kp_ref[...] = k_ref[rows(nblk - 1), :]
    vp_ref[...] = v_ref[rows(nblk - 1), :]


def _attn_group(qkv, gi, *, batch, seq):
    window, dilation = DILATION_PATTERNS[gi]
    hops = window // dilation
    n_heads_total = ATTN_HEADS_PER_GROUP * len(DILATION_PATTERNS)
    slopes = tuple(float(2.0 ** (-8.0 * (gi * ATTN_HEADS_PER_GROUP + h + 1) / n_heads_total))
                   for h in range(ATTN_HEADS_PER_GROUP))
    gw = ATTN_HEADS_PER_GROUP * ATTN_HEAD_DIM
    assert seq % (dilation * ATTN_BLOCK) == 0
    sub = seq // dilation
    blocks = math.gcd(sub // ATTN_BLOCK, ATTN_BLOCKS_PER_STEP)
    nres = math.gcd(dilation, ATTN_BLOCKS_PER_STEP // blocks)
    rows = ATTN_BLOCK * blocks
    nb = sub // rows
    nr = dilation // nres
    kern = functools.partial(_attn_kernel, slopes=slopes, dilation=dilation, hops=hops)

    def part(p):
        return pl.BlockSpec((rows, nres * gw), lambda b, r, n: (b * nb + n, p * nr + r))

    o, lse = pl.pallas_call(
        kern,
        grid=(batch, nr, nb),
        in_specs=[part(0), part(1), part(2)],
        out_specs=[
            pl.BlockSpec((rows, nres * gw), lambda b, r, n: (b * nb + n, r)),
            pl.BlockSpec((rows, nres * LANES), lambda b, r, n: (b * nb + n, r)),
        ],
        out_shape=[
            jax.ShapeDtypeStruct((batch * sub, dilation * gw), BF16),
            jax.ShapeDtypeStruct((batch * sub, dilation * LANES), F32),
        ],
        scratch_shapes=[pltpu.VMEM((ATTN_BLOCK, nres * gw), BF16),
                        pltpu.VMEM((ATTN_BLOCK, nres * gw), BF16)],
        compiler_params=_params("arbitrary", "arbitrary", "arbitrary"),
        name=f"attn_g{gi}",
    )(qkv, qkv, qkv)
    return o, lse


def _merge_kernel(yn_ref, o0_ref, o1_ref, o2_ref, l0_ref, l1_ref, l2_ref, gate_ref,
                  wssd_ref, wattn_ref, bg_ref, merged_ref, ostage_ref, lstage_ref, *, d_model):
    E = ATTN_HEAD_DIM
    tm = merged_ref.shape[0]
    n_slabs = o0_ref.shape[1] // LANES
    y_ssd = jnp.dot(yn_ref[...], wssd_ref[...], preferred_element_type=F32)

    def token_major(gi, o_ref, l_ref):
        dil = DILATION_PATTERNS[gi][1]
        if dil == 1:
            return o_ref[...].astype(F32), l_ref[...]
        for r in range(dil):
            rows = pl.ds(r, tm // dil, stride=dil)
            lstage_ref[gi - 1, rows, :] = l_ref[:, r * LANES:(r + 1) * LANES]
            for s in range(n_slabs):
                c0 = (r * n_slabs + s) * LANES
                ostage_ref[gi - 1, s, rows, :] = o_ref[:, c0:c0 + LANES].astype(F32)
        out = jnp.concatenate([ostage_ref[gi - 1, s] for s in range(n_slabs)], axis=1)
        return out, lstage_ref[gi - 1]

    (o0, l0), (o1, l1), (o2, l2) = (token_major(gi, o_ref, l_ref) for gi, (o_ref, l_ref) in
                                    enumerate(((o0_ref, l0_ref), (o1_ref, l1_ref), (o2_ref, l2_ref))))

    lm = jnp.maximum(jnp.maximum(l0, l1), l2)
    e0, e1, e2 = jnp.exp(l0 - lm), jnp.exp(l1 - lm), jnp.exp(l2 - lm)
    inv = 1.0 / (e0 + e1 + e2)
    parts = []
    for h in range(ATTN_HEADS_PER_GROUP):
        sl = slice(h * E, (h + 1) * E)
        parts.append((e0[:, h:h + 1] * inv[:, h:h + 1]) * o0[:, sl]
                     + (e1[:, h:h + 1] * inv[:, h:h + 1]) * o1[:, sl]
                     + (e2[:, h:h + 1] * inv[:, h:h + 1]) * o2[:, sl])
    o = jnp.concatenate(parts, axis=-1).astype(BF16)
    y_attn = jnp.dot(o, wattn_ref[...], preferred_element_type=F32)

    gates = _sigmoid(gate_ref[...].astype(F32) + bg_ref[...])
    merged_ref[...] = (gates[:, :d_model] * y_ssd + gates[:, d_model:] * y_attn).astype(BF16)


def _merge(yn, outs, lses, proj, w_ssd_out, w_attn_out, b_gate, *, tm, gate_block):
    n, d = yn.shape
    aw = outs[0].shape[1]
    dils = [dil for _, dil in DILATION_PATTERNS]
    assert dils[0] == 1 and all(tm % (dil * SUBLANES) == 0 for dil in dils)
    row = lambda w: pl.BlockSpec((tm, w), lambda i: (i, 0))
    grouped = lambda dil, w: pl.BlockSpec((tm // dil, dil * w), lambda i: (i, 0))
    const = lambda shape: pl.BlockSpec(shape, lambda i: (0, 0), pipeline_mode=pl.Buffered(1))
    return pl.pallas_call(
        functools.partial(_merge_kernel, d_model=d),
        grid=(n // tm,),
        in_specs=[
            row(d), *[grouped(dil, aw) for dil in dils], *[grouped(dil, LANES) for dil in dils],
            pl.BlockSpec((tm, 2 * d), lambda i: (i, gate_block)),
            const((d, d)), const((aw, d)), const((1, 2 * d)),
        ],
        out_specs=row(d),
        out_shape=jax.ShapeDtypeStruct((n, d), BF16),
        scratch_shapes=[pltpu.VMEM((len(dils) - 1, aw // LANES, tm, LANES), F32),
                        pltpu.VMEM((len(dils) - 1, tm, LANES), F32)],
        compiler_params=_params("arbitrary"),
        name="merge",
    )(yn, outs[0], outs[1], outs[2], lses[0], lses[1], lses[2], proj,
      w_ssd_out.astype(BF16), w_attn_out.astype(BF16), b_gate.reshape(1, -1))


def _route_kernel(merged_ref, x_ref, wout_ref, fnw_ref, wr_ref, rb_ref,
                  x1_ref, eid_ref, gcol_ref, rank_ref, cnt_ref, carry_ref):
    i = pl.program_id(0)
    tm = x_ref.shape[0]

    @pl.when(i == 0)
    def _():
        carry_ref[...] = jnp.zeros(carry_ref.shape, F32)

    x1 = x_ref[...] + jnp.dot(merged_ref[...], wout_ref[...], preferred_element_type=F32)
    x1_ref[...] = x1

    ms = jnp.mean(x1 * x1, axis=-1, keepdims=True)
    hn = x1 * lax.rsqrt(ms + NORM_EPS) * fnw_ref[...]

    hn_hi, hn_lo = _split_bf16(hn)
    by_hi = jnp.dot(hn_hi, wr_ref[...], preferred_element_type=F32)
    by_lo = jnp.dot(hn_lo, wr_ref[:, 0:LANES], preferred_element_type=F32)
    logits_tok = by_hi[:, 0:LANES] + by_hi[:, LANES:] + by_lo
    logits = logits_tok.T[0:ROUTER_ROWS, :] + rb_ref[...]

    grow = lax.broadcasted_iota(jnp.int32, (SUBLANES, tm), 0)
    gl = jnp.where(grow < N_EXPERT_GROUPS, logits[0:SUBLANES, :], -jnp.inf)
    gmax = jnp.max(gl, axis=0, keepdims=True)
    gidx = jnp.min(jnp.where(gl == gmax, grow, N_EXPERT_GROUPS), axis=0, keepdims=True)
    group_gate = 1.0 / jnp.sum(jnp.exp(gl - gmax), axis=0, keepdims=True)

    in_group = jnp.zeros((EXPERTS_PER_GROUP, tm), F32)
    for g in range(N_EXPERT_GROUPS):
        r0 = ROUTER_EXPERT_ROW0 + g * EXPERTS_PER_GROUP
        in_group = jnp.where(gidx == g, logits[r0:r0 + EXPERTS_PER_GROUP, :], in_group)
    erow = lax.broadcasted_iota(jnp.int32, in_group.shape, 0)
    v1 = jnp.max(in_group, axis=0, keepdims=True)
    i1 = jnp.min(jnp.where(in_group == v1, erow, EXPERTS_PER_GROUP), axis=0, keepdims=True)
    rest = jnp.where(erow == i1, -jnp.inf, in_group)
    v2 = jnp.max(rest, axis=0, keepdims=True)
    i2 = jnp.min(jnp.where(rest == v2, erow, EXPERTS_PER_GROUP), axis=0, keepdims=True)
    t = jnp.exp(v2 - v1)
    g1 = group_gate / (1.0 + t)
    g2 = group_gate * t / (1.0 + t)
    eid1 = gidx * EXPERTS_PER_GROUP + i1
    eid2 = gidx * EXPERTS_PER_GROUP + i2
    slot = lax.broadcasted_iota(jnp.int32, (SUBLANES, tm), 0)
    eid_ref[...] = jnp.where(slot == 0, eid1, jnp.where(slot == 1, eid2, 0))

    grow8 = lax.broadcasted_iota(jnp.int32, (LANES, tm), 0)
    gt = jnp.where(grow8 == 0, g1, jnp.where(grow8 == 1, g2, 0.0))
    gcol_ref[...] = gt.T

    xrow = lax.broadcasted_iota(jnp.int32, (N_EXPERTS, tm), 0)
    oh1 = xrow == eid1
    oh2 = xrow == eid2
    oh = jnp.where(oh1 | oh2, 1.0, 0.0)
    ti = lax.broadcasted_iota(jnp.int32, (tm, tm), 0)
    tj = lax.broadcasted_iota(jnp.int32, (tm, tm), 1)
    before = jnp.where(ti < tj, 1.0, 0.0).astype(BF16)
    prior = jnp.dot(oh.astype(BF16), before, preferred_element_type=F32) + carry_ref[:, 0:1]
    r1 = jnp.sum(jnp.where(oh1, prior, 0.0), axis=0, keepdims=True)
    r2 = jnp.sum(jnp.where(oh2, prior, 0.0), axis=0, keepdims=True)
    rank_ref[...] = jnp.where(slot == 0, r1, jnp.where(slot == 1, r2, 0.0)).astype(jnp.int32)
    carry_ref[...] = carry_ref[...] + jnp.sum(oh, axis=1, keepdims=True)
    cnt_ref[...] = carry_ref[...].astype(jnp.int32)


def _route(merged, x2, w_out, ffn_norm_w, wr, rb, *, tm):
    n, d = x2.shape
    wrc = jnp.zeros((d, LANES), F32)
    wrc = wrc.at[:, 0:N_EXPERT_GROUPS].set(wr[0]).at[:, ROUTER_EXPERT_ROW0:ROUTER_ROWS].set(wr[1])
    rbc = jnp.zeros((ROUTER_ROWS, 1), F32)
    rbc = rbc.at[0:N_EXPERT_GROUPS, 0].set(rb[0]).at[ROUTER_EXPERT_ROW0:, 0].set(rb[1])
    wr_hi_lo = jnp.concatenate(_split_bf16(wrc), axis=1)
    row = lambda w: pl.BlockSpec((tm, w), lambda i: (i, 0))
    const = lambda shape: pl.BlockSpec(shape, lambda i: (0, 0), pipeline_mode=pl.Buffered(1))
    return pl.pallas_call(
        _route_kernel,
        grid=(n // tm,),
        in_specs=[
            row(d), row(d),
            const((d, d)), const((1, d)),
            const((d, 2 * LANES)), const((ROUTER_ROWS, 1)),
        ],
        out_specs=[
            row(d),
            pl.BlockSpec((SUBLANES, tm), lambda i: (0, i)),
            row(LANES),
            pl.BlockSpec((SUBLANES, tm), lambda i: (0, i)),
            pl.BlockSpec((N_EXPERTS, LANES), lambda i: (0, 0)),
        ],
        out_shape=[
            jax.ShapeDtypeStruct((n, d), F32),
            jax.ShapeDtypeStruct((SUBLANES, n), jnp.int32),
            jax.ShapeDtypeStruct((n, LANES), F32),
            jax.ShapeDtypeStruct((SUBLANES, n), jnp.int32),
            jax.ShapeDtypeStruct((N_EXPERTS, LANES), jnp.int32),
        ],
        scratch_shapes=[pltpu.VMEM((N_EXPERTS, LANES), F32)],
        compiler_params=_params("arbitrary"),
        name="route",
    )(merged, x2, w_out.astype(BF16), ffn_norm_w.reshape(1, -1), wr_hi_lo, rbc)


def _scatter_kernel(pad_start_ref, pad_len_ref, n_active_ref, pos_ref, hn_ref, xs_ref, zero_ref,
                    sem, zsem, *, tme, n_tiles):
    tm = hn_ref.shape[0]
    zrows = zero_ref.shape[0]

    @pl.when(pl.program_id(0) == 0)
    def _():
        zero_ref[...] = jnp.zeros(zero_ref.shape, zero_ref.dtype)

        def fills(act):
            def per_expert(e, carry):
                off = pad_start_ref[e]
                left = pad_len_ref[e]
                head = left & (SUBLANES - 1)
                for r in range(SUBLANES - 1):
                    @pl.when(r < head)
                    def _(r=r):
                        act(pltpu.make_async_copy(zero_ref.at[pl.ds(0, 1), :],
                                                  xs_ref.at[pl.ds(off + r, 1), :], zsem))

                off = off + head
                bit = zrows
                while bit >= SUBLANES:
                    take = left & bit

                    @pl.when(take != 0)
                    def _(off=off, bit=bit):
                        act(pltpu.make_async_copy(
                            zero_ref.at[pl.ds(0, bit), :],
                            xs_ref.at[pl.ds(pl.multiple_of(off, SUBLANES), bit), :], zsem))

                    off = off + take
                    bit //= 2
                return carry

            def per_tile(t, carry):
                @pl.when(t >= n_active_ref[0])
                def _():
                    for part in range(tme // zrows):
                        row0 = pl.multiple_of(t * tme + part * zrows, zrows)
                        act(pltpu.make_async_copy(zero_ref, xs_ref.at[pl.ds(row0, zrows), :], zsem))

                return carry

            lax.fori_loop(0, N_EXPERTS, per_expert, 0)
            lax.fori_loop(0, n_tiles, per_tile, 0)

        fills(lambda cp: cp.start())
        fills(lambda cp: cp.wait())

    unroll = math.gcd(tm, ROW_DMA_UNROLL)

    def start(tb, carry):
        for u in range(unroll):
            t = tb * unroll + u
            for k in range(EXPERT_TOP_K):
                dst = pos_ref[0, 0, k * tm + t]
                pltpu.make_async_copy(hn_ref.at[pl.ds(t, 1), :], xs_ref.at[pl.ds(dst, 1), :],
                                      sem).start(priority=(u + k) % 2)
        return carry

    lax.fori_loop(0, tm // unroll, start, 0)
    for k in range(EXPERT_TOP_K):
        pltpu.make_async_copy(hn_ref, xs_ref.at[pl.ds(0, tm), :], sem).wait()


def _scatter(hn, pos_tiles, pad_start, pad_len, n_active, *, tm, tme, n_tiles):
    n, d = hn.shape
    assert tme % 2 == 0 and (tme // 2) & (tme // 2 - 1) == 0
    grid_spec = pltpu.PrefetchScalarGridSpec(
        num_scalar_prefetch=3,
        grid=(n // tm,),
        in_specs=[
            pl.BlockSpec((1, 1, EXPERT_TOP_K * tm), lambda i, *_: (i, 0, 0),
                         memory_space=pltpu.SMEM),
            pl.BlockSpec((tm, d), lambda i, *_: (i, 0)),
        ],
        out_specs=pl.BlockSpec(memory_space=pl.ANY),
        scratch_shapes=[pltpu.VMEM((tme // 2, d), hn.dtype), pltpu.SemaphoreType.DMA(()),
                        pltpu.SemaphoreType.DMA(())],
    )
    return pl.pallas_call(
        functools.partial(_scatter_kernel, tme=tme, n_tiles=n_tiles),
        grid_spec=grid_spec,
        out_shape=jax.ShapeDtypeStruct((n_tiles * tme, d), hn.dtype),
        compiler_params=_params("arbitrary"),
        name="scatter",
    )(pad_start, pad_len, n_active, pos_tiles, hn)


W_CHUNK_ROWS = 256
W_RING = 8


def _experts_kernel(te_ref, slot_ref, nxt_ref, lo_ref, hi_ref, na_ref, xs_ref, fnw_ref, wg_hbm,
                    wu_hbm, wd_hbm, ys_ref, wres_ref, stage_ref, sem, *, n_gu, n_r, n_h):
    i = pl.program_id(0)
    n_chunks = 2 * n_gu + n_h * n_r
    ring, cr, f = stage_ref.shape
    d = n_gu * cr

    def stage_copy(src, c):
        return pltpu.make_async_copy(src, stage_ref.at[c % ring], sem.at[c % ring])

    def start_chunk(e, c):
        @pl.when(c < n_gu)
        def _():
            stage_copy(wg_hbm.at[e, pl.ds(pl.multiple_of(c * cr, cr), cr), :], c).start()

        @pl.when((c >= n_gu) & (c < 2 * n_gu))
        def _():
            stage_copy(wu_hbm.at[e, pl.ds(pl.multiple_of((c - n_gu) * cr, cr), cr), :], c).start()

        @pl.when(c >= 2 * n_gu)
        def _():
            q = c - 2 * n_gu
            h = q // n_r
            r = q - h * n_r
            stage_copy(wd_hbm.at[e, pl.ds(pl.multiple_of(r * cr, cr), cr),
                                 pl.ds(pl.multiple_of(h * f, f), f)], c).start()

    def start_first(e):
        def body(c, carry):
            start_chunk(e, c)
            return carry

        lax.fori_loop(0, ring, body, 0)

    def convert(e, lo, hi, dst_slot):
        def body(c, carry):
            stage_copy(wg_hbm.at[0, pl.ds(0, cr), :], c).wait()
            wres_ref[dst_slot, pl.ds(pl.multiple_of(c * cr, cr), cr), :] = (
                stage_ref[c % ring].astype(BF16))

            @pl.when(c + ring < n_chunks)
            def _():
                start_chunk(e, c + ring)

            return carry

        lax.fori_loop(lo, hi, body, 0)

    @pl.when(i == 0)
    def _():
        start_first(te_ref[0])
        convert(te_ref[0], 0, n_chunks, slot_ref[0])

    @pl.when(i < na_ref[0])
    def _():
        slot = slot_ref[i]
        nxt = nxt_ref[i]

        @pl.when((nxt >= 0) & (lo_ref[i] == 0))
        def _():
            start_first(nxt)

        xr = xs_ref[...]
        inv_rms = lax.rsqrt(jnp.mean(xr * xr, axis=-1, keepdims=True) + NORM_EPS)
        xb = (xr * fnw_ref[...]).astype(BF16)
        hg = jnp.dot(xb, wres_ref[slot, 0:d, :], preferred_element_type=F32) * inv_rms
        hu = jnp.dot(xb, wres_ref[slot, d:2 * d, :], preferred_element_type=F32) * inv_rms
        hmid = (_silu(hg) * hu).astype(BF16)
        for h in range(n_h):
            r0 = 2 * d + h * n_r * cr
            ys_ref[:, h * f:(h + 1) * f] = jnp.dot(hmid, wres_ref[slot, r0:r0 + n_r * cr, :],
                                                    preferred_element_type=F32)

        @pl.when(nxt >= 0)
        def _():
            convert(nxt, lo_ref[i], hi_ref[i], 1 - slot)

    @pl.when(i >= na_ref[0])
    def _():
        ys_ref[...] = jnp.zeros(ys_ref.shape, F32)


def _experts(xs, norm_w, tile_expert, n_active, cnt, starts, padded, w_g, w_u, w_d, *, tm):
    p, d = xs.shape
    ne, _, f = w_g.shape
    n_tiles = p // tm
    cr = W_CHUNK_ROWS
    assert d % cr == 0 and f % cr == 0 and d % f == 0
    n_gu, n_r, n_h = d // cr, f // cr, d // f
    n_chunks = 2 * n_gu + n_h * n_r

    ids = jnp.arange(ne, dtype=jnp.int32)
    present = cnt > 0
    later = jnp.where(present[None, :] & (ids[None, :] > ids[:, None]), ids[None, :], ne)
    nxt_e = jnp.min(later, axis=1)
    nxt_e = jnp.where(nxt_e >= ne, -1, nxt_e)
    run_e = jnp.cumsum(present.astype(jnp.int32)) - 1
    onehot = (tile_expert[:, None] == ids[None, :]).astype(jnp.int32)
    pick = lambda v: jnp.sum(onehot * v[None, :].astype(jnp.int32), axis=1)
    tiles = jnp.arange(n_tiles, dtype=jnp.int32)
    active = tiles < n_active[0]
    j = tiles - pick(starts) // tm
    k = jnp.maximum(pick(padded) // tm, 1)
    nxt_t = jnp.where(active, pick(nxt_e), -1).astype(jnp.int32)
    lo_t = jnp.where(active, (n_chunks * j) // k, 0).astype(jnp.int32)
    hi_t = jnp.where(active, (n_chunks * (j + 1)) // k, 0).astype(jnp.int32)
    slot_t = (pick(run_e) % 2).astype(jnp.int32)

    def rows(i, te, sl, nx, lo, hi, na):
        return (jnp.minimum(i, na[0] - 1), 0)

    grid_spec = pltpu.PrefetchScalarGridSpec(
        num_scalar_prefetch=6,
        grid=(n_tiles,),
        in_specs=[
            pl.BlockSpec((tm, d), rows),
            pl.BlockSpec((1, d), lambda i, *_: (0, 0)),
            pl.BlockSpec(memory_space=pl.ANY),
            pl.BlockSpec(memory_space=pl.ANY),
            pl.BlockSpec(memory_space=pl.ANY),
        ],
        out_specs=pl.BlockSpec((tm, d), lambda i, *_: (i, 0)),
        scratch_shapes=[
            pltpu.VMEM((2, n_chunks * cr, f), BF16),
            pltpu.VMEM((W_RING, cr, f), F32),
            pltpu.SemaphoreType.DMA((W_RING,)),
        ],
    )
    return pl.pallas_call(
        functools.partial(_experts_kernel, n_gu=n_gu, n_r=n_r, n_h=n_h),
        grid_spec=grid_spec,
        out_shape=jax.ShapeDtypeStruct((p, d), F32),
        compiler_params=_params("arbitrary"),
        name="experts",
    )(tile_expert, slot_t, nxt_t, lo_t, hi_t, n_active, xs, norm_w.reshape(1, d), w_g, w_u, w_d)


def _combine_kernel(pos_ref, pos_next_ref, ys_ref, x1_ref, gcol_ref, nw_ref, out_ref, buf_ref, sem,
                    *, final):
    i = pl.program_id(0)
    n_steps = pl.num_programs(0)
    tm = x1_ref.shape[0]

    unroll = math.gcd(tm, ROW_DMA_UNROLL)

    def issue(p_ref, slot):
        def start(tb, carry):
            for u in range(unroll):
                t = tb * unroll + u
                for k in range(EXPERT_TOP_K):
                    src = p_ref[0, 0, k * tm + t]
                    pltpu.make_async_copy(ys_ref.at[pl.ds(src, 1), :],
                                          buf_ref.at[slot, k, pl.ds(t, 1), :],
                                          sem.at[slot]).start(priority=(u + k) % 2)
            return carry

        lax.fori_loop(0, tm // unroll, start, 0)

    slot = i % 2

    @pl.when(i == 0)
    def _():
        issue(pos_ref, 0)

    @pl.when(i + 1 < n_steps)
    def _():
        issue(pos_next_ref, 1 - slot)

    for k in range(EXPERT_TOP_K):
        pltpu.make_async_copy(ys_ref.at[pl.ds(0, tm), :], buf_ref.at[slot, k], sem.at[slot]).wait()
    g = gcol_ref[...]
    xo = x1_ref[...] + g[:, 0:1] * buf_ref[slot, 0] + g[:, 1:2] * buf_ref[slot, 1]
    if final:
        ms = jnp.mean(xo * xo, axis=-1, keepdims=True)
        xo = xo * lax.rsqrt(ms + NORM_EPS) * nw_ref[...]
    out_ref[...] = xo


def _combine(ys, pos_tiles, x1, gcol, norm_w, *, tm, final):
    n, d = x1.shape
    last = n // tm - 1
    return pl.pallas_call(
        functools.partial(_combine_kernel, final=final),
        grid=(n // tm,),
        in_specs=[
            pl.BlockSpec((1, 1, EXPERT_TOP_K * tm), lambda i: (i, 0, 0), memory_space=pltpu.SMEM),
            pl.BlockSpec((1, 1, EXPERT_TOP_K * tm), lambda i: (jnp.minimum(i + 1, last), 0, 0),
                         memory_space=pltpu.SMEM),
            pl.BlockSpec(memory_space=pl.ANY),
            pl.BlockSpec((tm, d), lambda i: (i, 0)),
            pl.BlockSpec((tm, LANES), lambda i: (i, 0)),
            pl.BlockSpec((1, d), lambda i: (0, 0)),
        ],
        out_specs=pl.BlockSpec((tm, d), lambda i: (i, 0)),
        out_shape=jax.ShapeDtypeStruct((n, d), F32),
        scratch_shapes=[pltpu.VMEM((2, EXPERT_TOP_K, tm, d), F32), pltpu.SemaphoreType.DMA((2,))],
        compiler_params=_params("arbitrary"),
        name="combine",
    )(pos_tiles, pos_tiles, ys, x1, gcol, norm_w.reshape(1, d))


def _tiles(n):
    return dict(proj_tm=math.gcd(n, 1024), mix_tm=math.gcd(n, 512), moe_tm=256,
                route_tm=math.gcd(n, 512))


def _layer(x2, batch, seq, layer, attn_norm_w, w_in_stack, b_gate, conv_w, conv_b, dt_bias, a_log,
           d_skip, ssd_norm_w, w_ssd_out, w_attn_out, w_out, ffn_norm_w, w_gr, b_gr, w_er, b_er,
           w_g, w_u, w_d):
    n, d = x2.shape
    n_heads = dt_bias.shape[0]
    d_inner = ssd_norm_w.shape[0]
    conv_dim = conv_w.shape[1]
    aw_total = ATTN_HEADS_PER_GROUP * len(DILATION_PATTERNS) * ATTN_HEAD_DIM
    gw = ATTN_HEADS_PER_GROUP * ATTN_HEAD_DIM
    tiles = _tiles(n)

    c_z, c_xbc, c_dt = d_inner, d_inner + conv_dim, d_inner + conv_dim + n_heads
    c_gate = c_dt + QKV_PARTS * aw_total
    segments = [(c_gate, 2 * d), (c_z, conv_dim), (0, d_inner)]
    segments += [(c_dt + p * aw_total + gi * gw, gw)
                 for gi in range(len(DILATION_PATTERNS)) for p in range(QKV_PARTS)]
    assert all(width % gw == 0 for _, width in segments)
    starts = tuple(start + b * gw for start, width in segments for b in range(width // gw))
    w_main, w_dt = _regroup_weight(w_in_stack, layer, starts, gw, c_xbc, n_heads)
    off_gate, off_xbc, off_z = 0, 2 * d, 2 * d + conv_dim
    plain_cols = off_z + d_inner
    assert off_xbc % conv_dim == 0 and off_z % d_inner == 0 and plain_cols % gw == 0

    proj, qkv0, qkv1, qkv2, dt, dtt = _in_proj(x2, attn_norm_w, w_main, w_dt, tm=tiles["proj_tm"],
                                               tn=gw, tp=4 * gw, plain_cols=plain_cols, nh=n_heads)

    yn = _ssd(proj, dt, dtt, conv_w, conv_b, dt_bias, a_log, d_skip, ssd_norm_w, batch=batch,
              seq=seq, d_inner=d_inner, xbc_block=off_xbc // conv_dim, z_block=off_z // d_inner)

    outs, lses = [], []
    for gi, qkv in enumerate((qkv0, qkv1, qkv2)):
        o_g, lse_g = _attn_group(qkv, gi, batch=batch, seq=seq)
        outs.append(o_g)
        lses.append(lse_g)

    merged = _merge(yn, outs, lses, proj, w_ssd_out, w_attn_out, b_gate, tm=tiles["mix_tm"],
                    gate_block=off_gate // (2 * d))
    x1, eid, gcol, rank, counts = _route(merged, x2, w_out, ffn_norm_w, (w_gr, w_er),
                                         (b_gr, b_er), tm=tiles["mix_tm"])

    tme = tiles["moe_tm"]
    cnt = counts[:, 0]
    padded = ((cnt + tme - 1) // tme) * tme
    ends = jnp.cumsum(padded)
    starts = ends - padded
    experts = jnp.arange(N_EXPERTS, dtype=jnp.int32)[:, None, None]
    pos = rank[:EXPERT_TOP_K] + jnp.sum(
        jnp.where(eid[None, :EXPERT_TOP_K] == experts, starts[:, None, None], 0), axis=0)
    n_tiles = EXPERT_TOP_K * n // tme + N_EXPERTS
    tile_start = jnp.arange(n_tiles, dtype=jnp.int32) * tme
    tile_expert = jnp.minimum(jnp.sum(ends[None, :] <= tile_start[:, None], axis=1),
                              N_EXPERTS - 1).astype(jnp.int32)
    n_active = (ends[-1:] // tme).astype(jnp.int32)

    rtm = tiles["route_tm"]
    pos_tiles = pos.reshape(EXPERT_TOP_K, n // rtm, rtm).transpose(1, 0, 2).reshape(
        n // rtm, 1, EXPERT_TOP_K * rtm)
    xs = _scatter(x1, pos_tiles, (starts + cnt).astype(jnp.int32), (padded - cnt).astype(jnp.int32),
                  n_active, tm=rtm, tme=tme, n_tiles=n_tiles)
    ys = _experts(xs, ffn_norm_w, tile_expert, n_active, cnt, starts, padded, w_g, w_u, w_d, tm=tme)
    return ys, pos_tiles, x1, gcol, rtm


def kernel(x, attn_norm_w, w_in, b_gate, conv_w, conv_b, dt_bias, a_log, d_skip, ssd_norm_w,
           w_ssd_out, w_attn_out, w_out, ffn_norm_w, w_group_router, b_group_router,
           w_expert_router, b_expert_router, w_exp_gate, w_exp_up, w_exp_down, final_norm_w):
    batch, seq, d = x.shape
    depth = w_in.shape[0]
    x2 = x.reshape(batch * seq, d)
    for layer in range(depth):
        ys, pos_tiles, x1, gcol, rtm = _layer(
            x2, batch, seq, layer, attn_norm_w[layer], w_in, b_gate[layer], conv_w[layer],
            conv_b[layer], dt_bias[layer], a_log[layer], d_skip[layer], ssd_norm_w[layer],
            w_ssd_out[layer], w_attn_out[layer], w_out[layer], ffn_norm_w[layer],
            w_group_router[layer], b_group_router[layer], w_expert_router[layer],
            b_expert_router[layer], w_exp_gate[layer], w_exp_up[layer], w_exp_down[layer])
        x2 = _combine(ys, pos_tiles, x1, gcol, final_norm_w, tm=rtm, final=layer == depth - 1)
    return x2.reshape(batch, seq, d)
```
